```python
import jax, jax.numpy as jnp
from jax import lax
import numpy as np

D_MODEL = 1024
BATCH = 8
SEQ = 4096
DEPTH = 4

N_META = 16
EPS = 1e-6
SSD_D_INNER = 2 * D_MODEL
SSD_HEAD_DIM = 64
SSD_HEADS = SSD_D_INNER // SSD_HEAD_DIM
SSD_GROUPS = 8
SSD_HPG = SSD_HEADS // SSD_GROUPS
SSD_STATE = 128
SSD_CONV = 4
SSD_CHUNK = 128
SSD_CONV_DIM = SSD_D_INNER + 2 * SSD_GROUPS * SSD_STATE
SSD_IN_DIM = SSD_D_INNER + SSD_CONV_DIM + SSD_HEADS
MLA_HEADS = 16
MLA_NOPE = 64
MLA_ROPE = 32
MLA_V = 64
MLA_QK = MLA_NOPE + MLA_ROPE
MLA_Q_RANK = 384
MLA_KV_RANK = 256
MLA_IN_DIM = MLA_Q_RANK + MLA_KV_RANK + MLA_ROPE
ROPE_THETA = 10000.0
ATTN_BLOCK = 128
D_FF = 4 * D_MODEL
N_SSD_LAYERS = (DEPTH + 1) // 2
N_MLA_LAYERS = DEPTH // 2

kernel_name = "hybrid_ssd_mla_meta_trunk"


def rms_norm(x, gain):
    xf = x.astype(jnp.float32)
    y = xf * lax.rsqrt(jnp.mean(xf * xf, axis=-1, keepdims=True) + EPS)
    return (y * gain.astype(jnp.float32)).astype(x.dtype)


def causal_depthwise_conv(u, w, b):
    out = lax.conv_general_dilated(
        u, w[:, None, :].astype(u.dtype), window_strides=(1,),
        padding=[(SSD_CONV - 1, 0)], dimension_numbers=('NWC', 'WIO', 'NWC'),
        feature_group_count=u.shape[-1])
    return out + b.astype(u.dtype)


def ssd_mixer(h, w_in, conv_w, conv_b, dt_bias, a_log, d_skip, norm_g, w_out):
    f32 = jnp.float32
    bsz, L, _ = h.shape
    zxbcdt = h @ w_in
    z, xbc, dt = jnp.split(zxbcdt, [SSD_D_INNER, SSD_D_INNER + SSD_CONV_DIM], axis=-1)
    xbc = jax.nn.silu(causal_depthwise_conv(xbc, conv_w, conv_b))
    xs, b_in, c_in = jnp.split(xbc, [SSD_D_INNER, SSD_D_INNER + SSD_GROUPS * SSD_STATE], axis=-1)
    dt = jax.nn.softplus(dt.astype(f32) + dt_bias.astype(f32))
    a = -jnp.exp(a_log.astype(f32))

    pad = (-L) % SSD_CHUNK
    n_chunks = (L + pad) // SSD_CHUNK

    def front_pad(t):
        return jnp.pad(t.astype(f32), [(0, 0), (pad, 0)] + [(0, 0)] * (t.ndim - 2))

    x_c = front_pad(xs).reshape(bsz, n_chunks, SSD_CHUNK, SSD_GROUPS, SSD_HPG, SSD_HEAD_DIM)
    b_c = front_pad(b_in).reshape(bsz, n_chunks, SSD_CHUNK, SSD_GROUPS, SSD_STATE)
    c_c = front_pad(c_in).reshape(bsz, n_chunks, SSD_CHUNK, SSD_GROUPS, SSD_STATE)
    dt_c = front_pad(dt).reshape(bsz, n_chunks, SSD_CHUNK, SSD_GROUPS, SSD_HPG)
    xdt = x_c * dt_c[..., None]
    a_dt = (dt_c * a.reshape(SSD_GROUPS, SSD_HPG)).transpose(0, 1, 3, 4, 2)
    a_cs = jnp.cumsum(a_dt, axis=-1)

    idx = jnp.arange(SSD_CHUNK)
    causal = idx[:, None] >= idx[None, :]
    decay = jnp.exp(jnp.where(causal, a_cs[..., :, None] - a_cs[..., None, :], -jnp.inf))
    cb = jnp.einsum('bclgn,bcsgn->bcgls', c_c, b_c)
    y_diag = jnp.einsum('bcgjls,bcsgjp->bclgjp', cb[:, :, :, None] * decay, xdt)

    decay_to_end = jnp.exp(a_cs[..., -1:] - a_cs).transpose(0, 1, 4, 2, 3)
    states = jnp.einsum('bclgn,bclgjp->bcgjpn', b_c, xdt * decay_to_end[..., None])
    chunk_decay = jnp.exp(a_cs[..., -1])

    def step(carry, inp):
        st, dec = inp
        return carry * dec[..., None, None] + st, carry

    init = jnp.zeros((bsz, SSD_GROUPS, SSD_HPG, SSD_HEAD_DIM, SSD_STATE), f32)
    _, prev = lax.scan(step, init, (jnp.moveaxis(states, 1, 0), jnp.moveaxis(chunk_decay, 1, 0)))
    prev = jnp.moveaxis(prev, 0, 1)
    decay_from_start = jnp.exp(a_cs).transpose(0, 1, 4, 2, 3)
    y_off = jnp.einsum('bclgn,bcgjpn->bclgjp', c_c, prev) * decay_from_start[..., None]

    y = (y_diag + y_off).reshape(bsz, n_chunks * SSD_CHUNK, SSD_D_INNER)[:, pad:]
    y = y + xs.astype(f32) * jnp.repeat(d_skip.astype(f32), SSD_HEAD_DIM)
    g = (y * jax.nn.silu(z.astype(f32))).reshape(bsz, L, SSD_GROUPS, SSD_D_INNER // SSD_GROUPS)
    g = g * lax.rsqrt(jnp.mean(g * g, axis=-1, keepdims=True) + EPS)
    g = g.reshape(bsz, L, SSD_D_INNER) * norm_g.astype(f32)
    return g.astype(h.dtype) @ w_out


def rope_tables(L):
    inv = 1.0 / (ROPE_THETA ** (jnp.arange(0, MLA_ROPE, 2, dtype=jnp.float32) / MLA_ROPE))
    ang = jnp.arange(L, dtype=jnp.float32)[:, None] * inv[None, :]
    return jnp.cos(ang)[None, :, None, :], jnp.sin(ang)[None, :, None, :]


def apply_rope(t, cos, sin):
    t1, t2 = jnp.split(t, 2, axis=-1)
    cos = cos.astype(t.dtype)
    sin = sin.astype(t.dtype)
    return jnp.concatenate([t1 * cos - t2 * sin, t1 * sin + t2 * cos], axis=-1)


def mla_mixer(h, w_in, q_a_g, w_q_b, kv_a_g, w_kv_b, q_norm_g, k_norm_g, w_out):
    bsz, L, _ = h.shape
    q_lat, kv_lat, k_pe = jnp.split(h @ w_in, [MLA_Q_RANK, MLA_Q_RANK + MLA_KV_RANK], axis=-1)
    q = (rms_norm(q_lat, q_a_g) @ w_q_b).reshape(bsz, L, MLA_HEADS, MLA_QK)
    kv = (rms_norm(kv_lat, kv_a_g) @ w_kv_b).reshape(bsz, L, MLA_HEADS, MLA_NOPE + MLA_V)
    k_nope, v = jnp.split(kv, [MLA_NOPE], axis=-1)
    k = jnp.concatenate(
        [k_nope, jnp.broadcast_to(k_pe[:, :, None, :], (bsz, L, MLA_HEADS, MLA_ROPE))], axis=-1)
    q = rms_norm(q, q_norm_g)
    k = rms_norm(k, k_norm_g)
    cos, sin = rope_tables(L)
    q = jnp.concatenate([q[..., :MLA_NOPE], apply_rope(q[..., MLA_NOPE:], cos, sin)], axis=-1)
    k = jnp.concatenate([k[..., :MLA_NOPE], apply_rope(k[..., MLA_NOPE:], cos, sin)], axis=-1)
    scale = MLA_QK ** -0.5

    blocks = [(0, N_META)] + [(s, min(s + ATTN_BLOCK, L)) for s in range(N_META, L, ATTN_BLOCK)]
    outs = []
    for s, e in blocks:
        sc = jnp.einsum('bqhd,bkhd->bhqk', q[:, s:e], k[:, :e]).astype(jnp.float32) * scale
        mask = jnp.arange(e)[None, :] <= jnp.arange(s, e)[:, None]
        p = jax.nn.softmax(jnp.where(mask, sc, -jnp.inf), axis=-1).astype(v.dtype)
        outs.append(jnp.einsum('bhqk,bkhd->bqhd', p, v[:, :e]))
    o = jnp.concatenate(outs, axis=1).reshape(bsz, L, MLA_HEADS * MLA_V)
    return o @ w_out


def sqrelu_mlp(h, w_up, w_down):
    return jnp.square(jax.nn.relu(h @ w_up)) @ w_down


def _fwd_setup_inputs(seed: int = 0) -> dict:
    key = jax.random.key(seed)
    ks = jax.random.split(key, 24)
    f32 = jnp.float32

    def nrm(k, shape, fan_in):
        return jax.random.normal(k, shape, f32) * (fan_in ** -0.5)

    def gain(k, shape):
        return 1.0 + 0.02 * jax.random.normal(k, shape, f32)

    ns, nm = N_SSD_LAYERS, N_MLA_LAYERS
    dt0 = jnp.exp(jax.random.uniform(ks[7], (ns, SSD_HEADS), f32, np.log(1e-3), np.log(1e-1)))
    return {
        "x": jax.random.normal(ks[0], (BATCH, SEQ, D_MODEL), f32),
        "meta_tokens": jax.random.normal(ks[1], (N_META, D_MODEL), f32),
        "ln_mix": gain(ks[2], (DEPTH, D_MODEL)),
        "ln_mlp": gain(ks[3], (DEPTH, D_MODEL)),
        "ssd_w_in": nrm(ks[4], (ns, D_MODEL, SSD_IN_DIM), D_MODEL),
        "ssd_conv_w": nrm(ks[5], (ns, SSD_CONV, SSD_CONV_DIM), SSD_CONV),
        "ssd_conv_b": 0.02 * jax.random.normal(ks[6], (ns, SSD_CONV_DIM), f32),
        "ssd_dt_bias": dt0 + jnp.log(-jnp.expm1(-dt0)),
        "ssd_a_log": jnp.log(jax.random.uniform(ks[8], (ns, SSD_HEADS), f32, 1.0, 16.0)),
        "ssd_d": 1.0 + 0.1 * jax.random.normal(ks[9], (ns, SSD_HEADS), f32),
        "ssd_norm": gain(ks[10], (ns, SSD_D_INNER)),
        "ssd_w_out": nrm(ks[11], (ns, SSD_D_INNER, D_MODEL), SSD_D_INNER),
        "mla_w_in": nrm(ks[12], (nm, D_MODEL, MLA_IN_DIM), D_MODEL),
        "mla_q_a_norm": gain(ks[13], (nm, MLA_Q_RANK)),
        "mla_w_q_b": nrm(ks[14], (nm, MLA_Q_RANK, MLA_HEADS * MLA_QK), MLA_Q_RANK),
        "mla_kv_a_norm": gain(ks[15], (nm, MLA_KV_RANK)),
        "mla_w_kv_b": nrm(ks[16], (nm, MLA_KV_RANK, MLA_HEADS * (MLA_NOPE + MLA_V)), MLA_KV_RANK),
        "mla_q_norm": gain(ks[17], (nm, MLA_QK)),
        "mla_k_norm": gain(ks[18], (nm, MLA_QK)),
        "mla_w_out": nrm(ks[19], (nm, MLA_HEADS * MLA_V, D_MODEL), MLA_HEADS * MLA_V),
        "mlp_w_up": nrm(ks[20], (DEPTH, D_MODEL, D_FF), D_MODEL),
        "mlp_w_down": nrm(ks[21], (DEPTH, D_FF, D_MODEL), D_FF),
    }


def _fwd_reference(x, meta_tokens, ln_mix, ln_mlp, ssd_w_in, ssd_conv_w, ssd_conv_b, ssd_dt_bias,
              ssd_a_log, ssd_d, ssd_norm, ssd_w_out, mla_w_in, mla_q_a_norm, mla_w_q_b,
              mla_kv_a_norm, mla_w_kv_b, mla_q_norm, mla_k_norm, mla_w_out, mlp_w_up, mlp_w_down):
    bsz = x.shape[0]
    meta = jnp.broadcast_to(meta_tokens[None].astype(x.dtype), (bsz, N_META, D_MODEL))
    h = jnp.concatenate([meta, x], axis=1)
    for i in range(DEPTH):
        j = i // 2
        hn = rms_norm(h, ln_mix[i])
        if i % 2 == 0:
            h = h + ssd_mixer(hn, ssd_w_in[j], ssd_conv_w[j], ssd_conv_b[j], ssd_dt_bias[j],
                              ssd_a_log[j], ssd_d[j], ssd_norm[j], ssd_w_out[j])
        else:
            h = h + mla_mixer(hn, mla_w_in[j], mla_q_a_norm[j], mla_w_q_b[j], mla_kv_a_norm[j],
                              mla_w_kv_b[j], mla_q_norm[j], mla_k_norm[j], mla_w_out[j])
        h = h + sqrelu_mlp(rms_norm(h, ln_mlp[i]), mlp_w_up[i], mlp_w_down[i])
    return h[:, N_META:]


import jax as _jax
import jax.numpy as _jnp

TWIN_FORMAT = 'train_step'
FWD_PARAMS = ['x', 'meta_tokens', 'ln_mix', 'ln_mlp', 'ssd_w_in', 'ssd_conv_w', 'ssd_conv_b', 'ssd_dt_bias', 'ssd_a_log', 'ssd_d', 'ssd_norm', 'ssd_w_out', 'mla_w_in', 'mla_q_a_norm', 'mla_w_q_b', 'mla_kv_a_norm', 'mla_w_kv_b', 'mla_q_norm', 'mla_k_norm', 'mla_w_out', 'mlp_w_up', 'mlp_w_down']
TWIN_WEIGHTS = ['meta_tokens', 'ln_mix', 'ln_mlp', 'ssd_w_in', 'ssd_conv_w', 'ssd_conv_b', 'ssd_dt_bias', 'ssd_a_log', 'ssd_d', 'ssd_norm', 'ssd_w_out', 'mla_w_in', 'mla_q_a_norm', 'mla_w_q_b', 'mla_kv_a_norm', 'mla_w_kv_b', 'mla_q_norm', 'mla_k_norm', 'mla_w_out', 'mlp_w_up', 'mlp_w_down']
TWIN_DIFF_INPUT = 'x'
TWIN_INPUTS = ['x', 'meta_tokens', 'ln_mix', 'ln_mlp', 'ssd_w_in', 'ssd_conv_w', 'ssd_conv_b', 'ssd_dt_bias', 'ssd_a_log', 'ssd_d', 'ssd_norm', 'ssd_w_out', 'mla_w_in', 'mla_q_a_norm', 'mla_w_q_b', 'mla_kv_a_norm', 'mla_w_kv_b', 'mla_q_norm', 'mla_k_norm', 'mla_w_out', 'mlp_w_up', 'mlp_w_down', 'loss_target', 'm_meta_tokens', 'm_ln_mix', 'm_ln_mlp', 'm_ssd_w_in', 'm_ssd_conv_w', 'm_ssd_conv_b', 'm_ssd_dt_bias', 'm_ssd_a_log', 'm_ssd_d', 'm_ssd_norm', 'm_ssd_w_out', 'm_mla_w_in', 'm_mla_q_a_norm', 'm_mla_w_q_b', 'm_mla_kv_a_norm', 'm_mla_w_kv_b', 'm_mla_q_norm', 'm_mla_k_norm', 'm_mla_w_out', 'm_mlp_w_up', 'm_mlp_w_down', 'v_meta_tokens', 'v_ln_mix', 'v_ln_mlp', 'v_ssd_w_in', 'v_ssd_conv_w', 'v_ssd_conv_b', 'v_ssd_dt_bias', 'v_ssd_a_log', 'v_ssd_d', 'v_ssd_norm', 'v_ssd_w_out', 'v_mla_w_in', 'v_mla_q_a_norm', 'v_mla_w_q_b', 'v_mla_kv_a_norm', 'v_mla_w_kv_b', 'v_mla_q_norm', 'v_mla_k_norm', 'v_mla_w_out', 'v_mlp_w_up', 'v_mlp_w_down']
TWIN_OUTPUTS = ['loss', 'grad_x', 'grad_meta_tokens', 'grad_ln_mix', 'grad_ln_mlp', 'grad_ssd_w_in', 'grad_ssd_conv_w', 'grad_ssd_conv_b', 'grad_ssd_dt_bias', 'grad_ssd_a_log', 'grad_ssd_d', 'grad_ssd_norm', 'grad_ssd_w_out', 'grad_mla_w_in', 'grad_mla_q_a_norm', 'grad_mla_w_q_b', 'grad_mla_kv_a_norm', 'grad_mla_w_kv_b', 'grad_mla_q_norm', 'grad_mla_k_norm', 'grad_mla_w_out', 'grad_mlp_w_up', 'grad_mlp_w_down', 'delta_meta_tokens', 'delta_ln_mix', 'delta_ln_mlp', 'delta_ssd_w_in', 'delta_ssd_conv_w', 'delta_ssd_conv_b', 'delta_ssd_dt_bias', 'delta_ssd_a_log', 'delta_ssd_d', 'delta_ssd_norm', 'delta_ssd_w_out', 'delta_mla_w_in', 'delta_mla_q_a_norm', 'delta_mla_w_q_b', 'delta_mla_kv_a_norm', 'delta_mla_w_kv_b', 'delta_mla_q_norm', 'delta_mla_k_norm', 'delta_mla_w_out', 'delta_mlp_w_up', 'delta_mlp_w_down', 'new_m_meta_tokens', 'new_m_ln_mix', 'new_m_ln_mlp', 'new_m_ssd_w_in', 'new_m_ssd_conv_w', 'new_m_ssd_conv_b', 'new_m_ssd_dt_bias', 'new_m_ssd_a_log', 'new_m_ssd_d', 'new_m_ssd_norm', 'new_m_ssd_w_out', 'new_m_mla_w_in', 'new_m_mla_q_a_norm', 'new_m_mla_w_q_b', 'new_m_mla_kv_a_norm', 'new_m_mla_w_kv_b', 'new_m_mla_q_norm', 'new_m_mla_k_norm', 'new_m_mla_w_out', 'new_m_mlp_w_up', 'new_m_mlp_w_down', 'new_v_meta_tokens', 'new_v_ln_mix', 'new_v_ln_mlp', 'new_v_ssd_w_in', 'new_v_ssd_conv_w', 'new_v_ssd_conv_b', 'new_v_ssd_dt_bias', 'new_v_ssd_a_log', 'new_v_ssd_d', 'new_v_ssd_norm', 'new_v_ssd_w_out', 'new_v_mla_w_in', 'new_v_mla_q_a_norm', 'new_v_mla_w_q_b', 'new_v_mla_kv_a_norm', 'new_v_mla_w_kv_b', 'new_v_mla_q_norm', 'new_v_mla_k_norm', 'new_v_mla_w_out', 'new_v_mlp_w_up', 'new_v_mlp_w_down']
TWIN_LEAF_KINDS = {'loss': 'loss', 'grad_x': 'grad_x', 'grad_meta_tokens': 'grad_w', 'grad_ln_mix': 'grad_w', 'grad_ln_mlp': 'grad_w', 'grad_ssd_w_in': 'grad_w', 'grad_ssd_conv_w': 'grad_w', 'grad_ssd_conv_b': 'grad_w', 'grad_ssd_dt_bias': 'grad_w', 'grad_ssd_a_log': 'grad_w', 'grad_ssd_d': 'grad_w', 'grad_ssd_norm': 'grad_w', 'grad_ssd_w_out': 'grad_w', 'grad_mla_w_in': 'grad_w', 'grad_mla_q_a_norm': 'grad_w', 'grad_mla_w_q_b': 'grad_w', 'grad_mla_kv_a_norm': 'grad_w', 'grad_mla_w_kv_b': 'grad_w', 'grad_mla_q_norm': 'grad_w', 'grad_mla_k_norm': 'grad_w', 'grad_mla_w_out': 'grad_w', 'grad_mlp_w_up': 'grad_w', 'grad_mlp_w_down': 'grad_w', 'delta_meta_tokens': 'delta_w', 'delta_ln_mix': 'delta_w', 'delta_ln_mlp': 'delta_w', 'delta_ssd_w_in': 'delta_w', 'delta_ssd_conv_w': 'delta_w', 'delta_ssd_conv_b': 'delta_w', 'delta_ssd_dt_bias': 'delta_w', 'delta_ssd_a_log': 'delta_w', 'delta_ssd_d': 'delta_w', 'delta_ssd_norm': 'delta_w', 'delta_ssd_w_out': 'delta_w', 'delta_mla_w_in': 'delta_w', 'delta_mla_q_a_norm': 'delta_w', 'delta_mla_w_q_b': 'delta_w', 'delta_mla_kv_a_norm': 'delta_w', 'delta_mla_w_kv_b': 'delta_w', 'delta_mla_q_norm': 'delta_w', 'delta_mla_k_norm': 'delta_w', 'delta_mla_w_out': 'delta_w', 'delta_mlp_w_up': 'delta_w', 'delta_mlp_w_down': 'delta_w', 'new_m_meta_tokens': 'new_m', 'new_m_ln_mix': 'new_m', 'new_m_ln_mlp': 'new_m', 'new_m_ssd_w_in': 'new_m', 'new_m_ssd_conv_w': 'new_m', 'new_m_ssd_conv_b': 'new_m', 'new_m_ssd_dt_bias': 'new_m', 'new_m_ssd_a_log': 'new_m', 'new_m_ssd_d': 'new_m', 'new_m_ssd_norm': 'new_m', 'new_m_ssd_w_out': 'new_m', 'new_m_mla_w_in': 'new_m', 'new_m_mla_q_a_norm': 'new_m', 'new_m_mla_w_q_b': 'new_m', 'new_m_mla_kv_a_norm': 'new_m', 'new_m_mla_w_kv_b': 'new_m', 'new_m_mla_q_norm': 'new_m', 'new_m_mla_k_norm': 'new_m', 'new_m_mla_w_out': 'new_m', 'new_m_mlp_w_up': 'new_m', 'new_m_mlp_w_down': 'new_m', 'new_v_meta_tokens': 'new_v', 'new_v_ln_mix': 'new_v', 'new_v_ln_mlp': 'new_v', 'new_v_ssd_w_in': 'new_v', 'new_v_ssd_conv_w': 'new_v', 'new_v_ssd_conv_b': 'new_v', 'new_v_ssd_dt_bias': 'new_v', 'new_v_ssd_a_log': 'new_v', 'new_v_ssd_d': 'new_v', 'new_v_ssd_norm': 'new_v', 'new_v_ssd_w_out': 'new_v', 'new_v_mla_w_in': 'new_v', 'new_v_mla_q_a_norm': 'new_v', 'new_v_mla_w_q_b': 'new_v', 'new_v_mla_kv_a_norm': 'new_v', 'new_v_mla_w_kv_b': 'new_v', 'new_v_mla_q_norm': 'new_v', 'new_v_mla_k_norm': 'new_v', 'new_v_mla_w_out': 'new_v', 'new_v_mlp_w_up': 'new_v', 'new_v_mlp_w_down': 'new_v'}


def _forward(args):
    return _fwd_reference(*[args[k] for k in FWD_PARAMS])


def _output_shape():
    out = _jax.eval_shape(lambda: _forward(_fwd_setup_inputs(0)))
    return out.shape, out.dtype

N_MICROBATCH = 1
ADAM_LR = 0.001
ADAM_B1 = 0.9
ADAM_B2 = 0.999
ADAM_EPS = 1e-08
ADAM_WD = 0.01
ADAM_STEP = 10
PER_EXAMPLE_BATCH_AXIS = {'x': 0, 'loss_target': 0}
SHARED_INPUTS = []
_WEIGHT_DTYPES = {'meta_tokens': _jnp.float32, 'ln_mix': _jnp.float32, 'ln_mlp': _jnp.float32, 'ssd_w_in': _jnp.float32, 'ssd_conv_w': _jnp.float32, 'ssd_conv_b': _jnp.float32, 'ssd_dt_bias': _jnp.float32, 'ssd_a_log': _jnp.float32, 'ssd_d': _jnp.float32, 'ssd_norm': _jnp.float32, 'ssd_w_out': _jnp.float32, 'mla_w_in': _jnp.float32, 'mla_q_a_norm': _jnp.float32, 'mla_w_q_b': _jnp.float32, 'mla_kv_a_norm': _jnp.float32, 'mla_w_kv_b': _jnp.float32, 'mla_q_norm': _jnp.float32, 'mla_k_norm': _jnp.float32, 'mla_w_out': _jnp.float32, 'mlp_w_up': _jnp.float32, 'mlp_w_down': _jnp.float32}
MOMENT_SCALE = {'meta_tokens': 3.812929e-01, 'ln_mix': 1.715022e+01, 'ln_mlp': 1.013851e+02, 'ssd_w_in': 4.131868e+00, 'ssd_conv_w': 6.095019e+00, 'ssd_conv_b': 1.917448e+01, 'ssd_dt_bias': 4.512455e+00, 'ssd_a_log': 3.647498e+01, 'ssd_d': 3.405964e+01, 'ssd_norm': 3.275421e+01, 'ssd_w_out': 1.844813e+01, 'mla_w_in': 2.791319e+01, 'mla_q_a_norm': 1.252520e+00, 'mla_w_q_b': 6.365291e-01, 'mla_kv_a_norm': 4.935368e+01, 'mla_w_kv_b': 1.429545e+01, 'mla_q_norm': 2.502386e+00, 'mla_k_norm': 2.483770e+00, 'mla_w_out': 1.923650e+01, 'mlp_w_up': 9.344404e+00, 'mlp_w_down': 3.479704e+01}


def _to_microbatches(a, axis):
    t = _jnp.moveaxis(a, axis, 0)
    t = t.reshape((N_MICROBATCH, t.shape[0] // N_MICROBATCH) + t.shape[1:])
    return _jnp.moveaxis(t, 1, axis + 1)


def setup_inputs(seed: int = 0) -> dict:
    inp = _fwd_setup_inputs(seed)
    key = _jax.random.fold_in(_jax.random.key(seed), 7919)
    shape, _ = _output_shape()
    out = dict(inp)
    out["loss_target"] = _jax.random.normal(_jax.random.fold_in(key, 0), shape, _jnp.float32)
    for i, name in enumerate(TWIN_WEIGHTS):
        w = inp[name].astype(_jnp.float32)
        if MOMENT_SCALE is None:
            s = _jnp.sqrt(_jnp.mean(_jnp.square(w)) + 1e-30)
        else:
            s = MOMENT_SCALE[name]
        km, kv = _jax.random.split(_jax.random.fold_in(key, i + 1))
        out[name] = w
        out["m_" + name] = s * _jax.random.normal(km, w.shape, _jnp.float32)
        out["v_" + name] = (s * s) * _jax.random.uniform(kv, w.shape, _jnp.float32, 0.5, 1.5)
    if N_MICROBATCH > 1:
        for name, axis in PER_EXAMPLE_BATCH_AXIS.items():
            out[name] = _to_microbatches(out[name], axis)
    return {'x': out['x'], 'meta_tokens': out['meta_tokens'], 'ln_mix': out['ln_mix'], 'ln_mlp': out['ln_mlp'], 'ssd_w_in': out['ssd_w_in'], 'ssd_conv_w': out['ssd_conv_w'], 'ssd_conv_b': out['ssd_conv_b'], 'ssd_dt_bias': out['ssd_dt_bias'], 'ssd_a_log': out['ssd_a_log'], 'ssd_d': out['ssd_d'], 'ssd_norm': out['ssd_norm'], 'ssd_w_out': out['ssd_w_out'], 'mla_w_in': out['mla_w_in'], 'mla_q_a_norm': out['mla_q_a_norm'], 'mla_w_q_b': out['mla_w_q_b'], 'mla_kv_a_norm': out['mla_kv_a_norm'], 'mla_w_kv_b': out['mla_w_kv_b'], 'mla_q_norm': out['mla_q_norm'], 'mla_k_norm': out['mla_k_norm'], 'mla_w_out': out['mla_w_out'], 'mlp_w_up': out['mlp_w_up'], 'mlp_w_down': out['mlp_w_down'], 'loss_target': out['loss_target'], 'm_meta_tokens': out['m_meta_tokens'], 'm_ln_mix': out['m_ln_mix'], 'm_ln_mlp': out['m_ln_mlp'], 'm_ssd_w_in': out['m_ssd_w_in'], 'm_ssd_conv_w': out['m_ssd_conv_w'], 'm_ssd_conv_b': out['m_ssd_conv_b'], 'm_ssd_dt_bias': out['m_ssd_dt_bias'], 'm_ssd_a_log': out['m_ssd_a_log'], 'm_ssd_d': out['m_ssd_d'], 'm_ssd_norm': out['m_ssd_norm'], 'm_ssd_w_out': out['m_ssd_w_out'], 'm_mla_w_in': out['m_mla_w_in'], 'm_mla_q_a_norm': out['m_mla_q_a_norm'], 'm_mla_w_q_b': out['m_mla_w_q_b'], 'm_mla_kv_a_norm': out['m_mla_kv_a_norm'], 'm_mla_w_kv_b': out['m_mla_w_kv_b'], 'm_mla_q_norm': out['m_mla_q_norm'], 'm_mla_k_norm': out['m_mla_k_norm'], 'm_mla_w_out': out['m_mla_w_out'], 'm_mlp_w_up': out['m_mlp_w_up'], 'm_mlp_w_down': out['m_mlp_w_down'], 'v_meta_tokens': out['v_meta_tokens'], 'v_ln_mix': out['v_ln_mix'], 'v_ln_mlp': out['v_ln_mlp'], 'v_ssd_w_in': out['v_ssd_w_in'], 'v_ssd_conv_w': out['v_ssd_conv_w'], 'v_ssd_conv_b': out['v_ssd_conv_b'], 'v_ssd_dt_bias': out['v_ssd_dt_bias'], 'v_ssd_a_log': out['v_ssd_a_log'], 'v_ssd_d': out['v_ssd_d'], 'v_ssd_norm': out['v_ssd_norm'], 'v_ssd_w_out': out['v_ssd_w_out'], 'v_mla_w_in': out['v_mla_w_in'], 'v_mla_q_a_norm': out['v_mla_q_a_norm'], 'v_mla_w_q_b': out['v_mla_w_q_b'], 'v_mla_kv_a_norm': out['v_mla_kv_a_norm'], 'v_mla_w_kv_b': out['v_mla_w_kv_b'], 'v_mla_q_norm': out['v_mla_q_norm'], 'v_mla_k_norm': out['v_mla_k_norm'], 'v_mla_w_out': out['v_mla_w_out'], 'v_mlp_w_up': out['v_mlp_w_up'], 'v_mlp_w_down': out['v_mlp_w_down']}


def _loss(weights, diff, rest, loss_target):
    with _jax.named_scope("forward"):
        args = {**rest, TWIN_DIFF_INPUT: diff, **{k: w.astype(_WEIGHT_DTYPES[k]) for k, w in weights.items()}}
        y = _forward(args)
    with _jax.named_scope("loss_head"):
        err = _jnp.square(y.astype(_jnp.float32) - loss_target)
        return 0.5 * _jnp.sum(_jnp.mean(err, axis=-1)) if err.ndim else 0.5 * err


def _adamw(w, g, m, v):
    m = ADAM_B1 * m + (1.0 - ADAM_B1) * g
    v = ADAM_B2 * v + (1.0 - ADAM_B2) * _jnp.square(g)
    m_hat = m / (1.0 - ADAM_B1 ** ADAM_STEP)
    v_hat = v / (1.0 - ADAM_B2 ** ADAM_STEP)
    delta = -ADAM_LR * (m_hat / (_jnp.sqrt(v_hat) + ADAM_EPS) + ADAM_WD * w)
    return delta, m, v


def reference(x, meta_tokens, ln_mix, ln_mlp, ssd_w_in, ssd_conv_w, ssd_conv_b, ssd_dt_bias, ssd_a_log, ssd_d, ssd_norm, ssd_w_out, mla_w_in, mla_q_a_norm, mla_w_q_b, mla_kv_a_norm, mla_w_kv_b, mla_q_norm, mla_k_norm, mla_w_out, mlp_w_up, mlp_w_down, loss_target, m_meta_tokens, m_ln_mix, m_ln_mlp, m_ssd_w_in, m_ssd_conv_w, m_ssd_conv_b, m_ssd_dt_bias, m_ssd_a_log, m_ssd_d, m_ssd_norm, m_ssd_w_out, m_mla_w_in, m_mla_q_a_norm, m_mla_w_q_b, m_mla_kv_a_norm, m_mla_w_kv_b, m_mla_q_norm, m_mla_k_norm, m_mla_w_out, m_mlp_w_up, m_mlp_w_down, v_meta_tokens, v_ln_mix, v_ln_mlp, v_ssd_w_in, v_ssd_conv_w, v_ssd_conv_b, v_ssd_dt_bias, v_ssd_a_log, v_ssd_d, v_ssd_norm, v_ssd_w_out, v_mla_w_in, v_mla_q_a_norm, v_mla_w_q_b, v_mla_kv_a_norm, v_mla_w_kv_b, v_mla_q_norm, v_mla_k_norm, v_mla_w_out, v_mlp_w_up, v_mlp_w_down):
    given = dict(x=x, meta_tokens=meta_tokens, ln_mix=ln_mix, ln_mlp=ln_mlp, ssd_w_in=ssd_w_in, ssd_conv_w=ssd_conv_w, ssd_conv_b=ssd_conv_b, ssd_dt_bias=ssd_dt_bias, ssd_a_log=ssd_a_log, ssd_d=ssd_d, ssd_norm=ssd_norm, ssd_w_out=ssd_w_out, mla_w_in=mla_w_in, mla_q_a_norm=mla_q_a_norm, mla_w_q_b=mla_w_q_b, mla_kv_a_norm=mla_kv_a_norm, mla_w_kv_b=mla_w_kv_b, mla_q_norm=mla_q_norm, mla_k_norm=mla_k_norm, mla_w_out=mla_w_out, mlp_w_up=mlp_w_up, mlp_w_down=mlp_w_down, loss_target=loss_target, m_meta_tokens=m_meta_tokens, m_ln_mix=m_ln_mix, m_ln_mlp=m_ln_mlp, m_ssd_w_in=m_ssd_w_in, m_ssd_conv_w=m_ssd_conv_w, m_ssd_conv_b=m_ssd_conv_b, m_ssd_dt_bias=m_ssd_dt_bias, m_ssd_a_log=m_ssd_a_log, m_ssd_d=m_ssd_d, m_ssd_norm=m_ssd_norm, m_ssd_w_out=m_ssd_w_out, m_mla_w_in=m_mla_w_in, m_mla_q_a_norm=m_mla_q_a_norm, m_mla_w_q_b=m_mla_w_q_b, m_mla_kv_a_norm=m_mla_kv_a_norm, m_mla_w_kv_b=m_mla_w_kv_b, m_mla_q_norm=m_mla_q_norm, m_mla_k_norm=m_mla_k_norm, m_mla_w_out=m_mla_w_out, m_mlp_w_up=m_mlp_w_up, m_mlp_w_down=m_mlp_w_down, v_meta_tokens=v_meta_tokens, v_ln_mix=v_ln_mix, v_ln_mlp=v_ln_mlp, v_ssd_w_in=v_ssd_w_in, v_ssd_conv_w=v_ssd_conv_w, v_ssd_conv_b=v_ssd_conv_b, v_ssd_dt_bias=v_ssd_dt_bias, v_ssd_a_log=v_ssd_a_log, v_ssd_d=v_ssd_d, v_ssd_norm=v_ssd_norm, v_ssd_w_out=v_ssd_w_out, v_mla_w_in=v_mla_w_in, v_mla_q_a_norm=v_mla_q_a_norm, v_mla_w_q_b=v_mla_w_q_b, v_mla_kv_a_norm=v_mla_kv_a_norm, v_mla_w_kv_b=v_mla_w_kv_b, v_mla_q_norm=v_mla_q_norm, v_mla_k_norm=v_mla_k_norm, v_mla_w_out=v_mla_w_out, v_mlp_w_up=v_mlp_w_up, v_mlp_w_down=v_mlp_w_down)
    weights = {n: given[n] for n in TWIN_WEIGHTS}
    shared = {n: given[n] for n in SHARED_INPUTS}
    per_example = {n: given[n] for n in ['x']}
    grad_fn = _jax.value_and_grad(_loss, argnums=(0, 1))

    def one_microbatch(ex, loss_target):
        ex = dict(ex)
        diff = ex.pop(TWIN_DIFF_INPUT)
        return grad_fn(weights, diff, {**shared, **ex}, loss_target)

    if N_MICROBATCH == 1:
        loss, (grad_w, grad_x) = one_microbatch(per_example, given["loss_target"])
    else:
        def body(carry, xs):
            loss_sum, grad_sum = carry
            l_k, (gw_k, gx_k) = one_microbatch(xs[0], xs[1])
            with _jax.named_scope("update"):
                return (loss_sum + l_k, _jax.tree.map(_jnp.add, grad_sum, gw_k)), gx_k

        init = (_jnp.zeros((), _jnp.float32), _jax.tree.map(_jnp.zeros_like, weights))
        (loss, grad_w), grad_x = _jax.lax.scan(body, init, (per_example, given["loss_target"]))
    with _jax.named_scope("update"):
        delta_w, new_m, new_v = {}, {}, {}
        for n in TWIN_WEIGHTS:
            delta_w[n], new_m[n], new_v[n] = _adamw(weights[n], grad_w[n], given["m_" + n], given["v_" + n])
    return (loss, grad_x, *[grad_w[n] for n in TWIN_WEIGHTS], *[delta_w[n] for n in TWIN_WEIGHTS],
            *[new_m[n] for n in TWIN_WEIGHTS], *[new_v[n] for n in TWIN_WEIGHTS])
```

```python
import functools

import jax
import jax.numpy as jnp
import numpy as np
from jax import lax
from jax.experimental import pallas as pl
from jax.experimental.pallas import tpu as pltpu

F32 = jnp.float32
BF16 = jnp.bfloat16

EPS = 1e-6
N_META = 16
CHUNK = 128
PAD = CHUNK - N_META
SSD_HEAD_DIM = 64
SSD_HEADS = 32
SSD_GROUPS = 8
SSD_HPG = 4
SSD_STATE = 128
SSD_D_INNER = 2048
SSD_CONV = 4
MLA_HEADS = 16
MLA_NOPE = 64
MLA_ROPE = 32
MLA_V = 64
MLA_QK = 96
MLA_Q_RANK = 384
MLA_KV_RANK = 256
ROPE_THETA = 10000.0
LANES = 128
SUBLANES = 8
N_DEV = 8
VMEM_LIMIT = 56 * 1024 * 1024

ADAM_LR = 0.001
ADAM_B1 = 0.9
ADAM_B2 = 0.999
ADAM_EPS = 1e-08
ADAM_WD = 0.01
ADAM_STEP = 10

NEG = -1e30


def _row_tile(rows):
    return 384 if (rows % 384 == 0 and rows > 384) else 128


def _params(n_axes, vmem=VMEM_LIMIT):
    return pltpu.CompilerParams(dimension_semantics=("arbitrary",) * n_axes, vmem_limit_bytes=vmem)


def _dot(a, b, dims):
    return lax.dot_general(a.astype(BF16), b.astype(BF16), (dims, ((), ())), preferred_element_type=F32)


NN = ((1,), (0,))
NT = ((1,), (1,))
TN = ((0,), (0,))


@jax.custom_vjp
def bdot_nn(a, b):
    return _dot(a, b, NN)


@jax.custom_vjp
def bdot_nt(a, b):
    return _dot(a, b, NT)


@jax.custom_vjp
def bdot_tn(a, b):
    return _dot(a, b, TN)


bdot_nn.defvjp(lambda a, b: (_dot(a, b, NN), (a, b)),
               lambda r, g: (_dot(g, r[1], NT), _dot(r[0], g, TN)))
bdot_nt.defvjp(lambda a, b: (_dot(a, b, NT), (a, b)),
               lambda r, g: (_dot(g, r[1], NN), _dot(g, r[0], TN)))
bdot_tn.defvjp(lambda a, b: (_dot(a, b, TN), (a, b)),
               lambda r, g: (_dot(r[1], g, NT), _dot(r[0], g, NN)))


def _rows8(v):
    r, n = v.shape
    return v.reshape(r // SUBLANES, SUBLANES, n).sum(axis=0)


def _row_mask(i, tm):
    return (i * tm + lax.broadcasted_iota(jnp.int32, (tm, 1), 0)) >= PAD


def fused_mm(name, *, rows, k, n, tm, tn, tk=None, a_ops, pro, w, w_block, w_imap, dot, e_ops=(), epi, outs):
    tk = tk or k
    ni, nj, nk = rows // tm, n // tn, k // tk
    assert rows % tm == 0 and n % tn == 0 and k % tk == 0
    assert nk == 1 or nj == 1
    cache = nk == 1 and nj > 1
    na, ne, no = len(a_ops), len(e_ops), len(outs)

    def body(*refs):
        a_refs = refs[:na]
        w_ref = refs[na]
        e_refs = refs[na + 1:na + 1 + ne]
        o_refs = refs[na + 1 + ne:na + 1 + ne + no]
        scr = refs[na + 1 + ne + no:]
        i, j, kk = pl.program_id(0), pl.program_id(1), pl.program_id(2)
        if cache:
            a_scr = scr[0]

            @pl.when(j == 0)
            def _():
                a_scr[...] = pro(a_refs, o_refs, i)

            a = a_scr[...]
        else:
            a = pro(a_refs, o_refs, i)
        part = dot(a, w_ref)
        if nk == 1:
            epi(part, e_refs, o_refs, i, j)
        else:
            acc_ref = scr[0]

            @pl.when(kk == 0)
            def _():
                acc_ref[...] = part

            @pl.when(kk > 0)
            def _():
                acc_ref[...] += part

            @pl.when(kk == nk - 1)
            def _():
                epi(acc_ref[...], e_refs, o_refs, i, j)

    scratch = []
    if cache:
        scratch.append(pltpu.VMEM((tm, k), BF16))
    if nk > 1:
        scratch.append(pltpu.VMEM((tm, tn), F32))
    in_specs = [pl.BlockSpec(b, m) for (_, b, m) in a_ops]
    in_specs.append(pl.BlockSpec(w_block, w_imap))
    in_specs += [pl.BlockSpec(b, m) for (_, b, m) in e_ops]
    return pl.pallas_call(
        body, name=name, grid=(ni, nj, nk),
        in_specs=in_specs,
        out_specs=[pl.BlockSpec(b, m) for (_, _, b, m) in outs],
        out_shape=[jax.ShapeDtypeStruct(s, d) for (s, d, _, _) in outs],
        scratch_shapes=scratch,
        compiler_params=_params(3),
    )(*[a for (a, _, _) in a_ops], w, *[e for (e, _, _) in e_ops])


def _lw(w, block, imap):
    if isinstance(w, tuple):
        arr, layer = w
        return arr, (None,) + block, (lambda i, j, kk: (layer,) + imap(i, j, kk))
    return w, block, imap


def _dot_w(a, w_ref):
    return jnp.dot(a, w_ref[...], preferred_element_type=F32)


def _dot_wt(a, w_ref):
    return lax.dot_general(a, w_ref[...], (NT, ((), ())), preferred_element_type=F32)


def _rms_pro(h, g):
    r = lax.rsqrt(jnp.mean(h * h, axis=-1, keepdims=True) + EPS)
    return h * r * g


def _rms_bwd(dyn, h, g):
    r = lax.rsqrt(jnp.mean(h * h, axis=-1, keepdims=True) + EPS)
    xh = h * r
    t = dyn * g
    dh = r * (t - xh * jnp.mean(t * xh, axis=-1, keepdims=True))
    return dh, _rows8(dyn * xh)


def _acc_out(ref, val, first):
    @pl.when(first)
    def _():
        ref[...] = val

    @pl.when(jnp.logical_not(first))
    def _():
        ref[...] += val


def norm_mm(name, h, g, w, *, tn, k_cols=None, col_block=0, w_stacked=False):
    rows = h.shape[0]
    k = k_cols or h.shape[1]
    wshape = (w[0].shape[1:] if isinstance(w, tuple) else w.shape)
    n = wshape[0] * wshape[2] if w_stacked else wshape[1]
    tm = _row_tile(rows)

    def pro(a_refs, o_refs, i):
        hn = _rms_pro(a_refs[0][...], a_refs[1][...]).astype(BF16)
        o_refs[1][...] = hn
        return hn

    def epi(acc, e_refs, o_refs, i, j):
        o_refs[0][...] = acc

    if w_stacked:
        w_block, w_imap = (None, k, tn), (lambda i, j, kk: (j, 0, 0))
    else:
        w_block, w_imap = (k, tn), (lambda i, j, kk: (0, j))
    w, w_block, w_imap = _lw(w, w_block, w_imap)
    return fused_mm(
        name, rows=rows, k=k, n=n, tm=tm, tn=tn,
        a_ops=[(h, (tm, k), lambda i, j, kk: (i, col_block)), (g, (1, k), lambda i, j, kk: (0, 0))],
        pro=pro, w=w, w_block=w_block, w_imap=w_imap, dot=_dot_w, epi=epi,
        outs=[((rows, n), F32, (tm, tn), lambda i, j, kk: (i, j)),
              ((rows, k), BF16, (tm, k), lambda i, j, kk: (i, 0))])


def res_mm(name, a_ops, pro, k, w, res, *, tn, save_dtype=None):
    rows, n = res.shape
    tm = _row_tile(rows)

    def pro2(a_refs, o_refs, i):
        a = pro(a_refs)
        if save_dtype is not None:
            o_refs[1][...] = a
        return a

    def epi(acc, e_refs, o_refs, i, j):
        o_refs[0][...] = e_refs[0][...] + acc

    outs = [((rows, n), F32, (tm, tn), lambda i, j, kk: (i, j))]
    if save_dtype is not None:
        outs.append(((rows, k), save_dtype, (tm, k), lambda i, j, kk: (i, 0)))
    w, w_block, w_imap = _lw(w, (k, tn), lambda i, j, kk: (0, j))
    out = fused_mm(
        name, rows=rows, k=k, n=n, tm=tm, tn=tn,
        a_ops=[(a, (tm if b[0] is None else b[0], b[1]), m) for (a, b, m) in a_ops],
        pro=pro2, w=w, w_block=w_block, w_imap=w_imap, dot=_dot_w,
        e_ops=[(res, (tm, tn), lambda i, j, kk: (i, j))], epi=epi, outs=outs)
    return out if save_dtype is not None else out[0]


def wgrad_mm(name, a_ops, pro_a, g_ops, pro_g, *, rows, k1, n, t1, tn, out_shape=None, out_block=None, out_imap=None):
    tt = _row_tile(rows)
    n1, n2, nt = k1 // t1, n // tn, rows // tt
    assert k1 % t1 == 0 and n % tn == 0
    na = len(a_ops)

    def body(*refs):
        a_refs = refs[:na]
        g_refs = refs[na:-1]
        o_ref = refs[-1]
        t = pl.program_id(2)
        a = pro_a(a_refs).astype(BF16)
        g = pro_g(g_refs).astype(BF16)
        part = lax.dot_general(a, g, (TN, ((), ())), preferred_element_type=F32)
        _acc_out(o_ref, part.reshape(o_ref.shape), t == 0)

    return pl.pallas_call(
        body, name=name, grid=(n1, n2, nt),
        in_specs=[pl.BlockSpec(b, m) for (_, b, m) in list(a_ops) + list(g_ops)],
        out_specs=pl.BlockSpec(out_block or (t1, tn), out_imap or (lambda a, b, t: (a, b))),
        out_shape=jax.ShapeDtypeStruct(out_shape or (k1, n), F32),
        compiler_params=_params(3),
    )(*[a for (a, _, _) in list(a_ops) + list(g_ops)])


def simple_wgrad(name, a, g, *, a_cols=None, a_col_block=0, t1=None, tn=None, **kw):
    rows = a.shape[0]
    k1 = a_cols or a.shape[1]
    n = g.shape[1]
    tt = _row_tile(rows)
    t1 = t1 or min(k1, 512)
    tn = tn or min(n, 1024)
    return wgrad_mm(
        name,
        [(a, (tt, t1), lambda x, y, t: (t, x + a_col_block * (k1 // t1)))], lambda r: r[0][...],
        [(g, (tt, tn), lambda x, y, t: (t, y))], lambda r: r[0][...],
        rows=rows, k1=k1, n=n, t1=t1, tn=tn, **kw)


def rms_bwd_mm(name, dz_ops, pro, k, w, w_block, w_imap, dot, h, g, dh, *, tk=None, h_cols=None, h_col_block=0,
               add_dh=True):
    rows = h.shape[0]
    n = h_cols or h.shape[1]
    tm = _row_tile(rows)
    ni = rows // tm
    w, w_block, w_imap = _lw(w, w_block, w_imap)

    def epi(acc, e_refs, o_refs, i, j):
        d, dg = _rms_bwd(acc, e_refs[0][...], e_refs[1][...])
        if add_dh:
            d = d + e_refs[2][...]
        o_refs[0][...] = jnp.where(_row_mask(i, tm), d, 0.0)
        _acc_out(o_refs[1], dg, i == 0)

    e_ops = [(h, (tm, n), lambda i, j, kk: (i, h_col_block)), (g, (1, n), lambda i, j, kk: (0, 0))]
    if add_dh:
        e_ops.append((dh, (tm, n), lambda i, j, kk: (i, 0)))
    return fused_mm(
        name, rows=rows, k=k, n=n, tm=tm, tn=n, tk=tk,
        a_ops=[(a, (tm if b[0] is None else b[0], b[1]), m) for (a, b, m) in dz_ops],
        pro=lambda a_refs, o_refs, i: pro(a_refs), w=w, w_block=w_block, w_imap=w_imap, dot=dot,
        e_ops=e_ops, epi=epi,
        outs=[((rows, n), F32, (tm, n), lambda i, j, kk: (i, 0)),
              ((SUBLANES, n), F32, (SUBLANES, n), lambda i, j, kk: (0, 0))])


def _relu2(u):
    r = jnp.maximum(u, 0.0)
    return r * r


def mlp_fwd(tag, h, g, w_up_st, w_down):
    d_ff = w_down[0].shape[1]
    u, hn = norm_mm(f"mlp_up_{tag}", h, g, w_up_st, tn=w_up_st[0].shape[3], w_stacked=True)
    out = res_mm(f"mlp_down_{tag}", [(u, (None, d_ff), lambda i, j, kk: (i, 0))],
                 lambda r: _relu2(r[0][...]).astype(BF16), d_ff, w_down, h, tn=512)
    return out, (h, hn, u)


def mlp_bwd(tag, dh, saved, g, w_up_st, w_down):
    h, hn, u = saved
    rows, d = h.shape
    d_ff = w_down[0].shape[1]
    ts = w_up_st[0].shape[3]
    tm = _row_tile(rows)
    wd, wd_block, wd_imap = _lw(w_down, (512, d), lambda i, j, kk: (j, 0))

    def epi_du(acc, e_refs, o_refs, i, j):
        o_refs[0][...] = (acc * (2.0 * jnp.maximum(e_refs[0][...], 0.0))).astype(BF16)

    du, = fused_mm(
        f"mlp_du_{tag}", rows=rows, k=d, n=d_ff, tm=tm, tn=512,
        a_ops=[(dh, (tm, d), lambda i, j, kk: (i, 0))], pro=lambda a, o, i: a[0][...].astype(BF16),
        w=wd, w_block=wd_block, w_imap=wd_imap, dot=_dot_wt,
        e_ops=[(u, (tm, 512), lambda i, j, kk: (i, j))], epi=epi_du,
        outs=[((rows, d_ff), BF16, (tm, 512), lambda i, j, kk: (i, j))])
    tt = tm
    dw_down = wgrad_mm(
        f"mlp_dwdown_{tag}",
        [(u, (tt, 512), lambda a, b, t: (t, a))], lambda r: _relu2(r[0][...]),
        [(dh, (tt, d), lambda a, b, t: (t, 0))], lambda r: r[0][...],
        rows=rows, k1=d_ff, n=d, t1=512, tn=d)
    dw_up = simple_wgrad(f"mlp_dwup_{tag}", hn, du, t1=d, tn=ts, out_shape=(N_DEV, d, ts), out_block=(None, d, ts),
                         out_imap=lambda a, b, t: (b, 0, 0))
    dh_in, dg = rms_bwd_mm(
        f"mlp_dh_{tag}", [(du, (None, ts), lambda i, j, kk: (i, kk))], lambda r: r[0][...], d_ff,
        w_up_st, (None, d, ts), lambda i, j, kk: (kk, 0, 0), _dot_wt, h, g, dh, tk=ts)
    return dh_in, dw_up, dw_down, dg


CONV_HALO = SUBLANES
CONV_TC = 512


def _silu(x):
    return x * jax.nn.sigmoid(x)


def _conv_pre(ext_ref, w, b, tm):
    pre = b
    for k in range(SSD_CONV):
        pre = pre + w[k:k + 1, :] * ext_ref[pl.ds(CONV_HALO - (SSD_CONV - 1) + k, tm), :]
    return pre


def conv_fwd(name, zx, col0, width, conv_w, conv_b):
    rows = zx.shape[0]
    tm = _row_tile(rows)
    cb0 = col0 // CONV_TC
    hb = tm // CONV_HALO

    def body(u_ref, halo_ref, w_ref, b_ref, o_ref, ext):
        i = pl.program_id(1)
        ext[pl.ds(0, CONV_HALO), :] = jnp.where(i > 0, halo_ref[...], 0.0)
        ext[pl.ds(CONV_HALO, tm), :] = u_ref[...]
        pre = _conv_pre(ext, w_ref[...], b_ref[...], tm)
        o_ref[...] = jnp.where(_row_mask(i, tm), _silu(pre), 0.0)

    return pl.pallas_call(
        body, name=name, grid=(width // CONV_TC, rows // tm),
        in_specs=[pl.BlockSpec((tm, CONV_TC), lambda j, i: (i, cb0 + j)),
                  pl.BlockSpec((CONV_HALO, CONV_TC), lambda j, i: (jnp.maximum(i * hb - 1, 0), cb0 + j)),
                  pl.BlockSpec((SSD_CONV, CONV_TC), lambda j, i: (0, j)),
                  pl.BlockSpec((1, CONV_TC), lambda j, i: (0, j))],
        out_specs=pl.BlockSpec((tm, CONV_TC), lambda j, i: (i, j)),
        out_shape=jax.ShapeDtypeStruct((rows, width), F32),
        scratch_shapes=[pltpu.VMEM((tm + CONV_HALO, CONV_TC), F32)],
        compiler_params=_params(2),
    )(zx, zx, conv_w, conv_b)


def conv_bwd(name, dact, zx, col0, conv_w, conv_b):
    rows, width = dact.shape
    tm = _row_tile(rows)
    ni = rows // tm
    cb0 = col0 // CONV_TC
    hb = tm // CONV_HALO

    def body(d_ref, u_ref, halo_ref, w_ref, b_ref, du_ref, dw_ref, db_ref, ext, dext):
        s = pl.program_id(1)
        i = ni - 1 - s
        w = w_ref[...]
        ext[pl.ds(0, CONV_HALO), :] = jnp.where(i > 0, halo_ref[...], 0.0)
        ext[pl.ds(CONV_HALO, tm), :] = u_ref[...]
        pre = _conv_pre(ext, w, b_ref[...], tm)
        sg = jax.nn.sigmoid(pre)
        dpre = jnp.where(_row_mask(i, tm), d_ref[...] * (sg * (1.0 + pre * (1.0 - sg))), 0.0)

        @pl.when(s == 0)
        def _():
            dext[pl.ds(tm, CONV_HALO), :] = jnp.zeros((CONV_HALO, CONV_TC), F32)

        dext[pl.ds(0, tm), :] = dpre
        du = jnp.zeros((tm, CONV_TC), F32)
        for k in range(SSD_CONV):
            du = du + w[k:k + 1, :] * dext[pl.ds(SSD_CONV - 1 - k, tm), :]
        du_ref[...] = du.astype(du_ref.dtype)
        _acc_out(db_ref, _rows8(dpre), s == 0)
        for k in range(SSD_CONV):
            uk = ext[pl.ds(CONV_HALO - (SSD_CONV - 1) + k, tm), :]
            _acc_out(dw_ref.at[pl.ds(k * SUBLANES, SUBLANES), :], _rows8(dpre * uk), s == 0)
        dext[pl.ds(tm, CONV_HALO), :] = dpre[0:CONV_HALO, :]

    return pl.pallas_call(
        body, name=name, grid=(width // CONV_TC, ni),
        in_specs=[pl.BlockSpec((tm, CONV_TC), lambda j, s: (ni - 1 - s, j)),
                  pl.BlockSpec((tm, CONV_TC), lambda j, s: (ni - 1 - s, cb0 + j)),
                  pl.BlockSpec((CONV_HALO, CONV_TC), lambda j, s: (jnp.maximum((ni - 1 - s) * hb - 1, 0), cb0 + j)),
                  pl.BlockSpec((SSD_CONV, CONV_TC), lambda j, s: (0, j)),
                  pl.BlockSpec((1, CONV_TC), lambda j, s: (0, j))],
        out_specs=[pl.BlockSpec((tm, CONV_TC), lambda j, s: (ni - 1 - s, j)),
                   pl.BlockSpec((SSD_CONV * SUBLANES, CONV_TC), lambda j, s: (0, j)),
                   pl.BlockSpec((SUBLANES, CONV_TC), lambda j, s: (0, j))],
        out_shape=[jax.ShapeDtypeStruct((rows, width), BF16),
                   jax.ShapeDtypeStruct((SSD_CONV * SUBLANES, width), F32),
                   jax.ShapeDtypeStruct((SUBLANES, width), F32)],
        scratch_shapes=[pltpu.VMEM((tm + CONV_HALO, CONV_TC), F32), pltpu.VMEM((tm + CONV_HALO, CONV_TC), F32)],
        compiler_params=_params(2),
    )(dact, zx, zx, conv_w, conv_b)


def _ssd_group(xs, bm, cm, dtraw, prev, par, g, c, tri):
    lane = lax.broadcasted_iota(jnp.int32, (1, LANES), 1)
    sub = lax.broadcasted_iota(jnp.int32, (LANES, 1), 0)
    li = lax.broadcasted_iota(jnp.int32, (CHUNK, CHUNK), 0)
    si = lax.broadcasted_iota(jnp.int32, (CHUNK, CHUNK), 1)
    causal = li >= si
    dt = jnp.where(_row_mask(c, CHUNK), jax.nn.softplus(dtraw + par[0:1, :]), 0.0)
    a = -jnp.exp(par[1:2, :])
    acs = jnp.dot(tri, dt * a, precision=lax.Precision.HIGHEST, preferred_element_type=F32)
    acs_t = acs.T
    cb = bdot_nt(cm, bm)
    ys, news = [], []
    for j in range(SSD_HPG):
        head = SSD_HPG * g + j
        oh = (lane == head).astype(F32)
        col = jnp.sum(acs * oh, axis=1, keepdims=True)
        dth = jnp.sum(dt * oh, axis=1, keepdims=True)
        row = jnp.sum(acs_t * (sub == head).astype(F32), axis=0, keepdims=True)
        d_skip = jnp.sum(par[2:3, :] * oh, axis=1, keepdims=True)
        last = col[CHUNK - 1:CHUNK, :]
        seg = jnp.where(causal, col - row, 0.0)
        decay = jnp.where(causal, jnp.exp(seg), 0.0)
        xh = xs[:, j * SSD_HEAD_DIM:(j + 1) * SSD_HEAD_DIM]
        xdt = xh * dth
        y_diag = bdot_nn(cb * decay, xdt)
        st = bdot_tn(xdt * jnp.exp(last - col), bm)
        ph = prev[j * SSD_HEAD_DIM:(j + 1) * SSD_HEAD_DIM, :]
        y_off = bdot_nt(cm, ph) * jnp.exp(col)
        ys.append(y_diag + y_off + xh * d_skip)
        news.append(ph * jnp.exp(last) + st)
    return jnp.concatenate(ys, axis=1), jnp.concatenate(news, axis=0)


def _tri():
    return jnp.asarray(np.tril(np.ones((CHUNK, CHUNK), np.float32)))


XS_W = SSD_HPG * SSD_HEAD_DIM


def ssd_fwd(name, xa, zx, dt_block, par):
    rows = xa.shape[0]
    nc = rows // CHUNK
    b0 = SSD_D_INNER // SSD_STATE

    def body(xs_ref, b_ref, c_ref, dt_ref, par_ref, tri_ref, y_ref, st_ref, state):
        c, g = pl.program_id(0), pl.program_id(1)

        @pl.when(c == 0)
        def _():
            state[g] = jnp.zeros((XS_W, SSD_STATE), F32)

        prev = state[g]
        st_ref[...] = prev
        y, new = _ssd_group(xs_ref[...], b_ref[...], c_ref[...], dt_ref[...], prev, par_ref[...], g, c, tri_ref[...])
        y_ref[...] = y
        state[g] = new

    return pl.pallas_call(
        body, name=name, grid=(nc, SSD_GROUPS),
        in_specs=[pl.BlockSpec((CHUNK, XS_W), lambda c, g: (c, g)),
                  pl.BlockSpec((CHUNK, SSD_STATE), lambda c, g: (c, b0 + g)),
                  pl.BlockSpec((CHUNK, SSD_STATE), lambda c, g: (c, b0 + SSD_GROUPS + g)),
                  pl.BlockSpec((CHUNK, LANES), lambda c, g: (c, dt_block)),
                  pl.BlockSpec((SUBLANES, LANES), lambda c, g: (0, 0)),
                  pl.BlockSpec((CHUNK, CHUNK), lambda c, g: (0, 0))],
        out_specs=[pl.BlockSpec((CHUNK, XS_W), lambda c, g: (c, g)),
                   pl.BlockSpec((None, None, XS_W, SSD_STATE), lambda c, g: (c, g, 0, 0))],
        out_shape=[jax.ShapeDtypeStruct((rows, SSD_D_INNER), F32),
                   jax.ShapeDtypeStruct((nc, SSD_GROUPS, XS_W, SSD_STATE), F32)],
        scratch_shapes=[pltpu.VMEM((SSD_GROUPS, XS_W, SSD_STATE), F32)],
        compiler_params=_params(2),
    )(xa, xa, xa, zx, par, _tri())


def ssd_bwd(name, dy, xa, zx, dt_block, states, par):
    rows = xa.shape[0]
    nc = rows // CHUNK
    b0 = SSD_D_INNER // SSD_STATE

    def body(dy_ref, xs_ref, b_ref, c_ref, dt_ref, st_ref, par_ref, tri_ref,
             dxs_ref, db_ref, dc_ref, ddt_ref, dpar_ref, dstate):
        s, g = pl.program_id(0), pl.program_id(1)
        c = nc - 1 - s

        @pl.when(s == 0)
        def _():
            dstate[g] = jnp.zeros((XS_W, SSD_STATE), F32)

        def f(xs, bm, cm, dtraw, prev, par_v):
            return _ssd_group(xs, bm, cm, dtraw, prev, par_v, g, c, tri_ref[...])

        _, vjp = jax.vjp(f, xs_ref[...], b_ref[...], c_ref[...], dt_ref[...], st_ref[...], par_ref[...])
        dxs, dbm, dcm, ddt, dprev, dpar = vjp((dy_ref[...], dstate[g]))
        dxs_ref[...] = dxs
        db_ref[...] = dbm
        dc_ref[...] = dcm
        dstate[g] = dprev
        _acc_out(ddt_ref, ddt, g == 0)
        _acc_out(dpar_ref, dpar, jnp.logical_and(s == 0, g == 0))

    return pl.pallas_call(
        body, name=name, grid=(nc, SSD_GROUPS),
        in_specs=[pl.BlockSpec((CHUNK, XS_W), lambda s, g: (nc - 1 - s, g)),
                  pl.BlockSpec((CHUNK, XS_W), lambda s, g: (nc - 1 - s, g)),
                  pl.BlockSpec((CHUNK, SSD_STATE), lambda s, g: (nc - 1 - s, b0 + g)),
                  pl.BlockSpec((CHUNK, SSD_STATE), lambda s, g: (nc - 1 - s, b0 + SSD_GROUPS + g)),
                  pl.BlockSpec((CHUNK, LANES), lambda s, g: (nc - 1 - s, dt_block)),
                  pl.BlockSpec((None, None, XS_W, SSD_STATE), lambda s, g: (nc - 1 - s, g, 0, 0)),
                  pl.BlockSpec((SUBLANES, LANES), lambda s, g: (0, 0)),
                  pl.BlockSpec((CHUNK, CHUNK), lambda s, g: (0, 0))],
        out_specs=[pl.BlockSpec((CHUNK, XS_W), lambda s, g: (nc - 1 - s, g)),
                   pl.BlockSpec((CHUNK, SSD_STATE), lambda s, g: (nc - 1 - s, g)),
                   pl.BlockSpec((CHUNK, SSD_STATE), lambda s, g: (nc - 1 - s, g)),
                   pl.BlockSpec((CHUNK, LANES), lambda s, g: (nc - 1 - s, 0)),
                   pl.BlockSpec((SUBLANES, LANES), lambda s, g: (0, 0))],
        out_shape=[jax.ShapeDtypeStruct((rows, SSD_D_INNER), F32),
                   jax.ShapeDtypeStruct((rows, SSD_GROUPS * SSD_STATE), F32),
                   jax.ShapeDtypeStruct((rows, SSD_GROUPS * SSD_STATE), F32),
                   jax.ShapeDtypeStruct((rows, LANES), F32),
                   jax.ShapeDtypeStruct((SUBLANES, LANES), F32)],
        scratch_shapes=[pltpu.VMEM((SSD_GROUPS, XS_W, SSD_STATE), F32)],
        compiler_params=_params(2),
    )(dy, xa, xa, xa, zx, states, par, _tri())


GN_W = SSD_D_INNER // SSD_GROUPS


def _gated_norm(y, z, ng):
    g = y * _silu(z)
    outs = []
    for q in range(SSD_GROUPS):
        gs = g[:, q * GN_W:(q + 1) * GN_W]
        outs.append(gs * lax.rsqrt(jnp.mean(gs * gs, axis=-1, keepdims=True) + EPS))
    return jnp.concatenate(outs, axis=1) * ng


def ssd_layer_fwd(tag, h, ln_g, w):
    zx, hn = norm_mm(f"ssd_in_{tag}", h, ln_g, w["w_in"], tn=896)
    xa = conv_fwd(f"ssd_conv_{tag}", zx, SSD_D_INNER, 2 * SSD_D_INNER, w["conv_w"], w["conv_b"])
    dt_block = 3 * SSD_D_INNER // LANES
    y, states = ssd_fwd(f"ssd_scan_{tag}", xa, zx, dt_block, w["par"])
    out, gn = res_mm(
        f"ssd_out_{tag}",
        [(y, (None, SSD_D_INNER), lambda i, j, kk: (i, 0)), (zx, (None, SSD_D_INNER), lambda i, j, kk: (i, 0)),
         (w["norm"], (1, SSD_D_INNER), lambda i, j, kk: (0, 0))],
        lambda r: _gated_norm(r[0][...], r[1][...], r[2][...]).astype(BF16),
        SSD_D_INNER, w["w_out"], h, tn=512, save_dtype=BF16)
    return out, (h, hn, zx, xa, y, states, gn)


def ssd_layer_bwd(tag, dh, saved, ln_g, w):
    h, hn, zx, xa, y, states, gn = saved
    rows, d = h.shape
    tm = _row_tile(rows)
    dt_block = 3 * SSD_D_INNER // LANES
    dw_out = simple_wgrad(f"ssd_dwout_{tag}", gn, dh, t1=512, tn=d)

    def epi_gate(acc, e_refs, o_refs, i, j):
        _, vjp = jax.vjp(_gated_norm, e_refs[0][...], e_refs[1][...], e_refs[2][...])
        dy, dz, dng = vjp(acc)
        o_refs[0][...] = dy
        o_refs[1][...] = dz.astype(BF16)
        row0 = lax.broadcasted_iota(jnp.int32, (SUBLANES, 1), 0) == 0
        _acc_out(o_refs[2], jnp.where(row0, dng, 0.0), i == 0)

    wo, wo_block, wo_imap = _lw(w["w_out"], (SSD_D_INNER, d), lambda i, j, kk: (0, 0))
    dy, dz, dnorm = fused_mm(
        f"ssd_dgate_{tag}", rows=rows, k=d, n=SSD_D_INNER, tm=tm, tn=SSD_D_INNER,
        a_ops=[(dh, (tm, d), lambda i, j, kk: (i, 0))], pro=lambda a, o, i: a[0][...].astype(BF16),
        w=wo, w_block=wo_block, w_imap=wo_imap, dot=_dot_wt,
        e_ops=[(y, (tm, SSD_D_INNER), lambda i, j, kk: (i, 0)), (zx, (tm, SSD_D_INNER), lambda i, j, kk: (i, 0)),
               (w["norm"], (1, SSD_D_INNER), lambda i, j, kk: (0, 0))],
        epi=epi_gate,
        outs=[((rows, SSD_D_INNER), F32, (tm, SSD_D_INNER), lambda i, j, kk: (i, 0)),
              ((rows, SSD_D_INNER), BF16, (tm, SSD_D_INNER), lambda i, j, kk: (i, 0)),
              ((SUBLANES, SSD_D_INNER), F32, (SUBLANES, SSD_D_INNER), lambda i, j, kk: (0, 0))])
    dxs, dbm, dcm, ddt, dpar = ssd_bwd(f"ssd_dscan_{tag}", dy, xa, zx, dt_block, states, w["par"])
    parts, dcw, dcb = [dz], [], []
    col = SSD_D_INNER
    for nm, dact in (("x", dxs), ("b", dbm), ("c", dcm)):
        wd = dact.shape[1]
        c0 = col - SSD_D_INNER
        du, dw_c, db_c = conv_bwd(f"ssd_dconv{nm}_{tag}", dact, zx, col, w["conv_w"][:, c0:c0 + wd],
                                  w["conv_b"][:, c0:c0 + wd])
        parts.append(du)
        dcw.append(dw_c)
        dcb.append(db_c)
        col += wd
    parts.append(ddt.astype(BF16))
    dzx = jnp.concatenate(parts, axis=1)
    k = dzx.shape[1]
    dw_in = simple_wgrad(f"ssd_dwin_{tag}", hn, dzx, t1=512, tn=896)
    dh_in, dln = rms_bwd_mm(
        f"ssd_dh_{tag}", [(dzx, (None, 896), lambda i, j, kk: (i, kk))], lambda r: r[0][...], k,
        w["w_in"], (d, 896), lambda i, j, kk: (0, kk), _dot_wt, h, ln_g, dh, tk=896)
    grads = dict(w_in=dw_in, w_out=dw_out, conv_w=jnp.concatenate(dcw, axis=1), conv_b=jnp.concatenate(dcb, axis=1),
                 par=dpar, norm=dnorm, ln=dln)
    return dh_in, grads


HP = 2 * LANES
VP = 2 * MLA_V
N_PAIRS = MLA_HEADS // 2
ATT_SCALE = MLA_QK ** -0.5
ROT = MLA_ROPE // 2


def rope_tables(rows):
    inv = 1.0 / (ROPE_THETA ** (jnp.arange(0, MLA_ROPE, 2, dtype=F32) / MLA_ROPE))
    pos = jnp.arange(rows, dtype=F32) - PAD
    ang = pos[:, None] * inv[None, :]
    cos, sin = jnp.cos(ang), jnp.sin(ang)
    one = jnp.ones((rows, MLA_NOPE), F32)
    zero = jnp.zeros((rows, LANES - MLA_QK), F32)
    zn = jnp.zeros((rows, MLA_NOPE), F32)
    zr = jnp.zeros((rows, ROT), F32)
    cosf = jnp.concatenate([one, cos, cos, zero], axis=1)
    sina = jnp.concatenate([zn, -sin, zr, zero], axis=1)
    sinb = jnp.concatenate([zn, zr, sin, zero], axis=1)
    return cosf, sina, sinb


def _qk_norm_rope(x, g, cosf, sina, sinb):
    r = lax.rsqrt(jnp.sum(x * x, axis=-1, keepdims=True) * (1.0 / MLA_QK) + EPS)
    xn = x * r * g
    return xn * cosf + pltpu.roll(xn, LANES - ROT, 1) * sina + pltpu.roll(xn, ROT, 1) * sinb


def _qk_norm_rope_bwd(dout, x, g, cosf, sina, sinb):
    dxn = dout * cosf + pltpu.roll(dout * sina, ROT, 1) + pltpu.roll(dout * sinb, LANES - ROT, 1)
    r = lax.rsqrt(jnp.sum(x * x, axis=-1, keepdims=True) * (1.0 / MLA_QK) + EPS)
    xh = x * r
    t = dxn * g
    dx = r * (t - xh * (jnp.sum(t * xh, axis=-1, keepdims=True) * (1.0 / MLA_QK)))
    return dx, _rows8(dxn * xh)


def _rope_lanes():
    lane = lax.broadcasted_iota(jnp.int32, (1, LANES), 1)
    return jnp.logical_and(lane >= MLA_NOPE, lane < MLA_QK)


QW = MLA_HEADS * LANES
VW = MLA_HEADS * MLA_V


def qk_prep(name, qraw, kvraw, lat, kpe_block, qg, kg, tabs):
    rows = qraw.shape[0]
    tm = _row_tile(rows)

    def body(q_ref, k0_ref, k1_ref, v_ref, pe_ref, qg_ref, kg_ref, c_ref, sa_ref, sb_ref, qo_ref, ko_ref, vo_ref):
        tab = (c_ref[...], sa_ref[...], sb_ref[...])
        pe = pe_ref[...]
        for hd in range(MLA_HEADS):
            sl = slice(hd * LANES, (hd + 1) * LANES)
            qo_ref[:, sl] = _qk_norm_rope(q_ref[:, sl], qg_ref[...], *tab).astype(BF16)
            kr = k0_ref if hd < MLA_HEADS // 2 else k1_ref
            ks = slice((hd % (MLA_HEADS // 2)) * LANES, (hd % (MLA_HEADS // 2) + 1) * LANES)
            ko_ref[:, sl] = _qk_norm_rope(kr[:, ks] + pe, kg_ref[...], *tab).astype(BF16)
        vo_ref[...] = v_ref[...].astype(BF16)

    row = lambda w, b: pl.BlockSpec((tm, w), lambda i: (i, b))
    one = pl.BlockSpec((1, LANES), lambda i: (0, 0))
    return pl.pallas_call(
        body, name=name, grid=(rows // tm,),
        in_specs=[row(QW, 0), row(VW, 0), row(VW, 1), row(VW, 2), row(LANES, kpe_block), one, one,
                  row(LANES, 0), row(LANES, 0), row(LANES, 0)],
        out_specs=[row(QW, 0), row(QW, 0), row(VW, 0)],
        out_shape=[jax.ShapeDtypeStruct((rows, QW), BF16), jax.ShapeDtypeStruct((rows, QW), BF16),
                   jax.ShapeDtypeStruct((rows, VW), BF16)],
        compiler_params=_params(1),
    )(qraw, kvraw, kvraw, kvraw, lat, qg, kg, *tabs)


def qk_prep_bwd(name, dq, dk, dv, qraw, kvraw, lat, kpe_block, qg, kg, tabs):
    rows = qraw.shape[0]
    tm = _row_tile(rows)

    def body(dq_ref, dk_ref, dv_ref, q_ref, k0_ref, k1_ref, pe_ref, qg_ref, kg_ref, c_ref, sa_ref, sb_ref,
             dqo_ref, dkvo_ref, dpe_ref, dqg_ref, dkg_ref):
        i = pl.program_id(0)
        tab = (c_ref[...], sa_ref[...], sb_ref[...])
        pe = pe_ref[...]
        dpe = jnp.zeros((tm, LANES), F32)
        dqg = jnp.zeros((SUBLANES, LANES), F32)
        dkg = jnp.zeros((SUBLANES, LANES), F32)
        for hd in range(MLA_HEADS):
            sl = slice(hd * LANES, (hd + 1) * LANES)
            dx, dg = _qk_norm_rope_bwd(dq_ref[:, sl], q_ref[:, sl], qg_ref[...], *tab)
            dqo_ref[:, sl] = dx.astype(BF16)
            dqg = dqg + dg
            kr = k0_ref if hd < MLA_HEADS // 2 else k1_ref
            ks = slice((hd % (MLA_HEADS // 2)) * LANES, (hd % (MLA_HEADS // 2) + 1) * LANES)
            dx, dg = _qk_norm_rope_bwd(dk_ref[:, sl], kr[:, ks] + pe, kg_ref[...], *tab)
            dkvo_ref[:, sl] = dx.astype(BF16)
            dpe = dpe + dx
            dkg = dkg + dg
        dkvo_ref[:, QW:QW + VW] = dv_ref[...].astype(BF16)
        dpe_ref[...] = jnp.where(_rope_lanes(), dpe, 0.0)
        _acc_out(dqg_ref, dqg, i == 0)
        _acc_out(dkg_ref, dkg, i == 0)

    row = lambda w, b: pl.BlockSpec((tm, w), lambda i: (i, b))
    one = pl.BlockSpec((1, LANES), lambda i: (0, 0))
    acc = pl.BlockSpec((SUBLANES, LANES), lambda i: (0, 0))
    return pl.pallas_call(
        body, name=name, grid=(rows // tm,),
        in_specs=[row(QW, 0), row(QW, 0), row(VW, 0), row(QW, 0), row(VW, 0), row(VW, 1), row(LANES, kpe_block),
                  one, one, row(LANES, 0), row(LANES, 0), row(LANES, 0)],
        out_specs=[row(QW, 0), row(QW + VW, 0), row(LANES, 0), acc, acc],
        out_shape=[jax.ShapeDtypeStruct((rows, QW), BF16), jax.ShapeDtypeStruct((rows, QW + VW), BF16),
                   jax.ShapeDtypeStruct((rows, LANES), F32),
                   jax.ShapeDtypeStruct((SUBLANES, LANES), F32), jax.ShapeDtypeStruct((SUBLANES, LANES), F32)],
        compiler_params=_params(1),
    )(dq, dk, dv, qraw, kvraw, kvraw, lat, qg, kg, *tabs)


def _att_mask(qb, kb):
    qpos = qb * CHUNK + lax.broadcasted_iota(jnp.int32, (CHUNK, CHUNK), 0)
    kpos = kb * CHUNK + lax.broadcasted_iota(jnp.int32, (CHUNK, CHUNK), 1)
    return jnp.logical_and(kpos <= qpos, jnp.logical_or(kpos >= PAD, qpos < PAD))


def attn_fwd(name, q, k, v):
    rows = q.shape[0]
    nb = rows // CHUNK

    def body(q_ref, k_ref, v_ref, o_ref, lse_ref):
        qi = pl.program_id(1)
        for hh in range(2):
            qs = slice(hh * LANES, (hh + 1) * LANES)
            vs = slice(hh * MLA_V, (hh + 1) * MLA_V)
            qv = q_ref[:, qs]

            def step(kb, carry):
                m, l, acc = carry
                r0 = pl.multiple_of(kb * CHUNK, CHUNK)
                s = lax.dot_general(qv, k_ref[pl.ds(r0, CHUNK), qs], (NT, ((), ())),
                                    preferred_element_type=F32) * ATT_SCALE
                s = jnp.where(_att_mask(qi, kb), s, NEG)
                m_new = jnp.maximum(m, jnp.max(s, axis=-1, keepdims=True))
                alpha = jnp.exp(m - m_new)
                p = jnp.exp(s - m_new)
                l = alpha * l + jnp.sum(p, axis=-1, keepdims=True)
                acc = alpha * acc + jnp.dot(p.astype(BF16), v_ref[pl.ds(r0, CHUNK), vs], preferred_element_type=F32)
                return m_new, l, acc

            m, l, acc = lax.fori_loop(
                0, qi + 1, step,
                (jnp.full((CHUNK, 1), NEG, F32), jnp.zeros((CHUNK, 1), F32), jnp.zeros((CHUNK, MLA_V), F32)))
            o_ref[:, vs] = acc / l
            lse_ref[:, vs] = jnp.broadcast_to(m + jnp.log(l), (CHUNK, MLA_V))

    return pl.pallas_call(
        body, name=name, grid=(N_PAIRS, nb),
        in_specs=[pl.BlockSpec((CHUNK, HP), lambda p, i: (i, p)),
                  pl.BlockSpec((rows, HP), lambda p, i: (0, p)),
                  pl.BlockSpec((rows, VP), lambda p, i: (0, p))],
        out_specs=[pl.BlockSpec((CHUNK, VP), lambda p, i: (i, p)),
                   pl.BlockSpec((None, CHUNK, VP), lambda p, i: (p, i, 0))],
        out_shape=[jax.ShapeDtypeStruct((rows, VW), F32), jax.ShapeDtypeStruct((N_PAIRS, rows, VP), F32)],
        compiler_params=_params(2),
    )(q, k, v)


def attn_bwd(name, q, k, v, do, lse, delta):
    rows = q.shape[0]
    nb = rows // CHUNK

    def body(q_ref, k_ref, v_ref, do_ref, lse_ref, dl_ref, dq_ref, dk_ref, dv_ref):
        ki = pl.program_id(1)

        @pl.when(ki == 0)
        def _():
            dq_ref[...] = jnp.zeros((rows, HP), F32)

        for hh in range(2):
            qs = slice(hh * LANES, (hh + 1) * LANES)
            vs = slice(hh * MLA_V, (hh + 1) * MLA_V)
            kv = k_ref[:, qs]
            vv = v_ref[:, vs]

            def step(qb, carry):
                dk, dv = carry
                r0 = pl.multiple_of(qb * CHUNK, CHUNK)
                qv = q_ref[pl.ds(r0, CHUNK), qs]
                dov = do_ref[pl.ds(r0, CHUNK), vs]
                lse = lse_ref[pl.ds(r0, CHUNK), hh * MLA_V:hh * MLA_V + 1]
                dl = dl_ref[pl.ds(r0, CHUNK), hh * MLA_V:hh * MLA_V + 1]
                s = lax.dot_general(qv, kv, (NT, ((), ())), preferred_element_type=F32) * ATT_SCALE
                p = jnp.where(_att_mask(qb, ki), jnp.exp(s - lse), 0.0)
                dp = lax.dot_general(dov, vv, (NT, ((), ())), preferred_element_type=F32)
                ds = (p * (dp - dl) * ATT_SCALE).astype(BF16)
                dv = dv + lax.dot_general(p.astype(BF16), dov, (TN, ((), ())), preferred_element_type=F32)
                dk = dk + lax.dot_general(ds, qv, (TN, ((), ())), preferred_element_type=F32)
                dq_ref[pl.ds(r0, CHUNK), qs] += jnp.dot(ds, kv, preferred_element_type=F32)
                return dk, dv

            dk, dv = lax.fori_loop(ki, nb, step, (jnp.zeros((CHUNK, LANES), F32), jnp.zeros((CHUNK, MLA_V), F32)))
            dk_ref[:, qs] = dk
            dv_ref[:, vs] = dv

    return pl.pallas_call(
        body, name=name, grid=(N_PAIRS, nb),
        in_specs=[pl.BlockSpec((rows, HP), lambda p, i: (0, p)),
                  pl.BlockSpec((CHUNK, HP), lambda p, i: (i, p)),
                  pl.BlockSpec((CHUNK, VP), lambda p, i: (i, p)),
                  pl.BlockSpec((rows, VP), lambda p, i: (0, p)),
                  pl.BlockSpec((None, rows, VP), lambda p, i: (p, 0, 0)),
                  pl.BlockSpec((None, rows, VP), lambda p, i: (p, 0, 0))],
        out_specs=[pl.BlockSpec((rows, HP), lambda p, i: (0, p)),
                   pl.BlockSpec((CHUNK, HP), lambda p, i: (i, p)),
                   pl.BlockSpec((CHUNK, VP), lambda p, i: (i, p))],
        out_shape=[jax.ShapeDtypeStruct((rows, QW), F32), jax.ShapeDtypeStruct((rows, QW), F32),
                   jax.ShapeDtypeStruct((rows, VW), F32)],
        compiler_params=_params(2),
    )(q, k, v, do, lse, delta)


LAT_W = 768
KPE_BLOCK = MLA_Q_RANK // LANES
KV_BLOCK = (MLA_Q_RANK + LANES) // MLA_KV_RANK


def mla_layer_fwd(tag, h, ln_g, w, tabs):
    lat, hn = norm_mm(f"mla_in_{tag}", h, ln_g, w["w_in"], tn=LAT_W)
    qraw, qn = norm_mm(f"mla_q_{tag}", lat, w["q_a"], w["w_q"], tn=512, k_cols=MLA_Q_RANK, col_block=0)
    kvraw, kvn = norm_mm(f"mla_kv_{tag}", lat, w["kv_a"], w["w_kv"], tn=512, k_cols=MLA_KV_RANK, col_block=KV_BLOCK)
    q, k, v = qk_prep(f"mla_prep_{tag}", qraw, kvraw, lat, KPE_BLOCK, w["q_norm"], w["k_norm"], tabs)
    o, lse = attn_fwd(f"mla_attn_{tag}", q, k, v)
    out = res_mm(f"mla_out_{tag}", [(o, (None, VW), lambda i, j, kk: (i, 0))], lambda r: r[0][...].astype(BF16),
                 VW, w["w_out"], h, tn=512)
    return out, (h, hn, lat, qn, kvn, qraw, kvraw, q, k, v, o, lse)


def mla_layer_bwd(tag, dh, saved, ln_g, w, tabs):
    h, hn, lat, qn, kvn, qraw, kvraw, q, k, v, o, lse = saved
    rows, d = h.shape
    tm = _row_tile(rows)
    dw_out = simple_wgrad(f"mla_dwout_{tag}", o, dh, t1=512, tn=d)

    def epi_do(acc, e_refs, o_refs, i, j):
        o_refs[0][...] = acc.astype(BF16)
        prod = acc * e_refs[0][...]
        for hd in range(MLA_HEADS):
            dl = jnp.sum(prod[:, hd * MLA_V:(hd + 1) * MLA_V], axis=-1, keepdims=True)
            o_refs[1][hd // 2, :, (hd % 2) * MLA_V:(hd % 2 + 1) * MLA_V] = jnp.broadcast_to(dl, (tm, MLA_V))

    wo, wo_block, wo_imap = _lw(w["w_out"], (VW, d), lambda i, j, kk: (0, 0))
    do, delta = fused_mm(
        f"mla_do_{tag}", rows=rows, k=d, n=VW, tm=tm, tn=VW,
        a_ops=[(dh, (tm, d), lambda i, j, kk: (i, 0))], pro=lambda a, o_, i: a[0][...].astype(BF16),
        w=wo, w_block=wo_block, w_imap=wo_imap, dot=_dot_wt,
        e_ops=[(o, (tm, VW), lambda i, j, kk: (i, 0))], epi=epi_do,
        outs=[((rows, VW), BF16, (tm, VW), lambda i, j, kk: (i, 0)),
              ((N_PAIRS, rows, VP), F32, (N_PAIRS, tm, VP), lambda i, j, kk: (0, i, 0))])
    dq, dk, dv = attn_bwd(f"mla_dattn_{tag}", q, k, v, do, lse, delta)
    dqraw, dkvraw, dpe, dqg, dkg = qk_prep_bwd(f"mla_dprep_{tag}", dq, dk, dv, qraw, kvraw, lat, KPE_BLOCK,
                                               w["q_norm"], w["k_norm"], tabs)
    dw_q = simple_wgrad(f"mla_dwq_{tag}", qn, dqraw, t1=MLA_Q_RANK, tn=512)
    dqlat, dqa = rms_bwd_mm(
        f"mla_dqlat_{tag}", [(dqraw, (None, QW), lambda i, j, kk: (i, 0))], lambda r: r[0][...], QW,
        w["w_q"], (MLA_Q_RANK, QW), lambda i, j, kk: (0, 0), _dot_wt, lat, w["q_a"], None,
        h_cols=MLA_Q_RANK, h_col_block=0, add_dh=False)
    dw_kv = simple_wgrad(f"mla_dwkv_{tag}", kvn, dkvraw, t1=MLA_KV_RANK, tn=512)
    dkvlat, dkva = rms_bwd_mm(
        f"mla_dkvlat_{tag}", [(dkvraw, (None, QW + VW), lambda i, j, kk: (i, 0))], lambda r: r[0][...], QW + VW,
        w["w_kv"], (MLA_KV_RANK, QW + VW), lambda i, j, kk: (0, 0), _dot_wt, lat, w["kv_a"], None,
        h_cols=MLA_KV_RANK, h_col_block=KV_BLOCK, add_dh=False)
    dlat = jnp.concatenate([dqlat.astype(BF16), dpe.astype(BF16), dkvlat.astype(BF16)], axis=1)
    dw_in = simple_wgrad(f"mla_dwin_{tag}", hn, dlat, t1=512, tn=LAT_W)
    dh_in, dln = rms_bwd_mm(
        f"mla_dh_{tag}", [(dlat, (None, LAT_W), lambda i, j, kk: (i, 0))], lambda r: r[0][...], LAT_W,
        w["w_in"], (d, LAT_W), lambda i, j, kk: (0, 0), _dot_wt, h, ln_g, dh)
    grads = dict(w_in=dw_in, w_q=dw_q, w_kv=dw_kv, w_out=dw_out, q_a=dqa, kv_a=dkva, q_norm=dqg, k_norm=dkg, ln=dln)
    return dh_in, grads


def loss_head(h, target):
    rows, d = h.shape
    nb = rows // CHUNK

    def body(h_ref, t_ref, l_ref, dh_ref):
        i = pl.program_id(0)
        err = jnp.where(i > 0, h_ref[...] - t_ref[...], 0.0)
        dh_ref[...] = err * (1.0 / d)
        _acc_out(l_ref, _rows8(err * err) * (0.5 / d), i == 0)

    return pl.pallas_call(
        body, name="loss_head", grid=(nb,),
        in_specs=[pl.BlockSpec((CHUNK, d), lambda i: (i, 0)),
                  pl.BlockSpec((CHUNK, d), lambda i: (jnp.maximum(i - 1, 0), 0))],
        out_specs=[pl.BlockSpec((SUBLANES, d), lambda i: (0, 0)), pl.BlockSpec((CHUNK, d), lambda i: (i, 0))],
        out_shape=[jax.ShapeDtypeStruct((SUBLANES, d), F32), jax.ShapeDtypeStruct((rows, d), F32)],
        compiler_params=_params(1),
    )(h, target)


def _adamw(w, g, m, v):
    m = ADAM_B1 * m + (1.0 - ADAM_B1) * g
    v = ADAM_B2 * v + (1.0 - ADAM_B2) * jnp.square(g)
    m_hat = m / (1.0 - ADAM_B1 ** ADAM_STEP)
    v_hat = v / (1.0 - ADAM_B2 ** ADAM_STEP)
    delta = -ADAM_LR * (m_hat / (jnp.sqrt(v_hat) + ADAM_EPS) + ADAM_WD * w)
    return delta, m, v


def reduce_adamw(name, recv, w, m, v):
    shape = w.shape
    c = shape[-1]
    r = int(np.prod(shape[:-1]))
    tr = 128 if r % 128 == 0 else r
    recv2 = recv.reshape(N_DEV, r, c)

    def body(r_ref, w_ref, m_ref, v_ref, g_ref, d_ref, mo_ref, vo_ref):
        g = r_ref[0].astype(F32)
        for s in range(1, N_DEV):
            g = g + r_ref[s].astype(F32)
        g_ref[...] = g
        d_ref[...], mo_ref[...], vo_ref[...] = _adamw(w_ref[...], g, m_ref[...], v_ref[...])

    blk = pl.BlockSpec((tr, c), lambda i: (i, 0))
    outs = pl.pallas_call(
        body, name=name, grid=(r // tr,),
        in_specs=[pl.BlockSpec((N_DEV, tr, c), lambda i: (0, i, 0)), blk, blk, blk],
        out_specs=[blk] * 4, out_shape=[jax.ShapeDtypeStruct((r, c), F32)] * 4,
        compiler_params=_params(1),
    )(recv2, w.reshape(r, c), m.reshape(r, c), v.reshape(r, c))
    return [o.reshape(shape) for o in outs]


def small_reduce(recv):
    def body(r_ref, o_ref):
        g = r_ref[0]
        for s in range(1, N_DEV):
            g = g + r_ref[s]
        o_ref[...] = g

    return pl.pallas_call(body, name="small_reduce", out_shape=jax.ShapeDtypeStruct(recv.shape[1:], F32))(recv)


def small_adamw(w, g, m, v):
    def body(w_ref, g_ref, m_ref, v_ref, d_ref, mo_ref, vo_ref):
        d_ref[...], mo_ref[...], vo_ref[...] = _adamw(w_ref[...], g_ref[...], m_ref[...], v_ref[...])

    return pl.pallas_call(body, name="small_adamw", out_shape=[jax.ShapeDtypeStruct(w.shape, F32)] * 3)(w, g, m, v)


def _pack(parts):
    flat, meta, off = [], [], 0
    for p in parts:
        n = int(np.prod(p.shape))
        flat.append(p.reshape(-1).astype(F32))
        meta.append((off, p.shape))
        off += n
    total = -(-off // (SUBLANES * LANES)) * (SUBLANES * LANES)
    flat.append(jnp.zeros((total - off,), F32))
    return jnp.concatenate(flat).reshape(total // LANES, LANES), meta


def _unpack(packed, meta):
    flat = packed.reshape(-1)
    return [flat[off:off + int(np.prod(shape))].reshape(shape) for off, shape in meta]


MESH = pl.DeviceIdType.MESH
N_PEERS = N_DEV - 1


def _me():
    return lax.axis_index("x"), lax.axis_index("y"), lax.axis_index("c")


def _peer(k):
    x, y, c = _me()
    return (1 - x if k & 4 else x, 1 - y if k & 2 else y, 1 - c if k & 1 else c)


def _dev_index(pos):
    return 4 * pos[0] + 2 * pos[1] + pos[2]


def all_gather(name, shards, layer_major):
    n = len(shards)

    def slot(ref, a, idx):
        return ref.at[:, idx] if layer_major[a] else ref.at[idx]

    def body(*refs):
        ins, outs = refs[:n], refs[n:2 * n]
        send_sems, recv_sems, local_sems = refs[2 * n:]
        me = _dev_index(_me())
        local, sends = [], []
        for a in range(n):
            cp = pltpu.make_async_copy(ins[a], slot(outs[a], a, me), local_sems.at[a])
            cp.start()
            local.append(cp)
            for k in range(1, N_DEV):
                cp = pltpu.make_async_remote_copy(
                    src_ref=ins[a], dst_ref=slot(outs[a], a, me), send_sem=send_sems.at[a, k - 1],
                    recv_sem=recv_sems.at[a, k - 1], device_id=_peer(k), device_id_type=MESH)
                cp.start()
                sends.append(cp)
        for a in range(n):
            for k in range(1, N_DEV):
                pltpu.make_async_remote_copy(
                    src_ref=ins[a], dst_ref=slot(outs[a], a, _dev_index(_peer(k))), send_sem=send_sems.at[a, k - 1],
                    recv_sem=recv_sems.at[a, k - 1], device_id=_peer(k), device_id_type=MESH).wait_recv()
        for cp in sends:
            cp.wait_send()
        for cp in local:
            cp.wait()

    def out_shape(a):
        s = shards[a].shape
        return (s[0], N_DEV) + s[1:] if layer_major[a] else (N_DEV,) + s

    any_spec = pl.BlockSpec(memory_space=pl.ANY)
    return pl.pallas_call(
        body, name=name, in_specs=[any_spec] * n, out_specs=[any_spec] * n,
        out_shape=[jax.ShapeDtypeStruct(out_shape(a), shards[a].dtype) for a in range(n)],
        scratch_shapes=[pltpu.SemaphoreType.DMA((n, N_PEERS)), pltpu.SemaphoreType.DMA((n, N_PEERS)),
                        pltpu.SemaphoreType.DMA((n,))],
    )(*shards)


def exchange(name, sends, outs):
    n, no = len(sends), len(outs)

    def part(ref, a, idx):
        rows = sends[a][1]
        if rows == "all":
            return ref
        return ref.at[idx] if rows is None else ref.at[pl.ds(idx * rows, rows)]

    def slot(ref, a, idx):
        layer = sends[a][3]
        return ref.at[idx] if layer is None else ref.at[idx, layer]

    def body(*refs):
        ins, out_refs = refs[:n], refs[n:n + no]
        send_sems, recv_sems, local_sems = refs[n + no:]
        me = _dev_index(_me())
        local, started = [], []
        for a in range(n):
            dst = out_refs[sends[a][2]]
            cp = pltpu.make_async_copy(part(ins[a], a, me), slot(dst, a, me), local_sems.at[a])
            cp.start()
            local.append(cp)
            for k in range(1, N_DEV):
                cp = pltpu.make_async_remote_copy(
                    src_ref=part(ins[a], a, _dev_index(_peer(k))), dst_ref=slot(dst, a, me),
                    send_sem=send_sems.at[a, k - 1], recv_sem=recv_sems.at[a, k - 1],
                    device_id=_peer(k), device_id_type=MESH)
                cp.start()
                started.append(cp)
        for a in range(n):
            dst = out_refs[sends[a][2]]
            for k in range(1, N_DEV):
                pltpu.make_async_remote_copy(
                    src_ref=part(ins[a], a, me), dst_ref=slot(dst, a, _dev_index(_peer(k))),
                    send_sem=send_sems.at[a, k - 1], recv_sem=recv_sems.at[a, k - 1],
                    device_id=_peer(k), device_id_type=MESH).wait_recv()
        for cp in started:
            cp.wait_send()
        for cp in local:
            cp.wait()

    any_spec = pl.BlockSpec(memory_space=pl.ANY)
    return pl.pallas_call(
        body, name=name, in_specs=[any_spec] * n, out_specs=[any_spec] * no,
        out_shape=[jax.ShapeDtypeStruct(s, d) for s, d in outs],
        scratch_shapes=[pltpu.SemaphoreType.DMA((n, N_PEERS)), pltpu.SemaphoreType.DMA((n, N_PEERS)),
                        pltpu.SemaphoreType.DMA((n,))],
    )(*[s[0] for s in sends])


WEIGHTS = ['meta_tokens', 'ln_mix', 'ln_mlp', 'ssd_w_in', 'ssd_conv_w', 'ssd_conv_b', 'ssd_dt_bias', 'ssd_a_log',
           'ssd_d', 'ssd_norm', 'ssd_w_out', 'mla_w_in', 'mla_q_a_norm', 'mla_w_q_b', 'mla_kv_a_norm', 'mla_w_kv_b',
           'mla_q_norm', 'mla_k_norm', 'mla_w_out', 'mlp_w_up', 'mlp_w_down']
BIG = ['ssd_w_in', 'ssd_w_out', 'mla_w_in', 'mla_w_q_b', 'mla_w_kv_b', 'mla_w_out', 'mlp_w_up', 'mlp_w_down']
SMALL_SHARDED = ['meta_tokens', 'ssd_conv_w', 'mla_q_a_norm', 'mla_kv_a_norm']
SMALL_REPL = ['ln_mix', 'ln_mlp', 'ssd_conv_b', 'ssd_dt_bias', 'ssd_a_log', 'ssd_d', 'ssd_norm', 'mla_q_norm',
              'mla_k_norm']
SMALL = SMALL_REPL + SMALL_SHARDED
SSD_IN_PAD = 6272
SSD_IN_TN = 896
MLA_IN = MLA_Q_RANK + MLA_KV_RANK + MLA_ROPE


def _pad_last(v, n):
    return jnp.pad(v, [(0, 0)] * (v.ndim - 1) + [(0, n - v.shape[-1])])


def _step(x, target, W, M, V):
    d = x.shape[-1]
    me = _dev_index(_me())
    depth = W['ln_mix'].shape[0]
    n_ssd, n_mla = W['ssd_w_in'].shape[0], W['mla_w_in'].shape[0]

    small_pack, small_meta = _pack([W[n] for n in SMALL_SHARDED])
    gathered = all_gather("gather_weights", [W[n].astype(BF16) for n in BIG] + [small_pack],
                          [True] * len(BIG) + [False])
    G = dict(zip(BIG, gathered[:len(BIG)]))
    per_dev = [_unpack(gathered[-1][s], small_meta) for s in range(N_DEV)]
    full = {n: jnp.concatenate([per_dev[s][i] for s in range(N_DEV)], axis=-1) for i, n in enumerate(SMALL_SHARDED)}

    ssd_out_all = G['ssd_w_out'].reshape(n_ssd, SSD_D_INNER, d)
    mla_out_all = G['mla_w_out'].reshape(n_mla, VW, d)
    up_all = G['mlp_w_up']
    down_all = G['mlp_w_down'].reshape(depth, -1, d)
    ssd_w, mla_w = [], []
    for j in range(n_ssd):
        wi = G['ssd_w_in'][j].transpose(1, 0, 2).reshape(d, -1)
        par = jnp.concatenate([_pad_last(W[n][j][None], LANES) for n in ('ssd_dt_bias', 'ssd_a_log', 'ssd_d')]
                              + [jnp.zeros((SUBLANES - 3, LANES), F32)])
        ssd_w.append(dict(w_in=_pad_last(wi, SSD_IN_PAD), conv_w=full['ssd_conv_w'][j], conv_b=W['ssd_conv_b'][j][None],
                          par=par, norm=W['ssd_norm'][j][None], w_out=(ssd_out_all, j)))
    for j in range(n_mla):
        wi = G['mla_w_in'][j].reshape(d, MLA_IN)
        kpe = jnp.pad(wi[:, MLA_Q_RANK + MLA_KV_RANK:], ((0, 0), (MLA_NOPE, LANES - MLA_QK)))
        wq = G['mla_w_q_b'][j].transpose(1, 0, 2).reshape(MLA_Q_RANK, MLA_HEADS, MLA_QK)
        wkv = G['mla_w_kv_b'][j].transpose(1, 0, 2).reshape(MLA_KV_RANK, MLA_HEADS, MLA_NOPE + MLA_V)
        mla_w.append(dict(
            w_in=jnp.concatenate([wi[:, :MLA_Q_RANK], kpe, wi[:, MLA_Q_RANK:MLA_Q_RANK + MLA_KV_RANK]], axis=1),
            w_q=_pad_last(wq, LANES).reshape(MLA_Q_RANK, QW),
            w_kv=jnp.concatenate([_pad_last(wkv[..., :MLA_NOPE], LANES).reshape(MLA_KV_RANK, QW),
                                  wkv[..., MLA_NOPE:].reshape(MLA_KV_RANK, VW)], axis=1),
            w_out=(mla_out_all, j), q_a=full['mla_q_a_norm'][j][None], kv_a=full['mla_kv_a_norm'][j][None],
            q_norm=_pad_last(W['mla_q_norm'][j][None], LANES), k_norm=_pad_last(W['mla_k_norm'][j][None], LANES)))

    h = jnp.concatenate([jnp.zeros((PAD, d), F32), full['meta_tokens'], x], axis=0)
    rows = h.shape[0]
    tabs = rope_tables(rows)
    saved = []
    for i in range(depth):
        j = i // 2
        if i % 2 == 0:
            h, s_mix = ssd_layer_fwd(f"{i}", h, W['ln_mix'][i][None], ssd_w[j])
        else:
            h, s_mix = mla_layer_fwd(f"{i}", h, W['ln_mix'][i][None], mla_w[j], tabs)
        h, s_mlp = mlp_fwd(f"{i}", h, W['ln_mlp'][i][None], (up_all, i), (down_all, i))
        saved.append((s_mix, s_mlp))
    loss_part, dh = loss_head(h, target)
    loss = lax.psum(jnp.sum(loss_part), ("x", "y", "c"))

    sends = []
    small = {n: [None] * W[n].shape[0] for n in SMALL if n != 'meta_tokens'}
    for i in reversed(range(depth)):
        j = i // 2
        s_mix, s_mlp = saved[i]
        dh, dw_up, dw_down, dg = mlp_bwd(f"{i}", dh, s_mlp, W['ln_mlp'][i][None], (up_all, i), (down_all, i))
        small['ln_mlp'][i] = dg.sum(0)
        sends.append((dw_up, None, 6, i))
        sends.append((dw_down, down_all.shape[1] // N_DEV, 7, i))
        if i % 2 == 0:
            dh, g = ssd_layer_bwd(f"{i}", dh, s_mix, W['ln_mix'][i][None], ssd_w[j])
            n_in = W['ssd_w_in'].shape[-1]
            sends.append((g['w_in'][:, :N_DEV * n_in].reshape(d, N_DEV, n_in).transpose(1, 0, 2), None, 0, j))
            sends.append((g['w_out'], SSD_D_INNER // N_DEV, 1, j))
            small['ssd_conv_w'][j] = g['conv_w'].reshape(SSD_CONV, SUBLANES, -1).sum(1)
            small['ssd_conv_b'][j] = g['conv_b'].sum(0)
            small['ssd_dt_bias'][j] = g['par'][0, :SSD_HEADS]
            small['ssd_a_log'][j] = g['par'][1, :SSD_HEADS]
            small['ssd_d'][j] = g['par'][2, :SSD_HEADS]
            small['ssd_norm'][j] = g['norm'].sum(0)
        else:
            dh, g = mla_layer_bwd(f"{i}", dh, s_mix, W['ln_mix'][i][None], mla_w[j], tabs)
            gi = g['w_in']
            gi = jnp.concatenate([gi[:, :MLA_Q_RANK], gi[:, MLA_Q_RANK + LANES:],
                                  gi[:, MLA_Q_RANK + MLA_NOPE:MLA_Q_RANK + MLA_QK]], axis=1)
            sends.append((gi, d // N_DEV, 2, j))
            gq = g['w_q'].reshape(MLA_Q_RANK, MLA_HEADS, LANES)[..., :MLA_QK]
            sends.append((gq.reshape(MLA_Q_RANK, N_DEV, -1).transpose(1, 0, 2), None, 3, j))
            gkv = jnp.concatenate([g['w_kv'][:, :QW].reshape(MLA_KV_RANK, MLA_HEADS, LANES)[..., :MLA_NOPE],
                                   g['w_kv'][:, QW:].reshape(MLA_KV_RANK, MLA_HEADS, MLA_V)], axis=-1)
            sends.append((gkv.reshape(MLA_KV_RANK, N_DEV, -1).transpose(1, 0, 2), None, 4, j))
            sends.append((g['w_out'], VW // N_DEV, 5, j))
            small['mla_q_a_norm'][j] = g['q_a'].sum(0)
            small['mla_kv_a_norm'][j] = g['kv_a'].sum(0)
            small['mla_q_norm'][j] = g['q_norm'].sum(0)[:MLA_QK]
            small['mla_k_norm'][j] = g['k_norm'].sum(0)[:MLA_QK]
        small['ln_mix'][i] = g['ln'].sum(0)
    grad_x = dh[CHUNK:]
    small_full = {n: jnp.stack(v) for n, v in small.items()}
    small_full['meta_tokens'] = dh[PAD:CHUNK]

    gpack, gmeta = _pack([small_full[n] for n in SMALL])
    sends.append((gpack, "all", len(BIG), None))
    outs = [((N_DEV,) + W[n].shape, F32) for n in BIG] + [((N_DEV,) + gpack.shape, F32)]
    recv = exchange("exchange_grads", sends, outs)
    res = {}
    for a, n in enumerate(BIG):
        res[n] = reduce_adamw(f"adamw_{n}", recv[a], W[n], M[n], V[n])
    gsum = dict(zip(SMALL, _unpack(small_reduce(recv[-1]), gmeta)))
    for n in SMALL_SHARDED:
        wl = W[n].shape[-1]
        gsum[n] = lax.dynamic_slice_in_dim(gsum[n], me * wl, wl, axis=gsum[n].ndim - 1)
    wp, wmeta = _pack([W[n] for n in SMALL])
    gp, _ = _pack([gsum[n] for n in SMALL])
    mp, _ = _pack([M[n] for n in SMALL])
    vp, _ = _pack([V[n] for n in SMALL])
    upd = [_unpack(o, wmeta) for o in small_adamw(wp, gp, mp, vp)]
    for a, n in enumerate(SMALL):
        res[n] = [gsum[n], upd[0][a], upd[1][a], upd[2][a]]
    return (loss, grad_x[None]) + tuple(res[n][q] for q in range(4) for n in WEIGHTS)


def kernel(x, meta_tokens, ln_mix, ln_mlp, ssd_w_in, ssd_conv_w, ssd_conv_b, ssd_dt_bias, ssd_a_log, ssd_d, ssd_norm, ssd_w_out, mla_w_in, mla_q_a_norm, mla_w_q_b, mla_kv_a_norm, mla_w_kv_b, mla_q_norm, mla_k_norm, mla_w_out, mlp_w_up, mlp_w_down, loss_target, m_meta_tokens, m_ln_mix, m_ln_mlp, m_ssd_w_in, m_ssd_conv_w, m_ssd_conv_b, m_ssd_dt_bias, m_ssd_a_log, m_ssd_d, m_ssd_norm, m_ssd_w_out, m_mla_w_in, m_mla_q_a_norm, m_mla_w_q_b, m_mla_kv_a_norm, m_mla_w_kv_b, m_mla_q_norm, m_mla_k_norm, m_mla_w_out, m_mlp_w_up, m_mlp_w_down, v_meta_tokens, v_ln_mix, v_ln_mlp, v_ssd_w_in, v_ssd_conv_w, v_ssd_conv_b, v_ssd_dt_bias, v_ssd_a_log, v_ssd_d, v_ssd_norm, v_ssd_w_out, v_mla_w_in, v_mla_q_a_norm, v_mla_w_q_b, v_mla_kv_a_norm, v_mla_w_kv_b, v_mla_q_norm, v_mla_k_norm, v_mla_w_out, v_mlp_w_up, v_mlp_w_down):
    given = dict(locals())
    W = {n: given[n] for n in WEIGHTS}
    M = {n: given["m_" + n] for n in WEIGHTS}
    V = {n: given["v_" + n] for n in WEIGHTS}
    return _step(x[0], loss_target[0], W, M, V)
```

```python
import functools

import jax
import jax.numpy as jnp
import numpy as np
from jax import lax
from jax.experimental import pallas as pl
from jax.experimental.pallas import tpu as pltpu

F32 = jnp.float32
BF16 = jnp.bfloat16

EPS = 1e-6
N_META = 16
CHUNK = 128
PAD = CHUNK - N_META
SSD_HEAD_DIM = 64
SSD_HEADS = 32
SSD_GROUPS = 8
SSD_HPG = 4
SSD_STATE = 128
SSD_D_INNER = 2048
SSD_CONV = 4
MLA_HEADS = 16
MLA_NOPE = 64
MLA_ROPE = 32
MLA_V = 64
MLA_QK = 96
MLA_Q_RANK = 384
MLA_KV_RANK = 256
ROPE_THETA = 10000.0
LANES = 128
SUBLANES = 8
N_DEV = 8
VMEM_LIMIT = 56 * 1024 * 1024

ADAM_LR = 0.001
ADAM_B1 = 0.9
ADAM_B2 = 0.999
ADAM_EPS = 1e-08
ADAM_WD = 0.01
ADAM_STEP = 10

NEG = -1e30


def _row_tile(rows):
    return 384 if (rows % 384 == 0 and rows > 384) else 128


def _params(n_axes, vmem=VMEM_LIMIT):
    return pltpu.CompilerParams(dimension_semantics=("arbitrary",) * n_axes, vmem_limit_bytes=vmem)


def _dot(a, b, dims):
    return lax.dot_general(a.astype(BF16), b.astype(BF16), (dims, ((), ())), preferred_element_type=F32)


NN = ((1,), (0,))
NT = ((1,), (1,))
TN = ((0,), (0,))


@jax.custom_vjp
def bdot_nn(a, b):
    return _dot(a, b, NN)


@jax.custom_vjp
def bdot_nt(a, b):
    return _dot(a, b, NT)


@jax.custom_vjp
def bdot_tn(a, b):
    return _dot(a, b, TN)


bdot_nn.defvjp(lambda a, b: (_dot(a, b, NN), (a, b)),
               lambda r, g: (_dot(g, r[1], NT), _dot(r[0], g, TN)))
bdot_nt.defvjp(lambda a, b: (_dot(a, b, NT), (a, b)),
               lambda r, g: (_dot(g, r[1], NN), _dot(g, r[0], TN)))
bdot_tn.defvjp(lambda a, b: (_dot(a, b, TN), (a, b)),
               lambda r, g: (_dot(r[1], g, NT), _dot(r[0], g, NN)))


def _rows8(v):
    r, n = v.shape
    return v.reshape(r // SUBLANES, SUBLANES, n).sum(axis=0)


def _row_mask(i, tm):
    return (i * tm + lax.broadcasted_iota(jnp.int32, (tm, 1), 0)) >= PAD


def fused_mm(name, *, rows, k, n, tm, tn, tk=None, a_ops, pro, w, w_block, w_imap, dot, e_ops=(), epi, outs):
    tk = tk or k
    ni, nj, nk = rows // tm, n // tn, k // tk
    assert rows % tm == 0 and n % tn == 0 and k % tk == 0
    assert nk == 1 or nj == 1
    cache = nk == 1 and nj > 1
    na, ne, no = len(a_ops), len(e_ops), len(outs)

    def body(*refs):
        a_refs = refs[:na]
        w_ref = refs[na]
        e_refs = refs[na + 1:na + 1 + ne]
        o_refs = refs[na + 1 + ne:na + 1 + ne + no]
        scr = refs[na + 1 + ne + no:]
        i, j, kk = pl.program_id(0), pl.program_id(1), pl.program_id(2)
        if cache:
            a_scr = scr[0]

            @pl.when(j == 0)
            def _():
                a_scr[...] = pro(a_refs, o_refs, i)

            a = a_scr[...]
        else:
            a = pro(a_refs, o_refs, i)
        part = dot(a, w_ref)
        if nk == 1:
            epi(part, e_refs, o_refs, i, j)
        else:
            acc_ref = scr[0]

            @pl.when(kk == 0)
            def _():
                acc_ref[...] = part

            @pl.when(kk > 0)
            def _():
                acc_ref[...] += part

            @pl.when(kk == nk - 1)
            def _():
                epi(acc_ref[...], e_refs, o_refs, i, j)

    scratch = []
    if cache:
        scratch.append(pltpu.VMEM((tm, k), BF16))
    if nk > 1:
        scratch.append(pltpu.VMEM((tm, tn), F32))
    in_specs = [pl.BlockSpec(b, m) for (_, b, m) in a_ops]
    in_specs.append(pl.BlockSpec(w_block, w_imap))
    in_specs += [pl.BlockSpec(b, m) for (_, b, m) in e_ops]
    return pl.pallas_call(
        body, name=name, grid=(ni, nj, nk),
        in_specs=in_specs,
        out_specs=[pl.BlockSpec(b, m) for (_, _, b, m) in outs],
        out_shape=[jax.ShapeDtypeStruct(s, d) for (s, d, _, _) in outs],
        scratch_shapes=scratch,
        compiler_params=_params(3),
    )(*[a for (a, _, _) in a_ops], w, *[e for (e, _, _) in e_ops])


def _lw(w, block, imap):
    if isinstance(w, tuple):
        arr, layer = w
        return arr, (None,) + block, (lambda i, j, kk: (layer,) + imap(i, j, kk))
    return w, block, imap


def _dot_w(a, w_ref):
    return jnp.dot(a, w_ref[...], preferred_element_type=F32)


def _dot_wt(a, w_ref):
    return lax.dot_general(a, w_ref[...], (NT, ((), ())), preferred_element_type=F32)


def _rms_pro(h, g):
    r = lax.rsqrt(jnp.mean(h * h, axis=-1, keepdims=True) + EPS)
    return h * r * g


def _rms_bwd(dyn, h, g):
    r = lax.rsqrt(jnp.mean(h * h, axis=-1, keepdims=True) + EPS)
    xh = h * r
    t = dyn * g
    dh = r * (t - xh * jnp.mean(t * xh, axis=-1, keepdims=True))
    return dh, _rows8(dyn * xh)


def _acc_out(ref, val, first):
    @pl.when(first)
    def _():
        ref[...] = val

    @pl.when(jnp.logical_not(first))
    def _():
        ref[...] += val


def norm_mm(name, h, g, w, *, tn, k_cols=None, col_block=0, w_stacked=False):
    rows = h.shape[0]
    k = k_cols or h.shape[1]
    wshape = (w[0].shape[1:] if isinstance(w, tuple) else w.shape)
    n = wshape[0] * wshape[2] if w_stacked else wshape[1]
    tm = _row_tile(rows)

    def pro(a_refs, o_refs, i):
        hn = _rms_pro(a_refs[0][...], a_refs[1][...]).astype(BF16)
        o_refs[1][...] = hn
        return hn

    def epi(acc, e_refs, o_refs, i, j):
        o_refs[0][...] = acc

    if w_stacked:
        w_block, w_imap = (None, k, tn), (lambda i, j, kk: (j, 0, 0))
    else:
        w_block, w_imap = (k, tn), (lambda i, j, kk: (0, j))
    w, w_block, w_imap = _lw(w, w_block, w_imap)
    return fused_mm(
        name, rows=rows, k=k, n=n, tm=tm, tn=tn,
        a_ops=[(h, (tm, k), lambda i, j, kk: (i, col_block)), (g, (1, k), lambda i, j, kk: (0, 0))],
        pro=pro, w=w, w_block=w_block, w_imap=w_imap, dot=_dot_w, epi=epi,
        outs=[((rows, n), F32, (tm, tn), lambda i, j, kk: (i, j)),
              ((rows, k), BF16, (tm, k), lambda i, j, kk: (i, 0))])


def res_mm(name, a_ops, pro, k, w, res, *, tn, save_dtype=None):
    rows, n = res.shape
    tm = _row_tile(rows)

    def pro2(a_refs, o_refs, i):
        a = pro(a_refs)
        if save_dtype is not None:
            o_refs[1][...] = a
        return a

    def epi(acc, e_refs, o_refs, i, j):
        o_refs[0][...] = e_refs[0][...] + acc

    outs = [((rows, n), F32, (tm, tn), lambda i, j, kk: (i, j))]
    if save_dtype is not None:
        outs.append(((rows, k), save_dtype, (tm, k), lambda i, j, kk: (i, 0)))
    w, w_block, w_imap = _lw(w, (k, tn), lambda i, j, kk: (0, j))
    out = fused_mm(
        name, rows=rows, k=k, n=n, tm=tm, tn=tn,
        a_ops=[(a, tuple(tm if x is None else x for x in b), m) for (a, b, m) in a_ops],
        pro=pro2, w=w, w_block=w_block, w_imap=w_imap, dot=_dot_w,
        e_ops=[(res, (tm, tn), lambda i, j, kk: (i, j))], epi=epi, outs=outs)
    return out if save_dtype is not None else out[0]


def wgrad_mm(name, a_ops, pro_a, g_ops, pro_g, *, rows, k1, n, t1, tn, out_shape=None, out_block=None, out_imap=None):
    tt = _row_tile(rows)
    n1, n2, nt = k1 // t1, n // tn, rows // tt
    assert k1 % t1 == 0 and n % tn == 0
    na = len(a_ops)

    def body(*refs):
        a_refs = refs[:na]
        g_refs = refs[na:-2]
        o_ref, acc = refs[-2:]
        t = pl.program_id(2)
        a = pro_a(a_refs).astype(BF16)
        g = pro_g(g_refs).astype(BF16)
        _acc_out(acc, lax.dot_general(a, g, (TN, ((), ())), preferred_element_type=F32), t == 0)

        @pl.when(t == nt - 1)
        def _():
            o_ref[...] = acc[...].astype(BF16).reshape(o_ref.shape)

    return pl.pallas_call(
        body, name=name, grid=(n1, n2, nt),
        in_specs=[pl.BlockSpec(b, m) for (_, b, m) in list(a_ops) + list(g_ops)],
        out_specs=pl.BlockSpec(out_block or (t1, tn), out_imap or (lambda a, b, t: (a, b))),
        out_shape=jax.ShapeDtypeStruct(out_shape or (k1, n), BF16),
        scratch_shapes=[pltpu.VMEM((t1, tn), F32)],
        compiler_params=_params(3),
    )(*[a for (a, _, _) in list(a_ops) + list(g_ops)])


def simple_wgrad(name, a, g, *, a_cols=None, a_col_block=0, t1=None, tn=None, **kw):
    rows = a.shape[0]
    k1 = a_cols or a.shape[1]
    n = g.shape[1]
    tt = _row_tile(rows)
    t1 = t1 or min(k1, 512)
    tn = tn or min(n, 1024)
    return wgrad_mm(
        name,
        [(a, (tt, t1), lambda x, y, t: (t, x + a_col_block * (k1 // t1)))], lambda r: r[0][...],
        [(g, (tt, tn), lambda x, y, t: (t, y))], lambda r: r[0][...],
        rows=rows, k1=k1, n=n, t1=t1, tn=tn, **kw)


def rms_bwd_mm(name, dz_ops, pro, k, w, w_block, w_imap, dot, h, g, dh, *, tk=None, h_cols=None, h_col_block=0,
               add_dh=True):
    rows = h.shape[0]
    n = h_cols or h.shape[1]
    tm = _row_tile(rows)
    ni = rows // tm
    w, w_block, w_imap = _lw(w, w_block, w_imap)

    def epi(acc, e_refs, o_refs, i, j):
        d, dg = _rms_bwd(acc, e_refs[0][...], e_refs[1][...])
        if add_dh:
            d = d + e_refs[2][...]
        o_refs[0][...] = jnp.where(_row_mask(i, tm), d, 0.0)
        _acc_out(o_refs[1], dg, i == 0)

    e_ops = [(h, (tm, n), lambda i, j, kk: (i, h_col_block)), (g, (1, n), lambda i, j, kk: (0, 0))]
    if add_dh:
        e_ops.append((dh, (tm, n), lambda i, j, kk: (i, 0)))
    return fused_mm(
        name, rows=rows, k=k, n=n, tm=tm, tn=n, tk=tk,
        a_ops=[(a, tuple(tm if x is None else x for x in b), m) for (a, b, m) in dz_ops],
        pro=lambda a_refs, o_refs, i: pro(a_refs), w=w, w_block=w_block, w_imap=w_imap, dot=dot,
        e_ops=e_ops, epi=epi,
        outs=[((rows, n), F32, (tm, n), lambda i, j, kk: (i, 0)),
              ((SUBLANES, n), F32, (SUBLANES, n), lambda i, j, kk: (0, 0))])


def _relu2(u):
    r = jnp.maximum(u, 0.0)
    return r * r


def mlp_fwd(tag, h, g, w_up_st, w_down):
    d_ff = w_down[0].shape[1]
    u, hn = norm_mm(f"mlp_up_{tag}", h, g, w_up_st, tn=w_up_st[0].shape[3], w_stacked=True)
    out = res_mm(f"mlp_down_{tag}", [(u, (None, d_ff), lambda i, j, kk: (i, 0))],
                 lambda r: _relu2(r[0][...]).astype(BF16), d_ff, w_down, h, tn=512)
    return out, (h, hn, u)


def mlp_bwd(tag, dh, saved, g, w_up_st, w_down):
    h, hn, u = saved
    rows, d = h.shape
    d_ff = w_down[0].shape[1]
    ts = w_up_st[0].shape[3]
    tm = _row_tile(rows)
    wd, wd_block, wd_imap = _lw(w_down, (512, d), lambda i, j, kk: (j, 0))

    def epi_du(acc, e_refs, o_refs, i, j):
        o_refs[0][...] = (acc * (2.0 * jnp.maximum(e_refs[0][...], 0.0))).astype(BF16)

    du, = fused_mm(
        f"mlp_du_{tag}", rows=rows, k=d, n=d_ff, tm=tm, tn=512,
        a_ops=[(dh, (tm, d), lambda i, j, kk: (i, 0))], pro=lambda a, o, i: a[0][...].astype(BF16),
        w=wd, w_block=wd_block, w_imap=wd_imap, dot=_dot_wt,
        e_ops=[(u, (tm, 512), lambda i, j, kk: (i, j))], epi=epi_du,
        outs=[((rows, d_ff), BF16, (tm, 512), lambda i, j, kk: (i, j))])
    tt = tm
    dw_down = wgrad_mm(
        f"mlp_dwdown_{tag}",
        [(u, (tt, 512), lambda a, b, t: (t, a))], lambda r: _relu2(r[0][...]),
        [(dh, (tt, d), lambda a, b, t: (t, 0))], lambda r: r[0][...],
        rows=rows, k1=d_ff, n=d, t1=512, tn=d)
    dw_up = simple_wgrad(f"mlp_dwup_{tag}", hn, du, t1=d, tn=ts, out_shape=(N_DEV, d, ts), out_block=(None, d, ts),
                         out_imap=lambda a, b, t: (b, 0, 0))
    dh_in, dg = rms_bwd_mm(
        f"mlp_dh_{tag}", [(du, (None, ts), lambda i, j, kk: (i, kk))], lambda r: r[0][...], d_ff,
        w_up_st, (None, d, ts), lambda i, j, kk: (kk, 0, 0), _dot_wt, h, g, dh, tk=ts)
    return dh_in, dw_up, dw_down, dg


CONV_HALO = SUBLANES
CONV_TC = 512


def _silu(x):
    return x * jax.nn.sigmoid(x)


def _conv_pre(ext_ref, w, b, tm):
    pre = b
    for k in range(SSD_CONV):
        pre = pre + w[k:k + 1, :] * ext_ref[pl.ds(CONV_HALO - (SSD_CONV - 1) + k, tm), :]
    return pre


def conv_fwd(name, zx, col0, width, conv_w, conv_b):
    rows = zx.shape[0]
    tm = _row_tile(rows)
    cb0 = col0 // CONV_TC
    hb = tm // CONV_HALO

    def body(u_ref, halo_ref, w_ref, b_ref, o_ref, ext):
        i = pl.program_id(1)
        ext[pl.ds(0, CONV_HALO), :] = jnp.where(i > 0, halo_ref[...], 0.0)
        ext[pl.ds(CONV_HALO, tm), :] = u_ref[...]
        pre = _conv_pre(ext, w_ref[...], b_ref[...], tm)
        o_ref[...] = jnp.where(_row_mask(i, tm), _silu(pre), 0.0)

    return pl.pallas_call(
        body, name=name, grid=(width // CONV_TC, rows // tm),
        in_specs=[pl.BlockSpec((tm, CONV_TC), lambda j, i: (i, cb0 + j)),
                  pl.BlockSpec((CONV_HALO, CONV_TC), lambda j, i: (jnp.maximum(i * hb - 1, 0), cb0 + j)),
                  pl.BlockSpec((SSD_CONV, CONV_TC), lambda j, i: (0, j)),
                  pl.BlockSpec((1, CONV_TC), lambda j, i: (0, j))],
        out_specs=pl.BlockSpec((tm, CONV_TC), lambda j, i: (i, j)),
        out_shape=jax.ShapeDtypeStruct((rows, width), F32),
        scratch_shapes=[pltpu.VMEM((tm + CONV_HALO, CONV_TC), F32)],
        compiler_params=_params(2),
    )(zx, zx, conv_w, conv_b)


def conv_bwd(name, dact, zx, col0, conv_w, conv_b):
    rows, width = dact.shape
    tm = _row_tile(rows)
    ni = rows // tm
    cb0 = col0 // CONV_TC
    hb = tm // CONV_HALO

    def body(d_ref, u_ref, halo_ref, w_ref, b_ref, du_ref, dw_ref, db_ref, ext, dext):
        s = pl.program_id(1)
        i = ni - 1 - s
        w = w_ref[...]
        ext[pl.ds(0, CONV_HALO), :] = jnp.where(i > 0, halo_ref[...], 0.0)
        ext[pl.ds(CONV_HALO, tm), :] = u_ref[...]
        pre = _conv_pre(ext, w, b_ref[...], tm)
        sg = jax.nn.sigmoid(pre)
        dpre = jnp.where(_row_mask(i, tm), d_ref[...] * (sg * (1.0 + pre * (1.0 - sg))), 0.0)

        @pl.when(s == 0)
        def _():
            dext[pl.ds(tm, CONV_HALO), :] = jnp.zeros((CONV_HALO, CONV_TC), F32)

        dext[pl.ds(0, tm), :] = dpre
        du = jnp.zeros((tm, CONV_TC), F32)
        for k in range(SSD_CONV):
            du = du + w[k:k + 1, :] * dext[pl.ds(SSD_CONV - 1 - k, tm), :]
        du_ref[...] = du.astype(du_ref.dtype)
        _acc_out(db_ref, _rows8(dpre), s == 0)
        for k in range(SSD_CONV):
            uk = ext[pl.ds(CONV_HALO - (SSD_CONV - 1) + k, tm), :]
            _acc_out(dw_ref.at[pl.ds(k * SUBLANES, SUBLANES), :], _rows8(dpre * uk), s == 0)
        dext[pl.ds(tm, CONV_HALO), :] = dpre[0:CONV_HALO, :]

    return pl.pallas_call(
        body, name=name, grid=(width // CONV_TC, ni),
        in_specs=[pl.BlockSpec((tm, CONV_TC), lambda j, s: (ni - 1 - s, j)),
                  pl.BlockSpec((tm, CONV_TC), lambda j, s: (ni - 1 - s, cb0 + j)),
                  pl.BlockSpec((CONV_HALO, CONV_TC), lambda j, s: (jnp.maximum((ni - 1 - s) * hb - 1, 0), cb0 + j)),
                  pl.BlockSpec((SSD_CONV, CONV_TC), lambda j, s: (0, j)),
                  pl.BlockSpec((1, CONV_TC), lambda j, s: (0, j))],
        out_specs=[pl.BlockSpec((tm, CONV_TC), lambda j, s: (ni - 1 - s, j)),
                   pl.BlockSpec((SSD_CONV * SUBLANES, CONV_TC), lambda j, s: (0, j)),
                   pl.BlockSpec((SUBLANES, CONV_TC), lambda j, s: (0, j))],
        out_shape=[jax.ShapeDtypeStruct((rows, width), BF16),
                   jax.ShapeDtypeStruct((SSD_CONV * SUBLANES, width), F32),
                   jax.ShapeDtypeStruct((SUBLANES, width), F32)],
        scratch_shapes=[pltpu.VMEM((tm + CONV_HALO, CONV_TC), F32), pltpu.VMEM((tm + CONV_HALO, CONV_TC), F32)],
        compiler_params=_params(2),
    )(dact, zx, zx, conv_w, conv_b)


def _ssd_group(xs, bm, cm, dtraw, prev, par, g, c, tri):
    lane = lax.broadcasted_iota(jnp.int32, (1, LANES), 1)
    sub = lax.broadcasted_iota(jnp.int32, (LANES, 1), 0)
    li = lax.broadcasted_iota(jnp.int32, (CHUNK, CHUNK), 0)
    si = lax.broadcasted_iota(jnp.int32, (CHUNK, CHUNK), 1)
    causal = li >= si
    dt = jnp.where(_row_mask(c, CHUNK), jax.nn.softplus(dtraw + par[0:1, :]), 0.0)
    a = -jnp.exp(par[1:2, :])
    acs = jnp.dot(tri, dt * a, precision=lax.Precision.HIGHEST, preferred_element_type=F32)
    acs_t = acs.T
    cb = bdot_nt(cm, bm)
    ys, news = [], []
    for j in range(SSD_HPG):
        head = SSD_HPG * g + j
        oh = (lane == head).astype(F32)
        col = jnp.sum(acs * oh, axis=1, keepdims=True)
        dth = jnp.sum(dt * oh, axis=1, keepdims=True)
        row = jnp.sum(acs_t * (sub == head).astype(F32), axis=0, keepdims=True)
        d_skip = jnp.sum(par[2:3, :] * oh, axis=1, keepdims=True)
        last = col[CHUNK - 1:CHUNK, :]
        seg = jnp.where(causal, col - row, 0.0)
        decay = jnp.where(causal, jnp.exp(seg), 0.0)
        xh = xs[:, j * SSD_HEAD_DIM:(j + 1) * SSD_HEAD_DIM]
        xdt = xh * dth
        y_diag = bdot_nn(cb * decay, xdt)
        st = bdot_tn(xdt * jnp.exp(last - col), bm)
        ph = prev[j * SSD_HEAD_DIM:(j + 1) * SSD_HEAD_DIM, :]
        y_off = bdot_nt(cm, ph) * jnp.exp(col)
        ys.append(y_diag + y_off + xh * d_skip)
        news.append(ph * jnp.exp(last) + st)
    return jnp.concatenate(ys, axis=1), jnp.concatenate(news, axis=0)


def _tri():
    return jnp.asarray(np.tril(np.ones((CHUNK, CHUNK), np.float32)))


XS_W = SSD_HPG * SSD_HEAD_DIM


def ssd_fwd(name, xa, zx, dt_block, par):
    rows = xa.shape[0]
    nc = rows // CHUNK
    b0 = SSD_D_INNER // SSD_STATE

    def body(xs_ref, b_ref, c_ref, dt_ref, par_ref, tri_ref, y_ref, st_ref, state):
        c, g = pl.program_id(0), pl.program_id(1)

        @pl.when(c == 0)
        def _():
            state[g] = jnp.zeros((XS_W, SSD_STATE), F32)

        prev = state[g]
        st_ref[...] = prev
        y, new = _ssd_group(xs_ref[...], b_ref[...], c_ref[...], dt_ref[...], prev, par_ref[...], g, c, tri_ref[...])
        y_ref[...] = y
        state[g] = new

    return pl.pallas_call(
        body, name=name, grid=(nc, SSD_GROUPS),
        in_specs=[pl.BlockSpec((CHUNK, XS_W), lambda c, g: (c, g)),
                  pl.BlockSpec((CHUNK, SSD_STATE), lambda c, g: (c, b0 + g)),
                  pl.BlockSpec((CHUNK, SSD_STATE), lambda c, g: (c, b0 + SSD_GROUPS + g)),
                  pl.BlockSpec((CHUNK, LANES), lambda c, g: (c, dt_block)),
                  pl.BlockSpec((SUBLANES, LANES), lambda c, g: (0, 0)),
                  pl.BlockSpec((CHUNK, CHUNK), lambda c, g: (0, 0))],
        out_specs=[pl.BlockSpec((CHUNK, XS_W), lambda c, g: (c, g)),
                   pl.BlockSpec((None, None, XS_W, SSD_STATE), lambda c, g: (c, g, 0, 0))],
        out_shape=[jax.ShapeDtypeStruct((rows, SSD_D_INNER), F32),
                   jax.ShapeDtypeStruct((nc, SSD_GROUPS, XS_W, SSD_STATE), F32)],
        scratch_shapes=[pltpu.VMEM((SSD_GROUPS, XS_W, SSD_STATE), F32)],
        compiler_params=_params(2),
    )(xa, xa, xa, zx, par, _tri())


def ssd_bwd(name, dy, xa, zx, dt_block, states, par):
    rows = xa.shape[0]
    nc = rows // CHUNK
    b0 = SSD_D_INNER // SSD_STATE

    def body(dy_ref, xs_ref, b_ref, c_ref, dt_ref, st_ref, par_ref, tri_ref,
             dxs_ref, db_ref, dc_ref, ddt_ref, dpar_ref, dstate):
        s, g = pl.program_id(0), pl.program_id(1)
        c = nc - 1 - s

        @pl.when(s == 0)
        def _():
            dstate[g] = jnp.zeros((XS_W, SSD_STATE), F32)

        def f(xs, bm, cm, dtraw, prev, par_v):
            return _ssd_group(xs, bm, cm, dtraw, prev, par_v, g, c, tri_ref[...])

        _, vjp = jax.vjp(f, xs_ref[...], b_ref[...], c_ref[...], dt_ref[...], st_ref[...], par_ref[...])
        dxs, dbm, dcm, ddt, dprev, dpar = vjp((dy_ref[...], dstate[g]))
        dxs_ref[...] = dxs
        db_ref[...] = dbm
        dc_ref[...] = dcm
        dstate[g] = dprev
        _acc_out(ddt_ref, ddt, g == 0)
        _acc_out(dpar_ref, dpar, jnp.logical_and(s == 0, g == 0))

    return pl.pallas_call(
        body, name=name, grid=(nc, SSD_GROUPS),
        in_specs=[pl.BlockSpec((CHUNK, XS_W), lambda s, g: (nc - 1 - s, g)),
                  pl.BlockSpec((CHUNK, XS_W), lambda s, g: (nc - 1 - s, g)),
                  pl.BlockSpec((CHUNK, SSD_STATE), lambda s, g: (nc - 1 - s, b0 + g)),
                  pl.BlockSpec((CHUNK, SSD_STATE), lambda s, g: (nc - 1 - s, b0 + SSD_GROUPS + g)),
                  pl.BlockSpec((CHUNK, LANES), lambda s, g: (nc - 1 - s, dt_block)),
                  pl.BlockSpec((None, None, XS_W, SSD_STATE), lambda s, g: (nc - 1 - s, g, 0, 0)),
                  pl.BlockSpec((SUBLANES, LANES), lambda s, g: (0, 0)),
                  pl.BlockSpec((CHUNK, CHUNK), lambda s, g: (0, 0))],
        out_specs=[pl.BlockSpec((CHUNK, XS_W), lambda s, g: (nc - 1 - s, g)),
                   pl.BlockSpec((CHUNK, SSD_STATE), lambda s, g: (nc - 1 - s, g)),
                   pl.BlockSpec((CHUNK, SSD_STATE), lambda s, g: (nc - 1 - s, g)),
                   pl.BlockSpec((CHUNK, LANES), lambda s, g: (nc - 1 - s, 0)),
                   pl.BlockSpec((SUBLANES, LANES), lambda s, g: (0, 0))],
        out_shape=[jax.ShapeDtypeStruct((rows, SSD_D_INNER), F32),
                   jax.ShapeDtypeStruct((rows, SSD_GROUPS * SSD_STATE), F32),
                   jax.ShapeDtypeStruct((rows, SSD_GROUPS * SSD_STATE), F32),
                   jax.ShapeDtypeStruct((rows, LANES), F32),
                   jax.ShapeDtypeStruct((SUBLANES, LANES), F32)],
        scratch_shapes=[pltpu.VMEM((SSD_GROUPS, XS_W, SSD_STATE), F32)],
        compiler_params=_params(2),
    )(dy, xa, xa, xa, zx, states, par, _tri())


GN_W = SSD_D_INNER // SSD_GROUPS


def _gated_norm(y, z, ng):
    g = y * _silu(z)
    outs = []
    for q in range(SSD_GROUPS):
        gs = g[:, q * GN_W:(q + 1) * GN_W]
        outs.append(gs * lax.rsqrt(jnp.mean(gs * gs, axis=-1, keepdims=True) + EPS))
    return jnp.concatenate(outs, axis=1) * ng


def ssd_layer_fwd(tag, h, ln_g, w):
    zx, hn = norm_mm(f"ssd_in_{tag}", h, ln_g, w["w_in"], tn=896)
    xa = conv_fwd(f"ssd_conv_{tag}", zx, SSD_D_INNER, 2 * SSD_D_INNER, w["conv_w"], w["conv_b"])
    dt_block = 3 * SSD_D_INNER // LANES
    y, states = ssd_fwd(f"ssd_scan_{tag}", xa, zx, dt_block, w["par"])
    out, gn = res_mm(
        f"ssd_out_{tag}",
        [(y, (None, SSD_D_INNER), lambda i, j, kk: (i, 0)), (zx, (None, SSD_D_INNER), lambda i, j, kk: (i, 0)),
         (w["norm"], (1, SSD_D_INNER), lambda i, j, kk: (0, 0))],
        lambda r: _gated_norm(r[0][...], r[1][...], r[2][...]).astype(BF16),
        SSD_D_INNER, w["w_out"], h, tn=512, save_dtype=BF16)
    return out, (h, hn, zx, xa, y, states, gn)


def ssd_layer_bwd(tag, dh, saved, ln_g, w):
    h, hn, zx, xa, y, states, gn = saved
    rows, d = h.shape
    tm = _row_tile(rows)
    dt_block = 3 * SSD_D_INNER // LANES
    dw_out = simple_wgrad(f"ssd_dwout_{tag}", gn, dh, t1=512, tn=d)

    def epi_gate(acc, e_refs, o_refs, i, j):
        _, vjp = jax.vjp(_gated_norm, e_refs[0][...], e_refs[1][...], e_refs[2][...])
        dy, dz, dng = vjp(acc)
        o_refs[0][...] = dy
        o_refs[1][...] = dz.astype(BF16)
        row0 = lax.broadcasted_iota(jnp.int32, (SUBLANES, 1), 0) == 0
        _acc_out(o_refs[2], jnp.where(row0, dng, 0.0), i == 0)

    wo, wo_block, wo_imap = _lw(w["w_out"], (SSD_D_INNER, d), lambda i, j, kk: (0, 0))
    dy, dz, dnorm = fused_mm(
        f"ssd_dgate_{tag}", rows=rows, k=d, n=SSD_D_INNER, tm=tm, tn=SSD_D_INNER,
        a_ops=[(dh, (tm, d), lambda i, j, kk: (i, 0))], pro=lambda a, o, i: a[0][...].astype(BF16),
        w=wo, w_block=wo_block, w_imap=wo_imap, dot=_dot_wt,
        e_ops=[(y, (tm, SSD_D_INNER), lambda i, j, kk: (i, 0)), (zx, (tm, SSD_D_INNER), lambda i, j, kk: (i, 0)),
               (w["norm"], (1, SSD_D_INNER), lambda i, j, kk: (0, 0))],
        epi=epi_gate,
        outs=[((rows, SSD_D_INNER), F32, (tm, SSD_D_INNER), lambda i, j, kk: (i, 0)),
              ((rows, SSD_D_INNER), BF16, (tm, SSD_D_INNER), lambda i, j, kk: (i, 0)),
              ((SUBLANES, SSD_D_INNER), F32, (SUBLANES, SSD_D_INNER), lambda i, j, kk: (0, 0))])
    dxs, dbm, dcm, ddt, dpar = ssd_bwd(f"ssd_dscan_{tag}", dy, xa, zx, dt_block, states, w["par"])
    parts, dcw, dcb = [dz], [], []
    col = SSD_D_INNER
    for nm, dact in (("x", dxs), ("b", dbm), ("c", dcm)):
        wd = dact.shape[1]
        c0 = col - SSD_D_INNER
        du, dw_c, db_c = conv_bwd(f"ssd_dconv{nm}_{tag}", dact, zx, col, w["conv_w"][:, c0:c0 + wd],
                                  w["conv_b"][:, c0:c0 + wd])
        parts.append(du)
        dcw.append(dw_c)
        dcb.append(db_c)
        col += wd
    parts.append(ddt.astype(BF16))
    dzx = jnp.concatenate(parts, axis=1)
    k = dzx.shape[1]
    dw_in = simple_wgrad(f"ssd_dwin_{tag}", hn, dzx, t1=512, tn=896)
    dh_in, dln = rms_bwd_mm(
        f"ssd_dh_{tag}", [(dzx, (None, 896), lambda i, j, kk: (i, kk))], lambda r: r[0][...], k,
        w["w_in"], (d, 896), lambda i, j, kk: (0, kk), _dot_wt, h, ln_g, dh, tk=896)
    grads = dict(w_in=dw_in, w_out=dw_out, conv_w=jnp.concatenate(dcw, axis=1), conv_b=jnp.concatenate(dcb, axis=1),
                 par=dpar, norm=dnorm, ln=dln)
    return dh_in, grads


HP = 2 * LANES
VP = 2 * MLA_V
N_PAIRS = MLA_HEADS // 2
ATT_SCALE = MLA_QK ** -0.5
ROT = MLA_ROPE // 2


def rope_tables(rows):
    inv = 1.0 / (ROPE_THETA ** (jnp.arange(0, MLA_ROPE, 2, dtype=F32) / MLA_ROPE))
    pos = jnp.arange(rows, dtype=F32) - PAD
    ang = pos[:, None] * inv[None, :]
    cos, sin = jnp.cos(ang), jnp.sin(ang)
    one = jnp.ones((rows, MLA_NOPE), F32)
    zero = jnp.zeros((rows, LANES - MLA_QK), F32)
    zn = jnp.zeros((rows, MLA_NOPE), F32)
    zr = jnp.zeros((rows, ROT), F32)
    cosf = jnp.concatenate([one, cos, cos, zero], axis=1)
    sina = jnp.concatenate([zn, -sin, zr, zero], axis=1)
    sinb = jnp.concatenate([zn, zr, sin, zero], axis=1)
    return cosf, sina, sinb


def _qk_norm_rope(x, g, cosf, sina, sinb):
    r = lax.rsqrt(jnp.sum(x * x, axis=-1, keepdims=True) * (1.0 / MLA_QK) + EPS)
    xn = x * r * g
    return xn * cosf + pltpu.roll(xn, LANES - ROT, 1) * sina + pltpu.roll(xn, ROT, 1) * sinb


def _qk_norm_rope_bwd(dout, x, g, cosf, sina, sinb):
    dxn = dout * cosf + pltpu.roll(dout * sina, ROT, 1) + pltpu.roll(dout * sinb, LANES - ROT, 1)
    r = lax.rsqrt(jnp.sum(x * x, axis=-1, keepdims=True) * (1.0 / MLA_QK) + EPS)
    xh = x * r
    t = dxn * g
    dx = r * (t - xh * (jnp.sum(t * xh, axis=-1, keepdims=True) * (1.0 / MLA_QK)))
    return dx, _rows8(dxn * xh)


def _rope_lanes():
    lane = lax.broadcasted_iota(jnp.int32, (1, LANES), 1)
    return jnp.logical_and(lane >= MLA_NOPE, lane < MLA_QK)


QW = MLA_HEADS * LANES
VW = MLA_HEADS * MLA_V


def qk_prep(name, qraw, kvraw, lat, kpe_block, qg, kg, tabs):
    rows = qraw.shape[0]
    tm = _row_tile(rows)

    def body(q_ref, k0_ref, k1_ref, v_ref, pe_ref, qg_ref, kg_ref, c_ref, sa_ref, sb_ref,
             qo_ref, ko_ref, kt_ref, vo_ref, vt_ref):
        tab = (c_ref[...], sa_ref[...], sb_ref[...])
        pe = pe_ref[...]
        for hd in range(MLA_HEADS):
            sl = slice(hd * LANES, (hd + 1) * LANES)
            qo_ref[:, sl] = _qk_norm_rope(q_ref[:, sl], qg_ref[...], *tab).astype(BF16)
            kr = k0_ref if hd < MLA_HEADS // 2 else k1_ref
            ks = slice((hd % (MLA_HEADS // 2)) * LANES, (hd % (MLA_HEADS // 2) + 1) * LANES)
            kk = _qk_norm_rope(kr[:, ks] + pe, kg_ref[...], *tab)
            ko_ref[:, sl] = kk.astype(BF16)
            kt_ref[sl, :] = kk.T.astype(BF16)
        vo_ref[...] = v_ref[...].astype(BF16)
        for c in range(VW // LANES):
            sl = slice(c * LANES, (c + 1) * LANES)
            vt_ref[sl, :] = v_ref[:, sl].T.astype(BF16)

    row = lambda w, b: pl.BlockSpec((tm, w), lambda i: (i, b))
    col = lambda w: pl.BlockSpec((w, tm), lambda i: (0, i))
    one = pl.BlockSpec((1, LANES), lambda i: (0, 0))
    return pl.pallas_call(
        body, name=name, grid=(rows // tm,),
        in_specs=[row(QW, 0), row(VW, 0), row(VW, 1), row(VW, 2), row(LANES, kpe_block), one, one,
                  row(LANES, 0), row(LANES, 0), row(LANES, 0)],
        out_specs=[row(QW, 0), row(QW, 0), col(QW), row(VW, 0), col(VW)],
        out_shape=[jax.ShapeDtypeStruct((rows, QW), BF16), jax.ShapeDtypeStruct((rows, QW), BF16),
                   jax.ShapeDtypeStruct((QW, rows), BF16), jax.ShapeDtypeStruct((rows, VW), BF16),
                   jax.ShapeDtypeStruct((VW, rows), BF16)],
        compiler_params=_params(1),
    )(qraw, kvraw, kvraw, kvraw, lat, qg, kg, *tabs)


def qk_prep_bwd(name, dq_t, dk, dv, qraw, kvraw, lat, kpe_block, qg, kg, tabs):
    rows = qraw.shape[0]
    tm = _row_tile(rows)

    def body(dq_ref, dk_ref, dv_ref, q_ref, k0_ref, k1_ref, pe_ref, qg_ref, kg_ref, c_ref, sa_ref, sb_ref,
             dqo_ref, dkvo_ref, dpe_ref, dqg_ref, dkg_ref):
        i = pl.program_id(0)
        tab = (c_ref[...], sa_ref[...], sb_ref[...])
        pe = pe_ref[...]
        dpe = jnp.zeros((tm, LANES), F32)
        dqg = jnp.zeros((SUBLANES, LANES), F32)
        dkg = jnp.zeros((SUBLANES, LANES), F32)
        for hd in range(MLA_HEADS):
            sl = slice(hd * LANES, (hd + 1) * LANES)
            dx, dg = _qk_norm_rope_bwd(dq_ref[sl, :].T, q_ref[:, sl], qg_ref[...], *tab)
            dqo_ref[:, sl] = dx.astype(BF16)
            dqg = dqg + dg
            kr = k0_ref if hd < MLA_HEADS // 2 else k1_ref
            ks = slice((hd % (MLA_HEADS // 2)) * LANES, (hd % (MLA_HEADS // 2) + 1) * LANES)
            dx, dg = _qk_norm_rope_bwd(dk_ref[:, sl], kr[:, ks] + pe, kg_ref[...], *tab)
            dkvo_ref[:, sl] = dx.astype(BF16)
            dpe = dpe + dx
            dkg = dkg + dg
        dkvo_ref[:, QW:QW + VW] = dv_ref[...].astype(BF16)
        dpe_ref[...] = jnp.where(_rope_lanes(), dpe, 0.0)
        _acc_out(dqg_ref, dqg, i == 0)
        _acc_out(dkg_ref, dkg, i == 0)

    row = lambda w, b: pl.BlockSpec((tm, w), lambda i: (i, b))
    one = pl.BlockSpec((1, LANES), lambda i: (0, 0))
    acc = pl.BlockSpec((SUBLANES, LANES), lambda i: (0, 0))
    return pl.pallas_call(
        body, name=name, grid=(rows // tm,),
        in_specs=[pl.BlockSpec((QW, tm), lambda i: (0, i)), row(QW, 0), row(VW, 0), row(QW, 0), row(VW, 0), row(VW, 1),
                  row(LANES, kpe_block), one, one, row(LANES, 0), row(LANES, 0), row(LANES, 0)],
        out_specs=[row(QW, 0), row(QW + VW, 0), row(LANES, 0), acc, acc],
        out_shape=[jax.ShapeDtypeStruct((rows, QW), BF16), jax.ShapeDtypeStruct((rows, QW + VW), BF16),
                   jax.ShapeDtypeStruct((rows, LANES), F32),
                   jax.ShapeDtypeStruct((SUBLANES, LANES), F32), jax.ShapeDtypeStruct((SUBLANES, LANES), F32)],
        compiler_params=_params(1),
    )(dq_t, dk, dv, qraw, kvraw, kvraw, lat, qg, kg, *tabs)


def _att_mask_t(qb, kb, bt):
    kpos = kb * bt + lax.broadcasted_iota(jnp.int32, (bt, bt), 0)
    qpos = qb * bt + lax.broadcasted_iota(jnp.int32, (bt, bt), 1)
    return jnp.logical_and(kpos <= qpos, jnp.logical_or(kpos >= PAD, qpos < PAD))


def attn_fwd(name, q, k, vt):
    rows = q.shape[0]
    bt = _row_tile(rows)
    nb = rows // bt
    assert bt >= CHUNK

    def body(q_ref, k_ref, vt_ref, o_ref, lse_ref, m_scr, l_scr, acc_scr):
        qi = pl.program_id(1)
        lse_ref[...] = jnp.zeros((SUBLANES, bt), F32)
        m_scr[...] = jnp.full((2, 1, bt), NEG, F32)
        l_scr[...] = jnp.zeros((2, 1, bt), F32)
        acc_scr[...] = jnp.zeros((2, MLA_V, bt), F32)

        def tile(kb, masked):
            r0 = pl.multiple_of(kb * bt, LANES)
            for hh in range(2):
                qs = slice(hh * LANES, (hh + 1) * LANES)
                vs = slice(hh * MLA_V, (hh + 1) * MLA_V)
                s = lax.dot_general(k_ref[pl.ds(r0, bt), qs], q_ref[:, qs], (NT, ((), ())),
                                    preferred_element_type=F32) * ATT_SCALE
                if masked:
                    s = jnp.where(_att_mask_t(qi, kb, bt), s, NEG)
                m = m_scr[hh]
                m_new = jnp.maximum(m, jnp.max(s, axis=0, keepdims=True))
                alpha = jnp.exp(m - m_new)
                p = jnp.exp(s - m_new)
                l_scr[hh] = alpha * l_scr[hh] + jnp.sum(p, axis=0, keepdims=True)
                acc_scr[hh] = alpha * acc_scr[hh] + jnp.dot(vt_ref[vs, pl.ds(r0, bt)], p.astype(BF16),
                                                            preferred_element_type=F32)
                m_scr[hh] = m_new

        tile(0, True)

        @pl.when(qi > 0)
        def _():
            def mid(kb, carry):
                tile(kb, False)
                return carry

            lax.fori_loop(1, qi, mid, 0)
            tile(qi, True)

        for hh in range(2):
            l = l_scr[hh]
            o_ref[hh * MLA_V:(hh + 1) * MLA_V, :] = acc_scr[hh] / l
            lse_ref[hh:hh + 1, :] = m_scr[hh] + jnp.log(l)

    return pl.pallas_call(
        body, name=name, grid=(N_PAIRS, nb),
        in_specs=[pl.BlockSpec((bt, HP), lambda p, i: (i, p)),
                  pl.BlockSpec((rows, HP), lambda p, i: (0, p)),
                  pl.BlockSpec((VP, rows), lambda p, i: (p, 0))],
        out_specs=[pl.BlockSpec((VP, bt), lambda p, i: (p, i)),
                   pl.BlockSpec((None, SUBLANES, bt), lambda p, i: (p, 0, i))],
        out_shape=[jax.ShapeDtypeStruct((VW, rows), F32), jax.ShapeDtypeStruct((N_PAIRS, SUBLANES, rows), F32)],
        scratch_shapes=[pltpu.VMEM((2, 1, bt), F32), pltpu.VMEM((2, 1, bt), F32), pltpu.VMEM((2, MLA_V, bt), F32)],
        compiler_params=_params(2),
    )(q, k, vt)


def attn_bwd(name, q, k, kt, v, do_t, lse, delta):
    rows = q.shape[0]
    bt = _row_tile(rows)
    nb = rows // bt

    def body(q_ref, k_ref, kt_ref, v_ref, do_ref, lse_ref, dl_ref, dq_ref, dk_ref, dv_ref, dk_scr, dv_scr):
        ki = pl.program_id(1)

        @pl.when(ki == 0)
        def _():
            dq_ref[...] = jnp.zeros((HP, rows), F32)

        dk_scr[...] = jnp.zeros((bt, HP), F32)
        dv_scr[...] = jnp.zeros((bt, VP), F32)

        def tile(qb, masked):
            c0 = pl.multiple_of(qb * bt, LANES)
            for hh in range(2):
                qs = slice(hh * LANES, (hh + 1) * LANES)
                vs = slice(hh * MLA_V, (hh + 1) * MLA_V)
                qv = q_ref[pl.ds(c0, bt), qs]
                dov = do_ref[vs, pl.ds(c0, bt)]
                lse = lse_ref[hh:hh + 1, pl.ds(c0, bt)]
                dl = dl_ref[hh:hh + 1, pl.ds(c0, bt)]
                s = lax.dot_general(k_ref[:, qs], qv, (NT, ((), ())), preferred_element_type=F32) * ATT_SCALE
                p = jnp.exp(s - lse)
                if masked:
                    p = jnp.where(_att_mask_t(qb, ki, bt), p, 0.0)
                dp = jnp.dot(v_ref[:, vs], dov, preferred_element_type=F32)
                ds = (p * (dp - dl) * ATT_SCALE).astype(BF16)
                dv_scr[:, vs] += lax.dot_general(p.astype(BF16), dov, (NT, ((), ())), preferred_element_type=F32)
                dk_scr[:, qs] += jnp.dot(ds, qv, preferred_element_type=F32)
                dq_ref[qs, pl.ds(c0, bt)] += jnp.dot(kt_ref[qs, :], ds, preferred_element_type=F32)

        @pl.when(ki == 0)
        def _():
            def every(qb, carry):
                tile(qb, True)
                return carry

            lax.fori_loop(0, nb, every, 0)

        @pl.when(ki > 0)
        def _():
            tile(ki, True)

            def later(qb, carry):
                tile(qb, False)
                return carry

            lax.fori_loop(ki + 1, nb, later, 0)

        dk_ref[...] = dk_scr[...]
        dv_ref[...] = dv_scr[...]

    stat = pl.BlockSpec((None, SUBLANES, rows), lambda p, i: (p, 0, 0))
    return pl.pallas_call(
        body, name=name, grid=(N_PAIRS, nb),
        in_specs=[pl.BlockSpec((rows, HP), lambda p, i: (0, p)),
                  pl.BlockSpec((bt, HP), lambda p, i: (i, p)),
                  pl.BlockSpec((HP, bt), lambda p, i: (p, i)),
                  pl.BlockSpec((bt, VP), lambda p, i: (i, p)),
                  pl.BlockSpec((VP, rows), lambda p, i: (p, 0)),
                  stat, stat],
        out_specs=[pl.BlockSpec((HP, rows), lambda p, i: (p, 0)),
                   pl.BlockSpec((bt, HP), lambda p, i: (i, p)),
                   pl.BlockSpec((bt, VP), lambda p, i: (i, p))],
        out_shape=[jax.ShapeDtypeStruct((QW, rows), F32), jax.ShapeDtypeStruct((rows, QW), F32),
                   jax.ShapeDtypeStruct((rows, VW), F32)],
        scratch_shapes=[pltpu.VMEM((bt, HP), F32), pltpu.VMEM((bt, VP), F32)],
        compiler_params=_params(2),
    )(q, k, kt, v, do_t, lse, delta)


def _dot_cast_w(a, w_ref):
    return jnp.dot(a, w_ref[...].astype(BF16), preferred_element_type=F32)


def _dot_cast_wt(a, w_ref):
    return lax.dot_general(a, w_ref[...].astype(BF16), (NT, ((), ())), preferred_element_type=F32)


LAT_W = 768
KPE_BLOCK = MLA_Q_RANK // LANES
KV_BLOCK = (MLA_Q_RANK + LANES) // MLA_KV_RANK


def mla_layer_fwd(tag, h, ln_g, w, tabs):
    lat, hn = norm_mm(f"mla_in_{tag}", h, ln_g, w["w_in"], tn=LAT_W)
    qraw, qn = norm_mm(f"mla_q_{tag}", lat, w["q_a"], w["w_q"], tn=512, k_cols=MLA_Q_RANK, col_block=0)
    kvraw, kvn = norm_mm(f"mla_kv_{tag}", lat, w["kv_a"], w["w_kv"], tn=512, k_cols=MLA_KV_RANK, col_block=KV_BLOCK)
    q, k, kt, v, vt = qk_prep(f"mla_prep_{tag}", qraw, kvraw, lat, KPE_BLOCK, w["q_norm"], w["k_norm"], tabs)
    o_t, lse = attn_fwd(f"mla_attn_{tag}", q, k, vt)
    out = res_mm(f"mla_out_{tag}", [(o_t, (VW, None), lambda i, j, kk: (0, i))],
                 lambda r: r[0][...].T.astype(BF16), VW, w["w_out"], h, tn=512)
    return out, (h, hn, lat, qn, kvn, qraw, kvraw, q, k, kt, v, o_t, lse)


def mla_layer_bwd(tag, dh, saved, ln_g, w, tabs):
    h, hn, lat, qn, kvn, qraw, kvraw, q, k, kt, v, o_t, lse = saved
    rows, d = h.shape
    tm = _row_tile(rows)

    def epi_set(acc, e_refs, o_refs, i, j):
        o_refs[0][...] = acc.astype(BF16)

    dw_out, = fused_mm(
        f"mla_dwout_{tag}", rows=VW, k=rows, n=d, tm=512, tn=d, tk=tm,
        a_ops=[(o_t, (512, tm), lambda i, j, kk: (i, kk))], pro=lambda a, o_, i: a[0][...].astype(BF16),
        w=dh, w_block=(tm, d), w_imap=lambda i, j, kk: (kk, 0), dot=_dot_cast_w, epi=epi_set,
        outs=[((VW, d), BF16, (512, d), lambda i, j, kk: (i, 0))])

    def epi_do(acc, e_refs, o_refs, i, j):
        o_refs[0][...] = acc.astype(BF16)
        prod = acc * e_refs[0][...]
        o_refs[1][...] = jnp.zeros((N_PAIRS, SUBLANES, tm), F32)
        for hd in range(MLA_HEADS):
            o_refs[1][hd // 2, hd % 2:hd % 2 + 1, :] = jnp.sum(prod[hd * MLA_V:(hd + 1) * MLA_V, :], axis=0,
                                                               keepdims=True)

    wo, wo_block, wo_imap = _lw(w["w_out"], (VW, d), lambda i, j, kk: (0, 0))
    do_t, delta = fused_mm(
        f"mla_do_{tag}", rows=VW, k=d, n=rows, tm=VW, tn=tm,
        a_ops=[(wo, wo_block, wo_imap)], pro=lambda a, o_, i: a[0][...],
        w=dh, w_block=(tm, d), w_imap=lambda i, j, kk: (j, 0), dot=_dot_cast_wt,
        e_ops=[(o_t, (VW, tm), lambda i, j, kk: (0, j))], epi=epi_do,
        outs=[((VW, rows), BF16, (VW, tm), lambda i, j, kk: (0, j)),
              ((N_PAIRS, SUBLANES, rows), F32, (N_PAIRS, SUBLANES, tm), lambda i, j, kk: (0, 0, j))])
    dq_t, dk, dv = attn_bwd(f"mla_dattn_{tag}", q, k, kt, v, do_t, lse, delta)
    dqraw, dkvraw, dpe, dqg, dkg = qk_prep_bwd(f"mla_dprep_{tag}", dq_t, dk, dv, qraw, kvraw, lat, KPE_BLOCK,
                                               w["q_norm"], w["k_norm"], tabs)
    dw_q = simple_wgrad(f"mla_dwq_{tag}", qn, dqraw, t1=MLA_Q_RANK, tn=512)
    dqlat, dqa = rms_bwd_mm(
        f"mla_dqlat_{tag}", [(dqraw, (None, QW), lambda i, j, kk: (i, 0))], lambda r: r[0][...], QW,
        w["w_q"], (MLA_Q_RANK, QW), lambda i, j, kk: (0, 0), _dot_wt, lat, w["q_a"], None,
        h_cols=MLA_Q_RANK, h_col_block=0, add_dh=False)
    dw_kv = simple_wgrad(f"mla_dwkv_{tag}", kvn, dkvraw, t1=MLA_KV_RANK, tn=512)
    dkvlat, dkva = rms_bwd_mm(
        f"mla_dkvlat_{tag}", [(dkvraw, (None, QW + VW), lambda i, j, kk: (i, 0))], lambda r: r[0][...], QW + VW,
        w["w_kv"], (MLA_KV_RANK, QW + VW), lambda i, j, kk: (0, 0), _dot_wt, lat, w["kv_a"], None,
        h_cols=MLA_KV_RANK, h_col_block=KV_BLOCK, add_dh=False)
    dlat = jnp.concatenate([dqlat.astype(BF16), dpe.astype(BF16), dkvlat.astype(BF16)], axis=1)
    dw_in = simple_wgrad(f"mla_dwin_{tag}", hn, dlat, t1=512, tn=LAT_W)
    dh_in, dln = rms_bwd_mm(
        f"mla_dh_{tag}", [(dlat, (None, LAT_W), lambda i, j, kk: (i, 0))], lambda r: r[0][...], LAT_W,
        w["w_in"], (d, LAT_W), lambda i, j, kk: (0, 0), _dot_wt, h, ln_g, dh)
    grads = dict(w_in=dw_in, w_q=dw_q, w_kv=dw_kv, w_out=dw_out, q_a=dqa, kv_a=dkva, q_norm=dqg, k_norm=dkg, ln=dln)
    return dh_in, grads


def loss_head(h, target):
    rows, d = h.shape
    nb = rows // CHUNK

    def body(h_ref, t_ref, l_ref, dh_ref):
        i = pl.program_id(0)
        err = jnp.where(i > 0, h_ref[...] - t_ref[...], 0.0)
        dh_ref[...] = err * (1.0 / d)
        _acc_out(l_ref, _rows8(err * err) * (0.5 / d), i == 0)

    return pl.pallas_call(
        body, name="loss_head", grid=(nb,),
        in_specs=[pl.BlockSpec((CHUNK, d), lambda i: (i, 0)),
                  pl.BlockSpec((CHUNK, d), lambda i: (jnp.maximum(i - 1, 0), 0))],
        out_specs=[pl.BlockSpec((SUBLANES, d), lambda i: (0, 0)), pl.BlockSpec((CHUNK, d), lambda i: (i, 0))],
        out_shape=[jax.ShapeDtypeStruct((SUBLANES, d), F32), jax.ShapeDtypeStruct((rows, d), F32)],
        compiler_params=_params(1),
    )(h, target)


def _adamw(w, g, m, v):
    m = ADAM_B1 * m + (1.0 - ADAM_B1) * g
    v = ADAM_B2 * v + (1.0 - ADAM_B2) * jnp.square(g)
    m_hat = m / (1.0 - ADAM_B1 ** ADAM_STEP)
    v_hat = v / (1.0 - ADAM_B2 ** ADAM_STEP)
    delta = -ADAM_LR * (m_hat / (jnp.sqrt(v_hat) + ADAM_EPS) + ADAM_WD * w)
    return delta, m, v


def reduce_adamw(name, recv, w, m, v):
    shape = w.shape
    c = shape[-1]
    r = int(np.prod(shape[:-1]))
    tr = 128 if r % 128 == 0 else r
    recv2 = recv.reshape(N_DEV, r, c)

    def body(r_ref, w_ref, m_ref, v_ref, g_ref, d_ref, mo_ref, vo_ref):
        g = r_ref[0].astype(F32)
        for s in range(1, N_DEV):
            g = g + r_ref[s].astype(F32)
        g_ref[...] = g
        d_ref[...], mo_ref[...], vo_ref[...] = _adamw(w_ref[...], g, m_ref[...], v_ref[...])

    blk = pl.BlockSpec((tr, c), lambda i: (i, 0))
    outs = pl.pallas_call(
        body, name=name, grid=(r // tr,),
        in_specs=[pl.BlockSpec((N_DEV, tr, c), lambda i: (0, i, 0)), blk, blk, blk],
        out_specs=[blk] * 4, out_shape=[jax.ShapeDtypeStruct((r, c), F32)] * 4,
        compiler_params=_params(1),
    )(recv2, w.reshape(r, c), m.reshape(r, c), v.reshape(r, c))
    return [o.reshape(shape) for o in outs]


def small_reduce(recv):
    def body(r_ref, o_ref):
        g = r_ref[0]
        for s in range(1, N_DEV):
            g = g + r_ref[s]
        o_ref[...] = g

    return pl.pallas_call(body, name="small_reduce", out_shape=jax.ShapeDtypeStruct(recv.shape[1:], F32))(recv)


def small_adamw(w, g, m, v):
    def body(w_ref, g_ref, m_ref, v_ref, d_ref, mo_ref, vo_ref):
        d_ref[...], mo_ref[...], vo_ref[...] = _adamw(w_ref[...], g_ref[...], m_ref[...], v_ref[...])

    return pl.pallas_call(body, name="small_adamw", out_shape=[jax.ShapeDtypeStruct(w.shape, F32)] * 3)(w, g, m, v)


def _pack(parts):
    flat, meta, off = [], [], 0
    for p in parts:
        n = int(np.prod(p.shape))
        flat.append(p.reshape(-1).astype(F32))
        meta.append((off, p.shape))
        off += n
    total = -(-off // (SUBLANES * LANES)) * (SUBLANES * LANES)
    flat.append(jnp.zeros((total - off,), F32))
    return jnp.concatenate(flat).reshape(total // LANES, LANES), meta


def _unpack(packed, meta):
    flat = packed.reshape(-1)
    return [flat[off:off + int(np.prod(shape))].reshape(shape) for off, shape in meta]


MESH = pl.DeviceIdType.MESH
N_PEERS = N_DEV - 1


def _me():
    return lax.axis_index("x"), lax.axis_index("y"), lax.axis_index("c")


def _peer(k):
    x, y, c = _me()
    return (1 - x if k & 4 else x, 1 - y if k & 2 else y, 1 - c if k & 1 else c)


def _dev_index(pos):
    return 4 * pos[0] + 2 * pos[1] + pos[2]


def all_gather(name, shards, layer_major):
    n = len(shards)

    def slot(ref, a, idx):
        return ref.at[:, idx] if layer_major[a] else ref.at[idx]

    def body(*refs):
        ins, outs = refs[:n], refs[n:2 * n]
        send_sems, recv_sems, local_sems = refs[2 * n:]
        me = _dev_index(_me())
        local, sends = [], []
        for a in range(n):
            cp = pltpu.make_async_copy(ins[a], slot(outs[a], a, me), local_sems.at[a])
            cp.start()
            local.append(cp)
            for k in range(1, N_DEV):
                cp = pltpu.make_async_remote_copy(
                    src_ref=ins[a], dst_ref=slot(outs[a], a, me), send_sem=send_sems.at[a, k - 1],
                    recv_sem=recv_sems.at[a, k - 1], device_id=_peer(k), device_id_type=MESH)
                cp.start()
                sends.append(cp)
        for a in range(n):
            for k in range(1, N_DEV):
                pltpu.make_async_remote_copy(
                    src_ref=ins[a], dst_ref=slot(outs[a], a, _dev_index(_peer(k))), send_sem=send_sems.at[a, k - 1],
                    recv_sem=recv_sems.at[a, k - 1], device_id=_peer(k), device_id_type=MESH).wait_recv()
        for cp in sends:
            cp.wait_send()
        for cp in local:
            cp.wait()

    def out_shape(a):
        s = shards[a].shape
        return (s[0], N_DEV) + s[1:] if layer_major[a] else (N_DEV,) + s

    any_spec = pl.BlockSpec(memory_space=pl.ANY)
    return pl.pallas_call(
        body, name=name, in_specs=[any_spec] * n, out_specs=[any_spec] * n,
        out_shape=[jax.ShapeDtypeStruct(out_shape(a), shards[a].dtype) for a in range(n)],
        scratch_shapes=[pltpu.SemaphoreType.DMA((n, N_PEERS)), pltpu.SemaphoreType.DMA((n, N_PEERS)),
                        pltpu.SemaphoreType.DMA((n,))],
    )(*shards)


def exchange(name, sends, outs):
    n, no = len(sends), len(outs)

    def part(ref, a, idx):
        rows = sends[a][1]
        if rows == "all":
            return ref
        return ref.at[idx] if rows is None else ref.at[pl.ds(idx * rows, rows)]

    def slot(ref, a, idx):
        layer = sends[a][3]
        return ref.at[idx] if layer is None else ref.at[idx, layer]

    def body(*refs):
        ins, out_refs = refs[:n], refs[n:n + no]
        send_sems, recv_sems, local_sems = refs[n + no:]
        me = _dev_index(_me())
        local, started = [], []
        for a in range(n):
            dst = out_refs[sends[a][2]]
            cp = pltpu.make_async_copy(part(ins[a], a, me), slot(dst, a, me), local_sems.at[a])
            cp.start()
            local.append(cp)
            for k in range(1, N_DEV):
                cp = pltpu.make_async_remote_copy(
                    src_ref=part(ins[a], a, _dev_index(_peer(k))), dst_ref=slot(dst, a, me),
                    send_sem=send_sems.at[a, k - 1], recv_sem=recv_sems.at[a, k - 1],
                    device_id=_peer(k), device_id_type=MESH)
                cp.start()
                started.append(cp)
        for a in range(n):
            dst = out_refs[sends[a][2]]
            for k in range(1, N_DEV):
                pltpu.make_async_remote_copy(
                    src_ref=part(ins[a], a, me), dst_ref=slot(dst, a, _dev_index(_peer(k))),
                    send_sem=send_sems.at[a, k - 1], recv_sem=recv_sems.at[a, k - 1],
                    device_id=_peer(k), device_id_type=MESH).wait_recv()
        for cp in started:
            cp.wait_send()
        for cp in local:
            cp.wait()

    any_spec = pl.BlockSpec(memory_space=pl.ANY)
    return pl.pallas_call(
        body, name=name, in_specs=[any_spec] * n, out_specs=[any_spec] * no,
        out_shape=[jax.ShapeDtypeStruct(s, d) for s, d in outs],
        scratch_shapes=[pltpu.SemaphoreType.DMA((n, N_PEERS)), pltpu.SemaphoreType.DMA((n, N_PEERS)),
                        pltpu.SemaphoreType.DMA((n,))],
    )(*[s[0] for s in sends])


WEIGHTS = ['meta_tokens', 'ln_mix', 'ln_mlp', 'ssd_w_in', 'ssd_conv_w', 'ssd_conv_b', 'ssd_dt_bias', 'ssd_a_log',
           'ssd_d', 'ssd_norm', 'ssd_w_out', 'mla_w_in', 'mla_q_a_norm', 'mla_w_q_b', 'mla_kv_a_norm', 'mla_w_kv_b',
           'mla_q_norm', 'mla_k_norm', 'mla_w_out', 'mlp_w_up', 'mlp_w_down']
BIG = ['ssd_w_in', 'ssd_w_out', 'mla_w_in', 'mla_w_q_b', 'mla_w_kv_b', 'mla_w_out', 'mlp_w_up', 'mlp_w_down']
SMALL_SHARDED = ['meta_tokens', 'ssd_conv_w', 'mla_q_a_norm', 'mla_kv_a_norm']
SMALL_REPL = ['ln_mix', 'ln_mlp', 'ssd_conv_b', 'ssd_dt_bias', 'ssd_a_log', 'ssd_d', 'ssd_norm', 'mla_q_norm',
              'mla_k_norm']
SMALL = SMALL_REPL + SMALL_SHARDED
SSD_IN_PAD = 6272
SSD_IN_TN = 896
MLA_IN = MLA_Q_RANK + MLA_KV_RANK + MLA_ROPE


def _pad_last(v, n):
    return jnp.pad(v, [(0, 0)] * (v.ndim - 1) + [(0, n - v.shape[-1])])


def _step(x, target, W, M, V):
    d = x.shape[-1]
    me = _dev_index(_me())
    depth = W['ln_mix'].shape[0]
    n_ssd, n_mla = W['ssd_w_in'].shape[0], W['mla_w_in'].shape[0]

    small_pack, small_meta = _pack([W[n] for n in SMALL_SHARDED])
    gathered = all_gather("gather_weights", [W[n].astype(BF16) for n in BIG] + [small_pack],
                          [True] * len(BIG) + [False])
    G = dict(zip(BIG, gathered[:len(BIG)]))
    per_dev = [_unpack(gathered[-1][s], small_meta) for s in range(N_DEV)]
    full = {n: jnp.concatenate([per_dev[s][i] for s in range(N_DEV)], axis=-1) for i, n in enumerate(SMALL_SHARDED)}

    ssd_out_all = G['ssd_w_out'].reshape(n_ssd, SSD_D_INNER, d)
    mla_out_all = G['mla_w_out'].reshape(n_mla, VW, d)
    up_all = G['mlp_w_up']
    down_all = G['mlp_w_down'].reshape(depth, -1, d)
    ssd_w, mla_w = [], []
    for j in range(n_ssd):
        wi = G['ssd_w_in'][j].transpose(1, 0, 2).reshape(d, -1)
        par = jnp.concatenate([_pad_last(W[n][j][None], LANES) for n in ('ssd_dt_bias', 'ssd_a_log', 'ssd_d')]
                              + [jnp.zeros((SUBLANES - 3, LANES), F32)])
        ssd_w.append(dict(w_in=_pad_last(wi, SSD_IN_PAD), conv_w=full['ssd_conv_w'][j], conv_b=W['ssd_conv_b'][j][None],
                          par=par, norm=W['ssd_norm'][j][None], w_out=(ssd_out_all, j)))
    for j in range(n_mla):
        wi = G['mla_w_in'][j].reshape(d, MLA_IN)
        kpe = jnp.pad(wi[:, MLA_Q_RANK + MLA_KV_RANK:], ((0, 0), (MLA_NOPE, LANES - MLA_QK)))
        wq = G['mla_w_q_b'][j].transpose(1, 0, 2).reshape(MLA_Q_RANK, MLA_HEADS, MLA_QK)
        wkv = G['mla_w_kv_b'][j].transpose(1, 0, 2).reshape(MLA_KV_RANK, MLA_HEADS, MLA_NOPE + MLA_V)
        mla_w.append(dict(
            w_in=jnp.concatenate([wi[:, :MLA_Q_RANK], kpe, wi[:, MLA_Q_RANK:MLA_Q_RANK + MLA_KV_RANK]], axis=1),
            w_q=_pad_last(wq, LANES).reshape(MLA_Q_RANK, QW),
            w_kv=jnp.concatenate([_pad_last(wkv[..., :MLA_NOPE], LANES).reshape(MLA_KV_RANK, QW),
                                  wkv[..., MLA_NOPE:].reshape(MLA_KV_RANK, VW)], axis=1),
            w_out=(mla_out_all, j), q_a=full['mla_q_a_norm'][j][None], kv_a=full['mla_kv_a_norm'][j][None],
            q_norm=_pad_last(W['mla_q_norm'][j][None], LANES), k_norm=_pad_last(W['mla_k_norm'][j][None], LANES)))

    h = jnp.concatenate([jnp.zeros((PAD, d), F32), full['meta_tokens'], x], axis=0)
    rows = h.shape[0]
    tabs = rope_tables(rows)
    saved = []
    for i in range(depth):
        j = i // 2
        if i % 2 == 0:
            h, s_mix = ssd_layer_fwd(f"{i}", h, W['ln_mix'][i][None], ssd_w[j])
        else:
            h, s_mix = mla_layer_fwd(f"{i}", h, W['ln_mix'][i][None], mla_w[j], tabs)
        h, s_mlp = mlp_fwd(f"{i}", h, W['ln_mlp'][i][None], (up_all, i), (down_all, i))
        saved.append((s_mix, s_mlp))
    loss_part, dh = loss_head(h, target)
    loss = lax.psum(jnp.sum(loss_part), ("x", "y", "c"))

    sends = []
    small = {n: [None] * W[n].shape[0] for n in SMALL if n != 'meta_tokens'}
    for i in reversed(range(depth)):
        j = i // 2
        s_mix, s_mlp = saved[i]
        dh, dw_up, dw_down, dg = mlp_bwd(f"{i}", dh, s_mlp, W['ln_mlp'][i][None], (up_all, i), (down_all, i))
        small['ln_mlp'][i] = dg.sum(0)
        sends.append((dw_up, None, 6, i))
        sends.append((dw_down, down_all.shape[1] // N_DEV, 7, i))
        if i % 2 == 0:
            dh, g = ssd_layer_bwd(f"{i}", dh, s_mix, W['ln_mix'][i][None], ssd_w[j])
            n_in = W['ssd_w_in'].shape[-1]
            sends.append((g['w_in'][:, :N_DEV * n_in].reshape(d, N_DEV, n_in).transpose(1, 0, 2), None, 0, j))
            sends.append((g['w_out'], SSD_D_INNER // N_DEV, 1, j))
            small['ssd_conv_w'][j] = g['conv_w'].reshape(SSD_CONV, SUBLANES, -1).sum(1)
            small['ssd_conv_b'][j] = g['conv_b'].sum(0)
            small['ssd_dt_bias'][j] = g['par'][0, :SSD_HEADS]
            small['ssd_a_log'][j] = g['par'][1, :SSD_HEADS]
            small['ssd_d'][j] = g['par'][2, :SSD_HEADS]
            small['ssd_norm'][j] = g['norm'].sum(0)
        else:
            dh, g = mla_layer_bwd(f"{i}", dh, s_mix, W['ln_mix'][i][None], mla_w[j], tabs)
            gi = g['w_in']
            gi = jnp.concatenate([gi[:, :MLA_Q_RANK], gi[:, MLA_Q_RANK + LANES:],
                                  gi[:, MLA_Q_RANK + MLA_NOPE:MLA_Q_RANK + MLA_QK]], axis=1)
            sends.append((gi, d // N_DEV, 2, j))
            gq = g['w_q'].reshape(MLA_Q_RANK, MLA_HEADS, LANES)[..., :MLA_QK]
            sends.append((gq.reshape(MLA_Q_RANK, N_DEV, -1).transpose(1, 0, 2), None, 3, j))
            gkv = jnp.concatenate([g['w_kv'][:, :QW].reshape(MLA_KV_RANK, MLA_HEADS, LANES)[..., :MLA_NOPE],
                                   g['w_kv'][:, QW:].reshape(MLA_KV_RANK, MLA_HEADS, MLA_V)], axis=-1)
            sends.append((gkv.reshape(MLA_KV_RANK, N_DEV, -1).transpose(1, 0, 2), None, 4, j))
            sends.append((g['w_out'], VW // N_DEV, 5, j))
            small['mla_q_a_norm'][j] = g['q_a'].sum(0)
            small['mla_kv_a_norm'][j] = g['kv_a'].sum(0)
            small['mla_q_norm'][j] = g['q_norm'].sum(0)[:MLA_QK]
            small['mla_k_norm'][j] = g['k_norm'].sum(0)[:MLA_QK]
        small['ln_mix'][i] = g['ln'].sum(0)
    grad_x = dh[CHUNK:]
    small_full = {n: jnp.stack(v) for n, v in small.items()}
    small_full['meta_tokens'] = dh[PAD:CHUNK]

    gpack, gmeta = _pack([small_full[n] for n in SMALL])
    sends.append((gpack, "all", len(BIG), None))
    outs = [((N_DEV,) + W[n].shape, BF16) for n in BIG] + [((N_DEV,) + gpack.shape, F32)]
    recv = exchange("exchange_grads", sends, outs)
    res = {}
    for a, n in enumerate(BIG):
        res[n] = reduce_adamw(f"adamw_{n}", recv[a], W[n], M[n], V[n])
    gsum = dict(zip(SMALL, _unpack(small_reduce(recv[-1]), gmeta)))
    for n in SMALL_SHARDED:
        wl = W[n].shape[-1]
        gsum[n] = lax.dynamic_slice_in_dim(gsum[n], me * wl, wl, axis=gsum[n].ndim - 1)
    wp, wmeta = _pack([W[n] for n in SMALL])
    gp, _ = _pack([gsum[n] for n in SMALL])
    mp, _ = _pack([M[n] for n in SMALL])
    vp, _ = _pack([V[n] for n in SMALL])
    upd = [_unpack(o, wmeta) for o in small_adamw(wp, gp, mp, vp)]
    for a, n in enumerate(SMALL):
        res[n] = [gsum[n], upd[0][a], upd[1][a], upd[2][a]]
    return (loss, grad_x[None]) + tuple(res[n][q] for q in range(4) for n in WEIGHTS)


def kernel(x, meta_tokens, ln_mix, ln_mlp, ssd_w_in, ssd_conv_w, ssd_conv_b, ssd_dt_bias, ssd_a_log, ssd_d, ssd_norm, ssd_w_out, mla_w_in, mla_q_a_norm, mla_w_q_b, mla_kv_a_norm, mla_w_kv_b, mla_q_norm, mla_k_norm, mla_w_out, mlp_w_up, mlp_w_down, loss_target, m_meta_tokens, m_ln_mix, m_ln_mlp, m_ssd_w_in, m_ssd_conv_w, m_ssd_conv_b, m_ssd_dt_bias, m_ssd_a_log, m_ssd_d, m_ssd_norm, m_ssd_w_out, m_mla_w_in, m_mla_q_a_norm, m_mla_w_q_b, m_mla_kv_a_norm, m_mla_w_kv_b, m_mla_q_norm, m_mla_k_norm, m_mla_w_out, m_mlp_w_up, m_mlp_w_down, v_meta_tokens, v_ln_mix, v_ln_mlp, v_ssd_w_in, v_ssd_conv_w, v_ssd_conv_b, v_ssd_dt_bias, v_ssd_a_log, v_ssd_d, v_ssd_norm, v_ssd_w_out, v_mla_w_in, v_mla_q_a_norm, v_mla_w_q_b, v_mla_kv_a_norm, v_mla_w_kv_b, v_mla_q_norm, v_mla_k_norm, v_mla_w_out, v_mlp_w_up, v_mlp_w_down):
    given = dict(locals())
    W = {n: given[n] for n in WEIGHTS}
    M = {n: given["m_" + n] for n in WEIGHTS}
    V = {n: given["v_" + n] for n in WEIGHTS}
    return _step(x[0], loss_target[0], W, M, V)
```

```python
import functools

import jax
import jax.numpy as jnp
import numpy as np
from jax import lax
from jax.experimental import pallas as pl
from jax.experimental.pallas import tpu as pltpu

F32 = jnp.float32
BF16 = jnp.bfloat16

EPS = 1e-6
N_META = 16
CHUNK = 128
PAD = CHUNK - N_META
SSD_HEAD_DIM = 64
SSD_HEADS = 32
SSD_GROUPS = 8
SSD_HPG = 4
SSD_STATE = 128
SSD_D_INNER = 2048
SSD_CONV = 4
MLA_HEADS = 16
MLA_NOPE = 64
MLA_ROPE = 32
MLA_V = 64
MLA_QK = 96
MLA_Q_RANK = 384
MLA_KV_RANK = 256
ROPE_THETA = 10000.0
LANES = 128
SUBLANES = 8
N_DEV = 8
VMEM_LIMIT = 56 * 1024 * 1024

ADAM_LR = 0.001
ADAM_B1 = 0.9
ADAM_B2 = 0.999
ADAM_EPS = 1e-08
ADAM_WD = 0.01
ADAM_STEP = 10

NEG = -1e30


def _row_tile(rows):
    return 384 if (rows % 384 == 0 and rows > 384) else 128


def _params(n_axes, vmem=VMEM_LIMIT):
    return pltpu.CompilerParams(dimension_semantics=("arbitrary",) * n_axes, vmem_limit_bytes=vmem)


def _dot(a, b, dims):
    return lax.dot_general(a.astype(BF16), b.astype(BF16), (dims, ((), ())), preferred_element_type=F32)


NN = ((1,), (0,))
NT = ((1,), (1,))
TN = ((0,), (0,))


@jax.custom_vjp
def bdot_nn(a, b):
    return _dot(a, b, NN)


@jax.custom_vjp
def bdot_nt(a, b):
    return _dot(a, b, NT)


@jax.custom_vjp
def bdot_tn(a, b):
    return _dot(a, b, TN)


bdot_nn.defvjp(lambda a, b: (_dot(a, b, NN), (a, b)),
               lambda r, g: (_dot(g, r[1], NT), _dot(r[0], g, TN)))
bdot_nt.defvjp(lambda a, b: (_dot(a, b, NT), (a, b)),
               lambda r, g: (_dot(g, r[1], NN), _dot(g, r[0], TN)))
bdot_tn.defvjp(lambda a, b: (_dot(a, b, TN), (a, b)),
               lambda r, g: (_dot(r[1], g, NT), _dot(r[0], g, NN)))


def _rows8(v):
    r, n = v.shape
    return v.reshape(r // SUBLANES, SUBLANES, n).sum(axis=0)


def _row_mask(i, tm):
    return (i * tm + lax.broadcasted_iota(jnp.int32, (tm, 1), 0)) >= PAD


def fused_mm(name, *, rows, k, n, tm, tn, tk=None, a_ops, pro, w, w_block, w_imap, dot, e_ops=(), epi, outs):
    tk = tk or k
    ni, nj, nk = rows // tm, n // tn, k // tk
    assert rows % tm == 0 and n % tn == 0 and k % tk == 0
    assert nk == 1 or nj == 1
    cache = nk == 1 and nj > 1
    na, ne, no = len(a_ops), len(e_ops), len(outs)

    def body(*refs):
        a_refs = refs[:na]
        w_ref = refs[na]
        e_refs = refs[na + 1:na + 1 + ne]
        o_refs = refs[na + 1 + ne:na + 1 + ne + no]
        scr = refs[na + 1 + ne + no:]
        i, j, kk = pl.program_id(0), pl.program_id(1), pl.program_id(2)
        if cache:
            a_scr = scr[0]

            @pl.when(j == 0)
            def _():
                a_scr[...] = pro(a_refs, o_refs, i)

            a = a_scr[...]
        else:
            a = pro(a_refs, o_refs, i)
        part = dot(a, w_ref)
        if nk == 1:
            epi(part, e_refs, o_refs, i, j)
        else:
            acc_ref = scr[0]

            @pl.when(kk == 0)
            def _():
                acc_ref[...] = part

            @pl.when(kk > 0)
            def _():
                acc_ref[...] += part

            @pl.when(kk == nk - 1)
            def _():
                epi(acc_ref[...], e_refs, o_refs, i, j)

    scratch = []
    if cache:
        scratch.append(pltpu.VMEM((tm, k), BF16))
    if nk > 1:
        scratch.append(pltpu.VMEM((tm, tn), F32))
    in_specs = [pl.BlockSpec(b, m) for (_, b, m) in a_ops]
    in_specs.append(pl.BlockSpec(w_block, w_imap))
    in_specs += [pl.BlockSpec(b, m) for (_, b, m) in e_ops]
    return pl.pallas_call(
        body, name=name, grid=(ni, nj, nk),
        in_specs=in_specs,
        out_specs=[pl.BlockSpec(b, m) for (_, _, b, m) in outs],
        out_shape=[jax.ShapeDtypeStruct(s, d) for (s, d, _, _) in outs],
        scratch_shapes=scratch,
        compiler_params=_params(3),
    )(*[a for (a, _, _) in a_ops], w, *[e for (e, _, _) in e_ops])


def _lw(w, block, imap):
    if isinstance(w, tuple):
        arr, layer = w
        return arr, (None,) + block, (lambda i, j, kk: (layer,) + imap(i, j, kk))
    return w, block, imap


def _dot_w(a, w_ref):
    return jnp.dot(a, w_ref[...], preferred_element_type=F32)


def _dot_wt(a, w_ref):
    return lax.dot_general(a, w_ref[...], (NT, ((), ())), preferred_element_type=F32)


def _rms_pro(h, g):
    r = lax.rsqrt(jnp.mean(h * h, axis=-1, keepdims=True) + EPS)
    return h * r * g


def _rms_bwd(dyn, h, g):
    r = lax.rsqrt(jnp.mean(h * h, axis=-1, keepdims=True) + EPS)
    xh = h * r
    t = dyn * g
    dh = r * (t - xh * jnp.mean(t * xh, axis=-1, keepdims=True))
    return dh, _rows8(dyn * xh)


def _acc_out(ref, val, first):
    @pl.when(first)
    def _():
        ref[...] = val

    @pl.when(jnp.logical_not(first))
    def _():
        ref[...] += val


def norm_mm(name, h, g, w, *, tn, k_cols=None, col_block=0, w_stacked=False):
    rows = h.shape[0]
    k = k_cols or h.shape[1]
    wshape = (w[0].shape[1:] if isinstance(w, tuple) else w.shape)
    n = wshape[0] * wshape[2] if w_stacked else wshape[1]
    tm = _row_tile(rows)

    def pro(a_refs, o_refs, i):
        hn = _rms_pro(a_refs[0][...], a_refs[1][...]).astype(BF16)
        o_refs[1][...] = hn
        return hn

    def epi(acc, e_refs, o_refs, i, j):
        o_refs[0][...] = acc

    if w_stacked:
        w_block, w_imap = (None, k, tn), (lambda i, j, kk: (j, 0, 0))
    else:
        w_block, w_imap = (k, tn), (lambda i, j, kk: (0, j))
    w, w_block, w_imap = _lw(w, w_block, w_imap)
    return fused_mm(
        name, rows=rows, k=k, n=n, tm=tm, tn=tn,
        a_ops=[(h, (tm, k), lambda i, j, kk: (i, col_block)), (g, (1, k), lambda i, j, kk: (0, 0))],
        pro=pro, w=w, w_block=w_block, w_imap=w_imap, dot=_dot_w, epi=epi,
        outs=[((rows, n), F32, (tm, tn), lambda i, j, kk: (i, j)),
              ((rows, k), BF16, (tm, k), lambda i, j, kk: (i, 0))])


def res_mm(name, a_ops, pro, k, w, res, *, tn, save_dtype=None):
    rows, n = res.shape
    tm = _row_tile(rows)

    def pro2(a_refs, o_refs, i):
        a = pro(a_refs)
        if save_dtype is not None:
            o_refs[1][...] = a
        return a

    def epi(acc, e_refs, o_refs, i, j):
        o_refs[0][...] = e_refs[0][...] + acc

    outs = [((rows, n), F32, (tm, tn), lambda i, j, kk: (i, j))]
    if save_dtype is not None:
        outs.append(((rows, k), save_dtype, (tm, k), lambda i, j, kk: (i, 0)))
    w, w_block, w_imap = _lw(w, (k, tn), lambda i, j, kk: (0, j))
    out = fused_mm(
        name, rows=rows, k=k, n=n, tm=tm, tn=tn,
        a_ops=[(a, tuple(tm if x is None else x for x in b), m) for (a, b, m) in a_ops],
        pro=pro2, w=w, w_block=w_block, w_imap=w_imap, dot=_dot_w,
        e_ops=[(res, (tm, tn), lambda i, j, kk: (i, j))], epi=epi, outs=outs)
    return out if save_dtype is not None else out[0]


def wgrad_mm(name, a_ops, pro_a, g_ops, pro_g, *, rows, k1, n, t1, tn, out_shape=None, out_block=None, out_imap=None):
    tt = _row_tile(rows)
    n1, n2, nt = k1 // t1, n // tn, rows // tt
    assert k1 % t1 == 0 and n % tn == 0
    na = len(a_ops)

    def body(*refs):
        a_refs = refs[:na]
        g_refs = refs[na:-2]
        o_ref, acc = refs[-2:]
        t = pl.program_id(2)
        a = pro_a(a_refs).astype(BF16)
        g = pro_g(g_refs).astype(BF16)
        _acc_out(acc, lax.dot_general(a, g, (TN, ((), ())), preferred_element_type=F32), t == 0)

        @pl.when(t == nt - 1)
        def _():
            o_ref[...] = acc[...].astype(BF16).reshape(o_ref.shape)

    return pl.pallas_call(
        body, name=name, grid=(n1, n2, nt),
        in_specs=[pl.BlockSpec(b, m) for (_, b, m) in list(a_ops) + list(g_ops)],
        out_specs=pl.BlockSpec(out_block or (t1, tn), out_imap or (lambda a, b, t: (a, b))),
        out_shape=jax.ShapeDtypeStruct(out_shape or (k1, n), BF16),
        scratch_shapes=[pltpu.VMEM((t1, tn), F32)],
        compiler_params=_params(3),
    )(*[a for (a, _, _) in list(a_ops) + list(g_ops)])


def simple_wgrad(name, a, g, *, a_cols=None, a_col_block=0, t1=None, tn=None, **kw):
    rows = a.shape[0]
    k1 = a_cols or a.shape[1]
    n = g.shape[1]
    tt = _row_tile(rows)
    t1 = t1 or min(k1, 512)
    tn = tn or min(n, 1024)
    return wgrad_mm(
        name,
        [(a, (tt, t1), lambda x, y, t: (t, x + a_col_block * (k1 // t1)))], lambda r: r[0][...],
        [(g, (tt, tn), lambda x, y, t: (t, y))], lambda r: r[0][...],
        rows=rows, k1=k1, n=n, t1=t1, tn=tn, **kw)


def rms_bwd_mm(name, dz_ops, pro, k, w, w_block, w_imap, dot, h, g, dh, *, tk=None, h_cols=None, h_col_block=0,
               add_dh=True):
    rows = h.shape[0]
    n = h_cols or h.shape[1]
    tm = _row_tile(rows)
    ni = rows // tm
    w, w_block, w_imap = _lw(w, w_block, w_imap)

    def epi(acc, e_refs, o_refs, i, j):
        d, dg = _rms_bwd(acc, e_refs[0][...], e_refs[1][...])
        if add_dh:
            d = d + e_refs[2][...]
        o_refs[0][...] = jnp.where(_row_mask(i, tm), d, 0.0)
        _acc_out(o_refs[1], dg, i == 0)

    e_ops = [(h, (tm, n), lambda i, j, kk: (i, h_col_block)), (g, (1, n), lambda i, j, kk: (0, 0))]
    if add_dh:
        e_ops.append((dh, (tm, n), lambda i, j, kk: (i, 0)))
    return fused_mm(
        name, rows=rows, k=k, n=n, tm=tm, tn=n, tk=tk,
        a_ops=[(a, tuple(tm if x is None else x for x in b), m) for (a, b, m) in dz_ops],
        pro=lambda a_refs, o_refs, i: pro(a_refs), w=w, w_block=w_block, w_imap=w_imap, dot=dot,
        e_ops=e_ops, epi=epi,
        outs=[((rows, n), F32, (tm, n), lambda i, j, kk: (i, 0)),
              ((SUBLANES, n), F32, (SUBLANES, n), lambda i, j, kk: (0, 0))])


def _relu2(u):
    r = jnp.maximum(u, 0.0)
    return r * r


def mlp_fwd(tag, h, g, w_up_st, w_down):
    d_ff = w_down.shape[0]
    u, hn = norm_mm(f"mlp_up_{tag}", h, g, w_up_st, tn=w_up_st.shape[2], w_stacked=True)
    out = res_mm(f"mlp_down_{tag}", [(u, (None, d_ff), lambda i, j, kk: (i, 0))],
                 lambda r: _relu2(r[0][...]).astype(BF16), d_ff, w_down, h, tn=512)
    return out, (h, hn, u)


def mlp_bwd(tag, dh, saved, g, w_up_st, w_down):
    h, hn, u = saved
    rows, d = h.shape
    d_ff = w_down.shape[0]
    ts = w_up_st.shape[2]
    tm = _row_tile(rows)
    wd, wd_block, wd_imap = _lw(w_down, (512, d), lambda i, j, kk: (j, 0))

    def epi_du(acc, e_refs, o_refs, i, j):
        o_refs[0][...] = (acc * (2.0 * jnp.maximum(e_refs[0][...], 0.0))).astype(BF16)

    du, = fused_mm(
        f"mlp_du_{tag}", rows=rows, k=d, n=d_ff, tm=tm, tn=512,
        a_ops=[(dh, (tm, d), lambda i, j, kk: (i, 0))], pro=lambda a, o, i: a[0][...].astype(BF16),
        w=wd, w_block=wd_block, w_imap=wd_imap, dot=_dot_wt,
        e_ops=[(u, (tm, 512), lambda i, j, kk: (i, j))], epi=epi_du,
        outs=[((rows, d_ff), BF16, (tm, 512), lambda i, j, kk: (i, j))])
    tt = tm
    dw_down = wgrad_mm(
        f"mlp_dwdown_{tag}",
        [(u, (tt, 512), lambda a, b, t: (t, a))], lambda r: _relu2(r[0][...]),
        [(dh, (tt, d), lambda a, b, t: (t, 0))], lambda r: r[0][...],
        rows=rows, k1=d_ff, n=d, t1=512, tn=d)
    dw_up = simple_wgrad(f"mlp_dwup_{tag}", hn, du, t1=d, tn=ts, out_shape=(N_DEV, d, ts), out_block=(None, d, ts),
                         out_imap=lambda a, b, t: (b, 0, 0))
    dh_in, dg = rms_bwd_mm(
        f"mlp_dh_{tag}", [(du, (None, ts), lambda i, j, kk: (i, kk))], lambda r: r[0][...], d_ff,
        w_up_st, (None, d, ts), lambda i, j, kk: (kk, 0, 0), _dot_wt, h, g, dh, tk=ts)
    return dh_in, dw_up, dw_down, dg


CONV_HALO = SUBLANES
CONV_TC = 512


def _silu(x):
    return x * jax.nn.sigmoid(x)


def _conv_pre(ext_ref, w, b, tm):
    pre = b
    for k in range(SSD_CONV):
        pre = pre + w[k:k + 1, :] * ext_ref[pl.ds(CONV_HALO - (SSD_CONV - 1) + k, tm), :]
    return pre


def conv_fwd(name, zx, col0, width, conv_w, conv_b):
    rows = zx.shape[0]
    tm = _row_tile(rows)
    cb0 = col0 // CONV_TC
    hb = tm // CONV_HALO

    def body(u_ref, halo_ref, w_ref, b_ref, o_ref, ext):
        i = pl.program_id(1)
        ext[pl.ds(0, CONV_HALO), :] = jnp.where(i > 0, halo_ref[...], 0.0)
        ext[pl.ds(CONV_HALO, tm), :] = u_ref[...]
        pre = _conv_pre(ext, w_ref[...], b_ref[...], tm)
        o_ref[...] = jnp.where(_row_mask(i, tm), _silu(pre), 0.0)

    return pl.pallas_call(
        body, name=name, grid=(width // CONV_TC, rows // tm),
        in_specs=[pl.BlockSpec((tm, CONV_TC), lambda j, i: (i, cb0 + j)),
                  pl.BlockSpec((CONV_HALO, CONV_TC), lambda j, i: (jnp.maximum(i * hb - 1, 0), cb0 + j)),
                  pl.BlockSpec((SSD_CONV, CONV_TC), lambda j, i: (0, j)),
                  pl.BlockSpec((1, CONV_TC), lambda j, i: (0, j))],
        out_specs=pl.BlockSpec((tm, CONV_TC), lambda j, i: (i, j)),
        out_shape=jax.ShapeDtypeStruct((rows, width), F32),
        scratch_shapes=[pltpu.VMEM((tm + CONV_HALO, CONV_TC), F32)],
        compiler_params=_params(2),
    )(zx, zx, conv_w, conv_b)


def conv_bwd(name, dact, zx, col0, conv_w, conv_b):
    rows, width = dact.shape
    tm = _row_tile(rows)
    ni = rows // tm
    cb0 = col0 // CONV_TC
    hb = tm // CONV_HALO

    def body(d_ref, u_ref, halo_ref, w_ref, b_ref, du_ref, dw_ref, db_ref, ext, dext):
        s = pl.program_id(1)
        i = ni - 1 - s
        w = w_ref[...]
        ext[pl.ds(0, CONV_HALO), :] = jnp.where(i > 0, halo_ref[...], 0.0)
        ext[pl.ds(CONV_HALO, tm), :] = u_ref[...]
        pre = _conv_pre(ext, w, b_ref[...], tm)
        sg = jax.nn.sigmoid(pre)
        dpre = jnp.where(_row_mask(i, tm), d_ref[...] * (sg * (1.0 + pre * (1.0 - sg))), 0.0)

        @pl.when(s == 0)
        def _():
            dext[pl.ds(tm, CONV_HALO), :] = jnp.zeros((CONV_HALO, CONV_TC), F32)

        dext[pl.ds(0, tm), :] = dpre
        du = jnp.zeros((tm, CONV_TC), F32)
        for k in range(SSD_CONV):
            du = du + w[k:k + 1, :] * dext[pl.ds(SSD_CONV - 1 - k, tm), :]
        du_ref[...] = du.astype(du_ref.dtype)
        _acc_out(db_ref, _rows8(dpre), s == 0)
        for k in range(SSD_CONV):
            uk = ext[pl.ds(CONV_HALO - (SSD_CONV - 1) + k, tm), :]
            _acc_out(dw_ref.at[pl.ds(k * SUBLANES, SUBLANES), :], _rows8(dpre * uk), s == 0)
        dext[pl.ds(tm, CONV_HALO), :] = dpre[0:CONV_HALO, :]

    return pl.pallas_call(
        body, name=name, grid=(width // CONV_TC, ni),
        in_specs=[pl.BlockSpec((tm, CONV_TC), lambda j, s: (ni - 1 - s, j)),
                  pl.BlockSpec((tm, CONV_TC), lambda j, s: (ni - 1 - s, cb0 + j)),
                  pl.BlockSpec((CONV_HALO, CONV_TC), lambda j, s: (jnp.maximum((ni - 1 - s) * hb - 1, 0), cb0 + j)),
                  pl.BlockSpec((SSD_CONV, CONV_TC), lambda j, s: (0, j)),
                  pl.BlockSpec((1, CONV_TC), lambda j, s: (0, j))],
        out_specs=[pl.BlockSpec((tm, CONV_TC), lambda j, s: (ni - 1 - s, j)),
                   pl.BlockSpec((SSD_CONV * SUBLANES, CONV_TC), lambda j, s: (0, j)),
                   pl.BlockSpec((SUBLANES, CONV_TC), lambda j, s: (0, j))],
        out_shape=[jax.ShapeDtypeStruct((rows, width), BF16),
                   jax.ShapeDtypeStruct((SSD_CONV * SUBLANES, width), F32),
                   jax.ShapeDtypeStruct((SUBLANES, width), F32)],
        scratch_shapes=[pltpu.VMEM((tm + CONV_HALO, CONV_TC), F32), pltpu.VMEM((tm + CONV_HALO, CONV_TC), F32)],
        compiler_params=_params(2),
    )(dact, zx, zx, conv_w, conv_b)


def _ssd_group(xs, bm, cm, dtraw, prev, par, g, c, tri):
    lane = lax.broadcasted_iota(jnp.int32, (1, LANES), 1)
    sub = lax.broadcasted_iota(jnp.int32, (LANES, 1), 0)
    li = lax.broadcasted_iota(jnp.int32, (CHUNK, CHUNK), 0)
    si = lax.broadcasted_iota(jnp.int32, (CHUNK, CHUNK), 1)
    causal = li >= si
    dt = jnp.where(_row_mask(c, CHUNK), jax.nn.softplus(dtraw + par[0:1, :]), 0.0)
    a = -jnp.exp(par[1:2, :])
    acs = jnp.dot(tri, dt * a, precision=lax.Precision.HIGHEST, preferred_element_type=F32)
    acs_t = acs.T
    cb = bdot_nt(cm, bm)
    ys, news = [], []
    for j in range(SSD_HPG):
        head = SSD_HPG * g + j
        oh = (lane == head).astype(F32)
        col = jnp.sum(acs * oh, axis=1, keepdims=True)
        dth = jnp.sum(dt * oh, axis=1, keepdims=True)
        row = jnp.sum(acs_t * (sub == head).astype(F32), axis=0, keepdims=True)
        d_skip = jnp.sum(par[2:3, :] * oh, axis=1, keepdims=True)
        last = col[CHUNK - 1:CHUNK, :]
        seg = jnp.where(causal, col - row, 0.0)
        decay = jnp.where(causal, jnp.exp(seg), 0.0)
        xh = xs[:, j * SSD_HEAD_DIM:(j + 1) * SSD_HEAD_DIM]
        xdt = xh * dth
        y_diag = bdot_nn(cb * decay, xdt)
        st = bdot_tn(xdt * jnp.exp(last - col), bm)
        ph = prev[j * SSD_HEAD_DIM:(j + 1) * SSD_HEAD_DIM, :]
        y_off = bdot_nt(cm, ph) * jnp.exp(col)
        ys.append(y_diag + y_off + xh * d_skip)
        news.append(ph * jnp.exp(last) + st)
    return jnp.concatenate(ys, axis=1), jnp.concatenate(news, axis=0)


def _tri():
    return jnp.asarray(np.tril(np.ones((CHUNK, CHUNK), np.float32)))


XS_W = SSD_HPG * SSD_HEAD_DIM


def ssd_fwd(name, xa, zx, dt_block, par, comm=None):
    rows = xa.shape[0]
    nc = rows // CHUNK
    b0 = SSD_D_INNER // SSD_STATE

    def body(xs_ref, b_ref, c_ref, dt_ref, par_ref, tri_ref, y_ref, st_ref, state):
        c, g = pl.program_id(0), pl.program_id(1)

        @pl.when(c == 0)
        def _():
            state[g] = jnp.zeros((XS_W, SSD_STATE), F32)

        prev = state[g]
        st_ref[...] = prev
        y, new = _ssd_group(xs_ref[...], b_ref[...], c_ref[...], dt_ref[...], prev, par_ref[...], g, c, tri_ref[...])
        y_ref[...] = y
        state[g] = new

    return carrier_call(
        name, body, (nc, SSD_GROUPS),
        [pl.BlockSpec((CHUNK, XS_W), lambda c, g: (c, g)),
         pl.BlockSpec((CHUNK, SSD_STATE), lambda c, g: (c, b0 + g)),
         pl.BlockSpec((CHUNK, SSD_STATE), lambda c, g: (c, b0 + SSD_GROUPS + g)),
         pl.BlockSpec((CHUNK, LANES), lambda c, g: (c, dt_block)),
         pl.BlockSpec((SUBLANES, LANES), lambda c, g: (0, 0)),
         pl.BlockSpec((CHUNK, CHUNK), lambda c, g: (0, 0))],
        [pl.BlockSpec((CHUNK, XS_W), lambda c, g: (c, g)),
         pl.BlockSpec((None, None, XS_W, SSD_STATE), lambda c, g: (c, g, 0, 0))],
        [jax.ShapeDtypeStruct((rows, SSD_D_INNER), F32),
         jax.ShapeDtypeStruct((nc, SSD_GROUPS, XS_W, SSD_STATE), F32)],
        [pltpu.VMEM((SSD_GROUPS, XS_W, SSD_STATE), F32)],
        (xa, xa, xa, zx, par, _tri()), comm)


def carrier_call(name, body, grid, in_specs, out_specs, out_shape, scratch_shapes, args, comm):
    body, in_specs, out_specs, out_shape, scratch_shapes, extra, n_own = with_comm(
        comm, body, grid, in_specs, out_specs, out_shape, scratch_shapes)
    res = pl.pallas_call(
        body, name=name, grid=grid, in_specs=in_specs, out_specs=out_specs, out_shape=out_shape,
        scratch_shapes=scratch_shapes, compiler_params=_params(len(grid)))(*args, *extra)
    return res[:n_own], res[n_own:]


def ssd_bwd(name, dy, xa, zx, dt_block, states, par, comm=None):
    rows = xa.shape[0]
    nc = rows // CHUNK
    b0 = SSD_D_INNER // SSD_STATE

    def body(dy_ref, xs_ref, b_ref, c_ref, dt_ref, st_ref, par_ref, tri_ref,
             dxs_ref, db_ref, dc_ref, ddt_ref, dpar_ref, dstate):
        s, g = pl.program_id(0), pl.program_id(1)
        c = nc - 1 - s

        @pl.when(s == 0)
        def _():
            dstate[g] = jnp.zeros((XS_W, SSD_STATE), F32)

        def f(xs, bm, cm, dtraw, prev, par_v):
            return _ssd_group(xs, bm, cm, dtraw, prev, par_v, g, c, tri_ref[...])

        _, vjp = jax.vjp(f, xs_ref[...], b_ref[...], c_ref[...], dt_ref[...], st_ref[...], par_ref[...])
        dxs, dbm, dcm, ddt, dprev, dpar = vjp((dy_ref[...], dstate[g]))
        dxs_ref[...] = dxs
        db_ref[...] = dbm
        dc_ref[...] = dcm
        dstate[g] = dprev
        _acc_out(ddt_ref, ddt, g == 0)
        _acc_out(dpar_ref, dpar, jnp.logical_and(s == 0, g == 0))

    return carrier_call(
        name, body, (nc, SSD_GROUPS),
        [pl.BlockSpec((CHUNK, XS_W), lambda s, g: (nc - 1 - s, g)),
         pl.BlockSpec((CHUNK, XS_W), lambda s, g: (nc - 1 - s, g)),
         pl.BlockSpec((CHUNK, SSD_STATE), lambda s, g: (nc - 1 - s, b0 + g)),
         pl.BlockSpec((CHUNK, SSD_STATE), lambda s, g: (nc - 1 - s, b0 + SSD_GROUPS + g)),
         pl.BlockSpec((CHUNK, LANES), lambda s, g: (nc - 1 - s, dt_block)),
         pl.BlockSpec((None, None, XS_W, SSD_STATE), lambda s, g: (nc - 1 - s, g, 0, 0)),
         pl.BlockSpec((SUBLANES, LANES), lambda s, g: (0, 0)),
         pl.BlockSpec((CHUNK, CHUNK), lambda s, g: (0, 0))],
        [pl.BlockSpec((CHUNK, XS_W), lambda s, g: (nc - 1 - s, g)),
         pl.BlockSpec((CHUNK, SSD_STATE), lambda s, g: (nc - 1 - s, g)),
         pl.BlockSpec((CHUNK, SSD_STATE), lambda s, g: (nc - 1 - s, g)),
         pl.BlockSpec((CHUNK, LANES), lambda s, g: (nc - 1 - s, 0)),
         pl.BlockSpec((SUBLANES, LANES), lambda s, g: (0, 0))],
        [jax.ShapeDtypeStruct((rows, SSD_D_INNER), F32),
         jax.ShapeDtypeStruct((rows, SSD_GROUPS * SSD_STATE), F32),
         jax.ShapeDtypeStruct((rows, SSD_GROUPS * SSD_STATE), F32),
         jax.ShapeDtypeStruct((rows, LANES), F32),
         jax.ShapeDtypeStruct((SUBLANES, LANES), F32)],
        [pltpu.VMEM((SSD_GROUPS, XS_W, SSD_STATE), F32)],
        (dy, xa, xa, xa, zx, states, par, _tri()), comm)


GN_W = SSD_D_INNER // SSD_GROUPS


def _gated_norm(y, z, ng):
    g = y * _silu(z)
    outs = []
    for q in range(SSD_GROUPS):
        gs = g[:, q * GN_W:(q + 1) * GN_W]
        outs.append(gs * lax.rsqrt(jnp.mean(gs * gs, axis=-1, keepdims=True) + EPS))
    return jnp.concatenate(outs, axis=1) * ng


def ssd_layer_fwd(tag, h, ln_g, w, comm=None):
    zx, hn = norm_mm(f"ssd_in_{tag}", h, ln_g, w["w_in"], tn=896)
    xa = conv_fwd(f"ssd_conv_{tag}", zx, SSD_D_INNER, 2 * SSD_D_INNER, w["conv_w"], w["conv_b"])
    dt_block = 3 * SSD_D_INNER // LANES
    (y, states), cres = ssd_fwd(f"ssd_scan_{tag}", xa, zx, dt_block, w["par"], comm)
    out, gn = res_mm(
        f"ssd_out_{tag}",
        [(y, (None, SSD_D_INNER), lambda i, j, kk: (i, 0)), (zx, (None, SSD_D_INNER), lambda i, j, kk: (i, 0)),
         (w["norm"], (1, SSD_D_INNER), lambda i, j, kk: (0, 0))],
        lambda r: _gated_norm(r[0][...], r[1][...], r[2][...]).astype(BF16),
        SSD_D_INNER, w["w_out"], h, tn=512, save_dtype=BF16)
    return out, (h, hn, zx, xa, y, states, gn), cres


def ssd_layer_bwd(tag, dh, saved, ln_g, w, comm=None):
    h, hn, zx, xa, y, states, gn = saved
    rows, d = h.shape
    tm = _row_tile(rows)
    dt_block = 3 * SSD_D_INNER // LANES
    dw_out = simple_wgrad(f"ssd_dwout_{tag}", gn, dh, t1=512, tn=d)

    def epi_gate(acc, e_refs, o_refs, i, j):
        _, vjp = jax.vjp(_gated_norm, e_refs[0][...], e_refs[1][...], e_refs[2][...])
        dy, dz, dng = vjp(acc)
        o_refs[0][...] = dy
        o_refs[1][...] = dz.astype(BF16)
        row0 = lax.broadcasted_iota(jnp.int32, (SUBLANES, 1), 0) == 0
        _acc_out(o_refs[2], jnp.where(row0, dng, 0.0), i == 0)

    wo, wo_block, wo_imap = _lw(w["w_out"], (SSD_D_INNER, d), lambda i, j, kk: (0, 0))
    dy, dz, dnorm = fused_mm(
        f"ssd_dgate_{tag}", rows=rows, k=d, n=SSD_D_INNER, tm=tm, tn=SSD_D_INNER,
        a_ops=[(dh, (tm, d), lambda i, j, kk: (i, 0))], pro=lambda a, o, i: a[0][...].astype(BF16),
        w=wo, w_block=wo_block, w_imap=wo_imap, dot=_dot_wt,
        e_ops=[(y, (tm, SSD_D_INNER), lambda i, j, kk: (i, 0)), (zx, (tm, SSD_D_INNER), lambda i, j, kk: (i, 0)),
               (w["norm"], (1, SSD_D_INNER), lambda i, j, kk: (0, 0))],
        epi=epi_gate,
        outs=[((rows, SSD_D_INNER), F32, (tm, SSD_D_INNER), lambda i, j, kk: (i, 0)),
              ((rows, SSD_D_INNER), BF16, (tm, SSD_D_INNER), lambda i, j, kk: (i, 0)),
              ((SUBLANES, SSD_D_INNER), F32, (SUBLANES, SSD_D_INNER), lambda i, j, kk: (0, 0))])
    (dxs, dbm, dcm, ddt, dpar), cres = ssd_bwd(f"ssd_dscan_{tag}", dy, xa, zx, dt_block, states, w["par"], comm)
    parts, dcw, dcb = [dz], [], []
    col = SSD_D_INNER
    for nm, dact in (("x", dxs), ("b", dbm), ("c", dcm)):
        wd = dact.shape[1]
        c0 = col - SSD_D_INNER
        du, dw_c, db_c = conv_bwd(f"ssd_dconv{nm}_{tag}", dact, zx, col, w["conv_w"][:, c0:c0 + wd],
                                  w["conv_b"][:, c0:c0 + wd])
        parts.append(du)
        dcw.append(dw_c)
        dcb.append(db_c)
        col += wd
    parts.append(ddt.astype(BF16))
    dzx = jnp.concatenate(parts, axis=1)
    k = dzx.shape[1]
    dw_in = simple_wgrad(f"ssd_dwin_{tag}", hn, dzx, t1=512, tn=896)
    dh_in, dln = rms_bwd_mm(
        f"ssd_dh_{tag}", [(dzx, (None, 896), lambda i, j, kk: (i, kk))], lambda r: r[0][...], k,
        w["w_in"], (d, 896), lambda i, j, kk: (0, kk), _dot_wt, h, ln_g, dh, tk=896)
    grads = dict(w_in=dw_in, w_out=dw_out, conv_w=jnp.concatenate(dcw, axis=1), conv_b=jnp.concatenate(dcb, axis=1),
                 par=dpar, norm=dnorm, ln=dln)
    return dh_in, grads, cres


HP = 2 * LANES
VP = 2 * MLA_V
N_PAIRS = MLA_HEADS // 2
ATT_SCALE = MLA_QK ** -0.5
ROT = MLA_ROPE // 2


def rope_tables(rows):
    inv = 1.0 / (ROPE_THETA ** (jnp.arange(0, MLA_ROPE, 2, dtype=F32) / MLA_ROPE))
    pos = jnp.arange(rows, dtype=F32) - PAD
    ang = pos[:, None] * inv[None, :]
    cos, sin = jnp.cos(ang), jnp.sin(ang)
    one = jnp.ones((rows, MLA_NOPE), F32)
    zero = jnp.zeros((rows, LANES - MLA_QK), F32)
    zn = jnp.zeros((rows, MLA_NOPE), F32)
    zr = jnp.zeros((rows, ROT), F32)
    cosf = jnp.concatenate([one, cos, cos, zero], axis=1)
    sina = jnp.concatenate([zn, -sin, zr, zero], axis=1)
    sinb = jnp.concatenate([zn, zr, sin, zero], axis=1)
    return cosf, sina, sinb


def _qk_norm_rope(x, g, cosf, sina, sinb):
    r = lax.rsqrt(jnp.sum(x * x, axis=-1, keepdims=True) * (1.0 / MLA_QK) + EPS)
    xn = x * r * g
    return xn * cosf + pltpu.roll(xn, LANES - ROT, 1) * sina + pltpu.roll(xn, ROT, 1) * sinb


def _qk_norm_rope_bwd(dout, x, g, cosf, sina, sinb):
    dxn = dout * cosf + pltpu.roll(dout * sina, ROT, 1) + pltpu.roll(dout * sinb, LANES - ROT, 1)
    r = lax.rsqrt(jnp.sum(x * x, axis=-1, keepdims=True) * (1.0 / MLA_QK) + EPS)
    xh = x * r
    t = dxn * g
    dx = r * (t - xh * (jnp.sum(t * xh, axis=-1, keepdims=True) * (1.0 / MLA_QK)))
    return dx, _rows8(dxn * xh)


def _rope_lanes():
    lane = lax.broadcasted_iota(jnp.int32, (1, LANES), 1)
    return jnp.logical_and(lane >= MLA_NOPE, lane < MLA_QK)


QW = MLA_HEADS * LANES
VW = MLA_HEADS * MLA_V


def qk_prep(name, qraw, kvraw, lat, kpe_block, qg, kg, tabs):
    rows = qraw.shape[0]
    tm = _row_tile(rows)

    def body(q_ref, k0_ref, k1_ref, v_ref, pe_ref, qg_ref, kg_ref, c_ref, sa_ref, sb_ref,
             qo_ref, ko_ref, kt_ref, vo_ref, vt_ref):
        tab = (c_ref[...], sa_ref[...], sb_ref[...])
        pe = pe_ref[...]
        for hd in range(MLA_HEADS):
            sl = slice(hd * LANES, (hd + 1) * LANES)
            qo_ref[:, sl] = _qk_norm_rope(q_ref[:, sl], qg_ref[...], *tab).astype(BF16)
            kr = k0_ref if hd < MLA_HEADS // 2 else k1_ref
            ks = slice((hd % (MLA_HEADS // 2)) * LANES, (hd % (MLA_HEADS // 2) + 1) * LANES)
            kk = _qk_norm_rope(kr[:, ks] + pe, kg_ref[...], *tab)
            ko_ref[:, sl] = kk.astype(BF16)
            kt_ref[sl, :] = kk.T.astype(BF16)
        vo_ref[...] = v_ref[...].astype(BF16)
        for c in range(VW // LANES):
            sl = slice(c * LANES, (c + 1) * LANES)
            vt_ref[sl, :] = v_ref[:, sl].T.astype(BF16)

    row = lambda w, b: pl.BlockSpec((tm, w), lambda i: (i, b))
    col = lambda w: pl.BlockSpec((w, tm), lambda i: (0, i))
    one = pl.BlockSpec((1, LANES), lambda i: (0, 0))
    return pl.pallas_call(
        body, name=name, grid=(rows // tm,),
        in_specs=[row(QW, 0), row(VW, 0), row(VW, 1), row(VW, 2), row(LANES, kpe_block), one, one,
                  row(LANES, 0), row(LANES, 0), row(LANES, 0)],
        out_specs=[row(QW, 0), row(QW, 0), col(QW), row(VW, 0), col(VW)],
        out_shape=[jax.ShapeDtypeStruct((rows, QW), BF16), jax.ShapeDtypeStruct((rows, QW), BF16),
                   jax.ShapeDtypeStruct((QW, rows), BF16), jax.ShapeDtypeStruct((rows, VW), BF16),
                   jax.ShapeDtypeStruct((VW, rows), BF16)],
        compiler_params=_params(1),
    )(qraw, kvraw, kvraw, kvraw, lat, qg, kg, *tabs)


def qk_prep_bwd(name, dq_t, dk, dv, qraw, kvraw, lat, kpe_block, qg, kg, tabs):
    rows = qraw.shape[0]
    tm = _row_tile(rows)

    def body(dq_ref, dk_ref, dv_ref, q_ref, k0_ref, k1_ref, pe_ref, qg_ref, kg_ref, c_ref, sa_ref, sb_ref,
             dqo_ref, dkvo_ref, dpe_ref, dqg_ref, dkg_ref):
        i = pl.program_id(0)
        tab = (c_ref[...], sa_ref[...], sb_ref[...])
        pe = pe_ref[...]
        dpe = jnp.zeros((tm, LANES), F32)
        dqg = jnp.zeros((SUBLANES, LANES), F32)
        dkg = jnp.zeros((SUBLANES, LANES), F32)
        for hd in range(MLA_HEADS):
            sl = slice(hd * LANES, (hd + 1) * LANES)
            dx, dg = _qk_norm_rope_bwd(dq_ref[sl, :].T, q_ref[:, sl], qg_ref[...], *tab)
            dqo_ref[:, sl] = dx.astype(BF16)
            dqg = dqg + dg
            kr = k0_ref if hd < MLA_HEADS // 2 else k1_ref
            ks = slice((hd % (MLA_HEADS // 2)) * LANES, (hd % (MLA_HEADS // 2) + 1) * LANES)
            dx, dg = _qk_norm_rope_bwd(dk_ref[:, sl], kr[:, ks] + pe, kg_ref[...], *tab)
            dkvo_ref[:, sl] = dx.astype(BF16)
            dpe = dpe + dx
            dkg = dkg + dg
        dkvo_ref[:, QW:QW + VW] = dv_ref[...].astype(BF16)
        dpe_ref[...] = jnp.where(_rope_lanes(), dpe, 0.0)
        _acc_out(dqg_ref, dqg, i == 0)
        _acc_out(dkg_ref, dkg, i == 0)

    row = lambda w, b: pl.BlockSpec((tm, w), lambda i: (i, b))
    one = pl.BlockSpec((1, LANES), lambda i: (0, 0))
    acc = pl.BlockSpec((SUBLANES, LANES), lambda i: (0, 0))
    return pl.pallas_call(
        body, name=name, grid=(rows // tm,),
        in_specs=[pl.BlockSpec((QW, tm), lambda i: (0, i)), row(QW, 0), row(VW, 0), row(QW, 0), row(VW, 0), row(VW, 1),
                  row(LANES, kpe_block), one, one, row(LANES, 0), row(LANES, 0), row(LANES, 0)],
        out_specs=[row(QW, 0), row(QW + VW, 0), row(LANES, 0), acc, acc],
        out_shape=[jax.ShapeDtypeStruct((rows, QW), BF16), jax.ShapeDtypeStruct((rows, QW + VW), BF16),
                   jax.ShapeDtypeStruct((rows, LANES), F32),
                   jax.ShapeDtypeStruct((SUBLANES, LANES), F32), jax.ShapeDtypeStruct((SUBLANES, LANES), F32)],
        compiler_params=_params(1),
    )(dq_t, dk, dv, qraw, kvraw, kvraw, lat, qg, kg, *tabs)


def _att_mask_t(qb, kb, bt):
    kpos = kb * bt + lax.broadcasted_iota(jnp.int32, (bt, bt), 0)
    qpos = qb * bt + lax.broadcasted_iota(jnp.int32, (bt, bt), 1)
    return jnp.logical_and(kpos <= qpos, jnp.logical_or(kpos >= PAD, qpos < PAD))


def attn_fwd(name, q, k, vt, comm=None):
    rows = q.shape[0]
    bt = _row_tile(rows)
    nb = rows // bt
    assert bt >= CHUNK

    def body(q_ref, k_ref, vt_ref, o_ref, lse_ref, m_scr, l_scr, acc_scr):
        qi = pl.program_id(1)
        lse_ref[...] = jnp.zeros((SUBLANES, bt), F32)
        m_scr[...] = jnp.full((2, 1, bt), NEG, F32)
        l_scr[...] = jnp.zeros((2, 1, bt), F32)
        acc_scr[...] = jnp.zeros((2, MLA_V, bt), F32)

        def tile(kb, masked):
            r0 = pl.multiple_of(kb * bt, LANES)
            for hh in range(2):
                qs = slice(hh * LANES, (hh + 1) * LANES)
                vs = slice(hh * MLA_V, (hh + 1) * MLA_V)
                s = lax.dot_general(k_ref[pl.ds(r0, bt), qs], q_ref[:, qs], (NT, ((), ())),
                                    preferred_element_type=F32) * ATT_SCALE
                if masked:
                    s = jnp.where(_att_mask_t(qi, kb, bt), s, NEG)
                m = m_scr[hh]
                m_new = jnp.maximum(m, jnp.max(s, axis=0, keepdims=True))
                alpha = jnp.exp(m - m_new)
                p = jnp.exp(s - m_new)
                l_scr[hh] = alpha * l_scr[hh] + jnp.sum(p, axis=0, keepdims=True)
                acc_scr[hh] = alpha * acc_scr[hh] + jnp.dot(vt_ref[vs, pl.ds(r0, bt)], p.astype(BF16),
                                                            preferred_element_type=F32)
                m_scr[hh] = m_new

        tile(0, True)

        @pl.when(qi > 0)
        def _():
            def mid(kb, carry):
                tile(kb, False)
                return carry

            lax.fori_loop(1, qi, mid, 0)
            tile(qi, True)

        for hh in range(2):
            l = l_scr[hh]
            o_ref[hh * MLA_V:(hh + 1) * MLA_V, :] = acc_scr[hh] / l
            lse_ref[hh:hh + 1, :] = m_scr[hh] + jnp.log(l)

    return carrier_call(
        name, body, (N_PAIRS, nb),
        [pl.BlockSpec((bt, HP), lambda p, i: (i, p)),
         pl.BlockSpec((rows, HP), lambda p, i: (0, p)),
         pl.BlockSpec((VP, rows), lambda p, i: (p, 0))],
        [pl.BlockSpec((VP, bt), lambda p, i: (p, i)),
         pl.BlockSpec((None, SUBLANES, bt), lambda p, i: (p, 0, i))],
        [jax.ShapeDtypeStruct((VW, rows), F32), jax.ShapeDtypeStruct((N_PAIRS, SUBLANES, rows), F32)],
        [pltpu.VMEM((2, 1, bt), F32), pltpu.VMEM((2, 1, bt), F32), pltpu.VMEM((2, MLA_V, bt), F32)],
        (q, k, vt), comm)


def attn_bwd(name, q, k, kt, v, do_t, lse, delta, comm=None):
    rows = q.shape[0]
    bt = _row_tile(rows)
    nb = rows // bt

    def body(q_ref, k_ref, kt_ref, v_ref, do_ref, lse_ref, dl_ref, dq_ref, dk_ref, dv_ref, dk_scr, dv_scr):
        ki = pl.program_id(1)

        @pl.when(ki == 0)
        def _():
            dq_ref[...] = jnp.zeros((HP, rows), F32)

        dk_scr[...] = jnp.zeros((bt, HP), F32)
        dv_scr[...] = jnp.zeros((bt, VP), F32)

        def tile(qb, masked):
            c0 = pl.multiple_of(qb * bt, LANES)
            for hh in range(2):
                qs = slice(hh * LANES, (hh + 1) * LANES)
                vs = slice(hh * MLA_V, (hh + 1) * MLA_V)
                qv = q_ref[pl.ds(c0, bt), qs]
                dov = do_ref[vs, pl.ds(c0, bt)]
                lse = lse_ref[hh:hh + 1, pl.ds(c0, bt)]
                dl = dl_ref[hh:hh + 1, pl.ds(c0, bt)]
                s = lax.dot_general(k_ref[:, qs], qv, (NT, ((), ())), preferred_element_type=F32) * ATT_SCALE
                p = jnp.exp(s - lse)
                if masked:
                    p = jnp.where(_att_mask_t(qb, ki, bt), p, 0.0)
                dp = jnp.dot(v_ref[:, vs], dov, preferred_element_type=F32)
                ds = (p * (dp - dl) * ATT_SCALE).astype(BF16)
                dv_scr[:, vs] += lax.dot_general(p.astype(BF16), dov, (NT, ((), ())), preferred_element_type=F32)
                dk_scr[:, qs] += jnp.dot(ds, qv, preferred_element_type=F32)
                dq_ref[qs, pl.ds(c0, bt)] += jnp.dot(kt_ref[qs, :], ds, preferred_element_type=F32)

        @pl.when(ki == 0)
        def _():
            def every(qb, carry):
                tile(qb, True)
                return carry

            lax.fori_loop(0, nb, every, 0)

        @pl.when(ki > 0)
        def _():
            tile(ki, True)

            def later(qb, carry):
                tile(qb, False)
                return carry

            lax.fori_loop(ki + 1, nb, later, 0)

        dk_ref[...] = dk_scr[...]
        dv_ref[...] = dv_scr[...]

    stat = pl.BlockSpec((None, SUBLANES, rows), lambda p, i: (p, 0, 0))
    return carrier_call(
        name, body, (N_PAIRS, nb),
        [pl.BlockSpec((rows, HP), lambda p, i: (0, p)),
         pl.BlockSpec((bt, HP), lambda p, i: (i, p)),
         pl.BlockSpec((HP, bt), lambda p, i: (p, i)),
         pl.BlockSpec((bt, VP), lambda p, i: (i, p)),
         pl.BlockSpec((VP, rows), lambda p, i: (p, 0)),
         stat, stat],
        [pl.BlockSpec((HP, rows), lambda p, i: (p, 0)),
         pl.BlockSpec((bt, HP), lambda p, i: (i, p)),
         pl.BlockSpec((bt, VP), lambda p, i: (i, p))],
        [jax.ShapeDtypeStruct((QW, rows), F32), jax.ShapeDtypeStruct((rows, QW), F32),
         jax.ShapeDtypeStruct((rows, VW), F32)],
        [pltpu.VMEM((bt, HP), F32), pltpu.VMEM((bt, VP), F32)],
        (q, k, kt, v, do_t, lse, delta), comm)


def _dot_cast_w(a, w_ref):
    return jnp.dot(a, w_ref[...].astype(BF16), preferred_element_type=F32)


def _dot_cast_wt(a, w_ref):
    return lax.dot_general(a, w_ref[...].astype(BF16), (NT, ((), ())), preferred_element_type=F32)


LAT_W = 768
KPE_BLOCK = MLA_Q_RANK // LANES
KV_BLOCK = (MLA_Q_RANK + LANES) // MLA_KV_RANK


def mla_layer_fwd(tag, h, ln_g, w, tabs, comm=None):
    lat, hn = norm_mm(f"mla_in_{tag}", h, ln_g, w["w_in"], tn=LAT_W)
    qraw, qn = norm_mm(f"mla_q_{tag}", lat, w["q_a"], w["w_q"], tn=512, k_cols=MLA_Q_RANK, col_block=0)
    kvraw, kvn = norm_mm(f"mla_kv_{tag}", lat, w["kv_a"], w["w_kv"], tn=512, k_cols=MLA_KV_RANK, col_block=KV_BLOCK)
    q, k, kt, v, vt = qk_prep(f"mla_prep_{tag}", qraw, kvraw, lat, KPE_BLOCK, w["q_norm"], w["k_norm"], tabs)
    (o_t, lse), cres = attn_fwd(f"mla_attn_{tag}", q, k, vt, comm)
    out = res_mm(f"mla_out_{tag}", [(o_t, (VW, None), lambda i, j, kk: (0, i))],
                 lambda r: r[0][...].T.astype(BF16), VW, w["w_out"], h, tn=512)
    return out, (h, hn, lat, qn, kvn, qraw, kvraw, q, k, kt, v, o_t, lse), cres


def mla_layer_bwd(tag, dh, saved, ln_g, w, tabs, comm=None):
    h, hn, lat, qn, kvn, qraw, kvraw, q, k, kt, v, o_t, lse = saved
    rows, d = h.shape
    tm = _row_tile(rows)

    def epi_set(acc, e_refs, o_refs, i, j):
        o_refs[0][...] = acc.astype(BF16)

    dw_out, = fused_mm(
        f"mla_dwout_{tag}", rows=VW, k=rows, n=d, tm=512, tn=d, tk=tm,
        a_ops=[(o_t, (512, tm), lambda i, j, kk: (i, kk))], pro=lambda a, o_, i: a[0][...].astype(BF16),
        w=dh, w_block=(tm, d), w_imap=lambda i, j, kk: (kk, 0), dot=_dot_cast_w, epi=epi_set,
        outs=[((VW, d), BF16, (512, d), lambda i, j, kk: (i, 0))])

    def epi_do(acc, e_refs, o_refs, i, j):
        o_refs[0][...] = acc.astype(BF16)
        prod = acc * e_refs[0][...]
        o_refs[1][...] = jnp.zeros((N_PAIRS, SUBLANES, tm), F32)
        for hd in range(MLA_HEADS):
            o_refs[1][hd // 2, hd % 2:hd % 2 + 1, :] = jnp.sum(prod[hd * MLA_V:(hd + 1) * MLA_V, :], axis=0,
                                                               keepdims=True)

    wo, wo_block, wo_imap = _lw(w["w_out"], (VW, d), lambda i, j, kk: (0, 0))
    do_t, delta = fused_mm(
        f"mla_do_{tag}", rows=VW, k=d, n=rows, tm=VW, tn=tm,
        a_ops=[(wo, wo_block, wo_imap)], pro=lambda a, o_, i: a[0][...],
        w=dh, w_block=(tm, d), w_imap=lambda i, j, kk: (j, 0), dot=_dot_cast_wt,
        e_ops=[(o_t, (VW, tm), lambda i, j, kk: (0, j))], epi=epi_do,
        outs=[((VW, rows), BF16, (VW, tm), lambda i, j, kk: (0, j)),
              ((N_PAIRS, SUBLANES, rows), F32, (N_PAIRS, SUBLANES, tm), lambda i, j, kk: (0, 0, j))])
    (dq_t, dk, dv), cres = attn_bwd(f"mla_dattn_{tag}", q, k, kt, v, do_t, lse, delta, comm)
    dqraw, dkvraw, dpe, dqg, dkg = qk_prep_bwd(f"mla_dprep_{tag}", dq_t, dk, dv, qraw, kvraw, lat, KPE_BLOCK,
                                               w["q_norm"], w["k_norm"], tabs)
    dw_q = simple_wgrad(f"mla_dwq_{tag}", qn, dqraw, t1=MLA_Q_RANK, tn=512)
    dqlat, dqa = rms_bwd_mm(
        f"mla_dqlat_{tag}", [(dqraw, (None, QW), lambda i, j, kk: (i, 0))], lambda r: r[0][...], QW,
        w["w_q"], (MLA_Q_RANK, QW), lambda i, j, kk: (0, 0), _dot_wt, lat, w["q_a"], None,
        h_cols=MLA_Q_RANK, h_col_block=0, add_dh=False)
    dw_kv = simple_wgrad(f"mla_dwkv_{tag}", kvn, dkvraw, t1=MLA_KV_RANK, tn=512)
    dkvlat, dkva = rms_bwd_mm(
        f"mla_dkvlat_{tag}", [(dkvraw, (None, QW + VW), lambda i, j, kk: (i, 0))], lambda r: r[0][...], QW + VW,
        w["w_kv"], (MLA_KV_RANK, QW + VW), lambda i, j, kk: (0, 0), _dot_wt, lat, w["kv_a"], None,
        h_cols=MLA_KV_RANK, h_col_block=KV_BLOCK, add_dh=False)
    dlat = jnp.concatenate([dqlat.astype(BF16), dpe.astype(BF16), dkvlat.astype(BF16)], axis=1)
    dw_in = simple_wgrad(f"mla_dwin_{tag}", hn, dlat, t1=512, tn=LAT_W)
    dh_in, dln = rms_bwd_mm(
        f"mla_dh_{tag}", [(dlat, (None, LAT_W), lambda i, j, kk: (i, 0))], lambda r: r[0][...], LAT_W,
        w["w_in"], (d, LAT_W), lambda i, j, kk: (0, 0), _dot_wt, h, ln_g, dh)
    grads = dict(w_in=dw_in, w_q=dw_q, w_kv=dw_kv, w_out=dw_out, q_a=dqa, kv_a=dkva, q_norm=dqg, k_norm=dkg, ln=dln)
    return dh_in, grads, cres


def loss_head(h, target):
    rows, d = h.shape
    nb = rows // CHUNK

    def body(h_ref, t_ref, l_ref, dh_ref):
        i = pl.program_id(0)
        err = jnp.where(i > 0, h_ref[...] - t_ref[...], 0.0)
        dh_ref[...] = err * (1.0 / d)
        _acc_out(l_ref, _rows8(err * err) * (0.5 / d), i == 0)

    return pl.pallas_call(
        body, name="loss_head", grid=(nb,),
        in_specs=[pl.BlockSpec((CHUNK, d), lambda i: (i, 0)),
                  pl.BlockSpec((CHUNK, d), lambda i: (jnp.maximum(i - 1, 0), 0))],
        out_specs=[pl.BlockSpec((SUBLANES, d), lambda i: (0, 0)), pl.BlockSpec((CHUNK, d), lambda i: (i, 0))],
        out_shape=[jax.ShapeDtypeStruct((SUBLANES, d), F32), jax.ShapeDtypeStruct((rows, d), F32)],
        compiler_params=_params(1),
    )(h, target)


def _adamw(w, g, m, v):
    m = ADAM_B1 * m + (1.0 - ADAM_B1) * g
    v = ADAM_B2 * v + (1.0 - ADAM_B2) * jnp.square(g)
    m_hat = m / (1.0 - ADAM_B1 ** ADAM_STEP)
    v_hat = v / (1.0 - ADAM_B2 ** ADAM_STEP)
    delta = -ADAM_LR * (m_hat / (jnp.sqrt(v_hat) + ADAM_EPS) + ADAM_WD * w)
    return delta, m, v


def reduce_adamw(name, recvs, w, m, v):
    nl, r, c = w.shape
    tr = 128 if r % 128 == 0 else r
    nr = r // tr

    def body(*refs):
        r_refs = refs[:nl]
        w_ref, m_ref, v_ref, g_ref, d_ref, mo_ref, vo_ref = refs[nl:]
        layer = pl.program_id(0)
        for l in range(nl):
            @pl.when(layer == l)
            def _(l=l):
                g = r_refs[l][0].astype(F32)
                for s in range(1, N_DEV):
                    g = g + r_refs[l][s].astype(F32)
                g_ref[...] = g
                d_ref[...], mo_ref[...], vo_ref[...] = _adamw(w_ref[...], g, m_ref[...], v_ref[...])

    def recv_spec(l):
        return pl.BlockSpec((N_DEV, tr, c),
                            lambda y, i: (0, jnp.where(y == l, i, jnp.where(y < l, 0, nr - 1)), 0))

    blk = pl.BlockSpec((None, tr, c), lambda y, i: (y, i, 0))
    return pl.pallas_call(
        body, name=name, grid=(nl, nr),
        in_specs=[recv_spec(l) for l in range(nl)] + [blk, blk, blk],
        out_specs=[blk] * 4, out_shape=[jax.ShapeDtypeStruct((nl, r, c), F32)] * 4,
        compiler_params=_params(2),
    )(*recvs, w, m, v)


def small_reduce(recv):
    def body(r_ref, o_ref):
        g = r_ref[0]
        for s in range(1, N_DEV):
            g = g + r_ref[s]
        o_ref[...] = g

    return pl.pallas_call(body, name="small_reduce", out_shape=jax.ShapeDtypeStruct(recv.shape[1:], F32))(recv)


def small_adamw(w, g, m, v):
    def body(w_ref, g_ref, m_ref, v_ref, d_ref, mo_ref, vo_ref):
        d_ref[...], mo_ref[...], vo_ref[...] = _adamw(w_ref[...], g_ref[...], m_ref[...], v_ref[...])

    return pl.pallas_call(body, name="small_adamw", out_shape=[jax.ShapeDtypeStruct(w.shape, F32)] * 3)(w, g, m, v)


def _pack(parts):
    flat, meta, off = [], [], 0
    for p in parts:
        n = int(np.prod(p.shape))
        flat.append(p.reshape(-1).astype(F32))
        meta.append((off, p.shape))
        off += n
    total = -(-off // (SUBLANES * LANES)) * (SUBLANES * LANES)
    flat.append(jnp.zeros((total - off,), F32))
    return jnp.concatenate(flat).reshape(total // LANES, LANES), meta


def _unpack(packed, meta):
    flat = packed.reshape(-1)
    return [flat[off:off + int(np.prod(shape))].reshape(shape) for off, shape in meta]


MESH = pl.DeviceIdType.MESH
N_PEERS = N_DEV - 1


def _me():
    return lax.axis_index("x"), lax.axis_index("y"), lax.axis_index("c")


def _peer(k):
    x, y, c = _me()
    return (1 - x if k & 4 else x, 1 - y if k & 2 else y, 1 - c if k & 1 else c)


def _dev_index(pos):
    return 4 * pos[0] + 2 * pos[1] + pos[2]


def make_comm(items):
    n = len(items)

    def part(ref, a, idx):
        rows = items[a][1]
        if rows == "all":
            return ref
        return ref.at[idx] if rows is None else ref.at[pl.ds(idx * rows, rows)]

    def part_shape(a):
        arr, rows = items[a]
        if rows == "all":
            return arr.shape
        return arr.shape[1:] if rows is None else (rows,) + arr.shape[1:]

    def run(phase, ins, outs, send_sems, recv_sems, local_sems):
        me = _dev_index(_me())
        for a in range(n):
            local = pltpu.make_async_copy(part(ins[a], a, me), outs[a].at[me], local_sems.at[a])
            if phase == "start":
                local.start()
            for k in range(1, N_DEV):
                peer = _peer(k)
                if phase == "start":
                    pltpu.make_async_remote_copy(
                        src_ref=part(ins[a], a, _dev_index(peer)), dst_ref=outs[a].at[me],
                        send_sem=send_sems.at[a, k - 1], recv_sem=recv_sems.at[a, k - 1],
                        device_id=peer, device_id_type=MESH).start()
                else:
                    cp = pltpu.make_async_remote_copy(
                        src_ref=part(ins[a], a, me), dst_ref=outs[a].at[_dev_index(peer)],
                        send_sem=send_sems.at[a, k - 1], recv_sem=recv_sems.at[a, k - 1],
                        device_id=peer, device_id_type=MESH)
                    cp.wait_recv()
                    cp.wait_send()
            if phase == "wait":
                local.wait()

    return dict(
        ins=[it[0] for it in items],
        outs=[jax.ShapeDtypeStruct((N_DEV,) + part_shape(a), items[a][0].dtype) for a in range(n)],
        sems=[pltpu.SemaphoreType.DMA((n, N_PEERS)), pltpu.SemaphoreType.DMA((n, N_PEERS)),
              pltpu.SemaphoreType.DMA((n,))],
        run=run)


ANY_SPEC = pl.BlockSpec(memory_space=pl.ANY)


def comm_call(name, comm):
    n, no = len(comm["ins"]), len(comm["outs"])

    def body(*refs):
        comm["run"]("start", refs[:n], refs[n:n + no], *refs[n + no:])
        comm["run"]("wait", refs[:n], refs[n:n + no], *refs[n + no:])

    return pl.pallas_call(
        body, name=name, in_specs=[ANY_SPEC] * n, out_specs=[ANY_SPEC] * no, out_shape=comm["outs"],
        scratch_shapes=comm["sems"])(*comm["ins"])


def with_comm(comm, body, grid, in_specs, out_specs, out_shape, scratch_shapes):
    if comm is None:
        return body, in_specs, out_specs, out_shape, scratch_shapes, [], len(out_shape)
    n_in, n_out, n_scr = len(in_specs), len(out_shape), len(scratch_shapes)
    ci, co = len(comm["ins"]), len(comm["outs"])

    def wrapped(*refs):
        ins, cins = refs[:n_in], refs[n_in:n_in + ci]
        outs = refs[n_in + ci:n_in + ci + n_out]
        couts = refs[n_in + ci + n_out:n_in + ci + n_out + co]
        rest = refs[n_in + ci + n_out + co:]
        scr, sems = rest[:n_scr], rest[n_scr:]
        first = functools.reduce(jnp.logical_and, [pl.program_id(a) == 0 for a in range(len(grid))])
        last = functools.reduce(jnp.logical_and, [pl.program_id(a) == grid[a] - 1 for a in range(len(grid))])

        @pl.when(first)
        def _():
            comm["run"]("start", cins, couts, *sems)

        body(*ins, *outs, *scr)

        @pl.when(last)
        def _():
            comm["run"]("wait", cins, couts, *sems)

    return (wrapped, list(in_specs) + [ANY_SPEC] * ci, list(out_specs) + [ANY_SPEC] * co,
            list(out_shape) + list(comm["outs"]), list(scratch_shapes) + list(comm["sems"]), list(comm["ins"]), n_out)


WEIGHTS = ['meta_tokens', 'ln_mix', 'ln_mlp', 'ssd_w_in', 'ssd_conv_w', 'ssd_conv_b', 'ssd_dt_bias', 'ssd_a_log',
           'ssd_d', 'ssd_norm', 'ssd_w_out', 'mla_w_in', 'mla_q_a_norm', 'mla_w_q_b', 'mla_kv_a_norm', 'mla_w_kv_b',
           'mla_q_norm', 'mla_k_norm', 'mla_w_out', 'mlp_w_up', 'mlp_w_down']
BIG = ['ssd_w_in', 'ssd_w_out', 'mla_w_in', 'mla_w_q_b', 'mla_w_kv_b', 'mla_w_out', 'mlp_w_up', 'mlp_w_down']
SMALL_SHARDED = ['meta_tokens', 'ssd_conv_w', 'mla_q_a_norm', 'mla_kv_a_norm']
SMALL_REPL = ['ln_mix', 'ln_mlp', 'ssd_conv_b', 'ssd_dt_bias', 'ssd_a_log', 'ssd_d', 'ssd_norm', 'mla_q_norm',
              'mla_k_norm']
SMALL = SMALL_REPL + SMALL_SHARDED
SSD_IN_PAD = 6272
SSD_IN_TN = 896
MLA_IN = MLA_Q_RANK + MLA_KV_RANK + MLA_ROPE


def _pad_last(v, n):
    return jnp.pad(v, [(0, 0)] * (v.ndim - 1) + [(0, n - v.shape[-1])])


SSD_BIG = ['ssd_w_in', 'ssd_w_out']
MLA_BIG = ['mla_w_in', 'mla_w_q_b', 'mla_w_kv_b', 'mla_w_out']
MLP_BIG = ['mlp_w_up', 'mlp_w_down']


def _layer_big(i):
    return [(n, i // 2) for n in (SSD_BIG if i % 2 == 0 else MLA_BIG)] + [(n, i) for n in MLP_BIG]


def _layer_weights(i, gw, W, full):
    j = i // 2
    d = W['ln_mix'].shape[-1]
    if i % 2 == 0:
        wi = gw['ssd_w_in'].transpose(1, 0, 2).reshape(d, -1)
        par = jnp.concatenate([_pad_last(W[n][j][None], LANES) for n in ('ssd_dt_bias', 'ssd_a_log', 'ssd_d')]
                              + [jnp.zeros((SUBLANES - 3, LANES), F32)])
        mix = dict(w_in=_pad_last(wi, SSD_IN_PAD), conv_w=full['ssd_conv_w'][j], conv_b=W['ssd_conv_b'][j][None],
                   par=par, norm=W['ssd_norm'][j][None], w_out=gw['ssd_w_out'].reshape(SSD_D_INNER, d))
    else:
        wi = gw['mla_w_in'].reshape(d, MLA_IN)
        kpe = jnp.pad(wi[:, MLA_Q_RANK + MLA_KV_RANK:], ((0, 0), (MLA_NOPE, LANES - MLA_QK)))
        wq = gw['mla_w_q_b'].transpose(1, 0, 2).reshape(MLA_Q_RANK, MLA_HEADS, MLA_QK)
        wkv = gw['mla_w_kv_b'].transpose(1, 0, 2).reshape(MLA_KV_RANK, MLA_HEADS, MLA_NOPE + MLA_V)
        mix = dict(
            w_in=jnp.concatenate([wi[:, :MLA_Q_RANK], kpe, wi[:, MLA_Q_RANK:MLA_Q_RANK + MLA_KV_RANK]], axis=1),
            w_q=_pad_last(wq, LANES).reshape(MLA_Q_RANK, QW),
            w_kv=jnp.concatenate([_pad_last(wkv[..., :MLA_NOPE], LANES).reshape(MLA_KV_RANK, QW),
                                  wkv[..., MLA_NOPE:].reshape(MLA_KV_RANK, VW)], axis=1),
            w_out=gw['mla_w_out'].reshape(VW, d), q_a=full['mla_q_a_norm'][j][None],
            kv_a=full['mla_kv_a_norm'][j][None],
            q_norm=_pad_last(W['mla_q_norm'][j][None], LANES), k_norm=_pad_last(W['mla_k_norm'][j][None], LANES))
    return mix, gw['mlp_w_up'], gw['mlp_w_down'].reshape(-1, d)


def _step(x, target, W, M, V):
    d = x.shape[-1]
    me = _dev_index(_me())
    depth = W['ln_mix'].shape[0]

    def gather_items(i):
        return [(W[n][l].astype(BF16), "all") for n, l in _layer_big(i)]

    small_pack, small_meta = _pack([W[n] for n in SMALL_SHARDED])
    got = comm_call("gather_0", make_comm(gather_items(0) + [(small_pack, "all")]))
    per_dev = [_unpack(got[-1][s], small_meta) for s in range(N_DEV)]
    full = {n: jnp.concatenate([per_dev[s][i] for s in range(N_DEV)], axis=-1) for i, n in enumerate(SMALL_SHARDED)}
    weights = {0: _layer_weights(0, {n: a for (n, _), a in zip(_layer_big(0), got)}, W, full)}

    h = jnp.concatenate([jnp.zeros((PAD, d), F32), full['meta_tokens'], x], axis=0)
    rows = h.shape[0]
    tabs = rope_tables(rows)
    saved = []
    for i in range(depth):
        comm = make_comm(gather_items(i + 1)) if i + 1 < depth else None
        mix, up, down = weights[i]
        if i % 2 == 0:
            h, s_mix, got = ssd_layer_fwd(f"{i}", h, W['ln_mix'][i][None], mix, comm)
        else:
            h, s_mix, got = mla_layer_fwd(f"{i}", h, W['ln_mix'][i][None], mix, tabs, comm)
        if comm is not None:
            weights[i + 1] = _layer_weights(i + 1, {n: a for (n, _), a in zip(_layer_big(i + 1), got)}, W, full)
        h, s_mlp = mlp_fwd(f"{i}", h, W['ln_mlp'][i][None], up, down)
        saved.append((s_mix, s_mlp))
    loss_part, dh = loss_head(h, target)
    loss = lax.psum(jnp.sum(loss_part), ("x", "y", "c"))

    recv = {}
    pending = None
    small = {n: [None] * W[n].shape[0] for n in SMALL if n != 'meta_tokens'}
    for i in reversed(range(depth)):
        j = i // 2
        s_mix, s_mlp = saved[i]
        mix, up, down = weights[i]
        dh, dw_up, dw_down, dg = mlp_bwd(f"{i}", dh, s_mlp, W['ln_mlp'][i][None], up, down)
        small['ln_mlp'][i] = dg.sum(0)
        comm = make_comm([it for _, it in pending]) if pending else None
        sends = []
        if i % 2 == 0:
            dh, g, got = ssd_layer_bwd(f"{i}", dh, s_mix, W['ln_mix'][i][None], mix, comm)
            n_in = W['ssd_w_in'].shape[-1]
            sends.append((g['w_in'][:, :N_DEV * n_in].reshape(d, N_DEV, n_in).transpose(1, 0, 2), None))
            sends.append((g['w_out'], SSD_D_INNER // N_DEV))
            small['ssd_conv_w'][j] = g['conv_w'].reshape(SSD_CONV, SUBLANES, -1).sum(1)
            small['ssd_conv_b'][j] = g['conv_b'].sum(0)
            small['ssd_dt_bias'][j] = g['par'][0, :SSD_HEADS]
            small['ssd_a_log'][j] = g['par'][1, :SSD_HEADS]
            small['ssd_d'][j] = g['par'][2, :SSD_HEADS]
            small['ssd_norm'][j] = g['norm'].sum(0)
        else:
            dh, g, got = mla_layer_bwd(f"{i}", dh, s_mix, W['ln_mix'][i][None], mix, tabs, comm)
            gi = g['w_in']
            gi = jnp.concatenate([gi[:, :MLA_Q_RANK], gi[:, MLA_Q_RANK + LANES:],
                                  gi[:, MLA_Q_RANK + MLA_NOPE:MLA_Q_RANK + MLA_QK]], axis=1)
            sends.append((gi, d // N_DEV))
            gq = g['w_q'].reshape(MLA_Q_RANK, MLA_HEADS, LANES)[..., :MLA_QK]
            sends.append((gq.reshape(MLA_Q_RANK, N_DEV, -1).transpose(1, 0, 2), None))
            gkv = jnp.concatenate([g['w_kv'][:, :QW].reshape(MLA_KV_RANK, MLA_HEADS, LANES)[..., :MLA_NOPE],
                                   g['w_kv'][:, QW:].reshape(MLA_KV_RANK, MLA_HEADS, MLA_V)], axis=-1)
            sends.append((gkv.reshape(MLA_KV_RANK, N_DEV, -1).transpose(1, 0, 2), None))
            sends.append((g['w_out'], VW // N_DEV))
            small['mla_q_a_norm'][j] = g['q_a'].sum(0)
            small['mla_kv_a_norm'][j] = g['kv_a'].sum(0)
            small['mla_q_norm'][j] = g['q_norm'].sum(0)[:MLA_QK]
            small['mla_k_norm'][j] = g['k_norm'].sum(0)[:MLA_QK]
        small['ln_mix'][i] = g['ln'].sum(0)
        if comm is not None:
            recv.update({key: a for (key, _), a in zip(pending, got)})
        sends += [(dw_up, None), (dw_down, down.shape[0] // N_DEV)]
        pending = list(zip(_layer_big(i), sends))
    grad_x = dh[CHUNK:]
    small_full = {n: jnp.stack(v) for n, v in small.items()}
    small_full['meta_tokens'] = dh[PAD:CHUNK]

    gpack, gmeta = _pack([small_full[n] for n in SMALL])
    got = comm_call("exchange_0", make_comm([it for _, it in pending] + [(gpack, "all")]))
    recv.update({key: a for (key, _), a in zip(pending, got)})
    res = {}
    for n in BIG:
        res[n] = reduce_adamw(f"adamw_{n}", [recv[(n, l)] for l in range(W[n].shape[0])], W[n], M[n], V[n])
    gsum = dict(zip(SMALL, _unpack(small_reduce(got[-1]), gmeta)))
    for n in SMALL_SHARDED:
        wl = W[n].shape[-1]
        gsum[n] = lax.dynamic_slice_in_dim(gsum[n], me * wl, wl, axis=gsum[n].ndim - 1)
    wp, wmeta = _pack([W[n] for n in SMALL])
    gp, _ = _pack([gsum[n] for n in SMALL])
    mp, _ = _pack([M[n] for n in SMALL])
    vp, _ = _pack([V[n] for n in SMALL])
    upd = [_unpack(o, wmeta) for o in small_adamw(wp, gp, mp, vp)]
    for a, n in enumerate(SMALL):
        res[n] = [gsum[n], upd[0][a], upd[1][a], upd[2][a]]
    return (loss, grad_x[None]) + tuple(res[n][q] for q in range(4) for n in WEIGHTS)


def kernel(x, meta_tokens, ln_mix, ln_mlp, ssd_w_in, ssd_conv_w, ssd_conv_b, ssd_dt_bias, ssd_a_log, ssd_d, ssd_norm, ssd_w_out, mla_w_in, mla_q_a_norm, mla_w_q_b, mla_kv_a_norm, mla_w_kv_b, mla_q_norm, mla_k_norm, mla_w_out, mlp_w_up, mlp_w_down, loss_target, m_meta_tokens, m_ln_mix, m_ln_mlp, m_ssd_w_in, m_ssd_conv_w, m_ssd_conv_b, m_ssd_dt_bias, m_ssd_a_log, m_ssd_d, m_ssd_norm, m_ssd_w_out, m_mla_w_in, m_mla_q_a_norm, m_mla_w_q_b, m_mla_kv_a_norm, m_mla_w_kv_b, m_mla_q_norm, m_mla_k_norm, m_mla_w_out, m_mlp_w_up, m_mlp_w_down, v_meta_tokens, v_ln_mix, v_ln_mlp, v_ssd_w_in, v_ssd_conv_w, v_ssd_conv_b, v_ssd_dt_bias, v_ssd_a_log, v_ssd_d, v_ssd_norm, v_ssd_w_out, v_mla_w_in, v_mla_q_a_norm, v_mla_w_q_b, v_mla_kv_a_norm, v_mla_w_kv_b, v_mla_q_norm, v_mla_k_norm, v_mla_w_out, v_mlp_w_up, v_mlp_w_down):
    given = dict(locals())
    W = {n: given[n] for n in WEIGHTS}
    M = {n: given["m_" + n] for n in WEIGHTS}
    V = {n: given["v_" + n] for n in WEIGHTS}
    return _step(x[0], loss_target[0], W, M, V)
```

```python
import functools

import jax
import jax.numpy as jnp
import numpy as np
from jax import lax
from jax.experimental import pallas as pl
from jax.experimental.pallas import tpu as pltpu

F32 = jnp.float32
BF16 = jnp.bfloat16

EPS = 1e-6
N_META = 16
CHUNK = 128
PAD = CHUNK - N_META
SSD_HEAD_DIM = 64
SSD_HEADS = 32
SSD_GROUPS = 8
SSD_HPG = 4
SSD_STATE = 128
SSD_D_INNER = 2048
SSD_CONV = 4
MLA_HEADS = 16
MLA_NOPE = 64
MLA_ROPE = 32
MLA_V = 64
MLA_QK = 96
MLA_Q_RANK = 384
MLA_KV_RANK = 256
ROPE_THETA = 10000.0
LANES = 128
SUBLANES = 8
N_DEV = 8
VMEM_LIMIT = 56 * 1024 * 1024

ADAM_LR = 0.001
ADAM_B1 = 0.9
ADAM_B2 = 0.999
ADAM_EPS = 1e-08
ADAM_WD = 0.01
ADAM_STEP = 10

NEG = -1e30


def _row_tile(rows):
    return 384 if (rows % 384 == 0 and rows > 384) else 128


def _big_tile(rows):
    return 1408 if rows % 1408 == 0 else _row_tile(rows)


def _params(n_axes, vmem=VMEM_LIMIT):
    return pltpu.CompilerParams(dimension_semantics=("arbitrary",) * n_axes, vmem_limit_bytes=vmem)


def _dot(a, b, dims):
    return lax.dot_general(a.astype(BF16), b.astype(BF16), (dims, ((), ())), preferred_element_type=F32)


NN = ((1,), (0,))
NT = ((1,), (1,))
TN = ((0,), (0,))


@jax.custom_vjp
def bdot_nn(a, b):
    return _dot(a, b, NN)


@jax.custom_vjp
def bdot_nt(a, b):
    return _dot(a, b, NT)


@jax.custom_vjp
def bdot_tn(a, b):
    return _dot(a, b, TN)


bdot_nn.defvjp(lambda a, b: (_dot(a, b, NN), (a, b)),
               lambda r, g: (_dot(g, r[1], NT), _dot(r[0], g, TN)))
bdot_nt.defvjp(lambda a, b: (_dot(a, b, NT), (a, b)),
               lambda r, g: (_dot(g, r[1], NN), _dot(g, r[0], TN)))
bdot_tn.defvjp(lambda a, b: (_dot(a, b, TN), (a, b)),
               lambda r, g: (_dot(r[1], g, NT), _dot(r[0], g, NN)))


def _rows8(v):
    r, n = v.shape
    return v.reshape(r // SUBLANES, SUBLANES, n).sum(axis=0)


def _row_mask(i, tm):
    return (i * tm + lax.broadcasted_iota(jnp.int32, (tm, 1), 0)) >= PAD


def fused_mm(name, *, rows, k, n, tm, tn, tk=None, a_ops, pro, w, w_block, w_imap, dot, e_ops=(), epi, outs):
    tk = tk or k
    ni, nj, nk = rows // tm, n // tn, k // tk
    assert rows % tm == 0 and n % tn == 0 and k % tk == 0
    assert nk == 1 or nj == 1
    cache = nk == 1 and nj > 1
    na, ne, no = len(a_ops), len(e_ops), len(outs)

    def body(*refs):
        a_refs = refs[:na]
        w_ref = refs[na]
        e_refs = refs[na + 1:na + 1 + ne]
        o_refs = refs[na + 1 + ne:na + 1 + ne + no]
        scr = refs[na + 1 + ne + no:]
        i, j, kk = pl.program_id(0), pl.program_id(1), pl.program_id(2)
        if cache:
            a_scr = scr[0]

            @pl.when(j == 0)
            def _():
                a_scr[...] = pro(a_refs, o_refs, i)

            a = a_scr[...]
        else:
            a = pro(a_refs, o_refs, i)
        part = dot(a, w_ref)
        if nk == 1:
            epi(part, e_refs, o_refs, i, j)
        else:
            acc_ref = scr[0]

            @pl.when(kk == 0)
            def _():
                acc_ref[...] = part

            @pl.when(kk > 0)
            def _():
                acc_ref[...] += part

            @pl.when(kk == nk - 1)
            def _():
                epi(acc_ref[...], e_refs, o_refs, i, j)

    scratch = []
    if cache:
        scratch.append(pltpu.VMEM((tm, k), BF16))
    if nk > 1:
        scratch.append(pltpu.VMEM((tm, tn), F32))
    in_specs = [pl.BlockSpec(b, m) for (_, b, m) in a_ops]
    in_specs.append(pl.BlockSpec(w_block, w_imap))
    in_specs += [pl.BlockSpec(b, m) for (_, b, m) in e_ops]
    return pl.pallas_call(
        body, name=name, grid=(ni, nj, nk),
        in_specs=in_specs,
        out_specs=[pl.BlockSpec(b, m) for (_, _, b, m) in outs],
        out_shape=[jax.ShapeDtypeStruct(s, d) for (s, d, _, _) in outs],
        scratch_shapes=scratch,
        compiler_params=_params(3),
    )(*[a for (a, _, _) in a_ops], w, *[e for (e, _, _) in e_ops])


def _lw(w, block, imap):
    if isinstance(w, tuple):
        arr, layer = w
        return arr, (None,) + block, (lambda i, j, kk: (layer,) + imap(i, j, kk))
    return w, block, imap


def _dot_w(a, w_ref):
    return jnp.dot(a, w_ref[...], preferred_element_type=F32)


def _dot_wt(a, w_ref):
    return lax.dot_general(a, w_ref[...], (NT, ((), ())), preferred_element_type=F32)


def _rms_pro(h, g):
    r = lax.rsqrt(jnp.mean(h * h, axis=-1, keepdims=True) + EPS)
    return h * r * g


def _rms_bwd(dyn, h, g):
    r = lax.rsqrt(jnp.mean(h * h, axis=-1, keepdims=True) + EPS)
    xh = h * r
    t = dyn * g
    dh = r * (t - xh * jnp.mean(t * xh, axis=-1, keepdims=True))
    return dh, _rows8(dyn * xh)


def _acc_out(ref, val, first):
    @pl.when(first)
    def _():
        ref[...] = val

    @pl.when(jnp.logical_not(first))
    def _():
        ref[...] += val


def norm_mm(name, h, g, w, *, tn, k_cols=None, col_block=0, w_stacked=False):
    rows = h.shape[0]
    k = k_cols or h.shape[1]
    wshape = (w[0].shape[1:] if isinstance(w, tuple) else w.shape)
    n = wshape[0] * wshape[2] if w_stacked else wshape[1]
    tm = _big_tile(rows)

    def pro(a_refs, o_refs, i):
        hn = _rms_pro(a_refs[0][...], a_refs[1][...]).astype(BF16)
        o_refs[1][...] = hn
        return hn

    def epi(acc, e_refs, o_refs, i, j):
        o_refs[0][...] = acc

    if w_stacked:
        w_block, w_imap = (None, k, tn), (lambda i, j, kk: (j, 0, 0))
    else:
        w_block, w_imap = (k, tn), (lambda i, j, kk: (0, j))
    w, w_block, w_imap = _lw(w, w_block, w_imap)
    return fused_mm(
        name, rows=rows, k=k, n=n, tm=tm, tn=tn,
        a_ops=[(h, (tm, k), lambda i, j, kk: (i, col_block)), (g, (1, k), lambda i, j, kk: (0, 0))],
        pro=pro, w=w, w_block=w_block, w_imap=w_imap, dot=_dot_w, epi=epi,
        outs=[((rows, n), F32, (tm, tn), lambda i, j, kk: (i, j)),
              ((rows, k), BF16, (tm, k), lambda i, j, kk: (i, 0))])


def res_mm(name, a_ops, pro, k, w, res, *, tn, save_dtype=None, tm=None, tk=None):
    rows, n = res.shape
    tm = tm or _row_tile(rows)
    assert tk is None or save_dtype is None

    def pro2(a_refs, o_refs, i):
        a = pro(a_refs)
        if save_dtype is not None:
            o_refs[1][...] = a
        return a

    def epi(acc, e_refs, o_refs, i, j):
        o_refs[0][...] = e_refs[0][...] + acc

    outs = [((rows, n), F32, (tm, tn), lambda i, j, kk: (i, j))]
    if save_dtype is not None:
        outs.append(((rows, k), save_dtype, (tm, k), lambda i, j, kk: (i, 0)))
    w, w_block, w_imap = _lw(w, (tk or k, tn), lambda i, j, kk: (kk, j))
    out = fused_mm(
        name, rows=rows, k=k, n=n, tm=tm, tn=tn, tk=tk,
        a_ops=[(a, tuple(tm if x is None else x for x in b), m) for (a, b, m) in a_ops],
        pro=pro2, w=w, w_block=w_block, w_imap=w_imap, dot=_dot_w,
        e_ops=[(res, (tm, tn), lambda i, j, kk: (i, j))], epi=epi, outs=outs)
    return out if save_dtype is not None else out[0]


def wgrad_mm(name, a_ops, pro_a, g_ops, pro_g, *, rows, k1, n, t1, tn, out_shape=None, out_block=None, out_imap=None):
    tt = _row_tile(rows)
    n1, n2, nt = k1 // t1, n // tn, rows // tt
    assert k1 % t1 == 0 and n % tn == 0
    na = len(a_ops)

    def body(*refs):
        a_refs = refs[:na]
        g_refs = refs[na:-2]
        o_ref, acc = refs[-2:]
        t = pl.program_id(2)
        a = pro_a(a_refs).astype(BF16)
        g = pro_g(g_refs).astype(BF16)
        _acc_out(acc, lax.dot_general(a, g, (TN, ((), ())), preferred_element_type=F32), t == 0)

        @pl.when(t == nt - 1)
        def _():
            if len(o_ref.shape) == 3:
                ws = o_ref.shape[2]
                for q in range(o_ref.shape[0]):
                    o_ref[q] = acc[:, q * ws:(q + 1) * ws].astype(BF16)
            else:
                o_ref[...] = acc[...].astype(BF16)

    return pl.pallas_call(
        body, name=name, grid=(n1, n2, nt),
        in_specs=[pl.BlockSpec(b, m) for (_, b, m) in list(a_ops) + list(g_ops)],
        out_specs=pl.BlockSpec(out_block or (t1, tn), out_imap or (lambda a, b, t: (a, b))),
        out_shape=jax.ShapeDtypeStruct(out_shape or (k1, n), BF16),
        scratch_shapes=[pltpu.VMEM((t1, tn), F32)],
        compiler_params=_params(3),
    )(*[a for (a, _, _) in list(a_ops) + list(g_ops)])


def simple_wgrad(name, a, g, *, a_cols=None, a_col_block=0, t1=None, tn=None, **kw):
    rows = a.shape[0]
    k1 = a_cols or a.shape[1]
    n = g.shape[1]
    tt = _row_tile(rows)
    t1 = t1 or min(k1, 512)
    tn = tn or min(n, 1024)
    return wgrad_mm(
        name,
        [(a, (tt, t1), lambda x, y, t: (t, x + a_col_block * (k1 // t1)))], lambda r: r[0][...],
        [(g, (tt, tn), lambda x, y, t: (t, y))], lambda r: r[0][...],
        rows=rows, k1=k1, n=n, t1=t1, tn=tn, **kw)


def rms_bwd_mm(name, dz_ops, pro, k, w, w_block, w_imap, dot, h, g, dh, *, tk=None, h_cols=None, h_col_block=0,
               add_dh=True, tm=None):
    rows = h.shape[0]
    n = h_cols or h.shape[1]
    tm = tm or _big_tile(rows)
    ni = rows // tm
    w, w_block, w_imap = _lw(w, w_block, w_imap)

    def epi(acc, e_refs, o_refs, i, j):
        d, dg = _rms_bwd(acc, e_refs[0][...], e_refs[1][...])
        if add_dh:
            d = d + e_refs[2][...]
        o_refs[0][...] = jnp.where(_row_mask(i, tm), d, 0.0)
        _acc_out(o_refs[1], dg, i == 0)

    e_ops = [(h, (tm, n), lambda i, j, kk: (i, h_col_block)), (g, (1, n), lambda i, j, kk: (0, 0))]
    if add_dh:
        e_ops.append((dh, (tm, n), lambda i, j, kk: (i, 0)))
    return fused_mm(
        name, rows=rows, k=k, n=n, tm=tm, tn=n, tk=tk,
        a_ops=[(a, tuple(tm if x is None else x for x in b), m) for (a, b, m) in dz_ops],
        pro=lambda a_refs, o_refs, i: pro(a_refs), w=w, w_block=w_block, w_imap=w_imap, dot=dot,
        e_ops=e_ops, epi=epi,
        outs=[((rows, n), F32, (tm, n), lambda i, j, kk: (i, 0)),
              ((SUBLANES, n), F32, (SUBLANES, n), lambda i, j, kk: (0, 0))])


def _relu2(u):
    r = jnp.maximum(u, 0.0)
    return r * r


def mlp_fwd(tag, h, g, w_up_st, w_down):
    d_ff = w_down.shape[0]
    u, hn = norm_mm(f"mlp_up_{tag}", h, g, w_up_st, tn=w_up_st.shape[2], w_stacked=True)
    out = res_mm(f"mlp_down_{tag}", [(u, (None, 512), lambda i, j, kk: (i, kk))],
                 lambda r: _relu2(r[0][...]).astype(BF16), d_ff, w_down, h, tn=h.shape[1],
                 tm=_big_tile(h.shape[0]), tk=512)
    return out, (h, hn, u)


def mlp_bwd(tag, dh, saved, g, w_up_st, w_down):
    h, hn, u = saved
    rows, d = h.shape
    d_ff = w_down.shape[0]
    ts = w_up_st.shape[2]
    tm = _big_tile(rows)
    tt = _row_tile(rows)
    wd, wd_block, wd_imap = _lw(w_down, (512, d), lambda i, j, kk: (j, 0))

    def epi_du(acc, e_refs, o_refs, i, j):
        o_refs[0][...] = (acc * (2.0 * jnp.maximum(e_refs[0][...], 0.0))).astype(BF16)

    du, = fused_mm(
        f"mlp_du_{tag}", rows=rows, k=d, n=d_ff, tm=tm, tn=512,
        a_ops=[(dh, (tm, d), lambda i, j, kk: (i, 0))], pro=lambda a, o, i: a[0][...].astype(BF16),
        w=wd, w_block=wd_block, w_imap=wd_imap, dot=_dot_wt,
        e_ops=[(u, (tm, 512), lambda i, j, kk: (i, j))], epi=epi_du,
        outs=[((rows, d_ff), BF16, (tm, 512), lambda i, j, kk: (i, j))])
    half = d_ff // 2
    dw_down = wgrad_mm(
        f"mlp_dwdown_{tag}",
        [(u, (tt, half), lambda a, b, t: (t, a))], lambda r: _relu2(r[0][...]),
        [(dh, (tt, d), lambda a, b, t: (t, 0))], lambda r: r[0][...],
        rows=rows, k1=d_ff, n=d, t1=half, tn=d)
    dw_up = simple_wgrad(f"mlp_dwup_{tag}", hn, du, t1=d, tn=half, out_shape=(N_DEV, d, ts),
                         out_block=(half // ts, d, ts), out_imap=lambda a, b, t: (b, 0, 0))
    dh_in, dg = rms_bwd_mm(
        f"mlp_dh_{tag}", [(du, (None, ts), lambda i, j, kk: (i, kk))], lambda r: r[0][...], d_ff,
        w_up_st, (None, d, ts), lambda i, j, kk: (kk, 0, 0), _dot_wt, h, g, dh, tk=ts)
    return dh_in, dw_up, dw_down, dg


CONV_HALO = SUBLANES
CONV_TC = 512


def _silu(x):
    return x * jax.nn.sigmoid(x)


def _conv_pre(ext_ref, w, b, tm):
    pre = b
    for k in range(SSD_CONV):
        pre = pre + w[k:k + 1, :] * ext_ref[pl.ds(CONV_HALO - (SSD_CONV - 1) + k, tm), :]
    return pre


def conv_fwd(name, zx, col0, width, conv_w, conv_b):
    rows = zx.shape[0]
    tm = _row_tile(rows)
    cb0 = col0 // CONV_TC
    hb = tm // CONV_HALO

    def body(u_ref, halo_ref, w_ref, b_ref, o_ref, ext):
        i = pl.program_id(1)
        ext[pl.ds(0, CONV_HALO), :] = jnp.where(i > 0, halo_ref[...], 0.0)
        ext[pl.ds(CONV_HALO, tm), :] = u_ref[...]
        pre = _conv_pre(ext, w_ref[...], b_ref[...], tm)
        o_ref[...] = jnp.where(_row_mask(i, tm), _silu(pre), 0.0)

    return pl.pallas_call(
        body, name=name, grid=(width // CONV_TC, rows // tm),
        in_specs=[pl.BlockSpec((tm, CONV_TC), lambda j, i: (i, cb0 + j)),
                  pl.BlockSpec((CONV_HALO, CONV_TC), lambda j, i: (jnp.maximum(i * hb - 1, 0), cb0 + j)),
                  pl.BlockSpec((SSD_CONV, CONV_TC), lambda j, i: (0, j)),
                  pl.BlockSpec((1, CONV_TC), lambda j, i: (0, j))],
        out_specs=pl.BlockSpec((tm, CONV_TC), lambda j, i: (i, j)),
        out_shape=jax.ShapeDtypeStruct((rows, width), F32),
        scratch_shapes=[pltpu.VMEM((tm + CONV_HALO, CONV_TC), F32)],
        compiler_params=_params(2),
    )(zx, zx, conv_w, conv_b)


def conv_bwd(name, dact, zx, col0, conv_w, conv_b):
    rows, width = dact.shape
    tm = _row_tile(rows)
    ni = rows // tm
    cb0 = col0 // CONV_TC
    hb = tm // CONV_HALO

    def body(d_ref, u_ref, halo_ref, w_ref, b_ref, du_ref, dw_ref, db_ref, ext, dext):
        s = pl.program_id(1)
        i = ni - 1 - s
        w = w_ref[...]
        ext[pl.ds(0, CONV_HALO), :] = jnp.where(i > 0, halo_ref[...], 0.0)
        ext[pl.ds(CONV_HALO, tm), :] = u_ref[...]
        pre = _conv_pre(ext, w, b_ref[...], tm)
        sg = jax.nn.sigmoid(pre)
        dpre = jnp.where(_row_mask(i, tm), d_ref[...] * (sg * (1.0 + pre * (1.0 - sg))), 0.0)

        @pl.when(s == 0)
        def _():
            dext[pl.ds(tm, CONV_HALO), :] = jnp.zeros((CONV_HALO, CONV_TC), F32)

        dext[pl.ds(0, tm), :] = dpre
        du = jnp.zeros((tm, CONV_TC), F32)
        for k in range(SSD_CONV):
            du = du + w[k:k + 1, :] * dext[pl.ds(SSD_CONV - 1 - k, tm), :]
        du_ref[...] = du.astype(du_ref.dtype)
        _acc_out(db_ref, _rows8(dpre), s == 0)
        for k in range(SSD_CONV):
            uk = ext[pl.ds(CONV_HALO - (SSD_CONV - 1) + k, tm), :]
            _acc_out(dw_ref.at[pl.ds(k * SUBLANES, SUBLANES), :], _rows8(dpre * uk), s == 0)
        dext[pl.ds(tm, CONV_HALO), :] = dpre[0:CONV_HALO, :]

    return pl.pallas_call(
        body, name=name, grid=(width // CONV_TC, ni),
        in_specs=[pl.BlockSpec((tm, CONV_TC), lambda j, s: (ni - 1 - s, j)),
                  pl.BlockSpec((tm, CONV_TC), lambda j, s: (ni - 1 - s, cb0 + j)),
                  pl.BlockSpec((CONV_HALO, CONV_TC), lambda j, s: (jnp.maximum((ni - 1 - s) * hb - 1, 0), cb0 + j)),
                  pl.BlockSpec((SSD_CONV, CONV_TC), lambda j, s: (0, j)),
                  pl.BlockSpec((1, CONV_TC), lambda j, s: (0, j))],
        out_specs=[pl.BlockSpec((tm, CONV_TC), lambda j, s: (ni - 1 - s, j)),
                   pl.BlockSpec((SSD_CONV * SUBLANES, CONV_TC), lambda j, s: (0, j)),
                   pl.BlockSpec((SUBLANES, CONV_TC), lambda j, s: (0, j))],
        out_shape=[jax.ShapeDtypeStruct((rows, width), BF16),
                   jax.ShapeDtypeStruct((SSD_CONV * SUBLANES, width), F32),
                   jax.ShapeDtypeStruct((SUBLANES, width), F32)],
        scratch_shapes=[pltpu.VMEM((tm + CONV_HALO, CONV_TC), F32), pltpu.VMEM((tm + CONV_HALO, CONV_TC), F32)],
        compiler_params=_params(2),
    )(dact, zx, zx, conv_w, conv_b)


@functools.partial(jax.custom_vjp, nondiff_argnums=(1,))
def _sub_row(x, h):
    return x[h:h + 1, :]


_sub_row.defvjp(
    lambda x, h: (x[h:h + 1, :], None),
    lambda h, _, g: (jnp.where(lax.broadcasted_iota(jnp.int32, (LANES, 1), 0) == h, g, 0.0),))


def _splitter(axis, size, count):
    def blocks(x):
        return tuple(lax.slice_in_dim(x, q * size, (q + 1) * size, axis=axis) for q in range(count))

    split = jax.custom_vjp(blocks)
    split.defvjp(lambda x: (blocks(x), None), lambda _, gs: (jnp.concatenate(gs, axis=axis),))
    return split


def _split3(x):
    hi = x.astype(BF16)
    r = x - hi.astype(F32)
    mid = r.astype(BF16)
    return hi, mid, (r - mid.astype(F32)).astype(BF16)


def _expand_impl(x, e):
    return sum(jnp.dot(t, e, preferred_element_type=F32) for t in _split3(x))


@jax.custom_vjp
def _expand(x, e):
    return _expand_impl(x, e)


def _expand_bwd(e, g):
    hi, mid, _ = _split3(g)
    dx = sum(lax.dot_general(t, e, (NT, ((), ())), preferred_element_type=F32) for t in (hi, mid))
    return dx, jnp.zeros_like(e)


_expand.defvjp(lambda x, e: (_expand_impl(x, e), e), _expand_bwd)

HEAD_PAIR = 2 * SSD_HEAD_DIM
GROUP_W = SSD_HPG * SSD_HEAD_DIM


def _ssd_chunk(xs, bm, cm, dtraw, prev, par, c, tri, e64, e128):
    li = lax.broadcasted_iota(jnp.int32, (CHUNK, CHUNK), 0)
    si = lax.broadcasted_iota(jnp.int32, (CHUNK, CHUNK), 1)
    causal = li >= si
    first_head = lax.broadcasted_iota(jnp.int32, (1, HEAD_PAIR), 1) < SSD_HEAD_DIM
    dt = jnp.where(_row_mask(c, CHUNK), jax.nn.softplus(dtraw + par[0:1, :]), 0.0)
    a = -jnp.exp(par[1:2, :])
    acs = jnp.dot(tri, dt * a, precision=lax.Precision.HIGHEST, preferred_element_type=F32)
    acs_t = acs.T
    last = acs[CHUNK - 1:CHUNK, :]
    misc = jnp.concatenate([jnp.exp(last), par[2:3, :], jnp.zeros((SUBLANES - 2, LANES), F32)], axis=0)
    wide = _expand(jnp.concatenate([dt, dt * jnp.exp(last - acs), jnp.exp(acs)], axis=0), e64)
    dt_w, dtend_w, start_w = _splitter(0, CHUNK, 3)(wide)
    misc_w = _expand(misc, e64)
    col_w = _splitter(1, CHUNK, SSD_HEADS)(_expand(acs, e128))
    groups = _splitter(1, GROUP_W, SSD_GROUPS)
    xs_g, prev_g, start_g = groups(xs), groups(prev), groups(start_w)
    xdt_p = _splitter(1, HEAD_PAIR, SSD_HEADS // 2)(xs * dt_w)
    xdtend_g = groups(xs * dtend_w)
    last_g, skip_g = groups(misc_w[0:1, :]), groups(misc_w[1:2, :])
    b_g, c_g = _splitter(1, SSD_STATE, SSD_GROUPS)(bm), _splitter(1, SSD_STATE, SSD_GROUPS)(cm)
    ys, news = [], []
    for g in range(SSD_GROUPS):
        cb = bdot_nt(c_g[g], b_g[g])
        st = bdot_nn(b_g[g].T, xdtend_g[g])
        y_off = bdot_nn(c_g[g], prev_g[g]) * start_g[g]
        pairs = []
        for q in range(SSD_HPG // 2):
            xp = xdt_p[g * (SSD_HPG // 2) + q]
            acc = None
            for r in range(2):
                head = SSD_HPG * g + 2 * q + r
                seg = jnp.where(causal, col_w[head] - _sub_row(acs_t, head), 0.0)
                decay = jnp.where(causal, jnp.exp(seg), 0.0)
                t = bdot_nn(cb * decay, jnp.where(first_head if r == 0 else jnp.logical_not(first_head), xp, 0.0))
                acc = t if acc is None else acc + t
            pairs.append(acc)
        ys.append(jnp.concatenate(pairs, axis=1) + y_off + xs_g[g] * skip_g[g])
        news.append(prev_g[g] * last_g[g] + st)
    return jnp.concatenate(ys, axis=1), jnp.concatenate(news, axis=1)


def _expanders():
    e64 = np.zeros((LANES, SSD_D_INNER), np.float32)
    e128 = np.zeros((LANES, SSD_HEADS * CHUNK), np.float32)
    for h in range(SSD_HEADS):
        e64[h, h * SSD_HEAD_DIM:(h + 1) * SSD_HEAD_DIM] = 1.0
        e128[h, h * CHUNK:(h + 1) * CHUNK] = 1.0
    return jnp.asarray(e64, BF16), jnp.asarray(e128, BF16)


def _tri():
    return jnp.asarray(np.tril(np.ones((CHUNK, CHUNK), np.float32)))


BC_W = SSD_GROUPS * SSD_STATE


def ssd_fwd(name, xa, zx, dt_block, par, comm=None):
    rows = xa.shape[0]
    nc = rows // CHUNK

    def body(xs_ref, b_ref, c_ref, dt_ref, par_ref, tri_ref, e64_ref, e128_ref, y_ref, st_ref, state):
        c = pl.program_id(0)

        @pl.when(c == 0)
        def _():
            state[...] = jnp.zeros((SSD_STATE, SSD_D_INNER), F32)

        prev = state[...]
        st_ref[...] = prev
        y, new = _ssd_chunk(xs_ref[...], b_ref[...], c_ref[...], dt_ref[...], prev, par_ref[...], c, tri_ref[...],
                            e64_ref[...], e128_ref[...])
        y_ref[...] = y
        state[...] = new

    const = lambda a: pl.BlockSpec(a.shape, lambda c: (0,) * a.ndim)
    consts = (par, _tri()) + _expanders()
    return carrier_call(
        name, body, (nc,),
        [pl.BlockSpec((CHUNK, SSD_D_INNER), lambda c: (c, 0)),
         pl.BlockSpec((CHUNK, BC_W), lambda c: (c, SSD_D_INNER // BC_W)),
         pl.BlockSpec((CHUNK, BC_W), lambda c: (c, SSD_D_INNER // BC_W + 1)),
         pl.BlockSpec((CHUNK, LANES), lambda c: (c, dt_block))] + [const(a) for a in consts],
        [pl.BlockSpec((CHUNK, SSD_D_INNER), lambda c: (c, 0)),
         pl.BlockSpec((None, SSD_STATE, SSD_D_INNER), lambda c: (c, 0, 0))],
        [jax.ShapeDtypeStruct((rows, SSD_D_INNER), F32),
         jax.ShapeDtypeStruct((nc, SSD_STATE, SSD_D_INNER), F32)],
        [pltpu.VMEM((SSD_STATE, SSD_D_INNER), F32)],
        (xa, xa, xa, zx) + consts, comm)


def carrier_call(name, body, grid, in_specs, out_specs, out_shape, scratch_shapes, args, comm):
    body, in_specs, out_specs, out_shape, scratch_shapes, extra, n_own = with_comm(
        comm, body, grid, in_specs, out_specs, out_shape, scratch_shapes)
    res = pl.pallas_call(
        body, name=name, grid=grid, in_specs=in_specs, out_specs=out_specs, out_shape=out_shape,
        scratch_shapes=scratch_shapes, compiler_params=_params(len(grid)))(*args, *extra)
    return res[:n_own], res[n_own:]


def ssd_bwd(name, dy, xa, zx, dt_block, states, par, comm=None):
    rows = xa.shape[0]
    nc = rows // CHUNK

    def body(dy_ref, xs_ref, b_ref, c_ref, dt_ref, st_ref, par_ref, tri_ref, e64_ref, e128_ref,
             dxs_ref, db_ref, dc_ref, ddt_ref, dpar_ref, dstate):
        s = pl.program_id(0)
        c = nc - 1 - s

        @pl.when(s == 0)
        def _():
            dstate[...] = jnp.zeros((SSD_STATE, SSD_D_INNER), F32)

        def f(xs, bm, cm, dtraw, prev, par_v):
            return _ssd_chunk(xs, bm, cm, dtraw, prev, par_v, c, tri_ref[...], e64_ref[...], e128_ref[...])

        _, vjp = jax.vjp(f, xs_ref[...], b_ref[...], c_ref[...], dt_ref[...], st_ref[...], par_ref[...])
        dxs, dbm, dcm, ddt, dprev, dpar = vjp((dy_ref[...], dstate[...]))
        dxs_ref[...] = dxs
        db_ref[...] = dbm
        dc_ref[...] = dcm
        ddt_ref[...] = ddt
        dstate[...] = dprev
        _acc_out(dpar_ref, dpar, s == 0)

    rev = lambda w, b: pl.BlockSpec((CHUNK, w), lambda s: (nc - 1 - s, b))
    const = lambda a: pl.BlockSpec(a.shape, lambda s: (0,) * a.ndim)
    consts = (par, _tri()) + _expanders()
    return carrier_call(
        name, body, (nc,),
        [rev(SSD_D_INNER, 0), rev(SSD_D_INNER, 0), rev(BC_W, SSD_D_INNER // BC_W), rev(BC_W, SSD_D_INNER // BC_W + 1),
         rev(LANES, dt_block),
         pl.BlockSpec((None, SSD_STATE, SSD_D_INNER), lambda s: (nc - 1 - s, 0, 0))] + [const(a) for a in consts],
        [rev(SSD_D_INNER, 0), rev(BC_W, 0), rev(BC_W, 0), rev(LANES, 0),
         pl.BlockSpec((SUBLANES, LANES), lambda s: (0, 0))],
        [jax.ShapeDtypeStruct((rows, SSD_D_INNER), F32),
         jax.ShapeDtypeStruct((rows, BC_W), F32),
         jax.ShapeDtypeStruct((rows, BC_W), F32),
         jax.ShapeDtypeStruct((rows, LANES), F32),
         jax.ShapeDtypeStruct((SUBLANES, LANES), F32)],
        [pltpu.VMEM((SSD_STATE, SSD_D_INNER), F32)],
        (dy, xa, xa, xa, zx, states) + consts, comm)


GN_W = SSD_D_INNER // SSD_GROUPS


def _gated_norm(y, z, ng):
    g = y * _silu(z)
    outs = []
    for q in range(SSD_GROUPS):
        gs = g[:, q * GN_W:(q + 1) * GN_W]
        outs.append(gs * lax.rsqrt(jnp.mean(gs * gs, axis=-1, keepdims=True) + EPS))
    return jnp.concatenate(outs, axis=1) * ng


def ssd_layer_fwd(tag, h, ln_g, w, comm=None):
    zx, hn = norm_mm(f"ssd_in_{tag}", h, ln_g, w["w_in"], tn=896)
    xa = conv_fwd(f"ssd_conv_{tag}", zx, SSD_D_INNER, 2 * SSD_D_INNER, w["conv_w"], w["conv_b"])
    dt_block = 3 * SSD_D_INNER // LANES
    (y, states), cres = ssd_fwd(f"ssd_scan_{tag}", xa, zx, dt_block, w["par"], comm)
    out, gn = res_mm(
        f"ssd_out_{tag}",
        [(y, (None, SSD_D_INNER), lambda i, j, kk: (i, 0)), (zx, (None, SSD_D_INNER), lambda i, j, kk: (i, 0)),
         (w["norm"], (1, SSD_D_INNER), lambda i, j, kk: (0, 0))],
        lambda r: _gated_norm(r[0][...], r[1][...], r[2][...]).astype(BF16),
        SSD_D_INNER, w["w_out"], h, tn=512, save_dtype=BF16)
    return out, (h, hn, zx, xa, y, states, gn), cres


def ssd_layer_bwd(tag, dh, saved, ln_g, w, comm=None):
    h, hn, zx, xa, y, states, gn = saved
    rows, d = h.shape
    tm = _row_tile(rows)
    dt_block = 3 * SSD_D_INNER // LANES
    dw_out = simple_wgrad(f"ssd_dwout_{tag}", gn, dh, t1=SSD_D_INNER, tn=d)

    def epi_gate(acc, e_refs, o_refs, i, j):
        _, vjp = jax.vjp(_gated_norm, e_refs[0][...], e_refs[1][...], e_refs[2][...])
        dy, dz, dng = vjp(acc)
        o_refs[0][...] = dy
        o_refs[1][...] = dz.astype(BF16)
        row0 = lax.broadcasted_iota(jnp.int32, (SUBLANES, 1), 0) == 0
        _acc_out(o_refs[2], jnp.where(row0, dng, 0.0), i == 0)

    wo, wo_block, wo_imap = _lw(w["w_out"], (SSD_D_INNER, d), lambda i, j, kk: (0, 0))
    dy, dz, dnorm = fused_mm(
        f"ssd_dgate_{tag}", rows=rows, k=d, n=SSD_D_INNER, tm=tm, tn=SSD_D_INNER,
        a_ops=[(dh, (tm, d), lambda i, j, kk: (i, 0))], pro=lambda a, o, i: a[0][...].astype(BF16),
        w=wo, w_block=wo_block, w_imap=wo_imap, dot=_dot_wt,
        e_ops=[(y, (tm, SSD_D_INNER), lambda i, j, kk: (i, 0)), (zx, (tm, SSD_D_INNER), lambda i, j, kk: (i, 0)),
               (w["norm"], (1, SSD_D_INNER), lambda i, j, kk: (0, 0))],
        epi=epi_gate,
        outs=[((rows, SSD_D_INNER), F32, (tm, SSD_D_INNER), lambda i, j, kk: (i, 0)),
              ((rows, SSD_D_INNER), BF16, (tm, SSD_D_INNER), lambda i, j, kk: (i, 0)),
              ((SUBLANES, SSD_D_INNER), F32, (SUBLANES, SSD_D_INNER), lambda i, j, kk: (0, 0))])
    (dxs, dbm, dcm, ddt, dpar), cres = ssd_bwd(f"ssd_dscan_{tag}", dy, xa, zx, dt_block, states, w["par"], comm)
    parts, dcw, dcb = [dz], [], []
    col = SSD_D_INNER
    for nm, dact in (("x", dxs), ("b", dbm), ("c", dcm)):
        wd = dact.shape[1]
        c0 = col - SSD_D_INNER
        du, dw_c, db_c = conv_bwd(f"ssd_dconv{nm}_{tag}", dact, zx, col, w["conv_w"][:, c0:c0 + wd],
                                  w["conv_b"][:, c0:c0 + wd])
        parts.append(du)
        dcw.append(dw_c)
        dcb.append(db_c)
        col += wd
    parts.append(ddt.astype(BF16))
    dzx = jnp.concatenate(parts, axis=1)
    k = dzx.shape[1]
    dw_in = simple_wgrad(f"ssd_dwin_{tag}", hn, dzx, t1=d, tn=896)
    dh_in, dln = rms_bwd_mm(
        f"ssd_dh_{tag}", [(dzx, (None, 896), lambda i, j, kk: (i, kk))], lambda r: r[0][...], k,
        w["w_in"], (d, 896), lambda i, j, kk: (0, kk), _dot_wt, h, ln_g, dh, tk=896,
        tm=704 if rows % 704 == 0 else None)
    grads = dict(w_in=dw_in, w_out=dw_out, conv_w=jnp.concatenate(dcw, axis=1), conv_b=jnp.concatenate(dcb, axis=1),
                 par=dpar, norm=dnorm, ln=dln)
    return dh_in, grads, cres


HP = 2 * LANES
VP = 2 * MLA_V
N_PAIRS = MLA_HEADS // 2
ATT_SCALE = MLA_QK ** -0.5
ROT = MLA_ROPE // 2


def rope_tables(rows):
    inv = 1.0 / (ROPE_THETA ** (jnp.arange(0, MLA_ROPE, 2, dtype=F32) / MLA_ROPE))
    pos = jnp.arange(rows, dtype=F32) - PAD
    ang = pos[:, None] * inv[None, :]
    cos, sin = jnp.cos(ang), jnp.sin(ang)
    one = jnp.ones((rows, MLA_NOPE), F32)
    zero = jnp.zeros((rows, LANES - MLA_QK), F32)
    zn = jnp.zeros((rows, MLA_NOPE), F32)
    zr = jnp.zeros((rows, ROT), F32)
    cosf = jnp.concatenate([one, cos, cos, zero], axis=1)
    sina = jnp.concatenate([zn, -sin, zr, zero], axis=1)
    sinb = jnp.concatenate([zn, zr, sin, zero], axis=1)
    return cosf, sina, sinb


def _qk_norm_rope(x, g, cosf, sina, sinb):
    r = lax.rsqrt(jnp.sum(x * x, axis=-1, keepdims=True) * (1.0 / MLA_QK) + EPS)
    xn = x * r * g
    return xn * cosf + pltpu.roll(xn, LANES - ROT, 1) * sina + pltpu.roll(xn, ROT, 1) * sinb


def _qk_norm_rope_bwd(dout, x, g, cosf, sina, sinb):
    dxn = dout * cosf + pltpu.roll(dout * sina, ROT, 1) + pltpu.roll(dout * sinb, LANES - ROT, 1)
    r = lax.rsqrt(jnp.sum(x * x, axis=-1, keepdims=True) * (1.0 / MLA_QK) + EPS)
    xh = x * r
    t = dxn * g
    dx = r * (t - xh * (jnp.sum(t * xh, axis=-1, keepdims=True) * (1.0 / MLA_QK)))
    return dx, _rows8(dxn * xh)


def _rope_lanes():
    lane = lax.broadcasted_iota(jnp.int32, (1, LANES), 1)
    return jnp.logical_and(lane >= MLA_NOPE, lane < MLA_QK)


QW = MLA_HEADS * LANES
VW = MLA_HEADS * MLA_V


def qk_prep(name, qraw, kvraw, lat, kpe_block, qg, kg, tabs):
    rows = qraw.shape[0]
    tm = _row_tile(rows)

    def body(q_ref, k0_ref, k1_ref, v_ref, pe_ref, qg_ref, kg_ref, c_ref, sa_ref, sb_ref,
             qo_ref, ko_ref, kt_ref, vo_ref, vt_ref):
        tab = (c_ref[...], sa_ref[...], sb_ref[...])
        pe = pe_ref[...]
        for hd in range(MLA_HEADS):
            sl = slice(hd * LANES, (hd + 1) * LANES)
            qo_ref[:, sl] = _qk_norm_rope(q_ref[:, sl], qg_ref[...], *tab).astype(BF16)
            kr = k0_ref if hd < MLA_HEADS // 2 else k1_ref
            ks = slice((hd % (MLA_HEADS // 2)) * LANES, (hd % (MLA_HEADS // 2) + 1) * LANES)
            kk = _qk_norm_rope(kr[:, ks] + pe, kg_ref[...], *tab)
            ko_ref[:, sl] = kk.astype(BF16)
            kt_ref[sl, :] = kk.T.astype(BF16)
        vo_ref[...] = v_ref[...].astype(BF16)
        for c in range(VW // LANES):
            sl = slice(c * LANES, (c + 1) * LANES)
            vt_ref[sl, :] = v_ref[:, sl].T.astype(BF16)

    row = lambda w, b: pl.BlockSpec((tm, w), lambda i: (i, b))
    col = lambda w: pl.BlockSpec((w, tm), lambda i: (0, i))
    one = pl.BlockSpec((1, LANES), lambda i: (0, 0))
    return pl.pallas_call(
        body, name=name, grid=(rows // tm,),
        in_specs=[row(QW, 0), row(VW, 0), row(VW, 1), row(VW, 2), row(LANES, kpe_block), one, one,
                  row(LANES, 0), row(LANES, 0), row(LANES, 0)],
        out_specs=[row(QW, 0), row(QW, 0), col(QW), row(VW, 0), col(VW)],
        out_shape=[jax.ShapeDtypeStruct((rows, QW), BF16), jax.ShapeDtypeStruct((rows, QW), BF16),
                   jax.ShapeDtypeStruct((QW, rows), BF16), jax.ShapeDtypeStruct((rows, VW), BF16),
                   jax.ShapeDtypeStruct((VW, rows), BF16)],
        compiler_params=_params(1),
    )(qraw, kvraw, kvraw, kvraw, lat, qg, kg, *tabs)


def qk_prep_bwd(name, dq_t, dk, dv, qraw, kvraw, lat, kpe_block, qg, kg, tabs):
    rows = qraw.shape[0]
    tm = _row_tile(rows)

    def body(dq_ref, dk_ref, dv_ref, q_ref, k0_ref, k1_ref, pe_ref, qg_ref, kg_ref, c_ref, sa_ref, sb_ref,
             dqo_ref, dkvo_ref, dpe_ref, dqg_ref, dkg_ref):
        i = pl.program_id(0)
        tab = (c_ref[...], sa_ref[...], sb_ref[...])
        pe = pe_ref[...]
        dpe = jnp.zeros((tm, LANES), F32)
        dqg = jnp.zeros((SUBLANES, LANES), F32)
        dkg = jnp.zeros((SUBLANES, LANES), F32)
        for hd in range(MLA_HEADS):
            sl = slice(hd * LANES, (hd + 1) * LANES)
            dx, dg = _qk_norm_rope_bwd(dq_ref[sl, :].T, q_ref[:, sl], qg_ref[...], *tab)
            dqo_ref[:, sl] = dx.astype(BF16)
            dqg = dqg + dg
            kr = k0_ref if hd < MLA_HEADS // 2 else k1_ref
            ks = slice((hd % (MLA_HEADS // 2)) * LANES, (hd % (MLA_HEADS // 2) + 1) * LANES)
            dx, dg = _qk_norm_rope_bwd(dk_ref[:, sl], kr[:, ks] + pe, kg_ref[...], *tab)
            dkvo_ref[:, sl] = dx.astype(BF16)
            dpe = dpe + dx
            dkg = dkg + dg
        dkvo_ref[:, QW:QW + VW] = dv_ref[...].astype(BF16)
        dpe_ref[...] = jnp.where(_rope_lanes(), dpe, 0.0)
        _acc_out(dqg_ref, dqg, i == 0)
        _acc_out(dkg_ref, dkg, i == 0)

    row = lambda w, b: pl.BlockSpec((tm, w), lambda i: (i, b))
    one = pl.BlockSpec((1, LANES), lambda i: (0, 0))
    acc = pl.BlockSpec((SUBLANES, LANES), lambda i: (0, 0))
    return pl.pallas_call(
        body, name=name, grid=(rows // tm,),
        in_specs=[pl.BlockSpec((QW, tm), lambda i: (0, i)), row(QW, 0), row(VW, 0), row(QW, 0), row(VW, 0), row(VW, 1),
                  row(LANES, kpe_block), one, one, row(LANES, 0), row(LANES, 0), row(LANES, 0)],
        out_specs=[row(QW, 0), row(QW + VW, 0), row(LANES, 0), acc, acc],
        out_shape=[jax.ShapeDtypeStruct((rows, QW), BF16), jax.ShapeDtypeStruct((rows, QW + VW), BF16),
                   jax.ShapeDtypeStruct((rows, LANES), F32),
                   jax.ShapeDtypeStruct((SUBLANES, LANES), F32), jax.ShapeDtypeStruct((SUBLANES, LANES), F32)],
        compiler_params=_params(1),
    )(dq_t, dk, dv, qraw, kvraw, kvraw, lat, qg, kg, *tabs)


def _att_mask_t(qb, kb, bt):
    kpos = kb * bt + lax.broadcasted_iota(jnp.int32, (bt, bt), 0)
    qpos = qb * bt + lax.broadcasted_iota(jnp.int32, (bt, bt), 1)
    return jnp.logical_and(kpos <= qpos, jnp.logical_or(kpos >= PAD, qpos < PAD))


def attn_fwd(name, q, k, vt, comm=None):
    rows = q.shape[0]
    bt = _row_tile(rows)
    nb = rows // bt
    assert bt >= CHUNK

    def body(q_ref, k_ref, vt_ref, o_ref, lse_ref, m_scr, l_scr, acc_scr):
        qi = pl.program_id(1)
        lse_ref[...] = jnp.zeros((SUBLANES, bt), F32)
        m_scr[...] = jnp.full((2, 1, bt), NEG, F32)
        l_scr[...] = jnp.zeros((2, 1, bt), F32)
        acc_scr[...] = jnp.zeros((2, MLA_V, bt), F32)

        def tile(kb, masked):
            r0 = pl.multiple_of(kb * bt, LANES)
            for hh in range(2):
                qs = slice(hh * LANES, (hh + 1) * LANES)
                vs = slice(hh * MLA_V, (hh + 1) * MLA_V)
                s = lax.dot_general(k_ref[pl.ds(r0, bt), qs], q_ref[:, qs], (NT, ((), ())),
                                    preferred_element_type=F32) * ATT_SCALE
                if masked:
                    s = jnp.where(_att_mask_t(qi, kb, bt), s, NEG)
                m = m_scr[hh]
                m_new = jnp.maximum(m, jnp.max(s, axis=0, keepdims=True))
                alpha = jnp.exp(m - m_new)
                p = jnp.exp(s - m_new)
                l_scr[hh] = alpha * l_scr[hh] + jnp.sum(p, axis=0, keepdims=True)
                acc_scr[hh] = alpha * acc_scr[hh] + jnp.dot(vt_ref[vs, pl.ds(r0, bt)], p.astype(BF16),
                                                            preferred_element_type=F32)
                m_scr[hh] = m_new

        tile(0, True)

        @pl.when(qi > 0)
        def _():
            def mid(kb, carry):
                tile(kb, False)
                return carry

            lax.fori_loop(1, qi, mid, 0)
            tile(qi, True)

        for hh in range(2):
            l = l_scr[hh]
            o_ref[hh * MLA_V:(hh + 1) * MLA_V, :] = acc_scr[hh] / l
            lse_ref[hh:hh + 1, :] = m_scr[hh] + jnp.log(l)

    return carrier_call(
        name, body, (N_PAIRS, nb),
        [pl.BlockSpec((bt, HP), lambda p, i: (i, p)),
         pl.BlockSpec((rows, HP), lambda p, i: (0, p)),
         pl.BlockSpec((VP, rows), lambda p, i: (p, 0))],
        [pl.BlockSpec((VP, bt), lambda p, i: (p, i)),
         pl.BlockSpec((None, SUBLANES, bt), lambda p, i: (p, 0, i))],
        [jax.ShapeDtypeStruct((VW, rows), F32), jax.ShapeDtypeStruct((N_PAIRS, SUBLANES, rows), F32)],
        [pltpu.VMEM((2, 1, bt), F32), pltpu.VMEM((2, 1, bt), F32), pltpu.VMEM((2, MLA_V, bt), F32)],
        (q, k, vt), comm)


def attn_bwd(name, q, k, kt, v, do_t, lse, delta, comm=None):
    rows = q.shape[0]
    bt = _row_tile(rows)
    nb = rows // bt

    def body(q_ref, k_ref, kt_ref, v_ref, do_ref, lse_ref, dl_ref, dq_ref, dk_ref, dv_ref, dk_scr, dv_scr):
        ki = pl.program_id(1)

        @pl.when(ki == 0)
        def _():
            dq_ref[...] = jnp.zeros((HP, rows), F32)

        dk_scr[...] = jnp.zeros((bt, HP), F32)
        dv_scr[...] = jnp.zeros((bt, VP), F32)

        def tile(qb, masked):
            c0 = pl.multiple_of(qb * bt, LANES)
            for hh in range(2):
                qs = slice(hh * LANES, (hh + 1) * LANES)
                vs = slice(hh * MLA_V, (hh + 1) * MLA_V)
                qv = q_ref[pl.ds(c0, bt), qs]
                dov = do_ref[vs, pl.ds(c0, bt)]
                lse = lse_ref[hh:hh + 1, pl.ds(c0, bt)]
                dl = dl_ref[hh:hh + 1, pl.ds(c0, bt)]
                s = lax.dot_general(k_ref[:, qs], qv, (NT, ((), ())), preferred_element_type=F32) * ATT_SCALE
                p = jnp.exp(s - lse)
                if masked:
                    p = jnp.where(_att_mask_t(qb, ki, bt), p, 0.0)
                dp = jnp.dot(v_ref[:, vs], dov, preferred_element_type=F32)
                ds = (p * (dp - dl) * ATT_SCALE).astype(BF16)
                dv_scr[:, vs] += lax.dot_general(p.astype(BF16), dov, (NT, ((), ())), preferred_element_type=F32)
                dk_scr[:, qs] += jnp.dot(ds, qv, preferred_element_type=F32)
                dq_ref[qs, pl.ds(c0, bt)] += jnp.dot(kt_ref[qs, :], ds, preferred_element_type=F32)

        @pl.when(ki == 0)
        def _():
            def every(qb, carry):
                tile(qb, True)
                return carry

            lax.fori_loop(0, nb, every, 0)

        @pl.when(ki > 0)
        def _():
            tile(ki, True)

            def later(qb, carry):
                tile(qb, False)
                return carry

            lax.fori_loop(ki + 1, nb, later, 0)

        dk_ref[...] = dk_scr[...]
        dv_ref[...] = dv_scr[...]

    stat = pl.BlockSpec((None, SUBLANES, rows), lambda p, i: (p, 0, 0))
    return carrier_call(
        name, body, (N_PAIRS, nb),
        [pl.BlockSpec((rows, HP), lambda p, i: (0, p)),
         pl.BlockSpec((bt, HP), lambda p, i: (i, p)),
         pl.BlockSpec((HP, bt), lambda p, i: (p, i)),
         pl.BlockSpec((bt, VP), lambda p, i: (i, p)),
         pl.BlockSpec((VP, rows), lambda p, i: (p, 0)),
         stat, stat],
        [pl.BlockSpec((HP, rows), lambda p, i: (p, 0)),
         pl.BlockSpec((bt, HP), lambda p, i: (i, p)),
         pl.BlockSpec((bt, VP), lambda p, i: (i, p))],
        [jax.ShapeDtypeStruct((QW, rows), F32), jax.ShapeDtypeStruct((rows, QW), F32),
         jax.ShapeDtypeStruct((rows, VW), F32)],
        [pltpu.VMEM((bt, HP), F32), pltpu.VMEM((bt, VP), F32)],
        (q, k, kt, v, do_t, lse, delta), comm)


def _dot_cast_w(a, w_ref):
    return jnp.dot(a, w_ref[...].astype(BF16), preferred_element_type=F32)


def _dot_cast_wt(a, w_ref):
    return lax.dot_general(a, w_ref[...].astype(BF16), (NT, ((), ())), preferred_element_type=F32)


LAT_W = 768
KPE_BLOCK = MLA_Q_RANK // LANES
KV_BLOCK = (MLA_Q_RANK + LANES) // MLA_KV_RANK


def mla_layer_fwd(tag, h, ln_g, w, tabs, comm=None):
    lat, hn = norm_mm(f"mla_in_{tag}", h, ln_g, w["w_in"], tn=LAT_W)
    qraw, qn = norm_mm(f"mla_q_{tag}", lat, w["q_a"], w["w_q"], tn=512, k_cols=MLA_Q_RANK, col_block=0)
    kvraw, kvn = norm_mm(f"mla_kv_{tag}", lat, w["kv_a"], w["w_kv"], tn=512, k_cols=MLA_KV_RANK, col_block=KV_BLOCK)
    q, k, kt, v, vt = qk_prep(f"mla_prep_{tag}", qraw, kvraw, lat, KPE_BLOCK, w["q_norm"], w["k_norm"], tabs)
    (o_t, lse), cres = attn_fwd(f"mla_attn_{tag}", q, k, vt, comm)
    out = res_mm(f"mla_out_{tag}", [(o_t, (VW, None), lambda i, j, kk: (0, i))],
                 lambda r: r[0][...].T.astype(BF16), VW, w["w_out"], h, tn=512, tm=_big_tile(h.shape[0]))
    return out, (h, hn, lat, qn, kvn, qraw, kvraw, q, k, kt, v, o_t, lse), cres


def mla_layer_bwd(tag, dh, saved, ln_g, w, tabs, comm=None):
    h, hn, lat, qn, kvn, qraw, kvraw, q, k, kt, v, o_t, lse = saved
    rows, d = h.shape
    tm = _row_tile(rows)

    def epi_set(acc, e_refs, o_refs, i, j):
        o_refs[0][...] = acc.astype(BF16)

    dw_out, = fused_mm(
        f"mla_dwout_{tag}", rows=VW, k=rows, n=d, tm=512, tn=d, tk=tm,
        a_ops=[(o_t, (512, tm), lambda i, j, kk: (i, kk))], pro=lambda a, o_, i: a[0][...].astype(BF16),
        w=dh, w_block=(tm, d), w_imap=lambda i, j, kk: (kk, 0), dot=_dot_cast_w, epi=epi_set,
        outs=[((VW, d), BF16, (512, d), lambda i, j, kk: (i, 0))])

    def epi_do(acc, e_refs, o_refs, i, j):
        o_refs[0][...] = acc.astype(BF16)
        prod = acc * e_refs[0][...]
        o_refs[1][...] = jnp.zeros((N_PAIRS, SUBLANES, tm), F32)
        for hd in range(MLA_HEADS):
            o_refs[1][hd // 2, hd % 2:hd % 2 + 1, :] = jnp.sum(prod[hd * MLA_V:(hd + 1) * MLA_V, :], axis=0,
                                                               keepdims=True)

    wo, wo_block, wo_imap = _lw(w["w_out"], (VW, d), lambda i, j, kk: (0, 0))
    do_t, delta = fused_mm(
        f"mla_do_{tag}", rows=VW, k=d, n=rows, tm=VW, tn=tm,
        a_ops=[(wo, wo_block, wo_imap)], pro=lambda a, o_, i: a[0][...],
        w=dh, w_block=(tm, d), w_imap=lambda i, j, kk: (j, 0), dot=_dot_cast_wt,
        e_ops=[(o_t, (VW, tm), lambda i, j, kk: (0, j))], epi=epi_do,
        outs=[((VW, rows), BF16, (VW, tm), lambda i, j, kk: (0, j)),
              ((N_PAIRS, SUBLANES, rows), F32, (N_PAIRS, SUBLANES, tm), lambda i, j, kk: (0, 0, j))])
    (dq_t, dk, dv), cres = attn_bwd(f"mla_dattn_{tag}", q, k, kt, v, do_t, lse, delta, comm)
    dqraw, dkvraw, dpe, dqg, dkg = qk_prep_bwd(f"mla_dprep_{tag}", dq_t, dk, dv, qraw, kvraw, lat, KPE_BLOCK,
                                               w["q_norm"], w["k_norm"], tabs)
    dw_q = simple_wgrad(f"mla_dwq_{tag}", qn, dqraw, t1=MLA_Q_RANK, tn=512)
    dqlat, dqa = rms_bwd_mm(
        f"mla_dqlat_{tag}", [(dqraw, (None, QW), lambda i, j, kk: (i, 0))], lambda r: r[0][...], QW,
        w["w_q"], (MLA_Q_RANK, QW), lambda i, j, kk: (0, 0), _dot_wt, lat, w["q_a"], None,
        h_cols=MLA_Q_RANK, h_col_block=0, add_dh=False)
    dw_kv = simple_wgrad(f"mla_dwkv_{tag}", kvn, dkvraw, t1=MLA_KV_RANK, tn=512)
    dkvlat, dkva = rms_bwd_mm(
        f"mla_dkvlat_{tag}", [(dkvraw, (None, QW + VW), lambda i, j, kk: (i, 0))], lambda r: r[0][...], QW + VW,
        w["w_kv"], (MLA_KV_RANK, QW + VW), lambda i, j, kk: (0, 0), _dot_wt, lat, w["kv_a"], None,
        h_cols=MLA_KV_RANK, h_col_block=KV_BLOCK, add_dh=False)
    dlat = jnp.concatenate([dqlat.astype(BF16), dpe.astype(BF16), dkvlat.astype(BF16)], axis=1)
    dw_in = simple_wgrad(f"mla_dwin_{tag}", hn, dlat, t1=512, tn=LAT_W)
    dh_in, dln = rms_bwd_mm(
        f"mla_dh_{tag}", [(dlat, (None, LAT_W), lambda i, j, kk: (i, 0))], lambda r: r[0][...], LAT_W,
        w["w_in"], (d, LAT_W), lambda i, j, kk: (0, 0), _dot_wt, h, ln_g, dh)
    grads = dict(w_in=dw_in, w_q=dw_q, w_kv=dw_kv, w_out=dw_out, q_a=dqa, kv_a=dkva, q_norm=dqg, k_norm=dkg, ln=dln)
    return dh_in, grads, cres


def loss_head(h, target):
    rows, d = h.shape
    nb = rows // CHUNK

    def body(h_ref, t_ref, l_ref, dh_ref):
        i = pl.program_id(0)
        err = jnp.where(i > 0, h_ref[...] - t_ref[...], 0.0)
        dh_ref[...] = err * (1.0 / d)
        _acc_out(l_ref, _rows8(err * err) * (0.5 / d), i == 0)

    return pl.pallas_call(
        body, name="loss_head", grid=(nb,),
        in_specs=[pl.BlockSpec((CHUNK, d), lambda i: (i, 0)),
                  pl.BlockSpec((CHUNK, d), lambda i: (jnp.maximum(i - 1, 0), 0))],
        out_specs=[pl.BlockSpec((SUBLANES, d), lambda i: (0, 0)), pl.BlockSpec((CHUNK, d), lambda i: (i, 0))],
        out_shape=[jax.ShapeDtypeStruct((SUBLANES, d), F32), jax.ShapeDtypeStruct((rows, d), F32)],
        compiler_params=_params(1),
    )(h, target)


def _adamw(w, g, m, v):
    m = ADAM_B1 * m + (1.0 - ADAM_B1) * g
    v = ADAM_B2 * v + (1.0 - ADAM_B2) * jnp.square(g)
    m_hat = m / (1.0 - ADAM_B1 ** ADAM_STEP)
    v_hat = v / (1.0 - ADAM_B2 ** ADAM_STEP)
    delta = -ADAM_LR * (m_hat / (jnp.sqrt(v_hat) + ADAM_EPS) + ADAM_WD * w)
    return delta, m, v


def reduce_adamw(name, recvs, w, m, v):
    nl, r, c = w.shape
    tr = 128 if r % 128 == 0 else r
    nr = r // tr

    def body(*refs):
        r_refs = refs[:nl]
        w_ref, m_ref, v_ref, g_ref, d_ref, mo_ref, vo_ref = refs[nl:]
        layer = pl.program_id(0)
        for l in range(nl):
            @pl.when(layer == l)
            def _(l=l):
                g = r_refs[l][0].astype(F32)
                for s in range(1, N_DEV):
                    g = g + r_refs[l][s].astype(F32)
                g_ref[...] = g
                d_ref[...], mo_ref[...], vo_ref[...] = _adamw(w_ref[...], g, m_ref[...], v_ref[...])

    def recv_spec(l):
        return pl.BlockSpec((N_DEV, tr, c),
                            lambda y, i: (0, jnp.where(y == l, i, jnp.where(y < l, 0, nr - 1)), 0))

    blk = pl.BlockSpec((None, tr, c), lambda y, i: (y, i, 0))
    return pl.pallas_call(
        body, name=name, grid=(nl, nr),
        in_specs=[recv_spec(l) for l in range(nl)] + [blk, blk, blk],
        out_specs=[blk] * 4, out_shape=[jax.ShapeDtypeStruct((nl, r, c), F32)] * 4,
        compiler_params=_params(2),
    )(*recvs, w, m, v)


def small_reduce(recv):
    def body(r_ref, o_ref):
        g = r_ref[0]
        for s in range(1, N_DEV):
            g = g + r_ref[s]
        o_ref[...] = g

    return pl.pallas_call(body, name="small_reduce", out_shape=jax.ShapeDtypeStruct(recv.shape[1:], F32))(recv)


def small_adamw(w, g, m, v):
    def body(w_ref, g_ref, m_ref, v_ref, d_ref, mo_ref, vo_ref):
        d_ref[...], mo_ref[...], vo_ref[...] = _adamw(w_ref[...], g_ref[...], m_ref[...], v_ref[...])

    return pl.pallas_call(body, name="small_adamw", out_shape=[jax.ShapeDtypeStruct(w.shape, F32)] * 3)(w, g, m, v)


def _pack(parts):
    flat, meta, off = [], [], 0
    for p in parts:
        n = int(np.prod(p.shape))
        flat.append(p.reshape(-1).astype(F32))
        meta.append((off, p.shape))
        off += n
    total = -(-off // (SUBLANES * LANES)) * (SUBLANES * LANES)
    flat.append(jnp.zeros((total - off,), F32))
    return jnp.concatenate(flat).reshape(total // LANES, LANES), meta


def _unpack(packed, meta):
    flat = packed.reshape(-1)
    return [flat[off:off + int(np.prod(shape))].reshape(shape) for off, shape in meta]


MESH = pl.DeviceIdType.MESH
N_PEERS = N_DEV - 1


def _me():
    return lax.axis_index("x"), lax.axis_index("y"), lax.axis_index("c")


def _peer(k):
    x, y, c = _me()
    return (1 - x if k & 4 else x, 1 - y if k & 2 else y, 1 - c if k & 1 else c)


def _dev_index(pos):
    return 4 * pos[0] + 2 * pos[1] + pos[2]


def make_comm(items):
    n = len(items)

    def part(ref, a, idx):
        rows = items[a][1]
        if rows == "all":
            return ref
        return ref.at[idx] if rows is None else ref.at[pl.ds(idx * rows, rows)]

    def part_shape(a):
        arr, rows = items[a]
        if rows == "all":
            return arr.shape
        return arr.shape[1:] if rows is None else (rows,) + arr.shape[1:]

    def run(phase, ins, outs, send_sems, recv_sems, local_sems):
        me = _dev_index(_me())
        for a in range(n):
            local = pltpu.make_async_copy(part(ins[a], a, me), outs[a].at[me], local_sems.at[a])
            if phase == "start":
                local.start()
            for k in range(1, N_DEV):
                peer = _peer(k)
                if phase == "start":
                    pltpu.make_async_remote_copy(
                        src_ref=part(ins[a], a, _dev_index(peer)), dst_ref=outs[a].at[me],
                        send_sem=send_sems.at[a, k - 1], recv_sem=recv_sems.at[a, k - 1],
                        device_id=peer, device_id_type=MESH).start()
                else:
                    cp = pltpu.make_async_remote_copy(
                        src_ref=part(ins[a], a, me), dst_ref=outs[a].at[_dev_index(peer)],
                        send_sem=send_sems.at[a, k - 1], recv_sem=recv_sems.at[a, k - 1],
                        device_id=peer, device_id_type=MESH)
                    cp.wait_recv()
                    cp.wait_send()
            if phase == "wait":
                local.wait()

    return dict(
        ins=[it[0] for it in items],
        outs=[jax.ShapeDtypeStruct((N_DEV,) + part_shape(a), items[a][0].dtype) for a in range(n)],
        sems=[pltpu.SemaphoreType.DMA((n, N_PEERS)), pltpu.SemaphoreType.DMA((n, N_PEERS)),
              pltpu.SemaphoreType.DMA((n,))],
        run=run)


ANY_SPEC = pl.BlockSpec(memory_space=pl.ANY)


def comm_call(name, comm):
    n, no = len(comm["ins"]), len(comm["outs"])

    def body(*refs):
        comm["run"]("start", refs[:n], refs[n:n + no], *refs[n + no:])
        comm["run"]("wait", refs[:n], refs[n:n + no], *refs[n + no:])

    return pl.pallas_call(
        body, name=name, in_specs=[ANY_SPEC] * n, out_specs=[ANY_SPEC] * no, out_shape=comm["outs"],
        scratch_shapes=comm["sems"])(*comm["ins"])


def with_comm(comm, body, grid, in_specs, out_specs, out_shape, scratch_shapes):
    if comm is None:
        return body, in_specs, out_specs, out_shape, scratch_shapes, [], len(out_shape)
    n_in, n_out, n_scr = len(in_specs), len(out_shape), len(scratch_shapes)
    ci, co = len(comm["ins"]), len(comm["outs"])

    def wrapped(*refs):
        ins, cins = refs[:n_in], refs[n_in:n_in + ci]
        outs = refs[n_in + ci:n_in + ci + n_out]
        couts = refs[n_in + ci + n_out:n_in + ci + n_out + co]
        rest = refs[n_in + ci + n_out + co:]
        scr, sems = rest[:n_scr], rest[n_scr:]
        first = functools.reduce(jnp.logical_and, [pl.program_id(a) == 0 for a in range(len(grid))])
        last = functools.reduce(jnp.logical_and, [pl.program_id(a) == grid[a] - 1 for a in range(len(grid))])

        @pl.when(first)
        def _():
            comm["run"]("start", cins, couts, *sems)

        body(*ins, *outs, *scr)

        @pl.when(last)
        def _():
            comm["run"]("wait", cins, couts, *sems)

    return (wrapped, list(in_specs) + [ANY_SPEC] * ci, list(out_specs) + [ANY_SPEC] * co,
            list(out_shape) + list(comm["outs"]), list(scratch_shapes) + list(comm["sems"]), list(comm["ins"]), n_out)


WEIGHTS = ['meta_tokens', 'ln_mix', 'ln_mlp', 'ssd_w_in', 'ssd_conv_w', 'ssd_conv_b', 'ssd_dt_bias', 'ssd_a_log',
           'ssd_d', 'ssd_norm', 'ssd_w_out', 'mla_w_in', 'mla_q_a_norm', 'mla_w_q_b', 'mla_kv_a_norm', 'mla_w_kv_b',
           'mla_q_norm', 'mla_k_norm', 'mla_w_out', 'mlp_w_up', 'mlp_w_down']
BIG = ['ssd_w_in', 'ssd_w_out', 'mla_w_in', 'mla_w_q_b', 'mla_w_kv_b', 'mla_w_out', 'mlp_w_up', 'mlp_w_down']
SMALL_SHARDED = ['meta_tokens', 'ssd_conv_w', 'mla_q_a_norm', 'mla_kv_a_norm']
SMALL_REPL = ['ln_mix', 'ln_mlp', 'ssd_conv_b', 'ssd_dt_bias', 'ssd_a_log', 'ssd_d', 'ssd_norm', 'mla_q_norm',
              'mla_k_norm']
SMALL = SMALL_REPL + SMALL_SHARDED
SSD_IN_PAD = 6272
SSD_IN_TN = 896
MLA_IN = MLA_Q_RANK + MLA_KV_RANK + MLA_ROPE


def _pad_last(v, n):
    return jnp.pad(v, [(0, 0)] * (v.ndim - 1) + [(0, n - v.shape[-1])])


SSD_BIG = ['ssd_w_in', 'ssd_w_out']
MLA_BIG = ['mla_w_in', 'mla_w_q_b', 'mla_w_kv_b', 'mla_w_out']
MLP_BIG = ['mlp_w_up', 'mlp_w_down']


def _layer_big(i):
    return [(n, i // 2) for n in (SSD_BIG if i % 2 == 0 else MLA_BIG)] + [(n, i) for n in MLP_BIG]


def _layer_weights(i, gw, W, full):
    j = i // 2
    d = W['ln_mix'].shape[-1]
    if i % 2 == 0:
        wi = gw['ssd_w_in'].transpose(1, 0, 2).reshape(d, -1)
        par = jnp.concatenate([_pad_last(W[n][j][None], LANES) for n in ('ssd_dt_bias', 'ssd_a_log', 'ssd_d')]
                              + [jnp.zeros((SUBLANES - 3, LANES), F32)])
        mix = dict(w_in=_pad_last(wi, SSD_IN_PAD), conv_w=full['ssd_conv_w'][j], conv_b=W['ssd_conv_b'][j][None],
                   par=par, norm=W['ssd_norm'][j][None], w_out=gw['ssd_w_out'].reshape(SSD_D_INNER, d))
    else:
        wi = gw['mla_w_in'].reshape(d, MLA_IN)
        kpe = jnp.pad(wi[:, MLA_Q_RANK + MLA_KV_RANK:], ((0, 0), (MLA_NOPE, LANES - MLA_QK)))
        wq = gw['mla_w_q_b'].transpose(1, 0, 2).reshape(MLA_Q_RANK, MLA_HEADS, MLA_QK)
        wkv = gw['mla_w_kv_b'].transpose(1, 0, 2).reshape(MLA_KV_RANK, MLA_HEADS, MLA_NOPE + MLA_V)
        mix = dict(
            w_in=jnp.concatenate([wi[:, :MLA_Q_RANK], kpe, wi[:, MLA_Q_RANK:MLA_Q_RANK + MLA_KV_RANK]], axis=1),
            w_q=_pad_last(wq, LANES).reshape(MLA_Q_RANK, QW),
            w_kv=jnp.concatenate([_pad_last(wkv[..., :MLA_NOPE], LANES).reshape(MLA_KV_RANK, QW),
                                  wkv[..., MLA_NOPE:].reshape(MLA_KV_RANK, VW)], axis=1),
            w_out=gw['mla_w_out'].reshape(VW, d), q_a=full['mla_q_a_norm'][j][None],
            kv_a=full['mla_kv_a_norm'][j][None],
            q_norm=_pad_last(W['mla_q_norm'][j][None], LANES), k_norm=_pad_last(W['mla_k_norm'][j][None], LANES))
    return mix, gw['mlp_w_up'], gw['mlp_w_down'].reshape(-1, d)


def _step(x, target, W, M, V):
    d = x.shape[-1]
    me = _dev_index(_me())
    depth = W['ln_mix'].shape[0]

    def gather_items(i):
        return [(W[n][l].astype(BF16), "all") for n, l in _layer_big(i)]

    small_pack, small_meta = _pack([W[n] for n in SMALL_SHARDED])
    got = comm_call("gather_0", make_comm(gather_items(0) + [(small_pack, "all")]))
    per_dev = [_unpack(got[-1][s], small_meta) for s in range(N_DEV)]
    full = {n: jnp.concatenate([per_dev[s][i] for s in range(N_DEV)], axis=-1) for i, n in enumerate(SMALL_SHARDED)}
    weights = {0: _layer_weights(0, {n: a for (n, _), a in zip(_layer_big(0), got)}, W, full)}

    h = jnp.concatenate([jnp.zeros((PAD, d), F32), full['meta_tokens'], x], axis=0)
    rows = h.shape[0]
    tabs = rope_tables(rows)
    saved = []
    for i in range(depth):
        comm = make_comm(gather_items(i + 1)) if i + 1 < depth else None
        mix, up, down = weights[i]
        if i % 2 == 0:
            h, s_mix, got = ssd_layer_fwd(f"{i}", h, W['ln_mix'][i][None], mix, comm)
        else:
            h, s_mix, got = mla_layer_fwd(f"{i}", h, W['ln_mix'][i][None], mix, tabs, comm)
        if comm is not None:
            weights[i + 1] = _layer_weights(i + 1, {n: a for (n, _), a in zip(_layer_big(i + 1), got)}, W, full)
        h, s_mlp = mlp_fwd(f"{i}", h, W['ln_mlp'][i][None], up, down)
        saved.append((s_mix, s_mlp))
    loss_part, dh = loss_head(h, target)
    loss = lax.psum(jnp.sum(loss_part), ("x", "y", "c"))

    recv = {}
    pending = None
    small = {n: [None] * W[n].shape[0] for n in SMALL if n != 'meta_tokens'}
    for i in reversed(range(depth)):
        j = i // 2
        s_mix, s_mlp = saved[i]
        mix, up, down = weights[i]
        dh, dw_up, dw_down, dg = mlp_bwd(f"{i}", dh, s_mlp, W['ln_mlp'][i][None], up, down)
        small['ln_mlp'][i] = dg.sum(0)
        comm = make_comm([it for _, it in pending]) if pending else None
        sends = []
        if i % 2 == 0:
            dh, g, got = ssd_layer_bwd(f"{i}", dh, s_mix, W['ln_mix'][i][None], mix, comm)
            n_in = W['ssd_w_in'].shape[-1]
            sends.append((g['w_in'][:, :N_DEV * n_in].reshape(d, N_DEV, n_in).transpose(1, 0, 2), None))
            sends.append((g['w_out'], SSD_D_INNER // N_DEV))
            small['ssd_conv_w'][j] = g['conv_w'].reshape(SSD_CONV, SUBLANES, -1).sum(1)
            small['ssd_conv_b'][j] = g['conv_b'].sum(0)
            small['ssd_dt_bias'][j] = g['par'][0, :SSD_HEADS]
            small['ssd_a_log'][j] = g['par'][1, :SSD_HEADS]
            small['ssd_d'][j] = g['par'][2, :SSD_HEADS]
            small['ssd_norm'][j] = g['norm'].sum(0)
        else:
            dh, g, got = mla_layer_bwd(f"{i}", dh, s_mix, W['ln_mix'][i][None], mix, tabs, comm)
            gi = g['w_in']
            gi = jnp.concatenate([gi[:, :MLA_Q_RANK], gi[:, MLA_Q_RANK + LANES:],
                                  gi[:, MLA_Q_RANK + MLA_NOPE:MLA_Q_RANK + MLA_QK]], axis=1)
            sends.append((gi, d // N_DEV))
            gq = g['w_q'].reshape(MLA_Q_RANK, MLA_HEADS, LANES)[..., :MLA_QK]
            sends.append((gq.reshape(MLA_Q_RANK, N_DEV, -1).transpose(1, 0, 2), None))
            gkv = jnp.concatenate([g['w_kv'][:, :QW].reshape(MLA_KV_RANK, MLA_HEADS, LANES)[..., :MLA_NOPE],
                                   g['w_kv'][:, QW:].reshape(MLA_KV_RANK, MLA_HEADS, MLA_V)], axis=-1)
            sends.append((gkv.reshape(MLA_KV_RANK, N_DEV, -1).transpose(1, 0, 2), None))
            sends.append((g['w_out'], VW // N_DEV))
            small['mla_q_a_norm'][j] = g['q_a'].sum(0)
            small['mla_kv_a_norm'][j] = g['kv_a'].sum(0)
            small['mla_q_norm'][j] = g['q_norm'].sum(0)[:MLA_QK]
            small['mla_k_norm'][j] = g['k_norm'].sum(0)[:MLA_QK]
        small['ln_mix'][i] = g['ln'].sum(0)
        if comm is not None:
            recv.update({key: a for (key, _), a in zip(pending, got)})
        sends += [(dw_up, None), (dw_down, down.shape[0] // N_DEV)]
        pending = list(zip(_layer_big(i), sends))
    grad_x = dh[CHUNK:]
    small_full = {n: jnp.stack(v) for n, v in small.items()}
    small_full['meta_tokens'] = dh[PAD:CHUNK]

    gpack, gmeta = _pack([small_full[n] for n in SMALL])
    got = comm_call("exchange_0", make_comm([it for _, it in pending] + [(gpack, "all")]))
    recv.update({key: a for (key, _), a in zip(pending, got)})
    res = {}
    for n in BIG:
        res[n] = reduce_adamw(f"adamw_{n}", [recv[(n, l)] for l in range(W[n].shape[0])], W[n], M[n], V[n])
    gsum = dict(zip(SMALL, _unpack(small_reduce(got[-1]), gmeta)))
    for n in SMALL_SHARDED:
        wl = W[n].shape[-1]
        gsum[n] = lax.dynamic_slice_in_dim(gsum[n], me * wl, wl, axis=gsum[n].ndim - 1)
    wp, wmeta = _pack([W[n] for n in SMALL])
    gp, _ = _pack([gsum[n] for n in SMALL])
    mp, _ = _pack([M[n] for n in SMALL])
    vp, _ = _pack([V[n] for n in SMALL])
    upd = [_unpack(o, wmeta) for o in small_adamw(wp, gp, mp, vp)]
    for a, n in enumerate(SMALL):
        res[n] = [gsum[n], upd[0][a], upd[1][a], upd[2][a]]
    return (loss, grad_x[None]) + tuple(res[n][q] for q in range(4) for n in WEIGHTS)


def kernel(x, meta_tokens, ln_mix, ln_mlp, ssd_w_in, ssd_conv_w, ssd_conv_b, ssd_dt_bias, ssd_a_log, ssd_d, ssd_norm, ssd_w_out, mla_w_in, mla_q_a_norm, mla_w_q_b, mla_kv_a_norm, mla_w_kv_b, mla_q_norm, mla_k_norm, mla_w_out, mlp_w_up, mlp_w_down, loss_target, m_meta_tokens, m_ln_mix, m_ln_mlp, m_ssd_w_in, m_ssd_conv_w, m_ssd_conv_b, m_ssd_dt_bias, m_ssd_a_log, m_ssd_d, m_ssd_norm, m_ssd_w_out, m_mla_w_in, m_mla_q_a_norm, m_mla_w_q_b, m_mla_kv_a_norm, m_mla_w_kv_b, m_mla_q_norm, m_mla_k_norm, m_mla_w_out, m_mlp_w_up, m_mlp_w_down, v_meta_tokens, v_ln_mix, v_ln_mlp, v_ssd_w_in, v_ssd_conv_w, v_ssd_conv_b, v_ssd_dt_bias, v_ssd_a_log, v_ssd_d, v_ssd_norm, v_ssd_w_out, v_mla_w_in, v_mla_q_a_norm, v_mla_w_q_b, v_mla_kv_a_norm, v_mla_w_kv_b, v_mla_q_norm, v_mla_k_norm, v_mla_w_out, v_mlp_w_up, v_mlp_w_down):
    given = dict(locals())
    W = {n: given[n] for n in WEIGHTS}
    M = {n: given["m_" + n] for n in WEIGHTS}
    V = {n: given["v_" + n] for n in WEIGHTS}
    return _step(x[0], loss_target[0], W, M, V)
```

```python
import functools

import jax
import jax.numpy as jnp
import numpy as np
from jax import lax
from jax.experimental import pallas as pl
from jax.experimental.pallas import tpu as pltpu

F32 = jnp.float32
BF16 = jnp.bfloat16

EPS = 1e-6
N_META = 16
CHUNK = 128
PAD = CHUNK - N_META
SSD_HEAD_DIM = 64
SSD_HEADS = 32
SSD_GROUPS = 8
SSD_HPG = 4
SSD_STATE = 128
SSD_D_INNER = 2048
SSD_CONV = 4
MLA_HEADS = 16
MLA_NOPE = 64
MLA_ROPE = 32
MLA_V = 64
MLA_QK = 96
MLA_Q_RANK = 384
MLA_KV_RANK = 256
ROPE_THETA = 10000.0
LANES = 128
SUBLANES = 8
N_DEV = 8
VMEM_LIMIT = 56 * 1024 * 1024

ADAM_LR = 0.001
ADAM_B1 = 0.9
ADAM_B2 = 0.999
ADAM_EPS = 1e-08
ADAM_WD = 0.01
ADAM_STEP = 10

NEG = -1e30


def _row_tile(rows):
    return 384 if (rows % 384 == 0 and rows > 384) else 128


def _big_tile(rows):
    return 1408 if rows % 1408 == 0 else _row_tile(rows)


def _params(n_axes, vmem=VMEM_LIMIT):
    return pltpu.CompilerParams(dimension_semantics=("arbitrary",) * n_axes, vmem_limit_bytes=vmem)


def _dot(a, b, dims):
    return lax.dot_general(a.astype(BF16), b.astype(BF16), (dims, ((), ())), preferred_element_type=F32)


NN = ((1,), (0,))
NT = ((1,), (1,))
TN = ((0,), (0,))


@jax.custom_vjp
def bdot_nn(a, b):
    return _dot(a, b, NN)


@jax.custom_vjp
def bdot_nt(a, b):
    return _dot(a, b, NT)


@jax.custom_vjp
def bdot_tn(a, b):
    return _dot(a, b, TN)


bdot_nn.defvjp(lambda a, b: (_dot(a, b, NN), (a, b)),
               lambda r, g: (_dot(g, r[1], NT), _dot(r[0], g, TN)))
bdot_nt.defvjp(lambda a, b: (_dot(a, b, NT), (a, b)),
               lambda r, g: (_dot(g, r[1], NN), _dot(g, r[0], TN)))
bdot_tn.defvjp(lambda a, b: (_dot(a, b, TN), (a, b)),
               lambda r, g: (_dot(r[1], g, NT), _dot(r[0], g, NN)))


def _rows8(v):
    r, n = v.shape
    return v.reshape(r // SUBLANES, SUBLANES, n).sum(axis=0)


def _row_mask(i, tm):
    return (i * tm + lax.broadcasted_iota(jnp.int32, (tm, 1), 0)) >= PAD


def fused_mm(name, *, rows, k, n, tm, tn, tk=None, a_ops, pro, w, w_block, w_imap, dot, e_ops=(), epi, outs):
    tk = tk or k
    ni, nj, nk = rows // tm, n // tn, k // tk
    assert rows % tm == 0 and n % tn == 0 and k % tk == 0
    assert nk == 1 or nj == 1
    cache = nk == 1 and nj > 1
    na, ne, no = len(a_ops), len(e_ops), len(outs)

    def body(*refs):
        a_refs = refs[:na]
        w_ref = refs[na]
        e_refs = refs[na + 1:na + 1 + ne]
        o_refs = refs[na + 1 + ne:na + 1 + ne + no]
        scr = refs[na + 1 + ne + no:]
        i, j, kk = pl.program_id(0), pl.program_id(1), pl.program_id(2)
        if cache:
            a_scr = scr[0]

            @pl.when(j == 0)
            def _():
                a_scr[...] = pro(a_refs, o_refs, i)

            a = a_scr[...]
        else:
            a = pro(a_refs, o_refs, i)
        part = dot(a, w_ref)
        if nk == 1:
            epi(part, e_refs, o_refs, i, j)
        else:
            acc_ref = scr[0]

            @pl.when(kk == 0)
            def _():
                acc_ref[...] = part

            @pl.when(kk > 0)
            def _():
                acc_ref[...] += part

            @pl.when(kk == nk - 1)
            def _():
                epi(acc_ref[...], e_refs, o_refs, i, j)

    scratch = []
    if cache:
        scratch.append(pltpu.VMEM((tm, k), BF16))
    if nk > 1:
        scratch.append(pltpu.VMEM((tm, tn), F32))
    in_specs = [pl.BlockSpec(b, m) for (_, b, m) in a_ops]
    in_specs.append(pl.BlockSpec(w_block, w_imap))
    in_specs += [pl.BlockSpec(b, m) for (_, b, m) in e_ops]
    return pl.pallas_call(
        body, name=name, grid=(ni, nj, nk),
        in_specs=in_specs,
        out_specs=[pl.BlockSpec(b, m) for (_, _, b, m) in outs],
        out_shape=[jax.ShapeDtypeStruct(s, d) for (s, d, _, _) in outs],
        scratch_shapes=scratch,
        compiler_params=_params(3),
    )(*[a for (a, _, _) in a_ops], w, *[e for (e, _, _) in e_ops])


def _lw(w, block, imap):
    if isinstance(w, tuple):
        arr, layer = w
        return arr, (None,) + block, (lambda i, j, kk: (layer,) + imap(i, j, kk))
    return w, block, imap


def _dot_w(a, w_ref):
    return jnp.dot(a, w_ref[...], preferred_element_type=F32)


def _dot_wt(a, w_ref):
    return lax.dot_general(a, w_ref[...], (NT, ((), ())), preferred_element_type=F32)


def _rms_pro(h, g):
    r = lax.rsqrt(jnp.mean(h * h, axis=-1, keepdims=True) + EPS)
    return h * r * g


def _rms_bwd(dyn, h, g):
    r = lax.rsqrt(jnp.mean(h * h, axis=-1, keepdims=True) + EPS)
    xh = h * r
    t = dyn * g
    dh = r * (t - xh * jnp.mean(t * xh, axis=-1, keepdims=True))
    return dh, _rows8(dyn * xh)


def _acc_out(ref, val, first):
    @pl.when(first)
    def _():
        ref[...] = val

    @pl.when(jnp.logical_not(first))
    def _():
        ref[...] += val


def norm_mm(name, h, g, w, *, tn, k_cols=None, col_block=0, w_stacked=False):
    rows = h.shape[0]
    k = k_cols or h.shape[1]
    wshape = (w[0].shape[1:] if isinstance(w, tuple) else w.shape)
    n = wshape[0] * wshape[2] if w_stacked else wshape[1]
    tm = _big_tile(rows)

    def pro(a_refs, o_refs, i):
        hn = _rms_pro(a_refs[0][...], a_refs[1][...]).astype(BF16)
        o_refs[1][...] = hn
        return hn

    def epi(acc, e_refs, o_refs, i, j):
        o_refs[0][...] = acc

    if w_stacked:
        w_block, w_imap = (None, k, tn), (lambda i, j, kk: (j, 0, 0))
    else:
        w_block, w_imap = (k, tn), (lambda i, j, kk: (0, j))
    w, w_block, w_imap = _lw(w, w_block, w_imap)
    return fused_mm(
        name, rows=rows, k=k, n=n, tm=tm, tn=tn,
        a_ops=[(h, (tm, k), lambda i, j, kk: (i, col_block)), (g, (1, k), lambda i, j, kk: (0, 0))],
        pro=pro, w=w, w_block=w_block, w_imap=w_imap, dot=_dot_w, epi=epi,
        outs=[((rows, n), F32, (tm, tn), lambda i, j, kk: (i, j)),
              ((rows, k), BF16, (tm, k), lambda i, j, kk: (i, 0))])


def res_mm(name, a_ops, pro, k, w, res, *, tn, save_dtype=None, tm=None, tk=None):
    rows, n = res.shape
    tm = tm or _row_tile(rows)
    assert tk is None or save_dtype is None

    def pro2(a_refs, o_refs, i):
        a = pro(a_refs)
        if save_dtype is not None:
            o_refs[1][...] = a
        return a

    def epi(acc, e_refs, o_refs, i, j):
        o_refs[0][...] = e_refs[0][...] + acc

    outs = [((rows, n), F32, (tm, tn), lambda i, j, kk: (i, j))]
    if save_dtype is not None:
        outs.append(((rows, k), save_dtype, (tm, k), lambda i, j, kk: (i, 0)))
    w, w_block, w_imap = _lw(w, (tk or k, tn), lambda i, j, kk: (kk, j))
    out = fused_mm(
        name, rows=rows, k=k, n=n, tm=tm, tn=tn, tk=tk,
        a_ops=[(a, tuple(tm if x is None else x for x in b), m) for (a, b, m) in a_ops],
        pro=pro2, w=w, w_block=w_block, w_imap=w_imap, dot=_dot_w,
        e_ops=[(res, (tm, tn), lambda i, j, kk: (i, j))], epi=epi, outs=outs)
    return out if save_dtype is not None else out[0]


def wgrad_mm(name, a_ops, pro_a, g_ops, pro_g, *, rows, k1, n, t1, tn, out_shape=None, out_block=None, out_imap=None):
    tt = _row_tile(rows)
    n1, n2, nt = k1 // t1, n // tn, rows // tt
    assert k1 % t1 == 0 and n % tn == 0
    na = len(a_ops)

    def body(*refs):
        a_refs = refs[:na]
        g_refs = refs[na:-2]
        o_ref, acc = refs[-2:]
        t = pl.program_id(2)
        a = pro_a(a_refs).astype(BF16)
        g = pro_g(g_refs).astype(BF16)
        _acc_out(acc, lax.dot_general(a, g, (TN, ((), ())), preferred_element_type=F32), t == 0)

        @pl.when(t == nt - 1)
        def _():
            if len(o_ref.shape) == 3:
                ws = o_ref.shape[2]
                for q in range(o_ref.shape[0]):
                    o_ref[q] = acc[:, q * ws:(q + 1) * ws].astype(BF16)
            else:
                o_ref[...] = acc[...].astype(BF16)

    return pl.pallas_call(
        body, name=name, grid=(n1, n2, nt),
        in_specs=[pl.BlockSpec(b, m) for (_, b, m) in list(a_ops) + list(g_ops)],
        out_specs=pl.BlockSpec(out_block or (t1, tn), out_imap or (lambda a, b, t: (a, b))),
        out_shape=jax.ShapeDtypeStruct(out_shape or (k1, n), BF16),
        scratch_shapes=[pltpu.VMEM((t1, tn), F32)],
        compiler_params=_params(3),
    )(*[a for (a, _, _) in list(a_ops) + list(g_ops)])


def simple_wgrad(name, a, g, *, a_cols=None, a_col_block=0, t1=None, tn=None, **kw):
    rows = a.shape[0]
    k1 = a_cols or a.shape[1]
    n = g.shape[1]
    tt = _row_tile(rows)
    t1 = t1 or min(k1, 512)
    tn = tn or min(n, 1024)
    return wgrad_mm(
        name,
        [(a, (tt, t1), lambda x, y, t: (t, x + a_col_block * (k1 // t1)))], lambda r: r[0][...],
        [(g, (tt, tn), lambda x, y, t: (t, y))], lambda r: r[0][...],
        rows=rows, k1=k1, n=n, t1=t1, tn=tn, **kw)


def rms_bwd_mm(name, dz_ops, pro, k, w, w_block, w_imap, dot, h, g, dh, *, tk=None, h_cols=None, h_col_block=0,
               add_dh=True, tm=None):
    rows = h.shape[0]
    n = h_cols or h.shape[1]
    tm = tm or _big_tile(rows)
    ni = rows // tm
    w, w_block, w_imap = _lw(w, w_block, w_imap)

    def epi(acc, e_refs, o_refs, i, j):
        d, dg = _rms_bwd(acc, e_refs[0][...], e_refs[1][...])
        if add_dh:
            d = d + e_refs[2][...]
        o_refs[0][...] = jnp.where(_row_mask(i, tm), d, 0.0)
        _acc_out(o_refs[1], dg, i == 0)

    e_ops = [(h, (tm, n), lambda i, j, kk: (i, h_col_block)), (g, (1, n), lambda i, j, kk: (0, 0))]
    if add_dh:
        e_ops.append((dh, (tm, n), lambda i, j, kk: (i, 0)))
    return fused_mm(
        name, rows=rows, k=k, n=n, tm=tm, tn=n, tk=tk,
        a_ops=[(a, tuple(tm if x is None else x for x in b), m) for (a, b, m) in dz_ops],
        pro=lambda a_refs, o_refs, i: pro(a_refs), w=w, w_block=w_block, w_imap=w_imap, dot=dot,
        e_ops=e_ops, epi=epi,
        outs=[((rows, n), F32, (tm, n), lambda i, j, kk: (i, 0)),
              ((SUBLANES, n), F32, (SUBLANES, n), lambda i, j, kk: (0, 0))])


def _relu2(u):
    r = jnp.maximum(u, 0.0)
    return r * r


def mlp_fwd(tag, h, g, w_up_st, w_down):
    d_ff = w_down.shape[0]
    u, hn = norm_mm(f"mlp_up_{tag}", h, g, w_up_st, tn=w_up_st.shape[2], w_stacked=True)
    out = res_mm(f"mlp_down_{tag}", [(u, (None, 512), lambda i, j, kk: (i, kk))],
                 lambda r: _relu2(r[0][...]).astype(BF16), d_ff, w_down, h, tn=h.shape[1],
                 tm=_big_tile(h.shape[0]), tk=512)
    return out, (h, hn, u)


def mlp_bwd(tag, dh, saved, g, w_up_st, w_down):
    h, hn, u = saved
    rows, d = h.shape
    d_ff = w_down.shape[0]
    ts = w_up_st.shape[2]
    tm = _big_tile(rows)
    tt = _row_tile(rows)
    wd, wd_block, wd_imap = _lw(w_down, (512, d), lambda i, j, kk: (j, 0))

    def epi_du(acc, e_refs, o_refs, i, j):
        o_refs[0][...] = (acc * (2.0 * jnp.maximum(e_refs[0][...], 0.0))).astype(BF16)

    du, = fused_mm(
        f"mlp_du_{tag}", rows=rows, k=d, n=d_ff, tm=tm, tn=512,
        a_ops=[(dh, (tm, d), lambda i, j, kk: (i, 0))], pro=lambda a, o, i: a[0][...].astype(BF16),
        w=wd, w_block=wd_block, w_imap=wd_imap, dot=_dot_wt,
        e_ops=[(u, (tm, 512), lambda i, j, kk: (i, j))], epi=epi_du,
        outs=[((rows, d_ff), BF16, (tm, 512), lambda i, j, kk: (i, j))])
    half = d_ff // 2
    dw_down = wgrad_mm(
        f"mlp_dwdown_{tag}",
        [(u, (tt, half), lambda a, b, t: (t, a))], lambda r: _relu2(r[0][...]),
        [(dh, (tt, d), lambda a, b, t: (t, 0))], lambda r: r[0][...],
        rows=rows, k1=d_ff, n=d, t1=half, tn=d)
    dw_up = simple_wgrad(f"mlp_dwup_{tag}", hn, du, t1=d, tn=half, out_shape=(N_DEV, d, ts),
                         out_block=(half // ts, d, ts), out_imap=lambda a, b, t: (b, 0, 0))
    dh_in, dg = rms_bwd_mm(
        f"mlp_dh_{tag}", [(du, (None, ts), lambda i, j, kk: (i, kk))], lambda r: r[0][...], d_ff,
        w_up_st, (None, d, ts), lambda i, j, kk: (kk, 0, 0), _dot_wt, h, g, dh, tk=ts)
    return dh_in, dw_up, dw_down, dg


CONV_HALO = SUBLANES
CONV_TC = 512


def _silu(x):
    return x * jax.nn.sigmoid(x)


def _conv_pre(ext_ref, w, b, tm):
    pre = b
    for k in range(SSD_CONV):
        pre = pre + w[k:k + 1, :] * ext_ref[pl.ds(CONV_HALO - (SSD_CONV - 1) + k, tm), :]
    return pre


def conv_fwd(name, zx, col0, width, conv_w, conv_b):
    rows = zx.shape[0]
    tm = _row_tile(rows)
    cb0 = col0 // CONV_TC
    hb = tm // CONV_HALO

    def body(u_ref, halo_ref, w_ref, b_ref, o_ref, ext):
        i = pl.program_id(1)
        ext[pl.ds(0, CONV_HALO), :] = jnp.where(i > 0, halo_ref[...], 0.0)
        ext[pl.ds(CONV_HALO, tm), :] = u_ref[...]
        pre = _conv_pre(ext, w_ref[...], b_ref[...], tm)
        o_ref[...] = jnp.where(_row_mask(i, tm), _silu(pre), 0.0)

    return pl.pallas_call(
        body, name=name, grid=(width // CONV_TC, rows // tm),
        in_specs=[pl.BlockSpec((tm, CONV_TC), lambda j, i: (i, cb0 + j)),
                  pl.BlockSpec((CONV_HALO, CONV_TC), lambda j, i: (jnp.maximum(i * hb - 1, 0), cb0 + j)),
                  pl.BlockSpec((SSD_CONV, CONV_TC), lambda j, i: (0, j)),
                  pl.BlockSpec((1, CONV_TC), lambda j, i: (0, j))],
        out_specs=pl.BlockSpec((tm, CONV_TC), lambda j, i: (i, j)),
        out_shape=jax.ShapeDtypeStruct((rows, width), F32),
        scratch_shapes=[pltpu.VMEM((tm + CONV_HALO, CONV_TC), F32)],
        compiler_params=_params(2),
    )(zx, zx, conv_w, conv_b)


def conv_bwd(name, dact, zx, col0, conv_w, conv_b):
    rows, width = dact.shape
    tm = _row_tile(rows)
    ni = rows // tm
    cb0 = col0 // CONV_TC
    hb = tm // CONV_HALO

    def body(d_ref, u_ref, halo_ref, w_ref, b_ref, du_ref, dw_ref, db_ref, ext, dext):
        s = pl.program_id(1)
        i = ni - 1 - s
        w = w_ref[...]
        ext[pl.ds(0, CONV_HALO), :] = jnp.where(i > 0, halo_ref[...], 0.0)
        ext[pl.ds(CONV_HALO, tm), :] = u_ref[...]
        pre = _conv_pre(ext, w, b_ref[...], tm)
        sg = jax.nn.sigmoid(pre)
        dpre = jnp.where(_row_mask(i, tm), d_ref[...] * (sg * (1.0 + pre * (1.0 - sg))), 0.0)

        @pl.when(s == 0)
        def _():
            dext[pl.ds(tm, CONV_HALO), :] = jnp.zeros((CONV_HALO, CONV_TC), F32)

        dext[pl.ds(0, tm), :] = dpre
        du = jnp.zeros((tm, CONV_TC), F32)
        for k in range(SSD_CONV):
            du = du + w[k:k + 1, :] * dext[pl.ds(SSD_CONV - 1 - k, tm), :]
        du_ref[...] = du.astype(du_ref.dtype)
        _acc_out(db_ref, _rows8(dpre), s == 0)
        for k in range(SSD_CONV):
            uk = ext[pl.ds(CONV_HALO - (SSD_CONV - 1) + k, tm), :]
            _acc_out(dw_ref.at[pl.ds(k * SUBLANES, SUBLANES), :], _rows8(dpre * uk), s == 0)
        dext[pl.ds(tm, CONV_HALO), :] = dpre[0:CONV_HALO, :]

    return pl.pallas_call(
        body, name=name, grid=(width // CONV_TC, ni),
        in_specs=[pl.BlockSpec((tm, CONV_TC), lambda j, s: (ni - 1 - s, j)),
                  pl.BlockSpec((tm, CONV_TC), lambda j, s: (ni - 1 - s, cb0 + j)),
                  pl.BlockSpec((CONV_HALO, CONV_TC), lambda j, s: (jnp.maximum((ni - 1 - s) * hb - 1, 0), cb0 + j)),
                  pl.BlockSpec((SSD_CONV, CONV_TC), lambda j, s: (0, j)),
                  pl.BlockSpec((1, CONV_TC), lambda j, s: (0, j))],
        out_specs=[pl.BlockSpec((tm, CONV_TC), lambda j, s: (ni - 1 - s, j)),
                   pl.BlockSpec((SSD_CONV * SUBLANES, CONV_TC), lambda j, s: (0, j)),
                   pl.BlockSpec((SUBLANES, CONV_TC), lambda j, s: (0, j))],
        out_shape=[jax.ShapeDtypeStruct((rows, width), BF16),
                   jax.ShapeDtypeStruct((SSD_CONV * SUBLANES, width), F32),
                   jax.ShapeDtypeStruct((SUBLANES, width), F32)],
        scratch_shapes=[pltpu.VMEM((tm + CONV_HALO, CONV_TC), F32), pltpu.VMEM((tm + CONV_HALO, CONV_TC), F32)],
        compiler_params=_params(2),
    )(dact, zx, zx, conv_w, conv_b)


@functools.partial(jax.custom_vjp, nondiff_argnums=(1,))
def _sub_row(x, h):
    return x[h:h + 1, :]


_sub_row.defvjp(
    lambda x, h: (x[h:h + 1, :], None),
    lambda h, _, g: (jnp.where(lax.broadcasted_iota(jnp.int32, (LANES, 1), 0) == h, g, 0.0),))


def _splitter(axis, size, count):
    def blocks(x):
        return tuple(lax.slice_in_dim(x, q * size, (q + 1) * size, axis=axis) for q in range(count))

    split = jax.custom_vjp(blocks)
    split.defvjp(lambda x: (blocks(x), None), lambda _, gs: (jnp.concatenate(gs, axis=axis),))
    return split


def _split3(x):
    hi = x.astype(BF16)
    r = x - hi.astype(F32)
    mid = r.astype(BF16)
    return hi, mid, (r - mid.astype(F32)).astype(BF16)


def _expand_impl(x, e):
    return sum(jnp.dot(t, e, preferred_element_type=F32) for t in _split3(x))


@jax.custom_vjp
def _expand(x, e):
    return _expand_impl(x, e)


def _expand_bwd(e, g):
    hi, mid, _ = _split3(g)
    dx = sum(lax.dot_general(t, e, (NT, ((), ())), preferred_element_type=F32) for t in (hi, mid))
    return dx, jnp.zeros_like(e)


_expand.defvjp(lambda x, e: (_expand_impl(x, e), e), _expand_bwd)

HEAD_PAIR = 2 * SSD_HEAD_DIM
GROUP_W = SSD_HPG * SSD_HEAD_DIM


def _ssd_chunk(xs, bm, cm, dtraw, prev, par, c, tri, e64, e128):
    li = lax.broadcasted_iota(jnp.int32, (CHUNK, CHUNK), 0)
    si = lax.broadcasted_iota(jnp.int32, (CHUNK, CHUNK), 1)
    causal = li >= si
    first_head = lax.broadcasted_iota(jnp.int32, (1, HEAD_PAIR), 1) < SSD_HEAD_DIM
    dt = jnp.where(_row_mask(c, CHUNK), jax.nn.softplus(dtraw + par[0:1, :]), 0.0)
    a = -jnp.exp(par[1:2, :])
    acs = jnp.dot(tri, dt * a, precision=lax.Precision.HIGHEST, preferred_element_type=F32)
    acs_t = acs.T
    last = acs[CHUNK - 1:CHUNK, :]
    misc = jnp.concatenate([jnp.exp(last), par[2:3, :], jnp.zeros((SUBLANES - 2, LANES), F32)], axis=0)
    wide = _expand(jnp.concatenate([dt, dt * jnp.exp(last - acs), jnp.exp(acs)], axis=0), e64)
    dt_w, dtend_w, start_w = _splitter(0, CHUNK, 3)(wide)
    misc_w = _expand(misc, e64)
    col_w = _splitter(1, CHUNK, SSD_HEADS)(_expand(acs, e128))
    groups = _splitter(1, GROUP_W, SSD_GROUPS)
    xs_g, prev_g, start_g = groups(xs), groups(prev), groups(start_w)
    xdt_p = _splitter(1, HEAD_PAIR, SSD_HEADS // 2)(xs * dt_w)
    xdtend_g = groups(xs * dtend_w)
    last_g, skip_g = groups(misc_w[0:1, :]), groups(misc_w[1:2, :])
    b_g, c_g = _splitter(1, SSD_STATE, SSD_GROUPS)(bm), _splitter(1, SSD_STATE, SSD_GROUPS)(cm)
    ys, news = [], []
    for g in range(SSD_GROUPS):
        cb = bdot_nt(c_g[g], b_g[g])
        st = bdot_nn(b_g[g].T, xdtend_g[g])
        y_off = bdot_nn(c_g[g], prev_g[g]) * start_g[g]
        pairs = []
        for q in range(SSD_HPG // 2):
            xp = xdt_p[g * (SSD_HPG // 2) + q]
            acc = None
            for r in range(2):
                head = SSD_HPG * g + 2 * q + r
                seg = jnp.where(causal, col_w[head] - _sub_row(acs_t, head), 0.0)
                decay = jnp.where(causal, jnp.exp(seg), 0.0)
                t = bdot_nn(cb * decay, jnp.where(first_head if r == 0 else jnp.logical_not(first_head), xp, 0.0))
                acc = t if acc is None else acc + t
            pairs.append(acc)
        ys.append(jnp.concatenate(pairs, axis=1) + y_off + xs_g[g] * skip_g[g])
        news.append(prev_g[g] * last_g[g] + st)
    return jnp.concatenate(ys, axis=1), jnp.concatenate(news, axis=1)


def _expanders():
    e64 = np.zeros((LANES, SSD_D_INNER), np.float32)
    e128 = np.zeros((LANES, SSD_HEADS * CHUNK), np.float32)
    for h in range(SSD_HEADS):
        e64[h, h * SSD_HEAD_DIM:(h + 1) * SSD_HEAD_DIM] = 1.0
        e128[h, h * CHUNK:(h + 1) * CHUNK] = 1.0
    return jnp.asarray(e64, BF16), jnp.asarray(e128, BF16)


def _tri():
    return jnp.asarray(np.tril(np.ones((CHUNK, CHUNK), np.float32)))


BC_W = SSD_GROUPS * SSD_STATE


def ssd_fwd(name, xa, zx, dt_block, par, comm=None):
    rows = xa.shape[0]
    nc = rows // CHUNK

    def body(xs_ref, b_ref, c_ref, dt_ref, par_ref, tri_ref, e64_ref, e128_ref, y_ref, st_ref, state):
        c = pl.program_id(0)

        @pl.when(c == 0)
        def _():
            state[...] = jnp.zeros((SSD_STATE, SSD_D_INNER), F32)

        prev = state[...]
        st_ref[...] = prev
        y, new = _ssd_chunk(xs_ref[...], b_ref[...], c_ref[...], dt_ref[...], prev, par_ref[...], c, tri_ref[...],
                            e64_ref[...], e128_ref[...])
        y_ref[...] = y
        state[...] = new

    const = lambda a: pl.BlockSpec(a.shape, lambda c: (0,) * a.ndim)
    consts = (par, _tri()) + _expanders()
    return carrier_call(
        name, body, (nc,),
        [pl.BlockSpec((CHUNK, SSD_D_INNER), lambda c: (c, 0)),
         pl.BlockSpec((CHUNK, BC_W), lambda c: (c, SSD_D_INNER // BC_W)),
         pl.BlockSpec((CHUNK, BC_W), lambda c: (c, SSD_D_INNER // BC_W + 1)),
         pl.BlockSpec((CHUNK, LANES), lambda c: (c, dt_block))] + [const(a) for a in consts],
        [pl.BlockSpec((CHUNK, SSD_D_INNER), lambda c: (c, 0)),
         pl.BlockSpec((None, SSD_STATE, SSD_D_INNER), lambda c: (c, 0, 0))],
        [jax.ShapeDtypeStruct((rows, SSD_D_INNER), F32),
         jax.ShapeDtypeStruct((nc, SSD_STATE, SSD_D_INNER), F32)],
        [pltpu.VMEM((SSD_STATE, SSD_D_INNER), F32)],
        (xa, xa, xa, zx) + consts, comm)


def carrier_call(name, body, grid, in_specs, out_specs, out_shape, scratch_shapes, args, comm):
    body, in_specs, out_specs, out_shape, scratch_shapes, extra, n_own = with_comm(
        comm, body, grid, in_specs, out_specs, out_shape, scratch_shapes)
    res = pl.pallas_call(
        body, name=name, grid=grid, in_specs=in_specs, out_specs=out_specs, out_shape=out_shape,
        scratch_shapes=scratch_shapes, compiler_params=_params(len(grid)))(*args, *extra)
    return res[:n_own], res[n_own:]


def ssd_bwd(name, dy, xa, zx, dt_block, states, par, comm=None):
    rows = xa.shape[0]
    nc = rows // CHUNK

    def body(dy_ref, xs_ref, b_ref, c_ref, dt_ref, st_ref, par_ref, tri_ref, e64_ref, e128_ref,
             dxs_ref, db_ref, dc_ref, ddt_ref, dpar_ref, dstate):
        s = pl.program_id(0)
        c = nc - 1 - s

        @pl.when(s == 0)
        def _():
            dstate[...] = jnp.zeros((SSD_STATE, SSD_D_INNER), F32)

        def f(xs, bm, cm, dtraw, prev, par_v):
            return _ssd_chunk(xs, bm, cm, dtraw, prev, par_v, c, tri_ref[...], e64_ref[...], e128_ref[...])

        _, vjp = jax.vjp(f, xs_ref[...], b_ref[...], c_ref[...], dt_ref[...], st_ref[...], par_ref[...])
        dxs, dbm, dcm, ddt, dprev, dpar = vjp((dy_ref[...], dstate[...]))
        dxs_ref[...] = dxs
        db_ref[...] = dbm
        dc_ref[...] = dcm
        ddt_ref[...] = ddt
        dstate[...] = dprev
        _acc_out(dpar_ref, dpar, s == 0)

    rev = lambda w, b: pl.BlockSpec((CHUNK, w), lambda s: (nc - 1 - s, b))
    const = lambda a: pl.BlockSpec(a.shape, lambda s: (0,) * a.ndim)
    consts = (par, _tri()) + _expanders()
    return carrier_call(
        name, body, (nc,),
        [rev(SSD_D_INNER, 0), rev(SSD_D_INNER, 0), rev(BC_W, SSD_D_INNER // BC_W), rev(BC_W, SSD_D_INNER // BC_W + 1),
         rev(LANES, dt_block),
         pl.BlockSpec((None, SSD_STATE, SSD_D_INNER), lambda s: (nc - 1 - s, 0, 0))] + [const(a) for a in consts],
        [rev(SSD_D_INNER, 0), rev(BC_W, 0), rev(BC_W, 0), rev(LANES, 0),
         pl.BlockSpec((SUBLANES, LANES), lambda s: (0, 0))],
        [jax.ShapeDtypeStruct((rows, SSD_D_INNER), F32),
         jax.ShapeDtypeStruct((rows, BC_W), F32),
         jax.ShapeDtypeStruct((rows, BC_W), F32),
         jax.ShapeDtypeStruct((rows, LANES), F32),
         jax.ShapeDtypeStruct((SUBLANES, LANES), F32)],
        [pltpu.VMEM((SSD_STATE, SSD_D_INNER), F32)],
        (dy, xa, xa, xa, zx, states) + consts, comm)


GN_W = SSD_D_INNER // SSD_GROUPS


def _gated_norm(y, z, ng):
    g = y * _silu(z)
    outs = []
    for q in range(SSD_GROUPS):
        gs = g[:, q * GN_W:(q + 1) * GN_W]
        outs.append(gs * lax.rsqrt(jnp.mean(gs * gs, axis=-1, keepdims=True) + EPS))
    return jnp.concatenate(outs, axis=1) * ng


def ssd_layer_fwd(tag, h, ln_g, w, comm=None):
    zx, hn = norm_mm(f"ssd_in_{tag}", h, ln_g, w["w_in"], tn=896)
    xa = conv_fwd(f"ssd_conv_{tag}", zx, SSD_D_INNER, 2 * SSD_D_INNER, w["conv_w"], w["conv_b"])
    dt_block = 3 * SSD_D_INNER // LANES
    (y, states), cres = ssd_fwd(f"ssd_scan_{tag}", xa, zx, dt_block, w["par"], comm)
    out, gn = res_mm(
        f"ssd_out_{tag}",
        [(y, (None, SSD_D_INNER), lambda i, j, kk: (i, 0)), (zx, (None, SSD_D_INNER), lambda i, j, kk: (i, 0)),
         (w["norm"], (1, SSD_D_INNER), lambda i, j, kk: (0, 0))],
        lambda r: _gated_norm(r[0][...], r[1][...], r[2][...]).astype(BF16),
        SSD_D_INNER, w["w_out"], h, tn=512, save_dtype=BF16)
    return out, (h, hn, zx, xa, y, states, gn), cres


def ssd_layer_bwd(tag, dh, saved, ln_g, w, comm=None):
    h, hn, zx, xa, y, states, gn = saved
    rows, d = h.shape
    tm = _row_tile(rows)
    dt_block = 3 * SSD_D_INNER // LANES
    dw_out = simple_wgrad(f"ssd_dwout_{tag}", gn, dh, t1=SSD_D_INNER, tn=d)

    def epi_gate(acc, e_refs, o_refs, i, j):
        _, vjp = jax.vjp(_gated_norm, e_refs[0][...], e_refs[1][...], e_refs[2][...])
        dy, dz, dng = vjp(acc)
        o_refs[0][...] = dy
        o_refs[1][...] = dz.astype(BF16)
        row0 = lax.broadcasted_iota(jnp.int32, (SUBLANES, 1), 0) == 0
        _acc_out(o_refs[2], jnp.where(row0, dng, 0.0), i == 0)

    wo, wo_block, wo_imap = _lw(w["w_out"], (SSD_D_INNER, d), lambda i, j, kk: (0, 0))
    dy, dz, dnorm = fused_mm(
        f"ssd_dgate_{tag}", rows=rows, k=d, n=SSD_D_INNER, tm=tm, tn=SSD_D_INNER,
        a_ops=[(dh, (tm, d), lambda i, j, kk: (i, 0))], pro=lambda a, o, i: a[0][...].astype(BF16),
        w=wo, w_block=wo_block, w_imap=wo_imap, dot=_dot_wt,
        e_ops=[(y, (tm, SSD_D_INNER), lambda i, j, kk: (i, 0)), (zx, (tm, SSD_D_INNER), lambda i, j, kk: (i, 0)),
               (w["norm"], (1, SSD_D_INNER), lambda i, j, kk: (0, 0))],
        epi=epi_gate,
        outs=[((rows, SSD_D_INNER), F32, (tm, SSD_D_INNER), lambda i, j, kk: (i, 0)),
              ((rows, SSD_D_INNER), BF16, (tm, SSD_D_INNER), lambda i, j, kk: (i, 0)),
              ((SUBLANES, SSD_D_INNER), F32, (SUBLANES, SSD_D_INNER), lambda i, j, kk: (0, 0))])
    (dxs, dbm, dcm, ddt, dpar), cres = ssd_bwd(f"ssd_dscan_{tag}", dy, xa, zx, dt_block, states, w["par"], comm)
    parts, dcw, dcb = [dz], [], []
    col = SSD_D_INNER
    for nm, dact in (("x", dxs), ("b", dbm), ("c", dcm)):
        wd = dact.shape[1]
        c0 = col - SSD_D_INNER
        du, dw_c, db_c = conv_bwd(f"ssd_dconv{nm}_{tag}", dact, zx, col, w["conv_w"][:, c0:c0 + wd],
                                  w["conv_b"][:, c0:c0 + wd])
        parts.append(du)
        dcw.append(dw_c)
        dcb.append(db_c)
        col += wd
    parts.append(ddt.astype(BF16))
    dzx = jnp.concatenate(parts, axis=1)
    k = dzx.shape[1]
    dw_in = simple_wgrad(f"ssd_dwin_{tag}", hn, dzx, t1=d, tn=896)
    dh_in, dln = rms_bwd_mm(
        f"ssd_dh_{tag}", [(dzx, (None, 896), lambda i, j, kk: (i, kk))], lambda r: r[0][...], k,
        w["w_in"], (d, 896), lambda i, j, kk: (0, kk), _dot_wt, h, ln_g, dh, tk=896,
        tm=704 if rows % 704 == 0 else None)
    grads = dict(w_in=dw_in, w_out=dw_out, conv_w=jnp.concatenate(dcw, axis=1), conv_b=jnp.concatenate(dcb, axis=1),
                 par=dpar, norm=dnorm, ln=dln)
    return dh_in, grads, cres


HP = 2 * LANES
VP = 2 * MLA_V
N_PAIRS = MLA_HEADS // 2
ATT_SCALE = MLA_QK ** -0.5
ROT = MLA_ROPE // 2


def rope_tables(rows):
    inv = 1.0 / (ROPE_THETA ** (jnp.arange(0, MLA_ROPE, 2, dtype=F32) / MLA_ROPE))
    pos = jnp.arange(rows, dtype=F32) - PAD
    ang = pos[:, None] * inv[None, :]
    cos, sin = jnp.cos(ang), jnp.sin(ang)
    one = jnp.ones((rows, MLA_NOPE), F32)
    zero = jnp.zeros((rows, LANES - MLA_QK), F32)
    zn = jnp.zeros((rows, MLA_NOPE), F32)
    zr = jnp.zeros((rows, ROT), F32)
    cosf = jnp.concatenate([one, cos, cos, zero], axis=1)
    sina = jnp.concatenate([zn, -sin, zr, zero], axis=1)
    sinb = jnp.concatenate([zn, zr, sin, zero], axis=1)
    return cosf, sina, sinb


def _qk_norm_rope(x, g, cosf, sina, sinb):
    r = lax.rsqrt(jnp.sum(x * x, axis=-1, keepdims=True) * (1.0 / MLA_QK) + EPS)
    xn = x * r * g
    return xn * cosf + pltpu.roll(xn, LANES - ROT, 1) * sina + pltpu.roll(xn, ROT, 1) * sinb


def _qk_norm_rope_bwd(dout, x, g, cosf, sina, sinb):
    dxn = dout * cosf + pltpu.roll(dout * sina, ROT, 1) + pltpu.roll(dout * sinb, LANES - ROT, 1)
    r = lax.rsqrt(jnp.sum(x * x, axis=-1, keepdims=True) * (1.0 / MLA_QK) + EPS)
    xh = x * r
    t = dxn * g
    dx = r * (t - xh * (jnp.sum(t * xh, axis=-1, keepdims=True) * (1.0 / MLA_QK)))
    return dx, _rows8(dxn * xh)


def _rope_lanes():
    lane = lax.broadcasted_iota(jnp.int32, (1, LANES), 1)
    return jnp.logical_and(lane >= MLA_NOPE, lane < MLA_QK)


QW = MLA_HEADS * LANES
VW = MLA_HEADS * MLA_V


def qk_prep(name, qraw, kvraw, lat, kpe_block, qg, kg, tabs):
    rows = qraw.shape[0]
    tm = _row_tile(rows)

    def body(q_ref, k0_ref, k1_ref, v_ref, pe_ref, qg_ref, kg_ref, c_ref, sa_ref, sb_ref,
             qo_ref, ko_ref, kt_ref, vo_ref, vt_ref):
        tab = (c_ref[...], sa_ref[...], sb_ref[...])
        pe = pe_ref[...]
        for hd in range(MLA_HEADS):
            sl = slice(hd * LANES, (hd + 1) * LANES)
            qo_ref[:, sl] = _qk_norm_rope(q_ref[:, sl], qg_ref[...], *tab).astype(BF16)
            kr = k0_ref if hd < MLA_HEADS // 2 else k1_ref
            ks = slice((hd % (MLA_HEADS // 2)) * LANES, (hd % (MLA_HEADS // 2) + 1) * LANES)
            kk = _qk_norm_rope(kr[:, ks] + pe, kg_ref[...], *tab)
            ko_ref[:, sl] = kk.astype(BF16)
            kt_ref[sl, :] = kk.T.astype(BF16)
        vo_ref[...] = v_ref[...].astype(BF16)
        for c in range(VW // LANES):
            sl = slice(c * LANES, (c + 1) * LANES)
            vt_ref[sl, :] = v_ref[:, sl].T.astype(BF16)

    row = lambda w, b: pl.BlockSpec((tm, w), lambda i: (i, b))
    col = lambda w: pl.BlockSpec((w, tm), lambda i: (0, i))
    one = pl.BlockSpec((1, LANES), lambda i: (0, 0))
    return pl.pallas_call(
        body, name=name, grid=(rows // tm,),
        in_specs=[row(QW, 0), row(VW, 0), row(VW, 1), row(VW, 2), row(LANES, kpe_block), one, one,
                  row(LANES, 0), row(LANES, 0), row(LANES, 0)],
        out_specs=[row(QW, 0), row(QW, 0), col(QW), row(VW, 0), col(VW)],
        out_shape=[jax.ShapeDtypeStruct((rows, QW), BF16), jax.ShapeDtypeStruct((rows, QW), BF16),
                   jax.ShapeDtypeStruct((QW, rows), BF16), jax.ShapeDtypeStruct((rows, VW), BF16),
                   jax.ShapeDtypeStruct((VW, rows), BF16)],
        compiler_params=_params(1),
    )(qraw, kvraw, kvraw, kvraw, lat, qg, kg, *tabs)


def qk_prep_bwd(name, dq_t, dk, dv, qraw, kvraw, lat, kpe_block, qg, kg, tabs):
    rows = qraw.shape[0]
    tm = _row_tile(rows)

    def body(dq_ref, dk_ref, dv_ref, q_ref, k0_ref, k1_ref, pe_ref, qg_ref, kg_ref, c_ref, sa_ref, sb_ref,
             dqo_ref, dkvo_ref, dpe_ref, dqg_ref, dkg_ref):
        i = pl.program_id(0)
        tab = (c_ref[...], sa_ref[...], sb_ref[...])
        pe = pe_ref[...]
        dpe = jnp.zeros((tm, LANES), F32)
        dqg = jnp.zeros((SUBLANES, LANES), F32)
        dkg = jnp.zeros((SUBLANES, LANES), F32)
        for hd in range(MLA_HEADS):
            sl = slice(hd * LANES, (hd + 1) * LANES)
            dx, dg = _qk_norm_rope_bwd(dq_ref[sl, :].T, q_ref[:, sl], qg_ref[...], *tab)
            dqo_ref[:, sl] = dx.astype(BF16)
            dqg = dqg + dg
            kr = k0_ref if hd < MLA_HEADS // 2 else k1_ref
            ks = slice((hd % (MLA_HEADS // 2)) * LANES, (hd % (MLA_HEADS // 2) + 1) * LANES)
            dx, dg = _qk_norm_rope_bwd(dk_ref[:, sl], kr[:, ks] + pe, kg_ref[...], *tab)
            dkvo_ref[:, sl] = dx.astype(BF16)
            dpe = dpe + dx
            dkg = dkg + dg
        dkvo_ref[:, QW:QW + VW] = dv_ref[...].astype(BF16)
        dpe_ref[...] = jnp.where(_rope_lanes(), dpe, 0.0)
        _acc_out(dqg_ref, dqg, i == 0)
        _acc_out(dkg_ref, dkg, i == 0)

    row = lambda w, b: pl.BlockSpec((tm, w), lambda i: (i, b))
    one = pl.BlockSpec((1, LANES), lambda i: (0, 0))
    acc = pl.BlockSpec((SUBLANES, LANES), lambda i: (0, 0))
    return pl.pallas_call(
        body, name=name, grid=(rows // tm,),
        in_specs=[pl.BlockSpec((QW, tm), lambda i: (0, i)), row(QW, 0), row(VW, 0), row(QW, 0), row(VW, 0), row(VW, 1),
                  row(LANES, kpe_block), one, one, row(LANES, 0), row(LANES, 0), row(LANES, 0)],
        out_specs=[row(QW, 0), row(QW + VW, 0), row(LANES, 0), acc, acc],
        out_shape=[jax.ShapeDtypeStruct((rows, QW), BF16), jax.ShapeDtypeStruct((rows, QW + VW), BF16),
                   jax.ShapeDtypeStruct((rows, LANES), F32),
                   jax.ShapeDtypeStruct((SUBLANES, LANES), F32), jax.ShapeDtypeStruct((SUBLANES, LANES), F32)],
        compiler_params=_params(1),
    )(dq_t, dk, dv, qraw, kvraw, kvraw, lat, qg, kg, *tabs)


def _att_mask_t(qb, kb, bt):
    kpos = kb * bt + lax.broadcasted_iota(jnp.int32, (bt, bt), 0)
    qpos = qb * bt + lax.broadcasted_iota(jnp.int32, (bt, bt), 1)
    return jnp.logical_and(kpos <= qpos, jnp.logical_or(kpos >= PAD, qpos < PAD))


def attn_fwd(name, q, k, vt, comm=None):
    rows = q.shape[0]
    bt = _row_tile(rows)
    nb = rows // bt
    assert bt >= CHUNK

    def body(q_ref, k_ref, vt_ref, o_ref, lse_ref):
        qi = pl.program_id(1)
        lse_ref[...] = jnp.zeros((SUBLANES, bt), F32)

        def tile(kb, carry, masked):
            r0 = pl.multiple_of(kb * bt, LANES)
            new = []
            for hh in range(2):
                m, l, acc = carry[3 * hh:3 * hh + 3]
                qs = slice(hh * LANES, (hh + 1) * LANES)
                vs = slice(hh * MLA_V, (hh + 1) * MLA_V)
                s = lax.dot_general(k_ref[pl.ds(r0, bt), qs], q_ref[:, qs], (NT, ((), ())),
                                    preferred_element_type=F32) * ATT_SCALE
                if masked:
                    s = jnp.where(_att_mask_t(qi, kb, bt), s, NEG)
                m_new = jnp.maximum(m, jnp.max(s, axis=0, keepdims=True))
                alpha = jnp.exp(m - m_new)
                p = jnp.exp(s - m_new)
                l = alpha * l + jnp.sum(p, axis=0, keepdims=True)
                acc = alpha * acc + jnp.dot(vt_ref[vs, pl.ds(r0, bt)], p.astype(BF16), preferred_element_type=F32)
                new += [m_new, l, acc]
            return tuple(new)

        init = (jnp.full((1, bt), NEG, F32), jnp.zeros((1, bt), F32), jnp.zeros((MLA_V, bt), F32)) * 2
        carry = tile(0, init, True)

        def rest(carry):
            carry = lax.fori_loop(1, qi, lambda kb, cr: tile(kb, cr, False), carry)
            return tile(qi, carry, True)

        carry = lax.cond(qi > 0, rest, lambda cr: cr, carry)
        for hh in range(2):
            m, l, acc = carry[3 * hh:3 * hh + 3]
            o_ref[hh * MLA_V:(hh + 1) * MLA_V, :] = acc / l
            lse_ref[hh:hh + 1, :] = m + jnp.log(l)

    return carrier_call(
        name, body, (N_PAIRS, nb),
        [pl.BlockSpec((bt, HP), lambda p, i: (i, p)),
         pl.BlockSpec((rows, HP), lambda p, i: (0, p)),
         pl.BlockSpec((VP, rows), lambda p, i: (p, 0))],
        [pl.BlockSpec((VP, bt), lambda p, i: (p, i)),
         pl.BlockSpec((None, SUBLANES, bt), lambda p, i: (p, 0, i))],
        [jax.ShapeDtypeStruct((VW, rows), F32), jax.ShapeDtypeStruct((N_PAIRS, SUBLANES, rows), F32)],
        [], (q, k, vt), comm)


def attn_bwd(name, q, k, kt, v, do_t, lse, delta, comm=None):
    rows = q.shape[0]
    bt = _row_tile(rows)
    nb = rows // bt

    def body(q_ref, k_ref, kt_ref, v_ref, do_ref, lse_ref, dl_ref, dq_ref, dk_ref, dv_ref, dk_scr, dv_scr):
        ki = pl.program_id(1)

        @pl.when(ki == 0)
        def _():
            dq_ref[...] = jnp.zeros((HP, rows), F32)

        dk_scr[...] = jnp.zeros((bt, HP), F32)
        dv_scr[...] = jnp.zeros((bt, VP), F32)

        def tile(qb, masked):
            c0 = pl.multiple_of(qb * bt, LANES)
            for hh in range(2):
                qs = slice(hh * LANES, (hh + 1) * LANES)
                vs = slice(hh * MLA_V, (hh + 1) * MLA_V)
                qv = q_ref[pl.ds(c0, bt), qs]
                dov = do_ref[vs, pl.ds(c0, bt)]
                lse = lse_ref[hh:hh + 1, pl.ds(c0, bt)]
                dl = dl_ref[hh:hh + 1, pl.ds(c0, bt)]
                s = lax.dot_general(k_ref[:, qs], qv, (NT, ((), ())), preferred_element_type=F32) * ATT_SCALE
                p = jnp.exp(s - lse)
                if masked:
                    p = jnp.where(_att_mask_t(qb, ki, bt), p, 0.0)
                dp = jnp.dot(v_ref[:, vs], dov, preferred_element_type=F32)
                ds = (p * (dp - dl) * ATT_SCALE).astype(BF16)
                dv_scr[:, vs] += lax.dot_general(p.astype(BF16), dov, (NT, ((), ())), preferred_element_type=F32)
                dk_scr[:, qs] += jnp.dot(ds, qv, preferred_element_type=F32)
                dq_ref[qs, pl.ds(c0, bt)] += jnp.dot(kt_ref[qs, :], ds, preferred_element_type=F32)

        @pl.when(ki == 0)
        def _():
            def every(qb, carry):
                tile(qb, True)
                return carry

            lax.fori_loop(0, nb, every, 0)

        @pl.when(ki > 0)
        def _():
            tile(ki, True)

            def later(qb, carry):
                tile(qb, False)
                return carry

            lax.fori_loop(ki + 1, nb, later, 0)

        dk_ref[...] = dk_scr[...]
        dv_ref[...] = dv_scr[...]

    stat = pl.BlockSpec((None, SUBLANES, rows), lambda p, i: (p, 0, 0))
    return carrier_call(
        name, body, (N_PAIRS, nb),
        [pl.BlockSpec((rows, HP), lambda p, i: (0, p)),
         pl.BlockSpec((bt, HP), lambda p, i: (i, p)),
         pl.BlockSpec((HP, bt), lambda p, i: (p, i)),
         pl.BlockSpec((bt, VP), lambda p, i: (i, p)),
         pl.BlockSpec((VP, rows), lambda p, i: (p, 0)),
         stat, stat],
        [pl.BlockSpec((HP, rows), lambda p, i: (p, 0)),
         pl.BlockSpec((bt, HP), lambda p, i: (i, p)),
         pl.BlockSpec((bt, VP), lambda p, i: (i, p))],
        [jax.ShapeDtypeStruct((QW, rows), F32), jax.ShapeDtypeStruct((rows, QW), F32),
         jax.ShapeDtypeStruct((rows, VW), F32)],
        [pltpu.VMEM((bt, HP), F32), pltpu.VMEM((bt, VP), F32)],
        (q, k, kt, v, do_t, lse, delta), comm)


def _dot_cast_w(a, w_ref):
    return jnp.dot(a, w_ref[...].astype(BF16), preferred_element_type=F32)


def _dot_cast_wt(a, w_ref):
    return lax.dot_general(a, w_ref[...].astype(BF16), (NT, ((), ())), preferred_element_type=F32)


LAT_W = 768
KPE_BLOCK = MLA_Q_RANK // LANES
KV_BLOCK = (MLA_Q_RANK + LANES) // MLA_KV_RANK


def mla_layer_fwd(tag, h, ln_g, w, tabs, comm=None):
    lat, hn = norm_mm(f"mla_in_{tag}", h, ln_g, w["w_in"], tn=LAT_W)
    qraw, qn = norm_mm(f"mla_q_{tag}", lat, w["q_a"], w["w_q"], tn=512, k_cols=MLA_Q_RANK, col_block=0)
    kvraw, kvn = norm_mm(f"mla_kv_{tag}", lat, w["kv_a"], w["w_kv"], tn=512, k_cols=MLA_KV_RANK, col_block=KV_BLOCK)
    q, k, kt, v, vt = qk_prep(f"mla_prep_{tag}", qraw, kvraw, lat, KPE_BLOCK, w["q_norm"], w["k_norm"], tabs)
    (o_t, lse), cres = attn_fwd(f"mla_attn_{tag}", q, k, vt, comm)
    out = res_mm(f"mla_out_{tag}", [(o_t, (VW, None), lambda i, j, kk: (0, i))],
                 lambda r: r[0][...].T.astype(BF16), VW, w["w_out"], h, tn=512, tm=_big_tile(h.shape[0]))
    return out, (h, hn, lat, qn, kvn, qraw, kvraw, q, k, kt, v, o_t, lse), cres


def mla_layer_bwd(tag, dh, saved, ln_g, w, tabs, comm=None):
    h, hn, lat, qn, kvn, qraw, kvraw, q, k, kt, v, o_t, lse = saved
    rows, d = h.shape
    tm = _row_tile(rows)

    def epi_set(acc, e_refs, o_refs, i, j):
        o_refs[0][...] = acc.astype(BF16)

    dw_out, = fused_mm(
        f"mla_dwout_{tag}", rows=VW, k=rows, n=d, tm=512, tn=d, tk=tm,
        a_ops=[(o_t, (512, tm), lambda i, j, kk: (i, kk))], pro=lambda a, o_, i: a[0][...].astype(BF16),
        w=dh, w_block=(tm, d), w_imap=lambda i, j, kk: (kk, 0), dot=_dot_cast_w, epi=epi_set,
        outs=[((VW, d), BF16, (512, d), lambda i, j, kk: (i, 0))])

    def epi_do(acc, e_refs, o_refs, i, j):
        o_refs[0][...] = acc.astype(BF16)
        prod = acc * e_refs[0][...]
        o_refs[1][...] = jnp.zeros((N_PAIRS, SUBLANES, tm), F32)
        for hd in range(MLA_HEADS):
            o_refs[1][hd // 2, hd % 2:hd % 2 + 1, :] = jnp.sum(prod[hd * MLA_V:(hd + 1) * MLA_V, :], axis=0,
                                                               keepdims=True)

    wo, wo_block, wo_imap = _lw(w["w_out"], (VW, d), lambda i, j, kk: (0, 0))
    do_t, delta = fused_mm(
        f"mla_do_{tag}", rows=VW, k=d, n=rows, tm=VW, tn=tm,
        a_ops=[(wo, wo_block, wo_imap)], pro=lambda a, o_, i: a[0][...],
        w=dh, w_block=(tm, d), w_imap=lambda i, j, kk: (j, 0), dot=_dot_cast_wt,
        e_ops=[(o_t, (VW, tm), lambda i, j, kk: (0, j))], epi=epi_do,
        outs=[((VW, rows), BF16, (VW, tm), lambda i, j, kk: (0, j)),
              ((N_PAIRS, SUBLANES, rows), F32, (N_PAIRS, SUBLANES, tm), lambda i, j, kk: (0, 0, j))])
    (dq_t, dk, dv), cres = attn_bwd(f"mla_dattn_{tag}", q, k, kt, v, do_t, lse, delta, comm)
    dqraw, dkvraw, dpe, dqg, dkg = qk_prep_bwd(f"mla_dprep_{tag}", dq_t, dk, dv, qraw, kvraw, lat, KPE_BLOCK,
                                               w["q_norm"], w["k_norm"], tabs)
    dw_q = simple_wgrad(f"mla_dwq_{tag}", qn, dqraw, t1=MLA_Q_RANK, tn=512)
    dqlat, dqa = rms_bwd_mm(
        f"mla_dqlat_{tag}", [(dqraw, (None, QW), lambda i, j, kk: (i, 0))], lambda r: r[0][...], QW,
        w["w_q"], (MLA_Q_RANK, QW), lambda i, j, kk: (0, 0), _dot_wt, lat, w["q_a"], None,
        h_cols=MLA_Q_RANK, h_col_block=0, add_dh=False)
    dw_kv = simple_wgrad(f"mla_dwkv_{tag}", kvn, dkvraw, t1=MLA_KV_RANK, tn=512)
    dkvlat, dkva = rms_bwd_mm(
        f"mla_dkvlat_{tag}", [(dkvraw, (None, QW + VW), lambda i, j, kk: (i, 0))], lambda r: r[0][...], QW + VW,
        w["w_kv"], (MLA_KV_RANK, QW + VW), lambda i, j, kk: (0, 0), _dot_wt, lat, w["kv_a"], None,
        h_cols=MLA_KV_RANK, h_col_block=KV_BLOCK, add_dh=False)
    dlat = jnp.concatenate([dqlat.astype(BF16), dpe.astype(BF16), dkvlat.astype(BF16)], axis=1)
    dw_in = simple_wgrad(f"mla_dwin_{tag}", hn, dlat, t1=512, tn=LAT_W)
    dh_in, dln = rms_bwd_mm(
        f"mla_dh_{tag}", [(dlat, (None, LAT_W), lambda i, j, kk: (i, 0))], lambda r: r[0][...], LAT_W,
        w["w_in"], (d, LAT_W), lambda i, j, kk: (0, 0), _dot_wt, h, ln_g, dh)
    grads = dict(w_in=dw_in, w_q=dw_q, w_kv=dw_kv, w_out=dw_out, q_a=dqa, kv_a=dkva, q_norm=dqg, k_norm=dkg, ln=dln)
    return dh_in, grads, cres


def loss_head(h, target):
    rows, d = h.shape
    nb = rows // CHUNK

    def body(h_ref, t_ref, l_ref, dh_ref):
        i = pl.program_id(0)
        err = jnp.where(i > 0, h_ref[...] - t_ref[...], 0.0)
        dh_ref[...] = err * (1.0 / d)
        _acc_out(l_ref, _rows8(err * err) * (0.5 / d), i == 0)

    return pl.pallas_call(
        body, name="loss_head", grid=(nb,),
        in_specs=[pl.BlockSpec((CHUNK, d), lambda i: (i, 0)),
                  pl.BlockSpec((CHUNK, d), lambda i: (jnp.maximum(i - 1, 0), 0))],
        out_specs=[pl.BlockSpec((SUBLANES, d), lambda i: (0, 0)), pl.BlockSpec((CHUNK, d), lambda i: (i, 0))],
        out_shape=[jax.ShapeDtypeStruct((SUBLANES, d), F32), jax.ShapeDtypeStruct((rows, d), F32)],
        compiler_params=_params(1),
    )(h, target)


def _adamw(w, g, m, v):
    m = ADAM_B1 * m + (1.0 - ADAM_B1) * g
    v = ADAM_B2 * v + (1.0 - ADAM_B2) * jnp.square(g)
    m_hat = m / (1.0 - ADAM_B1 ** ADAM_STEP)
    v_hat = v / (1.0 - ADAM_B2 ** ADAM_STEP)
    delta = -ADAM_LR * (m_hat / (jnp.sqrt(v_hat) + ADAM_EPS) + ADAM_WD * w)
    return delta, m, v


def reduce_adamw(name, recvs, w, m, v):
    nl, r, c = w.shape
    tr = 128 if r % 128 == 0 else r
    nr = r // tr

    def body(*refs):
        r_refs = refs[:nl]
        w_ref, m_ref, v_ref, g_ref, d_ref, mo_ref, vo_ref = refs[nl:]
        layer = pl.program_id(0)
        for l in range(nl):
            @pl.when(layer == l)
            def _(l=l):
                g = r_refs[l][0].astype(F32)
                for s in range(1, N_DEV):
                    g = g + r_refs[l][s].astype(F32)
                g_ref[...] = g
                d_ref[...], mo_ref[...], vo_ref[...] = _adamw(w_ref[...], g, m_ref[...], v_ref[...])

    def recv_spec(l):
        return pl.BlockSpec((N_DEV, tr, c),
                            lambda y, i: (0, jnp.where(y == l, i, jnp.where(y < l, 0, nr - 1)), 0))

    blk = pl.BlockSpec((None, tr, c), lambda y, i: (y, i, 0))
    return pl.pallas_call(
        body, name=name, grid=(nl, nr),
        in_specs=[recv_spec(l) for l in range(nl)] + [blk, blk, blk],
        out_specs=[blk] * 4, out_shape=[jax.ShapeDtypeStruct((nl, r, c), F32)] * 4,
        compiler_params=_params(2),
    )(*recvs, w, m, v)


def small_reduce(recv):
    def body(r_ref, o_ref):
        g = r_ref[0]
        for s in range(1, N_DEV):
            g = g + r_ref[s]
        o_ref[...] = g

    return pl.pallas_call(body, name="small_reduce", out_shape=jax.ShapeDtypeStruct(recv.shape[1:], F32))(recv)


def small_adamw(w, g, m, v):
    def body(w_ref, g_ref, m_ref, v_ref, d_ref, mo_ref, vo_ref):
        d_ref[...], mo_ref[...], vo_ref[...] = _adamw(w_ref[...], g_ref[...], m_ref[...], v_ref[...])

    return pl.pallas_call(body, name="small_adamw", out_shape=[jax.ShapeDtypeStruct(w.shape, F32)] * 3)(w, g, m, v)


def _pack(parts):
    flat, meta, off = [], [], 0
    for p in parts:
        n = int(np.prod(p.shape))
        flat.append(p.reshape(-1).astype(F32))
        meta.append((off, p.shape))
        off += n
    total = -(-off // (SUBLANES * LANES)) * (SUBLANES * LANES)
    flat.append(jnp.zeros((total - off,), F32))
    return jnp.concatenate(flat).reshape(total // LANES, LANES), meta


def _unpack(packed, meta):
    flat = packed.reshape(-1)
    return [flat[off:off + int(np.prod(shape))].reshape(shape) for off, shape in meta]


MESH = pl.DeviceIdType.MESH
N_PEERS = N_DEV - 1


def _me():
    return lax.axis_index("x"), lax.axis_index("y"), lax.axis_index("c")


def _peer(k):
    x, y, c = _me()
    return (1 - x if k & 4 else x, 1 - y if k & 2 else y, 1 - c if k & 1 else c)


def _dev_index(pos):
    return 4 * pos[0] + 2 * pos[1] + pos[2]


def make_comm(items):
    n = len(items)

    def part(ref, a, idx):
        rows = items[a][1]
        if rows == "all":
            return ref
        return ref.at[idx] if rows is None else ref.at[pl.ds(idx * rows, rows)]

    def part_shape(a):
        arr, rows = items[a]
        if rows == "all":
            return arr.shape
        return arr.shape[1:] if rows is None else (rows,) + arr.shape[1:]

    def run(phase, ins, outs, send_sems, recv_sems, local_sems):
        me = _dev_index(_me())
        for a in range(n):
            local = pltpu.make_async_copy(part(ins[a], a, me), outs[a].at[me], local_sems.at[a])
            if phase == "start":
                local.start()
            for k in range(1, N_DEV):
                peer = _peer(k)
                if phase == "start":
                    pltpu.make_async_remote_copy(
                        src_ref=part(ins[a], a, _dev_index(peer)), dst_ref=outs[a].at[me],
                        send_sem=send_sems.at[a, k - 1], recv_sem=recv_sems.at[a, k - 1],
                        device_id=peer, device_id_type=MESH).start()
                else:
                    cp = pltpu.make_async_remote_copy(
                        src_ref=part(ins[a], a, me), dst_ref=outs[a].at[_dev_index(peer)],
                        send_sem=send_sems.at[a, k - 1], recv_sem=recv_sems.at[a, k - 1],
                        device_id=peer, device_id_type=MESH)
                    cp.wait_recv()
                    cp.wait_send()
            if phase == "wait":
                local.wait()

    return dict(
        ins=[it[0] for it in items],
        outs=[jax.ShapeDtypeStruct((N_DEV,) + part_shape(a), items[a][0].dtype) for a in range(n)],
        sems=[pltpu.SemaphoreType.DMA((n, N_PEERS)), pltpu.SemaphoreType.DMA((n, N_PEERS)),
              pltpu.SemaphoreType.DMA((n,))],
        run=run)


ANY_SPEC = pl.BlockSpec(memory_space=pl.ANY)


def comm_call(name, comm):
    n, no = len(comm["ins"]), len(comm["outs"])

    def body(*refs):
        comm["run"]("start", refs[:n], refs[n:n + no], *refs[n + no:])
        comm["run"]("wait", refs[:n], refs[n:n + no], *refs[n + no:])

    return pl.pallas_call(
        body, name=name, in_specs=[ANY_SPEC] * n, out_specs=[ANY_SPEC] * no, out_shape=comm["outs"],
        scratch_shapes=comm["sems"])(*comm["ins"])


def with_comm(comm, body, grid, in_specs, out_specs, out_shape, scratch_shapes):
    if comm is None:
        return body, in_specs, out_specs, out_shape, scratch_shapes, [], len(out_shape)
    n_in, n_out, n_scr = len(in_specs), len(out_shape), len(scratch_shapes)
    ci, co = len(comm["ins"]), len(comm["outs"])

    def wrapped(*refs):
        ins, cins = refs[:n_in], refs[n_in:n_in + ci]
        outs = refs[n_in + ci:n_in + ci + n_out]
        couts = refs[n_in + ci + n_out:n_in + ci + n_out + co]
        rest = refs[n_in + ci + n_out + co:]
        scr, sems = rest[:n_scr], rest[n_scr:]
        first = functools.reduce(jnp.logical_and, [pl.program_id(a) == 0 for a in range(len(grid))])
        last = functools.reduce(jnp.logical_and, [pl.program_id(a) == grid[a] - 1 for a in range(len(grid))])

        @pl.when(first)
        def _():
            comm["run"]("start", cins, couts, *sems)

        body(*ins, *outs, *scr)

        @pl.when(last)
        def _():
            comm["run"]("wait", cins, couts, *sems)

    return (wrapped, list(in_specs) + [ANY_SPEC] * ci, list(out_specs) + [ANY_SPEC] * co,
            list(out_shape) + list(comm["outs"]), list(scratch_shapes) + list(comm["sems"]), list(comm["ins"]), n_out)


WEIGHTS = ['meta_tokens', 'ln_mix', 'ln_mlp', 'ssd_w_in', 'ssd_conv_w', 'ssd_conv_b', 'ssd_dt_bias', 'ssd_a_log',
           'ssd_d', 'ssd_norm', 'ssd_w_out', 'mla_w_in', 'mla_q_a_norm', 'mla_w_q_b', 'mla_kv_a_norm', 'mla_w_kv_b',
           'mla_q_norm', 'mla_k_norm', 'mla_w_out', 'mlp_w_up', 'mlp_w_down']
BIG = ['ssd_w_in', 'ssd_w_out', 'mla_w_in', 'mla_w_q_b', 'mla_w_kv_b', 'mla_w_out', 'mlp_w_up', 'mlp_w_down']
SMALL_SHARDED = ['meta_tokens', 'ssd_conv_w', 'mla_q_a_norm', 'mla_kv_a_norm']
SMALL_REPL = ['ln_mix', 'ln_mlp', 'ssd_conv_b', 'ssd_dt_bias', 'ssd_a_log', 'ssd_d', 'ssd_norm', 'mla_q_norm',
              'mla_k_norm']
SMALL = SMALL_REPL + SMALL_SHARDED
SSD_IN_PAD = 6272
SSD_IN_TN = 896
MLA_IN = MLA_Q_RANK + MLA_KV_RANK + MLA_ROPE


def _pad_last(v, n):
    return jnp.pad(v, [(0, 0)] * (v.ndim - 1) + [(0, n - v.shape[-1])])


SSD_BIG = ['ssd_w_in', 'ssd_w_out']
MLA_BIG = ['mla_w_in', 'mla_w_q_b', 'mla_w_kv_b', 'mla_w_out']
MLP_BIG = ['mlp_w_up', 'mlp_w_down']


def _mix_big(i):
    return [(n, i // 2) for n in (SSD_BIG if i % 2 == 0 else MLA_BIG)]


def _mlp_big(i):
    return [(n, i) for n in MLP_BIG]


def _mix_weights(i, gw, W, full):
    j = i // 2
    d = W['ln_mix'].shape[-1]
    if i % 2 == 0:
        wi = gw[('ssd_w_in', j)].transpose(1, 0, 2).reshape(d, -1)
        par = jnp.concatenate([_pad_last(W[n][j][None], LANES) for n in ('ssd_dt_bias', 'ssd_a_log', 'ssd_d')]
                              + [jnp.zeros((SUBLANES - 3, LANES), F32)])
        return dict(w_in=_pad_last(wi, SSD_IN_PAD), conv_w=full['ssd_conv_w'][j], conv_b=W['ssd_conv_b'][j][None],
                    par=par, norm=W['ssd_norm'][j][None], w_out=gw[('ssd_w_out', j)].reshape(SSD_D_INNER, d))
    wi = gw[('mla_w_in', j)].reshape(d, MLA_IN)
    kpe = jnp.pad(wi[:, MLA_Q_RANK + MLA_KV_RANK:], ((0, 0), (MLA_NOPE, LANES - MLA_QK)))
    wq = gw[('mla_w_q_b', j)].transpose(1, 0, 2).reshape(MLA_Q_RANK, MLA_HEADS, MLA_QK)
    wkv = gw[('mla_w_kv_b', j)].transpose(1, 0, 2).reshape(MLA_KV_RANK, MLA_HEADS, MLA_NOPE + MLA_V)
    return dict(
        w_in=jnp.concatenate([wi[:, :MLA_Q_RANK], kpe, wi[:, MLA_Q_RANK:MLA_Q_RANK + MLA_KV_RANK]], axis=1),
        w_q=_pad_last(wq, LANES).reshape(MLA_Q_RANK, QW),
        w_kv=jnp.concatenate([_pad_last(wkv[..., :MLA_NOPE], LANES).reshape(MLA_KV_RANK, QW),
                              wkv[..., MLA_NOPE:].reshape(MLA_KV_RANK, VW)], axis=1),
        w_out=gw[('mla_w_out', j)].reshape(VW, d), q_a=full['mla_q_a_norm'][j][None],
        kv_a=full['mla_kv_a_norm'][j][None],
        q_norm=_pad_last(W['mla_q_norm'][j][None], LANES), k_norm=_pad_last(W['mla_k_norm'][j][None], LANES))


def _step(x, target, W, M, V):
    d = x.shape[-1]
    me = _dev_index(_me())
    depth = W['ln_mix'].shape[0]

    def gather_keys(i):
        return _mlp_big(i) + (_mix_big(i + 1) if i + 1 < depth else [])

    def gather_items(keys):
        return [(W[n][l].astype(BF16), "all") for n, l in keys]

    small_pack, small_meta = _pack([W[n] for n in SMALL_SHARDED])
    got = comm_call("gather_0", make_comm(gather_items(_mix_big(0)) + [(small_pack, "all")]))
    per_dev = [_unpack(got[-1][s], small_meta) for s in range(N_DEV)]
    full = {n: jnp.concatenate([per_dev[s][i] for s in range(N_DEV)], axis=-1) for i, n in enumerate(SMALL_SHARDED)}
    gw = dict(zip(_mix_big(0), got))

    h = jnp.concatenate([jnp.zeros((PAD, d), F32), full['meta_tokens'], x], axis=0)
    rows = h.shape[0]
    tabs = rope_tables(rows)
    saved, weights = [], []
    for i in range(depth):
        comm = make_comm(gather_items(gather_keys(i)))
        mix = _mix_weights(i, gw, W, full)
        if i % 2 == 0:
            h, s_mix, got = ssd_layer_fwd(f"{i}", h, W['ln_mix'][i][None], mix, comm)
        else:
            h, s_mix, got = mla_layer_fwd(f"{i}", h, W['ln_mix'][i][None], mix, tabs, comm)
        gw.update(zip(gather_keys(i), got))
        up, down = gw[('mlp_w_up', i)], gw[('mlp_w_down', i)].reshape(-1, d)
        h, s_mlp = mlp_fwd(f"{i}", h, W['ln_mlp'][i][None], up, down)
        saved.append((s_mix, s_mlp))
        weights.append((mix, up, down))
    loss_part, dh = loss_head(h, target)
    loss = lax.psum(jnp.sum(loss_part), ("x", "y", "c"))

    recv = {}
    pending = []
    small = {n: [None] * W[n].shape[0] for n in SMALL if n != 'meta_tokens'}
    for i in reversed(range(depth)):
        j = i // 2
        s_mix, s_mlp = saved[i]
        mix, up, down = weights[i]
        dh, dw_up, dw_down, dg = mlp_bwd(f"{i}", dh, s_mlp, W['ln_mlp'][i][None], up, down)
        small['ln_mlp'][i] = dg.sum(0)
        pending += list(zip(_mlp_big(i), [(dw_up, None), (dw_down, down.shape[0] // N_DEV)]))
        comm = make_comm([it for _, it in pending])
        sends = []
        if i % 2 == 0:
            dh, g, got = ssd_layer_bwd(f"{i}", dh, s_mix, W['ln_mix'][i][None], mix, comm)
            n_in = W['ssd_w_in'].shape[-1]
            sends.append((g['w_in'][:, :N_DEV * n_in].reshape(d, N_DEV, n_in).transpose(1, 0, 2), None))
            sends.append((g['w_out'], SSD_D_INNER // N_DEV))
            small['ssd_conv_w'][j] = g['conv_w'].reshape(SSD_CONV, SUBLANES, -1).sum(1)
            small['ssd_conv_b'][j] = g['conv_b'].sum(0)
            small['ssd_dt_bias'][j] = g['par'][0, :SSD_HEADS]
            small['ssd_a_log'][j] = g['par'][1, :SSD_HEADS]
            small['ssd_d'][j] = g['par'][2, :SSD_HEADS]
            small['ssd_norm'][j] = g['norm'].sum(0)
        else:
            dh, g, got = mla_layer_bwd(f"{i}", dh, s_mix, W['ln_mix'][i][None], mix, tabs, comm)
            gi = g['w_in']
            gi = jnp.concatenate([gi[:, :MLA_Q_RANK], gi[:, MLA_Q_RANK + LANES:],
                                  gi[:, MLA_Q_RANK + MLA_NOPE:MLA_Q_RANK + MLA_QK]], axis=1)
            sends.append((gi, d // N_DEV))
            gq = g['w_q'].reshape(MLA_Q_RANK, MLA_HEADS, LANES)[..., :MLA_QK]
            sends.append((gq.reshape(MLA_Q_RANK, N_DEV, -1).transpose(1, 0, 2), None))
            gkv = jnp.concatenate([g['w_kv'][:, :QW].reshape(MLA_KV_RANK, MLA_HEADS, LANES)[..., :MLA_NOPE],
                                   g['w_kv'][:, QW:].reshape(MLA_KV_RANK, MLA_HEADS, MLA_V)], axis=-1)
            sends.append((gkv.reshape(MLA_KV_RANK, N_DEV, -1).transpose(1, 0, 2), None))
            sends.append((g['w_out'], VW // N_DEV))
            small['mla_q_a_norm'][j] = g['q_a'].sum(0)
            small['mla_kv_a_norm'][j] = g['kv_a'].sum(0)
            small['mla_q_norm'][j] = g['q_norm'].sum(0)[:MLA_QK]
            small['mla_k_norm'][j] = g['k_norm'].sum(0)[:MLA_QK]
        small['ln_mix'][i] = g['ln'].sum(0)
        recv.update({key: a for (key, _), a in zip(pending, got)})
        pending = list(zip(_mix_big(i), sends))
    grad_x = dh[CHUNK:]
    small_full = {n: jnp.stack(v) for n, v in small.items()}
    small_full['meta_tokens'] = dh[PAD:CHUNK]

    gpack, gmeta = _pack([small_full[n] for n in SMALL])
    got = comm_call("exchange_0", make_comm([it for _, it in pending] + [(gpack, "all")]))
    recv.update({key: a for (key, _), a in zip(pending, got)})
    res = {}
    for n in BIG:
        res[n] = reduce_adamw(f"adamw_{n}", [recv[(n, l)] for l in range(W[n].shape[0])], W[n], M[n], V[n])
    gsum = dict(zip(SMALL, _unpack(small_reduce(got[-1]), gmeta)))
    for n in SMALL_SHARDED:
        wl = W[n].shape[-1]
        gsum[n] = lax.dynamic_slice_in_dim(gsum[n], me * wl, wl, axis=gsum[n].ndim - 1)
    wp, wmeta = _pack([W[n] for n in SMALL])
    gp, _ = _pack([gsum[n] for n in SMALL])
    mp, _ = _pack([M[n] for n in SMALL])
    vp, _ = _pack([V[n] for n in SMALL])
    upd = [_unpack(o, wmeta) for o in small_adamw(wp, gp, mp, vp)]
    for a, n in enumerate(SMALL):
        res[n] = [gsum[n], upd[0][a], upd[1][a], upd[2][a]]
    return (loss, grad_x[None]) + tuple(res[n][q] for q in range(4) for n in WEIGHTS)


def kernel(x, meta_tokens, ln_mix, ln_mlp, ssd_w_in, ssd_conv_w, ssd_conv_b, ssd_dt_bias, ssd_a_log, ssd_d, ssd_norm, ssd_w_out, mla_w_in, mla_q_a_norm, mla_w_q_b, mla_kv_a_norm, mla_w_kv_b, mla_q_norm, mla_k_norm, mla_w_out, mlp_w_up, mlp_w_down, loss_target, m_meta_tokens, m_ln_mix, m_ln_mlp, m_ssd_w_in, m_ssd_conv_w, m_ssd_conv_b, m_ssd_dt_bias, m_ssd_a_log, m_ssd_d, m_ssd_norm, m_ssd_w_out, m_mla_w_in, m_mla_q_a_norm, m_mla_w_q_b, m_mla_kv_a_norm, m_mla_w_kv_b, m_mla_q_norm, m_mla_k_norm, m_mla_w_out, m_mlp_w_up, m_mlp_w_down, v_meta_tokens, v_ln_mix, v_ln_mlp, v_ssd_w_in, v_ssd_conv_w, v_ssd_conv_b, v_ssd_dt_bias, v_ssd_a_log, v_ssd_d, v_ssd_norm, v_ssd_w_out, v_mla_w_in, v_mla_q_a_norm, v_mla_w_q_b, v_mla_kv_a_norm, v_mla_w_kv_b, v_mla_q_norm, v_mla_k_norm, v_mla_w_out, v_mlp_w_up, v_mlp_w_down):
    given = dict(locals())
    W = {n: given[n] for n in WEIGHTS}
    M = {n: given["m_" + n] for n in WEIGHTS}
    V = {n: given["v_" + n] for n in WEIGHTS}
    return _step(x[0], loss_target[0], W, M, V)
```

```python
import functools

import jax
import jax.numpy as jnp
import numpy as np
from jax import lax
from jax.experimental import pallas as pl
from jax.experimental.pallas import tpu as pltpu

F32 = jnp.float32
BF16 = jnp.bfloat16

EPS = 1e-6
N_META = 16
CHUNK = 128
PAD = CHUNK - N_META
SSD_HEAD_DIM = 64
SSD_HEADS = 32
SSD_GROUPS = 8
SSD_HPG = 4
SSD_STATE = 128
SSD_D_INNER = 2048
SSD_CONV = 4
MLA_HEADS = 16
MLA_NOPE = 64
MLA_ROPE = 32
MLA_V = 64
MLA_QK = 96
MLA_Q_RANK = 384
MLA_KV_RANK = 256
ROPE_THETA = 10000.0
LANES = 128
SUBLANES = 8
N_DEV = 8
VMEM_LIMIT = 56 * 1024 * 1024

ADAM_LR = 0.001
ADAM_B1 = 0.9
ADAM_B2 = 0.999
ADAM_EPS = 1e-08
ADAM_WD = 0.01
ADAM_STEP = 10

NEG = -1e30


def _row_tile(rows):
    return 384 if (rows % 384 == 0 and rows > 384) else 128


def _big_tile(rows):
    return 1408 if rows % 1408 == 0 else _row_tile(rows)


def _params(n_axes, vmem=VMEM_LIMIT):
    return pltpu.CompilerParams(dimension_semantics=("arbitrary",) * n_axes, vmem_limit_bytes=vmem)


def _dot(a, b, dims):
    return lax.dot_general(a.astype(BF16), b.astype(BF16), (dims, ((), ())), preferred_element_type=F32)


NN = ((1,), (0,))
NT = ((1,), (1,))
TN = ((0,), (0,))


@jax.custom_vjp
def bdot_nn(a, b):
    return _dot(a, b, NN)


@jax.custom_vjp
def bdot_nt(a, b):
    return _dot(a, b, NT)


@jax.custom_vjp
def bdot_tn(a, b):
    return _dot(a, b, TN)


bdot_nn.defvjp(lambda a, b: (_dot(a, b, NN), (a, b)),
               lambda r, g: (_dot(g, r[1], NT), _dot(r[0], g, TN)))
bdot_nt.defvjp(lambda a, b: (_dot(a, b, NT), (a, b)),
               lambda r, g: (_dot(g, r[1], NN), _dot(g, r[0], TN)))
bdot_tn.defvjp(lambda a, b: (_dot(a, b, TN), (a, b)),
               lambda r, g: (_dot(r[1], g, NT), _dot(r[0], g, NN)))


def _rows8(v):
    r, n = v.shape
    return v.reshape(r // SUBLANES, SUBLANES, n).sum(axis=0)


def _row_mask(i, tm):
    return (i * tm + lax.broadcasted_iota(jnp.int32, (tm, 1), 0)) >= PAD


def fused_mm(name, *, rows, k, n, tm, tn, tk=None, a_ops, pro, w, w_block, w_imap, dot, e_ops=(), epi, outs):
    tk = tk or k
    ni, nj, nk = rows // tm, n // tn, k // tk
    assert rows % tm == 0 and n % tn == 0 and k % tk == 0
    assert nk == 1 or nj == 1
    cache = nk == 1 and nj > 1
    na, ne, no = len(a_ops), len(e_ops), len(outs)

    def body(*refs):
        a_refs = refs[:na]
        w_ref = refs[na]
        e_refs = refs[na + 1:na + 1 + ne]
        o_refs = refs[na + 1 + ne:na + 1 + ne + no]
        scr = refs[na + 1 + ne + no:]
        i, j, kk = pl.program_id(0), pl.program_id(1), pl.program_id(2)
        if cache:
            a_scr = scr[0]

            @pl.when(j == 0)
            def _():
                a_scr[...] = pro(a_refs, o_refs, i)

            a = a_scr[...]
        else:
            a = pro(a_refs, o_refs, i)
        part = dot(a, w_ref)
        if nk == 1:
            epi(part, e_refs, o_refs, i, j)
        else:
            acc_ref = scr[0]

            @pl.when(kk == 0)
            def _():
                acc_ref[...] = part

            @pl.when(kk > 0)
            def _():
                acc_ref[...] += part

            @pl.when(kk == nk - 1)
            def _():
                epi(acc_ref[...], e_refs, o_refs, i, j)

    scratch = []
    if cache:
        scratch.append(pltpu.VMEM((tm, k), BF16))
    if nk > 1:
        scratch.append(pltpu.VMEM((tm, tn), F32))
    in_specs = [pl.BlockSpec(b, m) for (_, b, m) in a_ops]
    in_specs.append(pl.BlockSpec(w_block, w_imap))
    in_specs += [pl.BlockSpec(b, m) for (_, b, m) in e_ops]
    return pl.pallas_call(
        body, name=name, grid=(ni, nj, nk),
        in_specs=in_specs,
        out_specs=[pl.BlockSpec(b, m) for (_, _, b, m) in outs],
        out_shape=[jax.ShapeDtypeStruct(s, d) for (s, d, _, _) in outs],
        scratch_shapes=scratch,
        compiler_params=_params(3),
    )(*[a for (a, _, _) in a_ops], w, *[e for (e, _, _) in e_ops])


def _lw(w, block, imap):
    if isinstance(w, tuple):
        arr, layer = w
        return arr, (None,) + block, (lambda i, j, kk: (layer,) + imap(i, j, kk))
    return w, block, imap


def _dot_w(a, w_ref):
    return jnp.dot(a, w_ref[...], preferred_element_type=F32)


def _dot_wt(a, w_ref):
    return lax.dot_general(a, w_ref[...], (NT, ((), ())), preferred_element_type=F32)


def _rms_pro(h, g):
    r = lax.rsqrt(jnp.mean(h * h, axis=-1, keepdims=True) + EPS)
    return h * r * g


def _rms_bwd(dyn, h, g):
    r = lax.rsqrt(jnp.mean(h * h, axis=-1, keepdims=True) + EPS)
    xh = h * r
    t = dyn * g
    dh = r * (t - xh * jnp.mean(t * xh, axis=-1, keepdims=True))
    return dh, _rows8(dyn * xh)


def _acc_out(ref, val, first):
    @pl.when(first)
    def _():
        ref[...] = val

    @pl.when(jnp.logical_not(first))
    def _():
        ref[...] += val


def norm_mm(name, h, g, w, *, tn, k_cols=None, col_block=0, w_stacked=False):
    rows = h.shape[0]
    k = k_cols or h.shape[1]
    wshape = (w[0].shape[1:] if isinstance(w, tuple) else w.shape)
    n = wshape[0] * wshape[2] if w_stacked else wshape[1]
    tm = _big_tile(rows)

    def pro(a_refs, o_refs, i):
        hn = _rms_pro(a_refs[0][...], a_refs[1][...]).astype(BF16)
        o_refs[1][...] = hn
        return hn

    def epi(acc, e_refs, o_refs, i, j):
        o_refs[0][...] = acc

    if w_stacked:
        w_block, w_imap = (None, k, tn), (lambda i, j, kk: (j, 0, 0))
    else:
        w_block, w_imap = (k, tn), (lambda i, j, kk: (0, j))
    w, w_block, w_imap = _lw(w, w_block, w_imap)
    return fused_mm(
        name, rows=rows, k=k, n=n, tm=tm, tn=tn,
        a_ops=[(h, (tm, k), lambda i, j, kk: (i, col_block)), (g, (1, k), lambda i, j, kk: (0, 0))],
        pro=pro, w=w, w_block=w_block, w_imap=w_imap, dot=_dot_w, epi=epi,
        outs=[((rows, n), F32, (tm, tn), lambda i, j, kk: (i, j)),
              ((rows, k), BF16, (tm, k), lambda i, j, kk: (i, 0))])


def res_mm(name, a_ops, pro, k, w, res, *, tn, save_dtype=None, tm=None, tk=None):
    rows, n = res.shape
    tm = tm or _row_tile(rows)
    assert tk is None or save_dtype is None

    def pro2(a_refs, o_refs, i):
        a = pro(a_refs)
        if save_dtype is not None:
            o_refs[1][...] = a
        return a

    def epi(acc, e_refs, o_refs, i, j):
        o_refs[0][...] = e_refs[0][...] + acc

    outs = [((rows, n), F32, (tm, tn), lambda i, j, kk: (i, j))]
    if save_dtype is not None:
        outs.append(((rows, k), save_dtype, (tm, k), lambda i, j, kk: (i, 0)))
    w, w_block, w_imap = _lw(w, (tk or k, tn), lambda i, j, kk: (kk, j))
    out = fused_mm(
        name, rows=rows, k=k, n=n, tm=tm, tn=tn, tk=tk,
        a_ops=[(a, tuple(tm if x is None else x for x in b), m) for (a, b, m) in a_ops],
        pro=pro2, w=w, w_block=w_block, w_imap=w_imap, dot=_dot_w,
        e_ops=[(res, (tm, tn), lambda i, j, kk: (i, j))], epi=epi, outs=outs)
    return out if save_dtype is not None else out[0]


def wgrad_mm(name, a_ops, pro_a, g_ops, pro_g, *, rows, k1, n, t1, tn, out_shape=None, out_block=None, out_imap=None):
    tt = _row_tile(rows)
    n1, n2, nt = k1 // t1, n // tn, rows // tt
    assert k1 % t1 == 0 and n % tn == 0
    na = len(a_ops)

    def body(*refs):
        a_refs = refs[:na]
        g_refs = refs[na:-2]
        o_ref, acc = refs[-2:]
        t = pl.program_id(2)
        a = pro_a(a_refs).astype(BF16)
        g = pro_g(g_refs).astype(BF16)
        _acc_out(acc, lax.dot_general(a, g, (TN, ((), ())), preferred_element_type=F32), t == 0)

        @pl.when(t == nt - 1)
        def _():
            if len(o_ref.shape) == 3:
                ws = o_ref.shape[2]
                for q in range(o_ref.shape[0]):
                    o_ref[q] = acc[:, q * ws:(q + 1) * ws].astype(BF16)
            else:
                o_ref[...] = acc[...].astype(BF16)

    return pl.pallas_call(
        body, name=name, grid=(n1, n2, nt),
        in_specs=[pl.BlockSpec(b, m) for (_, b, m) in list(a_ops) + list(g_ops)],
        out_specs=pl.BlockSpec(out_block or (t1, tn), out_imap or (lambda a, b, t: (a, b))),
        out_shape=jax.ShapeDtypeStruct(out_shape or (k1, n), BF16),
        scratch_shapes=[pltpu.VMEM((t1, tn), F32)],
        compiler_params=_params(3),
    )(*[a for (a, _, _) in list(a_ops) + list(g_ops)])


def simple_wgrad(name, a, g, *, a_cols=None, a_col_block=0, t1=None, tn=None, **kw):
    rows = a.shape[0]
    k1 = a_cols or a.shape[1]
    n = g.shape[1]
    tt = _row_tile(rows)
    t1 = t1 or min(k1, 512)
    tn = tn or min(n, 1024)
    return wgrad_mm(
        name,
        [(a, (tt, t1), lambda x, y, t: (t, x + a_col_block * (k1 // t1)))], lambda r: r[0][...],
        [(g, (tt, tn), lambda x, y, t: (t, y))], lambda r: r[0][...],
        rows=rows, k1=k1, n=n, t1=t1, tn=tn, **kw)


def rms_bwd_mm(name, dz_ops, pro, k, w, w_block, w_imap, dot, h, g, dh, *, tk=None, h_cols=None, h_col_block=0,
               add_dh=True, tm=None):
    rows = h.shape[0]
    n = h_cols or h.shape[1]
    tm = tm or _big_tile(rows)
    ni = rows // tm
    w, w_block, w_imap = _lw(w, w_block, w_imap)

    def epi(acc, e_refs, o_refs, i, j):
        d, dg = _rms_bwd(acc, e_refs[0][...], e_refs[1][...])
        if add_dh:
            d = d + e_refs[2][...]
        o_refs[0][...] = jnp.where(_row_mask(i, tm), d, 0.0)
        _acc_out(o_refs[1], dg, i == 0)

    e_ops = [(h, (tm, n), lambda i, j, kk: (i, h_col_block)), (g, (1, n), lambda i, j, kk: (0, 0))]
    if add_dh:
        e_ops.append((dh, (tm, n), lambda i, j, kk: (i, 0)))
    return fused_mm(
        name, rows=rows, k=k, n=n, tm=tm, tn=n, tk=tk,
        a_ops=[(a, tuple(tm if x is None else x for x in b), m) for (a, b, m) in dz_ops],
        pro=lambda a_refs, o_refs, i: pro(a_refs), w=w, w_block=w_block, w_imap=w_imap, dot=dot,
        e_ops=e_ops, epi=epi,
        outs=[((rows, n), F32, (tm, n), lambda i, j, kk: (i, 0)),
              ((SUBLANES, n), F32, (SUBLANES, n), lambda i, j, kk: (0, 0))])


def _relu2(u):
    r = jnp.maximum(u, 0.0)
    return r * r


def mlp_fwd(tag, h, g, w_up_st, w_down):
    d_ff = w_down.shape[0]
    u, hn = norm_mm(f"mlp_up_{tag}", h, g, w_up_st, tn=w_up_st.shape[2], w_stacked=True)
    out = res_mm(f"mlp_down_{tag}", [(u, (None, 512), lambda i, j, kk: (i, kk))],
                 lambda r: _relu2(r[0][...]).astype(BF16), d_ff, w_down, h, tn=h.shape[1],
                 tm=_big_tile(h.shape[0]), tk=512)
    return out, (h, hn, u)


def mlp_bwd(tag, dh, saved, g, w_up_st, w_down):
    h, hn, u = saved
    rows, d = h.shape
    d_ff = w_down.shape[0]
    ts = w_up_st.shape[2]
    tm = _big_tile(rows)
    tt = _row_tile(rows)
    wd, wd_block, wd_imap = _lw(w_down, (512, d), lambda i, j, kk: (j, 0))

    def epi_du(acc, e_refs, o_refs, i, j):
        o_refs[0][...] = (acc * (2.0 * jnp.maximum(e_refs[0][...], 0.0))).astype(BF16)

    du, = fused_mm(
        f"mlp_du_{tag}", rows=rows, k=d, n=d_ff, tm=tm, tn=512,
        a_ops=[(dh, (tm, d), lambda i, j, kk: (i, 0))], pro=lambda a, o, i: a[0][...].astype(BF16),
        w=wd, w_block=wd_block, w_imap=wd_imap, dot=_dot_wt,
        e_ops=[(u, (tm, 512), lambda i, j, kk: (i, j))], epi=epi_du,
        outs=[((rows, d_ff), BF16, (tm, 512), lambda i, j, kk: (i, j))])
    half = d_ff // 2
    dw_down = wgrad_mm(
        f"mlp_dwdown_{tag}",
        [(u, (tt, half), lambda a, b, t: (t, a))], lambda r: _relu2(r[0][...]),
        [(dh, (tt, d), lambda a, b, t: (t, 0))], lambda r: r[0][...],
        rows=rows, k1=d_ff, n=d, t1=half, tn=d)
    dw_up = simple_wgrad(f"mlp_dwup_{tag}", hn, du, t1=d, tn=half, out_shape=(N_DEV, d, ts),
                         out_block=(half // ts, d, ts), out_imap=lambda a, b, t: (b, 0, 0))
    dh_in, dg = rms_bwd_mm(
        f"mlp_dh_{tag}", [(du, (None, ts), lambda i, j, kk: (i, kk))], lambda r: r[0][...], d_ff,
        w_up_st, (None, d, ts), lambda i, j, kk: (kk, 0, 0), _dot_wt, h, g, dh, tk=ts)
    return dh_in, dw_up, dw_down, dg


CONV_HALO = SUBLANES
CONV_W = 2 * SSD_D_INNER
CONV_BLK = 512


def _silu(x):
    return x * jax.nn.sigmoid(x)


def _conv_pre(ext_ref, w, b, tm):
    pre = b
    for k in range(SSD_CONV):
        pre = pre + w[k:k + 1, :] * ext_ref[pl.ds(CONV_HALO - (SSD_CONV - 1) + k, tm), :]
    return pre


def _conv_load(ext, parts, c):
    for cur, halo, lo, hi in parts:
        ext[pl.ds(0, CONV_HALO), lo:hi] = jnp.where(c > 0, halo[...], 0.0)
        ext[pl.ds(CONV_HALO, CHUNK), lo:hi] = cur[...]


def _conv_specs(idx):
    hb = CHUNK // CONV_HALO
    specs = []
    for w, blk in ((SSD_D_INNER, 1), (BC_W, SSD_D_INNER // BC_W + 2), (BC_W, SSD_D_INNER // BC_W + 3)):
        specs.append(pl.BlockSpec((CHUNK, w), lambda s, blk=blk: (idx(s), blk)))
        specs.append(pl.BlockSpec((CONV_HALO, w), lambda s, blk=blk: (jnp.maximum(idx(s) * hb - 1, 0), blk)))
    return specs


CONV_COLS = ((0, SSD_D_INNER), (SSD_D_INNER, SSD_D_INNER + SSD_GROUPS * SSD_STATE),
             (SSD_D_INNER + SSD_GROUPS * SSD_STATE, 2 * SSD_D_INNER))


@functools.partial(jax.custom_vjp, nondiff_argnums=(1,))
def _sub_row(x, h):
    return x[h:h + 1, :]


_sub_row.defvjp(
    lambda x, h: (x[h:h + 1, :], None),
    lambda h, _, g: (jnp.where(lax.broadcasted_iota(jnp.int32, (LANES, 1), 0) == h, g, 0.0),))


def _splitter(axis, size, count):
    def blocks(x):
        return tuple(lax.slice_in_dim(x, q * size, (q + 1) * size, axis=axis) for q in range(count))

    split = jax.custom_vjp(blocks)
    split.defvjp(lambda x: (blocks(x), None), lambda _, gs: (jnp.concatenate(gs, axis=axis),))
    return split


def _split3(x):
    hi = x.astype(BF16)
    r = x - hi.astype(F32)
    mid = r.astype(BF16)
    return hi, mid, (r - mid.astype(F32)).astype(BF16)


def _expand_impl(x, e):
    return sum(jnp.dot(t, e, preferred_element_type=F32) for t in _split3(x))


@jax.custom_vjp
def _expand(x, e):
    return _expand_impl(x, e)


def _expand_bwd(e, g):
    hi, mid, _ = _split3(g)
    dx = sum(lax.dot_general(t, e, (NT, ((), ())), preferred_element_type=F32) for t in (hi, mid))
    return dx, jnp.zeros_like(e)


_expand.defvjp(lambda x, e: (_expand_impl(x, e), e), _expand_bwd)

HEAD_PAIR = 2 * SSD_HEAD_DIM
GROUP_W = SSD_HPG * SSD_HEAD_DIM


def _ssd_chunk(xs, bm, cm, dtraw, prev, par, c, tri, e64, e128):
    li = lax.broadcasted_iota(jnp.int32, (CHUNK, CHUNK), 0)
    si = lax.broadcasted_iota(jnp.int32, (CHUNK, CHUNK), 1)
    causal = li >= si
    first_head = lax.broadcasted_iota(jnp.int32, (1, HEAD_PAIR), 1) < SSD_HEAD_DIM
    dt = jnp.where(_row_mask(c, CHUNK), jax.nn.softplus(dtraw + par[0:1, :]), 0.0)
    a = -jnp.exp(par[1:2, :])
    acs = jnp.dot(tri, dt * a, precision=lax.Precision.HIGHEST, preferred_element_type=F32)
    acs_t = acs.T
    last = acs[CHUNK - 1:CHUNK, :]
    misc = jnp.concatenate([jnp.exp(last), par[2:3, :], jnp.zeros((SUBLANES - 2, LANES), F32)], axis=0)
    wide = _expand(jnp.concatenate([dt, dt * jnp.exp(last - acs), jnp.exp(acs)], axis=0), e64)
    dt_w, dtend_w, start_w = _splitter(0, CHUNK, 3)(wide)
    misc_w = _expand(misc, e64)
    col_w = _splitter(1, CHUNK, SSD_HEADS)(_expand(acs, e128))
    groups = _splitter(1, GROUP_W, SSD_GROUPS)
    xs_g, prev_g, start_g = groups(xs), groups(prev), groups(start_w)
    xdt_p = _splitter(1, HEAD_PAIR, SSD_HEADS // 2)(xs * dt_w)
    xdtend_g = groups(xs * dtend_w)
    last_g, skip_g = groups(misc_w[0:1, :]), groups(misc_w[1:2, :])
    b_g, c_g = _splitter(1, SSD_STATE, SSD_GROUPS)(bm), _splitter(1, SSD_STATE, SSD_GROUPS)(cm)
    ys, news = [], []
    for g in range(SSD_GROUPS):
        cb = bdot_nt(c_g[g], b_g[g])
        st = bdot_nn(b_g[g].T, xdtend_g[g])
        y_off = bdot_nn(c_g[g], prev_g[g]) * start_g[g]
        pairs = []
        for q in range(SSD_HPG // 2):
            xp = xdt_p[g * (SSD_HPG // 2) + q]
            acc = None
            for r in range(2):
                head = SSD_HPG * g + 2 * q + r
                seg = jnp.where(causal, col_w[head] - _sub_row(acs_t, head), 0.0)
                decay = jnp.where(causal, jnp.exp(seg), 0.0)
                t = bdot_nn(cb * decay, jnp.where(first_head if r == 0 else jnp.logical_not(first_head), xp, 0.0))
                acc = t if acc is None else acc + t
            pairs.append(acc)
        ys.append(jnp.concatenate(pairs, axis=1) + y_off + xs_g[g] * skip_g[g])
        news.append(prev_g[g] * last_g[g] + st)
    return jnp.concatenate(ys, axis=1), jnp.concatenate(news, axis=1)


def _expanders():
    e64 = np.zeros((LANES, SSD_D_INNER), np.float32)
    e128 = np.zeros((LANES, SSD_HEADS * CHUNK), np.float32)
    for h in range(SSD_HEADS):
        e64[h, h * SSD_HEAD_DIM:(h + 1) * SSD_HEAD_DIM] = 1.0
        e128[h, h * CHUNK:(h + 1) * CHUNK] = 1.0
    return jnp.asarray(e64, BF16), jnp.asarray(e128, BF16)


def _tri():
    return jnp.asarray(np.tril(np.ones((CHUNK, CHUNK), np.float32)))


BC_W = SSD_GROUPS * SSD_STATE


def _conv_act(ext, refs, w, b, c):
    _conv_load(ext, [(refs[2 * p], refs[2 * p + 1]) + CONV_COLS[p] for p in range(3)], c)
    pre = _conv_pre(ext, w, b, CHUNK)
    return pre, jnp.where(_row_mask(c, CHUNK), _silu(pre), 0.0)


def ssd_fwd(name, zx, dt_block, par, conv_w, conv_b, comm=None):
    rows = zx.shape[0]
    nc = rows // CHUNK

    def body(*refs):
        xbc_refs, (dt_ref, par_ref, tri_ref, e64_ref, e128_ref, cw_ref, cb_ref) = refs[:6], refs[6:13]
        y_ref, st_ref, state, ext = refs[13:]
        c = pl.program_id(0)

        @pl.when(c == 0)
        def _():
            state[...] = jnp.zeros((SSD_STATE, SSD_D_INNER), F32)

        prev = state[...]
        st_ref[...] = prev
        _, act = _conv_act(ext, xbc_refs, cw_ref[...], cb_ref[...], c)
        (x0, x1), (b0, b1), (c0, c1) = CONV_COLS
        y, new = _ssd_chunk(act[:, x0:x1], act[:, b0:b1], act[:, c0:c1], dt_ref[...], prev, par_ref[...], c,
                            tri_ref[...], e64_ref[...], e128_ref[...])
        y_ref[...] = y
        state[...] = new

    const = lambda a: pl.BlockSpec(a.shape, lambda c: (0,) * a.ndim)
    consts = (par, _tri()) + _expanders() + (conv_w, conv_b)
    return carrier_call(
        name, body, (nc,),
        _conv_specs(lambda c: c) + [pl.BlockSpec((CHUNK, LANES), lambda c: (c, dt_block))] + [const(a) for a in consts],
        [pl.BlockSpec((CHUNK, SSD_D_INNER), lambda c: (c, 0)),
         pl.BlockSpec((None, SSD_STATE, SSD_D_INNER), lambda c: (c, 0, 0))],
        [jax.ShapeDtypeStruct((rows, SSD_D_INNER), F32),
         jax.ShapeDtypeStruct((nc, SSD_STATE, SSD_D_INNER), F32)],
        [pltpu.VMEM((SSD_STATE, SSD_D_INNER), F32), pltpu.VMEM((CHUNK + CONV_HALO, CONV_W), F32)],
        (zx,) * 7 + consts, comm)


def carrier_call(name, body, grid, in_specs, out_specs, out_shape, scratch_shapes, args, comm):
    body, in_specs, out_specs, out_shape, scratch_shapes, extra, n_own = with_comm(
        comm, body, grid, in_specs, out_specs, out_shape, scratch_shapes)
    res = pl.pallas_call(
        body, name=name, grid=grid, in_specs=in_specs, out_specs=out_specs, out_shape=out_shape,
        scratch_shapes=scratch_shapes, compiler_params=_params(len(grid)))(*args, *extra)
    return res[:n_own], res[n_own:]


def ssd_bwd(name, dy, zx, dt_block, states, par, conv_w, conv_b, comm=None):
    rows = zx.shape[0]
    nc = rows // CHUNK

    def body(*refs):
        dy_ref, xbc_refs = refs[0], refs[1:7]
        dt_ref, st_ref, par_ref, tri_ref, e64_ref, e128_ref, cw_ref, cb_ref = refs[7:15]
        du_ref, ddt_ref, dpar_ref, dcw_ref, dcb_ref, dstate, ext, dext = refs[15:]
        s = pl.program_id(0)
        c = nc - 1 - s

        @pl.when(s == 0)
        def _():
            dstate[...] = jnp.zeros((SSD_STATE, SSD_D_INNER), F32)
            dext[pl.ds(CHUNK, CONV_HALO), :] = jnp.zeros((CONV_HALO, CONV_W), F32)

        _, act = _conv_act(ext, xbc_refs, cw_ref[...], cb_ref[...], c)
        (x0, x1), (b0, b1), (c0, c1) = CONV_COLS

        def f(xs, bm, cm, dtraw, prev, par_v):
            return _ssd_chunk(xs, bm, cm, dtraw, prev, par_v, c, tri_ref[...], e64_ref[...], e128_ref[...])

        _, vjp = jax.vjp(f, act[:, x0:x1], act[:, b0:b1], act[:, c0:c1], dt_ref[...], st_ref[...], par_ref[...])
        dxs, dbm, dcm, ddt, dprev, dpar = vjp((dy_ref[...], dstate[...]))
        ddt_ref[...] = ddt.astype(BF16)
        dstate[...] = dprev
        _acc_out(dpar_ref, dpar, s == 0)

        dacts = [dxs[:, q * CONV_BLK:(q + 1) * CONV_BLK] for q in range((x1 - x0) // CONV_BLK)]
        dacts += [dbm[:, q * CONV_BLK:(q + 1) * CONV_BLK] for q in range((b1 - b0) // CONV_BLK)]
        dacts += [dcm[:, q * CONV_BLK:(q + 1) * CONV_BLK] for q in range((c1 - c0) // CONV_BLK)]
        for q, dact in enumerate(dacts):
            cols = slice(q * CONV_BLK, (q + 1) * CONV_BLK)
            w = cw_ref[:, cols]
            taps = [ext[pl.ds(CONV_HALO - (SSD_CONV - 1) + k, CHUNK), cols] for k in range(SSD_CONV)]
            pre = cb_ref[:, cols]
            for k in range(SSD_CONV):
                pre = pre + w[k:k + 1, :] * taps[k]
            sg = jax.nn.sigmoid(pre)
            dpre = jnp.where(_row_mask(c, CHUNK), dact * (sg * (1.0 + pre * (1.0 - sg))), 0.0)
            dext[pl.ds(0, CHUNK), cols] = dpre
            du = jnp.zeros((CHUNK, CONV_BLK), F32)
            for k in range(SSD_CONV):
                du = du + w[k:k + 1, :] * dext[pl.ds(SSD_CONV - 1 - k, CHUNK), cols]
            du_ref[:, cols] = du.astype(BF16)
            _acc_out(dcb_ref.at[:, cols], _rows8(dpre), s == 0)
            for k in range(SSD_CONV):
                _acc_out(dcw_ref.at[pl.ds(k * SUBLANES, SUBLANES), cols], _rows8(dpre * taps[k]), s == 0)
            dext[pl.ds(CHUNK, CONV_HALO), cols] = dpre[0:CONV_HALO, :]

    rev = lambda w, b: pl.BlockSpec((CHUNK, w), lambda s: (nc - 1 - s, b))
    const = lambda a: pl.BlockSpec(a.shape, lambda s: (0,) * a.ndim)
    consts = (par, _tri()) + _expanders() + (conv_w, conv_b)
    return carrier_call(
        name, body, (nc,),
        [rev(SSD_D_INNER, 0)] + _conv_specs(lambda s: nc - 1 - s)
        + [rev(LANES, dt_block), pl.BlockSpec((None, SSD_STATE, SSD_D_INNER), lambda s: (nc - 1 - s, 0, 0))]
        + [const(a) for a in consts],
        [rev(CONV_W, 0), rev(LANES, 0), pl.BlockSpec((SUBLANES, LANES), lambda s: (0, 0)),
         pl.BlockSpec((SSD_CONV * SUBLANES, CONV_W), lambda s: (0, 0)),
         pl.BlockSpec((SUBLANES, CONV_W), lambda s: (0, 0))],
        [jax.ShapeDtypeStruct((rows, CONV_W), BF16),
         jax.ShapeDtypeStruct((rows, LANES), BF16),
         jax.ShapeDtypeStruct((SUBLANES, LANES), F32),
         jax.ShapeDtypeStruct((SSD_CONV * SUBLANES, CONV_W), F32),
         jax.ShapeDtypeStruct((SUBLANES, CONV_W), F32)],
        [pltpu.VMEM((SSD_STATE, SSD_D_INNER), F32), pltpu.VMEM((CHUNK + CONV_HALO, CONV_W), F32),
         pltpu.VMEM((CHUNK + CONV_HALO, CONV_W), F32)],
        (dy,) + (zx,) * 7 + (states,) + consts, comm)


GN_W = SSD_D_INNER // SSD_GROUPS


def _gated_norm(y, z, ng):
    g = y * _silu(z)
    outs = []
    for q in range(SSD_GROUPS):
        gs = g[:, q * GN_W:(q + 1) * GN_W]
        outs.append(gs * lax.rsqrt(jnp.mean(gs * gs, axis=-1, keepdims=True) + EPS))
    return jnp.concatenate(outs, axis=1) * ng


def ssd_layer_fwd(tag, h, ln_g, w, comm=None):
    zx, hn = norm_mm(f"ssd_in_{tag}", h, ln_g, w["w_in"], tn=896)
    dt_block = 3 * SSD_D_INNER // LANES
    (y, states), cres = ssd_fwd(f"ssd_scan_{tag}", zx, dt_block, w["par"], w["conv_w"], w["conv_b"], comm)
    out, gn = res_mm(
        f"ssd_out_{tag}",
        [(y, (None, SSD_D_INNER), lambda i, j, kk: (i, 0)), (zx, (None, SSD_D_INNER), lambda i, j, kk: (i, 0)),
         (w["norm"], (1, SSD_D_INNER), lambda i, j, kk: (0, 0))],
        lambda r: _gated_norm(r[0][...], r[1][...], r[2][...]).astype(BF16),
        SSD_D_INNER, w["w_out"], h, tn=512, save_dtype=BF16)
    return out, (h, hn, zx, y, states, gn), cres


def ssd_layer_bwd(tag, dh, saved, ln_g, w, comm=None):
    h, hn, zx, y, states, gn = saved
    rows, d = h.shape
    tm = _row_tile(rows)
    dt_block = 3 * SSD_D_INNER // LANES
    dw_out = simple_wgrad(f"ssd_dwout_{tag}", gn, dh, t1=SSD_D_INNER, tn=d)

    def epi_gate(acc, e_refs, o_refs, i, j):
        _, vjp = jax.vjp(_gated_norm, e_refs[0][...], e_refs[1][...], e_refs[2][...])
        dy, dz, dng = vjp(acc)
        o_refs[0][...] = dy
        o_refs[1][...] = dz.astype(BF16)
        row0 = lax.broadcasted_iota(jnp.int32, (SUBLANES, 1), 0) == 0
        _acc_out(o_refs[2], jnp.where(row0, dng, 0.0), i == 0)

    wo, wo_block, wo_imap = _lw(w["w_out"], (SSD_D_INNER, d), lambda i, j, kk: (0, 0))
    dy, dz, dnorm = fused_mm(
        f"ssd_dgate_{tag}", rows=rows, k=d, n=SSD_D_INNER, tm=tm, tn=SSD_D_INNER,
        a_ops=[(dh, (tm, d), lambda i, j, kk: (i, 0))], pro=lambda a, o, i: a[0][...].astype(BF16),
        w=wo, w_block=wo_block, w_imap=wo_imap, dot=_dot_wt,
        e_ops=[(y, (tm, SSD_D_INNER), lambda i, j, kk: (i, 0)), (zx, (tm, SSD_D_INNER), lambda i, j, kk: (i, 0)),
               (w["norm"], (1, SSD_D_INNER), lambda i, j, kk: (0, 0))],
        epi=epi_gate,
        outs=[((rows, SSD_D_INNER), F32, (tm, SSD_D_INNER), lambda i, j, kk: (i, 0)),
              ((rows, SSD_D_INNER), BF16, (tm, SSD_D_INNER), lambda i, j, kk: (i, 0)),
              ((SUBLANES, SSD_D_INNER), F32, (SUBLANES, SSD_D_INNER), lambda i, j, kk: (0, 0))])
    (dxbc, ddt, dpar, dcw, dcb), cres = ssd_bwd(f"ssd_dscan_{tag}", dy, zx, dt_block, states, w["par"],
                                                w["conv_w"], w["conv_b"], comm)
    dzx = jnp.concatenate([dz, dxbc, ddt], axis=1)
    k = dzx.shape[1]
    dw_in = simple_wgrad(f"ssd_dwin_{tag}", hn, dzx, t1=d, tn=896)
    dh_in, dln = rms_bwd_mm(
        f"ssd_dh_{tag}", [(dzx, (None, 896), lambda i, j, kk: (i, kk))], lambda r: r[0][...], k,
        w["w_in"], (d, 896), lambda i, j, kk: (0, kk), _dot_wt, h, ln_g, dh, tk=896,
        tm=704 if rows % 704 == 0 else None)
    grads = dict(w_in=dw_in, w_out=dw_out, conv_w=dcw, conv_b=dcb, par=dpar, norm=dnorm, ln=dln)
    return dh_in, grads, cres


HP = 2 * LANES
VP = 2 * MLA_V
N_PAIRS = MLA_HEADS // 2
ATT_SCALE = MLA_QK ** -0.5
ROT = MLA_ROPE // 2


def rope_tables(rows):
    inv = 1.0 / (ROPE_THETA ** (jnp.arange(0, MLA_ROPE, 2, dtype=F32) / MLA_ROPE))
    pos = jnp.arange(rows, dtype=F32) - PAD
    ang = pos[:, None] * inv[None, :]
    cos, sin = jnp.cos(ang), jnp.sin(ang)
    one = jnp.ones((rows, MLA_NOPE), F32)
    zero = jnp.zeros((rows, LANES - MLA_QK), F32)
    zn = jnp.zeros((rows, MLA_NOPE), F32)
    zr = jnp.zeros((rows, ROT), F32)
    cosf = jnp.concatenate([one, cos, cos, zero], axis=1)
    sina = jnp.concatenate([zn, -sin, zr, zero], axis=1)
    sinb = jnp.concatenate([zn, zr, sin, zero], axis=1)
    return cosf, sina, sinb


def _qk_norm_rope(x, g, cosf, sina, sinb):
    r = lax.rsqrt(jnp.sum(x * x, axis=-1, keepdims=True) * (1.0 / MLA_QK) + EPS)
    xn = x * r * g
    return xn * cosf + pltpu.roll(xn, LANES - ROT, 1) * sina + pltpu.roll(xn, ROT, 1) * sinb


def _qk_norm_rope_bwd(dout, x, g, cosf, sina, sinb):
    dxn = dout * cosf + pltpu.roll(dout * sina, ROT, 1) + pltpu.roll(dout * sinb, LANES - ROT, 1)
    r = lax.rsqrt(jnp.sum(x * x, axis=-1, keepdims=True) * (1.0 / MLA_QK) + EPS)
    xh = x * r
    t = dxn * g
    dx = r * (t - xh * (jnp.sum(t * xh, axis=-1, keepdims=True) * (1.0 / MLA_QK)))
    return dx, _rows8(dxn * xh)


def _rope_lanes():
    lane = lax.broadcasted_iota(jnp.int32, (1, LANES), 1)
    return jnp.logical_and(lane >= MLA_NOPE, lane < MLA_QK)


QW = MLA_HEADS * LANES
VW = MLA_HEADS * MLA_V


def qk_prep(name, qraw, kvraw, lat, kpe_block, qg, kg, tabs):
    rows = qraw.shape[0]
    tm = _row_tile(rows)

    def body(q_ref, k0_ref, k1_ref, v_ref, pe_ref, qg_ref, kg_ref, c_ref, sa_ref, sb_ref,
             qo_ref, ko_ref, kt_ref, vo_ref, vt_ref):
        tab = (c_ref[...], sa_ref[...], sb_ref[...])
        pe = pe_ref[...]
        for hd in range(MLA_HEADS):
            sl = slice(hd * LANES, (hd + 1) * LANES)
            qo_ref[:, sl] = _qk_norm_rope(q_ref[:, sl], qg_ref[...], *tab).astype(BF16)
            kr = k0_ref if hd < MLA_HEADS // 2 else k1_ref
            ks = slice((hd % (MLA_HEADS // 2)) * LANES, (hd % (MLA_HEADS // 2) + 1) * LANES)
            kk = _qk_norm_rope(kr[:, ks] + pe, kg_ref[...], *tab)
            ko_ref[:, sl] = kk.astype(BF16)
            kt_ref[sl, :] = kk.T.astype(BF16)
        vo_ref[...] = v_ref[...].astype(BF16)
        for c in range(VW // LANES):
            sl = slice(c * LANES, (c + 1) * LANES)
            vt_ref[sl, :] = v_ref[:, sl].T.astype(BF16)

    row = lambda w, b: pl.BlockSpec((tm, w), lambda i: (i, b))
    col = lambda w: pl.BlockSpec((w, tm), lambda i: (0, i))
    one = pl.BlockSpec((1, LANES), lambda i: (0, 0))
    return pl.pallas_call(
        body, name=name, grid=(rows // tm,),
        in_specs=[row(QW, 0), row(VW, 0), row(VW, 1), row(VW, 2), row(LANES, kpe_block), one, one,
                  row(LANES, 0), row(LANES, 0), row(LANES, 0)],
        out_specs=[row(QW, 0), row(QW, 0), col(QW), row(VW, 0), col(VW)],
        out_shape=[jax.ShapeDtypeStruct((rows, QW), BF16), jax.ShapeDtypeStruct((rows, QW), BF16),
                   jax.ShapeDtypeStruct((QW, rows), BF16), jax.ShapeDtypeStruct((rows, VW), BF16),
                   jax.ShapeDtypeStruct((VW, rows), BF16)],
        compiler_params=_params(1),
    )(qraw, kvraw, kvraw, kvraw, lat, qg, kg, *tabs)


def qk_prep_bwd(name, dq_t, dk, dv, qraw, kvraw, lat, kpe_block, qg, kg, tabs):
    rows = qraw.shape[0]
    tm = _row_tile(rows)

    def body(dq_ref, dk_ref, dv_ref, q_ref, k0_ref, k1_ref, pe_ref, qg_ref, kg_ref, c_ref, sa_ref, sb_ref,
             dqo_ref, dkvo_ref, dpe_ref, dqg_ref, dkg_ref):
        i = pl.program_id(0)
        tab = (c_ref[...], sa_ref[...], sb_ref[...])
        pe = pe_ref[...]
        dpe = jnp.zeros((tm, LANES), F32)
        dqg = jnp.zeros((SUBLANES, LANES), F32)
        dkg = jnp.zeros((SUBLANES, LANES), F32)
        for hd in range(MLA_HEADS):
            sl = slice(hd * LANES, (hd + 1) * LANES)
            dx, dg = _qk_norm_rope_bwd(dq_ref[sl, :].T, q_ref[:, sl], qg_ref[...], *tab)
            dqo_ref[:, sl] = dx.astype(BF16)
            dqg = dqg + dg
            kr = k0_ref if hd < MLA_HEADS // 2 else k1_ref
            ks = slice((hd % (MLA_HEADS // 2)) * LANES, (hd % (MLA_HEADS // 2) + 1) * LANES)
            dx, dg = _qk_norm_rope_bwd(dk_ref[:, sl], kr[:, ks] + pe, kg_ref[...], *tab)
            dkvo_ref[:, sl] = dx.astype(BF16)
            dpe = dpe + dx
            dkg = dkg + dg
        dkvo_ref[:, QW:QW + VW] = dv_ref[...].astype(BF16)
        dpe_ref[...] = jnp.where(_rope_lanes(), dpe, 0.0)
        _acc_out(dqg_ref, dqg, i == 0)
        _acc_out(dkg_ref, dkg, i == 0)

    row = lambda w, b: pl.BlockSpec((tm, w), lambda i: (i, b))
    one = pl.BlockSpec((1, LANES), lambda i: (0, 0))
    acc = pl.BlockSpec((SUBLANES, LANES), lambda i: (0, 0))
    return pl.pallas_call(
        body, name=name, grid=(rows // tm,),
        in_specs=[pl.BlockSpec((QW, tm), lambda i: (0, i)), row(QW, 0), row(VW, 0), row(QW, 0), row(VW, 0), row(VW, 1),
                  row(LANES, kpe_block), one, one, row(LANES, 0), row(LANES, 0), row(LANES, 0)],
        out_specs=[row(QW, 0), row(QW + VW, 0), row(LANES, 0), acc, acc],
        out_shape=[jax.ShapeDtypeStruct((rows, QW), BF16), jax.ShapeDtypeStruct((rows, QW + VW), BF16),
                   jax.ShapeDtypeStruct((rows, LANES), F32),
                   jax.ShapeDtypeStruct((SUBLANES, LANES), F32), jax.ShapeDtypeStruct((SUBLANES, LANES), F32)],
        compiler_params=_params(1),
    )(dq_t, dk, dv, qraw, kvraw, kvraw, lat, qg, kg, *tabs)


def _att_mask_t(qb, kb, bt):
    kpos = kb * bt + lax.broadcasted_iota(jnp.int32, (bt, bt), 0)
    qpos = qb * bt + lax.broadcasted_iota(jnp.int32, (bt, bt), 1)
    return jnp.logical_and(kpos <= qpos, jnp.logical_or(kpos >= PAD, qpos < PAD))


def attn_fwd(name, q, k, vt, comm=None):
    rows = q.shape[0]
    bt = _row_tile(rows)
    nb = rows // bt
    assert bt >= CHUNK

    def body(q_ref, k_ref, vt_ref, o_ref, lse_ref):
        qi = pl.program_id(1)
        lse_ref[...] = jnp.zeros((SUBLANES, bt), F32)

        def scores(kb):
            r0 = pl.multiple_of(kb * bt, LANES)
            return tuple(lax.dot_general(k_ref[pl.ds(r0, bt), hh * LANES:(hh + 1) * LANES],
                                         q_ref[:, hh * LANES:(hh + 1) * LANES], (NT, ((), ())),
                                         preferred_element_type=F32) for hh in range(2))

        def tile(kb, carry, s_pair, masked):
            r0 = pl.multiple_of(kb * bt, LANES)
            new = []
            for hh in range(2):
                m, l, acc = carry[3 * hh:3 * hh + 3]
                vs = slice(hh * MLA_V, (hh + 1) * MLA_V)
                s = s_pair[hh] * ATT_SCALE
                if masked:
                    s = jnp.where(_att_mask_t(qi, kb, bt), s, NEG)
                m_new = jnp.maximum(m, jnp.max(s, axis=0, keepdims=True))
                alpha = jnp.exp(m - m_new)
                p = jnp.exp(s - m_new)
                l = alpha * l + jnp.sum(p, axis=0, keepdims=True)
                acc = alpha * acc + jnp.dot(vt_ref[vs, pl.ds(r0, bt)], p.astype(BF16), preferred_element_type=F32)
                new += [m_new, l, acc]
            return tuple(new)

        init = (jnp.full((1, bt), NEG, F32), jnp.zeros((1, bt), F32), jnp.zeros((MLA_V, bt), F32)) * 2
        s_next = scores(jnp.minimum(1, qi))
        carry = tile(0, init, scores(0), True)

        def rest(args):
            def mid(kb, state):
                carry, s_cur = state
                s_after = scores(kb + 1)
                return tile(kb, carry, s_cur, False), s_after

            carry, s_last = lax.fori_loop(1, qi, mid, args)
            return tile(qi, carry, s_last, True)

        carry = lax.cond(qi > 0, rest, lambda args: args[0], (carry, s_next))
        for hh in range(2):
            m, l, acc = carry[3 * hh:3 * hh + 3]
            o_ref[hh * MLA_V:(hh + 1) * MLA_V, :] = acc / l
            lse_ref[hh:hh + 1, :] = m + jnp.log(l)

    return carrier_call(
        name, body, (N_PAIRS, nb),
        [pl.BlockSpec((bt, HP), lambda p, i: (i, p)),
         pl.BlockSpec((rows, HP), lambda p, i: (0, p)),
         pl.BlockSpec((VP, rows), lambda p, i: (p, 0))],
        [pl.BlockSpec((VP, bt), lambda p, i: (p, i)),
         pl.BlockSpec((None, SUBLANES, bt), lambda p, i: (p, 0, i))],
        [jax.ShapeDtypeStruct((VW, rows), F32), jax.ShapeDtypeStruct((N_PAIRS, SUBLANES, rows), F32)],
        [], (q, k, vt), comm)


def attn_bwd(name, q, k, kt, v, do_t, lse, delta, comm=None):
    rows = q.shape[0]
    bt = _row_tile(rows)
    nb = rows // bt

    def body(q_ref, k_ref, kt_ref, v_ref, do_ref, lse_ref, dl_ref, dq_ref, dk_ref, dv_ref, dk_scr, dv_scr):
        ki = pl.program_id(1)

        @pl.when(ki == 0)
        def _():
            dq_ref[...] = jnp.zeros((HP, rows), F32)

        dk_scr[...] = jnp.zeros((bt, HP), F32)
        dv_scr[...] = jnp.zeros((bt, VP), F32)

        def tile(qb, masked):
            c0 = pl.multiple_of(qb * bt, LANES)
            for hh in range(2):
                qs = slice(hh * LANES, (hh + 1) * LANES)
                vs = slice(hh * MLA_V, (hh + 1) * MLA_V)
                qv = q_ref[pl.ds(c0, bt), qs]
                dov = do_ref[vs, pl.ds(c0, bt)]
                lse = lse_ref[hh:hh + 1, pl.ds(c0, bt)]
                dl = dl_ref[hh:hh + 1, pl.ds(c0, bt)]
                s = lax.dot_general(k_ref[:, qs], qv, (NT, ((), ())), preferred_element_type=F32) * ATT_SCALE
                p = jnp.exp(s - lse)
                if masked:
                    p = jnp.where(_att_mask_t(qb, ki, bt), p, 0.0)
                dp = jnp.dot(v_ref[:, vs], dov, preferred_element_type=F32)
                ds = (p * (dp - dl) * ATT_SCALE).astype(BF16)
                dv_scr[:, vs] += lax.dot_general(p.astype(BF16), dov, (NT, ((), ())), preferred_element_type=F32)
                dk_scr[:, qs] += jnp.dot(ds, qv, preferred_element_type=F32)
                dq_ref[qs, pl.ds(c0, bt)] += jnp.dot(kt_ref[qs, :], ds, preferred_element_type=F32)

        @pl.when(ki == 0)
        def _():
            def every(qb, carry):
                tile(qb, True)
                return carry

            lax.fori_loop(0, nb, every, 0)

        @pl.when(ki > 0)
        def _():
            tile(ki, True)

            def later(qb, carry):
                tile(qb, False)
                return carry

            lax.fori_loop(ki + 1, nb, later, 0)

        dk_ref[...] = dk_scr[...]
        dv_ref[...] = dv_scr[...]

    stat = pl.BlockSpec((None, SUBLANES, rows), lambda p, i: (p, 0, 0))
    return carrier_call(
        name, body, (N_PAIRS, nb),
        [pl.BlockSpec((rows, HP), lambda p, i: (0, p)),
         pl.BlockSpec((bt, HP), lambda p, i: (i, p)),
         pl.BlockSpec((HP, bt), lambda p, i: (p, i)),
         pl.BlockSpec((bt, VP), lambda p, i: (i, p)),
         pl.BlockSpec((VP, rows), lambda p, i: (p, 0)),
         stat, stat],
        [pl.BlockSpec((HP, rows), lambda p, i: (p, 0)),
         pl.BlockSpec((bt, HP), lambda p, i: (i, p)),
         pl.BlockSpec((bt, VP), lambda p, i: (i, p))],
        [jax.ShapeDtypeStruct((QW, rows), F32), jax.ShapeDtypeStruct((rows, QW), F32),
         jax.ShapeDtypeStruct((rows, VW), F32)],
        [pltpu.VMEM((bt, HP), F32), pltpu.VMEM((bt, VP), F32)],
        (q, k, kt, v, do_t, lse, delta), comm)


def _dot_cast_w(a, w_ref):
    return jnp.dot(a, w_ref[...].astype(BF16), preferred_element_type=F32)


def _dot_cast_wt(a, w_ref):
    return lax.dot_general(a, w_ref[...].astype(BF16), (NT, ((), ())), preferred_element_type=F32)


LAT_W = 768
KPE_BLOCK = MLA_Q_RANK // LANES
KV_BLOCK = (MLA_Q_RANK + LANES) // MLA_KV_RANK


def mla_layer_fwd(tag, h, ln_g, w, tabs, comm=None):
    lat, hn = norm_mm(f"mla_in_{tag}", h, ln_g, w["w_in"], tn=LAT_W)
    qraw, qn = norm_mm(f"mla_q_{tag}", lat, w["q_a"], w["w_q"], tn=512, k_cols=MLA_Q_RANK, col_block=0)
    kvraw, kvn = norm_mm(f"mla_kv_{tag}", lat, w["kv_a"], w["w_kv"], tn=512, k_cols=MLA_KV_RANK, col_block=KV_BLOCK)
    q, k, kt, v, vt = qk_prep(f"mla_prep_{tag}", qraw, kvraw, lat, KPE_BLOCK, w["q_norm"], w["k_norm"], tabs)
    (o_t, lse), cres = attn_fwd(f"mla_attn_{tag}", q, k, vt, comm)
    out = res_mm(f"mla_out_{tag}", [(o_t, (VW, None), lambda i, j, kk: (0, i))],
                 lambda r: r[0][...].T.astype(BF16), VW, w["w_out"], h, tn=512, tm=_big_tile(h.shape[0]))
    return out, (h, hn, lat, qn, kvn, qraw, kvraw, q, k, kt, v, o_t, lse), cres


def mla_layer_bwd(tag, dh, saved, ln_g, w, tabs, comm=None):
    h, hn, lat, qn, kvn, qraw, kvraw, q, k, kt, v, o_t, lse = saved
    rows, d = h.shape
    tm = _row_tile(rows)

    def epi_set(acc, e_refs, o_refs, i, j):
        o_refs[0][...] = acc.astype(BF16)

    dw_out, = fused_mm(
        f"mla_dwout_{tag}", rows=VW, k=rows, n=d, tm=512, tn=d, tk=tm,
        a_ops=[(o_t, (512, tm), lambda i, j, kk: (i, kk))], pro=lambda a, o_, i: a[0][...].astype(BF16),
        w=dh, w_block=(tm, d), w_imap=lambda i, j, kk: (kk, 0), dot=_dot_cast_w, epi=epi_set,
        outs=[((VW, d), BF16, (512, d), lambda i, j, kk: (i, 0))])

    def epi_do(acc, e_refs, o_refs, i, j):
        o_refs[0][...] = acc.astype(BF16)
        prod = acc * e_refs[0][...]
        o_refs[1][...] = jnp.zeros((N_PAIRS, SUBLANES, tm), F32)
        for hd in range(MLA_HEADS):
            o_refs[1][hd // 2, hd % 2:hd % 2 + 1, :] = jnp.sum(prod[hd * MLA_V:(hd + 1) * MLA_V, :], axis=0,
                                                               keepdims=True)

    wo, wo_block, wo_imap = _lw(w["w_out"], (VW, d), lambda i, j, kk: (0, 0))
    do_t, delta = fused_mm(
        f"mla_do_{tag}", rows=VW, k=d, n=rows, tm=VW, tn=tm,
        a_ops=[(wo, wo_block, wo_imap)], pro=lambda a, o_, i: a[0][...],
        w=dh, w_block=(tm, d), w_imap=lambda i, j, kk: (j, 0), dot=_dot_cast_wt,
        e_ops=[(o_t, (VW, tm), lambda i, j, kk: (0, j))], epi=epi_do,
        outs=[((VW, rows), BF16, (VW, tm), lambda i, j, kk: (0, j)),
              ((N_PAIRS, SUBLANES, rows), F32, (N_PAIRS, SUBLANES, tm), lambda i, j, kk: (0, 0, j))])
    (dq_t, dk, dv), cres = attn_bwd(f"mla_dattn_{tag}", q, k, kt, v, do_t, lse, delta, comm)
    dqraw, dkvraw, dpe, dqg, dkg = qk_prep_bwd(f"mla_dprep_{tag}", dq_t, dk, dv, qraw, kvraw, lat, KPE_BLOCK,
                                               w["q_norm"], w["k_norm"], tabs)
    dw_q = simple_wgrad(f"mla_dwq_{tag}", qn, dqraw, t1=MLA_Q_RANK, tn=512)
    dqlat, dqa = rms_bwd_mm(
        f"mla_dqlat_{tag}", [(dqraw, (None, QW), lambda i, j, kk: (i, 0))], lambda r: r[0][...], QW,
        w["w_q"], (MLA_Q_RANK, QW), lambda i, j, kk: (0, 0), _dot_wt, lat, w["q_a"], None,
        h_cols=MLA_Q_RANK, h_col_block=0, add_dh=False)
    dw_kv = simple_wgrad(f"mla_dwkv_{tag}", kvn, dkvraw, t1=MLA_KV_RANK, tn=512)
    dkvlat, dkva = rms_bwd_mm(
        f"mla_dkvlat_{tag}", [(dkvraw, (None, QW + VW), lambda i, j, kk: (i, 0))], lambda r: r[0][...], QW + VW,
        w["w_kv"], (MLA_KV_RANK, QW + VW), lambda i, j, kk: (0, 0), _dot_wt, lat, w["kv_a"], None,
        h_cols=MLA_KV_RANK, h_col_block=KV_BLOCK, add_dh=False)
    dlat = jnp.concatenate([dqlat.astype(BF16), dpe.astype(BF16), dkvlat.astype(BF16)], axis=1)
    dw_in = simple_wgrad(f"mla_dwin_{tag}", hn, dlat, t1=512, tn=LAT_W)
    dh_in, dln = rms_bwd_mm(
        f"mla_dh_{tag}", [(dlat, (None, LAT_W), lambda i, j, kk: (i, 0))], lambda r: r[0][...], LAT_W,
        w["w_in"], (d, LAT_W), lambda i, j, kk: (0, 0), _dot_wt, h, ln_g, dh)
    grads = dict(w_in=dw_in, w_q=dw_q, w_kv=dw_kv, w_out=dw_out, q_a=dqa, kv_a=dkva, q_norm=dqg, k_norm=dkg, ln=dln)
    return dh_in, grads, cres


def loss_head(h, target):
    rows, d = h.shape
    nb = rows // CHUNK

    def body(h_ref, t_ref, l_ref, dh_ref):
        i = pl.program_id(0)
        err = jnp.where(i > 0, h_ref[...] - t_ref[...], 0.0)
        dh_ref[...] = err * (1.0 / d)
        _acc_out(l_ref, _rows8(err * err) * (0.5 / d), i == 0)

    return pl.pallas_call(
        body, name="loss_head", grid=(nb,),
        in_specs=[pl.BlockSpec((CHUNK, d), lambda i: (i, 0)),
                  pl.BlockSpec((CHUNK, d), lambda i: (jnp.maximum(i - 1, 0), 0))],
        out_specs=[pl.BlockSpec((SUBLANES, d), lambda i: (0, 0)), pl.BlockSpec((CHUNK, d), lambda i: (i, 0))],
        out_shape=[jax.ShapeDtypeStruct((SUBLANES, d), F32), jax.ShapeDtypeStruct((rows, d), F32)],
        compiler_params=_params(1),
    )(h, target)


def _adamw(w, g, m, v):
    m = ADAM_B1 * m + (1.0 - ADAM_B1) * g
    v = ADAM_B2 * v + (1.0 - ADAM_B2) * jnp.square(g)
    m_hat = m / (1.0 - ADAM_B1 ** ADAM_STEP)
    v_hat = v / (1.0 - ADAM_B2 ** ADAM_STEP)
    delta = -ADAM_LR * (m_hat / (jnp.sqrt(v_hat) + ADAM_EPS) + ADAM_WD * w)
    return delta, m, v


def reduce_adamw(name, recvs, w, m, v):
    nl, r, c = w.shape
    tr = 128 if r % 128 == 0 else r
    nr = r // tr

    def body(*refs):
        r_refs = refs[:nl]
        w_ref, m_ref, v_ref, g_ref, d_ref, mo_ref, vo_ref = refs[nl:]
        layer = pl.program_id(0)
        for l in range(nl):
            @pl.when(layer == l)
            def _(l=l):
                g = r_refs[l][0].astype(F32)
                for s in range(1, N_DEV):
                    g = g + r_refs[l][s].astype(F32)
                g_ref[...] = g
                d_ref[...], mo_ref[...], vo_ref[...] = _adamw(w_ref[...], g, m_ref[...], v_ref[...])

    def recv_spec(l):
        return pl.BlockSpec((N_DEV, tr, c),
                            lambda y, i: (0, jnp.where(y == l, i, jnp.where(y < l, 0, nr - 1)), 0))

    blk = pl.BlockSpec((None, tr, c), lambda y, i: (y, i, 0))
    return pl.pallas_call(
        body, name=name, grid=(nl, nr),
        in_specs=[recv_spec(l) for l in range(nl)] + [blk, blk, blk],
        out_specs=[blk] * 4, out_shape=[jax.ShapeDtypeStruct((nl, r, c), F32)] * 4,
        compiler_params=_params(2),
    )(*recvs, w, m, v)


def small_reduce(recv):
    def body(r_ref, o_ref):
        g = r_ref[0]
        for s in range(1, N_DEV):
            g = g + r_ref[s]
        o_ref[...] = g

    return pl.pallas_call(body, name="small_reduce", out_shape=jax.ShapeDtypeStruct(recv.shape[1:], F32))(recv)


def small_adamw(w, g, m, v):
    def body(w_ref, g_ref, m_ref, v_ref, d_ref, mo_ref, vo_ref):
        d_ref[...], mo_ref[...], vo_ref[...] = _adamw(w_ref[...], g_ref[...], m_ref[...], v_ref[...])

    return pl.pallas_call(body, name="small_adamw", out_shape=[jax.ShapeDtypeStruct(w.shape, F32)] * 3)(w, g, m, v)


def _pack(parts):
    flat, meta, off = [], [], 0
    for p in parts:
        n = int(np.prod(p.shape))
        flat.append(p.reshape(-1).astype(F32))
        meta.append((off, p.shape))
        off += n
    total = -(-off // (SUBLANES * LANES)) * (SUBLANES * LANES)
    flat.append(jnp.zeros((total - off,), F32))
    return jnp.concatenate(flat).reshape(total // LANES, LANES), meta


def _unpack(packed, meta):
    flat = packed.reshape(-1)
    return [flat[off:off + int(np.prod(shape))].reshape(shape) for off, shape in meta]


MESH = pl.DeviceIdType.MESH
N_PEERS = N_DEV - 1


def _me():
    return lax.axis_index("x"), lax.axis_index("y"), lax.axis_index("c")


def _peer(k):
    x, y, c = _me()
    return (1 - x if k & 4 else x, 1 - y if k & 2 else y, 1 - c if k & 1 else c)


def _dev_index(pos):
    return 4 * pos[0] + 2 * pos[1] + pos[2]


def make_comm(items):
    n = len(items)

    def part(ref, a, idx):
        rows = items[a][1]
        if rows == "all":
            return ref
        return ref.at[idx] if rows is None else ref.at[pl.ds(idx * rows, rows)]

    def part_shape(a):
        arr, rows = items[a]
        if rows == "all":
            return arr.shape
        return arr.shape[1:] if rows is None else (rows,) + arr.shape[1:]

    def run(phase, ins, outs, send_sems, recv_sems, local_sems):
        me = _dev_index(_me())
        for a in range(n):
            local = pltpu.make_async_copy(part(ins[a], a, me), outs[a].at[me], local_sems.at[a])
            if phase == "start":
                local.start()
            for k in range(1, N_DEV):
                peer = _peer(k)
                if phase == "start":
                    pltpu.make_async_remote_copy(
                        src_ref=part(ins[a], a, _dev_index(peer)), dst_ref=outs[a].at[me],
                        send_sem=send_sems.at[a, k - 1], recv_sem=recv_sems.at[a, k - 1],
                        device_id=peer, device_id_type=MESH).start()
                else:
                    cp = pltpu.make_async_remote_copy(
                        src_ref=part(ins[a], a, me), dst_ref=outs[a].at[_dev_index(peer)],
                        send_sem=send_sems.at[a, k - 1], recv_sem=recv_sems.at[a, k - 1],
                        device_id=peer, device_id_type=MESH)
                    cp.wait_recv()
                    cp.wait_send()
            if phase == "wait":
                local.wait()

    return dict(
        ins=[it[0] for it in items],
        outs=[jax.ShapeDtypeStruct((N_DEV,) + part_shape(a), items[a][0].dtype) for a in range(n)],
        sems=[pltpu.SemaphoreType.DMA((n, N_PEERS)), pltpu.SemaphoreType.DMA((n, N_PEERS)),
              pltpu.SemaphoreType.DMA((n,))],
        run=run)


ANY_SPEC = pl.BlockSpec(memory_space=pl.ANY)


def comm_call(name, comm):
    n, no = len(comm["ins"]), len(comm["outs"])

    def body(*refs):
        comm["run"]("start", refs[:n], refs[n:n + no], *refs[n + no:])
        comm["run"]("wait", refs[:n], refs[n:n + no], *refs[n + no:])

    return pl.pallas_call(
        body, name=name, in_specs=[ANY_SPEC] * n, out_specs=[ANY_SPEC] * no, out_shape=comm["outs"],
        scratch_shapes=comm["sems"])(*comm["ins"])


def with_comm(comm, body, grid, in_specs, out_specs, out_shape, scratch_shapes):
    if comm is None:
        return body, in_specs, out_specs, out_shape, scratch_shapes, [], len(out_shape)
    n_in, n_out, n_scr = len(in_specs), len(out_shape), len(scratch_shapes)
    ci, co = len(comm["ins"]), len(comm["outs"])

    def wrapped(*refs):
        ins, cins = refs[:n_in], refs[n_in:n_in + ci]
        outs = refs[n_in + ci:n_in + ci + n_out]
        couts = refs[n_in + ci + n_out:n_in + ci + n_out + co]
        rest = refs[n_in + ci + n_out + co:]
        scr, sems = rest[:n_scr], rest[n_scr:]
        first = functools.reduce(jnp.logical_and, [pl.program_id(a) == 0 for a in range(len(grid))])
        last = functools.reduce(jnp.logical_and, [pl.program_id(a) == grid[a] - 1 for a in range(len(grid))])

        @pl.when(first)
        def _():
            comm["run"]("start", cins, couts, *sems)

        body(*ins, *outs, *scr)

        @pl.when(last)
        def _():
            comm["run"]("wait", cins, couts, *sems)

    return (wrapped, list(in_specs) + [ANY_SPEC] * ci, list(out_specs) + [ANY_SPEC] * co,
            list(out_shape) + list(comm["outs"]), list(scratch_shapes) + list(comm["sems"]), list(comm["ins"]), n_out)


WEIGHTS = ['meta_tokens', 'ln_mix', 'ln_mlp', 'ssd_w_in', 'ssd_conv_w', 'ssd_conv_b', 'ssd_dt_bias', 'ssd_a_log',
           'ssd_d', 'ssd_norm', 'ssd_w_out', 'mla_w_in', 'mla_q_a_norm', 'mla_w_q_b', 'mla_kv_a_norm', 'mla_w_kv_b',
           'mla_q_norm', 'mla_k_norm', 'mla_w_out', 'mlp_w_up', 'mlp_w_down']
BIG = ['ssd_w_in', 'ssd_w_out', 'mla_w_in', 'mla_w_q_b', 'mla_w_kv_b', 'mla_w_out', 'mlp_w_up', 'mlp_w_down']
SMALL_SHARDED = ['meta_tokens', 'ssd_conv_w', 'mla_q_a_norm', 'mla_kv_a_norm']
SMALL_REPL = ['ln_mix', 'ln_mlp', 'ssd_conv_b', 'ssd_dt_bias', 'ssd_a_log', 'ssd_d', 'ssd_norm', 'mla_q_norm',
              'mla_k_norm']
SMALL = SMALL_REPL + SMALL_SHARDED
SSD_IN_PAD = 6272
SSD_IN_TN = 896
MLA_IN = MLA_Q_RANK + MLA_KV_RANK + MLA_ROPE


def _pad_last(v, n):
    return jnp.pad(v, [(0, 0)] * (v.ndim - 1) + [(0, n - v.shape[-1])])


SSD_BIG = ['ssd_w_in', 'ssd_w_out']
MLA_BIG = ['mla_w_in', 'mla_w_q_b', 'mla_w_kv_b', 'mla_w_out']
MLP_BIG = ['mlp_w_up', 'mlp_w_down']


def _mix_big(i):
    return [(n, i // 2) for n in (SSD_BIG if i % 2 == 0 else MLA_BIG)]


def _mlp_big(i):
    return [(n, i) for n in MLP_BIG]


def _mix_weights(i, gw, W, full):
    j = i // 2
    d = W['ln_mix'].shape[-1]
    if i % 2 == 0:
        wi = gw[('ssd_w_in', j)].transpose(1, 0, 2).reshape(d, -1)
        par = jnp.concatenate([_pad_last(W[n][j][None], LANES) for n in ('ssd_dt_bias', 'ssd_a_log', 'ssd_d')]
                              + [jnp.zeros((SUBLANES - 3, LANES), F32)])
        return dict(w_in=_pad_last(wi, SSD_IN_PAD), conv_w=full['ssd_conv_w'][j], conv_b=W['ssd_conv_b'][j][None],
                    par=par, norm=W['ssd_norm'][j][None], w_out=gw[('ssd_w_out', j)].reshape(SSD_D_INNER, d))
    wi = gw[('mla_w_in', j)].reshape(d, MLA_IN)
    kpe = jnp.pad(wi[:, MLA_Q_RANK + MLA_KV_RANK:], ((0, 0), (MLA_NOPE, LANES - MLA_QK)))
    wq = gw[('mla_w_q_b', j)].transpose(1, 0, 2).reshape(MLA_Q_RANK, MLA_HEADS, MLA_QK)
    wkv = gw[('mla_w_kv_b', j)].transpose(1, 0, 2).reshape(MLA_KV_RANK, MLA_HEADS, MLA_NOPE + MLA_V)
    return dict(
        w_in=jnp.concatenate([wi[:, :MLA_Q_RANK], kpe, wi[:, MLA_Q_RANK:MLA_Q_RANK + MLA_KV_RANK]], axis=1),
        w_q=_pad_last(wq, LANES).reshape(MLA_Q_RANK, QW),
        w_kv=jnp.concatenate([_pad_last(wkv[..., :MLA_NOPE], LANES).reshape(MLA_KV_RANK, QW),
                              wkv[..., MLA_NOPE:].reshape(MLA_KV_RANK, VW)], axis=1),
        w_out=gw[('mla_w_out', j)].reshape(VW, d), q_a=full['mla_q_a_norm'][j][None],
        kv_a=full['mla_kv_a_norm'][j][None],
        q_norm=_pad_last(W['mla_q_norm'][j][None], LANES), k_norm=_pad_last(W['mla_k_norm'][j][None], LANES))


def _step(x, target, W, M, V):
    d = x.shape[-1]
    me = _dev_index(_me())
    depth = W['ln_mix'].shape[0]

    def gather_keys(i):
        return _mlp_big(i) + (_mix_big(i + 1) if i + 1 < depth else [])

    def gather_items(keys):
        return [(W[n][l].astype(BF16), "all") for n, l in keys]

    small_pack, small_meta = _pack([W[n] for n in SMALL_SHARDED])
    got = comm_call("gather_0", make_comm(gather_items(_mix_big(0)) + [(small_pack, "all")]))
    per_dev = [_unpack(got[-1][s], small_meta) for s in range(N_DEV)]
    full = {n: jnp.concatenate([per_dev[s][i] for s in range(N_DEV)], axis=-1) for i, n in enumerate(SMALL_SHARDED)}
    gw = dict(zip(_mix_big(0), got))

    h = jnp.concatenate([jnp.zeros((PAD, d), F32), full['meta_tokens'], x], axis=0)
    rows = h.shape[0]
    tabs = rope_tables(rows)
    saved, weights = [], []
    for i in range(depth):
        comm = make_comm(gather_items(gather_keys(i)))
        mix = _mix_weights(i, gw, W, full)
        if i % 2 == 0:
            h, s_mix, got = ssd_layer_fwd(f"{i}", h, W['ln_mix'][i][None], mix, comm)
        else:
            h, s_mix, got = mla_layer_fwd(f"{i}", h, W['ln_mix'][i][None], mix, tabs, comm)
        gw.update(zip(gather_keys(i), got))
        up, down = gw[('mlp_w_up', i)], gw[('mlp_w_down', i)].reshape(-1, d)
        h, s_mlp = mlp_fwd(f"{i}", h, W['ln_mlp'][i][None], up, down)
        saved.append((s_mix, s_mlp))
        weights.append((mix, up, down))
    loss_part, dh = loss_head(h, target)
    loss = lax.psum(jnp.sum(loss_part), ("x", "y", "c"))

    recv = {}
    pending = []
    small = {n: [None] * W[n].shape[0] for n in SMALL if n != 'meta_tokens'}
    for i in reversed(range(depth)):
        j = i // 2
        s_mix, s_mlp = saved[i]
        mix, up, down = weights[i]
        dh, dw_up, dw_down, dg = mlp_bwd(f"{i}", dh, s_mlp, W['ln_mlp'][i][None], up, down)
        small['ln_mlp'][i] = dg.sum(0)
        pending += list(zip(_mlp_big(i), [(dw_up, None), (dw_down, down.shape[0] // N_DEV)]))
        comm = make_comm([it for _, it in pending])
        sends = []
        if i % 2 == 0:
            dh, g, got = ssd_layer_bwd(f"{i}", dh, s_mix, W['ln_mix'][i][None], mix, comm)
            n_in = W['ssd_w_in'].shape[-1]
            sends.append((g['w_in'][:, :N_DEV * n_in].reshape(d, N_DEV, n_in).transpose(1, 0, 2), None))
            sends.append((g['w_out'], SSD_D_INNER // N_DEV))
            small['ssd_conv_w'][j] = g['conv_w'].reshape(SSD_CONV, SUBLANES, -1).sum(1)
            small['ssd_conv_b'][j] = g['conv_b'].sum(0)
            small['ssd_dt_bias'][j] = g['par'][0, :SSD_HEADS]
            small['ssd_a_log'][j] = g['par'][1, :SSD_HEADS]
            small['ssd_d'][j] = g['par'][2, :SSD_HEADS]
            small['ssd_norm'][j] = g['norm'].sum(0)
        else:
            dh, g, got = mla_layer_bwd(f"{i}", dh, s_mix, W['ln_mix'][i][None], mix, tabs, comm)
            gi = g['w_in']
            gi = jnp.concatenate([gi[:, :MLA_Q_RANK], gi[:, MLA_Q_RANK + LANES:],
                                  gi[:, MLA_Q_RANK + MLA_NOPE:MLA_Q_RANK + MLA_QK]], axis=1)
            sends.append((gi, d // N_DEV))
            gq = g['w_q'].reshape(MLA_Q_RANK, MLA_HEADS, LANES)[..., :MLA_QK]
            sends.append((gq.reshape(MLA_Q_RANK, N_DEV, -1).transpose(1, 0, 2), None))
            gkv = jnp.concatenate([g['w_kv'][:, :QW].reshape(MLA_KV_RANK, MLA_HEADS, LANES)[..., :MLA_NOPE],
                                   g['w_kv'][:, QW:].reshape(MLA_KV_RANK, MLA_HEADS, MLA_V)], axis=-1)
            sends.append((gkv.reshape(MLA_KV_RANK, N_DEV, -1).transpose(1, 0, 2), None))
            sends.append((g['w_out'], VW // N_DEV))
            small['mla_q_a_norm'][j] = g['q_a'].sum(0)
            small['mla_kv_a_norm'][j] = g['kv_a'].sum(0)
            small['mla_q_norm'][j] = g['q_norm'].sum(0)[:MLA_QK]
            small['mla_k_norm'][j] = g['k_norm'].sum(0)[:MLA_QK]
        small['ln_mix'][i] = g['ln'].sum(0)
        recv.update({key: a for (key, _), a in zip(pending, got)})
        pending = list(zip(_mix_big(i), sends))
    grad_x = dh[CHUNK:]
    small_full = {n: jnp.stack(v) for n, v in small.items()}
    small_full['meta_tokens'] = dh[PAD:CHUNK]

    gpack, gmeta = _pack([small_full[n] for n in SMALL])
    got = comm_call("exchange_0", make_comm([it for _, it in pending] + [(gpack, "all")]))
    recv.update({key: a for (key, _), a in zip(pending, got)})
    res = {}
    for n in BIG:
        res[n] = reduce_adamw(f"adamw_{n}", [recv[(n, l)] for l in range(W[n].shape[0])], W[n], M[n], V[n])
    gsum = dict(zip(SMALL, _unpack(small_reduce(got[-1]), gmeta)))
    for n in SMALL_SHARDED:
        wl = W[n].shape[-1]
        gsum[n] = lax.dynamic_slice_in_dim(gsum[n], me * wl, wl, axis=gsum[n].ndim - 1)
    wp, wmeta = _pack([W[n] for n in SMALL])
    gp, _ = _pack([gsum[n] for n in SMALL])
    mp, _ = _pack([M[n] for n in SMALL])
    vp, _ = _pack([V[n] for n in SMALL])
    upd = [_unpack(o, wmeta) for o in small_adamw(wp, gp, mp, vp)]
    for a, n in enumerate(SMALL):
        res[n] = [gsum[n], upd[0][a], upd[1][a], upd[2][a]]
    return (loss, grad_x[None]) + tuple(res[n][q] for q in range(4) for n in WEIGHTS)


def kernel(x, meta_tokens, ln_mix, ln_mlp, ssd_w_in, ssd_conv_w, ssd_conv_b, ssd_dt_bias, ssd_a_log, ssd_d, ssd_norm, ssd_w_out, mla_w_in, mla_q_a_norm, mla_w_q_b, mla_kv_a_norm, mla_w_kv_b, mla_q_norm, mla_k_norm, mla_w_out, mlp_w_up, mlp_w_down, loss_target, m_meta_tokens, m_ln_mix, m_ln_mlp, m_ssd_w_in, m_ssd_conv_w, m_ssd_conv_b, m_ssd_dt_bias, m_ssd_a_log, m_ssd_d, m_ssd_norm, m_ssd_w_out, m_mla_w_in, m_mla_q_a_norm, m_mla_w_q_b, m_mla_kv_a_norm, m_mla_w_kv_b, m_mla_q_norm, m_mla_k_norm, m_mla_w_out, m_mlp_w_up, m_mlp_w_down, v_meta_tokens, v_ln_mix, v_ln_mlp, v_ssd_w_in, v_ssd_conv_w, v_ssd_conv_b, v_ssd_dt_bias, v_ssd_a_log, v_ssd_d, v_ssd_norm, v_ssd_w_out, v_mla_w_in, v_mla_q_a_norm, v_mla_w_q_b, v_mla_kv_a_norm, v_mla_w_kv_b, v_mla_q_norm, v_mla_k_norm, v_mla_w_out, v_mlp_w_up, v_mlp_w_down):
    given = dict(locals())
    W = {n: given[n] for n in WEIGHTS}
    M = {n: given["m_" + n] for n in WEIGHTS}
    V = {n: given["v_" + n] for n in WEIGHTS}
    return _step(x[0], loss_target[0], W, M, V)
```

```python
import functools

import jax
import jax.numpy as jnp
import numpy as np
from jax import lax
from jax.experimental import pallas as pl
from jax.experimental.pallas import tpu as pltpu

F32 = jnp.float32
BF16 = jnp.bfloat16

EPS = 1e-6
N_META = 16
CHUNK = 128
PAD = CHUNK - N_META
SSD_HEAD_DIM = 64
SSD_HEADS = 32
SSD_GROUPS = 8
SSD_HPG = 4
SSD_STATE = 128
SSD_D_INNER = 2048
SSD_CONV = 4
MLA_HEADS = 16
MLA_NOPE = 64
MLA_ROPE = 32
MLA_V = 64
MLA_QK = 96
MLA_Q_RANK = 384
MLA_KV_RANK = 256
ROPE_THETA = 10000.0
LANES = 128
SUBLANES = 8
N_DEV = 8
VMEM_LIMIT = 56 * 1024 * 1024

ADAM_LR = 0.001
ADAM_B1 = 0.9
ADAM_B2 = 0.999
ADAM_EPS = 1e-08
ADAM_WD = 0.01
ADAM_STEP = 10

NEG = -1e30


def _row_tile(rows):
    return 384 if (rows % 384 == 0 and rows > 384) else 128


def _big_tile(rows):
    return 1408 if rows % 1408 == 0 else _row_tile(rows)


def _params(n_axes, vmem=VMEM_LIMIT):
    return pltpu.CompilerParams(dimension_semantics=("arbitrary",) * n_axes, vmem_limit_bytes=vmem)


def _dot(a, b, dims):
    return lax.dot_general(a.astype(BF16), b.astype(BF16), (dims, ((), ())), preferred_element_type=F32)


NN = ((1,), (0,))
NT = ((1,), (1,))
TN = ((0,), (0,))


@jax.custom_vjp
def bdot_nn(a, b):
    return _dot(a, b, NN)


@jax.custom_vjp
def bdot_nt(a, b):
    return _dot(a, b, NT)


@jax.custom_vjp
def bdot_tn(a, b):
    return _dot(a, b, TN)


bdot_nn.defvjp(lambda a, b: (_dot(a, b, NN), (a, b)),
               lambda r, g: (_dot(g, r[1], NT), _dot(r[0], g, TN)))
bdot_nt.defvjp(lambda a, b: (_dot(a, b, NT), (a, b)),
               lambda r, g: (_dot(g, r[1], NN), _dot(g, r[0], TN)))
bdot_tn.defvjp(lambda a, b: (_dot(a, b, TN), (a, b)),
               lambda r, g: (_dot(r[1], g, NT), _dot(r[0], g, NN)))


def _rows8(v):
    r, n = v.shape
    return v.reshape(r // SUBLANES, SUBLANES, n).sum(axis=0)


def _row_mask(i, tm):
    return (i * tm + lax.broadcasted_iota(jnp.int32, (tm, 1), 0)) >= PAD


def fused_mm(name, *, rows, k, n, tm, tn, tk=None, a_ops, pro, w, w_block, w_imap, dot, e_ops=(), epi, outs):
    tk = tk or k
    ni, nj, nk = rows // tm, n // tn, k // tk
    assert rows % tm == 0 and n % tn == 0 and k % tk == 0
    assert nk == 1 or nj == 1
    cache = nk == 1 and nj > 1
    na, ne, no = len(a_ops), len(e_ops), len(outs)

    def body(*refs):
        a_refs = refs[:na]
        w_ref = refs[na]
        e_refs = refs[na + 1:na + 1 + ne]
        o_refs = refs[na + 1 + ne:na + 1 + ne + no]
        scr = refs[na + 1 + ne + no:]
        i, j, kk = pl.program_id(0), pl.program_id(1), pl.program_id(2)
        if cache:
            a_scr = scr[0]

            @pl.when(j == 0)
            def _():
                a_scr[...] = pro(a_refs, o_refs, i)

            a = a_scr[...]
        else:
            a = pro(a_refs, o_refs, i)
        part = dot(a, w_ref)
        if nk == 1:
            epi(part, e_refs, o_refs, i, j)
        else:
            acc_ref = scr[0]

            @pl.when(kk == 0)
            def _():
                acc_ref[...] = part

            @pl.when(kk > 0)
            def _():
                acc_ref[...] += part

            @pl.when(kk == nk - 1)
            def _():
                epi(acc_ref[...], e_refs, o_refs, i, j)

    scratch = []
    if cache:
        scratch.append(pltpu.VMEM((tm, k), BF16))
    if nk > 1:
        scratch.append(pltpu.VMEM((tm, tn), F32))
    in_specs = [pl.BlockSpec(b, m) for (_, b, m) in a_ops]
    in_specs.append(pl.BlockSpec(w_block, w_imap))
    in_specs += [pl.BlockSpec(b, m) for (_, b, m) in e_ops]
    return pl.pallas_call(
        body, name=name, grid=(ni, nj, nk),
        in_specs=in_specs,
        out_specs=[pl.BlockSpec(b, m) for (_, _, b, m) in outs],
        out_shape=[jax.ShapeDtypeStruct(s, d) for (s, d, _, _) in outs],
        scratch_shapes=scratch,
        compiler_params=_params(3),
    )(*[a for (a, _, _) in a_ops], w, *[e for (e, _, _) in e_ops])


def _lw(w, block, imap):
    if isinstance(w, tuple):
        arr, layer = w
        return arr, (None,) + block, (lambda i, j, kk: (layer,) + imap(i, j, kk))
    return w, block, imap


def _dot_w(a, w_ref):
    return jnp.dot(a, w_ref[...], preferred_element_type=F32)


def _dot_wt(a, w_ref):
    return lax.dot_general(a, w_ref[...], (NT, ((), ())), preferred_element_type=F32)


def _rms_pro(h, g):
    r = lax.rsqrt(jnp.mean(h * h, axis=-1, keepdims=True) + EPS)
    return h * r * g


def _rms_bwd(dyn, h, g):
    r = lax.rsqrt(jnp.mean(h * h, axis=-1, keepdims=True) + EPS)
    xh = h * r
    t = dyn * g
    dh = r * (t - xh * jnp.mean(t * xh, axis=-1, keepdims=True))
    return dh, _rows8(dyn * xh)


def _acc_out(ref, val, first):
    @pl.when(first)
    def _():
        ref[...] = val

    @pl.when(jnp.logical_not(first))
    def _():
        ref[...] += val


def norm_mm(name, h, g, w, *, tn, k_cols=None, col_block=0, w_stacked=False, out_dtype=F32):
    rows = h.shape[0]
    k = k_cols or h.shape[1]
    wshape = (w[0].shape[1:] if isinstance(w, tuple) else w.shape)
    n = wshape[0] * wshape[2] if w_stacked else wshape[1]
    tm = _big_tile(rows)

    def pro(a_refs, o_refs, i):
        hn = _rms_pro(a_refs[0][...], a_refs[1][...]).astype(BF16)
        o_refs[1][...] = hn
        return hn

    def epi(acc, e_refs, o_refs, i, j):
        o_refs[0][...] = acc.astype(out_dtype)

    if w_stacked:
        w_block, w_imap = (None, k, tn), (lambda i, j, kk: (j, 0, 0))
    else:
        w_block, w_imap = (k, tn), (lambda i, j, kk: (0, j))
    w, w_block, w_imap = _lw(w, w_block, w_imap)
    return fused_mm(
        name, rows=rows, k=k, n=n, tm=tm, tn=tn,
        a_ops=[(h, (tm, k), lambda i, j, kk: (i, col_block)), (g, (1, k), lambda i, j, kk: (0, 0))],
        pro=pro, w=w, w_block=w_block, w_imap=w_imap, dot=_dot_w, epi=epi,
        outs=[((rows, n), out_dtype, (tm, tn), lambda i, j, kk: (i, j)),
              ((rows, k), BF16, (tm, k), lambda i, j, kk: (i, 0))])


def res_mm(name, a_ops, pro, k, w, res, *, tn, save_dtype=None, tm=None, tk=None):
    rows, n = res.shape
    tm = tm or _row_tile(rows)
    assert tk is None or save_dtype is None

    def pro2(a_refs, o_refs, i):
        a = pro(a_refs)
        if save_dtype is not None:
            o_refs[1][...] = a
        return a

    def epi(acc, e_refs, o_refs, i, j):
        o_refs[0][...] = e_refs[0][...] + acc

    outs = [((rows, n), F32, (tm, tn), lambda i, j, kk: (i, j))]
    if save_dtype is not None:
        outs.append(((rows, k), save_dtype, (tm, k), lambda i, j, kk: (i, 0)))
    w, w_block, w_imap = _lw(w, (tk or k, tn), lambda i, j, kk: (kk, j))
    out = fused_mm(
        name, rows=rows, k=k, n=n, tm=tm, tn=tn, tk=tk,
        a_ops=[(a, tuple(tm if x is None else x for x in b), m) for (a, b, m) in a_ops],
        pro=pro2, w=w, w_block=w_block, w_imap=w_imap, dot=_dot_w,
        e_ops=[(res, (tm, tn), lambda i, j, kk: (i, j))], epi=epi, outs=outs)
    return out if save_dtype is not None else out[0]


def wgrad_mm(name, a_ops, pro_a, g_ops, pro_g, *, rows, k1, n, t1, tn, out_shape=None, out_block=None, out_imap=None):
    tt = _row_tile(rows)
    n1, n2, nt = k1 // t1, n // tn, rows // tt
    assert k1 % t1 == 0 and n % tn == 0
    na = len(a_ops)

    def body(*refs):
        a_refs = refs[:na]
        g_refs = refs[na:-2]
        o_ref, acc = refs[-2:]
        t = pl.program_id(2)
        a = pro_a(a_refs).astype(BF16)
        g = pro_g(g_refs).astype(BF16)
        _acc_out(acc, lax.dot_general(a, g, (TN, ((), ())), preferred_element_type=F32), t == 0)

        @pl.when(t == nt - 1)
        def _():
            if len(o_ref.shape) == 3:
                ws = o_ref.shape[2]
                for q in range(o_ref.shape[0]):
                    o_ref[q] = acc[:, q * ws:(q + 1) * ws].astype(BF16)
            else:
                o_ref[...] = acc[...].astype(BF16)

    return pl.pallas_call(
        body, name=name, grid=(n1, n2, nt),
        in_specs=[pl.BlockSpec(b, m) for (_, b, m) in list(a_ops) + list(g_ops)],
        out_specs=pl.BlockSpec(out_block or (t1, tn), out_imap or (lambda a, b, t: (a, b))),
        out_shape=jax.ShapeDtypeStruct(out_shape or (k1, n), BF16),
        scratch_shapes=[pltpu.VMEM((t1, tn), F32)],
        compiler_params=_params(3),
    )(*[a for (a, _, _) in list(a_ops) + list(g_ops)])


def simple_wgrad(name, a, g, *, a_cols=None, a_col_block=0, t1=None, tn=None, **kw):
    rows = a.shape[0]
    k1 = a_cols or a.shape[1]
    n = g.shape[1]
    tt = _row_tile(rows)
    t1 = t1 or min(k1, 512)
    tn = tn or min(n, 1024)
    return wgrad_mm(
        name,
        [(a, (tt, t1), lambda x, y, t: (t, x + a_col_block * (k1 // t1)))], lambda r: r[0][...],
        [(g, (tt, tn), lambda x, y, t: (t, y))], lambda r: r[0][...],
        rows=rows, k1=k1, n=n, t1=t1, tn=tn, **kw)


def rms_bwd_mm(name, dz_ops, pro, k, w, w_block, w_imap, dot, h, g, dh, *, tk=None, h_cols=None, h_col_block=0,
               add_dh=True, tm=None):
    rows = h.shape[0]
    n = h_cols or h.shape[1]
    tm = tm or _big_tile(rows)
    ni = rows // tm
    w, w_block, w_imap = _lw(w, w_block, w_imap)

    def epi(acc, e_refs, o_refs, i, j):
        d, dg = _rms_bwd(acc, e_refs[0][...], e_refs[1][...])
        if add_dh:
            d = d + e_refs[2][...]
        o_refs[0][...] = jnp.where(_row_mask(i, tm), d, 0.0)
        _acc_out(o_refs[1], dg, i == 0)

    e_ops = [(h, (tm, n), lambda i, j, kk: (i, h_col_block)), (g, (1, n), lambda i, j, kk: (0, 0))]
    if add_dh:
        e_ops.append((dh, (tm, n), lambda i, j, kk: (i, 0)))
    return fused_mm(
        name, rows=rows, k=k, n=n, tm=tm, tn=n, tk=tk,
        a_ops=[(a, tuple(tm if x is None else x for x in b), m) for (a, b, m) in dz_ops],
        pro=lambda a_refs, o_refs, i: pro(a_refs), w=w, w_block=w_block, w_imap=w_imap, dot=dot,
        e_ops=e_ops, epi=epi,
        outs=[((rows, n), F32, (tm, n), lambda i, j, kk: (i, 0)),
              ((SUBLANES, n), F32, (SUBLANES, n), lambda i, j, kk: (0, 0))])


def _relu2(u):
    r = jnp.maximum(u.astype(F32), 0.0)
    return r * r


def mlp_fwd(tag, h, g, w_up_st, w_down):
    d_ff = w_down.shape[0]
    u, hn = norm_mm(f"mlp_up_{tag}", h, g, w_up_st, tn=w_up_st.shape[2], w_stacked=True, out_dtype=BF16)
    out = res_mm(f"mlp_down_{tag}", [(u, (None, 512), lambda i, j, kk: (i, kk))],
                 lambda r: _relu2(r[0][...]).astype(BF16), d_ff, w_down, h, tn=h.shape[1],
                 tm=_big_tile(h.shape[0]), tk=512)
    return out, (h, hn, u)


def mlp_bwd(tag, dh, saved, g, w_up_st, w_down):
    h, hn, u = saved
    rows, d = h.shape
    d_ff = w_down.shape[0]
    ts = w_up_st.shape[2]
    tm = _big_tile(rows)
    tt = _row_tile(rows)
    wd, wd_block, wd_imap = _lw(w_down, (512, d), lambda i, j, kk: (j, 0))

    def epi_du(acc, e_refs, o_refs, i, j):
        o_refs[0][...] = (acc * (2.0 * jnp.maximum(e_refs[0][...].astype(F32), 0.0))).astype(BF16)

    du, = fused_mm(
        f"mlp_du_{tag}", rows=rows, k=d, n=d_ff, tm=tm, tn=512,
        a_ops=[(dh, (tm, d), lambda i, j, kk: (i, 0))], pro=lambda a, o, i: a[0][...].astype(BF16),
        w=wd, w_block=wd_block, w_imap=wd_imap, dot=_dot_wt,
        e_ops=[(u, (tm, 512), lambda i, j, kk: (i, j))], epi=epi_du,
        outs=[((rows, d_ff), BF16, (tm, 512), lambda i, j, kk: (i, j))])
    half = d_ff // 2
    dw_down = wgrad_mm(
        f"mlp_dwdown_{tag}",
        [(u, (tt, half), lambda a, b, t: (t, a))], lambda r: _relu2(r[0][...]),
        [(dh, (tt, d), lambda a, b, t: (t, 0))], lambda r: r[0][...],
        rows=rows, k1=d_ff, n=d, t1=half, tn=d)
    dw_up = simple_wgrad(f"mlp_dwup_{tag}", hn, du, t1=d, tn=half, out_shape=(N_DEV, d, ts),
                         out_block=(half // ts, d, ts), out_imap=lambda a, b, t: (b, 0, 0))
    dh_in, dg = rms_bwd_mm(
        f"mlp_dh_{tag}", [(du, (None, ts), lambda i, j, kk: (i, kk))], lambda r: r[0][...], d_ff,
        w_up_st, (None, d, ts), lambda i, j, kk: (kk, 0, 0), _dot_wt, h, g, dh, tk=ts)
    return dh_in, dw_up, dw_down, dg


CONV_HALO = SUBLANES
CONV_W = 2 * SSD_D_INNER
CONV_BLK = 512


def _silu(x):
    return x * jax.nn.sigmoid(x)


def _conv_pre(ext_ref, w, b, tm):
    pre = b
    for k in range(SSD_CONV):
        pre = pre + w[k:k + 1, :] * ext_ref[pl.ds(CONV_HALO - (SSD_CONV - 1) + k, tm), :]
    return pre


def _conv_load(ext, parts, c):
    for cur, halo, lo, hi in parts:
        ext[pl.ds(0, CONV_HALO), lo:hi] = jnp.where(c > 0, halo[...], 0.0)
        ext[pl.ds(CONV_HALO, CHUNK), lo:hi] = cur[...]


def _conv_specs(idx):
    hb = CHUNK // CONV_HALO
    specs = []
    for w, blk in ((SSD_D_INNER, 1), (BC_W, SSD_D_INNER // BC_W + 2), (BC_W, SSD_D_INNER // BC_W + 3)):
        specs.append(pl.BlockSpec((CHUNK, w), lambda s, blk=blk: (idx(s), blk)))
        specs.append(pl.BlockSpec((CONV_HALO, w), lambda s, blk=blk: (jnp.maximum(idx(s) * hb - 1, 0), blk)))
    return specs


CONV_COLS = ((0, SSD_D_INNER), (SSD_D_INNER, SSD_D_INNER + SSD_GROUPS * SSD_STATE),
             (SSD_D_INNER + SSD_GROUPS * SSD_STATE, 2 * SSD_D_INNER))


@functools.partial(jax.custom_vjp, nondiff_argnums=(1,))
def _sub_row(x, h):
    return x[h:h + 1, :]


_sub_row.defvjp(
    lambda x, h: (x[h:h + 1, :], None),
    lambda h, _, g: (jnp.where(lax.broadcasted_iota(jnp.int32, (LANES, 1), 0) == h, g, 0.0),))


def _splitter(axis, size, count):
    def blocks(x):
        return tuple(lax.slice_in_dim(x, q * size, (q + 1) * size, axis=axis) for q in range(count))

    split = jax.custom_vjp(blocks)
    split.defvjp(lambda x: (blocks(x), None), lambda _, gs: (jnp.concatenate(gs, axis=axis),))
    return split


def _split3(x):
    hi = x.astype(BF16)
    r = x - hi.astype(F32)
    mid = r.astype(BF16)
    return hi, mid, (r - mid.astype(F32)).astype(BF16)


def _expand_impl(x, e):
    return sum(jnp.dot(t, e, preferred_element_type=F32) for t in _split3(x))


@jax.custom_vjp
def _expand(x, e):
    return _expand_impl(x, e)


def _expand_bwd(e, g):
    hi, mid, _ = _split3(g)
    dx = sum(lax.dot_general(t, e, (NT, ((), ())), preferred_element_type=F32) for t in (hi, mid))
    return dx, jnp.zeros_like(e)


_expand.defvjp(lambda x, e: (_expand_impl(x, e), e), _expand_bwd)

HEAD_PAIR = 2 * SSD_HEAD_DIM
GROUP_W = SSD_HPG * SSD_HEAD_DIM


def _ssd_chunk(xs, bm, cm, dtraw, prev, par, c, tri, e64, e128):
    li = lax.broadcasted_iota(jnp.int32, (CHUNK, CHUNK), 0)
    si = lax.broadcasted_iota(jnp.int32, (CHUNK, CHUNK), 1)
    causal = li >= si
    first_head = lax.broadcasted_iota(jnp.int32, (1, HEAD_PAIR), 1) < SSD_HEAD_DIM
    dt = jnp.where(_row_mask(c, CHUNK), jax.nn.softplus(dtraw + par[0:1, :]), 0.0)
    a = -jnp.exp(par[1:2, :])
    acs = jnp.dot(tri, dt * a, precision=lax.Precision.HIGHEST, preferred_element_type=F32)
    acs_t = acs.T
    last = acs[CHUNK - 1:CHUNK, :]
    misc = jnp.concatenate([jnp.exp(last), par[2:3, :], jnp.zeros((SUBLANES - 2, LANES), F32)], axis=0)
    wide = _expand(jnp.concatenate([dt, dt * jnp.exp(last - acs), jnp.exp(acs)], axis=0), e64)
    dt_w, dtend_w, start_w = _splitter(0, CHUNK, 3)(wide)
    misc_w = _expand(misc, e64)
    col_w = _splitter(1, CHUNK, SSD_HEADS)(_expand(acs, e128))
    groups = _splitter(1, GROUP_W, SSD_GROUPS)
    xs_g, prev_g, start_g = groups(xs), groups(prev), groups(start_w)
    xdt_p = _splitter(1, HEAD_PAIR, SSD_HEADS // 2)(xs * dt_w)
    xdtend_g = groups(xs * dtend_w)
    last_g, skip_g = groups(misc_w[0:1, :]), groups(misc_w[1:2, :])
    b_g, c_g = _splitter(1, SSD_STATE, SSD_GROUPS)(bm), _splitter(1, SSD_STATE, SSD_GROUPS)(cm)
    ys, news = [], []
    for g in range(SSD_GROUPS):
        cb = bdot_nt(c_g[g], b_g[g])
        st = bdot_nn(b_g[g].T, xdtend_g[g])
        y_off = bdot_nn(c_g[g], prev_g[g]) * start_g[g]
        pairs = []
        for q in range(SSD_HPG // 2):
            xp = xdt_p[g * (SSD_HPG // 2) + q]
            acc = None
            for r in range(2):
                head = SSD_HPG * g + 2 * q + r
                seg = jnp.where(causal, col_w[head] - _sub_row(acs_t, head), 0.0)
                decay = jnp.where(causal, jnp.exp(seg), 0.0)
                t = bdot_nn(cb * decay, jnp.where(first_head if r == 0 else jnp.logical_not(first_head), xp, 0.0))
                acc = t if acc is None else acc + t
            pairs.append(acc)
        ys.append(jnp.concatenate(pairs, axis=1) + y_off + xs_g[g] * skip_g[g])
        news.append(prev_g[g] * last_g[g] + st)
    return jnp.concatenate(ys, axis=1), jnp.concatenate(news, axis=1)


def _expanders():
    e64 = np.zeros((LANES, SSD_D_INNER), np.float32)
    e128 = np.zeros((LANES, SSD_HEADS * CHUNK), np.float32)
    for h in range(SSD_HEADS):
        e64[h, h * SSD_HEAD_DIM:(h + 1) * SSD_HEAD_DIM] = 1.0
        e128[h, h * CHUNK:(h + 1) * CHUNK] = 1.0
    return jnp.asarray(e64, BF16), jnp.asarray(e128, BF16)


def _tri():
    return jnp.asarray(np.tril(np.ones((CHUNK, CHUNK), np.float32)))


BC_W = SSD_GROUPS * SSD_STATE


def _conv_act(ext, refs, w, b, c):
    _conv_load(ext, [(refs[2 * p], refs[2 * p + 1]) + CONV_COLS[p] for p in range(3)], c)
    pre = _conv_pre(ext, w, b, CHUNK)
    return pre, jnp.where(_row_mask(c, CHUNK), _silu(pre), 0.0)


def ssd_fwd(name, zx, dt_block, par, conv_w, conv_b, comm=None):
    rows = zx.shape[0]
    nc = rows // CHUNK

    def body(*refs):
        xbc_refs, (dt_ref, par_ref, tri_ref, e64_ref, e128_ref, cw_ref, cb_ref) = refs[:6], refs[6:13]
        y_ref, st_ref, state, ext = refs[13:]
        c = pl.program_id(0)

        @pl.when(c == 0)
        def _():
            state[...] = jnp.zeros((SSD_STATE, SSD_D_INNER), F32)

        prev = state[...]
        st_ref[...] = prev
        _, act = _conv_act(ext, xbc_refs, cw_ref[...], cb_ref[...], c)
        (x0, x1), (b0, b1), (c0, c1) = CONV_COLS
        y, new = _ssd_chunk(act[:, x0:x1], act[:, b0:b1], act[:, c0:c1], dt_ref[...], prev, par_ref[...], c,
                            tri_ref[...], e64_ref[...], e128_ref[...])
        y_ref[...] = y
        state[...] = new

    const = lambda a: pl.BlockSpec(a.shape, lambda c: (0,) * a.ndim)
    consts = (par, _tri()) + _expanders() + (conv_w, conv_b)
    return carrier_call(
        name, body, (nc,),
        _conv_specs(lambda c: c) + [pl.BlockSpec((CHUNK, LANES), lambda c: (c, dt_block))] + [const(a) for a in consts],
        [pl.BlockSpec((CHUNK, SSD_D_INNER), lambda c: (c, 0)),
         pl.BlockSpec((None, SSD_STATE, SSD_D_INNER), lambda c: (c, 0, 0))],
        [jax.ShapeDtypeStruct((rows, SSD_D_INNER), F32),
         jax.ShapeDtypeStruct((nc, SSD_STATE, SSD_D_INNER), F32)],
        [pltpu.VMEM((SSD_STATE, SSD_D_INNER), F32), pltpu.VMEM((CHUNK + CONV_HALO, CONV_W), F32)],
        (zx,) * 7 + consts, comm)


def carrier_call(name, body, grid, in_specs, out_specs, out_shape, scratch_shapes, args, comm):
    body, in_specs, out_specs, out_shape, scratch_shapes, extra, n_own = with_comm(
        comm, body, grid, in_specs, out_specs, out_shape, scratch_shapes)
    res = pl.pallas_call(
        body, name=name, grid=grid, in_specs=in_specs, out_specs=out_specs, out_shape=out_shape,
        scratch_shapes=scratch_shapes, compiler_params=_params(len(grid)))(*args, *extra)
    return res[:n_own], res[n_own:]


def ssd_bwd(name, dy, zx, dt_block, states, par, conv_w, conv_b, comm=None):
    rows = zx.shape[0]
    nc = rows // CHUNK

    def body(*refs):
        dy_ref, xbc_refs = refs[0], refs[1:7]
        dt_ref, st_ref, par_ref, tri_ref, e64_ref, e128_ref, cw_ref, cb_ref = refs[7:15]
        du_ref, ddt_ref, dpar_ref, dcw_ref, dcb_ref, dstate, ext, dext = refs[15:]
        s = pl.program_id(0)
        c = nc - 1 - s

        @pl.when(s == 0)
        def _():
            dstate[...] = jnp.zeros((SSD_STATE, SSD_D_INNER), F32)
            dext[pl.ds(CHUNK, CONV_HALO), :] = jnp.zeros((CONV_HALO, CONV_W), F32)

        _, act = _conv_act(ext, xbc_refs, cw_ref[...], cb_ref[...], c)
        (x0, x1), (b0, b1), (c0, c1) = CONV_COLS

        def f(xs, bm, cm, dtraw, prev, par_v):
            return _ssd_chunk(xs, bm, cm, dtraw, prev, par_v, c, tri_ref[...], e64_ref[...], e128_ref[...])

        _, vjp = jax.vjp(f, act[:, x0:x1], act[:, b0:b1], act[:, c0:c1], dt_ref[...], st_ref[...], par_ref[...])
        dxs, dbm, dcm, ddt, dprev, dpar = vjp((dy_ref[...], dstate[...]))
        ddt_ref[...] = ddt.astype(BF16)
        dstate[...] = dprev
        _acc_out(dpar_ref, dpar, s == 0)

        dacts = [dxs[:, q * CONV_BLK:(q + 1) * CONV_BLK] for q in range((x1 - x0) // CONV_BLK)]
        dacts += [dbm[:, q * CONV_BLK:(q + 1) * CONV_BLK] for q in range((b1 - b0) // CONV_BLK)]
        dacts += [dcm[:, q * CONV_BLK:(q + 1) * CONV_BLK] for q in range((c1 - c0) // CONV_BLK)]
        for q, dact in enumerate(dacts):
            cols = slice(q * CONV_BLK, (q + 1) * CONV_BLK)
            w = cw_ref[:, cols]
            taps = [ext[pl.ds(CONV_HALO - (SSD_CONV - 1) + k, CHUNK), cols] for k in range(SSD_CONV)]
            pre = cb_ref[:, cols]
            for k in range(SSD_CONV):
                pre = pre + w[k:k + 1, :] * taps[k]
            sg = jax.nn.sigmoid(pre)
            dpre = jnp.where(_row_mask(c, CHUNK), dact * (sg * (1.0 + pre * (1.0 - sg))), 0.0)
            dext[pl.ds(0, CHUNK), cols] = dpre
            du = jnp.zeros((CHUNK, CONV_BLK), F32)
            for k in range(SSD_CONV):
                du = du + w[k:k + 1, :] * dext[pl.ds(SSD_CONV - 1 - k, CHUNK), cols]
            du_ref[:, cols] = du.astype(BF16)
            _acc_out(dcb_ref.at[:, cols], _rows8(dpre), s == 0)
            for k in range(SSD_CONV):
                _acc_out(dcw_ref.at[pl.ds(k * SUBLANES, SUBLANES), cols], _rows8(dpre * taps[k]), s == 0)
            dext[pl.ds(CHUNK, CONV_HALO), cols] = dpre[0:CONV_HALO, :]

    rev = lambda w, b: pl.BlockSpec((CHUNK, w), lambda s: (nc - 1 - s, b))
    const = lambda a: pl.BlockSpec(a.shape, lambda s: (0,) * a.ndim)
    consts = (par, _tri()) + _expanders() + (conv_w, conv_b)
    return carrier_call(
        name, body, (nc,),
        [rev(SSD_D_INNER, 0)] + _conv_specs(lambda s: nc - 1 - s)
        + [rev(LANES, dt_block), pl.BlockSpec((None, SSD_STATE, SSD_D_INNER), lambda s: (nc - 1 - s, 0, 0))]
        + [const(a) for a in consts],
        [rev(CONV_W, 0), rev(LANES, 0), pl.BlockSpec((SUBLANES, LANES), lambda s: (0, 0)),
         pl.BlockSpec((SSD_CONV * SUBLANES, CONV_W), lambda s: (0, 0)),
         pl.BlockSpec((SUBLANES, CONV_W), lambda s: (0, 0))],
        [jax.ShapeDtypeStruct((rows, CONV_W), BF16),
         jax.ShapeDtypeStruct((rows, LANES), BF16),
         jax.ShapeDtypeStruct((SUBLANES, LANES), F32),
         jax.ShapeDtypeStruct((SSD_CONV * SUBLANES, CONV_W), F32),
         jax.ShapeDtypeStruct((SUBLANES, CONV_W), F32)],
        [pltpu.VMEM((SSD_STATE, SSD_D_INNER), F32), pltpu.VMEM((CHUNK + CONV_HALO, CONV_W), F32),
         pltpu.VMEM((CHUNK + CONV_HALO, CONV_W), F32)],
        (dy,) + (zx,) * 7 + (states,) + consts, comm)


GN_W = SSD_D_INNER // SSD_GROUPS


def _gated_norm(y, z, ng):
    g = y * _silu(z)
    outs = []
    for q in range(SSD_GROUPS):
        gs = g[:, q * GN_W:(q + 1) * GN_W]
        outs.append(gs * lax.rsqrt(jnp.mean(gs * gs, axis=-1, keepdims=True) + EPS))
    return jnp.concatenate(outs, axis=1) * ng


def ssd_layer_fwd(tag, h, ln_g, w, comm=None):
    zx, hn = norm_mm(f"ssd_in_{tag}", h, ln_g, w["w_in"], tn=896)
    dt_block = 3 * SSD_D_INNER // LANES
    (y, states), cres = ssd_fwd(f"ssd_scan_{tag}", zx, dt_block, w["par"], w["conv_w"], w["conv_b"], comm)
    out, gn = res_mm(
        f"ssd_out_{tag}",
        [(y, (None, SSD_D_INNER), lambda i, j, kk: (i, 0)), (zx, (None, SSD_D_INNER), lambda i, j, kk: (i, 0)),
         (w["norm"], (1, SSD_D_INNER), lambda i, j, kk: (0, 0))],
        lambda r: _gated_norm(r[0][...], r[1][...], r[2][...]).astype(BF16),
        SSD_D_INNER, w["w_out"], h, tn=512, save_dtype=BF16)
    return out, (h, hn, zx, y, states, gn), cres


def ssd_layer_bwd(tag, dh, saved, ln_g, w, comm=None):
    h, hn, zx, y, states, gn = saved
    rows, d = h.shape
    tm = _row_tile(rows)
    dt_block = 3 * SSD_D_INNER // LANES
    dw_out = simple_wgrad(f"ssd_dwout_{tag}", gn, dh, t1=SSD_D_INNER, tn=d)

    def epi_gate(acc, e_refs, o_refs, i, j):
        _, vjp = jax.vjp(_gated_norm, e_refs[0][...], e_refs[1][...], e_refs[2][...])
        dy, dz, dng = vjp(acc)
        o_refs[0][...] = dy
        o_refs[1][...] = dz.astype(BF16)
        row0 = lax.broadcasted_iota(jnp.int32, (SUBLANES, 1), 0) == 0
        _acc_out(o_refs[2], jnp.where(row0, dng, 0.0), i == 0)

    wo, wo_block, wo_imap = _lw(w["w_out"], (SSD_D_INNER, d), lambda i, j, kk: (0, 0))
    dy, dz, dnorm = fused_mm(
        f"ssd_dgate_{tag}", rows=rows, k=d, n=SSD_D_INNER, tm=tm, tn=SSD_D_INNER,
        a_ops=[(dh, (tm, d), lambda i, j, kk: (i, 0))], pro=lambda a, o, i: a[0][...].astype(BF16),
        w=wo, w_block=wo_block, w_imap=wo_imap, dot=_dot_wt,
        e_ops=[(y, (tm, SSD_D_INNER), lambda i, j, kk: (i, 0)), (zx, (tm, SSD_D_INNER), lambda i, j, kk: (i, 0)),
               (w["norm"], (1, SSD_D_INNER), lambda i, j, kk: (0, 0))],
        epi=epi_gate,
        outs=[((rows, SSD_D_INNER), F32, (tm, SSD_D_INNER), lambda i, j, kk: (i, 0)),
              ((rows, SSD_D_INNER), BF16, (tm, SSD_D_INNER), lambda i, j, kk: (i, 0)),
              ((SUBLANES, SSD_D_INNER), F32, (SUBLANES, SSD_D_INNER), lambda i, j, kk: (0, 0))])
    (dxbc, ddt, dpar, dcw, dcb), cres = ssd_bwd(f"ssd_dscan_{tag}", dy, zx, dt_block, states, w["par"],
                                                w["conv_w"], w["conv_b"], comm)
    dzx = jnp.concatenate([dz, dxbc, ddt], axis=1)
    k = dzx.shape[1]
    dw_in = simple_wgrad(f"ssd_dwin_{tag}", hn, dzx, t1=d, tn=896)
    dh_in, dln = rms_bwd_mm(
        f"ssd_dh_{tag}", [(dzx, (None, 896), lambda i, j, kk: (i, kk))], lambda r: r[0][...], k,
        w["w_in"], (d, 896), lambda i, j, kk: (0, kk), _dot_wt, h, ln_g, dh, tk=896,
        tm=704 if rows % 704 == 0 else None)
    grads = dict(w_in=dw_in, w_out=dw_out, conv_w=dcw, conv_b=dcb, par=dpar, norm=dnorm, ln=dln)
    return dh_in, grads, cres


HP = 2 * LANES
VP = 2 * MLA_V
N_PAIRS = MLA_HEADS // 2
ATT_SCALE = MLA_QK ** -0.5
LOG2E = float(np.log2(np.e))
LN2 = float(np.log(2.0))
ROT = MLA_ROPE // 2


def rope_tables(rows):
    inv = 1.0 / (ROPE_THETA ** (jnp.arange(0, MLA_ROPE, 2, dtype=F32) / MLA_ROPE))
    pos = jnp.arange(rows, dtype=F32) - PAD
    ang = pos[:, None] * inv[None, :]
    cos, sin = jnp.cos(ang), jnp.sin(ang)
    one = jnp.ones((rows, MLA_NOPE), F32)
    zero = jnp.zeros((rows, LANES - MLA_QK), F32)
    zn = jnp.zeros((rows, MLA_NOPE), F32)
    zr = jnp.zeros((rows, ROT), F32)
    cosf = jnp.concatenate([one, cos, cos, zero], axis=1)
    sina = jnp.concatenate([zn, -sin, zr, zero], axis=1)
    sinb = jnp.concatenate([zn, zr, sin, zero], axis=1)
    return cosf, sina, sinb


def _qk_norm_rope(x, g, cosf, sina, sinb):
    r = lax.rsqrt(jnp.sum(x * x, axis=-1, keepdims=True) * (1.0 / MLA_QK) + EPS)
    xn = x * r * g
    return xn * cosf + pltpu.roll(xn, LANES - ROT, 1) * sina + pltpu.roll(xn, ROT, 1) * sinb


def _qk_norm_rope_bwd(dout, x, g, cosf, sina, sinb):
    dxn = dout * cosf + pltpu.roll(dout * sina, ROT, 1) + pltpu.roll(dout * sinb, LANES - ROT, 1)
    r = lax.rsqrt(jnp.sum(x * x, axis=-1, keepdims=True) * (1.0 / MLA_QK) + EPS)
    xh = x * r
    t = dxn * g
    dx = r * (t - xh * (jnp.sum(t * xh, axis=-1, keepdims=True) * (1.0 / MLA_QK)))
    return dx, _rows8(dxn * xh)


def _rope_lanes():
    lane = lax.broadcasted_iota(jnp.int32, (1, LANES), 1)
    return jnp.logical_and(lane >= MLA_NOPE, lane < MLA_QK)


QW = MLA_HEADS * LANES
VW = MLA_HEADS * MLA_V


def qk_prep(name, qraw, kvraw, lat, kpe_block, qg, kg, tabs):
    rows = qraw.shape[0]
    tm = _row_tile(rows)

    def body(q_ref, k0_ref, k1_ref, v_ref, pe_ref, qg_ref, kg_ref, c_ref, sa_ref, sb_ref,
             qo_ref, ko_ref, kt_ref, vo_ref, vt_ref):
        tab = (c_ref[...], sa_ref[...], sb_ref[...])
        pe = pe_ref[...]
        for hd in range(MLA_HEADS):
            sl = slice(hd * LANES, (hd + 1) * LANES)
            qo_ref[:, sl] = _qk_norm_rope(q_ref[:, sl].astype(F32), qg_ref[...], *tab).astype(BF16)
            kr = k0_ref if hd < MLA_HEADS // 2 else k1_ref
            ks = slice((hd % (MLA_HEADS // 2)) * LANES, (hd % (MLA_HEADS // 2) + 1) * LANES)
            kk = _qk_norm_rope(kr[:, ks].astype(F32) + pe, kg_ref[...], *tab)
            ko_ref[:, sl] = kk.astype(BF16)
            kt_ref[sl, :] = kk.T.astype(BF16)
        vo_ref[...] = v_ref[...].astype(BF16)
        for c in range(VW // LANES):
            sl = slice(c * LANES, (c + 1) * LANES)
            vt_ref[sl, :] = v_ref[:, sl].astype(F32).T.astype(BF16)

    row = lambda w, b: pl.BlockSpec((tm, w), lambda i: (i, b))
    col = lambda w: pl.BlockSpec((w, tm), lambda i: (0, i))
    one = pl.BlockSpec((1, LANES), lambda i: (0, 0))
    return pl.pallas_call(
        body, name=name, grid=(rows // tm,),
        in_specs=[row(QW, 0), row(VW, 0), row(VW, 1), row(VW, 2), row(LANES, kpe_block), one, one,
                  row(LANES, 0), row(LANES, 0), row(LANES, 0)],
        out_specs=[row(QW, 0), row(QW, 0), col(QW), row(VW, 0), col(VW)],
        out_shape=[jax.ShapeDtypeStruct((rows, QW), BF16), jax.ShapeDtypeStruct((rows, QW), BF16),
                   jax.ShapeDtypeStruct((QW, rows), BF16), jax.ShapeDtypeStruct((rows, VW), BF16),
                   jax.ShapeDtypeStruct((VW, rows), BF16)],
        compiler_params=_params(1),
    )(qraw, kvraw, kvraw, kvraw, lat, qg, kg, *tabs)


def qk_prep_bwd(name, dq_t, dk, dv, qraw, kvraw, lat, kpe_block, qg, kg, tabs):
    rows = qraw.shape[0]
    tm = _row_tile(rows)

    def body(dq_ref, dk_ref, dv_ref, q_ref, k0_ref, k1_ref, pe_ref, qg_ref, kg_ref, c_ref, sa_ref, sb_ref,
             dqo_ref, dkvo_ref, dpe_ref, dqg_ref, dkg_ref):
        i = pl.program_id(0)
        tab = (c_ref[...], sa_ref[...], sb_ref[...])
        pe = pe_ref[...]
        dpe = jnp.zeros((tm, LANES), F32)
        dqg = jnp.zeros((SUBLANES, LANES), F32)
        dkg = jnp.zeros((SUBLANES, LANES), F32)
        for hd in range(MLA_HEADS):
            sl = slice(hd * LANES, (hd + 1) * LANES)
            dx, dg = _qk_norm_rope_bwd(dq_ref[sl, :].T, q_ref[:, sl].astype(F32), qg_ref[...], *tab)
            dqo_ref[:, sl] = dx.astype(BF16)
            dqg = dqg + dg
            kr = k0_ref if hd < MLA_HEADS // 2 else k1_ref
            ks = slice((hd % (MLA_HEADS // 2)) * LANES, (hd % (MLA_HEADS // 2) + 1) * LANES)
            dx, dg = _qk_norm_rope_bwd(dk_ref[:, sl], kr[:, ks].astype(F32) + pe, kg_ref[...], *tab)
            dkvo_ref[:, sl] = dx.astype(BF16)
            dpe = dpe + dx
            dkg = dkg + dg
        dkvo_ref[:, QW:QW + VW] = dv_ref[...].astype(BF16)
        dpe_ref[...] = jnp.where(_rope_lanes(), dpe, 0.0)
        _acc_out(dqg_ref, dqg, i == 0)
        _acc_out(dkg_ref, dkg, i == 0)

    row = lambda w, b: pl.BlockSpec((tm, w), lambda i: (i, b))
    one = pl.BlockSpec((1, LANES), lambda i: (0, 0))
    acc = pl.BlockSpec((SUBLANES, LANES), lambda i: (0, 0))
    return pl.pallas_call(
        body, name=name, grid=(rows // tm,),
        in_specs=[pl.BlockSpec((QW, tm), lambda i: (0, i)), row(QW, 0), row(VW, 0), row(QW, 0), row(VW, 0), row(VW, 1),
                  row(LANES, kpe_block), one, one, row(LANES, 0), row(LANES, 0), row(LANES, 0)],
        out_specs=[row(QW, 0), row(QW + VW, 0), row(LANES, 0), acc, acc],
        out_shape=[jax.ShapeDtypeStruct((rows, QW), BF16), jax.ShapeDtypeStruct((rows, QW + VW), BF16),
                   jax.ShapeDtypeStruct((rows, LANES), F32),
                   jax.ShapeDtypeStruct((SUBLANES, LANES), F32), jax.ShapeDtypeStruct((SUBLANES, LANES), F32)],
        compiler_params=_params(1),
    )(dq_t, dk, dv, qraw, kvraw, kvraw, lat, qg, kg, *tabs)


def _att_mask_t(qb, kb, bt):
    kpos = kb * bt + lax.broadcasted_iota(jnp.int32, (bt, bt), 0)
    qpos = qb * bt + lax.broadcasted_iota(jnp.int32, (bt, bt), 1)
    return jnp.logical_and(kpos <= qpos, jnp.logical_or(kpos >= PAD, qpos < PAD))


def attn_fwd(name, q, k, vt, comm=None):
    rows = q.shape[0]
    bt = _row_tile(rows)
    nb = rows // bt
    assert bt >= CHUNK

    def body(q_ref, k_ref, vt_ref, o_ref, lse_ref):
        qi = pl.program_id(1)
        lse_ref[...] = jnp.zeros((SUBLANES, bt), F32)

        def scores(kb):
            r0 = pl.multiple_of(kb * bt, LANES)
            return tuple(lax.dot_general(k_ref[pl.ds(r0, bt), hh * LANES:(hh + 1) * LANES],
                                         q_ref[:, hh * LANES:(hh + 1) * LANES], (NT, ((), ())),
                                         preferred_element_type=F32) for hh in range(2))

        def tile(kb, carry, s_pair, masked):
            r0 = pl.multiple_of(kb * bt, LANES)
            new = []
            for hh in range(2):
                m, l, acc = carry[3 * hh:3 * hh + 3]
                vs = slice(hh * MLA_V, (hh + 1) * MLA_V)
                s = s_pair[hh] * (ATT_SCALE * LOG2E)
                if masked:
                    s = jnp.where(_att_mask_t(qi, kb, bt), s, NEG)
                m_new = jnp.maximum(m, jnp.max(s, axis=0, keepdims=True))
                alpha = jnp.exp2(m - m_new)
                p = jnp.exp2(s - m_new)
                l = alpha * l + jnp.sum(p, axis=0, keepdims=True)
                acc = alpha * acc + jnp.dot(vt_ref[vs, pl.ds(r0, bt)], p.astype(BF16), preferred_element_type=F32)
                new += [m_new, l, acc]
            return tuple(new)

        init = (jnp.full((1, bt), NEG, F32), jnp.zeros((1, bt), F32), jnp.zeros((MLA_V, bt), F32)) * 2
        s_next = scores(jnp.minimum(1, qi))
        carry = tile(0, init, scores(0), True)

        def rest(args):
            def mid(kb, state):
                carry, s_cur = state
                s_after = scores(kb + 1)
                return tile(kb, carry, s_cur, False), s_after

            carry, s_last = lax.fori_loop(1, qi, mid, args)
            return tile(qi, carry, s_last, True)

        carry = lax.cond(qi > 0, rest, lambda args: args[0], (carry, s_next))
        for hh in range(2):
            m, l, acc = carry[3 * hh:3 * hh + 3]
            o_ref[hh * MLA_V:(hh + 1) * MLA_V, :] = acc / l
            lse_ref[hh:hh + 1, :] = m * LN2 + jnp.log(l)

    return carrier_call(
        name, body, (N_PAIRS, nb),
        [pl.BlockSpec((bt, HP), lambda p, i: (i, p)),
         pl.BlockSpec((rows, HP), lambda p, i: (0, p)),
         pl.BlockSpec((VP, rows), lambda p, i: (p, 0))],
        [pl.BlockSpec((VP, bt), lambda p, i: (p, i)),
         pl.BlockSpec((None, SUBLANES, bt), lambda p, i: (p, 0, i))],
        [jax.ShapeDtypeStruct((VW, rows), F32), jax.ShapeDtypeStruct((N_PAIRS, SUBLANES, rows), F32)],
        [], (q, k, vt), comm)


def attn_bwd(name, q, k, kt, v, do_t, lse, delta, comm=None):
    rows = q.shape[0]
    bt = _row_tile(rows)
    nb = rows // bt

    def body(q_ref, k_ref, kt_ref, v_ref, do_ref, lse_ref, dl_ref, dq_ref, dk_ref, dv_ref, dk_scr, dv_scr):
        ki = pl.program_id(1)

        @pl.when(ki == 0)
        def _():
            dq_ref[...] = jnp.zeros((HP, rows), F32)

        dk_scr[...] = jnp.zeros((bt, HP), F32)
        dv_scr[...] = jnp.zeros((bt, VP), F32)

        def tile(qb, masked):
            c0 = pl.multiple_of(qb * bt, LANES)
            for hh in range(2):
                qs = slice(hh * LANES, (hh + 1) * LANES)
                vs = slice(hh * MLA_V, (hh + 1) * MLA_V)
                qv = q_ref[pl.ds(c0, bt), qs]
                dov = do_ref[vs, pl.ds(c0, bt)]
                lse = lse_ref[hh:hh + 1, pl.ds(c0, bt)]
                dl = dl_ref[hh:hh + 1, pl.ds(c0, bt)]
                s = lax.dot_general(k_ref[:, qs], qv, (NT, ((), ())), preferred_element_type=F32) * ATT_SCALE
                p = jnp.exp(s - lse)
                if masked:
                    p = jnp.where(_att_mask_t(qb, ki, bt), p, 0.0)
                dp = jnp.dot(v_ref[:, vs], dov, preferred_element_type=F32)
                ds = (p * (dp - dl) * ATT_SCALE).astype(BF16)
                dv_scr[:, vs] += lax.dot_general(p.astype(BF16), dov, (NT, ((), ())), preferred_element_type=F32)
                dk_scr[:, qs] += jnp.dot(ds, qv, preferred_element_type=F32)
                dq_ref[qs, pl.ds(c0, bt)] += jnp.dot(kt_ref[qs, :], ds, preferred_element_type=F32)

        @pl.when(ki == 0)
        def _():
            def every(qb, carry):
                tile(qb, True)
                return carry

            lax.fori_loop(0, nb, every, 0)

        @pl.when(ki > 0)
        def _():
            tile(ki, True)

            def later(qb, carry):
                tile(qb, False)
                return carry

            lax.fori_loop(ki + 1, nb, later, 0)

        dk_ref[...] = dk_scr[...]
        dv_ref[...] = dv_scr[...]

    stat = pl.BlockSpec((None, SUBLANES, rows), lambda p, i: (p, 0, 0))
    return carrier_call(
        name, body, (N_PAIRS, nb),
        [pl.BlockSpec((rows, HP), lambda p, i: (0, p)),
         pl.BlockSpec((bt, HP), lambda p, i: (i, p)),
         pl.BlockSpec((HP, bt), lambda p, i: (p, i)),
         pl.BlockSpec((bt, VP), lambda p, i: (i, p)),
         pl.BlockSpec((VP, rows), lambda p, i: (p, 0)),
         stat, stat],
        [pl.BlockSpec((HP, rows), lambda p, i: (p, 0)),
         pl.BlockSpec((bt, HP), lambda p, i: (i, p)),
         pl.BlockSpec((bt, VP), lambda p, i: (i, p))],
        [jax.ShapeDtypeStruct((QW, rows), F32), jax.ShapeDtypeStruct((rows, QW), F32),
         jax.ShapeDtypeStruct((rows, VW), F32)],
        [pltpu.VMEM((bt, HP), F32), pltpu.VMEM((bt, VP), F32)],
        (q, k, kt, v, do_t, lse, delta), comm)


def _dot_cast_w(a, w_ref):
    return jnp.dot(a, w_ref[...].astype(BF16), preferred_element_type=F32)


def _dot_cast_wt(a, w_ref):
    return lax.dot_general(a, w_ref[...].astype(BF16), (NT, ((), ())), preferred_element_type=F32)


LAT_W = 768
KPE_BLOCK = MLA_Q_RANK // LANES
KV_BLOCK = (MLA_Q_RANK + LANES) // MLA_KV_RANK


def mla_layer_fwd(tag, h, ln_g, w, tabs, comm=None):
    lat, hn = norm_mm(f"mla_in_{tag}", h, ln_g, w["w_in"], tn=LAT_W)
    qraw, qn = norm_mm(f"mla_q_{tag}", lat, w["q_a"], w["w_q"], tn=512, k_cols=MLA_Q_RANK, col_block=0,
                       out_dtype=BF16)
    kvraw, kvn = norm_mm(f"mla_kv_{tag}", lat, w["kv_a"], w["w_kv"], tn=512, k_cols=MLA_KV_RANK, col_block=KV_BLOCK,
                         out_dtype=BF16)
    q, k, kt, v, vt = qk_prep(f"mla_prep_{tag}", qraw, kvraw, lat, KPE_BLOCK, w["q_norm"], w["k_norm"], tabs)
    (o_t, lse), cres = attn_fwd(f"mla_attn_{tag}", q, k, vt, comm)
    out = res_mm(f"mla_out_{tag}", [(o_t, (VW, None), lambda i, j, kk: (0, i))],
                 lambda r: r[0][...].T.astype(BF16), VW, w["w_out"], h, tn=512, tm=_big_tile(h.shape[0]))
    return out, (h, hn, lat, qn, kvn, qraw, kvraw, q, k, kt, v, o_t, lse), cres


def mla_layer_bwd(tag, dh, saved, ln_g, w, tabs, comm=None):
    h, hn, lat, qn, kvn, qraw, kvraw, q, k, kt, v, o_t, lse = saved
    rows, d = h.shape
    tm = _row_tile(rows)

    def epi_set(acc, e_refs, o_refs, i, j):
        o_refs[0][...] = acc.astype(BF16)

    dw_out, = fused_mm(
        f"mla_dwout_{tag}", rows=VW, k=rows, n=d, tm=512, tn=d, tk=tm,
        a_ops=[(o_t, (512, tm), lambda i, j, kk: (i, kk))], pro=lambda a, o_, i: a[0][...].astype(BF16),
        w=dh, w_block=(tm, d), w_imap=lambda i, j, kk: (kk, 0), dot=_dot_cast_w, epi=epi_set,
        outs=[((VW, d), BF16, (512, d), lambda i, j, kk: (i, 0))])

    def epi_do(acc, e_refs, o_refs, i, j):
        o_refs[0][...] = acc.astype(BF16)
        prod = acc * e_refs[0][...]
        o_refs[1][...] = jnp.zeros((N_PAIRS, SUBLANES, tm), F32)
        for hd in range(MLA_HEADS):
            o_refs[1][hd // 2, hd % 2:hd % 2 + 1, :] = jnp.sum(prod[hd * MLA_V:(hd + 1) * MLA_V, :], axis=0,
                                                               keepdims=True)

    wo, wo_block, wo_imap = _lw(w["w_out"], (VW, d), lambda i, j, kk: (0, 0))
    do_t, delta = fused_mm(
        f"mla_do_{tag}", rows=VW, k=d, n=rows, tm=VW, tn=tm,
        a_ops=[(wo, wo_block, wo_imap)], pro=lambda a, o_, i: a[0][...],
        w=dh, w_block=(tm, d), w_imap=lambda i, j, kk: (j, 0), dot=_dot_cast_wt,
        e_ops=[(o_t, (VW, tm), lambda i, j, kk: (0, j))], epi=epi_do,
        outs=[((VW, rows), BF16, (VW, tm), lambda i, j, kk: (0, j)),
              ((N_PAIRS, SUBLANES, rows), F32, (N_PAIRS, SUBLANES, tm), lambda i, j, kk: (0, 0, j))])
    (dq_t, dk, dv), cres = attn_bwd(f"mla_dattn_{tag}", q, k, kt, v, do_t, lse, delta, comm)
    dqraw, dkvraw, dpe, dqg, dkg = qk_prep_bwd(f"mla_dprep_{tag}", dq_t, dk, dv, qraw, kvraw, lat, KPE_BLOCK,
                                               w["q_norm"], w["k_norm"], tabs)
    dw_q = simple_wgrad(f"mla_dwq_{tag}", qn, dqraw, t1=MLA_Q_RANK, tn=512)
    dqlat, dqa = rms_bwd_mm(
        f"mla_dqlat_{tag}", [(dqraw, (None, QW), lambda i, j, kk: (i, 0))], lambda r: r[0][...], QW,
        w["w_q"], (MLA_Q_RANK, QW), lambda i, j, kk: (0, 0), _dot_wt, lat, w["q_a"], None,
        h_cols=MLA_Q_RANK, h_col_block=0, add_dh=False)
    dw_kv = simple_wgrad(f"mla_dwkv_{tag}", kvn, dkvraw, t1=MLA_KV_RANK, tn=512)
    dkvlat, dkva = rms_bwd_mm(
        f"mla_dkvlat_{tag}", [(dkvraw, (None, QW + VW), lambda i, j, kk: (i, 0))], lambda r: r[0][...], QW + VW,
        w["w_kv"], (MLA_KV_RANK, QW + VW), lambda i, j, kk: (0, 0), _dot_wt, lat, w["kv_a"], None,
        h_cols=MLA_KV_RANK, h_col_block=KV_BLOCK, add_dh=False)
    dlat = jnp.concatenate([dqlat.astype(BF16), dpe.astype(BF16), dkvlat.astype(BF16)], axis=1)
    dw_in = simple_wgrad(f"mla_dwin_{tag}", hn, dlat, t1=512, tn=LAT_W)
    dh_in, dln = rms_bwd_mm(
        f"mla_dh_{tag}", [(dlat, (None, LAT_W), lambda i, j, kk: (i, 0))], lambda r: r[0][...], LAT_W,
        w["w_in"], (d, LAT_W), lambda i, j, kk: (0, 0), _dot_wt, h, ln_g, dh)
    grads = dict(w_in=dw_in, w_q=dw_q, w_kv=dw_kv, w_out=dw_out, q_a=dqa, kv_a=dkva, q_norm=dqg, k_norm=dkg, ln=dln)
    return dh_in, grads, cres


def loss_head(h, target):
    rows, d = h.shape
    nb = rows // CHUNK

    def body(h_ref, t_ref, l_ref, dh_ref):
        i = pl.program_id(0)
        err = jnp.where(i > 0, h_ref[...] - t_ref[...], 0.0)
        dh_ref[...] = err * (1.0 / d)
        _acc_out(l_ref, _rows8(err * err) * (0.5 / d), i == 0)

    return pl.pallas_call(
        body, name="loss_head", grid=(nb,),
        in_specs=[pl.BlockSpec((CHUNK, d), lambda i: (i, 0)),
                  pl.BlockSpec((CHUNK, d), lambda i: (jnp.maximum(i - 1, 0), 0))],
        out_specs=[pl.BlockSpec((SUBLANES, d), lambda i: (0, 0)), pl.BlockSpec((CHUNK, d), lambda i: (i, 0))],
        out_shape=[jax.ShapeDtypeStruct((SUBLANES, d), F32), jax.ShapeDtypeStruct((rows, d), F32)],
        compiler_params=_params(1),
    )(h, target)


def _adamw(w, g, m, v):
    m = ADAM_B1 * m + (1.0 - ADAM_B1) * g
    v = ADAM_B2 * v + (1.0 - ADAM_B2) * jnp.square(g)
    m_hat = m / (1.0 - ADAM_B1 ** ADAM_STEP)
    v_hat = v / (1.0 - ADAM_B2 ** ADAM_STEP)
    delta = -ADAM_LR * (m_hat / (jnp.sqrt(v_hat) + ADAM_EPS) + ADAM_WD * w)
    return delta, m, v


def reduce_adamw(name, recvs, w, m, v):
    nl, r, c = w.shape
    tr = 128 if r % 128 == 0 else r
    nr = r // tr

    def body(*refs):
        r_refs = refs[:nl]
        w_ref, m_ref, v_ref, g_ref, d_ref, mo_ref, vo_ref = refs[nl:]
        layer = pl.program_id(0)
        for l in range(nl):
            @pl.when(layer == l)
            def _(l=l):
                g = r_refs[l][0].astype(F32)
                for s in range(1, N_DEV):
                    g = g + r_refs[l][s].astype(F32)
                g_ref[...] = g
                d_ref[...], mo_ref[...], vo_ref[...] = _adamw(w_ref[...], g, m_ref[...], v_ref[...])

    def recv_spec(l):
        return pl.BlockSpec((N_DEV, tr, c),
                            lambda y, i: (0, jnp.where(y == l, i, jnp.where(y < l, 0, nr - 1)), 0))

    blk = pl.BlockSpec((None, tr, c), lambda y, i: (y, i, 0))
    return pl.pallas_call(
        body, name=name, grid=(nl, nr),
        in_specs=[recv_spec(l) for l in range(nl)] + [blk, blk, blk],
        out_specs=[blk] * 4, out_shape=[jax.ShapeDtypeStruct((nl, r, c), F32)] * 4,
        compiler_params=_params(2),
    )(*recvs, w, m, v)


def small_reduce(recv):
    def body(r_ref, o_ref):
        g = r_ref[0]
        for s in range(1, N_DEV):
            g = g + r_ref[s]
        o_ref[...] = g

    return pl.pallas_call(body, name="small_reduce", out_shape=jax.ShapeDtypeStruct(recv.shape[1:], F32))(recv)


def small_adamw(w, g, m, v):
    def body(w_ref, g_ref, m_ref, v_ref, d_ref, mo_ref, vo_ref):
        d_ref[...], mo_ref[...], vo_ref[...] = _adamw(w_ref[...], g_ref[...], m_ref[...], v_ref[...])

    return pl.pallas_call(body, name="small_adamw", out_shape=[jax.ShapeDtypeStruct(w.shape, F32)] * 3)(w, g, m, v)


def _pack(parts):
    flat, meta, off = [], [], 0
    for p in parts:
        n = int(np.prod(p.shape))
        flat.append(p.reshape(-1).astype(F32))
        meta.append((off, p.shape))
        off += n
    total = -(-off // (SUBLANES * LANES)) * (SUBLANES * LANES)
    flat.append(jnp.zeros((total - off,), F32))
    return jnp.concatenate(flat).reshape(total // LANES, LANES), meta


def _unpack(packed, meta):
    flat = packed.reshape(-1)
    return [flat[off:off + int(np.prod(shape))].reshape(shape) for off, shape in meta]


MESH = pl.DeviceIdType.MESH
N_PEERS = N_DEV - 1


def _me():
    return lax.axis_index("x"), lax.axis_index("y"), lax.axis_index("c")


def _peer(k):
    x, y, c = _me()
    return (1 - x if k & 4 else x, 1 - y if k & 2 else y, 1 - c if k & 1 else c)


def _dev_index(pos):
    return 4 * pos[0] + 2 * pos[1] + pos[2]


def make_comm(items):
    n = len(items)

    def part(ref, a, idx):
        rows = items[a][1]
        if rows == "all":
            return ref
        return ref.at[idx] if rows is None else ref.at[pl.ds(idx * rows, rows)]

    def part_shape(a):
        arr, rows = items[a]
        if rows == "all":
            return arr.shape
        return arr.shape[1:] if rows is None else (rows,) + arr.shape[1:]

    def run(phase, ins, outs, send_sems, recv_sems, local_sems):
        me = _dev_index(_me())
        for a in range(n):
            local = pltpu.make_async_copy(part(ins[a], a, me), outs[a].at[me], local_sems.at[a])
            if phase == "start":
                local.start()
            for k in range(1, N_DEV):
                peer = _peer(k)
                if phase == "start":
                    pltpu.make_async_remote_copy(
                        src_ref=part(ins[a], a, _dev_index(peer)), dst_ref=outs[a].at[me],
                        send_sem=send_sems.at[a, k - 1], recv_sem=recv_sems.at[a, k - 1],
                        device_id=peer, device_id_type=MESH).start()
                else:
                    cp = pltpu.make_async_remote_copy(
                        src_ref=part(ins[a], a, me), dst_ref=outs[a].at[_dev_index(peer)],
                        send_sem=send_sems.at[a, k - 1], recv_sem=recv_sems.at[a, k - 1],
                        device_id=peer, device_id_type=MESH)
                    cp.wait_recv()
                    cp.wait_send()
            if phase == "wait":
                local.wait()

    return dict(
        ins=[it[0] for it in items],
        outs=[jax.ShapeDtypeStruct((N_DEV,) + part_shape(a), items[a][0].dtype) for a in range(n)],
        sems=[pltpu.SemaphoreType.DMA((n, N_PEERS)), pltpu.SemaphoreType.DMA((n, N_PEERS)),
              pltpu.SemaphoreType.DMA((n,))],
        run=run)


ANY_SPEC = pl.BlockSpec(memory_space=pl.ANY)


def comm_call(name, comm):
    n, no = len(comm["ins"]), len(comm["outs"])

    def body(*refs):
        comm["run"]("start", refs[:n], refs[n:n + no], *refs[n + no:])
        comm["run"]("wait", refs[:n], refs[n:n + no], *refs[n + no:])

    return pl.pallas_call(
        body, name=name, in_specs=[ANY_SPEC] * n, out_specs=[ANY_SPEC] * no, out_shape=comm["outs"],
        scratch_shapes=comm["sems"])(*comm["ins"])


def with_comm(comm, body, grid, in_specs, out_specs, out_shape, scratch_shapes):
    if comm is None:
        return body, in_specs, out_specs, out_shape, scratch_shapes, [], len(out_shape)
    n_in, n_out, n_scr = len(in_specs), len(out_shape), len(scratch_shapes)
    ci, co = len(comm["ins"]), len(comm["outs"])

    def wrapped(*refs):
        ins, cins = refs[:n_in], refs[n_in:n_in + ci]
        outs = refs[n_in + ci:n_in + ci + n_out]
        couts = refs[n_in + ci + n_out:n_in + ci + n_out + co]
        rest = refs[n_in + ci + n_out + co:]
        scr, sems = rest[:n_scr], rest[n_scr:]
        first = functools.reduce(jnp.logical_and, [pl.program_id(a) == 0 for a in range(len(grid))])
        last = functools.reduce(jnp.logical_and, [pl.program_id(a) == grid[a] - 1 for a in range(len(grid))])

        @pl.when(first)
        def _():
            comm["run"]("start", cins, couts, *sems)

        body(*ins, *outs, *scr)

        @pl.when(last)
        def _():
            comm["run"]("wait", cins, couts, *sems)

    return (wrapped, list(in_specs) + [ANY_SPEC] * ci, list(out_specs) + [ANY_SPEC] * co,
            list(out_shape) + list(comm["outs"]), list(scratch_shapes) + list(comm["sems"]), list(comm["ins"]), n_out)


WEIGHTS = ['meta_tokens', 'ln_mix', 'ln_mlp', 'ssd_w_in', 'ssd_conv_w', 'ssd_conv_b', 'ssd_dt_bias', 'ssd_a_log',
           'ssd_d', 'ssd_norm', 'ssd_w_out', 'mla_w_in', 'mla_q_a_norm', 'mla_w_q_b', 'mla_kv_a_norm', 'mla_w_kv_b',
           'mla_q_norm', 'mla_k_norm', 'mla_w_out', 'mlp_w_up', 'mlp_w_down']
BIG = ['ssd_w_in', 'ssd_w_out', 'mla_w_in', 'mla_w_q_b', 'mla_w_kv_b', 'mla_w_out', 'mlp_w_up', 'mlp_w_down']
SMALL_SHARDED = ['meta_tokens', 'ssd_conv_w', 'mla_q_a_norm', 'mla_kv_a_norm']
SMALL_REPL = ['ln_mix', 'ln_mlp', 'ssd_conv_b', 'ssd_dt_bias', 'ssd_a_log', 'ssd_d', 'ssd_norm', 'mla_q_norm',
              'mla_k_norm']
SMALL = SMALL_REPL + SMALL_SHARDED
SSD_IN_PAD = 6272
SSD_IN_TN = 896
MLA_IN = MLA_Q_RANK + MLA_KV_RANK + MLA_ROPE


def _pad_last(v, n):
    return jnp.pad(v, [(0, 0)] * (v.ndim - 1) + [(0, n - v.shape[-1])])


SSD_BIG = ['ssd_w_in', 'ssd_w_out']
MLA_BIG = ['mla_w_in', 'mla_w_q_b', 'mla_w_kv_b', 'mla_w_out']
MLP_BIG = ['mlp_w_up', 'mlp_w_down']


def _mix_big(i):
    return [(n, i // 2) for n in (SSD_BIG if i % 2 == 0 else MLA_BIG)]


def _mlp_big(i):
    return [(n, i) for n in MLP_BIG]


def _mix_weights(i, gw, W, full):
    j = i // 2
    d = W['ln_mix'].shape[-1]
    if i % 2 == 0:
        wi = gw[('ssd_w_in', j)].transpose(1, 0, 2).reshape(d, -1)
        par = jnp.concatenate([_pad_last(W[n][j][None], LANES) for n in ('ssd_dt_bias', 'ssd_a_log', 'ssd_d')]
                              + [jnp.zeros((SUBLANES - 3, LANES), F32)])
        return dict(w_in=_pad_last(wi, SSD_IN_PAD), conv_w=full['ssd_conv_w'][j], conv_b=W['ssd_conv_b'][j][None],
                    par=par, norm=W['ssd_norm'][j][None], w_out=gw[('ssd_w_out', j)].reshape(SSD_D_INNER, d))
    wi = gw[('mla_w_in', j)].reshape(d, MLA_IN)
    kpe = jnp.pad(wi[:, MLA_Q_RANK + MLA_KV_RANK:], ((0, 0), (MLA_NOPE, LANES - MLA_QK)))
    wq = gw[('mla_w_q_b', j)].transpose(1, 0, 2).reshape(MLA_Q_RANK, MLA_HEADS, MLA_QK)
    wkv = gw[('mla_w_kv_b', j)].transpose(1, 0, 2).reshape(MLA_KV_RANK, MLA_HEADS, MLA_NOPE + MLA_V)
    return dict(
        w_in=jnp.concatenate([wi[:, :MLA_Q_RANK], kpe, wi[:, MLA_Q_RANK:MLA_Q_RANK + MLA_KV_RANK]], axis=1),
        w_q=_pad_last(wq, LANES).reshape(MLA_Q_RANK, QW),
        w_kv=jnp.concatenate([_pad_last(wkv[..., :MLA_NOPE], LANES).reshape(MLA_KV_RANK, QW),
                              wkv[..., MLA_NOPE:].reshape(MLA_KV_RANK, VW)], axis=1),
        w_out=gw[('mla_w_out', j)].reshape(VW, d), q_a=full['mla_q_a_norm'][j][None],
        kv_a=full['mla_kv_a_norm'][j][None],
        q_norm=_pad_last(W['mla_q_norm'][j][None], LANES), k_norm=_pad_last(W['mla_k_norm'][j][None], LANES))


def _step(x, target, W, M, V):
    d = x.shape[-1]
    me = _dev_index(_me())
    depth = W['ln_mix'].shape[0]

    def gather_keys(i):
        return _mlp_big(i) + (_mix_big(i + 1) if i + 1 < depth else [])

    def gather_items(keys):
        return [(W[n][l].astype(BF16), "all") for n, l in keys]

    small_pack, small_meta = _pack([W[n] for n in SMALL_SHARDED])
    got = comm_call("gather_0", make_comm(gather_items(_mix_big(0)) + [(small_pack, "all")]))
    per_dev = [_unpack(got[-1][s], small_meta) for s in range(N_DEV)]
    full = {n: jnp.concatenate([per_dev[s][i] for s in range(N_DEV)], axis=-1) for i, n in enumerate(SMALL_SHARDED)}
    gw = dict(zip(_mix_big(0), got))

    h = jnp.concatenate([jnp.zeros((PAD, d), F32), full['meta_tokens'], x], axis=0)
    rows = h.shape[0]
    tabs = rope_tables(rows)
    saved, weights = [], []
    for i in range(depth):
        comm = make_comm(gather_items(gather_keys(i)))
        mix = _mix_weights(i, gw, W, full)
        if i % 2 == 0:
            h, s_mix, got = ssd_layer_fwd(f"{i}", h, W['ln_mix'][i][None], mix, comm)
        else:
            h, s_mix, got = mla_layer_fwd(f"{i}", h, W['ln_mix'][i][None], mix, tabs, comm)
        gw.update(zip(gather_keys(i), got))
        up, down = gw[('mlp_w_up', i)], gw[('mlp_w_down', i)].reshape(-1, d)
        h, s_mlp = mlp_fwd(f"{i}", h, W['ln_mlp'][i][None], up, down)
        saved.append((s_mix, s_mlp))
        weights.append((mix, up, down))
    loss_part, dh = loss_head(h, target)
    loss = lax.psum(jnp.sum(loss_part), ("x", "y", "c"))

    recv = {}
    pending = []
    small = {n: [None] * W[n].shape[0] for n in SMALL if n != 'meta_tokens'}
    for i in reversed(range(depth)):
        j = i // 2
        s_mix, s_mlp = saved[i]
        mix, up, down = weights[i]
        dh, dw_up, dw_down, dg = mlp_bwd(f"{i}", dh, s_mlp, W['ln_mlp'][i][None], up, down)
        small['ln_mlp'][i] = dg.sum(0)
        pending += list(zip(_mlp_big(i), [(dw_up, None), (dw_down, down.shape[0] // N_DEV)]))
        comm = make_comm([it for _, it in pending])
        sends = []
        if i % 2 == 0:
            dh, g, got = ssd_layer_bwd(f"{i}", dh, s_mix, W['ln_mix'][i][None], mix, comm)
            n_in = W['ssd_w_in'].shape[-1]
            sends.append((g['w_in'][:, :N_DEV * n_in].reshape(d, N_DEV, n_in).transpose(1, 0, 2), None))
            sends.append((g['w_out'], SSD_D_INNER // N_DEV))
            small['ssd_conv_w'][j] = g['conv_w'].reshape(SSD_CONV, SUBLANES, -1).sum(1)
            small['ssd_conv_b'][j] = g['conv_b'].sum(0)
            small['ssd_dt_bias'][j] = g['par'][0, :SSD_HEADS]
            small['ssd_a_log'][j] = g['par'][1, :SSD_HEADS]
            small['ssd_d'][j] = g['par'][2, :SSD_HEADS]
            small['ssd_norm'][j] = g['norm'].sum(0)
        else:
            dh, g, got = mla_layer_bwd(f"{i}", dh, s_mix, W['ln_mix'][i][None], mix, tabs, comm)
            gi = g['w_in']
            gi = jnp.concatenate([gi[:, :MLA_Q_RANK], gi[:, MLA_Q_RANK + LANES:],
                                  gi[:, MLA_Q_RANK + MLA_NOPE:MLA_Q_RANK + MLA_QK]], axis=1)
            sends.append((gi, d // N_DEV))
            gq = g['w_q'].reshape(MLA_Q_RANK, MLA_HEADS, LANES)[..., :MLA_QK]
            sends.append((gq.reshape(MLA_Q_RANK, N_DEV, -1).transpose(1, 0, 2), None))
            gkv = jnp.concatenate([g['w_kv'][:, :QW].reshape(MLA_KV_RANK, MLA_HEADS, LANES)[..., :MLA_NOPE],
                                   g['w_kv'][:, QW:].reshape(MLA_KV_RANK, MLA_HEADS, MLA_V)], axis=-1)
            sends.append((gkv.reshape(MLA_KV_RANK, N_DEV, -1).transpose(1, 0, 2), None))
            sends.append((g['w_out'], VW // N_DEV))
            small['mla_q_a_norm'][j] = g['q_a'].sum(0)
            small['mla_kv_a_norm'][j] = g['kv_a'].sum(0)
            small['mla_q_norm'][j] = g['q_norm'].sum(0)[:MLA_QK]
            small['mla_k_norm'][j] = g['k_norm'].sum(0)[:MLA_QK]
        small['ln_mix'][i] = g['ln'].sum(0)
        recv.update({key: a for (key, _), a in zip(pending, got)})
        pending = list(zip(_mix_big(i), sends))
    grad_x = dh[CHUNK:]
    small_full = {n: jnp.stack(v) for n, v in small.items()}
    small_full['meta_tokens'] = dh[PAD:CHUNK]

    gpack, gmeta = _pack([small_full[n] for n in SMALL])
    got = comm_call("exchange_0", make_comm([it for _, it in pending] + [(gpack, "all")]))
    recv.update({key: a for (key, _), a in zip(pending, got)})
    res = {}
    for n in BIG:
        res[n] = reduce_adamw(f"adamw_{n}", [recv[(n, l)] for l in range(W[n].shape[0])], W[n], M[n], V[n])
    gsum = dict(zip(SMALL, _unpack(small_reduce(got[-1]), gmeta)))
    for n in SMALL_SHARDED:
        wl = W[n].shape[-1]
        gsum[n] = lax.dynamic_slice_in_dim(gsum[n], me * wl, wl, axis=gsum[n].ndim - 1)
    wp, wmeta = _pack([W[n] for n in SMALL])
    gp, _ = _pack([gsum[n] for n in SMALL])
    mp, _ = _pack([M[n] for n in SMALL])
    vp, _ = _pack([V[n] for n in SMALL])
    upd = [_unpack(o, wmeta) for o in small_adamw(wp, gp, mp, vp)]
    for a, n in enumerate(SMALL):
        res[n] = [gsum[n], upd[0][a], upd[1][a], upd[2][a]]
    return (loss, grad_x[None]) + tuple(res[n][q] for q in range(4) for n in WEIGHTS)


def kernel(x, meta_tokens, ln_mix, ln_mlp, ssd_w_in, ssd_conv_w, ssd_conv_b, ssd_dt_bias, ssd_a_log, ssd_d, ssd_norm, ssd_w_out, mla_w_in, mla_q_a_norm, mla_w_q_b, mla_kv_a_norm, mla_w_kv_b, mla_q_norm, mla_k_norm, mla_w_out, mlp_w_up, mlp_w_down, loss_target, m_meta_tokens, m_ln_mix, m_ln_mlp, m_ssd_w_in, m_ssd_conv_w, m_ssd_conv_b, m_ssd_dt_bias, m_ssd_a_log, m_ssd_d, m_ssd_norm, m_ssd_w_out, m_mla_w_in, m_mla_q_a_norm, m_mla_w_q_b, m_mla_kv_a_norm, m_mla_w_kv_b, m_mla_q_norm, m_mla_k_norm, m_mla_w_out, m_mlp_w_up, m_mlp_w_down, v_meta_tokens, v_ln_mix, v_ln_mlp, v_ssd_w_in, v_ssd_conv_w, v_ssd_conv_b, v_ssd_dt_bias, v_ssd_a_log, v_ssd_d, v_ssd_norm, v_ssd_w_out, v_mla_w_in, v_mla_q_a_norm, v_mla_w_q_b, v_mla_kv_a_norm, v_mla_w_kv_b, v_mla_q_norm, v_mla_k_norm, v_mla_w_out, v_mlp_w_up, v_mlp_w_down):
    given = dict(locals())
    W = {n: given[n] for n in WEIGHTS}
    M = {n: given["m_" + n] for n in WEIGHTS}
    V = {n: given["v_" + n] for n in WEIGHTS}
    return _step(x[0], loss_target[0], W, M, V)
```

```python
import functools

import jax
import jax.numpy as jnp
import numpy as np
from jax import lax
from jax.experimental import pallas as pl
from jax.experimental.pallas import tpu as pltpu

F32 = jnp.float32
BF16 = jnp.bfloat16

EPS = 1e-6
N_META = 16
CHUNK = 128
PAD = CHUNK - N_META
SSD_HEAD_DIM = 64
SSD_HEADS = 32
SSD_GROUPS = 8
SSD_HPG = 4
SSD_STATE = 128
SSD_D_INNER = 2048
SSD_CONV = 4
MLA_HEADS = 16
MLA_NOPE = 64
MLA_ROPE = 32
MLA_V = 64
MLA_QK = 96
MLA_Q_RANK = 384
MLA_KV_RANK = 256
ROPE_THETA = 10000.0
LANES = 128
SUBLANES = 8
N_DEV = 8
VMEM_LIMIT = 56 * 1024 * 1024

ADAM_LR = 0.001
ADAM_B1 = 0.9
ADAM_B2 = 0.999
ADAM_EPS = 1e-08
ADAM_WD = 0.01
ADAM_STEP = 10

NEG = -1e30


def _row_tile(rows):
    return 384 if (rows % 384 == 0 and rows > 384) else 128


def _big_tile(rows):
    return 1408 if rows % 1408 == 0 else _row_tile(rows)


def _params(n_axes, vmem=VMEM_LIMIT):
    return pltpu.CompilerParams(dimension_semantics=("arbitrary",) * n_axes, vmem_limit_bytes=vmem)


def _dot(a, b, dims):
    return lax.dot_general(a.astype(BF16), b.astype(BF16), (dims, ((), ())), preferred_element_type=F32)


NN = ((1,), (0,))
NT = ((1,), (1,))
TN = ((0,), (0,))


@jax.custom_vjp
def bdot_nn(a, b):
    return _dot(a, b, NN)


@jax.custom_vjp
def bdot_nt(a, b):
    return _dot(a, b, NT)


@jax.custom_vjp
def bdot_tn(a, b):
    return _dot(a, b, TN)


bdot_nn.defvjp(lambda a, b: (_dot(a, b, NN), (a, b)),
               lambda r, g: (_dot(g, r[1], NT), _dot(r[0], g, TN)))
bdot_nt.defvjp(lambda a, b: (_dot(a, b, NT), (a, b)),
               lambda r, g: (_dot(g, r[1], NN), _dot(g, r[0], TN)))
bdot_tn.defvjp(lambda a, b: (_dot(a, b, TN), (a, b)),
               lambda r, g: (_dot(r[1], g, NT), _dot(r[0], g, NN)))


def _rows8(v):
    r, n = v.shape
    return v.reshape(r // SUBLANES, SUBLANES, n).sum(axis=0)


def _row_mask(i, tm):
    return (i * tm + lax.broadcasted_iota(jnp.int32, (tm, 1), 0)) >= PAD


def fused_mm(name, *, rows, k, n, tm, tn, tk=None, a_ops, pro, w, w_block, w_imap, dot, e_ops=(), epi, outs):
    tk = tk or k
    ni, nj, nk = rows // tm, n // tn, k // tk
    assert rows % tm == 0 and n % tn == 0 and k % tk == 0
    assert nk == 1 or nj == 1
    cache = nk == 1 and nj > 1
    na, ne, no = len(a_ops), len(e_ops), len(outs)

    def body(*refs):
        a_refs = refs[:na]
        w_ref = refs[na]
        e_refs = refs[na + 1:na + 1 + ne]
        o_refs = refs[na + 1 + ne:na + 1 + ne + no]
        scr = refs[na + 1 + ne + no:]
        i, j, kk = pl.program_id(0), pl.program_id(1), pl.program_id(2)
        if cache:
            a_scr = scr[0]

            @pl.when(j == 0)
            def _():
                a_scr[...] = pro(a_refs, o_refs, i)

            a = a_scr[...]
        else:
            a = pro(a_refs, o_refs, i)
        part = dot(a, w_ref)
        if nk == 1:
            epi(part, e_refs, o_refs, i, j)
        else:
            acc_ref = scr[0]

            @pl.when(kk == 0)
            def _():
                acc_ref[...] = part

            @pl.when(kk > 0)
            def _():
                acc_ref[...] += part

            @pl.when(kk == nk - 1)
            def _():
                epi(acc_ref[...], e_refs, o_refs, i, j)

    scratch = []
    if cache:
        scratch.append(pltpu.VMEM((tm, k), BF16))
    if nk > 1:
        scratch.append(pltpu.VMEM((tm, tn), F32))
    in_specs = [pl.BlockSpec(b, m) for (_, b, m) in a_ops]
    in_specs.append(pl.BlockSpec(w_block, w_imap))
    in_specs += [pl.BlockSpec(b, m) for (_, b, m) in e_ops]
    return pl.pallas_call(
        body, name=name, grid=(ni, nj, nk),
        in_specs=in_specs,
        out_specs=[pl.BlockSpec(b, m) for (_, _, b, m) in outs],
        out_shape=[jax.ShapeDtypeStruct(s, d) for (s, d, _, _) in outs],
        scratch_shapes=scratch,
        compiler_params=_params(3),
    )(*[a for (a, _, _) in a_ops], w, *[e for (e, _, _) in e_ops])


def _lw(w, block, imap):
    if isinstance(w, tuple):
        arr, layer = w
        return arr, (None,) + block, (lambda i, j, kk: (layer,) + imap(i, j, kk))
    return w, block, imap


def _dot_w(a, w_ref):
    return jnp.dot(a, w_ref[...], preferred_element_type=F32)


def _dot_wt(a, w_ref):
    return lax.dot_general(a, w_ref[...], (NT, ((), ())), preferred_element_type=F32)


def _rms_pro(h, g):
    r = lax.rsqrt(jnp.mean(h * h, axis=-1, keepdims=True) + EPS)
    return h * r * g


def _rms_bwd(dyn, h, g):
    r = lax.rsqrt(jnp.mean(h * h, axis=-1, keepdims=True) + EPS)
    xh = h * r
    t = dyn * g
    dh = r * (t - xh * jnp.mean(t * xh, axis=-1, keepdims=True))
    return dh, _rows8(dyn * xh)


def _acc_out(ref, val, first):
    @pl.when(first)
    def _():
        ref[...] = val

    @pl.when(jnp.logical_not(first))
    def _():
        ref[...] += val


def norm_mm(name, h, g, w, *, tn, k_cols=None, col_block=0, w_stacked=False, out_dtype=F32):
    rows = h.shape[0]
    k = k_cols or h.shape[1]
    wshape = (w[0].shape[1:] if isinstance(w, tuple) else w.shape)
    n = wshape[0] * wshape[2] if w_stacked else wshape[1]
    tm = _big_tile(rows)

    def pro(a_refs, o_refs, i):
        hn = _rms_pro(a_refs[0][...], a_refs[1][...]).astype(BF16)
        o_refs[1][...] = hn
        return hn

    def epi(acc, e_refs, o_refs, i, j):
        o_refs[0][...] = acc.astype(out_dtype)

    if w_stacked:
        w_block, w_imap = (None, k, tn), (lambda i, j, kk: (j, 0, 0))
    else:
        w_block, w_imap = (k, tn), (lambda i, j, kk: (0, j))
    w, w_block, w_imap = _lw(w, w_block, w_imap)
    return fused_mm(
        name, rows=rows, k=k, n=n, tm=tm, tn=tn,
        a_ops=[(h, (tm, k), lambda i, j, kk: (i, col_block)), (g, (1, k), lambda i, j, kk: (0, 0))],
        pro=pro, w=w, w_block=w_block, w_imap=w_imap, dot=_dot_w, epi=epi,
        outs=[((rows, n), out_dtype, (tm, tn), lambda i, j, kk: (i, j)),
              ((rows, k), BF16, (tm, k), lambda i, j, kk: (i, 0))])


def res_mm(name, a_ops, pro, k, w, res, *, tn, save_dtype=None, tm=None, tk=None):
    rows, n = res.shape
    tm = tm or _row_tile(rows)
    assert tk is None or save_dtype is None

    def pro2(a_refs, o_refs, i):
        a = pro(a_refs)
        if save_dtype is not None:
            o_refs[1][...] = a
        return a

    def epi(acc, e_refs, o_refs, i, j):
        o_refs[0][...] = e_refs[0][...] + acc

    outs = [((rows, n), F32, (tm, tn), lambda i, j, kk: (i, j))]
    if save_dtype is not None:
        outs.append(((rows, k), save_dtype, (tm, k), lambda i, j, kk: (i, 0)))
    w, w_block, w_imap = _lw(w, (tk or k, tn), lambda i, j, kk: (kk, j))
    out = fused_mm(
        name, rows=rows, k=k, n=n, tm=tm, tn=tn, tk=tk,
        a_ops=[(a, tuple(tm if x is None else x for x in b), m) for (a, b, m) in a_ops],
        pro=pro2, w=w, w_block=w_block, w_imap=w_imap, dot=_dot_w,
        e_ops=[(res, (tm, tn), lambda i, j, kk: (i, j))], epi=epi, outs=outs)
    return out if save_dtype is not None else out[0]


def wgrad_mm(name, a_ops, pro_a, g_ops, pro_g, *, rows, k1, n, t1, tn, out_shape=None, out_block=None, out_imap=None):
    tt = _row_tile(rows)
    n1, n2, nt = k1 // t1, n // tn, rows // tt
    assert k1 % t1 == 0 and n % tn == 0
    na = len(a_ops)

    def body(*refs):
        a_refs = refs[:na]
        g_refs = refs[na:-2]
        o_ref, acc = refs[-2:]
        t = pl.program_id(2)
        a = pro_a(a_refs).astype(BF16)
        g = pro_g(g_refs).astype(BF16)
        _acc_out(acc, lax.dot_general(a, g, (TN, ((), ())), preferred_element_type=F32), t == 0)

        @pl.when(t == nt - 1)
        def _():
            if len(o_ref.shape) == 3:
                ws = o_ref.shape[2]
                for q in range(o_ref.shape[0]):
                    o_ref[q] = acc[:, q * ws:(q + 1) * ws].astype(BF16)
            else:
                o_ref[...] = acc[...].astype(BF16)

    return pl.pallas_call(
        body, name=name, grid=(n1, n2, nt),
        in_specs=[pl.BlockSpec(b, m) for (_, b, m) in list(a_ops) + list(g_ops)],
        out_specs=pl.BlockSpec(out_block or (t1, tn), out_imap or (lambda a, b, t: (a, b))),
        out_shape=jax.ShapeDtypeStruct(out_shape or (k1, n), BF16),
        scratch_shapes=[pltpu.VMEM((t1, tn), F32)],
        compiler_params=_params(3),
    )(*[a for (a, _, _) in list(a_ops) + list(g_ops)])


def simple_wgrad(name, a, g, *, a_cols=None, a_col_block=0, t1=None, tn=None, **kw):
    rows = a.shape[0]
    k1 = a_cols or a.shape[1]
    n = g.shape[1]
    tt = _row_tile(rows)
    t1 = t1 or min(k1, 512)
    tn = tn or min(n, 1024)
    return wgrad_mm(
        name,
        [(a, (tt, t1), lambda x, y, t: (t, x + a_col_block * (k1 // t1)))], lambda r: r[0][...],
        [(g, (tt, tn), lambda x, y, t: (t, y))], lambda r: r[0][...],
        rows=rows, k1=k1, n=n, t1=t1, tn=tn, **kw)


def rms_bwd_mm(name, dz_ops, pro, k, w, w_block, w_imap, dot, h, g, dh, *, tk=None, h_cols=None, h_col_block=0,
               add_dh=True, tm=None):
    rows = h.shape[0]
    n = h_cols or h.shape[1]
    tm = tm or _big_tile(rows)
    ni = rows // tm
    w, w_block, w_imap = _lw(w, w_block, w_imap)

    def epi(acc, e_refs, o_refs, i, j):
        d, dg = _rms_bwd(acc, e_refs[0][...], e_refs[1][...])
        if add_dh:
            d = d + e_refs[2][...]
        o_refs[0][...] = jnp.where(_row_mask(i, tm), d, 0.0)
        _acc_out(o_refs[1], dg, i == 0)

    e_ops = [(h, (tm, n), lambda i, j, kk: (i, h_col_block)), (g, (1, n), lambda i, j, kk: (0, 0))]
    if add_dh:
        e_ops.append((dh, (tm, n), lambda i, j, kk: (i, 0)))
    return fused_mm(
        name, rows=rows, k=k, n=n, tm=tm, tn=n, tk=tk,
        a_ops=[(a, tuple(tm if x is None else x for x in b), m) for (a, b, m) in dz_ops],
        pro=lambda a_refs, o_refs, i: pro(a_refs), w=w, w_block=w_block, w_imap=w_imap, dot=dot,
        e_ops=e_ops, epi=epi,
        outs=[((rows, n), F32, (tm, n), lambda i, j, kk: (i, 0)),
              ((SUBLANES, n), F32, (SUBLANES, n), lambda i, j, kk: (0, 0))])


def _relu2(u):
    r = jnp.maximum(u.astype(F32), 0.0)
    return r * r


def mlp_fwd(tag, h, g, w_up_st, w_down):
    d_ff = w_down.shape[0]
    u, hn = norm_mm(f"mlp_up_{tag}", h, g, w_up_st, tn=w_up_st.shape[2], w_stacked=True, out_dtype=BF16)
    out = res_mm(f"mlp_down_{tag}", [(u, (None, 512), lambda i, j, kk: (i, kk))],
                 lambda r: _relu2(r[0][...]).astype(BF16), d_ff, w_down, h, tn=h.shape[1],
                 tm=_big_tile(h.shape[0]), tk=512)
    return out, (h, hn, u)


def mlp_bwd(tag, dh, saved, g, w_up_st, w_down):
    h, hn, u = saved
    rows, d = h.shape
    d_ff = w_down.shape[0]
    ts = w_up_st.shape[2]
    tm = _big_tile(rows)
    tt = _row_tile(rows)
    wd, wd_block, wd_imap = _lw(w_down, (512, d), lambda i, j, kk: (j, 0))

    def epi_du(acc, e_refs, o_refs, i, j):
        o_refs[0][...] = (acc * (2.0 * jnp.maximum(e_refs[0][...].astype(F32), 0.0))).astype(BF16)

    du, = fused_mm(
        f"mlp_du_{tag}", rows=rows, k=d, n=d_ff, tm=tm, tn=512,
        a_ops=[(dh, (tm, d), lambda i, j, kk: (i, 0))], pro=lambda a, o, i: a[0][...].astype(BF16),
        w=wd, w_block=wd_block, w_imap=wd_imap, dot=_dot_wt,
        e_ops=[(u, (tm, 512), lambda i, j, kk: (i, j))], epi=epi_du,
        outs=[((rows, d_ff), BF16, (tm, 512), lambda i, j, kk: (i, j))])
    half = d_ff // 2
    dw_down = wgrad_mm(
        f"mlp_dwdown_{tag}",
        [(u, (tt, half), lambda a, b, t: (t, a))], lambda r: _relu2(r[0][...]),
        [(dh, (tt, d), lambda a, b, t: (t, 0))], lambda r: r[0][...],
        rows=rows, k1=d_ff, n=d, t1=half, tn=d)
    dw_up = simple_wgrad(f"mlp_dwup_{tag}", hn, du, t1=d, tn=half, out_shape=(N_DEV, d, ts),
                         out_block=(half // ts, d, ts), out_imap=lambda a, b, t: (b, 0, 0))
    dh_in, dg = rms_bwd_mm(
        f"mlp_dh_{tag}", [(du, (None, ts), lambda i, j, kk: (i, kk))], lambda r: r[0][...], d_ff,
        w_up_st, (None, d, ts), lambda i, j, kk: (kk, 0, 0), _dot_wt, h, g, dh, tk=ts)
    return dh_in, dw_up, dw_down, dg


CONV_HALO = SUBLANES
CONV_W = 2 * SSD_D_INNER
CONV_BLK = 512


def _silu(x):
    return x * jax.nn.sigmoid(x)


def _conv_pre(ext_ref, w, b, tm):
    pre = b
    for k in range(SSD_CONV):
        pre = pre + w[k:k + 1, :] * ext_ref[pl.ds(CONV_HALO - (SSD_CONV - 1) + k, tm), :]
    return pre


def _conv_load(ext, parts, c):
    for cur, halo, lo, hi in parts:
        ext[pl.ds(0, CONV_HALO), lo:hi] = jnp.where(c > 0, halo[...], 0.0)
        ext[pl.ds(CONV_HALO, CHUNK), lo:hi] = cur[...]


def _conv_specs(idx):
    hb = CHUNK // CONV_HALO
    specs = []
    for w, blk in ((SSD_D_INNER, 1), (BC_W, SSD_D_INNER // BC_W + 2), (BC_W, SSD_D_INNER // BC_W + 3)):
        specs.append(pl.BlockSpec((CHUNK, w), lambda s, blk=blk: (idx(s), blk)))
        specs.append(pl.BlockSpec((CONV_HALO, w), lambda s, blk=blk: (jnp.maximum(idx(s) * hb - 1, 0), blk)))
    return specs


CONV_COLS = ((0, SSD_D_INNER), (SSD_D_INNER, SSD_D_INNER + SSD_GROUPS * SSD_STATE),
             (SSD_D_INNER + SSD_GROUPS * SSD_STATE, 2 * SSD_D_INNER))


@functools.partial(jax.custom_vjp, nondiff_argnums=(1,))
def _sub_row(x, h):
    return x[h:h + 1, :]


_sub_row.defvjp(
    lambda x, h: (x[h:h + 1, :], None),
    lambda h, _, g: (jnp.where(lax.broadcasted_iota(jnp.int32, (LANES, 1), 0) == h, g, 0.0),))


def _splitter(axis, size, count):
    def blocks(x):
        return tuple(lax.slice_in_dim(x, q * size, (q + 1) * size, axis=axis) for q in range(count))

    split = jax.custom_vjp(blocks)
    split.defvjp(lambda x: (blocks(x), None), lambda _, gs: (jnp.concatenate(gs, axis=axis),))
    return split


def _split3(x):
    hi = x.astype(BF16)
    r = x - hi.astype(F32)
    mid = r.astype(BF16)
    return hi, mid, (r - mid.astype(F32)).astype(BF16)


def _expand_impl(x, e):
    return sum(jnp.dot(t, e, preferred_element_type=F32) for t in _split3(x))


@jax.custom_vjp
def _expand(x, e):
    return _expand_impl(x, e)


def _expand_bwd(e, g):
    hi, mid, _ = _split3(g)
    dx = sum(lax.dot_general(t, e, (NT, ((), ())), preferred_element_type=F32) for t in (hi, mid))
    return dx, jnp.zeros_like(e)


_expand.defvjp(lambda x, e: (_expand_impl(x, e), e), _expand_bwd)

HEAD_PAIR = 2 * SSD_HEAD_DIM
GROUP_W = SSD_HPG * SSD_HEAD_DIM


def _ssd_chunk(xs, bm, cm, dtraw, prev, par, c, tri, e64, e128):
    li = lax.broadcasted_iota(jnp.int32, (CHUNK, CHUNK), 0)
    si = lax.broadcasted_iota(jnp.int32, (CHUNK, CHUNK), 1)
    causal = li >= si
    first_head = lax.broadcasted_iota(jnp.int32, (1, HEAD_PAIR), 1) < SSD_HEAD_DIM
    dt = jnp.where(_row_mask(c, CHUNK), jax.nn.softplus(dtraw + par[0:1, :]), 0.0)
    a = -jnp.exp(par[1:2, :])
    acs = jnp.dot(tri, dt * a, precision=lax.Precision.HIGHEST, preferred_element_type=F32)
    acs_t = acs.T
    last = acs[CHUNK - 1:CHUNK, :]
    misc = jnp.concatenate([jnp.exp(last), par[2:3, :], jnp.zeros((SUBLANES - 2, LANES), F32)], axis=0)
    wide = _expand(jnp.concatenate([dt, dt * jnp.exp(last - acs), jnp.exp(acs)], axis=0), e64)
    dt_w, dtend_w, start_w = _splitter(0, CHUNK, 3)(wide)
    misc_w = _expand(misc, e64)
    col_w = _splitter(1, CHUNK, SSD_HEADS)(_expand(acs, e128))
    groups = _splitter(1, GROUP_W, SSD_GROUPS)
    xs_g, prev_g, start_g = groups(xs), groups(prev), groups(start_w)
    xdt_p = _splitter(1, HEAD_PAIR, SSD_HEADS // 2)(xs * dt_w)
    xdtend_g = groups(xs * dtend_w)
    last_g, skip_g = groups(misc_w[0:1, :]), groups(misc_w[1:2, :])
    b_g, c_g = _splitter(1, SSD_STATE, SSD_GROUPS)(bm), _splitter(1, SSD_STATE, SSD_GROUPS)(cm)
    ys, news = [], []
    for g in range(SSD_GROUPS):
        cb = bdot_nt(c_g[g], b_g[g])
        st = bdot_nn(b_g[g].T, xdtend_g[g])
        y_off = bdot_nn(c_g[g], prev_g[g]) * start_g[g]
        pairs = []
        for q in range(SSD_HPG // 2):
            xp = xdt_p[g * (SSD_HPG // 2) + q]
            acc = None
            for r in range(2):
                head = SSD_HPG * g + 2 * q + r
                seg = jnp.where(causal, col_w[head] - _sub_row(acs_t, head), 0.0)
                decay = jnp.where(causal, jnp.exp(seg), 0.0)
                t = bdot_nn(cb * decay, jnp.where(first_head if r == 0 else jnp.logical_not(first_head), xp, 0.0))
                acc = t if acc is None else acc + t
            pairs.append(acc)
        ys.append(jnp.concatenate(pairs, axis=1) + y_off + xs_g[g] * skip_g[g])
        news.append(prev_g[g] * last_g[g] + st)
    return jnp.concatenate(ys, axis=1), jnp.concatenate(news, axis=1)


def _expanders():
    e64 = np.zeros((LANES, SSD_D_INNER), np.float32)
    e128 = np.zeros((LANES, SSD_HEADS * CHUNK), np.float32)
    for h in range(SSD_HEADS):
        e64[h, h * SSD_HEAD_DIM:(h + 1) * SSD_HEAD_DIM] = 1.0
        e128[h, h * CHUNK:(h + 1) * CHUNK] = 1.0
    return jnp.asarray(e64, BF16), jnp.asarray(e128, BF16)


def _tri():
    return jnp.asarray(np.tril(np.ones((CHUNK, CHUNK), np.float32)))


BC_W = SSD_GROUPS * SSD_STATE


def _conv_act(ext, refs, w, b, c):
    _conv_load(ext, [(refs[2 * p], refs[2 * p + 1]) + CONV_COLS[p] for p in range(3)], c)
    pre = _conv_pre(ext, w, b, CHUNK)
    return pre, jnp.where(_row_mask(c, CHUNK), _silu(pre), 0.0)


def ssd_fwd(name, zx, dt_block, par, conv_w, conv_b, comm=None):
    rows = zx.shape[0]
    nc = rows // CHUNK

    def body(*refs):
        xbc_refs, (dt_ref, par_ref, tri_ref, e64_ref, e128_ref, cw_ref, cb_ref) = refs[:6], refs[6:13]
        y_ref, st_ref, state, ext = refs[13:]
        c = pl.program_id(0)

        @pl.when(c == 0)
        def _():
            state[...] = jnp.zeros((SSD_STATE, SSD_D_INNER), F32)

        prev = state[...]
        st_ref[...] = prev
        _, act = _conv_act(ext, xbc_refs, cw_ref[...], cb_ref[...], c)
        (x0, x1), (b0, b1), (c0, c1) = CONV_COLS
        y, new = _ssd_chunk(act[:, x0:x1], act[:, b0:b1], act[:, c0:c1], dt_ref[...], prev, par_ref[...], c,
                            tri_ref[...], e64_ref[...], e128_ref[...])
        y_ref[...] = y
        state[...] = new

    const = lambda a: pl.BlockSpec(a.shape, lambda c: (0,) * a.ndim)
    consts = (par, _tri()) + _expanders() + (conv_w, conv_b)
    return carrier_call(
        name, body, (nc,),
        _conv_specs(lambda c: c) + [pl.BlockSpec((CHUNK, LANES), lambda c: (c, dt_block))] + [const(a) for a in consts],
        [pl.BlockSpec((CHUNK, SSD_D_INNER), lambda c: (c, 0)),
         pl.BlockSpec((None, SSD_STATE, SSD_D_INNER), lambda c: (c, 0, 0))],
        [jax.ShapeDtypeStruct((rows, SSD_D_INNER), F32),
         jax.ShapeDtypeStruct((nc, SSD_STATE, SSD_D_INNER), F32)],
        [pltpu.VMEM((SSD_STATE, SSD_D_INNER), F32), pltpu.VMEM((CHUNK + CONV_HALO, CONV_W), F32)],
        (zx,) * 7 + consts, comm)


def carrier_call(name, body, grid, in_specs, out_specs, out_shape, scratch_shapes, args, comm):
    body, in_specs, out_specs, out_shape, scratch_shapes, extra, n_own = with_comm(
        comm, body, grid, in_specs, out_specs, out_shape, scratch_shapes)
    res = pl.pallas_call(
        body, name=name, grid=grid, in_specs=in_specs, out_specs=out_specs, out_shape=out_shape,
        scratch_shapes=scratch_shapes, compiler_params=_params(len(grid)))(*args, *extra)
    return res[:n_own], res[n_own:]


def ssd_bwd(name, dy, zx, dt_block, states, par, conv_w, conv_b, comm=None):
    rows = zx.shape[0]
    nc = rows // CHUNK

    def body(*refs):
        dy_ref, xbc_refs = refs[0], refs[1:7]
        dt_ref, st_ref, par_ref, tri_ref, e64_ref, e128_ref, cw_ref, cb_ref = refs[7:15]
        du_ref, ddt_ref, dpar_ref, dcw_ref, dcb_ref, dstate, ext, dext = refs[15:]
        s = pl.program_id(0)
        c = nc - 1 - s

        @pl.when(s == 0)
        def _():
            dstate[...] = jnp.zeros((SSD_STATE, SSD_D_INNER), F32)
            dext[pl.ds(CHUNK, CONV_HALO), :] = jnp.zeros((CONV_HALO, CONV_W), F32)

        _, act = _conv_act(ext, xbc_refs, cw_ref[...], cb_ref[...], c)
        (x0, x1), (b0, b1), (c0, c1) = CONV_COLS

        def f(xs, bm, cm, dtraw, prev, par_v):
            return _ssd_chunk(xs, bm, cm, dtraw, prev, par_v, c, tri_ref[...], e64_ref[...], e128_ref[...])

        _, vjp = jax.vjp(f, act[:, x0:x1], act[:, b0:b1], act[:, c0:c1], dt_ref[...], st_ref[...], par_ref[...])
        dxs, dbm, dcm, ddt, dprev, dpar = vjp((dy_ref[...], dstate[...]))
        ddt_ref[...] = ddt.astype(BF16)
        dstate[...] = dprev
        _acc_out(dpar_ref, dpar, s == 0)

        dacts = [dxs[:, q * CONV_BLK:(q + 1) * CONV_BLK] for q in range((x1 - x0) // CONV_BLK)]
        dacts += [dbm[:, q * CONV_BLK:(q + 1) * CONV_BLK] for q in range((b1 - b0) // CONV_BLK)]
        dacts += [dcm[:, q * CONV_BLK:(q + 1) * CONV_BLK] for q in range((c1 - c0) // CONV_BLK)]
        for q, dact in enumerate(dacts):
            cols = slice(q * CONV_BLK, (q + 1) * CONV_BLK)
            w = cw_ref[:, cols]
            taps = [ext[pl.ds(CONV_HALO - (SSD_CONV - 1) + k, CHUNK), cols] for k in range(SSD_CONV)]
            pre = cb_ref[:, cols]
            for k in range(SSD_CONV):
                pre = pre + w[k:k + 1, :] * taps[k]
            sg = jax.nn.sigmoid(pre)
            dpre = jnp.where(_row_mask(c, CHUNK), dact * (sg * (1.0 + pre * (1.0 - sg))), 0.0)
            dext[pl.ds(0, CHUNK), cols] = dpre
            du = jnp.zeros((CHUNK, CONV_BLK), F32)
            for k in range(SSD_CONV):
                du = du + w[k:k + 1, :] * dext[pl.ds(SSD_CONV - 1 - k, CHUNK), cols]
            du_ref[:, cols] = du.astype(BF16)
            _acc_out(dcb_ref.at[:, cols], _rows8(dpre), s == 0)
            for k in range(SSD_CONV):
                _acc_out(dcw_ref.at[pl.ds(k * SUBLANES, SUBLANES), cols], _rows8(dpre * taps[k]), s == 0)
            dext[pl.ds(CHUNK, CONV_HALO), cols] = dpre[0:CONV_HALO, :]

    rev = lambda w, b: pl.BlockSpec((CHUNK, w), lambda s: (nc - 1 - s, b))
    const = lambda a: pl.BlockSpec(a.shape, lambda s: (0,) * a.ndim)
    consts = (par, _tri()) + _expanders() + (conv_w, conv_b)
    return carrier_call(
        name, body, (nc,),
        [rev(SSD_D_INNER, 0)] + _conv_specs(lambda s: nc - 1 - s)
        + [rev(LANES, dt_block), pl.BlockSpec((None, SSD_STATE, SSD_D_INNER), lambda s: (nc - 1 - s, 0, 0))]
        + [const(a) for a in consts],
        [rev(CONV_W, 0), rev(LANES, 0), pl.BlockSpec((SUBLANES, LANES), lambda s: (0, 0)),
         pl.BlockSpec((SSD_CONV * SUBLANES, CONV_W), lambda s: (0, 0)),
         pl.BlockSpec((SUBLANES, CONV_W), lambda s: (0, 0))],
        [jax.ShapeDtypeStruct((rows, CONV_W), BF16),
         jax.ShapeDtypeStruct((rows, LANES), BF16),
         jax.ShapeDtypeStruct((SUBLANES, LANES), F32),
         jax.ShapeDtypeStruct((SSD_CONV * SUBLANES, CONV_W), F32),
         jax.ShapeDtypeStruct((SUBLANES, CONV_W), F32)],
        [pltpu.VMEM((SSD_STATE, SSD_D_INNER), F32), pltpu.VMEM((CHUNK + CONV_HALO, CONV_W), F32),
         pltpu.VMEM((CHUNK + CONV_HALO, CONV_W), F32)],
        (dy,) + (zx,) * 7 + (states,) + consts, comm)


GN_W = SSD_D_INNER // SSD_GROUPS


def _gated_norm(y, z, ng):
    g = y * _silu(z)
    outs = []
    for q in range(SSD_GROUPS):
        gs = g[:, q * GN_W:(q + 1) * GN_W]
        outs.append(gs * lax.rsqrt(jnp.mean(gs * gs, axis=-1, keepdims=True) + EPS))
    return jnp.concatenate(outs, axis=1) * ng


def ssd_layer_fwd(tag, h, ln_g, w, comm=None):
    zx, hn = norm_mm(f"ssd_in_{tag}", h, ln_g, w["w_in"], tn=896)
    dt_block = 3 * SSD_D_INNER // LANES
    (y, states), cres = ssd_fwd(f"ssd_scan_{tag}", zx, dt_block, w["par"], w["conv_w"], w["conv_b"], comm)
    out, gn = res_mm(
        f"ssd_out_{tag}",
        [(y, (None, SSD_D_INNER), lambda i, j, kk: (i, 0)), (zx, (None, SSD_D_INNER), lambda i, j, kk: (i, 0)),
         (w["norm"], (1, SSD_D_INNER), lambda i, j, kk: (0, 0))],
        lambda r: _gated_norm(r[0][...], r[1][...], r[2][...]).astype(BF16),
        SSD_D_INNER, w["w_out"], h, tn=512, save_dtype=BF16)
    return out, (h, hn, zx, y, states, gn), cres


def ssd_layer_bwd(tag, dh, saved, ln_g, w, comm=None):
    h, hn, zx, y, states, gn = saved
    rows, d = h.shape
    tm = _row_tile(rows)
    dt_block = 3 * SSD_D_INNER // LANES
    dw_out = simple_wgrad(f"ssd_dwout_{tag}", gn, dh, t1=SSD_D_INNER, tn=d)

    def epi_gate(acc, e_refs, o_refs, i, j):
        _, vjp = jax.vjp(_gated_norm, e_refs[0][...], e_refs[1][...], e_refs[2][...])
        dy, dz, dng = vjp(acc)
        o_refs[0][...] = dy
        o_refs[1][...] = dz.astype(BF16)
        row0 = lax.broadcasted_iota(jnp.int32, (SUBLANES, 1), 0) == 0
        _acc_out(o_refs[2], jnp.where(row0, dng, 0.0), i == 0)

    wo, wo_block, wo_imap = _lw(w["w_out"], (SSD_D_INNER, d), lambda i, j, kk: (0, 0))
    dy, dz, dnorm = fused_mm(
        f"ssd_dgate_{tag}", rows=rows, k=d, n=SSD_D_INNER, tm=tm, tn=SSD_D_INNER,
        a_ops=[(dh, (tm, d), lambda i, j, kk: (i, 0))], pro=lambda a, o, i: a[0][...].astype(BF16),
        w=wo, w_block=wo_block, w_imap=wo_imap, dot=_dot_wt,
        e_ops=[(y, (tm, SSD_D_INNER), lambda i, j, kk: (i, 0)), (zx, (tm, SSD_D_INNER), lambda i, j, kk: (i, 0)),
               (w["norm"], (1, SSD_D_INNER), lambda i, j, kk: (0, 0))],
        epi=epi_gate,
        outs=[((rows, SSD_D_INNER), F32, (tm, SSD_D_INNER), lambda i, j, kk: (i, 0)),
              ((rows, SSD_D_INNER), BF16, (tm, SSD_D_INNER), lambda i, j, kk: (i, 0)),
              ((SUBLANES, SSD_D_INNER), F32, (SUBLANES, SSD_D_INNER), lambda i, j, kk: (0, 0))])
    (dxbc, ddt, dpar, dcw, dcb), cres = ssd_bwd(f"ssd_dscan_{tag}", dy, zx, dt_block, states, w["par"],
                                                w["conv_w"], w["conv_b"], comm)
    dzx = jnp.concatenate([dz, dxbc, ddt], axis=1)
    k = dzx.shape[1]
    dw_in = simple_wgrad(f"ssd_dwin_{tag}", hn, dzx, t1=d, tn=896)
    dh_in, dln = rms_bwd_mm(
        f"ssd_dh_{tag}", [(dzx, (None, 896), lambda i, j, kk: (i, kk))], lambda r: r[0][...], k,
        w["w_in"], (d, 896), lambda i, j, kk: (0, kk), _dot_wt, h, ln_g, dh, tk=896,
        tm=704 if rows % 704 == 0 else None)
    grads = dict(w_in=dw_in, w_out=dw_out, conv_w=dcw, conv_b=dcb, par=dpar, norm=dnorm, ln=dln)
    return dh_in, grads, cres


HP = 2 * LANES
VP = 2 * MLA_V
N_PAIRS = MLA_HEADS // 2
ATT_SCALE = MLA_QK ** -0.5
LOG2E = float(np.log2(np.e))
LN2 = float(np.log(2.0))
ROT = MLA_ROPE // 2


def rope_tables(rows):
    inv = 1.0 / (ROPE_THETA ** (jnp.arange(0, MLA_ROPE, 2, dtype=F32) / MLA_ROPE))
    pos = jnp.arange(rows, dtype=F32) - PAD
    ang = pos[:, None] * inv[None, :]
    cos, sin = jnp.cos(ang), jnp.sin(ang)
    one = jnp.ones((rows, MLA_NOPE), F32)
    zero = jnp.zeros((rows, LANES - MLA_QK), F32)
    zn = jnp.zeros((rows, MLA_NOPE), F32)
    zr = jnp.zeros((rows, ROT), F32)
    cosf = jnp.concatenate([one, cos, cos, zero], axis=1)
    sina = jnp.concatenate([zn, -sin, zr, zero], axis=1)
    sinb = jnp.concatenate([zn, zr, sin, zero], axis=1)
    return cosf, sina, sinb


def _qk_norm_rope(x, g, cosf, sina, sinb):
    r = lax.rsqrt(jnp.sum(x * x, axis=-1, keepdims=True) * (1.0 / MLA_QK) + EPS)
    xn = x * r * g
    return xn * cosf + pltpu.roll(xn, LANES - ROT, 1) * sina + pltpu.roll(xn, ROT, 1) * sinb


def _qk_norm_rope_bwd(dout, x, g, cosf, sina, sinb):
    dxn = dout * cosf + pltpu.roll(dout * sina, ROT, 1) + pltpu.roll(dout * sinb, LANES - ROT, 1)
    r = lax.rsqrt(jnp.sum(x * x, axis=-1, keepdims=True) * (1.0 / MLA_QK) + EPS)
    xh = x * r
    t = dxn * g
    dx = r * (t - xh * (jnp.sum(t * xh, axis=-1, keepdims=True) * (1.0 / MLA_QK)))
    return dx, _rows8(dxn * xh)


def _rope_lanes():
    lane = lax.broadcasted_iota(jnp.int32, (1, LANES), 1)
    return jnp.logical_and(lane >= MLA_NOPE, lane < MLA_QK)


QW = MLA_HEADS * LANES
VW = MLA_HEADS * MLA_V


def qk_prep(name, qraw, kvraw, lat, kpe_block, qg, kg, tabs):
    rows = qraw.shape[0]
    tm = _row_tile(rows)

    def body(q_ref, k0_ref, k1_ref, v_ref, pe_ref, qg_ref, kg_ref, c_ref, sa_ref, sb_ref,
             qo_ref, ko_ref, kt_ref, vo_ref, vt_ref):
        tab = (c_ref[...], sa_ref[...], sb_ref[...])
        pe = pe_ref[...]
        for hd in range(MLA_HEADS):
            sl = slice(hd * LANES, (hd + 1) * LANES)
            qo_ref[:, sl] = _qk_norm_rope(q_ref[:, sl].astype(F32), qg_ref[...], *tab).astype(BF16)
            kr = k0_ref if hd < MLA_HEADS // 2 else k1_ref
            ks = slice((hd % (MLA_HEADS // 2)) * LANES, (hd % (MLA_HEADS // 2) + 1) * LANES)
            kk = _qk_norm_rope(kr[:, ks].astype(F32) + pe, kg_ref[...], *tab)
            ko_ref[:, sl] = kk.astype(BF16)
            kt_ref[sl, :] = kk.T.astype(BF16)
        vo_ref[...] = v_ref[...].astype(BF16)
        for c in range(VW // LANES):
            sl = slice(c * LANES, (c + 1) * LANES)
            vt_ref[sl, :] = v_ref[:, sl].astype(F32).T.astype(BF16)

    row = lambda w, b: pl.BlockSpec((tm, w), lambda i: (i, b))
    col = lambda w: pl.BlockSpec((w, tm), lambda i: (0, i))
    one = pl.BlockSpec((1, LANES), lambda i: (0, 0))
    return pl.pallas_call(
        body, name=name, grid=(rows // tm,),
        in_specs=[row(QW, 0), row(VW, 0), row(VW, 1), row(VW, 2), row(LANES, kpe_block), one, one,
                  row(LANES, 0), row(LANES, 0), row(LANES, 0)],
        out_specs=[row(QW, 0), row(QW, 0), col(QW), row(VW, 0), col(VW)],
        out_shape=[jax.ShapeDtypeStruct((rows, QW), BF16), jax.ShapeDtypeStruct((rows, QW), BF16),
                   jax.ShapeDtypeStruct((QW, rows), BF16), jax.ShapeDtypeStruct((rows, VW), BF16),
                   jax.ShapeDtypeStruct((VW, rows), BF16)],
        compiler_params=_params(1),
    )(qraw, kvraw, kvraw, kvraw, lat, qg, kg, *tabs)


def qk_prep_bwd(name, dq_t, dk, dv, qraw, kvraw, lat, kpe_block, qg, kg, tabs):
    rows = qraw.shape[0]
    tm = _row_tile(rows)

    def body(dq_ref, dk_ref, dv_ref, q_ref, k0_ref, k1_ref, pe_ref, qg_ref, kg_ref, c_ref, sa_ref, sb_ref,
             dqo_ref, dkvo_ref, dpe_ref, dqg_ref, dkg_ref):
        i = pl.program_id(0)
        tab = (c_ref[...], sa_ref[...], sb_ref[...])
        pe = pe_ref[...]
        dpe = jnp.zeros((tm, LANES), F32)
        dqg = jnp.zeros((SUBLANES, LANES), F32)
        dkg = jnp.zeros((SUBLANES, LANES), F32)
        for hd in range(MLA_HEADS):
            sl = slice(hd * LANES, (hd + 1) * LANES)
            dx, dg = _qk_norm_rope_bwd(dq_ref[sl, :].T, q_ref[:, sl].astype(F32), qg_ref[...], *tab)
            dqo_ref[:, sl] = dx.astype(BF16)
            dqg = dqg + dg
            kr = k0_ref if hd < MLA_HEADS // 2 else k1_ref
            ks = slice((hd % (MLA_HEADS // 2)) * LANES, (hd % (MLA_HEADS // 2) + 1) * LANES)
            dx, dg = _qk_norm_rope_bwd(dk_ref[:, sl], kr[:, ks].astype(F32) + pe, kg_ref[...], *tab)
            dkvo_ref[:, sl] = dx.astype(BF16)
            dpe = dpe + dx
            dkg = dkg + dg
        dkvo_ref[:, QW:QW + VW] = dv_ref[...].astype(BF16)
        dpe_ref[...] = jnp.where(_rope_lanes(), dpe, 0.0)
        _acc_out(dqg_ref, dqg, i == 0)
        _acc_out(dkg_ref, dkg, i == 0)

    row = lambda w, b: pl.BlockSpec((tm, w), lambda i: (i, b))
    one = pl.BlockSpec((1, LANES), lambda i: (0, 0))
    acc = pl.BlockSpec((SUBLANES, LANES), lambda i: (0, 0))
    return pl.pallas_call(
        body, name=name, grid=(rows // tm,),
        in_specs=[pl.BlockSpec((QW, tm), lambda i: (0, i)), row(QW, 0), row(VW, 0), row(QW, 0), row(VW, 0), row(VW, 1),
                  row(LANES, kpe_block), one, one, row(LANES, 0), row(LANES, 0), row(LANES, 0)],
        out_specs=[row(QW, 0), row(QW + VW, 0), row(LANES, 0), acc, acc],
        out_shape=[jax.ShapeDtypeStruct((rows, QW), BF16), jax.ShapeDtypeStruct((rows, QW + VW), BF16),
                   jax.ShapeDtypeStruct((rows, LANES), F32),
                   jax.ShapeDtypeStruct((SUBLANES, LANES), F32), jax.ShapeDtypeStruct((SUBLANES, LANES), F32)],
        compiler_params=_params(1),
    )(dq_t, dk, dv, qraw, kvraw, kvraw, lat, qg, kg, *tabs)


def _att_mask_t(qb, kb, bt):
    kpos = kb * bt + lax.broadcasted_iota(jnp.int32, (bt, bt), 0)
    qpos = qb * bt + lax.broadcasted_iota(jnp.int32, (bt, bt), 1)
    return jnp.logical_and(kpos <= qpos, jnp.logical_or(kpos >= PAD, qpos < PAD))


def attn_fwd(name, q, k, vt, comm=None):
    rows = q.shape[0]
    bt = _row_tile(rows)
    nb = rows // bt
    assert bt >= CHUNK

    def body(q_ref, k_ref, vt_ref, o_ref, lse_ref):
        qi = pl.program_id(1)
        lse_ref[...] = jnp.zeros((SUBLANES, bt), F32)

        def scores(kb):
            r0 = pl.multiple_of(kb * bt, LANES)
            return tuple(lax.dot_general(k_ref[pl.ds(r0, bt), hh * LANES:(hh + 1) * LANES],
                                         q_ref[:, hh * LANES:(hh + 1) * LANES], (NT, ((), ())),
                                         preferred_element_type=F32) for hh in range(2))

        def tile(kb, carry, s_pair, masked):
            r0 = pl.multiple_of(kb * bt, LANES)
            new = []
            for hh in range(2):
                m, l, acc = carry[3 * hh:3 * hh + 3]
                vs = slice(hh * MLA_V, (hh + 1) * MLA_V)
                s = s_pair[hh] * (ATT_SCALE * LOG2E)
                if masked:
                    s = jnp.where(_att_mask_t(qi, kb, bt), s, NEG)
                m_new = jnp.maximum(m, jnp.max(s, axis=0, keepdims=True))
                alpha = jnp.exp2(m - m_new)
                p = jnp.exp2(s - m_new)
                l = alpha * l + jnp.sum(p, axis=0, keepdims=True)
                acc = alpha * acc + jnp.dot(vt_ref[vs, pl.ds(r0, bt)], p.astype(BF16), preferred_element_type=F32)
                new += [m_new, l, acc]
            return tuple(new)

        init = (jnp.full((1, bt), NEG, F32), jnp.zeros((1, bt), F32), jnp.zeros((MLA_V, bt), F32)) * 2
        s_next = scores(jnp.minimum(1, qi))
        carry = tile(0, init, scores(0), True)

        def rest(args):
            def mid(kb, state):
                carry, s_cur = state
                s_after = scores(kb + 1)
                return tile(kb, carry, s_cur, False), s_after

            carry, s_last = lax.fori_loop(1, qi, mid, args)
            return tile(qi, carry, s_last, True)

        carry = lax.cond(qi > 0, rest, lambda args: args[0], (carry, s_next))
        for hh in range(2):
            m, l, acc = carry[3 * hh:3 * hh + 3]
            o_ref[hh * MLA_V:(hh + 1) * MLA_V, :] = acc / l
            lse_ref[hh:hh + 1, :] = m * LN2 + jnp.log(l)

    return carrier_call(
        name, body, (N_PAIRS, nb),
        [pl.BlockSpec((bt, HP), lambda p, i: (i, p)),
         pl.BlockSpec((rows, HP), lambda p, i: (0, p)),
         pl.BlockSpec((VP, rows), lambda p, i: (p, 0))],
        [pl.BlockSpec((VP, bt), lambda p, i: (p, i)),
         pl.BlockSpec((None, SUBLANES, bt), lambda p, i: (p, 0, i))],
        [jax.ShapeDtypeStruct((VW, rows), F32), jax.ShapeDtypeStruct((N_PAIRS, SUBLANES, rows), F32)],
        [], (q, k, vt), comm)


def attn_bwd(name, q, k, kt, v, do_t, lse, delta, comm=None):
    rows = q.shape[0]
    bt = _row_tile(rows)
    nb = rows // bt

    def body(q_ref, k_ref, kt_ref, v_ref, do_ref, lse_ref, dl_ref, dq_ref, dk_ref, dv_ref, dk_scr, dv_scr):
        ki = pl.program_id(1)

        @pl.when(ki == 0)
        def _():
            dq_ref[...] = jnp.zeros((HP, rows), F32)

        dk_scr[...] = jnp.zeros((bt, HP), F32)
        dv_scr[...] = jnp.zeros((bt, VP), F32)

        def tile(qb, masked):
            c0 = pl.multiple_of(qb * bt, LANES)
            for hh in range(2):
                qs = slice(hh * LANES, (hh + 1) * LANES)
                vs = slice(hh * MLA_V, (hh + 1) * MLA_V)
                qv = q_ref[pl.ds(c0, bt), qs]
                dov = do_ref[vs, pl.ds(c0, bt)]
                lse = lse_ref[hh:hh + 1, pl.ds(c0, bt)]
                dl = dl_ref[hh:hh + 1, pl.ds(c0, bt)]
                s = lax.dot_general(k_ref[:, qs], qv, (NT, ((), ())), preferred_element_type=F32) * ATT_SCALE
                p = jnp.exp(s - lse)
                if masked:
                    p = jnp.where(_att_mask_t(qb, ki, bt), p, 0.0)
                dp = jnp.dot(v_ref[:, vs], dov, preferred_element_type=F32)
                ds = (p * (dp - dl) * ATT_SCALE).astype(BF16)
                dv_scr[:, vs] += lax.dot_general(p.astype(BF16), dov, (NT, ((), ())), preferred_element_type=F32)
                dk_scr[:, qs] += jnp.dot(ds, qv, preferred_element_type=F32)
                dq_ref[qs, pl.ds(c0, bt)] += jnp.dot(kt_ref[qs, :], ds, preferred_element_type=F32)

        @pl.when(ki == 0)
        def _():
            def every(qb, carry):
                tile(qb, True)
                return carry

            lax.fori_loop(0, nb, every, 0)

        @pl.when(ki > 0)
        def _():
            tile(ki, True)

            def later(qb, carry):
                tile(qb, False)
                return carry

            lax.fori_loop(ki + 1, nb, later, 0)

        dk_ref[...] = dk_scr[...]
        dv_ref[...] = dv_scr[...]

    stat = pl.BlockSpec((None, SUBLANES, rows), lambda p, i: (p, 0, 0))
    return carrier_call(
        name, body, (N_PAIRS, nb),
        [pl.BlockSpec((rows, HP), lambda p, i: (0, p)),
         pl.BlockSpec((bt, HP), lambda p, i: (i, p)),
         pl.BlockSpec((HP, bt), lambda p, i: (p, i)),
         pl.BlockSpec((bt, VP), lambda p, i: (i, p)),
         pl.BlockSpec((VP, rows), lambda p, i: (p, 0)),
         stat, stat],
        [pl.BlockSpec((HP, rows), lambda p, i: (p, 0)),
         pl.BlockSpec((bt, HP), lambda p, i: (i, p)),
         pl.BlockSpec((bt, VP), lambda p, i: (i, p))],
        [jax.ShapeDtypeStruct((QW, rows), F32), jax.ShapeDtypeStruct((rows, QW), F32),
         jax.ShapeDtypeStruct((rows, VW), F32)],
        [pltpu.VMEM((bt, HP), F32), pltpu.VMEM((bt, VP), F32)],
        (q, k, kt, v, do_t, lse, delta), comm)


def _dot_cast_w(a, w_ref):
    return jnp.dot(a, w_ref[...].astype(BF16), preferred_element_type=F32)


def _dot_cast_wt(a, w_ref):
    return lax.dot_general(a, w_ref[...].astype(BF16), (NT, ((), ())), preferred_element_type=F32)


LAT_W = 768
KPE_BLOCK = MLA_Q_RANK // LANES
KV_BLOCK = (MLA_Q_RANK + LANES) // MLA_KV_RANK


def mla_layer_fwd(tag, h, ln_g, w, tabs, comm=None):
    lat, hn = norm_mm(f"mla_in_{tag}", h, ln_g, w["w_in"], tn=LAT_W)
    qraw, qn = norm_mm(f"mla_q_{tag}", lat, w["q_a"], w["w_q"], tn=512, k_cols=MLA_Q_RANK, col_block=0,
                       out_dtype=BF16)
    kvraw, kvn = norm_mm(f"mla_kv_{tag}", lat, w["kv_a"], w["w_kv"], tn=512, k_cols=MLA_KV_RANK, col_block=KV_BLOCK,
                         out_dtype=BF16)
    q, k, kt, v, vt = qk_prep(f"mla_prep_{tag}", qraw, kvraw, lat, KPE_BLOCK, w["q_norm"], w["k_norm"], tabs)
    (o_t, lse), cres = attn_fwd(f"mla_attn_{tag}", q, k, vt, comm)
    out = res_mm(f"mla_out_{tag}", [(o_t, (VW, None), lambda i, j, kk: (0, i))],
                 lambda r: r[0][...].T.astype(BF16), VW, w["w_out"], h, tn=512, tm=_big_tile(h.shape[0]))
    return out, (h, hn, lat, qn, kvn, qraw, kvraw, q, k, kt, v, o_t, lse), cres


def mla_layer_bwd(tag, dh, saved, ln_g, w, tabs, comm=None):
    h, hn, lat, qn, kvn, qraw, kvraw, q, k, kt, v, o_t, lse = saved
    rows, d = h.shape
    tm = _row_tile(rows)

    def epi_set(acc, e_refs, o_refs, i, j):
        o_refs[0][...] = acc.astype(BF16)

    dw_out, = fused_mm(
        f"mla_dwout_{tag}", rows=VW, k=rows, n=d, tm=512, tn=d, tk=tm,
        a_ops=[(o_t, (512, tm), lambda i, j, kk: (i, kk))], pro=lambda a, o_, i: a[0][...].astype(BF16),
        w=dh, w_block=(tm, d), w_imap=lambda i, j, kk: (kk, 0), dot=_dot_cast_w, epi=epi_set,
        outs=[((VW, d), BF16, (512, d), lambda i, j, kk: (i, 0))])

    def epi_do(acc, e_refs, o_refs, i, j):
        o_refs[0][...] = acc.astype(BF16)
        prod = acc * e_refs[0][...]
        o_refs[1][...] = jnp.zeros((N_PAIRS, SUBLANES, tm), F32)
        for hd in range(MLA_HEADS):
            o_refs[1][hd // 2, hd % 2:hd % 2 + 1, :] = jnp.sum(prod[hd * MLA_V:(hd + 1) * MLA_V, :], axis=0,
                                                               keepdims=True)

    wo, wo_block, wo_imap = _lw(w["w_out"], (VW, d), lambda i, j, kk: (0, 0))
    do_t, delta = fused_mm(
        f"mla_do_{tag}", rows=VW, k=d, n=rows, tm=VW, tn=tm,
        a_ops=[(wo, wo_block, wo_imap)], pro=lambda a, o_, i: a[0][...],
        w=dh, w_block=(tm, d), w_imap=lambda i, j, kk: (j, 0), dot=_dot_cast_wt,
        e_ops=[(o_t, (VW, tm), lambda i, j, kk: (0, j))], epi=epi_do,
        outs=[((VW, rows), BF16, (VW, tm), lambda i, j, kk: (0, j)),
              ((N_PAIRS, SUBLANES, rows), F32, (N_PAIRS, SUBLANES, tm), lambda i, j, kk: (0, 0, j))])
    (dq_t, dk, dv), cres = attn_bwd(f"mla_dattn_{tag}", q, k, kt, v, do_t, lse, delta, comm)
    dqraw, dkvraw, dpe, dqg, dkg = qk_prep_bwd(f"mla_dprep_{tag}", dq_t, dk, dv, qraw, kvraw, lat, KPE_BLOCK,
                                               w["q_norm"], w["k_norm"], tabs)
    dw_q = simple_wgrad(f"mla_dwq_{tag}", qn, dqraw, t1=MLA_Q_RANK, tn=512)
    dqlat, dqa = rms_bwd_mm(
        f"mla_dqlat_{tag}", [(dqraw, (None, QW), lambda i, j, kk: (i, 0))], lambda r: r[0][...], QW,
        w["w_q"], (MLA_Q_RANK, QW), lambda i, j, kk: (0, 0), _dot_wt, lat, w["q_a"], None,
        h_cols=MLA_Q_RANK, h_col_block=0, add_dh=False)
    dw_kv = simple_wgrad(f"mla_dwkv_{tag}", kvn, dkvraw, t1=MLA_KV_RANK, tn=512)
    dkvlat, dkva = rms_bwd_mm(
        f"mla_dkvlat_{tag}", [(dkvraw, (None, QW + VW), lambda i, j, kk: (i, 0))], lambda r: r[0][...], QW + VW,
        w["w_kv"], (MLA_KV_RANK, QW + VW), lambda i, j, kk: (0, 0), _dot_wt, lat, w["kv_a"], None,
        h_cols=MLA_KV_RANK, h_col_block=KV_BLOCK, add_dh=False)
    dlat = jnp.concatenate([dqlat.astype(BF16), dpe.astype(BF16), dkvlat.astype(BF16)], axis=1)
    dw_in = simple_wgrad(f"mla_dwin_{tag}", hn, dlat, t1=512, tn=LAT_W)
    dh_in, dln = rms_bwd_mm(
        f"mla_dh_{tag}", [(dlat, (None, LAT_W), lambda i, j, kk: (i, 0))], lambda r: r[0][...], LAT_W,
        w["w_in"], (d, LAT_W), lambda i, j, kk: (0, 0), _dot_wt, h, ln_g, dh)
    grads = dict(w_in=dw_in, w_q=dw_q, w_kv=dw_kv, w_out=dw_out, q_a=dqa, kv_a=dkva, q_norm=dqg, k_norm=dkg, ln=dln)
    return dh_in, grads, cres


def loss_head(h, target):
    rows, d = h.shape
    nb = rows // CHUNK

    def body(h_ref, t_ref, l_ref, dh_ref):
        i = pl.program_id(0)
        err = jnp.where(i > 0, h_ref[...] - t_ref[...], 0.0)
        dh_ref[...] = err * (1.0 / d)
        _acc_out(l_ref, _rows8(err * err) * (0.5 / d), i == 0)

    return pl.pallas_call(
        body, name="loss_head", grid=(nb,),
        in_specs=[pl.BlockSpec((CHUNK, d), lambda i: (i, 0)),
                  pl.BlockSpec((CHUNK, d), lambda i: (jnp.maximum(i - 1, 0), 0))],
        out_specs=[pl.BlockSpec((SUBLANES, d), lambda i: (0, 0)), pl.BlockSpec((CHUNK, d), lambda i: (i, 0))],
        out_shape=[jax.ShapeDtypeStruct((SUBLANES, d), F32), jax.ShapeDtypeStruct((rows, d), F32)],
        compiler_params=_params(1),
    )(h, target)


def _adamw(w, g, m, v):
    m = ADAM_B1 * m + (1.0 - ADAM_B1) * g
    v = ADAM_B2 * v + (1.0 - ADAM_B2) * jnp.square(g)
    m_hat = m / (1.0 - ADAM_B1 ** ADAM_STEP)
    v_hat = v / (1.0 - ADAM_B2 ** ADAM_STEP)
    delta = -ADAM_LR * (m_hat / (jnp.sqrt(v_hat) + ADAM_EPS) + ADAM_WD * w)
    return delta, m, v


def reduce_adamw(name, recvs, w, m, v):
    nl, r, c = w.shape
    tr = 128 if r % 128 == 0 else r
    nr = r // tr

    def body(*refs):
        r_refs = refs[:nl]
        w_ref, m_ref, v_ref, g_ref, d_ref, mo_ref, vo_ref = refs[nl:]
        layer = pl.program_id(0)
        for l in range(nl):
            @pl.when(layer == l)
            def _(l=l):
                g = r_refs[l][0].astype(F32)
                for s in range(1, N_DEV):
                    g = g + r_refs[l][s].astype(F32)
                g_ref[...] = g
                d_ref[...], mo_ref[...], vo_ref[...] = _adamw(w_ref[...], g, m_ref[...], v_ref[...])

    def recv_spec(l):
        return pl.BlockSpec((N_DEV, tr, c),
                            lambda y, i: (0, jnp.where(y == l, i, jnp.where(y < l, 0, nr - 1)), 0))

    blk = pl.BlockSpec((None, tr, c), lambda y, i: (y, i, 0))
    return pl.pallas_call(
        body, name=name, grid=(nl, nr),
        in_specs=[recv_spec(l) for l in range(nl)] + [blk, blk, blk],
        out_specs=[blk] * 4, out_shape=[jax.ShapeDtypeStruct((nl, r, c), F32)] * 4,
        compiler_params=_params(2),
    )(*recvs, w, m, v)


def small_reduce(recv):
    def body(r_ref, o_ref):
        g = r_ref[0]
        for s in range(1, N_DEV):
            g = g + r_ref[s]
        o_ref[...] = g

    return pl.pallas_call(body, name="small_reduce", out_shape=jax.ShapeDtypeStruct(recv.shape[1:], F32))(recv)


def small_adamw(w, g, m, v):
    def body(w_ref, g_ref, m_ref, v_ref, d_ref, mo_ref, vo_ref):
        d_ref[...], mo_ref[...], vo_ref[...] = _adamw(w_ref[...], g_ref[...], m_ref[...], v_ref[...])

    return pl.pallas_call(body, name="small_adamw", out_shape=[jax.ShapeDtypeStruct(w.shape, F32)] * 3)(w, g, m, v)


def _pack(parts):
    flat, meta, off = [], [], 0
    for p in parts:
        n = int(np.prod(p.shape))
        flat.append(p.reshape(-1).astype(F32))
        meta.append((off, p.shape))
        off += n
    total = -(-off // (SUBLANES * LANES)) * (SUBLANES * LANES)
    flat.append(jnp.zeros((total - off,), F32))
    return jnp.concatenate(flat).reshape(total // LANES, LANES), meta


def _unpack(packed, meta):
    flat = packed.reshape(-1)
    return [flat[off:off + int(np.prod(shape))].reshape(shape) for off, shape in meta]


MESH = pl.DeviceIdType.MESH
N_PEERS = N_DEV - 1


def _me():
    return lax.axis_index("x"), lax.axis_index("y"), lax.axis_index("c")


def _peer(k):
    x, y, c = _me()
    return (1 - x if k & 4 else x, 1 - y if k & 2 else y, 1 - c if k & 1 else c)


def _dev_index(pos):
    return 4 * pos[0] + 2 * pos[1] + pos[2]


RELAYED = (3, 5, 7)


def make_comm(items):
    n = len(items)

    def part(ref, a, idx):
        rows = items[a][1]
        if rows == "all":
            return ref
        return ref.at[idx] if rows is None else ref.at[pl.ds(idx * rows, rows)]

    def part_shape(a):
        arr, rows = items[a]
        if rows == "all":
            return arr.shape
        return arr.shape[1:] if rows is None else (rows,) + arr.shape[1:]

    def run(phase, ins, outs, send_sems, recv_sems, local_sems):
        me = _dev_index(_me())

        def copy(a, k, src, slot, to):
            return pltpu.make_async_remote_copy(
                src_ref=src, dst_ref=outs[a].at[slot], send_sem=send_sems.at[a, k - 1],
                recv_sem=recv_sems.at[a, k - 1], device_id=to, device_id_type=MESH)

        for a in range(n):
            gather = items[a][1] == "all"
            local = pltpu.make_async_copy(part(ins[a], a, me), outs[a].at[me], local_sems.at[a])
            if phase == "start":
                local.start()
            for k in range(1, N_DEV):
                if gather and k in RELAYED:
                    continue
                peer = _peer(k)
                if phase == "start":
                    copy(a, k, part(ins[a], a, _dev_index(peer)), me, peer).start()
                else:
                    cp = copy(a, k, part(ins[a], a, me), _dev_index(peer), peer)
                    cp.wait_recv()
                    cp.wait_send()
            if phase == "wait":
                local.wait()
                if gather:
                    sibling = _peer(1)
                    relays = []
                    for k in RELAYED:
                        origin = _dev_index(_peer(k - 1))
                        relays.append(copy(a, k, outs[a].at[origin], origin, sibling))
                        relays[-1].start()
                    for k in RELAYED:
                        copy(a, k, ins[a], _dev_index(_peer(k)), sibling).wait_recv()
                    for cp in relays:
                        cp.wait_send()

    return dict(
        ins=[it[0] for it in items],
        outs=[jax.ShapeDtypeStruct((N_DEV,) + part_shape(a), items[a][0].dtype) for a in range(n)],
        sems=[pltpu.SemaphoreType.DMA((n, N_PEERS)), pltpu.SemaphoreType.DMA((n, N_PEERS)),
              pltpu.SemaphoreType.DMA((n,))],
        run=run)


ANY_SPEC = pl.BlockSpec(memory_space=pl.ANY)


def comm_call(name, comm):
    n, no = len(comm["ins"]), len(comm["outs"])

    def body(*refs):
        comm["run"]("start", refs[:n], refs[n:n + no], *refs[n + no:])
        comm["run"]("wait", refs[:n], refs[n:n + no], *refs[n + no:])

    return pl.pallas_call(
        body, name=name, in_specs=[ANY_SPEC] * n, out_specs=[ANY_SPEC] * no, out_shape=comm["outs"],
        scratch_shapes=comm["sems"])(*comm["ins"])


def with_comm(comm, body, grid, in_specs, out_specs, out_shape, scratch_shapes):
    if comm is None:
        return body, in_specs, out_specs, out_shape, scratch_shapes, [], len(out_shape)
    n_in, n_out, n_scr = len(in_specs), len(out_shape), len(scratch_shapes)
    ci, co = len(comm["ins"]), len(comm["outs"])

    def wrapped(*refs):
        ins, cins = refs[:n_in], refs[n_in:n_in + ci]
        outs = refs[n_in + ci:n_in + ci + n_out]
        couts = refs[n_in + ci + n_out:n_in + ci + n_out + co]
        rest = refs[n_in + ci + n_out + co:]
        scr, sems = rest[:n_scr], rest[n_scr:]
        first = functools.reduce(jnp.logical_and, [pl.program_id(a) == 0 for a in range(len(grid))])
        last = functools.reduce(jnp.logical_and, [pl.program_id(a) == grid[a] - 1 for a in range(len(grid))])

        @pl.when(first)
        def _():
            comm["run"]("start", cins, couts, *sems)

        body(*ins, *outs, *scr)

        @pl.when(last)
        def _():
            comm["run"]("wait", cins, couts, *sems)

    return (wrapped, list(in_specs) + [ANY_SPEC] * ci, list(out_specs) + [ANY_SPEC] * co,
            list(out_shape) + list(comm["outs"]), list(scratch_shapes) + list(comm["sems"]), list(comm["ins"]), n_out)


WEIGHTS = ['meta_tokens', 'ln_mix', 'ln_mlp', 'ssd_w_in', 'ssd_conv_w', 'ssd_conv_b', 'ssd_dt_bias', 'ssd_a_log',
           'ssd_d', 'ssd_norm', 'ssd_w_out', 'mla_w_in', 'mla_q_a_norm', 'mla_w_q_b', 'mla_kv_a_norm', 'mla_w_kv_b',
           'mla_q_norm', 'mla_k_norm', 'mla_w_out', 'mlp_w_up', 'mlp_w_down']
BIG = ['ssd_w_in', 'ssd_w_out', 'mla_w_in', 'mla_w_q_b', 'mla_w_kv_b', 'mla_w_out', 'mlp_w_up', 'mlp_w_down']
SMALL_SHARDED = ['meta_tokens', 'ssd_conv_w', 'mla_q_a_norm', 'mla_kv_a_norm']
SMALL_REPL = ['ln_mix', 'ln_mlp', 'ssd_conv_b', 'ssd_dt_bias', 'ssd_a_log', 'ssd_d', 'ssd_norm', 'mla_q_norm',
              'mla_k_norm']
SMALL = SMALL_REPL + SMALL_SHARDED
SSD_IN_PAD = 6272
SSD_IN_TN = 896
MLA_IN = MLA_Q_RANK + MLA_KV_RANK + MLA_ROPE


def _pad_last(v, n):
    return jnp.pad(v, [(0, 0)] * (v.ndim - 1) + [(0, n - v.shape[-1])])


SSD_BIG = ['ssd_w_in', 'ssd_w_out']
MLA_BIG = ['mla_w_in', 'mla_w_q_b', 'mla_w_kv_b', 'mla_w_out']
MLP_BIG = ['mlp_w_up', 'mlp_w_down']


def _mix_big(i):
    return [(n, i // 2) for n in (SSD_BIG if i % 2 == 0 else MLA_BIG)]


def _mlp_big(i):
    return [(n, i) for n in MLP_BIG]


def _mix_weights(i, gw, W, full):
    j = i // 2
    d = W['ln_mix'].shape[-1]
    if i % 2 == 0:
        wi = gw[('ssd_w_in', j)].transpose(1, 0, 2).reshape(d, -1)
        par = jnp.concatenate([_pad_last(W[n][j][None], LANES) for n in ('ssd_dt_bias', 'ssd_a_log', 'ssd_d')]
                              + [jnp.zeros((SUBLANES - 3, LANES), F32)])
        return dict(w_in=_pad_last(wi, SSD_IN_PAD), conv_w=full['ssd_conv_w'][j], conv_b=W['ssd_conv_b'][j][None],
                    par=par, norm=W['ssd_norm'][j][None], w_out=gw[('ssd_w_out', j)].reshape(SSD_D_INNER, d))
    wi = gw[('mla_w_in', j)].reshape(d, MLA_IN)
    kpe = jnp.pad(wi[:, MLA_Q_RANK + MLA_KV_RANK:], ((0, 0), (MLA_NOPE, LANES - MLA_QK)))
    wq = gw[('mla_w_q_b', j)].transpose(1, 0, 2).reshape(MLA_Q_RANK, MLA_HEADS, MLA_QK)
    wkv = gw[('mla_w_kv_b', j)].transpose(1, 0, 2).reshape(MLA_KV_RANK, MLA_HEADS, MLA_NOPE + MLA_V)
    return dict(
        w_in=jnp.concatenate([wi[:, :MLA_Q_RANK], kpe, wi[:, MLA_Q_RANK:MLA_Q_RANK + MLA_KV_RANK]], axis=1),
        w_q=_pad_last(wq, LANES).reshape(MLA_Q_RANK, QW),
        w_kv=jnp.concatenate([_pad_last(wkv[..., :MLA_NOPE], LANES).reshape(MLA_KV_RANK, QW),
                              wkv[..., MLA_NOPE:].reshape(MLA_KV_RANK, VW)], axis=1),
        w_out=gw[('mla_w_out', j)].reshape(VW, d), q_a=full['mla_q_a_norm'][j][None],
        kv_a=full['mla_kv_a_norm'][j][None],
        q_norm=_pad_last(W['mla_q_norm'][j][None], LANES), k_norm=_pad_last(W['mla_k_norm'][j][None], LANES))


def _step(x, target, W, M, V):
    d = x.shape[-1]
    me = _dev_index(_me())
    depth = W['ln_mix'].shape[0]

    def gather_keys(i):
        return _mlp_big(i) + (_mix_big(i + 1) if i + 1 < depth else [])

    def gather_items(keys):
        return [(W[n][l].astype(BF16), "all") for n, l in keys]

    small_pack, small_meta = _pack([W[n] for n in SMALL_SHARDED])
    got = comm_call("gather_0", make_comm(gather_items(_mix_big(0)) + [(small_pack, "all")]))
    per_dev = [_unpack(got[-1][s], small_meta) for s in range(N_DEV)]
    full = {n: jnp.concatenate([per_dev[s][i] for s in range(N_DEV)], axis=-1) for i, n in enumerate(SMALL_SHARDED)}
    gw = dict(zip(_mix_big(0), got))

    h = jnp.concatenate([jnp.zeros((PAD, d), F32), full['meta_tokens'], x], axis=0)
    rows = h.shape[0]
    tabs = rope_tables(rows)
    saved, weights = [], []
    for i in range(depth):
        comm = make_comm(gather_items(gather_keys(i)))
        mix = _mix_weights(i, gw, W, full)
        if i % 2 == 0:
            h, s_mix, got = ssd_layer_fwd(f"{i}", h, W['ln_mix'][i][None], mix, comm)
        else:
            h, s_mix, got = mla_layer_fwd(f"{i}", h, W['ln_mix'][i][None], mix, tabs, comm)
        gw.update(zip(gather_keys(i), got))
        up, down = gw[('mlp_w_up', i)], gw[('mlp_w_down', i)].reshape(-1, d)
        h, s_mlp = mlp_fwd(f"{i}", h, W['ln_mlp'][i][None], up, down)
        saved.append((s_mix, s_mlp))
        weights.append((mix, up, down))
    loss_part, dh = loss_head(h, target)
    loss = lax.psum(jnp.sum(loss_part), ("x", "y", "c"))

    recv = {}
    pending = []
    small = {n: [None] * W[n].shape[0] for n in SMALL if n != 'meta_tokens'}
    for i in reversed(range(depth)):
        j = i // 2
        s_mix, s_mlp = saved[i]
        mix, up, down = weights[i]
        dh, dw_up, dw_down, dg = mlp_bwd(f"{i}", dh, s_mlp, W['ln_mlp'][i][None], up, down)
        small['ln_mlp'][i] = dg.sum(0)
        pending += list(zip(_mlp_big(i), [(dw_up, None), (dw_down, down.shape[0] // N_DEV)]))
        comm = make_comm([it for _, it in pending])
        sends = []
        if i % 2 == 0:
            dh, g, got = ssd_layer_bwd(f"{i}", dh, s_mix, W['ln_mix'][i][None], mix, comm)
            n_in = W['ssd_w_in'].shape[-1]
            sends.append((g['w_in'][:, :N_DEV * n_in].reshape(d, N_DEV, n_in).transpose(1, 0, 2), None))
            sends.append((g['w_out'], SSD_D_INNER // N_DEV))
            small['ssd_conv_w'][j] = g['conv_w'].reshape(SSD_CONV, SUBLANES, -1).sum(1)
            small['ssd_conv_b'][j] = g['conv_b'].sum(0)
            small['ssd_dt_bias'][j] = g['par'][0, :SSD_HEADS]
            small['ssd_a_log'][j] = g['par'][1, :SSD_HEADS]
            small['ssd_d'][j] = g['par'][2, :SSD_HEADS]
            small['ssd_norm'][j] = g['norm'].sum(0)
        else:
            dh, g, got = mla_layer_bwd(f"{i}", dh, s_mix, W['ln_mix'][i][None], mix, tabs, comm)
            gi = g['w_in']
            gi = jnp.concatenate([gi[:, :MLA_Q_RANK], gi[:, MLA_Q_RANK + LANES:],
                                  gi[:, MLA_Q_RANK + MLA_NOPE:MLA_Q_RANK + MLA_QK]], axis=1)
            sends.append((gi, d // N_DEV))
            gq = g['w_q'].reshape(MLA_Q_RANK, MLA_HEADS, LANES)[..., :MLA_QK]
            sends.append((gq.reshape(MLA_Q_RANK, N_DEV, -1).transpose(1, 0, 2), None))
            gkv = jnp.concatenate([g['w_kv'][:, :QW].reshape(MLA_KV_RANK, MLA_HEADS, LANES)[..., :MLA_NOPE],
                                   g['w_kv'][:, QW:].reshape(MLA_KV_RANK, MLA_HEADS, MLA_V)], axis=-1)
            sends.append((gkv.reshape(MLA_KV_RANK, N_DEV, -1).transpose(1, 0, 2), None))
            sends.append((g['w_out'], VW // N_DEV))
            small['mla_q_a_norm'][j] = g['q_a'].sum(0)
            small['mla_kv_a_norm'][j] = g['kv_a'].sum(0)
            small['mla_q_norm'][j] = g['q_norm'].sum(0)[:MLA_QK]
            small['mla_k_norm'][j] = g['k_norm'].sum(0)[:MLA_QK]
        small['ln_mix'][i] = g['ln'].sum(0)
        recv.update({key: a for (key, _), a in zip(pending, got)})
        pending = list(zip(_mix_big(i), sends))
    grad_x = dh[CHUNK:]
    small_full = {n: jnp.stack(v) for n, v in small.items()}
    small_full['meta_tokens'] = dh[PAD:CHUNK]

    gpack, gmeta = _pack([small_full[n] for n in SMALL])
    got = comm_call("exchange_0", make_comm([it for _, it in pending] + [(gpack, "all")]))
    recv.update({key: a for (key, _), a in zip(pending, got)})
    res = {}
    for n in BIG:
        res[n] = reduce_adamw(f"adamw_{n}", [recv[(n, l)] for l in range(W[n].shape[0])], W[n], M[n], V[n])
    gsum = dict(zip(SMALL, _unpack(small_reduce(got[-1]), gmeta)))
    for n in SMALL_SHARDED:
        wl = W[n].shape[-1]
        gsum[n] = lax.dynamic_slice_in_dim(gsum[n], me * wl, wl, axis=gsum[n].ndim - 1)
    wp, wmeta = _pack([W[n] for n in SMALL])
    gp, _ = _pack([gsum[n] for n in SMALL])
    mp, _ = _pack([M[n] for n in SMALL])
    vp, _ = _pack([V[n] for n in SMALL])
    upd = [_unpack(o, wmeta) for o in small_adamw(wp, gp, mp, vp)]
    for a, n in enumerate(SMALL):
        res[n] = [gsum[n], upd[0][a], upd[1][a], upd[2][a]]
    return (loss, grad_x[None]) + tuple(res[n][q] for q in range(4) for n in WEIGHTS)


def kernel(x, meta_tokens, ln_mix, ln_mlp, ssd_w_in, ssd_conv_w, ssd_conv_b, ssd_dt_bias, ssd_a_log, ssd_d, ssd_norm, ssd_w_out, mla_w_in, mla_q_a_norm, mla_w_q_b, mla_kv_a_norm, mla_w_kv_b, mla_q_norm, mla_k_norm, mla_w_out, mlp_w_up, mlp_w_down, loss_target, m_meta_tokens, m_ln_mix, m_ln_mlp, m_ssd_w_in, m_ssd_conv_w, m_ssd_conv_b, m_ssd_dt_bias, m_ssd_a_log, m_ssd_d, m_ssd_norm, m_ssd_w_out, m_mla_w_in, m_mla_q_a_norm, m_mla_w_q_b, m_mla_kv_a_norm, m_mla_w_kv_b, m_mla_q_norm, m_mla_k_norm, m_mla_w_out, m_mlp_w_up, m_mlp_w_down, v_meta_tokens, v_ln_mix, v_ln_mlp, v_ssd_w_in, v_ssd_conv_w, v_ssd_conv_b, v_ssd_dt_bias, v_ssd_a_log, v_ssd_d, v_ssd_norm, v_ssd_w_out, v_mla_w_in, v_mla_q_a_norm, v_mla_w_q_b, v_mla_kv_a_norm, v_mla_w_kv_b, v_mla_q_norm, v_mla_k_norm, v_mla_w_out, v_mlp_w_up, v_mlp_w_down):
    given = dict(locals())
    W = {n: given[n] for n in WEIGHTS}
    M = {n: given["m_" + n] for n in WEIGHTS}
    V = {n: given["v_" + n] for n in WEIGHTS}
    return _step(x[0], loss_target[0], W, M, V)
```

```python
import functools

import jax
import jax.numpy as jnp
import numpy as np
from jax import lax
from jax.experimental import pallas as pl
from jax.experimental.pallas import tpu as pltpu

F32 = jnp.float32
BF16 = jnp.bfloat16

EPS = 1e-6
N_META = 16
CHUNK = 128
PAD = CHUNK - N_META
SSD_HEAD_DIM = 64
SSD_HEADS = 32
SSD_GROUPS = 8
SSD_HPG = 4
SSD_STATE = 128
SSD_D_INNER = 2048
SSD_CONV = 4
MLA_HEADS = 16
MLA_NOPE = 64
MLA_ROPE = 32
MLA_V = 64
MLA_QK = 96
MLA_Q_RANK = 384
MLA_KV_RANK = 256
ROPE_THETA = 10000.0
LANES = 128
SUBLANES = 8
N_DEV = 8
VMEM_LIMIT = 56 * 1024 * 1024

ADAM_LR = 0.001
ADAM_B1 = 0.9
ADAM_B2 = 0.999
ADAM_EPS = 1e-08
ADAM_WD = 0.01
ADAM_STEP = 10

NEG = -1e30


def _row_tile(rows):
    return 384 if (rows % 384 == 0 and rows > 384) else 128


def _big_tile(rows):
    return 1408 if rows % 1408 == 0 else _row_tile(rows)


def _params(n_axes, vmem=VMEM_LIMIT):
    return pltpu.CompilerParams(dimension_semantics=("arbitrary",) * n_axes, vmem_limit_bytes=vmem)


def _dot(a, b, dims):
    return lax.dot_general(a.astype(BF16), b.astype(BF16), (dims, ((), ())), preferred_element_type=F32)


NN = ((1,), (0,))
NT = ((1,), (1,))
TN = ((0,), (0,))


@jax.custom_vjp
def bdot_nn(a, b):
    return _dot(a, b, NN)


@jax.custom_vjp
def bdot_nt(a, b):
    return _dot(a, b, NT)


@jax.custom_vjp
def bdot_tn(a, b):
    return _dot(a, b, TN)


bdot_nn.defvjp(lambda a, b: (_dot(a, b, NN), (a, b)),
               lambda r, g: (_dot(g, r[1], NT), _dot(r[0], g, TN)))
bdot_nt.defvjp(lambda a, b: (_dot(a, b, NT), (a, b)),
               lambda r, g: (_dot(g, r[1], NN), _dot(g, r[0], TN)))
bdot_tn.defvjp(lambda a, b: (_dot(a, b, TN), (a, b)),
               lambda r, g: (_dot(r[1], g, NT), _dot(r[0], g, NN)))


def _rows8(v):
    r, n = v.shape
    return v.reshape(r // SUBLANES, SUBLANES, n).sum(axis=0)


def _row_mask(i, tm):
    return (i * tm + lax.broadcasted_iota(jnp.int32, (tm, 1), 0)) >= PAD


def fused_mm(name, *, rows, k, n, tm, tn, tk=None, a_ops, pro, w, w_block, w_imap, dot, e_ops=(), epi, outs):
    tk = tk or k
    ni, nj, nk = rows // tm, n // tn, k // tk
    assert rows % tm == 0 and n % tn == 0 and k % tk == 0
    assert nk == 1 or nj == 1
    cache = nk == 1 and nj > 1
    na, ne, no = len(a_ops), len(e_ops), len(outs)

    def body(*refs):
        a_refs = refs[:na]
        w_ref = refs[na]
        e_refs = refs[na + 1:na + 1 + ne]
        o_refs = refs[na + 1 + ne:na + 1 + ne + no]
        scr = refs[na + 1 + ne + no:]
        i, j, kk = pl.program_id(0), pl.program_id(1), pl.program_id(2)
        if cache:
            a_scr = scr[0]

            @pl.when(j == 0)
            def _():
                a_scr[...] = pro(a_refs, o_refs, i)

            a = a_scr[...]
        else:
            a = pro(a_refs, o_refs, i)
        part = dot(a, w_ref)
        if nk == 1:
            epi(part, e_refs, o_refs, i, j)
        else:
            acc_ref = scr[0]

            @pl.when(kk == 0)
            def _():
                acc_ref[...] = part

            @pl.when(kk > 0)
            def _():
                acc_ref[...] += part

            @pl.when(kk == nk - 1)
            def _():
                epi(acc_ref[...], e_refs, o_refs, i, j)

    scratch = []
    if cache:
        scratch.append(pltpu.VMEM((tm, k), BF16))
    if nk > 1:
        scratch.append(pltpu.VMEM((tm, tn), F32))
    in_specs = [pl.BlockSpec(b, m) for (_, b, m) in a_ops]
    in_specs.append(pl.BlockSpec(w_block, w_imap))
    in_specs += [pl.BlockSpec(b, m) for (_, b, m) in e_ops]
    return pl.pallas_call(
        body, name=name, grid=(ni, nj, nk),
        in_specs=in_specs,
        out_specs=[pl.BlockSpec(b, m) for (_, _, b, m) in outs],
        out_shape=[jax.ShapeDtypeStruct(s, d) for (s, d, _, _) in outs],
        scratch_shapes=scratch,
        compiler_params=_params(3),
    )(*[a for (a, _, _) in a_ops], w, *[e for (e, _, _) in e_ops])


def _lw(w, block, imap):
    if isinstance(w, tuple):
        arr, layer = w
        return arr, (None,) + block, (lambda i, j, kk: (layer,) + imap(i, j, kk))
    return w, block, imap


def _dot_w(a, w_ref):
    return jnp.dot(a, w_ref[...], preferred_element_type=F32)


def _dot_wt(a, w_ref):
    return lax.dot_general(a, w_ref[...], (NT, ((), ())), preferred_element_type=F32)


def _rms_pro(h, g):
    r = lax.rsqrt(jnp.mean(h * h, axis=-1, keepdims=True) + EPS)
    return h * r * g


def _rms_bwd(dyn, h, g):
    r = lax.rsqrt(jnp.mean(h * h, axis=-1, keepdims=True) + EPS)
    xh = h * r
    t = dyn * g
    dh = r * (t - xh * jnp.mean(t * xh, axis=-1, keepdims=True))
    return dh, _rows8(dyn * xh)


def _acc_out(ref, val, first):
    @pl.when(first)
    def _():
        ref[...] = val

    @pl.when(jnp.logical_not(first))
    def _():
        ref[...] += val


def norm_mm(name, h, g, w, *, tn, k_cols=None, col_block=0, w_stacked=False, out_dtype=F32):
    rows = h.shape[0]
    k = k_cols or h.shape[1]
    wshape = (w[0].shape[1:] if isinstance(w, tuple) else w.shape)
    n = wshape[0] * wshape[2] if w_stacked else wshape[1]
    tm = _big_tile(rows)

    def pro(a_refs, o_refs, i):
        hn = _rms_pro(a_refs[0][...], a_refs[1][...]).astype(BF16)
        o_refs[1][...] = hn
        return hn

    def epi(acc, e_refs, o_refs, i, j):
        o_refs[0][...] = acc.astype(out_dtype)

    if w_stacked:
        w_block, w_imap = (None, k, tn), (lambda i, j, kk: (j, 0, 0))
    else:
        w_block, w_imap = (k, tn), (lambda i, j, kk: (0, j))
    w, w_block, w_imap = _lw(w, w_block, w_imap)
    return fused_mm(
        name, rows=rows, k=k, n=n, tm=tm, tn=tn,
        a_ops=[(h, (tm, k), lambda i, j, kk: (i, col_block)), (g, (1, k), lambda i, j, kk: (0, 0))],
        pro=pro, w=w, w_block=w_block, w_imap=w_imap, dot=_dot_w, epi=epi,
        outs=[((rows, n), out_dtype, (tm, tn), lambda i, j, kk: (i, j)),
              ((rows, k), BF16, (tm, k), lambda i, j, kk: (i, 0))])


def res_mm(name, a_ops, pro, k, w, res, *, tn, save_dtype=None, tm=None, tk=None):
    rows, n = res.shape
    tm = tm or _row_tile(rows)
    assert tk is None or save_dtype is None

    def pro2(a_refs, o_refs, i):
        a = pro(a_refs)
        if save_dtype is not None:
            o_refs[1][...] = a
        return a

    def epi(acc, e_refs, o_refs, i, j):
        o_refs[0][...] = e_refs[0][...] + acc

    outs = [((rows, n), F32, (tm, tn), lambda i, j, kk: (i, j))]
    if save_dtype is not None:
        outs.append(((rows, k), save_dtype, (tm, k), lambda i, j, kk: (i, 0)))
    w, w_block, w_imap = _lw(w, (tk or k, tn), lambda i, j, kk: (kk, j))
    out = fused_mm(
        name, rows=rows, k=k, n=n, tm=tm, tn=tn, tk=tk,
        a_ops=[(a, tuple(tm if x is None else x for x in b), m) for (a, b, m) in a_ops],
        pro=pro2, w=w, w_block=w_block, w_imap=w_imap, dot=_dot_w,
        e_ops=[(res, (tm, tn), lambda i, j, kk: (i, j))], epi=epi, outs=outs)
    return out if save_dtype is not None else out[0]


def wgrad_mm(name, a_ops, pro_a, g_ops, pro_g, *, rows, k1, n, t1, tn, out_shape=None, out_block=None, out_imap=None):
    tt = _row_tile(rows)
    n1, n2, nt = k1 // t1, n // tn, rows // tt
    assert k1 % t1 == 0 and n % tn == 0
    na = len(a_ops)

    def body(*refs):
        a_refs = refs[:na]
        g_refs = refs[na:-2]
        o_ref, acc = refs[-2:]
        t = pl.program_id(2)
        a = pro_a(a_refs).astype(BF16)
        g = pro_g(g_refs).astype(BF16)
        _acc_out(acc, lax.dot_general(a, g, (TN, ((), ())), preferred_element_type=F32), t == 0)

        @pl.when(t == nt - 1)
        def _():
            if len(o_ref.shape) == 3:
                ws = o_ref.shape[2]
                for q in range(o_ref.shape[0]):
                    o_ref[q] = acc[:, q * ws:(q + 1) * ws].astype(BF16)
            else:
                o_ref[...] = acc[...].astype(BF16)

    return pl.pallas_call(
        body, name=name, grid=(n1, n2, nt),
        in_specs=[pl.BlockSpec(b, m) for (_, b, m) in list(a_ops) + list(g_ops)],
        out_specs=pl.BlockSpec(out_block or (t1, tn), out_imap or (lambda a, b, t: (a, b))),
        out_shape=jax.ShapeDtypeStruct(out_shape or (k1, n), BF16),
        scratch_shapes=[pltpu.VMEM((t1, tn), F32)],
        compiler_params=_params(3),
    )(*[a for (a, _, _) in list(a_ops) + list(g_ops)])


def simple_wgrad(name, a, g, *, a_cols=None, a_col_block=0, t1=None, tn=None, **kw):
    rows = a.shape[0]
    k1 = a_cols or a.shape[1]
    n = g.shape[1]
    tt = _row_tile(rows)
    t1 = t1 or min(k1, 512)
    tn = tn or min(n, 1024)
    return wgrad_mm(
        name,
        [(a, (tt, t1), lambda x, y, t: (t, x + a_col_block * (k1 // t1)))], lambda r: r[0][...],
        [(g, (tt, tn), lambda x, y, t: (t, y))], lambda r: r[0][...],
        rows=rows, k1=k1, n=n, t1=t1, tn=tn, **kw)


def rms_bwd_mm(name, dz_ops, pro, k, w, w_block, w_imap, dot, h, g, dh, *, tk=None, h_cols=None, h_col_block=0,
               add_dh=True, tm=None):
    rows = h.shape[0]
    n = h_cols or h.shape[1]
    tm = tm or _big_tile(rows)
    ni = rows // tm
    w, w_block, w_imap = _lw(w, w_block, w_imap)

    def epi(acc, e_refs, o_refs, i, j):
        d, dg = _rms_bwd(acc, e_refs[0][...], e_refs[1][...])
        if add_dh:
            d = d + e_refs[2][...]
        o_refs[0][...] = jnp.where(_row_mask(i, tm), d, 0.0)
        _acc_out(o_refs[1], dg, i == 0)

    e_ops = [(h, (tm, n), lambda i, j, kk: (i, h_col_block)), (g, (1, n), lambda i, j, kk: (0, 0))]
    if add_dh:
        e_ops.append((dh, (tm, n), lambda i, j, kk: (i, 0)))
    return fused_mm(
        name, rows=rows, k=k, n=n, tm=tm, tn=n, tk=tk,
        a_ops=[(a, tuple(tm if x is None else x for x in b), m) for (a, b, m) in dz_ops],
        pro=lambda a_refs, o_refs, i: pro(a_refs), w=w, w_block=w_block, w_imap=w_imap, dot=dot,
        e_ops=e_ops, epi=epi,
        outs=[((rows, n), F32, (tm, n), lambda i, j, kk: (i, 0)),
              ((SUBLANES, n), F32, (SUBLANES, n), lambda i, j, kk: (0, 0))])


def _relu2(u):
    r = jnp.maximum(u.astype(F32), 0.0)
    return r * r


def mlp_fwd(tag, h, g, w_up_st, w_down):
    d_ff = w_down.shape[0]
    u, hn = norm_mm(f"mlp_up_{tag}", h, g, w_up_st, tn=w_up_st.shape[2], w_stacked=True, out_dtype=BF16)
    out = res_mm(f"mlp_down_{tag}", [(u, (None, 512), lambda i, j, kk: (i, kk))],
                 lambda r: _relu2(r[0][...]).astype(BF16), d_ff, w_down, h, tn=h.shape[1],
                 tm=_big_tile(h.shape[0]), tk=512)
    return out, (h, hn, u)


def mlp_bwd(tag, dh, saved, g, w_up_st, w_down):
    h, hn, u = saved
    rows, d = h.shape
    d_ff = w_down.shape[0]
    ts = w_up_st.shape[2]
    tm = _big_tile(rows)
    tt = _row_tile(rows)
    wd, wd_block, wd_imap = _lw(w_down, (512, d), lambda i, j, kk: (j, 0))

    def epi_du(acc, e_refs, o_refs, i, j):
        o_refs[0][...] = (acc * (2.0 * jnp.maximum(e_refs[0][...].astype(F32), 0.0))).astype(BF16)

    du, = fused_mm(
        f"mlp_du_{tag}", rows=rows, k=d, n=d_ff, tm=tm, tn=512,
        a_ops=[(dh, (tm, d), lambda i, j, kk: (i, 0))], pro=lambda a, o, i: a[0][...].astype(BF16),
        w=wd, w_block=wd_block, w_imap=wd_imap, dot=_dot_wt,
        e_ops=[(u, (tm, 512), lambda i, j, kk: (i, j))], epi=epi_du,
        outs=[((rows, d_ff), BF16, (tm, 512), lambda i, j, kk: (i, j))])
    half = d_ff // 2
    dw_down = wgrad_mm(
        f"mlp_dwdown_{tag}",
        [(u, (tt, half), lambda a, b, t: (t, a))], lambda r: _relu2(r[0][...]),
        [(dh, (tt, d), lambda a, b, t: (t, 0))], lambda r: r[0][...],
        rows=rows, k1=d_ff, n=d, t1=half, tn=d)
    dw_up = simple_wgrad(f"mlp_dwup_{tag}", hn, du, t1=d, tn=half, out_shape=(N_DEV, d, ts),
                         out_block=(half // ts, d, ts), out_imap=lambda a, b, t: (b, 0, 0))
    dh_in, dg = rms_bwd_mm(
        f"mlp_dh_{tag}", [(du, (None, ts), lambda i, j, kk: (i, kk))], lambda r: r[0][...], d_ff,
        w_up_st, (None, d, ts), lambda i, j, kk: (kk, 0, 0), _dot_wt, h, g, dh, tk=ts)
    return dh_in, dw_up, dw_down, dg


CONV_HALO = SUBLANES
CONV_W = 2 * SSD_D_INNER
CONV_BLK = 512


def _silu(x):
    return x * jax.nn.sigmoid(x)


def _conv_pre(ext_ref, w, b, tm):
    pre = b
    for k in range(SSD_CONV):
        pre = pre + w[k:k + 1, :] * ext_ref[pl.ds(CONV_HALO - (SSD_CONV - 1) + k, tm), :]
    return pre


def _conv_load(ext, parts, c):
    for cur, halo, lo, hi in parts:
        ext[pl.ds(0, CONV_HALO), lo:hi] = jnp.where(c > 0, halo[...], 0.0)
        ext[pl.ds(CONV_HALO, CHUNK), lo:hi] = cur[...]


def _conv_specs(idx):
    hb = CHUNK // CONV_HALO
    specs = []
    for w, blk in ((SSD_D_INNER, 1), (BC_W, SSD_D_INNER // BC_W + 2), (BC_W, SSD_D_INNER // BC_W + 3)):
        specs.append(pl.BlockSpec((CHUNK, w), lambda s, blk=blk: (idx(s), blk)))
        specs.append(pl.BlockSpec((CONV_HALO, w), lambda s, blk=blk: (jnp.maximum(idx(s) * hb - 1, 0), blk)))
    return specs


CONV_COLS = ((0, SSD_D_INNER), (SSD_D_INNER, SSD_D_INNER + SSD_GROUPS * SSD_STATE),
             (SSD_D_INNER + SSD_GROUPS * SSD_STATE, 2 * SSD_D_INNER))


@functools.partial(jax.custom_vjp, nondiff_argnums=(1,))
def _sub_row(x, h):
    return x[h:h + 1, :]


_sub_row.defvjp(
    lambda x, h: (x[h:h + 1, :], None),
    lambda h, _, g: (jnp.where(lax.broadcasted_iota(jnp.int32, (LANES, 1), 0) == h, g, 0.0),))


def _splitter(axis, size, count):
    def blocks(x):
        return tuple(lax.slice_in_dim(x, q * size, (q + 1) * size, axis=axis) for q in range(count))

    split = jax.custom_vjp(blocks)
    split.defvjp(lambda x: (blocks(x), None), lambda _, gs: (jnp.concatenate(gs, axis=axis),))
    return split


def _split3(x):
    hi = x.astype(BF16)
    r = x - hi.astype(F32)
    mid = r.astype(BF16)
    return hi, mid, (r - mid.astype(F32)).astype(BF16)


def _expand_impl(x, e):
    return sum(jnp.dot(t, e, preferred_element_type=F32) for t in _split3(x))


@jax.custom_vjp
def _expand(x, e):
    return _expand_impl(x, e)


def _expand_bwd(e, g):
    hi, mid, _ = _split3(g)
    dx = sum(lax.dot_general(t, e, (NT, ((), ())), preferred_element_type=F32) for t in (hi, mid))
    return dx, jnp.zeros_like(e)


_expand.defvjp(lambda x, e: (_expand_impl(x, e), e), _expand_bwd)

HEAD_PAIR = 2 * SSD_HEAD_DIM
GROUP_W = SSD_HPG * SSD_HEAD_DIM


def _ssd_chunk(xs, bm, cm, dtraw, prev, par, c, tri, e64, e128):
    li = lax.broadcasted_iota(jnp.int32, (CHUNK, CHUNK), 0)
    si = lax.broadcasted_iota(jnp.int32, (CHUNK, CHUNK), 1)
    causal = li >= si
    first_head = lax.broadcasted_iota(jnp.int32, (1, HEAD_PAIR), 1) < SSD_HEAD_DIM
    dt = jnp.where(_row_mask(c, CHUNK), jax.nn.softplus(dtraw + par[0:1, :]), 0.0)
    a = -jnp.exp(par[1:2, :])
    acs = jnp.dot(tri, dt * a, precision=lax.Precision.HIGHEST, preferred_element_type=F32)
    acs_t = acs.T
    last = acs[CHUNK - 1:CHUNK, :]
    misc = jnp.concatenate([jnp.exp(last), par[2:3, :], jnp.zeros((SUBLANES - 2, LANES), F32)], axis=0)
    wide = _expand(jnp.concatenate([dt, dt * jnp.exp(last - acs), jnp.exp(acs)], axis=0), e64)
    dt_w, dtend_w, start_w = _splitter(0, CHUNK, 3)(wide)
    misc_w = _expand(misc, e64)
    col_w = _splitter(1, CHUNK, SSD_HEADS)(_expand(acs, e128))
    groups = _splitter(1, GROUP_W, SSD_GROUPS)
    xs_g, prev_g, start_g = groups(xs), groups(prev), groups(start_w)
    xdt_p = _splitter(1, HEAD_PAIR, SSD_HEADS // 2)(xs * dt_w)
    xdtend_g = groups(xs * dtend_w)
    last_g, skip_g = groups(misc_w[0:1, :]), groups(misc_w[1:2, :])
    b_g, c_g = _splitter(1, SSD_STATE, SSD_GROUPS)(bm), _splitter(1, SSD_STATE, SSD_GROUPS)(cm)
    ys, news = [], []
    for g in range(SSD_GROUPS):
        cb = bdot_nt(c_g[g], b_g[g])
        st = bdot_nn(b_g[g].T, xdtend_g[g])
        y_off = bdot_nn(c_g[g], prev_g[g]) * start_g[g]
        pairs = []
        for q in range(SSD_HPG // 2):
            xp = xdt_p[g * (SSD_HPG // 2) + q]
            acc = None
            for r in range(2):
                head = SSD_HPG * g + 2 * q + r
                seg = jnp.where(causal, col_w[head] - _sub_row(acs_t, head), 0.0)
                decay = jnp.where(causal, jnp.exp(seg), 0.0)
                t = bdot_nn(cb * decay, jnp.where(first_head if r == 0 else jnp.logical_not(first_head), xp, 0.0))
                acc = t if acc is None else acc + t
            pairs.append(acc)
        ys.append(jnp.concatenate(pairs, axis=1) + y_off + xs_g[g] * skip_g[g])
        news.append(prev_g[g] * last_g[g] + st)
    return jnp.concatenate(ys, axis=1), jnp.concatenate(news, axis=1)


def _expanders():
    e64 = np.zeros((LANES, SSD_D_INNER), np.float32)
    e128 = np.zeros((LANES, SSD_HEADS * CHUNK), np.float32)
    for h in range(SSD_HEADS):
        e64[h, h * SSD_HEAD_DIM:(h + 1) * SSD_HEAD_DIM] = 1.0
        e128[h, h * CHUNK:(h + 1) * CHUNK] = 1.0
    return jnp.asarray(e64, BF16), jnp.asarray(e128, BF16)


def _tri():
    return jnp.asarray(np.tril(np.ones((CHUNK, CHUNK), np.float32)))


BC_W = SSD_GROUPS * SSD_STATE


def _conv_act(ext, refs, w, b, c):
    _conv_load(ext, [(refs[2 * p], refs[2 * p + 1]) + CONV_COLS[p] for p in range(3)], c)
    pre = _conv_pre(ext, w, b, CHUNK)
    return pre, jnp.where(_row_mask(c, CHUNK), _silu(pre), 0.0)


def ssd_fwd(name, zx, dt_block, par, conv_w, conv_b, comm=None):
    rows = zx.shape[0]
    nc = rows // CHUNK

    def body(*refs):
        xbc_refs, (dt_ref, par_ref, tri_ref, e64_ref, e128_ref, cw_ref, cb_ref) = refs[:6], refs[6:13]
        y_ref, st_ref, state, ext = refs[13:]
        c = pl.program_id(0)

        @pl.when(c == 0)
        def _():
            state[...] = jnp.zeros((SSD_STATE, SSD_D_INNER), F32)

        prev = state[...]
        st_ref[...] = prev
        _, act = _conv_act(ext, xbc_refs, cw_ref[...], cb_ref[...], c)
        (x0, x1), (b0, b1), (c0, c1) = CONV_COLS
        y, new = _ssd_chunk(act[:, x0:x1], act[:, b0:b1], act[:, c0:c1], dt_ref[...], prev, par_ref[...], c,
                            tri_ref[...], e64_ref[...], e128_ref[...])
        y_ref[...] = y
        state[...] = new

    const = lambda a: pl.BlockSpec(a.shape, lambda c: (0,) * a.ndim)
    consts = (par, _tri()) + _expanders() + (conv_w, conv_b)
    return carrier_call(
        name, body, (nc,),
        _conv_specs(lambda c: c) + [pl.BlockSpec((CHUNK, LANES), lambda c: (c, dt_block))] + [const(a) for a in consts],
        [pl.BlockSpec((CHUNK, SSD_D_INNER), lambda c: (c, 0)),
         pl.BlockSpec((None, SSD_STATE, SSD_D_INNER), lambda c: (c, 0, 0))],
        [jax.ShapeDtypeStruct((rows, SSD_D_INNER), F32),
         jax.ShapeDtypeStruct((nc, SSD_STATE, SSD_D_INNER), F32)],
        [pltpu.VMEM((SSD_STATE, SSD_D_INNER), F32), pltpu.VMEM((CHUNK + CONV_HALO, CONV_W), F32)],
        (zx,) * 7 + consts, comm)


def carrier_call(name, body, grid, in_specs, out_specs, out_shape, scratch_shapes, args, comm):
    body, in_specs, out_specs, out_shape, scratch_shapes, extra, n_own = with_comm(
        comm, body, grid, in_specs, out_specs, out_shape, scratch_shapes)
    res = pl.pallas_call(
        body, name=name, grid=grid, in_specs=in_specs, out_specs=out_specs, out_shape=out_shape,
        scratch_shapes=scratch_shapes, compiler_params=_params(len(grid)))(*args, *extra)
    return res[:n_own], res[n_own:]


def ssd_bwd(name, dy, zx, dt_block, states, par, conv_w, conv_b, comm=None):
    rows = zx.shape[0]
    nc = rows // CHUNK

    def body(*refs):
        dy_ref, xbc_refs = refs[0], refs[1:7]
        dt_ref, st_ref, par_ref, tri_ref, e64_ref, e128_ref, cw_ref, cb_ref = refs[7:15]
        du_ref, ddt_ref, dpar_ref, dcw_ref, dcb_ref, dstate, ext, dext = refs[15:]
        s = pl.program_id(0)
        c = nc - 1 - s

        @pl.when(s == 0)
        def _():
            dstate[...] = jnp.zeros((SSD_STATE, SSD_D_INNER), F32)
            dext[pl.ds(CHUNK, CONV_HALO), :] = jnp.zeros((CONV_HALO, CONV_W), F32)

        _, act = _conv_act(ext, xbc_refs, cw_ref[...], cb_ref[...], c)
        (x0, x1), (b0, b1), (c0, c1) = CONV_COLS

        def f(xs, bm, cm, dtraw, prev, par_v):
            return _ssd_chunk(xs, bm, cm, dtraw, prev, par_v, c, tri_ref[...], e64_ref[...], e128_ref[...])

        _, vjp = jax.vjp(f, act[:, x0:x1], act[:, b0:b1], act[:, c0:c1], dt_ref[...], st_ref[...], par_ref[...])
        dxs, dbm, dcm, ddt, dprev, dpar = vjp((dy_ref[...], dstate[...]))
        ddt_ref[...] = ddt.astype(BF16)
        dstate[...] = dprev
        _acc_out(dpar_ref, dpar, s == 0)

        dacts = [dxs[:, q * CONV_BLK:(q + 1) * CONV_BLK] for q in range((x1 - x0) // CONV_BLK)]
        dacts += [dbm[:, q * CONV_BLK:(q + 1) * CONV_BLK] for q in range((b1 - b0) // CONV_BLK)]
        dacts += [dcm[:, q * CONV_BLK:(q + 1) * CONV_BLK] for q in range((c1 - c0) // CONV_BLK)]
        for q, dact in enumerate(dacts):
            cols = slice(q * CONV_BLK, (q + 1) * CONV_BLK)
            w = cw_ref[:, cols]
            taps = [ext[pl.ds(CONV_HALO - (SSD_CONV - 1) + k, CHUNK), cols] for k in range(SSD_CONV)]
            pre = cb_ref[:, cols]
            for k in range(SSD_CONV):
                pre = pre + w[k:k + 1, :] * taps[k]
            sg = jax.nn.sigmoid(pre)
            dpre = jnp.where(_row_mask(c, CHUNK), dact * (sg * (1.0 + pre * (1.0 - sg))), 0.0)
            dext[pl.ds(0, CHUNK), cols] = dpre
            du = jnp.zeros((CHUNK, CONV_BLK), F32)
            for k in range(SSD_CONV):
                du = du + w[k:k + 1, :] * dext[pl.ds(SSD_CONV - 1 - k, CHUNK), cols]
            du_ref[:, cols] = du.astype(BF16)
            _acc_out(dcb_ref.at[:, cols], _rows8(dpre), s == 0)
            for k in range(SSD_CONV):
                _acc_out(dcw_ref.at[pl.ds(k * SUBLANES, SUBLANES), cols], _rows8(dpre * taps[k]), s == 0)
            dext[pl.ds(CHUNK, CONV_HALO), cols] = dpre[0:CONV_HALO, :]

    rev = lambda w, b: pl.BlockSpec((CHUNK, w), lambda s: (nc - 1 - s, b))
    const = lambda a: pl.BlockSpec(a.shape, lambda s: (0,) * a.ndim)
    consts = (par, _tri()) + _expanders() + (conv_w, conv_b)
    return carrier_call(
        name, body, (nc,),
        [rev(SSD_D_INNER, 0)] + _conv_specs(lambda s: nc - 1 - s)
        + [rev(LANES, dt_block), pl.BlockSpec((None, SSD_STATE, SSD_D_INNER), lambda s: (nc - 1 - s, 0, 0))]
        + [const(a) for a in consts],
        [rev(CONV_W, 0), rev(LANES, 0), pl.BlockSpec((SUBLANES, LANES), lambda s: (0, 0)),
         pl.BlockSpec((SSD_CONV * SUBLANES, CONV_W), lambda s: (0, 0)),
         pl.BlockSpec((SUBLANES, CONV_W), lambda s: (0, 0))],
        [jax.ShapeDtypeStruct((rows, CONV_W), BF16),
         jax.ShapeDtypeStruct((rows, LANES), BF16),
         jax.ShapeDtypeStruct((SUBLANES, LANES), F32),
         jax.ShapeDtypeStruct((SSD_CONV * SUBLANES, CONV_W), F32),
         jax.ShapeDtypeStruct((SUBLANES, CONV_W), F32)],
        [pltpu.VMEM((SSD_STATE, SSD_D_INNER), F32), pltpu.VMEM((CHUNK + CONV_HALO, CONV_W), F32),
         pltpu.VMEM((CHUNK + CONV_HALO, CONV_W), F32)],
        (dy,) + (zx,) * 7 + (states,) + consts, comm)


GN_W = SSD_D_INNER // SSD_GROUPS


def _gated_norm(y, z, ng):
    g = y * _silu(z)
    outs = []
    for q in range(SSD_GROUPS):
        gs = g[:, q * GN_W:(q + 1) * GN_W]
        outs.append(gs * lax.rsqrt(jnp.mean(gs * gs, axis=-1, keepdims=True) + EPS))
    return jnp.concatenate(outs, axis=1) * ng


def ssd_layer_fwd(tag, h, ln_g, w, comm=None):
    zx, hn = norm_mm(f"ssd_in_{tag}", h, ln_g, w["w_in"], tn=896)
    dt_block = 3 * SSD_D_INNER // LANES
    (y, states), cres = ssd_fwd(f"ssd_scan_{tag}", zx, dt_block, w["par"], w["conv_w"], w["conv_b"], comm)
    out, gn = res_mm(
        f"ssd_out_{tag}",
        [(y, (None, SSD_D_INNER), lambda i, j, kk: (i, 0)), (zx, (None, SSD_D_INNER), lambda i, j, kk: (i, 0)),
         (w["norm"], (1, SSD_D_INNER), lambda i, j, kk: (0, 0))],
        lambda r: _gated_norm(r[0][...], r[1][...], r[2][...]).astype(BF16),
        SSD_D_INNER, w["w_out"], h, tn=512, save_dtype=BF16)
    return out, (h, hn, zx, y, states, gn), cres


def ssd_dwout(tag, dh, saved):
    return simple_wgrad(f"ssd_dwout_{tag}", saved[5], dh, t1=SSD_D_INNER, tn=dh.shape[1])


def ssd_layer_bwd(tag, dh, saved, ln_g, w, dw_out, comm=None):
    h, hn, zx, y, states, gn = saved
    rows, d = h.shape
    tm = _row_tile(rows)
    dt_block = 3 * SSD_D_INNER // LANES

    def epi_gate(acc, e_refs, o_refs, i, j):
        _, vjp = jax.vjp(_gated_norm, e_refs[0][...], e_refs[1][...], e_refs[2][...])
        dy, dz, dng = vjp(acc)
        o_refs[0][...] = dy
        o_refs[1][...] = dz.astype(BF16)
        row0 = lax.broadcasted_iota(jnp.int32, (SUBLANES, 1), 0) == 0
        _acc_out(o_refs[2], jnp.where(row0, dng, 0.0), i == 0)

    wo, wo_block, wo_imap = _lw(w["w_out"], (SSD_D_INNER, d), lambda i, j, kk: (0, 0))
    dy, dz, dnorm = fused_mm(
        f"ssd_dgate_{tag}", rows=rows, k=d, n=SSD_D_INNER, tm=tm, tn=SSD_D_INNER,
        a_ops=[(dh, (tm, d), lambda i, j, kk: (i, 0))], pro=lambda a, o, i: a[0][...].astype(BF16),
        w=wo, w_block=wo_block, w_imap=wo_imap, dot=_dot_wt,
        e_ops=[(y, (tm, SSD_D_INNER), lambda i, j, kk: (i, 0)), (zx, (tm, SSD_D_INNER), lambda i, j, kk: (i, 0)),
               (w["norm"], (1, SSD_D_INNER), lambda i, j, kk: (0, 0))],
        epi=epi_gate,
        outs=[((rows, SSD_D_INNER), F32, (tm, SSD_D_INNER), lambda i, j, kk: (i, 0)),
              ((rows, SSD_D_INNER), BF16, (tm, SSD_D_INNER), lambda i, j, kk: (i, 0)),
              ((SUBLANES, SSD_D_INNER), F32, (SUBLANES, SSD_D_INNER), lambda i, j, kk: (0, 0))])
    (dxbc, ddt, dpar, dcw, dcb), cres = ssd_bwd(f"ssd_dscan_{tag}", dy, zx, dt_block, states, w["par"],
                                                w["conv_w"], w["conv_b"], comm)
    dzx = jnp.concatenate([dz, dxbc, ddt], axis=1)
    k = dzx.shape[1]
    dw_in = simple_wgrad(f"ssd_dwin_{tag}", hn, dzx, t1=d, tn=896)
    dh_in, dln = rms_bwd_mm(
        f"ssd_dh_{tag}", [(dzx, (None, 896), lambda i, j, kk: (i, kk))], lambda r: r[0][...], k,
        w["w_in"], (d, 896), lambda i, j, kk: (0, kk), _dot_wt, h, ln_g, dh, tk=896,
        tm=704 if rows % 704 == 0 else None)
    grads = dict(w_in=dw_in, w_out=dw_out, conv_w=dcw, conv_b=dcb, par=dpar, norm=dnorm, ln=dln)
    return dh_in, grads, cres


HP = 2 * LANES
VP = 2 * MLA_V
N_PAIRS = MLA_HEADS // 2
ATT_SCALE = MLA_QK ** -0.5
LOG2E = float(np.log2(np.e))
LN2 = float(np.log(2.0))
ROT = MLA_ROPE // 2


def rope_tables(rows):
    inv = 1.0 / (ROPE_THETA ** (jnp.arange(0, MLA_ROPE, 2, dtype=F32) / MLA_ROPE))
    pos = jnp.arange(rows, dtype=F32) - PAD
    ang = pos[:, None] * inv[None, :]
    cos, sin = jnp.cos(ang), jnp.sin(ang)
    one = jnp.ones((rows, MLA_NOPE), F32)
    zero = jnp.zeros((rows, LANES - MLA_QK), F32)
    zn = jnp.zeros((rows, MLA_NOPE), F32)
    zr = jnp.zeros((rows, ROT), F32)
    cosf = jnp.concatenate([one, cos, cos, zero], axis=1)
    sina = jnp.concatenate([zn, -sin, zr, zero], axis=1)
    sinb = jnp.concatenate([zn, zr, sin, zero], axis=1)
    return cosf, sina, sinb


def _qk_norm_rope(x, g, cosf, sina, sinb):
    r = lax.rsqrt(jnp.sum(x * x, axis=-1, keepdims=True) * (1.0 / MLA_QK) + EPS)
    xn = x * r * g
    return xn * cosf + pltpu.roll(xn, LANES - ROT, 1) * sina + pltpu.roll(xn, ROT, 1) * sinb


def _qk_norm_rope_bwd(dout, x, g, cosf, sina, sinb):
    dxn = dout * cosf + pltpu.roll(dout * sina, ROT, 1) + pltpu.roll(dout * sinb, LANES - ROT, 1)
    r = lax.rsqrt(jnp.sum(x * x, axis=-1, keepdims=True) * (1.0 / MLA_QK) + EPS)
    xh = x * r
    t = dxn * g
    dx = r * (t - xh * (jnp.sum(t * xh, axis=-1, keepdims=True) * (1.0 / MLA_QK)))
    return dx, _rows8(dxn * xh)


def _rope_lanes():
    lane = lax.broadcasted_iota(jnp.int32, (1, LANES), 1)
    return jnp.logical_and(lane >= MLA_NOPE, lane < MLA_QK)


QW = MLA_HEADS * LANES
VW = MLA_HEADS * MLA_V


def qk_prep(name, qraw, kvraw, lat, kpe_block, qg, kg, tabs):
    rows = qraw.shape[0]
    tm = _row_tile(rows)

    def body(q_ref, k0_ref, k1_ref, v_ref, pe_ref, qg_ref, kg_ref, c_ref, sa_ref, sb_ref,
             qo_ref, ko_ref, kt_ref, vo_ref, vt_ref):
        tab = (c_ref[...], sa_ref[...], sb_ref[...])
        pe = pe_ref[...]
        for hd in range(MLA_HEADS):
            sl = slice(hd * LANES, (hd + 1) * LANES)
            qo_ref[:, sl] = _qk_norm_rope(q_ref[:, sl].astype(F32), qg_ref[...], *tab).astype(BF16)
            kr = k0_ref if hd < MLA_HEADS // 2 else k1_ref
            ks = slice((hd % (MLA_HEADS // 2)) * LANES, (hd % (MLA_HEADS // 2) + 1) * LANES)
            kk = _qk_norm_rope(kr[:, ks].astype(F32) + pe, kg_ref[...], *tab)
            ko_ref[:, sl] = kk.astype(BF16)
            kt_ref[sl, :] = kk.T.astype(BF16)
        vo_ref[...] = v_ref[...].astype(BF16)
        for c in range(VW // LANES):
            sl = slice(c * LANES, (c + 1) * LANES)
            vt_ref[sl, :] = v_ref[:, sl].astype(F32).T.astype(BF16)

    row = lambda w, b: pl.BlockSpec((tm, w), lambda i: (i, b))
    col = lambda w: pl.BlockSpec((w, tm), lambda i: (0, i))
    one = pl.BlockSpec((1, LANES), lambda i: (0, 0))
    return pl.pallas_call(
        body, name=name, grid=(rows // tm,),
        in_specs=[row(QW, 0), row(VW, 0), row(VW, 1), row(VW, 2), row(LANES, kpe_block), one, one,
                  row(LANES, 0), row(LANES, 0), row(LANES, 0)],
        out_specs=[row(QW, 0), row(QW, 0), col(QW), row(VW, 0), col(VW)],
        out_shape=[jax.ShapeDtypeStruct((rows, QW), BF16), jax.ShapeDtypeStruct((rows, QW), BF16),
                   jax.ShapeDtypeStruct((QW, rows), BF16), jax.ShapeDtypeStruct((rows, VW), BF16),
                   jax.ShapeDtypeStruct((VW, rows), BF16)],
        compiler_params=_params(1),
    )(qraw, kvraw, kvraw, kvraw, lat, qg, kg, *tabs)


def qk_prep_bwd(name, dq_t, dk, dv, qraw, kvraw, lat, kpe_block, qg, kg, tabs):
    rows = qraw.shape[0]
    tm = _row_tile(rows)

    def body(dq_ref, dk_ref, dv_ref, q_ref, k0_ref, k1_ref, pe_ref, qg_ref, kg_ref, c_ref, sa_ref, sb_ref,
             dqo_ref, dkvo_ref, dpe_ref, dqg_ref, dkg_ref):
        i = pl.program_id(0)
        tab = (c_ref[...], sa_ref[...], sb_ref[...])
        pe = pe_ref[...]
        dpe = jnp.zeros((tm, LANES), F32)
        dqg = jnp.zeros((SUBLANES, LANES), F32)
        dkg = jnp.zeros((SUBLANES, LANES), F32)
        for hd in range(MLA_HEADS):
            sl = slice(hd * LANES, (hd + 1) * LANES)
            dx, dg = _qk_norm_rope_bwd(dq_ref[sl, :].T, q_ref[:, sl].astype(F32), qg_ref[...], *tab)
            dqo_ref[:, sl] = dx.astype(BF16)
            dqg = dqg + dg
            kr = k0_ref if hd < MLA_HEADS // 2 else k1_ref
            ks = slice((hd % (MLA_HEADS // 2)) * LANES, (hd % (MLA_HEADS // 2) + 1) * LANES)
            dx, dg = _qk_norm_rope_bwd(dk_ref[:, sl], kr[:, ks].astype(F32) + pe, kg_ref[...], *tab)
            dkvo_ref[:, sl] = dx.astype(BF16)
            dpe = dpe + dx
            dkg = dkg + dg
        dkvo_ref[:, QW:QW + VW] = dv_ref[...].astype(BF16)
        dpe_ref[...] = jnp.where(_rope_lanes(), dpe, 0.0)
        _acc_out(dqg_ref, dqg, i == 0)
        _acc_out(dkg_ref, dkg, i == 0)

    row = lambda w, b: pl.BlockSpec((tm, w), lambda i: (i, b))
    one = pl.BlockSpec((1, LANES), lambda i: (0, 0))
    acc = pl.BlockSpec((SUBLANES, LANES), lambda i: (0, 0))
    return pl.pallas_call(
        body, name=name, grid=(rows // tm,),
        in_specs=[pl.BlockSpec((QW, tm), lambda i: (0, i)), row(QW, 0), row(VW, 0), row(QW, 0), row(VW, 0), row(VW, 1),
                  row(LANES, kpe_block), one, one, row(LANES, 0), row(LANES, 0), row(LANES, 0)],
        out_specs=[row(QW, 0), row(QW + VW, 0), row(LANES, 0), acc, acc],
        out_shape=[jax.ShapeDtypeStruct((rows, QW), BF16), jax.ShapeDtypeStruct((rows, QW + VW), BF16),
                   jax.ShapeDtypeStruct((rows, LANES), F32),
                   jax.ShapeDtypeStruct((SUBLANES, LANES), F32), jax.ShapeDtypeStruct((SUBLANES, LANES), F32)],
        compiler_params=_params(1),
    )(dq_t, dk, dv, qraw, kvraw, kvraw, lat, qg, kg, *tabs)


def _att_mask_t(qb, kb, bt):
    kpos = kb * bt + lax.broadcasted_iota(jnp.int32, (bt, bt), 0)
    qpos = qb * bt + lax.broadcasted_iota(jnp.int32, (bt, bt), 1)
    return jnp.logical_and(kpos <= qpos, jnp.logical_or(kpos >= PAD, qpos < PAD))


def attn_fwd(name, q, k, vt, comm=None):
    rows = q.shape[0]
    bt = _row_tile(rows)
    nb = rows // bt
    assert bt >= CHUNK

    def body(q_ref, k_ref, vt_ref, o_ref, lse_ref):
        qi = pl.program_id(1)
        lse_ref[...] = jnp.zeros((SUBLANES, bt), F32)

        def scores(kb):
            r0 = pl.multiple_of(kb * bt, LANES)
            return tuple(lax.dot_general(k_ref[pl.ds(r0, bt), hh * LANES:(hh + 1) * LANES],
                                         q_ref[:, hh * LANES:(hh + 1) * LANES], (NT, ((), ())),
                                         preferred_element_type=F32) for hh in range(2))

        def tile(kb, carry, s_pair, masked):
            r0 = pl.multiple_of(kb * bt, LANES)
            new = []
            for hh in range(2):
                m, l, acc = carry[3 * hh:3 * hh + 3]
                vs = slice(hh * MLA_V, (hh + 1) * MLA_V)
                s = s_pair[hh] * (ATT_SCALE * LOG2E)
                if masked:
                    s = jnp.where(_att_mask_t(qi, kb, bt), s, NEG)
                m_new = jnp.maximum(m, jnp.max(s, axis=0, keepdims=True))
                alpha = jnp.exp2(m - m_new)
                p = jnp.exp2(s - m_new)
                l = alpha * l + jnp.sum(p, axis=0, keepdims=True)
                acc = alpha * acc + jnp.dot(vt_ref[vs, pl.ds(r0, bt)], p.astype(BF16), preferred_element_type=F32)
                new += [m_new, l, acc]
            return tuple(new)

        init = (jnp.full((1, bt), NEG, F32), jnp.zeros((1, bt), F32), jnp.zeros((MLA_V, bt), F32)) * 2
        s_next = scores(jnp.minimum(1, qi))
        carry = tile(0, init, scores(0), True)

        def rest(args):
            def mid(kb, state):
                carry, s_cur = state
                s_after = scores(kb + 1)
                return tile(kb, carry, s_cur, False), s_after

            carry, s_last = lax.fori_loop(1, qi, mid, args)
            return tile(qi, carry, s_last, True)

        carry = lax.cond(qi > 0, rest, lambda args: args[0], (carry, s_next))
        for hh in range(2):
            m, l, acc = carry[3 * hh:3 * hh + 3]
            o_ref[hh * MLA_V:(hh + 1) * MLA_V, :] = acc / l
            lse_ref[hh:hh + 1, :] = m * LN2 + jnp.log(l)

    return carrier_call(
        name, body, (N_PAIRS, nb),
        [pl.BlockSpec((bt, HP), lambda p, i: (i, p)),
         pl.BlockSpec((rows, HP), lambda p, i: (0, p)),
         pl.BlockSpec((VP, rows), lambda p, i: (p, 0))],
        [pl.BlockSpec((VP, bt), lambda p, i: (p, i)),
         pl.BlockSpec((None, SUBLANES, bt), lambda p, i: (p, 0, i))],
        [jax.ShapeDtypeStruct((VW, rows), F32), jax.ShapeDtypeStruct((N_PAIRS, SUBLANES, rows), F32)],
        [], (q, k, vt), comm)


def attn_bwd(name, q, k, kt, v, do_t, lse, delta, comm=None):
    rows = q.shape[0]
    bt = _row_tile(rows)
    nb = rows // bt

    def body(q_ref, k_ref, kt_ref, v_ref, do_ref, lse_ref, dl_ref, dq_ref, dk_ref, dv_ref, dk_scr, dv_scr):
        ki = pl.program_id(1)

        @pl.when(ki == 0)
        def _():
            dq_ref[...] = jnp.zeros((HP, rows), F32)

        dk_scr[...] = jnp.zeros((bt, HP), F32)
        dv_scr[...] = jnp.zeros((bt, VP), F32)

        def tile(qb, masked):
            c0 = pl.multiple_of(qb * bt, LANES)
            for hh in range(2):
                qs = slice(hh * LANES, (hh + 1) * LANES)
                vs = slice(hh * MLA_V, (hh + 1) * MLA_V)
                qv = q_ref[pl.ds(c0, bt), qs]
                dov = do_ref[vs, pl.ds(c0, bt)]
                lse = lse_ref[hh:hh + 1, pl.ds(c0, bt)]
                dl = dl_ref[hh:hh + 1, pl.ds(c0, bt)]
                s = lax.dot_general(k_ref[:, qs], qv, (NT, ((), ())), preferred_element_type=F32) * ATT_SCALE
                p = jnp.exp(s - lse)
                if masked:
                    p = jnp.where(_att_mask_t(qb, ki, bt), p, 0.0)
                dp = jnp.dot(v_ref[:, vs], dov, preferred_element_type=F32)
                ds = (p * (dp - dl) * ATT_SCALE).astype(BF16)
                dv_scr[:, vs] += lax.dot_general(p.astype(BF16), dov, (NT, ((), ())), preferred_element_type=F32)
                dk_scr[:, qs] += jnp.dot(ds, qv, preferred_element_type=F32)
                dq_ref[qs, pl.ds(c0, bt)] += jnp.dot(kt_ref[qs, :], ds, preferred_element_type=F32)

        @pl.when(ki == 0)
        def _():
            def every(qb, carry):
                tile(qb, True)
                return carry

            lax.fori_loop(0, nb, every, 0)

        @pl.when(ki > 0)
        def _():
            tile(ki, True)

            def later(qb, carry):
                tile(qb, False)
                return carry

            lax.fori_loop(ki + 1, nb, later, 0)

        dk_ref[...] = dk_scr[...]
        dv_ref[...] = dv_scr[...]

    stat = pl.BlockSpec((None, SUBLANES, rows), lambda p, i: (p, 0, 0))
    return carrier_call(
        name, body, (N_PAIRS, nb),
        [pl.BlockSpec((rows, HP), lambda p, i: (0, p)),
         pl.BlockSpec((bt, HP), lambda p, i: (i, p)),
         pl.BlockSpec((HP, bt), lambda p, i: (p, i)),
         pl.BlockSpec((bt, VP), lambda p, i: (i, p)),
         pl.BlockSpec((VP, rows), lambda p, i: (p, 0)),
         stat, stat],
        [pl.BlockSpec((HP, rows), lambda p, i: (p, 0)),
         pl.BlockSpec((bt, HP), lambda p, i: (i, p)),
         pl.BlockSpec((bt, VP), lambda p, i: (i, p))],
        [jax.ShapeDtypeStruct((QW, rows), F32), jax.ShapeDtypeStruct((rows, QW), F32),
         jax.ShapeDtypeStruct((rows, VW), F32)],
        [pltpu.VMEM((bt, HP), F32), pltpu.VMEM((bt, VP), F32)],
        (q, k, kt, v, do_t, lse, delta), comm)


def _dot_cast_w(a, w_ref):
    return jnp.dot(a, w_ref[...].astype(BF16), preferred_element_type=F32)


def _dot_cast_wt(a, w_ref):
    return lax.dot_general(a, w_ref[...].astype(BF16), (NT, ((), ())), preferred_element_type=F32)


LAT_W = 768
KPE_BLOCK = MLA_Q_RANK // LANES
KV_BLOCK = (MLA_Q_RANK + LANES) // MLA_KV_RANK


def mla_layer_fwd(tag, h, ln_g, w, tabs, comm=None):
    lat, hn = norm_mm(f"mla_in_{tag}", h, ln_g, w["w_in"], tn=LAT_W)
    qraw, qn = norm_mm(f"mla_q_{tag}", lat, w["q_a"], w["w_q"], tn=512, k_cols=MLA_Q_RANK, col_block=0,
                       out_dtype=BF16)
    kvraw, kvn = norm_mm(f"mla_kv_{tag}", lat, w["kv_a"], w["w_kv"], tn=512, k_cols=MLA_KV_RANK, col_block=KV_BLOCK,
                         out_dtype=BF16)
    q, k, kt, v, vt = qk_prep(f"mla_prep_{tag}", qraw, kvraw, lat, KPE_BLOCK, w["q_norm"], w["k_norm"], tabs)
    (o_t, lse), cres = attn_fwd(f"mla_attn_{tag}", q, k, vt, comm)
    out = res_mm(f"mla_out_{tag}", [(o_t, (VW, None), lambda i, j, kk: (0, i))],
                 lambda r: r[0][...].T.astype(BF16), VW, w["w_out"], h, tn=512, tm=_big_tile(h.shape[0]))
    return out, (h, hn, lat, qn, kvn, qraw, kvraw, q, k, kt, v, o_t, lse), cres


def mla_dwout(tag, dh, saved):
    o_t = saved[11]
    rows, d = dh.shape
    tm = _row_tile(rows)

    def epi_set(acc, e_refs, o_refs, i, j):
        o_refs[0][...] = acc.astype(BF16)

    return fused_mm(
        f"mla_dwout_{tag}", rows=VW, k=rows, n=d, tm=512, tn=d, tk=tm,
        a_ops=[(o_t, (512, tm), lambda i, j, kk: (i, kk))], pro=lambda a, o_, i: a[0][...].astype(BF16),
        w=dh, w_block=(tm, d), w_imap=lambda i, j, kk: (kk, 0), dot=_dot_cast_w, epi=epi_set,
        outs=[((VW, d), BF16, (512, d), lambda i, j, kk: (i, 0))])[0]


def mla_layer_bwd(tag, dh, saved, ln_g, w, tabs, dw_out, comm=None):
    h, hn, lat, qn, kvn, qraw, kvraw, q, k, kt, v, o_t, lse = saved
    rows, d = h.shape
    tm = _row_tile(rows)

    def epi_do(acc, e_refs, o_refs, i, j):
        o_refs[0][...] = acc.astype(BF16)
        prod = acc * e_refs[0][...]
        o_refs[1][...] = jnp.zeros((N_PAIRS, SUBLANES, tm), F32)
        for hd in range(MLA_HEADS):
            o_refs[1][hd // 2, hd % 2:hd % 2 + 1, :] = jnp.sum(prod[hd * MLA_V:(hd + 1) * MLA_V, :], axis=0,
                                                               keepdims=True)

    wo, wo_block, wo_imap = _lw(w["w_out"], (VW, d), lambda i, j, kk: (0, 0))
    do_t, delta = fused_mm(
        f"mla_do_{tag}", rows=VW, k=d, n=rows, tm=VW, tn=tm,
        a_ops=[(wo, wo_block, wo_imap)], pro=lambda a, o_, i: a[0][...],
        w=dh, w_block=(tm, d), w_imap=lambda i, j, kk: (j, 0), dot=_dot_cast_wt,
        e_ops=[(o_t, (VW, tm), lambda i, j, kk: (0, j))], epi=epi_do,
        outs=[((VW, rows), BF16, (VW, tm), lambda i, j, kk: (0, j)),
              ((N_PAIRS, SUBLANES, rows), F32, (N_PAIRS, SUBLANES, tm), lambda i, j, kk: (0, 0, j))])
    (dq_t, dk, dv), cres = attn_bwd(f"mla_dattn_{tag}", q, k, kt, v, do_t, lse, delta, comm)
    dqraw, dkvraw, dpe, dqg, dkg = qk_prep_bwd(f"mla_dprep_{tag}", dq_t, dk, dv, qraw, kvraw, lat, KPE_BLOCK,
                                               w["q_norm"], w["k_norm"], tabs)
    dw_q = simple_wgrad(f"mla_dwq_{tag}", qn, dqraw, t1=MLA_Q_RANK, tn=512)
    dqlat, dqa = rms_bwd_mm(
        f"mla_dqlat_{tag}", [(dqraw, (None, QW), lambda i, j, kk: (i, 0))], lambda r: r[0][...], QW,
        w["w_q"], (MLA_Q_RANK, QW), lambda i, j, kk: (0, 0), _dot_wt, lat, w["q_a"], None,
        h_cols=MLA_Q_RANK, h_col_block=0, add_dh=False)
    dw_kv = simple_wgrad(f"mla_dwkv_{tag}", kvn, dkvraw, t1=MLA_KV_RANK, tn=512)
    dkvlat, dkva = rms_bwd_mm(
        f"mla_dkvlat_{tag}", [(dkvraw, (None, QW + VW), lambda i, j, kk: (i, 0))], lambda r: r[0][...], QW + VW,
        w["w_kv"], (MLA_KV_RANK, QW + VW), lambda i, j, kk: (0, 0), _dot_wt, lat, w["kv_a"], None,
        h_cols=MLA_KV_RANK, h_col_block=KV_BLOCK, add_dh=False)
    dlat = jnp.concatenate([dqlat.astype(BF16), dpe.astype(BF16), dkvlat.astype(BF16)], axis=1)
    dw_in = simple_wgrad(f"mla_dwin_{tag}", hn, dlat, t1=512, tn=LAT_W)
    dh_in, dln = rms_bwd_mm(
        f"mla_dh_{tag}", [(dlat, (None, LAT_W), lambda i, j, kk: (i, 0))], lambda r: r[0][...], LAT_W,
        w["w_in"], (d, LAT_W), lambda i, j, kk: (0, 0), _dot_wt, h, ln_g, dh)
    grads = dict(w_in=dw_in, w_q=dw_q, w_kv=dw_kv, w_out=dw_out, q_a=dqa, kv_a=dkva, q_norm=dqg, k_norm=dkg, ln=dln)
    return dh_in, grads, cres


def loss_head(h, target):
    rows, d = h.shape
    nb = rows // CHUNK

    def body(h_ref, t_ref, l_ref, dh_ref):
        i = pl.program_id(0)
        err = jnp.where(i > 0, h_ref[...] - t_ref[...], 0.0)
        dh_ref[...] = err * (1.0 / d)
        _acc_out(l_ref, _rows8(err * err) * (0.5 / d), i == 0)

    return pl.pallas_call(
        body, name="loss_head", grid=(nb,),
        in_specs=[pl.BlockSpec((CHUNK, d), lambda i: (i, 0)),
                  pl.BlockSpec((CHUNK, d), lambda i: (jnp.maximum(i - 1, 0), 0))],
        out_specs=[pl.BlockSpec((SUBLANES, d), lambda i: (0, 0)), pl.BlockSpec((CHUNK, d), lambda i: (i, 0))],
        out_shape=[jax.ShapeDtypeStruct((SUBLANES, d), F32), jax.ShapeDtypeStruct((rows, d), F32)],
        compiler_params=_params(1),
    )(h, target)


def _adamw(w, g, m, v):
    m = ADAM_B1 * m + (1.0 - ADAM_B1) * g
    v = ADAM_B2 * v + (1.0 - ADAM_B2) * jnp.square(g)
    m_hat = m / (1.0 - ADAM_B1 ** ADAM_STEP)
    v_hat = v / (1.0 - ADAM_B2 ** ADAM_STEP)
    delta = -ADAM_LR * (m_hat / (jnp.sqrt(v_hat) + ADAM_EPS) + ADAM_WD * w)
    return delta, m, v


def reduce_adamw(name, recvs, w, m, v):
    nl, r, c = w.shape
    tr = 128 if r % 128 == 0 else r
    nr = r // tr

    def body(*refs):
        r_refs = refs[:nl]
        w_ref, m_ref, v_ref, g_ref, d_ref, mo_ref, vo_ref = refs[nl:]
        layer = pl.program_id(0)
        for l in range(nl):
            @pl.when(layer == l)
            def _(l=l):
                g = r_refs[l][0].astype(F32)
                for s in range(1, N_DEV):
                    g = g + r_refs[l][s].astype(F32)
                g_ref[...] = g
                d_ref[...], mo_ref[...], vo_ref[...] = _adamw(w_ref[...], g, m_ref[...], v_ref[...])

    def recv_spec(l):
        return pl.BlockSpec((N_DEV, tr, c),
                            lambda y, i: (0, jnp.where(y == l, i, jnp.where(y < l, 0, nr - 1)), 0))

    blk = pl.BlockSpec((None, tr, c), lambda y, i: (y, i, 0))
    return pl.pallas_call(
        body, name=name, grid=(nl, nr),
        in_specs=[recv_spec(l) for l in range(nl)] + [blk, blk, blk],
        out_specs=[blk] * 4, out_shape=[jax.ShapeDtypeStruct((nl, r, c), F32)] * 4,
        compiler_params=_params(2),
    )(*recvs, w, m, v)


def small_reduce(recv):
    def body(r_ref, o_ref):
        g = r_ref[0]
        for s in range(1, N_DEV):
            g = g + r_ref[s]
        o_ref[...] = g

    return pl.pallas_call(body, name="small_reduce", out_shape=jax.ShapeDtypeStruct(recv.shape[1:], F32))(recv)


def small_adamw(w, g, m, v):
    def body(w_ref, g_ref, m_ref, v_ref, d_ref, mo_ref, vo_ref):
        d_ref[...], mo_ref[...], vo_ref[...] = _adamw(w_ref[...], g_ref[...], m_ref[...], v_ref[...])

    return pl.pallas_call(body, name="small_adamw", out_shape=[jax.ShapeDtypeStruct(w.shape, F32)] * 3)(w, g, m, v)


def _pack(parts):
    flat, meta, off = [], [], 0
    for p in parts:
        n = int(np.prod(p.shape))
        flat.append(p.reshape(-1).astype(F32))
        meta.append((off, p.shape))
        off += n
    total = -(-off // (SUBLANES * LANES)) * (SUBLANES * LANES)
    flat.append(jnp.zeros((total - off,), F32))
    return jnp.concatenate(flat).reshape(total // LANES, LANES), meta


def _unpack(packed, meta):
    flat = packed.reshape(-1)
    return [flat[off:off + int(np.prod(shape))].reshape(shape) for off, shape in meta]


MESH = pl.DeviceIdType.MESH
N_PEERS = N_DEV - 1


def _me():
    return lax.axis_index("x"), lax.axis_index("y"), lax.axis_index("c")


def _peer(k):
    x, y, c = _me()
    return (1 - x if k & 4 else x, 1 - y if k & 2 else y, 1 - c if k & 1 else c)


def _dev_index(pos):
    return 4 * pos[0] + 2 * pos[1] + pos[2]


RELAYED = (3, 5, 7)


def make_comm(items):
    n = len(items)

    def part(ref, a, idx):
        rows = items[a][1]
        if rows == "all":
            return ref
        return ref.at[idx] if rows is None else ref.at[pl.ds(idx * rows, rows)]

    def part_shape(a):
        arr, rows = items[a]
        if rows == "all":
            return arr.shape
        return arr.shape[1:] if rows is None else (rows,) + arr.shape[1:]

    def run(phase, ins, outs, send_sems, recv_sems, local_sems):
        me = _dev_index(_me())

        def copy(a, k, src, slot, to):
            return pltpu.make_async_remote_copy(
                src_ref=src, dst_ref=outs[a].at[slot], send_sem=send_sems.at[a, k - 1],
                recv_sem=recv_sems.at[a, k - 1], device_id=to, device_id_type=MESH)

        for a in range(n):
            gather = items[a][1] == "all"
            local = pltpu.make_async_copy(part(ins[a], a, me), outs[a].at[me], local_sems.at[a])
            if phase == "start":
                local.start()
            for k in range(1, N_DEV):
                if gather and k in RELAYED:
                    continue
                peer = _peer(k)
                if phase == "start":
                    copy(a, k, part(ins[a], a, _dev_index(peer)), me, peer).start()
                else:
                    cp = copy(a, k, part(ins[a], a, me), _dev_index(peer), peer)
                    cp.wait_recv()
                    cp.wait_send()
            if phase == "wait":
                local.wait()
                if gather:
                    sibling = _peer(1)
                    relays = []
                    for k in RELAYED:
                        origin = _dev_index(_peer(k - 1))
                        relays.append(copy(a, k, outs[a].at[origin], origin, sibling))
                        relays[-1].start()
                    for k in RELAYED:
                        copy(a, k, ins[a], _dev_index(_peer(k)), sibling).wait_recv()
                    for cp in relays:
                        cp.wait_send()

    return dict(
        ins=[it[0] for it in items],
        outs=[jax.ShapeDtypeStruct((N_DEV,) + part_shape(a), items[a][0].dtype) for a in range(n)],
        sems=[pltpu.SemaphoreType.DMA((n, N_PEERS)), pltpu.SemaphoreType.DMA((n, N_PEERS)),
              pltpu.SemaphoreType.DMA((n,))],
        run=run)


ANY_SPEC = pl.BlockSpec(memory_space=pl.ANY)


def comm_call(name, comm):
    n, no = len(comm["ins"]), len(comm["outs"])

    def body(*refs):
        comm["run"]("start", refs[:n], refs[n:n + no], *refs[n + no:])
        comm["run"]("wait", refs[:n], refs[n:n + no], *refs[n + no:])

    return pl.pallas_call(
        body, name=name, in_specs=[ANY_SPEC] * n, out_specs=[ANY_SPEC] * no, out_shape=comm["outs"],
        scratch_shapes=comm["sems"])(*comm["ins"])


def with_comm(comm, body, grid, in_specs, out_specs, out_shape, scratch_shapes):
    if comm is None:
        return body, in_specs, out_specs, out_shape, scratch_shapes, [], len(out_shape)
    n_in, n_out, n_scr = len(in_specs), len(out_shape), len(scratch_shapes)
    ci, co = len(comm["ins"]), len(comm["outs"])

    def wrapped(*refs):
        ins, cins = refs[:n_in], refs[n_in:n_in + ci]
        outs = refs[n_in + ci:n_in + ci + n_out]
        couts = refs[n_in + ci + n_out:n_in + ci + n_out + co]
        rest = refs[n_in + ci + n_out + co:]
        scr, sems = rest[:n_scr], rest[n_scr:]
        first = functools.reduce(jnp.logical_and, [pl.program_id(a) == 0 for a in range(len(grid))])
        last = functools.reduce(jnp.logical_and, [pl.program_id(a) == grid[a] - 1 for a in range(len(grid))])

        @pl.when(first)
        def _():
            comm["run"]("start", cins, couts, *sems)

        body(*ins, *outs, *scr)

        @pl.when(last)
        def _():
            comm["run"]("wait", cins, couts, *sems)

    return (wrapped, list(in_specs) + [ANY_SPEC] * ci, list(out_specs) + [ANY_SPEC] * co,
            list(out_shape) + list(comm["outs"]), list(scratch_shapes) + list(comm["sems"]), list(comm["ins"]), n_out)


WEIGHTS = ['meta_tokens', 'ln_mix', 'ln_mlp', 'ssd_w_in', 'ssd_conv_w', 'ssd_conv_b', 'ssd_dt_bias', 'ssd_a_log',
           'ssd_d', 'ssd_norm', 'ssd_w_out', 'mla_w_in', 'mla_q_a_norm', 'mla_w_q_b', 'mla_kv_a_norm', 'mla_w_kv_b',
           'mla_q_norm', 'mla_k_norm', 'mla_w_out', 'mlp_w_up', 'mlp_w_down']
BIG = ['ssd_w_in', 'ssd_w_out', 'mla_w_in', 'mla_w_q_b', 'mla_w_kv_b', 'mla_w_out', 'mlp_w_up', 'mlp_w_down']
SMALL_SHARDED = ['meta_tokens', 'ssd_conv_w', 'mla_q_a_norm', 'mla_kv_a_norm']
SMALL_REPL = ['ln_mix', 'ln_mlp', 'ssd_conv_b', 'ssd_dt_bias', 'ssd_a_log', 'ssd_d', 'ssd_norm', 'mla_q_norm',
              'mla_k_norm']
SMALL = SMALL_REPL + SMALL_SHARDED
SSD_IN_PAD = 6272
SSD_IN_TN = 896
MLA_IN = MLA_Q_RANK + MLA_KV_RANK + MLA_ROPE


def _pad_last(v, n):
    return jnp.pad(v, [(0, 0)] * (v.ndim - 1) + [(0, n - v.shape[-1])])


SSD_BIG = ['ssd_w_in', 'ssd_w_out']
MLA_BIG = ['mla_w_in', 'mla_w_q_b', 'mla_w_kv_b', 'mla_w_out']
MLP_BIG = ['mlp_w_up', 'mlp_w_down']


def _mix_big(i):
    return [(n, i // 2) for n in (SSD_BIG if i % 2 == 0 else MLA_BIG)]


def _mlp_big(i):
    return [(n, i) for n in MLP_BIG]


def _mix_weights(i, gw, W, full):
    j = i // 2
    d = W['ln_mix'].shape[-1]
    if i % 2 == 0:
        wi = gw[('ssd_w_in', j)].transpose(1, 0, 2).reshape(d, -1)
        par = jnp.concatenate([_pad_last(W[n][j][None], LANES) for n in ('ssd_dt_bias', 'ssd_a_log', 'ssd_d')]
                              + [jnp.zeros((SUBLANES - 3, LANES), F32)])
        return dict(w_in=_pad_last(wi, SSD_IN_PAD), conv_w=full['ssd_conv_w'][j], conv_b=W['ssd_conv_b'][j][None],
                    par=par, norm=W['ssd_norm'][j][None], w_out=gw[('ssd_w_out', j)].reshape(SSD_D_INNER, d))
    wi = gw[('mla_w_in', j)].reshape(d, MLA_IN)
    kpe = jnp.pad(wi[:, MLA_Q_RANK + MLA_KV_RANK:], ((0, 0), (MLA_NOPE, LANES - MLA_QK)))
    wq = gw[('mla_w_q_b', j)].transpose(1, 0, 2).reshape(MLA_Q_RANK, MLA_HEADS, MLA_QK)
    wkv = gw[('mla_w_kv_b', j)].transpose(1, 0, 2).reshape(MLA_KV_RANK, MLA_HEADS, MLA_NOPE + MLA_V)
    return dict(
        w_in=jnp.concatenate([wi[:, :MLA_Q_RANK], kpe, wi[:, MLA_Q_RANK:MLA_Q_RANK + MLA_KV_RANK]], axis=1),
        w_q=_pad_last(wq, LANES).reshape(MLA_Q_RANK, QW),
        w_kv=jnp.concatenate([_pad_last(wkv[..., :MLA_NOPE], LANES).reshape(MLA_KV_RANK, QW),
                              wkv[..., MLA_NOPE:].reshape(MLA_KV_RANK, VW)], axis=1),
        w_out=gw[('mla_w_out', j)].reshape(VW, d), q_a=full['mla_q_a_norm'][j][None],
        kv_a=full['mla_kv_a_norm'][j][None],
        q_norm=_pad_last(W['mla_q_norm'][j][None], LANES), k_norm=_pad_last(W['mla_k_norm'][j][None], LANES))


def _step(x, target, W, M, V):
    d = x.shape[-1]
    me = _dev_index(_me())
    depth = W['ln_mix'].shape[0]

    def gather_keys(i):
        return _mlp_big(i) + (_mix_big(i + 1) if i + 1 < depth else [])

    def gather_items(keys):
        return [(W[n][l].astype(BF16), "all") for n, l in keys]

    small_pack, small_meta = _pack([W[n] for n in SMALL_SHARDED])
    got = comm_call("gather_0", make_comm(gather_items(_mix_big(0)) + [(small_pack, "all")]))
    per_dev = [_unpack(got[-1][s], small_meta) for s in range(N_DEV)]
    full = {n: jnp.concatenate([per_dev[s][i] for s in range(N_DEV)], axis=-1) for i, n in enumerate(SMALL_SHARDED)}
    gw = dict(zip(_mix_big(0), got))

    h = jnp.concatenate([jnp.zeros((PAD, d), F32), full['meta_tokens'], x], axis=0)
    rows = h.shape[0]
    tabs = rope_tables(rows)
    saved, weights = [], []
    for i in range(depth):
        comm = make_comm(gather_items(gather_keys(i)))
        mix = _mix_weights(i, gw, W, full)
        if i % 2 == 0:
            h, s_mix, got = ssd_layer_fwd(f"{i}", h, W['ln_mix'][i][None], mix, comm)
        else:
            h, s_mix, got = mla_layer_fwd(f"{i}", h, W['ln_mix'][i][None], mix, tabs, comm)
        gw.update(zip(gather_keys(i), got))
        up, down = gw[('mlp_w_up', i)], gw[('mlp_w_down', i)].reshape(-1, d)
        h, s_mlp = mlp_fwd(f"{i}", h, W['ln_mlp'][i][None], up, down)
        saved.append((s_mix, s_mlp))
        weights.append((mix, up, down))
    loss_part, dh = loss_head(h, target)
    loss = lax.psum(jnp.sum(loss_part), ("x", "y", "c"))

    recv = {}
    pending = []
    small = {n: [None] * W[n].shape[0] for n in SMALL if n != 'meta_tokens'}
    for i in reversed(range(depth)):
        j = i // 2
        s_mix, s_mlp = saved[i]
        mix, up, down = weights[i]
        dh, dw_up, dw_down, dg = mlp_bwd(f"{i}", dh, s_mlp, W['ln_mlp'][i][None], up, down)
        small['ln_mlp'][i] = dg.sum(0)
        pending += list(zip(_mlp_big(i), [(dw_up, None), (dw_down, down.shape[0] // N_DEV)]))
        dw_out = (ssd_dwout if i % 2 == 0 else mla_dwout)(f"{i}", dh, s_mix)
        pending.append((_mix_big(i)[-1], (dw_out, dw_out.shape[0] // N_DEV)))
        comm = make_comm([it for _, it in pending])
        sends = []
        if i % 2 == 0:
            dh, g, got = ssd_layer_bwd(f"{i}", dh, s_mix, W['ln_mix'][i][None], mix, dw_out, comm)
            n_in = W['ssd_w_in'].shape[-1]
            sends.append((g['w_in'][:, :N_DEV * n_in].reshape(d, N_DEV, n_in).transpose(1, 0, 2), None))
            small['ssd_conv_w'][j] = g['conv_w'].reshape(SSD_CONV, SUBLANES, -1).sum(1)
            small['ssd_conv_b'][j] = g['conv_b'].sum(0)
            small['ssd_dt_bias'][j] = g['par'][0, :SSD_HEADS]
            small['ssd_a_log'][j] = g['par'][1, :SSD_HEADS]
            small['ssd_d'][j] = g['par'][2, :SSD_HEADS]
            small['ssd_norm'][j] = g['norm'].sum(0)
        else:
            dh, g, got = mla_layer_bwd(f"{i}", dh, s_mix, W['ln_mix'][i][None], mix, tabs, dw_out, comm)
            gi = g['w_in']
            gi = jnp.concatenate([gi[:, :MLA_Q_RANK], gi[:, MLA_Q_RANK + LANES:],
                                  gi[:, MLA_Q_RANK + MLA_NOPE:MLA_Q_RANK + MLA_QK]], axis=1)
            sends.append((gi, d // N_DEV))
            gq = g['w_q'].reshape(MLA_Q_RANK, MLA_HEADS, LANES)[..., :MLA_QK]
            sends.append((gq.reshape(MLA_Q_RANK, N_DEV, -1).transpose(1, 0, 2), None))
            gkv = jnp.concatenate([g['w_kv'][:, :QW].reshape(MLA_KV_RANK, MLA_HEADS, LANES)[..., :MLA_NOPE],
                                   g['w_kv'][:, QW:].reshape(MLA_KV_RANK, MLA_HEADS, MLA_V)], axis=-1)
            sends.append((gkv.reshape(MLA_KV_RANK, N_DEV, -1).transpose(1, 0, 2), None))
            small['mla_q_a_norm'][j] = g['q_a'].sum(0)
            small['mla_kv_a_norm'][j] = g['kv_a'].sum(0)
            small['mla_q_norm'][j] = g['q_norm'].sum(0)[:MLA_QK]
            small['mla_k_norm'][j] = g['k_norm'].sum(0)[:MLA_QK]
        small['ln_mix'][i] = g['ln'].sum(0)
        recv.update({key: a for (key, _), a in zip(pending, got)})
        pending = list(zip(_mix_big(i)[:-1], sends))
    grad_x = dh[CHUNK:]
    small_full = {n: jnp.stack(v) for n, v in small.items()}
    small_full['meta_tokens'] = dh[PAD:CHUNK]

    gpack, gmeta = _pack([small_full[n] for n in SMALL])
    got = comm_call("exchange_0", make_comm([it for _, it in pending] + [(gpack, "all")]))
    recv.update({key: a for (key, _), a in zip(pending, got)})
    res = {}
    for n in BIG:
        res[n] = reduce_adamw(f"adamw_{n}", [recv[(n, l)] for l in range(W[n].shape[0])], W[n], M[n], V[n])
    gsum = dict(zip(SMALL, _unpack(small_reduce(got[-1]), gmeta)))
    for n in SMALL_SHARDED:
        wl = W[n].shape[-1]
        gsum[n] = lax.dynamic_slice_in_dim(gsum[n], me * wl, wl, axis=gsum[n].ndim - 1)
    wp, wmeta = _pack([W[n] for n in SMALL])
    gp, _ = _pack([gsum[n] for n in SMALL])
    mp, _ = _pack([M[n] for n in SMALL])
    vp, _ = _pack([V[n] for n in SMALL])
    upd = [_unpack(o, wmeta) for o in small_adamw(wp, gp, mp, vp)]
    for a, n in enumerate(SMALL):
        res[n] = [gsum[n], upd[0][a], upd[1][a], upd[2][a]]
    return (loss, grad_x[None]) + tuple(res[n][q] for q in range(4) for n in WEIGHTS)


def kernel(x, meta_tokens, ln_mix, ln_mlp, ssd_w_in, ssd_conv_w, ssd_conv_b, ssd_dt_bias, ssd_a_log, ssd_d, ssd_norm, ssd_w_out, mla_w_in, mla_q_a_norm, mla_w_q_b, mla_kv_a_norm, mla_w_kv_b, mla_q_norm, mla_k_norm, mla_w_out, mlp_w_up, mlp_w_down, loss_target, m_meta_tokens, m_ln_mix, m_ln_mlp, m_ssd_w_in, m_ssd_conv_w, m_ssd_conv_b, m_ssd_dt_bias, m_ssd_a_log, m_ssd_d, m_ssd_norm, m_ssd_w_out, m_mla_w_in, m_mla_q_a_norm, m_mla_w_q_b, m_mla_kv_a_norm, m_mla_w_kv_b, m_mla_q_norm, m_mla_k_norm, m_mla_w_out, m_mlp_w_up, m_mlp_w_down, v_meta_tokens, v_ln_mix, v_ln_mlp, v_ssd_w_in, v_ssd_conv_w, v_ssd_conv_b, v_ssd_dt_bias, v_ssd_a_log, v_ssd_d, v_ssd_norm, v_ssd_w_out, v_mla_w_in, v_mla_q_a_norm, v_mla_w_q_b, v_mla_kv_a_norm, v_mla_w_kv_b, v_mla_q_norm, v_mla_k_norm, v_mla_w_out, v_mlp_w_up, v_mlp_w_down):
    given = dict(locals())
    W = {n: given[n] for n in WEIGHTS}
    M = {n: given["m_" + n] for n in WEIGHTS}
    V = {n: given["v_" + n] for n in WEIGHTS}
    return _step(x[0], loss_target[0], W, M, V)
```

```python
import functools

import jax
import jax.numpy as jnp
import numpy as np
from jax import lax
from jax.experimental import pallas as pl
from jax.experimental.pallas import tpu as pltpu

F32 = jnp.float32
BF16 = jnp.bfloat16

EPS = 1e-6
N_META = 16
CHUNK = 128
PAD = CHUNK - N_META
SSD_HEAD_DIM = 64
SSD_HEADS = 32
SSD_GROUPS = 8
SSD_HPG = 4
SSD_STATE = 128
SSD_D_INNER = 2048
SSD_CONV = 4
MLA_HEADS = 16
MLA_NOPE = 64
MLA_ROPE = 32
MLA_V = 64
MLA_QK = 96
MLA_Q_RANK = 384
MLA_KV_RANK = 256
ROPE_THETA = 10000.0
LANES = 128
SUBLANES = 8
N_DEV = 8
VMEM_LIMIT = 56 * 1024 * 1024

ADAM_LR = 0.001
ADAM_B1 = 0.9
ADAM_B2 = 0.999
ADAM_EPS = 1e-08
ADAM_WD = 0.01
ADAM_STEP = 10

NEG = -1e30


def _row_tile(rows):
    return 384 if (rows % 384 == 0 and rows > 384) else 128


def _big_tile(rows):
    return 1408 if rows % 1408 == 0 else _row_tile(rows)


def _params(n_axes, vmem=VMEM_LIMIT):
    return pltpu.CompilerParams(dimension_semantics=("arbitrary",) * n_axes, vmem_limit_bytes=vmem)


def _dot(a, b, dims):
    return lax.dot_general(a.astype(BF16), b.astype(BF16), (dims, ((), ())), preferred_element_type=F32)


NN = ((1,), (0,))
NT = ((1,), (1,))
TN = ((0,), (0,))


@jax.custom_vjp
def bdot_nn(a, b):
    return _dot(a, b, NN)


@jax.custom_vjp
def bdot_nt(a, b):
    return _dot(a, b, NT)


@jax.custom_vjp
def bdot_tn(a, b):
    return _dot(a, b, TN)


bdot_nn.defvjp(lambda a, b: (_dot(a, b, NN), (a, b)),
               lambda r, g: (_dot(g, r[1], NT), _dot(r[0], g, TN)))
bdot_nt.defvjp(lambda a, b: (_dot(a, b, NT), (a, b)),
               lambda r, g: (_dot(g, r[1], NN), _dot(g, r[0], TN)))
bdot_tn.defvjp(lambda a, b: (_dot(a, b, TN), (a, b)),
               lambda r, g: (_dot(r[1], g, NT), _dot(r[0], g, NN)))


def _rows8(v):
    r, n = v.shape
    return v.reshape(r // SUBLANES, SUBLANES, n).sum(axis=0)


def _row_mask(i, tm):
    return (i * tm + lax.broadcasted_iota(jnp.int32, (tm, 1), 0)) >= PAD


def fused_mm(name, *, rows, k, n, tm, tn, tk=None, a_ops, pro, w, w_block, w_imap, dot, e_ops=(), epi, outs):
    tk = tk or k
    ni, nj, nk = rows // tm, n // tn, k // tk
    assert rows % tm == 0 and n % tn == 0 and k % tk == 0
    assert nk == 1 or nj == 1
    cache = nk == 1 and nj > 1
    na, ne, no = len(a_ops), len(e_ops), len(outs)

    def body(*refs):
        a_refs = refs[:na]
        w_ref = refs[na]
        e_refs = refs[na + 1:na + 1 + ne]
        o_refs = refs[na + 1 + ne:na + 1 + ne + no]
        scr = refs[na + 1 + ne + no:]
        i, j, kk = pl.program_id(0), pl.program_id(1), pl.program_id(2)
        if cache:
            a_scr = scr[0]

            @pl.when(j == 0)
            def _():
                a_scr[...] = pro(a_refs, o_refs, i)

            a = a_scr[...]
        else:
            a = pro(a_refs, o_refs, i)
        part = dot(a, w_ref)
        if nk == 1:
            epi(part, e_refs, o_refs, i, j)
        else:
            acc_ref = scr[0]

            @pl.when(kk == 0)
            def _():
                acc_ref[...] = part

            @pl.when(kk > 0)
            def _():
                acc_ref[...] += part

            @pl.when(kk == nk - 1)
            def _():
                epi(acc_ref[...], e_refs, o_refs, i, j)

    scratch = []
    if cache:
        scratch.append(pltpu.VMEM((tm, k), BF16))
    if nk > 1:
        scratch.append(pltpu.VMEM((tm, tn), F32))
    in_specs = [pl.BlockSpec(b, m) for (_, b, m) in a_ops]
    in_specs.append(pl.BlockSpec(w_block, w_imap))
    in_specs += [pl.BlockSpec(b, m) for (_, b, m) in e_ops]
    return pl.pallas_call(
        body, name=name, grid=(ni, nj, nk),
        in_specs=in_specs,
        out_specs=[pl.BlockSpec(b, m) for (_, _, b, m) in outs],
        out_shape=[jax.ShapeDtypeStruct(s, d) for (s, d, _, _) in outs],
        scratch_shapes=scratch,
        compiler_params=_params(3),
    )(*[a for (a, _, _) in a_ops], w, *[e for (e, _, _) in e_ops])


def _lw(w, block, imap):
    if isinstance(w, tuple):
        arr, layer = w
        return arr, (None,) + block, (lambda i, j, kk: (layer,) + imap(i, j, kk))
    return w, block, imap


def _dot_w(a, w_ref):
    return jnp.dot(a, w_ref[...], preferred_element_type=F32)


def _dot_wt(a, w_ref):
    return lax.dot_general(a, w_ref[...], (NT, ((), ())), preferred_element_type=F32)


def _rms_pro(h, g):
    r = lax.rsqrt(jnp.mean(h * h, axis=-1, keepdims=True) + EPS)
    return h * r * g


def _rms_bwd(dyn, h, g):
    r = lax.rsqrt(jnp.mean(h * h, axis=-1, keepdims=True) + EPS)
    xh = h * r
    t = dyn * g
    dh = r * (t - xh * jnp.mean(t * xh, axis=-1, keepdims=True))
    return dh, _rows8(dyn * xh)


def _acc_out(ref, val, first):
    @pl.when(first)
    def _():
        ref[...] = val

    @pl.when(jnp.logical_not(first))
    def _():
        ref[...] += val


def norm_mm(name, h, g, w, *, tn, k_cols=None, col_block=0, w_stacked=False, out_dtype=F32):
    rows = h.shape[0]
    k = k_cols or h.shape[1]
    wshape = (w[0].shape[1:] if isinstance(w, tuple) else w.shape)
    n = wshape[0] * wshape[2] if w_stacked else wshape[1]
    tm = _big_tile(rows)

    def pro(a_refs, o_refs, i):
        hn = _rms_pro(a_refs[0][...], a_refs[1][...]).astype(BF16)
        o_refs[1][...] = hn
        return hn

    def epi(acc, e_refs, o_refs, i, j):
        o_refs[0][...] = acc.astype(out_dtype)

    if w_stacked:
        w_block, w_imap = (None, k, tn), (lambda i, j, kk: (j, 0, 0))
    else:
        w_block, w_imap = (k, tn), (lambda i, j, kk: (0, j))
    w, w_block, w_imap = _lw(w, w_block, w_imap)
    return fused_mm(
        name, rows=rows, k=k, n=n, tm=tm, tn=tn,
        a_ops=[(h, (tm, k), lambda i, j, kk: (i, col_block)), (g, (1, k), lambda i, j, kk: (0, 0))],
        pro=pro, w=w, w_block=w_block, w_imap=w_imap, dot=_dot_w, epi=epi,
        outs=[((rows, n), out_dtype, (tm, tn), lambda i, j, kk: (i, j)),
              ((rows, k), BF16, (tm, k), lambda i, j, kk: (i, 0))])


def res_mm(name, a_ops, pro, k, w, res, *, tn, save_dtype=None, tm=None, tk=None):
    rows, n = res.shape
    tm = tm or _row_tile(rows)
    assert tk is None or save_dtype is None

    def pro2(a_refs, o_refs, i):
        a = pro(a_refs)
        if save_dtype is not None:
            o_refs[1][...] = a
        return a

    def epi(acc, e_refs, o_refs, i, j):
        o_refs[0][...] = e_refs[0][...] + acc

    outs = [((rows, n), F32, (tm, tn), lambda i, j, kk: (i, j))]
    if save_dtype is not None:
        outs.append(((rows, k), save_dtype, (tm, k), lambda i, j, kk: (i, 0)))
    w, w_block, w_imap = _lw(w, (tk or k, tn), lambda i, j, kk: (kk, j))
    out = fused_mm(
        name, rows=rows, k=k, n=n, tm=tm, tn=tn, tk=tk,
        a_ops=[(a, tuple(tm if x is None else x for x in b), m) for (a, b, m) in a_ops],
        pro=pro2, w=w, w_block=w_block, w_imap=w_imap, dot=_dot_w,
        e_ops=[(res, (tm, tn), lambda i, j, kk: (i, j))], epi=epi, outs=outs)
    return out if save_dtype is not None else out[0]


def wgrad_mm(name, a_ops, pro_a, g_ops, pro_g, *, rows, k1, n, t1, tn, out_shape=None, out_block=None, out_imap=None):
    tt = _row_tile(rows)
    n1, n2, nt = k1 // t1, n // tn, rows // tt
    assert k1 % t1 == 0 and n % tn == 0
    na = len(a_ops)

    def body(*refs):
        a_refs = refs[:na]
        g_refs = refs[na:-2]
        o_ref, acc = refs[-2:]
        t = pl.program_id(2)
        a = pro_a(a_refs).astype(BF16)
        g = pro_g(g_refs).astype(BF16)
        _acc_out(acc, lax.dot_general(a, g, (TN, ((), ())), preferred_element_type=F32), t == 0)

        @pl.when(t == nt - 1)
        def _():
            if len(o_ref.shape) == 3:
                ws = o_ref.shape[2]
                for q in range(o_ref.shape[0]):
                    o_ref[q] = acc[:, q * ws:(q + 1) * ws].astype(BF16)
            else:
                o_ref[...] = acc[...].astype(BF16)

    return pl.pallas_call(
        body, name=name, grid=(n1, n2, nt),
        in_specs=[pl.BlockSpec(b, m) for (_, b, m) in list(a_ops) + list(g_ops)],
        out_specs=pl.BlockSpec(out_block or (t1, tn), out_imap or (lambda a, b, t: (a, b))),
        out_shape=jax.ShapeDtypeStruct(out_shape or (k1, n), BF16),
        scratch_shapes=[pltpu.VMEM((t1, tn), F32)],
        compiler_params=_params(3),
    )(*[a for (a, _, _) in list(a_ops) + list(g_ops)])


def simple_wgrad(name, a, g, *, a_cols=None, a_col_block=0, t1=None, tn=None, **kw):
    rows = a.shape[0]
    k1 = a_cols or a.shape[1]
    n = g.shape[1]
    tt = _row_tile(rows)
    t1 = t1 or min(k1, 512)
    tn = tn or min(n, 1024)
    return wgrad_mm(
        name,
        [(a, (tt, t1), lambda x, y, t: (t, x + a_col_block * (k1 // t1)))], lambda r: r[0][...],
        [(g, (tt, tn), lambda x, y, t: (t, y))], lambda r: r[0][...],
        rows=rows, k1=k1, n=n, t1=t1, tn=tn, **kw)


def rms_bwd_mm(name, dz_ops, pro, k, w, w_block, w_imap, dot, h, g, dh, *, tk=None, h_cols=None, h_col_block=0,
               add_dh=True, tm=None):
    rows = h.shape[0]
    n = h_cols or h.shape[1]
    tm = tm or _big_tile(rows)
    ni = rows // tm
    w, w_block, w_imap = _lw(w, w_block, w_imap)

    def epi(acc, e_refs, o_refs, i, j):
        d, dg = _rms_bwd(acc, e_refs[0][...], e_refs[1][...])
        if add_dh:
            d = d + e_refs[2][...]
        o_refs[0][...] = jnp.where(_row_mask(i, tm), d, 0.0)
        _acc_out(o_refs[1], dg, i == 0)

    e_ops = [(h, (tm, n), lambda i, j, kk: (i, h_col_block)), (g, (1, n), lambda i, j, kk: (0, 0))]
    if add_dh:
        e_ops.append((dh, (tm, n), lambda i, j, kk: (i, 0)))
    return fused_mm(
        name, rows=rows, k=k, n=n, tm=tm, tn=n, tk=tk,
        a_ops=[(a, tuple(tm if x is None else x for x in b), m) for (a, b, m) in dz_ops],
        pro=lambda a_refs, o_refs, i: pro(a_refs), w=w, w_block=w_block, w_imap=w_imap, dot=dot,
        e_ops=e_ops, epi=epi,
        outs=[((rows, n), F32, (tm, n), lambda i, j, kk: (i, 0)),
              ((SUBLANES, n), F32, (SUBLANES, n), lambda i, j, kk: (0, 0))])


def _relu2(u):
    r = jnp.maximum(u.astype(F32), 0.0)
    return r * r


def mlp_fwd(tag, h, g, w_up_st, w_down):
    d_ff = w_down.shape[0]
    u, hn = norm_mm(f"mlp_up_{tag}", h, g, w_up_st, tn=w_up_st.shape[2], w_stacked=True, out_dtype=BF16)
    out = res_mm(f"mlp_down_{tag}", [(u, (None, 512), lambda i, j, kk: (i, kk))],
                 lambda r: _relu2(r[0][...]).astype(BF16), d_ff, w_down, h, tn=h.shape[1],
                 tm=_big_tile(h.shape[0]), tk=512)
    return out, (h, hn, u)


def mlp_bwd(tag, dh, saved, g, w_up_st, w_down):
    h, hn, u = saved
    rows, d = h.shape
    d_ff = w_down.shape[0]
    ts = w_up_st.shape[2]
    tm = _big_tile(rows)
    tt = _row_tile(rows)
    wd, wd_block, wd_imap = _lw(w_down, (512, d), lambda i, j, kk: (j, 0))

    def epi_du(acc, e_refs, o_refs, i, j):
        o_refs[0][...] = (acc * (2.0 * jnp.maximum(e_refs[0][...].astype(F32), 0.0))).astype(BF16)

    du, = fused_mm(
        f"mlp_du_{tag}", rows=rows, k=d, n=d_ff, tm=tm, tn=512,
        a_ops=[(dh, (tm, d), lambda i, j, kk: (i, 0))], pro=lambda a, o, i: a[0][...].astype(BF16),
        w=wd, w_block=wd_block, w_imap=wd_imap, dot=_dot_wt,
        e_ops=[(u, (tm, 512), lambda i, j, kk: (i, j))], epi=epi_du,
        outs=[((rows, d_ff), BF16, (tm, 512), lambda i, j, kk: (i, j))])
    half = d_ff // 2
    dw_down = wgrad_mm(
        f"mlp_dwdown_{tag}",
        [(u, (tt, half), lambda a, b, t: (t, a))], lambda r: _relu2(r[0][...]),
        [(dh, (tt, d), lambda a, b, t: (t, 0))], lambda r: r[0][...],
        rows=rows, k1=d_ff, n=d, t1=half, tn=d)
    dw_up = simple_wgrad(f"mlp_dwup_{tag}", hn, du, t1=d, tn=half, out_shape=(N_DEV, d, ts),
                         out_block=(half // ts, d, ts), out_imap=lambda a, b, t: (b, 0, 0))
    dh_in, dg = rms_bwd_mm(
        f"mlp_dh_{tag}", [(du, (None, ts), lambda i, j, kk: (i, kk))], lambda r: r[0][...], d_ff,
        w_up_st, (None, d, ts), lambda i, j, kk: (kk, 0, 0), _dot_wt, h, g, dh, tk=ts)
    return dh_in, dw_up, dw_down, dg


CONV_HALO = SUBLANES
CONV_W = 2 * SSD_D_INNER
CONV_BLK = 512


def _silu(x):
    return x * jax.nn.sigmoid(x)


def _conv_pre(ext_ref, w, b, tm):
    pre = b
    for k in range(SSD_CONV):
        pre = pre + w[k:k + 1, :] * ext_ref[pl.ds(CONV_HALO - (SSD_CONV - 1) + k, tm), :]
    return pre


def _conv_load(ext, parts, c):
    for cur, halo, lo, hi in parts:
        ext[pl.ds(0, CONV_HALO), lo:hi] = jnp.where(c > 0, halo[...], 0.0)
        ext[pl.ds(CONV_HALO, CHUNK), lo:hi] = cur[...]


def _conv_specs(idx):
    hb = CHUNK // CONV_HALO
    specs = []
    for w, blk in ((SSD_D_INNER, 1), (BC_W, SSD_D_INNER // BC_W + 2), (BC_W, SSD_D_INNER // BC_W + 3)):
        specs.append(pl.BlockSpec((CHUNK, w), lambda s, blk=blk: (idx(s), blk)))
        specs.append(pl.BlockSpec((CONV_HALO, w), lambda s, blk=blk: (jnp.maximum(idx(s) * hb - 1, 0), blk)))
    return specs


CONV_COLS = ((0, SSD_D_INNER), (SSD_D_INNER, SSD_D_INNER + SSD_GROUPS * SSD_STATE),
             (SSD_D_INNER + SSD_GROUPS * SSD_STATE, 2 * SSD_D_INNER))


@functools.partial(jax.custom_vjp, nondiff_argnums=(1,))
def _sub_row(x, h):
    return x[h:h + 1, :]


_sub_row.defvjp(
    lambda x, h: (x[h:h + 1, :], None),
    lambda h, _, g: (jnp.where(lax.broadcasted_iota(jnp.int32, (LANES, 1), 0) == h, g, 0.0),))


def _splitter(axis, size, count):
    def blocks(x):
        return tuple(lax.slice_in_dim(x, q * size, (q + 1) * size, axis=axis) for q in range(count))

    split = jax.custom_vjp(blocks)
    split.defvjp(lambda x: (blocks(x), None), lambda _, gs: (jnp.concatenate(gs, axis=axis),))
    return split


def _split3(x):
    hi = x.astype(BF16)
    r = x - hi.astype(F32)
    mid = r.astype(BF16)
    return hi, mid, (r - mid.astype(F32)).astype(BF16)


def _expand_impl(x, e):
    return sum(jnp.dot(t, e, preferred_element_type=F32) for t in _split3(x))


@jax.custom_vjp
def _expand(x, e):
    return _expand_impl(x, e)


def _expand_bwd(e, g):
    hi, mid, _ = _split3(g)
    dx = sum(lax.dot_general(t, e, (NT, ((), ())), preferred_element_type=F32) for t in (hi, mid))
    return dx, jnp.zeros_like(e)


_expand.defvjp(lambda x, e: (_expand_impl(x, e), e), _expand_bwd)

HEAD_PAIR = 2 * SSD_HEAD_DIM
GROUP_W = SSD_HPG * SSD_HEAD_DIM


def _ssd_chunk(xs, bm, cm, dtraw, prev, par, c, tri, e64, e128):
    li = lax.broadcasted_iota(jnp.int32, (CHUNK, CHUNK), 0)
    si = lax.broadcasted_iota(jnp.int32, (CHUNK, CHUNK), 1)
    causal = li >= si
    first_head = lax.broadcasted_iota(jnp.int32, (1, HEAD_PAIR), 1) < SSD_HEAD_DIM
    dt = jnp.where(_row_mask(c, CHUNK), jax.nn.softplus(dtraw + par[0:1, :]), 0.0)
    a = -jnp.exp(par[1:2, :])
    acs = jnp.dot(tri, dt * a, precision=lax.Precision.HIGHEST, preferred_element_type=F32)
    acs_t = acs.T
    last = acs[CHUNK - 1:CHUNK, :]
    misc = jnp.concatenate([jnp.exp(last), par[2:3, :], jnp.zeros((SUBLANES - 2, LANES), F32)], axis=0)
    wide = _expand(jnp.concatenate([dt, dt * jnp.exp(last - acs), jnp.exp(acs)], axis=0), e64)
    dt_w, dtend_w, start_w = _splitter(0, CHUNK, 3)(wide)
    misc_w = _expand(misc, e64)
    col_w = _splitter(1, CHUNK, SSD_HEADS)(_expand(acs, e128))
    groups = _splitter(1, GROUP_W, SSD_GROUPS)
    xs_g, prev_g, start_g = groups(xs), groups(prev), groups(start_w)
    xdt_p = _splitter(1, HEAD_PAIR, SSD_HEADS // 2)(xs * dt_w)
    xdtend_g = groups(xs * dtend_w)
    last_g, skip_g = groups(misc_w[0:1, :]), groups(misc_w[1:2, :])
    b_g, c_g = _splitter(1, SSD_STATE, SSD_GROUPS)(bm), _splitter(1, SSD_STATE, SSD_GROUPS)(cm)
    ys, news = [], []
    for g in range(SSD_GROUPS):
        cb = bdot_nt(c_g[g], b_g[g])
        st = bdot_nn(b_g[g].T, xdtend_g[g])
        y_off = bdot_nn(c_g[g], prev_g[g]) * start_g[g]
        pairs = []
        for q in range(SSD_HPG // 2):
            xp = xdt_p[g * (SSD_HPG // 2) + q]
            acc = None
            for r in range(2):
                head = SSD_HPG * g + 2 * q + r
                seg = jnp.where(causal, col_w[head] - _sub_row(acs_t, head), 0.0)
                decay = jnp.where(causal, jnp.exp(seg), 0.0)
                t = bdot_nn(cb * decay, jnp.where(first_head if r == 0 else jnp.logical_not(first_head), xp, 0.0))
                acc = t if acc is None else acc + t
            pairs.append(acc)
        ys.append(jnp.concatenate(pairs, axis=1) + y_off + xs_g[g] * skip_g[g])
        news.append(prev_g[g] * last_g[g] + st)
    return jnp.concatenate(ys, axis=1), jnp.concatenate(news, axis=1)


def _expanders():
    e64 = np.zeros((LANES, SSD_D_INNER), np.float32)
    e128 = np.zeros((LANES, SSD_HEADS * CHUNK), np.float32)
    for h in range(SSD_HEADS):
        e64[h, h * SSD_HEAD_DIM:(h + 1) * SSD_HEAD_DIM] = 1.0
        e128[h, h * CHUNK:(h + 1) * CHUNK] = 1.0
    return jnp.asarray(e64, BF16), jnp.asarray(e128, BF16)


def _tri():
    return jnp.asarray(np.tril(np.ones((CHUNK, CHUNK), np.float32)))


BC_W = SSD_GROUPS * SSD_STATE


def _conv_act(ext, refs, w, b, c):
    _conv_load(ext, [(refs[2 * p], refs[2 * p + 1]) + CONV_COLS[p] for p in range(3)], c)
    pre = _conv_pre(ext, w, b, CHUNK)
    return pre, jnp.where(_row_mask(c, CHUNK), _silu(pre), 0.0)


def ssd_fwd(name, zx, dt_block, par, conv_w, conv_b, comm=None):
    rows = zx.shape[0]
    nc = rows // CHUNK

    def body(*refs):
        xbc_refs, (dt_ref, par_ref, tri_ref, e64_ref, e128_ref, cw_ref, cb_ref) = refs[:6], refs[6:13]
        y_ref, st_ref, state, ext = refs[13:]
        c = pl.program_id(0)

        @pl.when(c == 0)
        def _():
            state[...] = jnp.zeros((SSD_STATE, SSD_D_INNER), F32)

        prev = state[...]
        st_ref[...] = prev
        _, act = _conv_act(ext, xbc_refs, cw_ref[...], cb_ref[...], c)
        (x0, x1), (b0, b1), (c0, c1) = CONV_COLS
        y, new = _ssd_chunk(act[:, x0:x1], act[:, b0:b1], act[:, c0:c1], dt_ref[...], prev, par_ref[...], c,
                            tri_ref[...], e64_ref[...], e128_ref[...])
        y_ref[...] = y
        state[...] = new

    const = lambda a: pl.BlockSpec(a.shape, lambda c: (0,) * a.ndim)
    consts = (par, _tri()) + _expanders() + (conv_w, conv_b)
    return carrier_call(
        name, body, (nc,),
        _conv_specs(lambda c: c) + [pl.BlockSpec((CHUNK, LANES), lambda c: (c, dt_block))] + [const(a) for a in consts],
        [pl.BlockSpec((CHUNK, SSD_D_INNER), lambda c: (c, 0)),
         pl.BlockSpec((None, SSD_STATE, SSD_D_INNER), lambda c: (c, 0, 0))],
        [jax.ShapeDtypeStruct((rows, SSD_D_INNER), F32),
         jax.ShapeDtypeStruct((nc, SSD_STATE, SSD_D_INNER), F32)],
        [pltpu.VMEM((SSD_STATE, SSD_D_INNER), F32), pltpu.VMEM((CHUNK + CONV_HALO, CONV_W), F32)],
        (zx,) * 7 + consts, comm)


def carrier_call(name, body, grid, in_specs, out_specs, out_shape, scratch_shapes, args, comm):
    body, in_specs, out_specs, out_shape, scratch_shapes, extra, n_own = with_comm(
        comm, body, grid, in_specs, out_specs, out_shape, scratch_shapes)
    res = pl.pallas_call(
        body, name=name, grid=grid, in_specs=in_specs, out_specs=out_specs, out_shape=out_shape,
        scratch_shapes=scratch_shapes, compiler_params=_params(len(grid)))(*args, *extra)
    return res[:n_own], res[n_own:]


def ssd_bwd(name, dy, zx, dt_block, states, par, conv_w, conv_b, comm=None):
    rows = zx.shape[0]
    nc = rows // CHUNK

    def body(*refs):
        dy_ref, xbc_refs = refs[0], refs[1:7]
        dt_ref, st_ref, par_ref, tri_ref, e64_ref, e128_ref, cw_ref, cb_ref = refs[7:15]
        du_ref, ddt_ref, dpar_ref, dcw_ref, dcb_ref, dstate, ext, dext = refs[15:]
        s = pl.program_id(0)
        c = nc - 1 - s

        @pl.when(s == 0)
        def _():
            dstate[...] = jnp.zeros((SSD_STATE, SSD_D_INNER), F32)
            dext[pl.ds(CHUNK, CONV_HALO), :] = jnp.zeros((CONV_HALO, CONV_W), F32)

        _, act = _conv_act(ext, xbc_refs, cw_ref[...], cb_ref[...], c)
        (x0, x1), (b0, b1), (c0, c1) = CONV_COLS

        def f(xs, bm, cm, dtraw, prev, par_v):
            return _ssd_chunk(xs, bm, cm, dtraw, prev, par_v, c, tri_ref[...], e64_ref[...], e128_ref[...])

        _, vjp = jax.vjp(f, act[:, x0:x1], act[:, b0:b1], act[:, c0:c1], dt_ref[...], st_ref[...], par_ref[...])
        dxs, dbm, dcm, ddt, dprev, dpar = vjp((dy_ref[...], dstate[...]))
        ddt_ref[...] = ddt.astype(BF16)
        dstate[...] = dprev
        _acc_out(dpar_ref, dpar, s == 0)

        dacts = [dxs[:, q * CONV_BLK:(q + 1) * CONV_BLK] for q in range((x1 - x0) // CONV_BLK)]
        dacts += [dbm[:, q * CONV_BLK:(q + 1) * CONV_BLK] for q in range((b1 - b0) // CONV_BLK)]
        dacts += [dcm[:, q * CONV_BLK:(q + 1) * CONV_BLK] for q in range((c1 - c0) // CONV_BLK)]
        for q, dact in enumerate(dacts):
            cols = slice(q * CONV_BLK, (q + 1) * CONV_BLK)
            w = cw_ref[:, cols]
            taps = [ext[pl.ds(CONV_HALO - (SSD_CONV - 1) + k, CHUNK), cols] for k in range(SSD_CONV)]
            pre = cb_ref[:, cols]
            for k in range(SSD_CONV):
                pre = pre + w[k:k + 1, :] * taps[k]
            sg = jax.nn.sigmoid(pre)
            dpre = jnp.where(_row_mask(c, CHUNK), dact * (sg * (1.0 + pre * (1.0 - sg))), 0.0)
            dext[pl.ds(0, CHUNK), cols] = dpre
            du = jnp.zeros((CHUNK, CONV_BLK), F32)
            for k in range(SSD_CONV):
                du = du + w[k:k + 1, :] * dext[pl.ds(SSD_CONV - 1 - k, CHUNK), cols]
            du_ref[:, cols] = du.astype(BF16)
            _acc_out(dcb_ref.at[:, cols], _rows8(dpre), s == 0)
            for k in range(SSD_CONV):
                _acc_out(dcw_ref.at[pl.ds(k * SUBLANES, SUBLANES), cols], _rows8(dpre * taps[k]), s == 0)
            dext[pl.ds(CHUNK, CONV_HALO), cols] = dpre[0:CONV_HALO, :]

    rev = lambda w, b: pl.BlockSpec((CHUNK, w), lambda s: (nc - 1 - s, b))
    const = lambda a: pl.BlockSpec(a.shape, lambda s: (0,) * a.ndim)
    consts = (par, _tri()) + _expanders() + (conv_w, conv_b)
    return carrier_call(
        name, body, (nc,),
        [rev(SSD_D_INNER, 0)] + _conv_specs(lambda s: nc - 1 - s)
        + [rev(LANES, dt_block), pl.BlockSpec((None, SSD_STATE, SSD_D_INNER), lambda s: (nc - 1 - s, 0, 0))]
        + [const(a) for a in consts],
        [rev(CONV_W, 0), rev(LANES, 0), pl.BlockSpec((SUBLANES, LANES), lambda s: (0, 0)),
         pl.BlockSpec((SSD_CONV * SUBLANES, CONV_W), lambda s: (0, 0)),
         pl.BlockSpec((SUBLANES, CONV_W), lambda s: (0, 0))],
        [jax.ShapeDtypeStruct((rows, CONV_W), BF16),
         jax.ShapeDtypeStruct((rows, LANES), BF16),
         jax.ShapeDtypeStruct((SUBLANES, LANES), F32),
         jax.ShapeDtypeStruct((SSD_CONV * SUBLANES, CONV_W), F32),
         jax.ShapeDtypeStruct((SUBLANES, CONV_W), F32)],
        [pltpu.VMEM((SSD_STATE, SSD_D_INNER), F32), pltpu.VMEM((CHUNK + CONV_HALO, CONV_W), F32),
         pltpu.VMEM((CHUNK + CONV_HALO, CONV_W), F32)],
        (dy,) + (zx,) * 7 + (states,) + consts, comm)


GN_W = SSD_D_INNER // SSD_GROUPS


def _gated_norm(y, z, ng):
    g = y * _silu(z)
    outs = []
    for q in range(SSD_GROUPS):
        gs = g[:, q * GN_W:(q + 1) * GN_W]
        outs.append(gs * lax.rsqrt(jnp.mean(gs * gs, axis=-1, keepdims=True) + EPS))
    return jnp.concatenate(outs, axis=1) * ng


def ssd_layer_fwd(tag, h, ln_g, w, comm=None):
    zx, hn = norm_mm(f"ssd_in_{tag}", h, ln_g, w["w_in"], tn=896)
    dt_block = 3 * SSD_D_INNER // LANES
    (y, states), cres = ssd_fwd(f"ssd_scan_{tag}", zx, dt_block, w["par"], w["conv_w"], w["conv_b"], comm)
    out, gn = res_mm(
        f"ssd_out_{tag}",
        [(y, (None, SSD_D_INNER), lambda i, j, kk: (i, 0)), (zx, (None, SSD_D_INNER), lambda i, j, kk: (i, 0)),
         (w["norm"], (1, SSD_D_INNER), lambda i, j, kk: (0, 0))],
        lambda r: _gated_norm(r[0][...], r[1][...], r[2][...]).astype(BF16),
        SSD_D_INNER, w["w_out"], h, tn=512, save_dtype=BF16)
    return out, (h, hn, zx, y, states, gn), cres


def ssd_dwout(tag, dh, saved):
    return simple_wgrad(f"ssd_dwout_{tag}", saved[5], dh, t1=SSD_D_INNER, tn=dh.shape[1])


def ssd_layer_bwd(tag, dh, saved, ln_g, w, dw_out, comm=None):
    h, hn, zx, y, states, gn = saved
    rows, d = h.shape
    tm = _row_tile(rows)
    dt_block = 3 * SSD_D_INNER // LANES

    def epi_gate(acc, e_refs, o_refs, i, j):
        _, vjp = jax.vjp(_gated_norm, e_refs[0][...], e_refs[1][...], e_refs[2][...])
        dy, dz, dng = vjp(acc)
        o_refs[0][...] = dy
        o_refs[1][...] = dz.astype(BF16)
        row0 = lax.broadcasted_iota(jnp.int32, (SUBLANES, 1), 0) == 0
        _acc_out(o_refs[2], jnp.where(row0, dng, 0.0), i == 0)

    wo, wo_block, wo_imap = _lw(w["w_out"], (SSD_D_INNER, d), lambda i, j, kk: (0, 0))
    dy, dz, dnorm = fused_mm(
        f"ssd_dgate_{tag}", rows=rows, k=d, n=SSD_D_INNER, tm=tm, tn=SSD_D_INNER,
        a_ops=[(dh, (tm, d), lambda i, j, kk: (i, 0))], pro=lambda a, o, i: a[0][...].astype(BF16),
        w=wo, w_block=wo_block, w_imap=wo_imap, dot=_dot_wt,
        e_ops=[(y, (tm, SSD_D_INNER), lambda i, j, kk: (i, 0)), (zx, (tm, SSD_D_INNER), lambda i, j, kk: (i, 0)),
               (w["norm"], (1, SSD_D_INNER), lambda i, j, kk: (0, 0))],
        epi=epi_gate,
        outs=[((rows, SSD_D_INNER), F32, (tm, SSD_D_INNER), lambda i, j, kk: (i, 0)),
              ((rows, SSD_D_INNER), BF16, (tm, SSD_D_INNER), lambda i, j, kk: (i, 0)),
              ((SUBLANES, SSD_D_INNER), F32, (SUBLANES, SSD_D_INNER), lambda i, j, kk: (0, 0))])
    (dxbc, ddt, dpar, dcw, dcb), cres = ssd_bwd(f"ssd_dscan_{tag}", dy, zx, dt_block, states, w["par"],
                                                w["conv_w"], w["conv_b"], comm)
    dzx = jnp.concatenate([dz, dxbc, ddt], axis=1)
    k = dzx.shape[1]
    dw_in = simple_wgrad(f"ssd_dwin_{tag}", hn, dzx, t1=d, tn=896)
    dh_in, dln = rms_bwd_mm(
        f"ssd_dh_{tag}", [(dzx, (None, 896), lambda i, j, kk: (i, kk))], lambda r: r[0][...], k,
        w["w_in"], (d, 896), lambda i, j, kk: (0, kk), _dot_wt, h, ln_g, dh, tk=896,
        tm=704 if rows % 704 == 0 else None)
    grads = dict(w_in=dw_in, w_out=dw_out, conv_w=dcw, conv_b=dcb, par=dpar, norm=dnorm, ln=dln)
    return dh_in, grads, cres


HP = 2 * LANES
VP = 2 * MLA_V
N_PAIRS = MLA_HEADS // 2
ATT_SCALE = MLA_QK ** -0.5
LOG2E = float(np.log2(np.e))
LN2 = float(np.log(2.0))
ROT = MLA_ROPE // 2


def rope_tables(rows):
    inv = 1.0 / (ROPE_THETA ** (jnp.arange(0, MLA_ROPE, 2, dtype=F32) / MLA_ROPE))
    pos = jnp.arange(rows, dtype=F32) - PAD
    ang = pos[:, None] * inv[None, :]
    cos, sin = jnp.cos(ang), jnp.sin(ang)
    one = jnp.ones((rows, MLA_NOPE), F32)
    zero = jnp.zeros((rows, LANES - MLA_QK), F32)
    zn = jnp.zeros((rows, MLA_NOPE), F32)
    zr = jnp.zeros((rows, ROT), F32)
    cosf = jnp.concatenate([one, cos, cos, zero], axis=1)
    sina = jnp.concatenate([zn, -sin, zr, zero], axis=1)
    sinb = jnp.concatenate([zn, zr, sin, zero], axis=1)
    return cosf, sina, sinb


def _qk_norm_rope(x, g, cosf, sina, sinb):
    r = lax.rsqrt(jnp.sum(x * x, axis=-1, keepdims=True) * (1.0 / MLA_QK) + EPS)
    xn = x * r * g
    return xn * cosf + pltpu.roll(xn, LANES - ROT, 1) * sina + pltpu.roll(xn, ROT, 1) * sinb


def _qk_norm_rope_bwd(dout, x, g, cosf, sina, sinb):
    dxn = dout * cosf + pltpu.roll(dout * sina, ROT, 1) + pltpu.roll(dout * sinb, LANES - ROT, 1)
    r = lax.rsqrt(jnp.sum(x * x, axis=-1, keepdims=True) * (1.0 / MLA_QK) + EPS)
    xh = x * r
    t = dxn * g
    dx = r * (t - xh * (jnp.sum(t * xh, axis=-1, keepdims=True) * (1.0 / MLA_QK)))
    return dx, _rows8(dxn * xh)


def _rope_lanes():
    lane = lax.broadcasted_iota(jnp.int32, (1, LANES), 1)
    return jnp.logical_and(lane >= MLA_NOPE, lane < MLA_QK)


QW = MLA_HEADS * LANES
VW = MLA_HEADS * MLA_V


def qk_prep(name, qraw, kvraw, lat, kpe_block, qg, kg, tabs):
    rows = qraw.shape[0]
    tm = _row_tile(rows)

    def body(q_ref, k0_ref, k1_ref, v_ref, pe_ref, qg_ref, kg_ref, c_ref, sa_ref, sb_ref,
             qo_ref, ko_ref, kt_ref, vo_ref, vt_ref):
        tab = (c_ref[...], sa_ref[...], sb_ref[...])
        pe = pe_ref[...]
        for hd in range(MLA_HEADS):
            sl = slice(hd * LANES, (hd + 1) * LANES)
            qo_ref[:, sl] = _qk_norm_rope(q_ref[:, sl].astype(F32), qg_ref[...], *tab).astype(BF16)
            kr = k0_ref if hd < MLA_HEADS // 2 else k1_ref
            ks = slice((hd % (MLA_HEADS // 2)) * LANES, (hd % (MLA_HEADS // 2) + 1) * LANES)
            kk = _qk_norm_rope(kr[:, ks].astype(F32) + pe, kg_ref[...], *tab)
            ko_ref[:, sl] = kk.astype(BF16)
            kt_ref[sl, :] = kk.T.astype(BF16)
        vo_ref[...] = v_ref[...].astype(BF16)
        for c in range(VW // LANES):
            sl = slice(c * LANES, (c + 1) * LANES)
            vt_ref[sl, :] = v_ref[:, sl].astype(F32).T.astype(BF16)

    row = lambda w, b: pl.BlockSpec((tm, w), lambda i: (i, b))
    col = lambda w: pl.BlockSpec((w, tm), lambda i: (0, i))
    one = pl.BlockSpec((1, LANES), lambda i: (0, 0))
    return pl.pallas_call(
        body, name=name, grid=(rows // tm,),
        in_specs=[row(QW, 0), row(VW, 0), row(VW, 1), row(VW, 2), row(LANES, kpe_block), one, one,
                  row(LANES, 0), row(LANES, 0), row(LANES, 0)],
        out_specs=[row(QW, 0), row(QW, 0), col(QW), row(VW, 0), col(VW)],
        out_shape=[jax.ShapeDtypeStruct((rows, QW), BF16), jax.ShapeDtypeStruct((rows, QW), BF16),
                   jax.ShapeDtypeStruct((QW, rows), BF16), jax.ShapeDtypeStruct((rows, VW), BF16),
                   jax.ShapeDtypeStruct((VW, rows), BF16)],
        compiler_params=_params(1),
    )(qraw, kvraw, kvraw, kvraw, lat, qg, kg, *tabs)


def qk_prep_bwd(name, dq_t, dk, dv, qraw, kvraw, lat, kpe_block, qg, kg, tabs):
    rows = qraw.shape[0]
    tm = _row_tile(rows)

    def body(dq_ref, dk_ref, dv_ref, q_ref, k0_ref, k1_ref, pe_ref, qg_ref, kg_ref, c_ref, sa_ref, sb_ref,
             dqo_ref, dkvo_ref, dpe_ref, dqg_ref, dkg_ref):
        i = pl.program_id(0)
        tab = (c_ref[...], sa_ref[...], sb_ref[...])
        pe = pe_ref[...]
        dpe = jnp.zeros((tm, LANES), F32)
        dqg = jnp.zeros((SUBLANES, LANES), F32)
        dkg = jnp.zeros((SUBLANES, LANES), F32)
        for hd in range(MLA_HEADS):
            sl = slice(hd * LANES, (hd + 1) * LANES)
            dx, dg = _qk_norm_rope_bwd(dq_ref[sl, :].T, q_ref[:, sl].astype(F32), qg_ref[...], *tab)
            dqo_ref[:, sl] = dx.astype(BF16)
            dqg = dqg + dg
            kr = k0_ref if hd < MLA_HEADS // 2 else k1_ref
            ks = slice((hd % (MLA_HEADS // 2)) * LANES, (hd % (MLA_HEADS // 2) + 1) * LANES)
            dx, dg = _qk_norm_rope_bwd(dk_ref[:, sl], kr[:, ks].astype(F32) + pe, kg_ref[...], *tab)
            dkvo_ref[:, sl] = dx.astype(BF16)
            dpe = dpe + dx
            dkg = dkg + dg
        dkvo_ref[:, QW:QW + VW] = dv_ref[...].astype(BF16)
        dpe_ref[...] = jnp.where(_rope_lanes(), dpe, 0.0)
        _acc_out(dqg_ref, dqg, i == 0)
        _acc_out(dkg_ref, dkg, i == 0)

    row = lambda w, b: pl.BlockSpec((tm, w), lambda i: (i, b))
    one = pl.BlockSpec((1, LANES), lambda i: (0, 0))
    acc = pl.BlockSpec((SUBLANES, LANES), lambda i: (0, 0))
    return pl.pallas_call(
        body, name=name, grid=(rows // tm,),
        in_specs=[pl.BlockSpec((QW, tm), lambda i: (0, i)), row(QW, 0), row(VW, 0), row(QW, 0), row(VW, 0), row(VW, 1),
                  row(LANES, kpe_block), one, one, row(LANES, 0), row(LANES, 0), row(LANES, 0)],
        out_specs=[row(QW, 0), row(QW + VW, 0), row(LANES, 0), acc, acc],
        out_shape=[jax.ShapeDtypeStruct((rows, QW), BF16), jax.ShapeDtypeStruct((rows, QW + VW), BF16),
                   jax.ShapeDtypeStruct((rows, LANES), F32),
                   jax.ShapeDtypeStruct((SUBLANES, LANES), F32), jax.ShapeDtypeStruct((SUBLANES, LANES), F32)],
        compiler_params=_params(1),
    )(dq_t, dk, dv, qraw, kvraw, kvraw, lat, qg, kg, *tabs)


def _att_mask_t(qb, kb, bt):
    kpos = kb * bt + lax.broadcasted_iota(jnp.int32, (bt, bt), 0)
    qpos = qb * bt + lax.broadcasted_iota(jnp.int32, (bt, bt), 1)
    return jnp.logical_and(kpos <= qpos, jnp.logical_or(kpos >= PAD, qpos < PAD))


def attn_fwd(name, q, k, vt, comm=None):
    rows = q.shape[0]
    bt = _row_tile(rows)
    nb = rows // bt
    assert bt >= CHUNK

    def body(q_ref, k_ref, vt_ref, o_ref, lse_ref):
        qi = pl.program_id(1)
        lse_ref[...] = jnp.zeros((SUBLANES, bt), F32)

        def scores(kb):
            r0 = pl.multiple_of(kb * bt, LANES)
            return tuple(lax.dot_general(k_ref[pl.ds(r0, bt), hh * LANES:(hh + 1) * LANES],
                                         q_ref[:, hh * LANES:(hh + 1) * LANES], (NT, ((), ())),
                                         preferred_element_type=F32) for hh in range(2))

        def tile(kb, carry, s_pair, masked):
            r0 = pl.multiple_of(kb * bt, LANES)
            new = []
            for hh in range(2):
                m, l, acc = carry[3 * hh:3 * hh + 3]
                vs = slice(hh * MLA_V, (hh + 1) * MLA_V)
                s = s_pair[hh] * (ATT_SCALE * LOG2E)
                if masked:
                    s = jnp.where(_att_mask_t(qi, kb, bt), s, NEG)
                m_new = jnp.maximum(m, jnp.max(s, axis=0, keepdims=True))
                alpha = jnp.exp2(m - m_new)
                p = jnp.exp2(s - m_new)
                l = alpha * l + jnp.sum(p, axis=0, keepdims=True)
                acc = alpha * acc + jnp.dot(vt_ref[vs, pl.ds(r0, bt)], p.astype(BF16), preferred_element_type=F32)
                new += [m_new, l, acc]
            return tuple(new)

        init = (jnp.full((1, bt), NEG, F32), jnp.zeros((1, bt), F32), jnp.zeros((MLA_V, bt), F32)) * 2
        s_next = scores(jnp.minimum(1, qi))
        carry = tile(0, init, scores(0), True)

        def rest(args):
            def mid(kb, state):
                carry, s_cur = state
                s_after = scores(kb + 1)
                return tile(kb, carry, s_cur, False), s_after

            carry, s_last = lax.fori_loop(1, qi, mid, args)
            return tile(qi, carry, s_last, True)

        carry = lax.cond(qi > 0, rest, lambda args: args[0], (carry, s_next))
        for hh in range(2):
            m, l, acc = carry[3 * hh:3 * hh + 3]
            o_ref[hh * MLA_V:(hh + 1) * MLA_V, :] = acc / l
            lse_ref[hh:hh + 1, :] = m * LN2 + jnp.log(l)

    return carrier_call(
        name, body, (N_PAIRS, nb),
        [pl.BlockSpec((bt, HP), lambda p, i: (i, p)),
         pl.BlockSpec((rows, HP), lambda p, i: (0, p)),
         pl.BlockSpec((VP, rows), lambda p, i: (p, 0))],
        [pl.BlockSpec((VP, bt), lambda p, i: (p, i)),
         pl.BlockSpec((None, SUBLANES, bt), lambda p, i: (p, 0, i))],
        [jax.ShapeDtypeStruct((VW, rows), F32), jax.ShapeDtypeStruct((N_PAIRS, SUBLANES, rows), F32)],
        [], (q, k, vt), comm)


def attn_bwd(name, q, k, kt, v, do_t, lse, delta, comm=None):
    rows = q.shape[0]
    bt = _row_tile(rows)
    nb = rows // bt

    def body(q_ref, k_ref, kt_ref, v_ref, do_ref, lse_ref, dl_ref, dq_ref, dk_ref, dv_ref, dk_scr, dv_scr):
        ki = pl.program_id(1)

        @pl.when(ki == 0)
        def _():
            dq_ref[...] = jnp.zeros((HP, rows), F32)

        dk_scr[...] = jnp.zeros((bt, HP), F32)
        dv_scr[...] = jnp.zeros((bt, VP), F32)

        def scores(qb):
            c0 = pl.multiple_of(jnp.minimum(qb, nb - 1) * bt, LANES)
            out = []
            for hh in range(2):
                qs = slice(hh * LANES, (hh + 1) * LANES)
                vs = slice(hh * MLA_V, (hh + 1) * MLA_V)
                out.append(lax.dot_general(k_ref[:, qs], q_ref[pl.ds(c0, bt), qs], (NT, ((), ())),
                                           preferred_element_type=F32))
                out.append(jnp.dot(v_ref[:, vs], do_ref[vs, pl.ds(c0, bt)], preferred_element_type=F32))
            return tuple(out)

        def tile(qb, sd, masked):
            c0 = pl.multiple_of(qb * bt, LANES)
            for hh in range(2):
                qs = slice(hh * LANES, (hh + 1) * LANES)
                vs = slice(hh * MLA_V, (hh + 1) * MLA_V)
                qv = q_ref[pl.ds(c0, bt), qs]
                dov = do_ref[vs, pl.ds(c0, bt)]
                lse = lse_ref[hh:hh + 1, pl.ds(c0, bt)]
                dl = dl_ref[hh:hh + 1, pl.ds(c0, bt)]
                p = jnp.exp(sd[2 * hh] * ATT_SCALE - lse)
                if masked:
                    p = jnp.where(_att_mask_t(qb, ki, bt), p, 0.0)
                ds = (p * (sd[2 * hh + 1] - dl) * ATT_SCALE).astype(BF16)
                dv_scr[:, vs] += lax.dot_general(p.astype(BF16), dov, (NT, ((), ())), preferred_element_type=F32)
                dk_scr[:, qs] += jnp.dot(ds, qv, preferred_element_type=F32)
                dq_ref[qs, pl.ds(c0, bt)] += jnp.dot(kt_ref[qs, :], ds, preferred_element_type=F32)

        def sweep(first, sd, masked):
            def step(qb, sd):
                ahead = scores(qb + 1)
                tile(qb, sd, masked)
                return ahead

            lax.fori_loop(first, nb, step, sd)

        @pl.when(ki == 0)
        def _():
            sweep(0, scores(0), True)

        @pl.when(ki > 0)
        def _():
            ahead = scores(ki + 1)
            tile(ki, scores(ki), True)
            sweep(ki + 1, ahead, False)

        dk_ref[...] = dk_scr[...]
        dv_ref[...] = dv_scr[...]

    stat = pl.BlockSpec((None, SUBLANES, rows), lambda p, i: (p, 0, 0))
    return carrier_call(
        name, body, (N_PAIRS, nb),
        [pl.BlockSpec((rows, HP), lambda p, i: (0, p)),
         pl.BlockSpec((bt, HP), lambda p, i: (i, p)),
         pl.BlockSpec((HP, bt), lambda p, i: (p, i)),
         pl.BlockSpec((bt, VP), lambda p, i: (i, p)),
         pl.BlockSpec((VP, rows), lambda p, i: (p, 0)),
         stat, stat],
        [pl.BlockSpec((HP, rows), lambda p, i: (p, 0)),
         pl.BlockSpec((bt, HP), lambda p, i: (i, p)),
         pl.BlockSpec((bt, VP), lambda p, i: (i, p))],
        [jax.ShapeDtypeStruct((QW, rows), F32), jax.ShapeDtypeStruct((rows, QW), F32),
         jax.ShapeDtypeStruct((rows, VW), F32)],
        [pltpu.VMEM((bt, HP), F32), pltpu.VMEM((bt, VP), F32)],
        (q, k, kt, v, do_t, lse, delta), comm)


def _dot_cast_w(a, w_ref):
    return jnp.dot(a, w_ref[...].astype(BF16), preferred_element_type=F32)


def _dot_cast_wt(a, w_ref):
    return lax.dot_general(a, w_ref[...].astype(BF16), (NT, ((), ())), preferred_element_type=F32)


LAT_W = 768
KPE_BLOCK = MLA_Q_RANK // LANES
KV_BLOCK = (MLA_Q_RANK + LANES) // MLA_KV_RANK


def mla_layer_fwd(tag, h, ln_g, w, tabs, comm=None):
    lat, hn = norm_mm(f"mla_in_{tag}", h, ln_g, w["w_in"], tn=LAT_W)
    qraw, qn = norm_mm(f"mla_q_{tag}", lat, w["q_a"], w["w_q"], tn=512, k_cols=MLA_Q_RANK, col_block=0,
                       out_dtype=BF16)
    kvraw, kvn = norm_mm(f"mla_kv_{tag}", lat, w["kv_a"], w["w_kv"], tn=512, k_cols=MLA_KV_RANK, col_block=KV_BLOCK,
                         out_dtype=BF16)
    q, k, kt, v, vt = qk_prep(f"mla_prep_{tag}", qraw, kvraw, lat, KPE_BLOCK, w["q_norm"], w["k_norm"], tabs)
    (o_t, lse), cres = attn_fwd(f"mla_attn_{tag}", q, k, vt, comm)
    out = res_mm(f"mla_out_{tag}", [(o_t, (VW, None), lambda i, j, kk: (0, i))],
                 lambda r: r[0][...].T.astype(BF16), VW, w["w_out"], h, tn=512, tm=_big_tile(h.shape[0]))
    return out, (h, hn, lat, qn, kvn, qraw, kvraw, q, k, kt, v, o_t, lse), cres


def mla_dwout(tag, dh, saved):
    o_t = saved[11]
    rows, d = dh.shape
    tm = _row_tile(rows)

    def epi_set(acc, e_refs, o_refs, i, j):
        o_refs[0][...] = acc.astype(BF16)

    return fused_mm(
        f"mla_dwout_{tag}", rows=VW, k=rows, n=d, tm=512, tn=d, tk=tm,
        a_ops=[(o_t, (512, tm), lambda i, j, kk: (i, kk))], pro=lambda a, o_, i: a[0][...].astype(BF16),
        w=dh, w_block=(tm, d), w_imap=lambda i, j, kk: (kk, 0), dot=_dot_cast_w, epi=epi_set,
        outs=[((VW, d), BF16, (512, d), lambda i, j, kk: (i, 0))])[0]


def mla_layer_bwd(tag, dh, saved, ln_g, w, tabs, dw_out, comm=None):
    h, hn, lat, qn, kvn, qraw, kvraw, q, k, kt, v, o_t, lse = saved
    rows, d = h.shape
    tm = _row_tile(rows)

    def epi_do(acc, e_refs, o_refs, i, j):
        o_refs[0][...] = acc.astype(BF16)
        prod = acc * e_refs[0][...]
        o_refs[1][...] = jnp.zeros((N_PAIRS, SUBLANES, tm), F32)
        for hd in range(MLA_HEADS):
            o_refs[1][hd // 2, hd % 2:hd % 2 + 1, :] = jnp.sum(prod[hd * MLA_V:(hd + 1) * MLA_V, :], axis=0,
                                                               keepdims=True)

    wo, wo_block, wo_imap = _lw(w["w_out"], (VW, d), lambda i, j, kk: (0, 0))
    do_t, delta = fused_mm(
        f"mla_do_{tag}", rows=VW, k=d, n=rows, tm=VW, tn=tm,
        a_ops=[(wo, wo_block, wo_imap)], pro=lambda a, o_, i: a[0][...],
        w=dh, w_block=(tm, d), w_imap=lambda i, j, kk: (j, 0), dot=_dot_cast_wt,
        e_ops=[(o_t, (VW, tm), lambda i, j, kk: (0, j))], epi=epi_do,
        outs=[((VW, rows), BF16, (VW, tm), lambda i, j, kk: (0, j)),
              ((N_PAIRS, SUBLANES, rows), F32, (N_PAIRS, SUBLANES, tm), lambda i, j, kk: (0, 0, j))])
    (dq_t, dk, dv), cres = attn_bwd(f"mla_dattn_{tag}", q, k, kt, v, do_t, lse, delta, comm)
    dqraw, dkvraw, dpe, dqg, dkg = qk_prep_bwd(f"mla_dprep_{tag}", dq_t, dk, dv, qraw, kvraw, lat, KPE_BLOCK,
                                               w["q_norm"], w["k_norm"], tabs)
    dw_q = simple_wgrad(f"mla_dwq_{tag}", qn, dqraw, t1=MLA_Q_RANK, tn=512)
    dqlat, dqa = rms_bwd_mm(
        f"mla_dqlat_{tag}", [(dqraw, (None, QW), lambda i, j, kk: (i, 0))], lambda r: r[0][...], QW,
        w["w_q"], (MLA_Q_RANK, QW), lambda i, j, kk: (0, 0), _dot_wt, lat, w["q_a"], None,
        h_cols=MLA_Q_RANK, h_col_block=0, add_dh=False)
    dw_kv = simple_wgrad(f"mla_dwkv_{tag}", kvn, dkvraw, t1=MLA_KV_RANK, tn=512)
    dkvlat, dkva = rms_bwd_mm(
        f"mla_dkvlat_{tag}", [(dkvraw, (None, QW + VW), lambda i, j, kk: (i, 0))], lambda r: r[0][...], QW + VW,
        w["w_kv"], (MLA_KV_RANK, QW + VW), lambda i, j, kk: (0, 0), _dot_wt, lat, w["kv_a"], None,
        h_cols=MLA_KV_RANK, h_col_block=KV_BLOCK, add_dh=False)
    dlat = jnp.concatenate([dqlat.astype(BF16), dpe.astype(BF16), dkvlat.astype(BF16)], axis=1)
    dw_in = simple_wgrad(f"mla_dwin_{tag}", hn, dlat, t1=512, tn=LAT_W)
    dh_in, dln = rms_bwd_mm(
        f"mla_dh_{tag}", [(dlat, (None, LAT_W), lambda i, j, kk: (i, 0))], lambda r: r[0][...], LAT_W,
        w["w_in"], (d, LAT_W), lambda i, j, kk: (0, 0), _dot_wt, h, ln_g, dh)
    grads = dict(w_in=dw_in, w_q=dw_q, w_kv=dw_kv, w_out=dw_out, q_a=dqa, kv_a=dkva, q_norm=dqg, k_norm=dkg, ln=dln)
    return dh_in, grads, cres


def loss_head(h, target):
    rows, d = h.shape
    nb = rows // CHUNK

    def body(h_ref, t_ref, l_ref, dh_ref):
        i = pl.program_id(0)
        err = jnp.where(i > 0, h_ref[...] - t_ref[...], 0.0)
        dh_ref[...] = err * (1.0 / d)
        _acc_out(l_ref, _rows8(err * err) * (0.5 / d), i == 0)

    return pl.pallas_call(
        body, name="loss_head", grid=(nb,),
        in_specs=[pl.BlockSpec((CHUNK, d), lambda i: (i, 0)),
                  pl.BlockSpec((CHUNK, d), lambda i: (jnp.maximum(i - 1, 0), 0))],
        out_specs=[pl.BlockSpec((SUBLANES, d), lambda i: (0, 0)), pl.BlockSpec((CHUNK, d), lambda i: (i, 0))],
        out_shape=[jax.ShapeDtypeStruct((SUBLANES, d), F32), jax.ShapeDtypeStruct((rows, d), F32)],
        compiler_params=_params(1),
    )(h, target)


def _adamw(w, g, m, v):
    m = ADAM_B1 * m + (1.0 - ADAM_B1) * g
    v = ADAM_B2 * v + (1.0 - ADAM_B2) * jnp.square(g)
    m_hat = m / (1.0 - ADAM_B1 ** ADAM_STEP)
    v_hat = v / (1.0 - ADAM_B2 ** ADAM_STEP)
    delta = -ADAM_LR * (m_hat / (jnp.sqrt(v_hat) + ADAM_EPS) + ADAM_WD * w)
    return delta, m, v


def reduce_adamw(name, recvs, w, m, v):
    nl, r, c = w.shape
    tr = 128 if r % 128 == 0 else r
    nr = r // tr

    def body(*refs):
        r_refs = refs[:nl]
        w_ref, m_ref, v_ref, g_ref, d_ref, mo_ref, vo_ref = refs[nl:]
        layer = pl.program_id(0)
        for l in range(nl):
            @pl.when(layer == l)
            def _(l=l):
                g = r_refs[l][0].astype(F32)
                for s in range(1, N_DEV):
                    g = g + r_refs[l][s].astype(F32)
                g_ref[...] = g
                d_ref[...], mo_ref[...], vo_ref[...] = _adamw(w_ref[...], g, m_ref[...], v_ref[...])

    def recv_spec(l):
        return pl.BlockSpec((N_DEV, tr, c),
                            lambda y, i: (0, jnp.where(y == l, i, jnp.where(y < l, 0, nr - 1)), 0))

    blk = pl.BlockSpec((None, tr, c), lambda y, i: (y, i, 0))
    return pl.pallas_call(
        body, name=name, grid=(nl, nr),
        in_specs=[recv_spec(l) for l in range(nl)] + [blk, blk, blk],
        out_specs=[blk] * 4, out_shape=[jax.ShapeDtypeStruct((nl, r, c), F32)] * 4,
        compiler_params=_params(2),
    )(*recvs, w, m, v)


def small_reduce(recv):
    def body(r_ref, o_ref):
        g = r_ref[0]
        for s in range(1, N_DEV):
            g = g + r_ref[s]
        o_ref[...] = g

    return pl.pallas_call(body, name="small_reduce", out_shape=jax.ShapeDtypeStruct(recv.shape[1:], F32))(recv)


def small_adamw(w, g, m, v):
    def body(w_ref, g_ref, m_ref, v_ref, d_ref, mo_ref, vo_ref):
        d_ref[...], mo_ref[...], vo_ref[...] = _adamw(w_ref[...], g_ref[...], m_ref[...], v_ref[...])

    return pl.pallas_call(body, name="small_adamw", out_shape=[jax.ShapeDtypeStruct(w.shape, F32)] * 3)(w, g, m, v)


def _pack(parts):
    flat, meta, off = [], [], 0
    for p in parts:
        n = int(np.prod(p.shape))
        flat.append(p.reshape(-1).astype(F32))
        meta.append((off, p.shape))
        off += n
    total = -(-off // (SUBLANES * LANES)) * (SUBLANES * LANES)
    flat.append(jnp.zeros((total - off,), F32))
    return jnp.concatenate(flat).reshape(total // LANES, LANES), meta


def _unpack(packed, meta):
    flat = packed.reshape(-1)
    return [flat[off:off + int(np.prod(shape))].reshape(shape) for off, shape in meta]


MESH = pl.DeviceIdType.MESH
N_PEERS = N_DEV - 1


def _me():
    return lax.axis_index("x"), lax.axis_index("y"), lax.axis_index("c")


def _peer(k):
    x, y, c = _me()
    return (1 - x if k & 4 else x, 1 - y if k & 2 else y, 1 - c if k & 1 else c)


def _dev_index(pos):
    return 4 * pos[0] + 2 * pos[1] + pos[2]


RELAYED = (3, 5, 7)


def make_comm(items):
    n = len(items)

    def part(ref, a, idx):
        rows = items[a][1]
        if rows == "all":
            return ref
        return ref.at[idx] if rows is None else ref.at[pl.ds(idx * rows, rows)]

    def part_shape(a):
        arr, rows = items[a]
        if rows == "all":
            return arr.shape
        return arr.shape[1:] if rows is None else (rows,) + arr.shape[1:]

    def run(phase, ins, outs, send_sems, recv_sems, local_sems):
        me = _dev_index(_me())

        def copy(a, k, src, slot, to):
            return pltpu.make_async_remote_copy(
                src_ref=src, dst_ref=outs[a].at[slot], send_sem=send_sems.at[a, k - 1],
                recv_sem=recv_sems.at[a, k - 1], device_id=to, device_id_type=MESH)

        for a in range(n):
            gather = items[a][1] == "all"
            local = pltpu.make_async_copy(part(ins[a], a, me), outs[a].at[me], local_sems.at[a])
            if phase == "start":
                local.start()
            for k in range(1, N_DEV):
                if gather and k in RELAYED:
                    continue
                peer = _peer(k)
                if phase == "start":
                    copy(a, k, part(ins[a], a, _dev_index(peer)), me, peer).start()
                else:
                    cp = copy(a, k, part(ins[a], a, me), _dev_index(peer), peer)
                    cp.wait_recv()
                    cp.wait_send()
            if phase == "wait":
                local.wait()
                if gather:
                    sibling = _peer(1)
                    relays = []
                    for k in RELAYED:
                        origin = _dev_index(_peer(k - 1))
                        relays.append(copy(a, k, outs[a].at[origin], origin, sibling))
                        relays[-1].start()
                    for k in RELAYED:
                        copy(a, k, ins[a], _dev_index(_peer(k)), sibling).wait_recv()
                    for cp in relays:
                        cp.wait_send()

    return dict(
        ins=[it[0] for it in items],
        outs=[jax.ShapeDtypeStruct((N_DEV,) + part_shape(a), items[a][0].dtype) for a in range(n)],
        sems=[pltpu.SemaphoreType.DMA((n, N_PEERS)), pltpu.SemaphoreType.DMA((n, N_PEERS)),
              pltpu.SemaphoreType.DMA((n,))],
        run=run)


ANY_SPEC = pl.BlockSpec(memory_space=pl.ANY)


def comm_call(name, comm):
    n, no = len(comm["ins"]), len(comm["outs"])

    def body(*refs):
        comm["run"]("start", refs[:n], refs[n:n + no], *refs[n + no:])
        comm["run"]("wait", refs[:n], refs[n:n + no], *refs[n + no:])

    return pl.pallas_call(
        body, name=name, in_specs=[ANY_SPEC] * n, out_specs=[ANY_SPEC] * no, out_shape=comm["outs"],
        scratch_shapes=comm["sems"])(*comm["ins"])


def with_comm(comm, body, grid, in_specs, out_specs, out_shape, scratch_shapes):
    if comm is None:
        return body, in_specs, out_specs, out_shape, scratch_shapes, [], len(out_shape)
    n_in, n_out, n_scr = len(in_specs), len(out_shape), len(scratch_shapes)
    ci, co = len(comm["ins"]), len(comm["outs"])

    def wrapped(*refs):
        ins, cins = refs[:n_in], refs[n_in:n_in + ci]
        outs = refs[n_in + ci:n_in + ci + n_out]
        couts = refs[n_in + ci + n_out:n_in + ci + n_out + co]
        rest = refs[n_in + ci + n_out + co:]
        scr, sems = rest[:n_scr], rest[n_scr:]
        first = functools.reduce(jnp.logical_and, [pl.program_id(a) == 0 for a in range(len(grid))])
        last = functools.reduce(jnp.logical_and, [pl.program_id(a) == grid[a] - 1 for a in range(len(grid))])

        @pl.when(first)
        def _():
            comm["run"]("start", cins, couts, *sems)

        body(*ins, *outs, *scr)

        @pl.when(last)
        def _():
            comm["run"]("wait", cins, couts, *sems)

    return (wrapped, list(in_specs) + [ANY_SPEC] * ci, list(out_specs) + [ANY_SPEC] * co,
            list(out_shape) + list(comm["outs"]), list(scratch_shapes) + list(comm["sems"]), list(comm["ins"]), n_out)


WEIGHTS = ['meta_tokens', 'ln_mix', 'ln_mlp', 'ssd_w_in', 'ssd_conv_w', 'ssd_conv_b', 'ssd_dt_bias', 'ssd_a_log',
           'ssd_d', 'ssd_norm', 'ssd_w_out', 'mla_w_in', 'mla_q_a_norm', 'mla_w_q_b', 'mla_kv_a_norm', 'mla_w_kv_b',
           'mla_q_norm', 'mla_k_norm', 'mla_w_out', 'mlp_w_up', 'mlp_w_down']
BIG = ['ssd_w_in', 'ssd_w_out', 'mla_w_in', 'mla_w_q_b', 'mla_w_kv_b', 'mla_w_out', 'mlp_w_up', 'mlp_w_down']
SMALL_SHARDED = ['meta_tokens', 'ssd_conv_w', 'mla_q_a_norm', 'mla_kv_a_norm']
SMALL_REPL = ['ln_mix', 'ln_mlp', 'ssd_conv_b', 'ssd_dt_bias', 'ssd_a_log', 'ssd_d', 'ssd_norm', 'mla_q_norm',
              'mla_k_norm']
SMALL = SMALL_REPL + SMALL_SHARDED
SSD_IN_PAD = 6272
SSD_IN_TN = 896
MLA_IN = MLA_Q_RANK + MLA_KV_RANK + MLA_ROPE


def _pad_last(v, n):
    return jnp.pad(v, [(0, 0)] * (v.ndim - 1) + [(0, n - v.shape[-1])])


SSD_BIG = ['ssd_w_in', 'ssd_w_out']
MLA_BIG = ['mla_w_in', 'mla_w_q_b', 'mla_w_kv_b', 'mla_w_out']
MLP_BIG = ['mlp_w_up', 'mlp_w_down']


def _mix_big(i):
    return [(n, i // 2) for n in (SSD_BIG if i % 2 == 0 else MLA_BIG)]


def _mlp_big(i):
    return [(n, i) for n in MLP_BIG]


def _mix_weights(i, gw, W, full):
    j = i // 2
    d = W['ln_mix'].shape[-1]
    if i % 2 == 0:
        wi = gw[('ssd_w_in', j)].transpose(1, 0, 2).reshape(d, -1)
        par = jnp.concatenate([_pad_last(W[n][j][None], LANES) for n in ('ssd_dt_bias', 'ssd_a_log', 'ssd_d')]
                              + [jnp.zeros((SUBLANES - 3, LANES), F32)])
        return dict(w_in=_pad_last(wi, SSD_IN_PAD), conv_w=full['ssd_conv_w'][j], conv_b=W['ssd_conv_b'][j][None],
                    par=par, norm=W['ssd_norm'][j][None], w_out=gw[('ssd_w_out', j)].reshape(SSD_D_INNER, d))
    wi = gw[('mla_w_in', j)].reshape(d, MLA_IN)
    kpe = jnp.pad(wi[:, MLA_Q_RANK + MLA_KV_RANK:], ((0, 0), (MLA_NOPE, LANES - MLA_QK)))
    wq = gw[('mla_w_q_b', j)].transpose(1, 0, 2).reshape(MLA_Q_RANK, MLA_HEADS, MLA_QK)
    wkv = gw[('mla_w_kv_b', j)].transpose(1, 0, 2).reshape(MLA_KV_RANK, MLA_HEADS, MLA_NOPE + MLA_V)
    return dict(
        w_in=jnp.concatenate([wi[:, :MLA_Q_RANK], kpe, wi[:, MLA_Q_RANK:MLA_Q_RANK + MLA_KV_RANK]], axis=1),
        w_q=_pad_last(wq, LANES).reshape(MLA_Q_RANK, QW),
        w_kv=jnp.concatenate([_pad_last(wkv[..., :MLA_NOPE], LANES).reshape(MLA_KV_RANK, QW),
                              wkv[..., MLA_NOPE:].reshape(MLA_KV_RANK, VW)], axis=1),
        w_out=gw[('mla_w_out', j)].reshape(VW, d), q_a=full['mla_q_a_norm'][j][None],
        kv_a=full['mla_kv_a_norm'][j][None],
        q_norm=_pad_last(W['mla_q_norm'][j][None], LANES), k_norm=_pad_last(W['mla_k_norm'][j][None], LANES))


def _step(x, target, W, M, V):
    d = x.shape[-1]
    me = _dev_index(_me())
    depth = W['ln_mix'].shape[0]

    def gather_keys(i):
        return _mlp_big(i) + (_mix_big(i + 1) if i + 1 < depth else [])

    def gather_items(keys):
        return [(W[n][l].astype(BF16), "all") for n, l in keys]

    small_pack, small_meta = _pack([W[n] for n in SMALL_SHARDED])
    got = comm_call("gather_0", make_comm(gather_items(_mix_big(0)) + [(small_pack, "all")]))
    per_dev = [_unpack(got[-1][s], small_meta) for s in range(N_DEV)]
    full = {n: jnp.concatenate([per_dev[s][i] for s in range(N_DEV)], axis=-1) for i, n in enumerate(SMALL_SHARDED)}
    gw = dict(zip(_mix_big(0), got))

    h = jnp.concatenate([jnp.zeros((PAD, d), F32), full['meta_tokens'], x], axis=0)
    rows = h.shape[0]
    tabs = rope_tables(rows)
    saved, weights = [], []
    for i in range(depth):
        comm = make_comm(gather_items(gather_keys(i)))
        mix = _mix_weights(i, gw, W, full)
        if i % 2 == 0:
            h, s_mix, got = ssd_layer_fwd(f"{i}", h, W['ln_mix'][i][None], mix, comm)
        else:
            h, s_mix, got = mla_layer_fwd(f"{i}", h, W['ln_mix'][i][None], mix, tabs, comm)
        gw.update(zip(gather_keys(i), got))
        up, down = gw[('mlp_w_up', i)], gw[('mlp_w_down', i)].reshape(-1, d)
        h, s_mlp = mlp_fwd(f"{i}", h, W['ln_mlp'][i][None], up, down)
        saved.append((s_mix, s_mlp))
        weights.append((mix, up, down))
    loss_part, dh = loss_head(h, target)
    loss = lax.psum(jnp.sum(loss_part), ("x", "y", "c"))

    recv = {}
    pending = []
    small = {n: [None] * W[n].shape[0] for n in SMALL if n != 'meta_tokens'}
    for i in reversed(range(depth)):
        j = i // 2
        s_mix, s_mlp = saved[i]
        mix, up, down = weights[i]
        dh, dw_up, dw_down, dg = mlp_bwd(f"{i}", dh, s_mlp, W['ln_mlp'][i][None], up, down)
        small['ln_mlp'][i] = dg.sum(0)
        pending += list(zip(_mlp_big(i), [(dw_up, None), (dw_down, down.shape[0] // N_DEV)]))
        dw_out = (ssd_dwout if i % 2 == 0 else mla_dwout)(f"{i}", dh, s_mix)
        pending.append((_mix_big(i)[-1], (dw_out, dw_out.shape[0] // N_DEV)))
        comm = make_comm([it for _, it in pending])
        sends = []
        if i % 2 == 0:
            dh, g, got = ssd_layer_bwd(f"{i}", dh, s_mix, W['ln_mix'][i][None], mix, dw_out, comm)
            n_in = W['ssd_w_in'].shape[-1]
            sends.append((g['w_in'][:, :N_DEV * n_in].reshape(d, N_DEV, n_in).transpose(1, 0, 2), None))
            small['ssd_conv_w'][j] = g['conv_w'].reshape(SSD_CONV, SUBLANES, -1).sum(1)
            small['ssd_conv_b'][j] = g['conv_b'].sum(0)
            small['ssd_dt_bias'][j] = g['par'][0, :SSD_HEADS]
            small['ssd_a_log'][j] = g['par'][1, :SSD_HEADS]
            small['ssd_d'][j] = g['par'][2, :SSD_HEADS]
            small['ssd_norm'][j] = g['norm'].sum(0)
        else:
            dh, g, got = mla_layer_bwd(f"{i}", dh, s_mix, W['ln_mix'][i][None], mix, tabs, dw_out, comm)
            gi = g['w_in']
            gi = jnp.concatenate([gi[:, :MLA_Q_RANK], gi[:, MLA_Q_RANK + LANES:],
                                  gi[:, MLA_Q_RANK + MLA_NOPE:MLA_Q_RANK + MLA_QK]], axis=1)
            sends.append((gi, d // N_DEV))
            gq = g['w_q'].reshape(MLA_Q_RANK, MLA_HEADS, LANES)[..., :MLA_QK]
            sends.append((gq.reshape(MLA_Q_RANK, N_DEV, -1).transpose(1, 0, 2), None))
            gkv = jnp.concatenate([g['w_kv'][:, :QW].reshape(MLA_KV_RANK, MLA_HEADS, LANES)[..., :MLA_NOPE],
                                   g['w_kv'][:, QW:].reshape(MLA_KV_RANK, MLA_HEADS, MLA_V)], axis=-1)
            sends.append((gkv.reshape(MLA_KV_RANK, N_DEV, -1).transpose(1, 0, 2), None))
            small['mla_q_a_norm'][j] = g['q_a'].sum(0)
            small['mla_kv_a_norm'][j] = g['kv_a'].sum(0)
            small['mla_q_norm'][j] = g['q_norm'].sum(0)[:MLA_QK]
            small['mla_k_norm'][j] = g['k_norm'].sum(0)[:MLA_QK]
        small['ln_mix'][i] = g['ln'].sum(0)
        recv.update({key: a for (key, _), a in zip(pending, got)})
        pending = list(zip(_mix_big(i)[:-1], sends))
    grad_x = dh[CHUNK:]
    small_full = {n: jnp.stack(v) for n, v in small.items()}
    small_full['meta_tokens'] = dh[PAD:CHUNK]

    gpack, gmeta = _pack([small_full[n] for n in SMALL])
    got = comm_call("exchange_0", make_comm([it for _, it in pending] + [(gpack, "all")]))
    recv.update({key: a for (key, _), a in zip(pending, got)})
    res = {}
    for n in BIG:
        res[n] = reduce_adamw(f"adamw_{n}", [recv[(n, l)] for l in range(W[n].shape[0])], W[n], M[n], V[n])
    gsum = dict(zip(SMALL, _unpack(small_reduce(got[-1]), gmeta)))
    for n in SMALL_SHARDED:
        wl = W[n].shape[-1]
        gsum[n] = lax.dynamic_slice_in_dim(gsum[n], me * wl, wl, axis=gsum[n].ndim - 1)
    wp, wmeta = _pack([W[n] for n in SMALL])
    gp, _ = _pack([gsum[n] for n in SMALL])
    mp, _ = _pack([M[n] for n in SMALL])
    vp, _ = _pack([V[n] for n in SMALL])
    upd = [_unpack(o, wmeta) for o in small_adamw(wp, gp, mp, vp)]
    for a, n in enumerate(SMALL):
        res[n] = [gsum[n], upd[0][a], upd[1][a], upd[2][a]]
    return (loss, grad_x[None]) + tuple(res[n][q] for q in range(4) for n in WEIGHTS)


def kernel(x, meta_tokens, ln_mix, ln_mlp, ssd_w_in, ssd_conv_w, ssd_conv_b, ssd_dt_bias, ssd_a_log, ssd_d, ssd_norm, ssd_w_out, mla_w_in, mla_q_a_norm, mla_w_q_b, mla_kv_a_norm, mla_w_kv_b, mla_q_norm, mla_k_norm, mla_w_out, mlp_w_up, mlp_w_down, loss_target, m_meta_tokens, m_ln_mix, m_ln_mlp, m_ssd_w_in, m_ssd_conv_w, m_ssd_conv_b, m_ssd_dt_bias, m_ssd_a_log, m_ssd_d, m_ssd_norm, m_ssd_w_out, m_mla_w_in, m_mla_q_a_norm, m_mla_w_q_b, m_mla_kv_a_norm, m_mla_w_kv_b, m_mla_q_norm, m_mla_k_norm, m_mla_w_out, m_mlp_w_up, m_mlp_w_down, v_meta_tokens, v_ln_mix, v_ln_mlp, v_ssd_w_in, v_ssd_conv_w, v_ssd_conv_b, v_ssd_dt_bias, v_ssd_a_log, v_ssd_d, v_ssd_norm, v_ssd_w_out, v_mla_w_in, v_mla_q_a_norm, v_mla_w_q_b, v_mla_kv_a_norm, v_mla_w_kv_b, v_mla_q_norm, v_mla_k_norm, v_mla_w_out, v_mlp_w_up, v_mlp_w_down):
    given = dict(locals())
    W = {n: given[n] for n in WEIGHTS}
    M = {n: given["m_" + n] for n in WEIGHTS}
    V = {n: given["v_" + n] for n in WEIGHTS}
    return _step(x[0], loss_target[0], W, M, V)
```

```python
import functools

import jax
import jax.numpy as jnp
import numpy as np
from jax import lax
from jax.experimental import pallas as pl
from jax.experimental.pallas import tpu as pltpu

F32 = jnp.float32
BF16 = jnp.bfloat16

EPS = 1e-6
N_META = 16
CHUNK = 128
PAD = CHUNK - N_META
SSD_HEAD_DIM = 64
SSD_HEADS = 32
SSD_GROUPS = 8
SSD_HPG = 4
SSD_STATE = 128
SSD_D_INNER = 2048
SSD_CONV = 4
MLA_HEADS = 16
MLA_NOPE = 64
MLA_ROPE = 32
MLA_V = 64
MLA_QK = 96
MLA_Q_RANK = 384
MLA_KV_RANK = 256
ROPE_THETA = 10000.0
LANES = 128
SUBLANES = 8
N_DEV = 8
VMEM_LIMIT = 56 * 1024 * 1024

ADAM_LR = 0.001
ADAM_B1 = 0.9
ADAM_B2 = 0.999
ADAM_EPS = 1e-08
ADAM_WD = 0.01
ADAM_STEP = 10

NEG = -1e30


def _row_tile(rows):
    return 384 if (rows % 384 == 0 and rows > 384) else 128


def _big_tile(rows):
    return 1408 if rows % 1408 == 0 else _row_tile(rows)


def _params(n_axes, vmem=VMEM_LIMIT):
    return pltpu.CompilerParams(dimension_semantics=("arbitrary",) * n_axes, vmem_limit_bytes=vmem)


def _dot(a, b, dims):
    return lax.dot_general(a.astype(BF16), b.astype(BF16), (dims, ((), ())), preferred_element_type=F32)


NN = ((1,), (0,))
NT = ((1,), (1,))
TN = ((0,), (0,))


@jax.custom_vjp
def bdot_nn(a, b):
    return _dot(a, b, NN)


@jax.custom_vjp
def bdot_nt(a, b):
    return _dot(a, b, NT)


@jax.custom_vjp
def bdot_tn(a, b):
    return _dot(a, b, TN)


bdot_nn.defvjp(lambda a, b: (_dot(a, b, NN), (a, b)),
               lambda r, g: (_dot(g, r[1], NT), _dot(r[0], g, TN)))
bdot_nt.defvjp(lambda a, b: (_dot(a, b, NT), (a, b)),
               lambda r, g: (_dot(g, r[1], NN), _dot(g, r[0], TN)))
bdot_tn.defvjp(lambda a, b: (_dot(a, b, TN), (a, b)),
               lambda r, g: (_dot(r[1], g, NT), _dot(r[0], g, NN)))


def _rows8(v):
    r, n = v.shape
    return v.reshape(r // SUBLANES, SUBLANES, n).sum(axis=0)


def _row_mask(i, tm):
    return (i * tm + lax.broadcasted_iota(jnp.int32, (tm, 1), 0)) >= PAD


def fused_mm(name, *, rows, k, n, tm, tn, tk=None, a_ops, pro, w, w_block, w_imap, dot, e_ops=(), epi, outs):
    tk = tk or k
    ni, nj, nk = rows // tm, n // tn, k // tk
    assert rows % tm == 0 and n % tn == 0 and k % tk == 0
    assert nk == 1 or nj == 1
    cache = nk == 1 and nj > 1
    na, ne, no = len(a_ops), len(e_ops), len(outs)

    def body(*refs):
        a_refs = refs[:na]
        w_ref = refs[na]
        e_refs = refs[na + 1:na + 1 + ne]
        o_refs = refs[na + 1 + ne:na + 1 + ne + no]
        scr = refs[na + 1 + ne + no:]
        i, j, kk = pl.program_id(0), pl.program_id(1), pl.program_id(2)
        if cache:
            a_scr = scr[0]

            @pl.when(j == 0)
            def _():
                a_scr[...] = pro(a_refs, o_refs, i)

            a = a_scr[...]
        else:
            a = pro(a_refs, o_refs, i)
        part = dot(a, w_ref)
        if nk == 1:
            epi(part, e_refs, o_refs, i, j)
        else:
            acc_ref = scr[0]

            @pl.when(kk == 0)
            def _():
                acc_ref[...] = part

            @pl.when(kk > 0)
            def _():
                acc_ref[...] += part

            @pl.when(kk == nk - 1)
            def _():
                epi(acc_ref[...], e_refs, o_refs, i, j)

    scratch = []
    if cache:
        scratch.append(pltpu.VMEM((tm, k), BF16))
    if nk > 1:
        scratch.append(pltpu.VMEM((tm, tn), F32))
    in_specs = [pl.BlockSpec(b, m) for (_, b, m) in a_ops]
    in_specs.append(pl.BlockSpec(w_block, w_imap))
    in_specs += [pl.BlockSpec(b, m) for (_, b, m) in e_ops]
    return pl.pallas_call(
        body, name=name, grid=(ni, nj, nk),
        in_specs=in_specs,
        out_specs=[pl.BlockSpec(b, m) for (_, _, b, m) in outs],
        out_shape=[jax.ShapeDtypeStruct(s, d) for (s, d, _, _) in outs],
        scratch_shapes=scratch,
        compiler_params=_params(3),
    )(*[a for (a, _, _) in a_ops], w, *[e for (e, _, _) in e_ops])


def _lw(w, block, imap):
    if isinstance(w, tuple):
        arr, layer = w
        return arr, (None,) + block, (lambda i, j, kk: (layer,) + imap(i, j, kk))
    return w, block, imap


def _dot_w(a, w_ref):
    return jnp.dot(a, w_ref[...], preferred_element_type=F32)


def _dot_wt(a, w_ref):
    return lax.dot_general(a, w_ref[...], (NT, ((), ())), preferred_element_type=F32)


def _rms_pro(h, g):
    r = lax.rsqrt(jnp.mean(h * h, axis=-1, keepdims=True) + EPS)
    return h * r * g


def _rms_bwd(dyn, h, g):
    r = lax.rsqrt(jnp.mean(h * h, axis=-1, keepdims=True) + EPS)
    xh = h * r
    t = dyn * g
    dh = r * (t - xh * jnp.mean(t * xh, axis=-1, keepdims=True))
    return dh, _rows8(dyn * xh)


def _acc_out(ref, val, first):
    @pl.when(first)
    def _():
        ref[...] = val

    @pl.when(jnp.logical_not(first))
    def _():
        ref[...] += val


def norm_mm(name, h, g, w, *, tn, k_cols=None, col_block=0, w_stacked=False, out_dtype=F32):
    rows = h.shape[0]
    k = k_cols or h.shape[1]
    wshape = (w[0].shape[1:] if isinstance(w, tuple) else w.shape)
    n = wshape[0] * wshape[2] if w_stacked else wshape[1]
    tm = _big_tile(rows)

    def pro(a_refs, o_refs, i):
        hn = _rms_pro(a_refs[0][...], a_refs[1][...]).astype(BF16)
        o_refs[1][...] = hn
        return hn

    def epi(acc, e_refs, o_refs, i, j):
        o_refs[0][...] = acc.astype(out_dtype)

    if w_stacked:
        w_block, w_imap = (None, k, tn), (lambda i, j, kk: (j, 0, 0))
    else:
        w_block, w_imap = (k, tn), (lambda i, j, kk: (0, j))
    w, w_block, w_imap = _lw(w, w_block, w_imap)
    return fused_mm(
        name, rows=rows, k=k, n=n, tm=tm, tn=tn,
        a_ops=[(h, (tm, k), lambda i, j, kk: (i, col_block)), (g, (1, k), lambda i, j, kk: (0, 0))],
        pro=pro, w=w, w_block=w_block, w_imap=w_imap, dot=_dot_w, epi=epi,
        outs=[((rows, n), out_dtype, (tm, tn), lambda i, j, kk: (i, j)),
              ((rows, k), BF16, (tm, k), lambda i, j, kk: (i, 0))])


def res_mm(name, a_ops, pro, k, w, res, *, tn, save_dtype=None, tm=None, tk=None):
    rows, n = res.shape
    tm = tm or _row_tile(rows)
    assert tk is None or save_dtype is None

    def pro2(a_refs, o_refs, i):
        a = pro(a_refs)
        if save_dtype is not None:
            o_refs[1][...] = a
        return a

    def epi(acc, e_refs, o_refs, i, j):
        o_refs[0][...] = e_refs[0][...] + acc

    outs = [((rows, n), F32, (tm, tn), lambda i, j, kk: (i, j))]
    if save_dtype is not None:
        outs.append(((rows, k), save_dtype, (tm, k), lambda i, j, kk: (i, 0)))
    w, w_block, w_imap = _lw(w, (tk or k, tn), lambda i, j, kk: (kk, j))
    out = fused_mm(
        name, rows=rows, k=k, n=n, tm=tm, tn=tn, tk=tk,
        a_ops=[(a, tuple(tm if x is None else x for x in b), m) for (a, b, m) in a_ops],
        pro=pro2, w=w, w_block=w_block, w_imap=w_imap, dot=_dot_w,
        e_ops=[(res, (tm, tn), lambda i, j, kk: (i, j))], epi=epi, outs=outs)
    return out if save_dtype is not None else out[0]


def wgrad_mm(name, a_ops, pro_a, g_ops, pro_g, *, rows, k1, n, t1, tn, out_shape=None, out_block=None, out_imap=None):
    tt = _row_tile(rows)
    n1, n2, nt = k1 // t1, n // tn, rows // tt
    assert k1 % t1 == 0 and n % tn == 0
    na = len(a_ops)

    def body(*refs):
        a_refs = refs[:na]
        g_refs = refs[na:-2]
        o_ref, acc = refs[-2:]
        t = pl.program_id(2)
        a = pro_a(a_refs).astype(BF16)
        g = pro_g(g_refs).astype(BF16)
        _acc_out(acc, lax.dot_general(a, g, (TN, ((), ())), preferred_element_type=F32), t == 0)

        @pl.when(t == nt - 1)
        def _():
            if len(o_ref.shape) == 3:
                ws = o_ref.shape[2]
                for q in range(o_ref.shape[0]):
                    o_ref[q] = acc[:, q * ws:(q + 1) * ws].astype(BF16)
            else:
                o_ref[...] = acc[...].astype(BF16)

    return pl.pallas_call(
        body, name=name, grid=(n1, n2, nt),
        in_specs=[pl.BlockSpec(b, m) for (_, b, m) in list(a_ops) + list(g_ops)],
        out_specs=pl.BlockSpec(out_block or (t1, tn), out_imap or (lambda a, b, t: (a, b))),
        out_shape=jax.ShapeDtypeStruct(out_shape or (k1, n), BF16),
        scratch_shapes=[pltpu.VMEM((t1, tn), F32)],
        compiler_params=_params(3),
    )(*[a for (a, _, _) in list(a_ops) + list(g_ops)])


def simple_wgrad(name, a, g, *, a_cols=None, a_col_block=0, t1=None, tn=None, **kw):
    rows = a.shape[0]
    k1 = a_cols or a.shape[1]
    n = g.shape[1]
    tt = _row_tile(rows)
    t1 = t1 or min(k1, 512)
    tn = tn or min(n, 1024)
    return wgrad_mm(
        name,
        [(a, (tt, t1), lambda x, y, t: (t, x + a_col_block * (k1 // t1)))], lambda r: r[0][...],
        [(g, (tt, tn), lambda x, y, t: (t, y))], lambda r: r[0][...],
        rows=rows, k1=k1, n=n, t1=t1, tn=tn, **kw)


def rms_bwd_mm(name, dz_ops, pro, k, w, w_block, w_imap, dot, h, g, dh, *, tk=None, h_cols=None, h_col_block=0,
               add_dh=True, tm=None):
    rows = h.shape[0]
    n = h_cols or h.shape[1]
    tm = tm or _big_tile(rows)
    ni = rows // tm
    w, w_block, w_imap = _lw(w, w_block, w_imap)

    def epi(acc, e_refs, o_refs, i, j):
        d, dg = _rms_bwd(acc, e_refs[0][...], e_refs[1][...])
        if add_dh:
            d = d + e_refs[2][...]
        o_refs[0][...] = jnp.where(_row_mask(i, tm), d, 0.0)
        _acc_out(o_refs[1], dg, i == 0)

    e_ops = [(h, (tm, n), lambda i, j, kk: (i, h_col_block)), (g, (1, n), lambda i, j, kk: (0, 0))]
    if add_dh:
        e_ops.append((dh, (tm, n), lambda i, j, kk: (i, 0)))
    return fused_mm(
        name, rows=rows, k=k, n=n, tm=tm, tn=n, tk=tk,
        a_ops=[(a, tuple(tm if x is None else x for x in b), m) for (a, b, m) in dz_ops],
        pro=lambda a_refs, o_refs, i: pro(a_refs), w=w, w_block=w_block, w_imap=w_imap, dot=dot,
        e_ops=e_ops, epi=epi,
        outs=[((rows, n), F32, (tm, n), lambda i, j, kk: (i, 0)),
              ((SUBLANES, n), F32, (SUBLANES, n), lambda i, j, kk: (0, 0))])


def _relu2(u):
    r = jnp.maximum(u.astype(F32), 0.0)
    return r * r


def mlp_fwd(tag, h, g, w_up_st, w_down):
    d_ff = w_down.shape[0]
    u, hn = norm_mm(f"mlp_up_{tag}", h, g, w_up_st, tn=w_up_st.shape[2], w_stacked=True, out_dtype=BF16)
    out = res_mm(f"mlp_down_{tag}", [(u, (None, 512), lambda i, j, kk: (i, kk))],
                 lambda r: _relu2(r[0][...]).astype(BF16), d_ff, w_down, h, tn=h.shape[1],
                 tm=_big_tile(h.shape[0]), tk=512)
    return out, (h, hn, u)


def mlp_bwd(tag, dh, saved, g, w_up_st, w_down):
    h, hn, u = saved
    rows, d = h.shape
    d_ff = w_down.shape[0]
    ts = w_up_st.shape[2]
    tm = _big_tile(rows)
    tt = _row_tile(rows)
    wd, wd_block, wd_imap = _lw(w_down, (512, d), lambda i, j, kk: (j, 0))

    def epi_du(acc, e_refs, o_refs, i, j):
        o_refs[0][...] = (acc * (2.0 * jnp.maximum(e_refs[0][...].astype(F32), 0.0))).astype(BF16)

    du, = fused_mm(
        f"mlp_du_{tag}", rows=rows, k=d, n=d_ff, tm=tm, tn=512,
        a_ops=[(dh, (tm, d), lambda i, j, kk: (i, 0))], pro=lambda a, o, i: a[0][...].astype(BF16),
        w=wd, w_block=wd_block, w_imap=wd_imap, dot=_dot_wt,
        e_ops=[(u, (tm, 512), lambda i, j, kk: (i, j))], epi=epi_du,
        outs=[((rows, d_ff), BF16, (tm, 512), lambda i, j, kk: (i, j))])
    half = d_ff // 2
    dw_down = wgrad_mm(
        f"mlp_dwdown_{tag}",
        [(u, (tt, half), lambda a, b, t: (t, a))], lambda r: _relu2(r[0][...]),
        [(dh, (tt, d), lambda a, b, t: (t, 0))], lambda r: r[0][...],
        rows=rows, k1=d_ff, n=d, t1=half, tn=d)
    dw_up = simple_wgrad(f"mlp_dwup_{tag}", hn, du, t1=d, tn=half, out_shape=(N_DEV, d, ts),
                         out_block=(half // ts, d, ts), out_imap=lambda a, b, t: (b, 0, 0))
    dh_in, dg = rms_bwd_mm(
        f"mlp_dh_{tag}", [(du, (None, ts), lambda i, j, kk: (i, kk))], lambda r: r[0][...], d_ff,
        w_up_st, (None, d, ts), lambda i, j, kk: (kk, 0, 0), _dot_wt, h, g, dh, tk=ts)
    return dh_in, dw_up, dw_down, dg


CONV_HALO = SUBLANES
CONV_W = 2 * SSD_D_INNER
CONV_BLK = 512


def _silu(x):
    return x * jax.nn.sigmoid(x)


def _conv_pre(ext_ref, w, b, tm):
    pre = b
    for k in range(SSD_CONV):
        pre = pre + w[k:k + 1, :] * ext_ref[pl.ds(CONV_HALO - (SSD_CONV - 1) + k, tm), :]
    return pre


def _conv_load(ext, parts, c):
    for cur, halo, lo, hi in parts:
        ext[pl.ds(0, CONV_HALO), lo:hi] = jnp.where(c > 0, halo[...], 0.0)
        ext[pl.ds(CONV_HALO, CHUNK), lo:hi] = cur[...]


def _conv_specs(idx):
    hb = CHUNK // CONV_HALO
    specs = []
    for w, blk in ((SSD_D_INNER, 1), (BC_W, SSD_D_INNER // BC_W + 2), (BC_W, SSD_D_INNER // BC_W + 3)):
        specs.append(pl.BlockSpec((CHUNK, w), lambda s, blk=blk: (idx(s), blk)))
        specs.append(pl.BlockSpec((CONV_HALO, w), lambda s, blk=blk: (jnp.maximum(idx(s) * hb - 1, 0), blk)))
    return specs


CONV_COLS = ((0, SSD_D_INNER), (SSD_D_INNER, SSD_D_INNER + SSD_GROUPS * SSD_STATE),
             (SSD_D_INNER + SSD_GROUPS * SSD_STATE, 2 * SSD_D_INNER))


@functools.partial(jax.custom_vjp, nondiff_argnums=(1,))
def _sub_row(x, h):
    return x[h:h + 1, :]


_sub_row.defvjp(
    lambda x, h: (x[h:h + 1, :], None),
    lambda h, _, g: (jnp.where(lax.broadcasted_iota(jnp.int32, (LANES, 1), 0) == h, g, 0.0),))


def _splitter(axis, size, count):
    def blocks(x):
        return tuple(lax.slice_in_dim(x, q * size, (q + 1) * size, axis=axis) for q in range(count))

    split = jax.custom_vjp(blocks)
    split.defvjp(lambda x: (blocks(x), None), lambda _, gs: (jnp.concatenate(gs, axis=axis),))
    return split


def _split3(x):
    hi = x.astype(BF16)
    r = x - hi.astype(F32)
    mid = r.astype(BF16)
    return hi, mid, (r - mid.astype(F32)).astype(BF16)


def _expand_impl(x, e):
    return sum(jnp.dot(t, e, preferred_element_type=F32) for t in _split3(x))


@jax.custom_vjp
def _expand(x, e):
    return _expand_impl(x, e)


def _expand_bwd(e, g):
    hi, mid, _ = _split3(g)
    dx = sum(lax.dot_general(t, e, (NT, ((), ())), preferred_element_type=F32) for t in (hi, mid))
    return dx, jnp.zeros_like(e)


_expand.defvjp(lambda x, e: (_expand_impl(x, e), e), _expand_bwd)

HEAD_PAIR = 2 * SSD_HEAD_DIM
GROUP_W = SSD_HPG * SSD_HEAD_DIM


def _ssd_chunk(xs, bm, cm, dtraw, prev, par, c, tri, e64, e128):
    li = lax.broadcasted_iota(jnp.int32, (CHUNK, CHUNK), 0)
    si = lax.broadcasted_iota(jnp.int32, (CHUNK, CHUNK), 1)
    causal = li >= si
    first_head = lax.broadcasted_iota(jnp.int32, (1, HEAD_PAIR), 1) < SSD_HEAD_DIM
    dt = jnp.where(_row_mask(c, CHUNK), jax.nn.softplus(dtraw + par[0:1, :]), 0.0)
    a = -jnp.exp(par[1:2, :])
    acs = jnp.dot(tri, dt * a, precision=lax.Precision.HIGHEST, preferred_element_type=F32)
    acs_t = acs.T
    last = acs[CHUNK - 1:CHUNK, :]
    misc = jnp.concatenate([jnp.exp(last), par[2:3, :], jnp.zeros((SUBLANES - 2, LANES), F32)], axis=0)
    wide = _expand(jnp.concatenate([dt, dt * jnp.exp(last - acs), jnp.exp(acs)], axis=0), e64)
    dt_w, dtend_w, start_w = _splitter(0, CHUNK, 3)(wide)
    misc_w = _expand(misc, e64)
    col_w = _splitter(1, CHUNK, SSD_HEADS)(_expand(acs, e128))
    groups = _splitter(1, GROUP_W, SSD_GROUPS)
    xs_g, prev_g, start_g = groups(xs), groups(prev), groups(start_w)
    xdt_p = _splitter(1, HEAD_PAIR, SSD_HEADS // 2)(xs * dt_w)
    xdtend_g = groups(xs * dtend_w)
    last_g, skip_g = groups(misc_w[0:1, :]), groups(misc_w[1:2, :])
    b_g, c_g = _splitter(1, SSD_STATE, SSD_GROUPS)(bm), _splitter(1, SSD_STATE, SSD_GROUPS)(cm)
    ys, news = [], []
    for g in range(SSD_GROUPS):
        cb = bdot_nt(c_g[g], b_g[g])
        st = bdot_nn(b_g[g].T, xdtend_g[g])
        y_off = bdot_nn(c_g[g], prev_g[g]) * start_g[g]
        pairs = []
        for q in range(SSD_HPG // 2):
            xp = xdt_p[g * (SSD_HPG // 2) + q]
            acc = None
            for r in range(2):
                head = SSD_HPG * g + 2 * q + r
                seg = jnp.where(causal, col_w[head] - _sub_row(acs_t, head), 0.0)
                decay = jnp.where(causal, jnp.exp(seg), 0.0)
                t = bdot_nn(cb * decay, jnp.where(first_head if r == 0 else jnp.logical_not(first_head), xp, 0.0))
                acc = t if acc is None else acc + t
            pairs.append(acc)
        ys.append(jnp.concatenate(pairs, axis=1) + y_off + xs_g[g] * skip_g[g])
        news.append(prev_g[g] * last_g[g] + st)
    return jnp.concatenate(ys, axis=1), jnp.concatenate(news, axis=1)


def _expanders():
    e64 = np.zeros((LANES, SSD_D_INNER), np.float32)
    e128 = np.zeros((LANES, SSD_HEADS * CHUNK), np.float32)
    for h in range(SSD_HEADS):
        e64[h, h * SSD_HEAD_DIM:(h + 1) * SSD_HEAD_DIM] = 1.0
        e128[h, h * CHUNK:(h + 1) * CHUNK] = 1.0
    return jnp.asarray(e64, BF16), jnp.asarray(e128, BF16)


def _tri():
    return jnp.asarray(np.tril(np.ones((CHUNK, CHUNK), np.float32)))


BC_W = SSD_GROUPS * SSD_STATE


def _conv_act(ext, refs, w, b, c):
    _conv_load(ext, [(refs[2 * p], refs[2 * p + 1]) + CONV_COLS[p] for p in range(3)], c)
    pre = _conv_pre(ext, w, b, CHUNK)
    return pre, jnp.where(_row_mask(c, CHUNK), _silu(pre), 0.0)


def ssd_fwd(name, zx, dt_block, par, conv_w, conv_b, comm=None):
    rows = zx.shape[0]
    nc = rows // CHUNK

    def body(*refs):
        xbc_refs, (dt_ref, par_ref, tri_ref, e64_ref, e128_ref, cw_ref, cb_ref) = refs[:6], refs[6:13]
        y_ref, st_ref, state, ext = refs[13:]
        c = pl.program_id(0)

        @pl.when(c == 0)
        def _():
            state[...] = jnp.zeros((SSD_STATE, SSD_D_INNER), F32)

        prev = state[...]
        st_ref[...] = prev
        _, act = _conv_act(ext, xbc_refs, cw_ref[...], cb_ref[...], c)
        (x0, x1), (b0, b1), (c0, c1) = CONV_COLS
        y, new = _ssd_chunk(act[:, x0:x1], act[:, b0:b1], act[:, c0:c1], dt_ref[...], prev, par_ref[...], c,
                            tri_ref[...], e64_ref[...], e128_ref[...])
        y_ref[...] = y
        state[...] = new

    const = lambda a: pl.BlockSpec(a.shape, lambda c: (0,) * a.ndim)
    consts = (par, _tri()) + _expanders() + (conv_w, conv_b)
    return carrier_call(
        name, body, (nc,),
        _conv_specs(lambda c: c) + [pl.BlockSpec((CHUNK, LANES), lambda c: (c, dt_block))] + [const(a) for a in consts],
        [pl.BlockSpec((CHUNK, SSD_D_INNER), lambda c: (c, 0)),
         pl.BlockSpec((None, SSD_STATE, SSD_D_INNER), lambda c: (c, 0, 0))],
        [jax.ShapeDtypeStruct((rows, SSD_D_INNER), F32),
         jax.ShapeDtypeStruct((nc, SSD_STATE, SSD_D_INNER), F32)],
        [pltpu.VMEM((SSD_STATE, SSD_D_INNER), F32), pltpu.VMEM((CHUNK + CONV_HALO, CONV_W), F32)],
        (zx,) * 7 + consts, comm)


def carrier_call(name, body, grid, in_specs, out_specs, out_shape, scratch_shapes, args, comm):
    body, in_specs, out_specs, out_shape, scratch_shapes, extra, n_own = with_comm(
        comm, body, grid, in_specs, out_specs, out_shape, scratch_shapes)
    res = pl.pallas_call(
        body, name=name, grid=grid, in_specs=in_specs, out_specs=out_specs, out_shape=out_shape,
        scratch_shapes=scratch_shapes, compiler_params=_params(len(grid)))(*args, *extra)
    return res[:n_own], res[n_own:]


def ssd_bwd(name, dy, zx, dt_block, states, par, conv_w, conv_b, comm=None):
    rows = zx.shape[0]
    nc = rows // CHUNK

    def body(*refs):
        dy_ref, xbc_refs = refs[0], refs[1:7]
        dt_ref, st_ref, par_ref, tri_ref, e64_ref, e128_ref, cw_ref, cb_ref = refs[7:15]
        du_ref, ddt_ref, dpar_ref, dcw_ref, dcb_ref, dstate, ext, dext = refs[15:]
        s = pl.program_id(0)
        c = nc - 1 - s

        @pl.when(s == 0)
        def _():
            dstate[...] = jnp.zeros((SSD_STATE, SSD_D_INNER), F32)
            dext[pl.ds(CHUNK, CONV_HALO), :] = jnp.zeros((CONV_HALO, CONV_W), F32)

        _, act = _conv_act(ext, xbc_refs, cw_ref[...], cb_ref[...], c)
        (x0, x1), (b0, b1), (c0, c1) = CONV_COLS

        def f(xs, bm, cm, dtraw, prev, par_v):
            return _ssd_chunk(xs, bm, cm, dtraw, prev, par_v, c, tri_ref[...], e64_ref[...], e128_ref[...])

        _, vjp = jax.vjp(f, act[:, x0:x1], act[:, b0:b1], act[:, c0:c1], dt_ref[...], st_ref[...], par_ref[...])
        dxs, dbm, dcm, ddt, dprev, dpar = vjp((dy_ref[...], dstate[...]))
        ddt_ref[...] = ddt.astype(BF16)
        dstate[...] = dprev
        _acc_out(dpar_ref, dpar, s == 0)

        dacts = [dxs[:, q * CONV_BLK:(q + 1) * CONV_BLK] for q in range((x1 - x0) // CONV_BLK)]
        dacts += [dbm[:, q * CONV_BLK:(q + 1) * CONV_BLK] for q in range((b1 - b0) // CONV_BLK)]
        dacts += [dcm[:, q * CONV_BLK:(q + 1) * CONV_BLK] for q in range((c1 - c0) // CONV_BLK)]
        for q, dact in enumerate(dacts):
            cols = slice(q * CONV_BLK, (q + 1) * CONV_BLK)
            w = cw_ref[:, cols]
            taps = [ext[pl.ds(CONV_HALO - (SSD_CONV - 1) + k, CHUNK), cols] for k in range(SSD_CONV)]
            pre = cb_ref[:, cols]
            for k in range(SSD_CONV):
                pre = pre + w[k:k + 1, :] * taps[k]
            sg = jax.nn.sigmoid(pre)
            dpre = jnp.where(_row_mask(c, CHUNK), dact * (sg * (1.0 + pre * (1.0 - sg))), 0.0)
            dext[pl.ds(0, CHUNK), cols] = dpre
            du = jnp.zeros((CHUNK, CONV_BLK), F32)
            for k in range(SSD_CONV):
                du = du + w[k:k + 1, :] * dext[pl.ds(SSD_CONV - 1 - k, CHUNK), cols]
            du_ref[:, cols] = du.astype(BF16)
            _acc_out(dcb_ref.at[:, cols], _rows8(dpre), s == 0)
            for k in range(SSD_CONV):
                _acc_out(dcw_ref.at[pl.ds(k * SUBLANES, SUBLANES), cols], _rows8(dpre * taps[k]), s == 0)
            dext[pl.ds(CHUNK, CONV_HALO), cols] = dpre[0:CONV_HALO, :]

    rev = lambda w, b: pl.BlockSpec((CHUNK, w), lambda s: (nc - 1 - s, b))
    const = lambda a: pl.BlockSpec(a.shape, lambda s: (0,) * a.ndim)
    consts = (par, _tri()) + _expanders() + (conv_w, conv_b)
    return carrier_call(
        name, body, (nc,),
        [rev(SSD_D_INNER, 0)] + _conv_specs(lambda s: nc - 1 - s)
        + [rev(LANES, dt_block), pl.BlockSpec((None, SSD_STATE, SSD_D_INNER), lambda s: (nc - 1 - s, 0, 0))]
        + [const(a) for a in consts],
        [rev(CONV_W, 0), rev(LANES, 0), pl.BlockSpec((SUBLANES, LANES), lambda s: (0, 0)),
         pl.BlockSpec((SSD_CONV * SUBLANES, CONV_W), lambda s: (0, 0)),
         pl.BlockSpec((SUBLANES, CONV_W), lambda s: (0, 0))],
        [jax.ShapeDtypeStruct((rows, CONV_W), BF16),
         jax.ShapeDtypeStruct((rows, LANES), BF16),
         jax.ShapeDtypeStruct((SUBLANES, LANES), F32),
         jax.ShapeDtypeStruct((SSD_CONV * SUBLANES, CONV_W), F32),
         jax.ShapeDtypeStruct((SUBLANES, CONV_W), F32)],
        [pltpu.VMEM((SSD_STATE, SSD_D_INNER), F32), pltpu.VMEM((CHUNK + CONV_HALO, CONV_W), F32),
         pltpu.VMEM((CHUNK + CONV_HALO, CONV_W), F32)],
        (dy,) + (zx,) * 7 + (states,) + consts, comm)


GN_W = SSD_D_INNER // SSD_GROUPS


def _gated_norm(y, z, ng):
    g = y * _silu(z)
    outs = []
    for q in range(SSD_GROUPS):
        gs = g[:, q * GN_W:(q + 1) * GN_W]
        outs.append(gs * lax.rsqrt(jnp.mean(gs * gs, axis=-1, keepdims=True) + EPS))
    return jnp.concatenate(outs, axis=1) * ng


def ssd_layer_fwd(tag, h, ln_g, w, comm=None):
    zx, hn = norm_mm(f"ssd_in_{tag}", h, ln_g, w["w_in"], tn=896)
    dt_block = 3 * SSD_D_INNER // LANES
    (y, states), cres = ssd_fwd(f"ssd_scan_{tag}", zx, dt_block, w["par"], w["conv_w"], w["conv_b"], comm)
    out, gn = res_mm(
        f"ssd_out_{tag}",
        [(y, (None, SSD_D_INNER), lambda i, j, kk: (i, 0)), (zx, (None, SSD_D_INNER), lambda i, j, kk: (i, 0)),
         (w["norm"], (1, SSD_D_INNER), lambda i, j, kk: (0, 0))],
        lambda r: _gated_norm(r[0][...], r[1][...], r[2][...]).astype(BF16),
        SSD_D_INNER, w["w_out"], h, tn=512, save_dtype=BF16)
    return out, (h, hn, zx, y, states, gn), cres


def ssd_dwout(tag, dh, saved):
    return simple_wgrad(f"ssd_dwout_{tag}", saved[5], dh, t1=SSD_D_INNER, tn=dh.shape[1])


def ssd_layer_bwd(tag, dh, saved, ln_g, w, dw_out, comm=None):
    h, hn, zx, y, states, gn = saved
    rows, d = h.shape
    tm = _row_tile(rows)
    dt_block = 3 * SSD_D_INNER // LANES

    def epi_gate(acc, e_refs, o_refs, i, j):
        _, vjp = jax.vjp(_gated_norm, e_refs[0][...], e_refs[1][...], e_refs[2][...])
        dy, dz, dng = vjp(acc)
        o_refs[0][...] = dy
        o_refs[1][...] = dz.astype(BF16)
        row0 = lax.broadcasted_iota(jnp.int32, (SUBLANES, 1), 0) == 0
        _acc_out(o_refs[2], jnp.where(row0, dng, 0.0), i == 0)

    wo, wo_block, wo_imap = _lw(w["w_out"], (SSD_D_INNER, d), lambda i, j, kk: (0, 0))
    dy, dz, dnorm = fused_mm(
        f"ssd_dgate_{tag}", rows=rows, k=d, n=SSD_D_INNER, tm=tm, tn=SSD_D_INNER,
        a_ops=[(dh, (tm, d), lambda i, j, kk: (i, 0))], pro=lambda a, o, i: a[0][...].astype(BF16),
        w=wo, w_block=wo_block, w_imap=wo_imap, dot=_dot_wt,
        e_ops=[(y, (tm, SSD_D_INNER), lambda i, j, kk: (i, 0)), (zx, (tm, SSD_D_INNER), lambda i, j, kk: (i, 0)),
               (w["norm"], (1, SSD_D_INNER), lambda i, j, kk: (0, 0))],
        epi=epi_gate,
        outs=[((rows, SSD_D_INNER), F32, (tm, SSD_D_INNER), lambda i, j, kk: (i, 0)),
              ((rows, SSD_D_INNER), BF16, (tm, SSD_D_INNER), lambda i, j, kk: (i, 0)),
              ((SUBLANES, SSD_D_INNER), F32, (SUBLANES, SSD_D_INNER), lambda i, j, kk: (0, 0))])
    (dxbc, ddt, dpar, dcw, dcb), cres = ssd_bwd(f"ssd_dscan_{tag}", dy, zx, dt_block, states, w["par"],
                                                w["conv_w"], w["conv_b"], comm)
    dzx = jnp.concatenate([dz, dxbc, ddt], axis=1)
    k = dzx.shape[1]
    dw_in = simple_wgrad(f"ssd_dwin_{tag}", hn, dzx, t1=d, tn=896)
    dh_in, dln = rms_bwd_mm(
        f"ssd_dh_{tag}", [(dzx, (None, 896), lambda i, j, kk: (i, kk))], lambda r: r[0][...], k,
        w["w_in"], (d, 896), lambda i, j, kk: (0, kk), _dot_wt, h, ln_g, dh, tk=896,
        tm=704 if rows % 704 == 0 else None)
    grads = dict(w_in=dw_in, w_out=dw_out, conv_w=dcw, conv_b=dcb, par=dpar, norm=dnorm, ln=dln)
    return dh_in, grads, cres


HP = 2 * LANES
VP = 2 * MLA_V
N_PAIRS = MLA_HEADS // 2
ATT_SCALE = MLA_QK ** -0.5
LOG2E = float(np.log2(np.e))
LN2 = float(np.log(2.0))
ROT = MLA_ROPE // 2


def rope_tables(rows):
    inv = 1.0 / (ROPE_THETA ** (jnp.arange(0, MLA_ROPE, 2, dtype=F32) / MLA_ROPE))
    pos = jnp.arange(rows, dtype=F32) - PAD
    ang = pos[:, None] * inv[None, :]
    cos, sin = jnp.cos(ang), jnp.sin(ang)
    one = jnp.ones((rows, MLA_NOPE), F32)
    zero = jnp.zeros((rows, LANES - MLA_QK), F32)
    zn = jnp.zeros((rows, MLA_NOPE), F32)
    zr = jnp.zeros((rows, ROT), F32)
    cosf = jnp.concatenate([one, cos, cos, zero], axis=1)
    sina = jnp.concatenate([zn, -sin, zr, zero], axis=1)
    sinb = jnp.concatenate([zn, zr, sin, zero], axis=1)
    return cosf, sina, sinb


def _qk_norm_rope(x, g, cosf, sina, sinb):
    r = lax.rsqrt(jnp.sum(x * x, axis=-1, keepdims=True) * (1.0 / MLA_QK) + EPS)
    xn = x * r * g
    return xn * cosf + pltpu.roll(xn, LANES - ROT, 1) * sina + pltpu.roll(xn, ROT, 1) * sinb


def _qk_norm_rope_bwd(dout, x, g, cosf, sina, sinb):
    dxn = dout * cosf + pltpu.roll(dout * sina, ROT, 1) + pltpu.roll(dout * sinb, LANES - ROT, 1)
    r = lax.rsqrt(jnp.sum(x * x, axis=-1, keepdims=True) * (1.0 / MLA_QK) + EPS)
    xh = x * r
    t = dxn * g
    dx = r * (t - xh * (jnp.sum(t * xh, axis=-1, keepdims=True) * (1.0 / MLA_QK)))
    return dx, _rows8(dxn * xh)


def _rope_lanes():
    lane = lax.broadcasted_iota(jnp.int32, (1, LANES), 1)
    return jnp.logical_and(lane >= MLA_NOPE, lane < MLA_QK)


QW = MLA_HEADS * LANES
VW = MLA_HEADS * MLA_V


def qk_prep(name, qraw, kvraw, lat, kpe_block, qg, kg, tabs):
    rows = qraw.shape[0]
    tm = _row_tile(rows)

    def body(q_ref, k0_ref, k1_ref, v_ref, pe_ref, qg_ref, kg_ref, c_ref, sa_ref, sb_ref,
             qo_ref, ko_ref, kt_ref, vo_ref, vt_ref):
        tab = (c_ref[...], sa_ref[...], sb_ref[...])
        pe = pe_ref[...]
        for hd in range(MLA_HEADS):
            sl = slice(hd * LANES, (hd + 1) * LANES)
            qo_ref[:, sl] = _qk_norm_rope(q_ref[:, sl].astype(F32), qg_ref[...], *tab).astype(BF16)
            kr = k0_ref if hd < MLA_HEADS // 2 else k1_ref
            ks = slice((hd % (MLA_HEADS // 2)) * LANES, (hd % (MLA_HEADS // 2) + 1) * LANES)
            kk = _qk_norm_rope(kr[:, ks].astype(F32) + pe, kg_ref[...], *tab)
            ko_ref[:, sl] = kk.astype(BF16)
            kt_ref[sl, :] = kk.T.astype(BF16)
        vo_ref[...] = v_ref[...].astype(BF16)
        for c in range(VW // LANES):
            sl = slice(c * LANES, (c + 1) * LANES)
            vt_ref[sl, :] = v_ref[:, sl].astype(F32).T.astype(BF16)

    row = lambda w, b: pl.BlockSpec((tm, w), lambda i: (i, b))
    col = lambda w: pl.BlockSpec((w, tm), lambda i: (0, i))
    one = pl.BlockSpec((1, LANES), lambda i: (0, 0))
    return pl.pallas_call(
        body, name=name, grid=(rows // tm,),
        in_specs=[row(QW, 0), row(VW, 0), row(VW, 1), row(VW, 2), row(LANES, kpe_block), one, one,
                  row(LANES, 0), row(LANES, 0), row(LANES, 0)],
        out_specs=[row(QW, 0), row(QW, 0), col(QW), row(VW, 0), col(VW)],
        out_shape=[jax.ShapeDtypeStruct((rows, QW), BF16), jax.ShapeDtypeStruct((rows, QW), BF16),
                   jax.ShapeDtypeStruct((QW, rows), BF16), jax.ShapeDtypeStruct((rows, VW), BF16),
                   jax.ShapeDtypeStruct((VW, rows), BF16)],
        compiler_params=_params(1),
    )(qraw, kvraw, kvraw, kvraw, lat, qg, kg, *tabs)


def qk_prep_bwd(name, dq_t, dk, dv, qraw, kvraw, lat, kpe_block, qg, kg, tabs):
    rows = qraw.shape[0]
    tm = _row_tile(rows)

    def body(dq_ref, dk_ref, dv_ref, q_ref, k0_ref, k1_ref, pe_ref, qg_ref, kg_ref, c_ref, sa_ref, sb_ref,
             dqo_ref, dkvo_ref, dpe_ref, dqg_ref, dkg_ref):
        i = pl.program_id(0)
        tab = (c_ref[...], sa_ref[...], sb_ref[...])
        pe = pe_ref[...]
        dpe = jnp.zeros((tm, LANES), F32)
        dqg = jnp.zeros((SUBLANES, LANES), F32)
        dkg = jnp.zeros((SUBLANES, LANES), F32)
        for hd in range(MLA_HEADS):
            sl = slice(hd * LANES, (hd + 1) * LANES)
            dx, dg = _qk_norm_rope_bwd(dq_ref[sl, :].T, q_ref[:, sl].astype(F32), qg_ref[...], *tab)
            dqo_ref[:, sl] = dx.astype(BF16)
            dqg = dqg + dg
            kr = k0_ref if hd < MLA_HEADS // 2 else k1_ref
            ks = slice((hd % (MLA_HEADS // 2)) * LANES, (hd % (MLA_HEADS // 2) + 1) * LANES)
            dx, dg = _qk_norm_rope_bwd(dk_ref[:, sl], kr[:, ks].astype(F32) + pe, kg_ref[...], *tab)
            dkvo_ref[:, sl] = dx.astype(BF16)
            dpe = dpe + dx
            dkg = dkg + dg
        dkvo_ref[:, QW:QW + VW] = dv_ref[...].astype(BF16)
        dpe_ref[...] = jnp.where(_rope_lanes(), dpe, 0.0)
        _acc_out(dqg_ref, dqg, i == 0)
        _acc_out(dkg_ref, dkg, i == 0)

    row = lambda w, b: pl.BlockSpec((tm, w), lambda i: (i, b))
    one = pl.BlockSpec((1, LANES), lambda i: (0, 0))
    acc = pl.BlockSpec((SUBLANES, LANES), lambda i: (0, 0))
    return pl.pallas_call(
        body, name=name, grid=(rows // tm,),
        in_specs=[pl.BlockSpec((QW, tm), lambda i: (0, i)), row(QW, 0), row(VW, 0), row(QW, 0), row(VW, 0), row(VW, 1),
                  row(LANES, kpe_block), one, one, row(LANES, 0), row(LANES, 0), row(LANES, 0)],
        out_specs=[row(QW, 0), row(QW + VW, 0), row(LANES, 0), acc, acc],
        out_shape=[jax.ShapeDtypeStruct((rows, QW), BF16), jax.ShapeDtypeStruct((rows, QW + VW), BF16),
                   jax.ShapeDtypeStruct((rows, LANES), F32),
                   jax.ShapeDtypeStruct((SUBLANES, LANES), F32), jax.ShapeDtypeStruct((SUBLANES, LANES), F32)],
        compiler_params=_params(1),
    )(dq_t, dk, dv, qraw, kvraw, kvraw, lat, qg, kg, *tabs)


def _att_mask_t(qb, kb, bt):
    kpos = kb * bt + lax.broadcasted_iota(jnp.int32, (bt, bt), 0)
    qpos = qb * bt + lax.broadcasted_iota(jnp.int32, (bt, bt), 1)
    return jnp.logical_and(kpos <= qpos, jnp.logical_or(kpos >= PAD, qpos < PAD))


def attn_fwd(name, q, k, vt, comm=None):
    rows = q.shape[0]
    bt = _row_tile(rows)
    nb = rows // bt
    assert bt >= CHUNK

    def body(q_ref, k_ref, vt_ref, o_ref, lse_ref):
        qi = pl.program_id(1)
        lse_ref[...] = jnp.zeros((SUBLANES, bt), F32)

        def scores(kb):
            r0 = pl.multiple_of(kb * bt, LANES)
            return tuple(lax.dot_general(k_ref[pl.ds(r0, bt), hh * LANES:(hh + 1) * LANES],
                                         q_ref[:, hh * LANES:(hh + 1) * LANES], (NT, ((), ())),
                                         preferred_element_type=F32) for hh in range(2))

        def tile(kb, carry, s_pair, masked):
            r0 = pl.multiple_of(kb * bt, LANES)
            new = []
            for hh in range(2):
                m, l, acc = carry[3 * hh:3 * hh + 3]
                vs = slice(hh * MLA_V, (hh + 1) * MLA_V)
                s = s_pair[hh] * (ATT_SCALE * LOG2E)
                if masked:
                    s = jnp.where(_att_mask_t(qi, kb, bt), s, NEG)
                m_new = jnp.maximum(m, jnp.max(s, axis=0, keepdims=True))
                alpha = jnp.exp2(m - m_new)
                p = jnp.exp2(s - m_new)
                l = alpha * l + jnp.sum(p, axis=0, keepdims=True)
                acc = alpha * acc + jnp.dot(vt_ref[vs, pl.ds(r0, bt)], p.astype(BF16), preferred_element_type=F32)
                new += [m_new, l, acc]
            return tuple(new)

        init = (jnp.full((1, bt), NEG, F32), jnp.zeros((1, bt), F32), jnp.zeros((MLA_V, bt), F32)) * 2
        s_next = scores(jnp.minimum(1, qi))
        carry = tile(0, init, scores(0), True)

        def rest(args):
            def mid(kb, state):
                carry, s_cur = state
                s_after = scores(kb + 1)
                return tile(kb, carry, s_cur, False), s_after

            carry, s_last = lax.fori_loop(1, qi, mid, args)
            return tile(qi, carry, s_last, True)

        carry = lax.cond(qi > 0, rest, lambda args: args[0], (carry, s_next))
        for hh in range(2):
            m, l, acc = carry[3 * hh:3 * hh + 3]
            o_ref[hh * MLA_V:(hh + 1) * MLA_V, :] = acc / l
            lse_ref[hh:hh + 1, :] = m * LN2 + jnp.log(l)

    return carrier_call(
        name, body, (N_PAIRS, nb),
        [pl.BlockSpec((bt, HP), lambda p, i: (i, p)),
         pl.BlockSpec((rows, HP), lambda p, i: (0, p)),
         pl.BlockSpec((VP, rows), lambda p, i: (p, 0))],
        [pl.BlockSpec((VP, bt), lambda p, i: (p, i)),
         pl.BlockSpec((None, SUBLANES, bt), lambda p, i: (p, 0, i))],
        [jax.ShapeDtypeStruct((VW, rows), F32), jax.ShapeDtypeStruct((N_PAIRS, SUBLANES, rows), F32)],
        [], (q, k, vt), comm)


def attn_bwd(name, q, k, kt, v, do_t, lse, delta, comm=None):
    rows = q.shape[0]
    bt = _row_tile(rows)
    nb = rows // bt

    def body(q_ref, k_ref, kt_ref, v_ref, do_ref, lse_ref, dl_ref, dq_ref, dk_ref, dv_ref, dk_scr, dv_scr):
        ki = pl.program_id(1)

        @pl.when(ki == 0)
        def _():
            dq_ref[...] = jnp.zeros((HP, rows), F32)

        dk_scr[...] = jnp.zeros((bt, HP), F32)
        dv_scr[...] = jnp.zeros((bt, VP), F32)

        def tile(qb, masked, heads):
            c0 = pl.multiple_of(qb * bt, LANES)
            for hh in heads:
                qs = slice(hh * LANES, (hh + 1) * LANES)
                vs = slice(hh * MLA_V, (hh + 1) * MLA_V)
                qv = q_ref[pl.ds(c0, bt), qs]
                dov = do_ref[vs, pl.ds(c0, bt)]
                lse = lse_ref[hh:hh + 1, pl.ds(c0, bt)]
                dl = dl_ref[hh:hh + 1, pl.ds(c0, bt)]
                s = lax.dot_general(k_ref[:, qs], qv, (NT, ((), ())), preferred_element_type=F32) * ATT_SCALE
                p = jnp.exp(s - lse)
                if masked:
                    p = jnp.where(_att_mask_t(qb, ki, bt), p, 0.0)
                dp = jnp.dot(v_ref[:, vs], dov, preferred_element_type=F32)
                ds = (p * (dp - dl) * ATT_SCALE).astype(BF16)
                dv_scr[:, vs] += lax.dot_general(p.astype(BF16), dov, (NT, ((), ())), preferred_element_type=F32)
                dk_scr[:, qs] += jnp.dot(ds, qv, preferred_element_type=F32)
                dq_ref[qs, pl.ds(c0, bt)] += jnp.dot(kt_ref[qs, :], ds, preferred_element_type=F32)

        for hh in range(2):
            heads = (hh,)

            @pl.when(ki == 0)
            def _(heads=heads):
                def every(qb, carry):
                    tile(qb, True, heads)
                    return carry

                lax.fori_loop(0, nb, every, 0)

            @pl.when(ki > 0)
            def _(heads=heads):
                tile(ki, True, heads)

                def later(qb, carry):
                    tile(qb, False, heads)
                    return carry

                lax.fori_loop(ki + 1, nb, later, 0)

        dk_ref[...] = dk_scr[...]
        dv_ref[...] = dv_scr[...]

    stat = pl.BlockSpec((None, SUBLANES, rows), lambda p, i: (p, 0, 0))
    return carrier_call(
        name, body, (N_PAIRS, nb),
        [pl.BlockSpec((rows, HP), lambda p, i: (0, p)),
         pl.BlockSpec((bt, HP), lambda p, i: (i, p)),
         pl.BlockSpec((HP, bt), lambda p, i: (p, i)),
         pl.BlockSpec((bt, VP), lambda p, i: (i, p)),
         pl.BlockSpec((VP, rows), lambda p, i: (p, 0)),
         stat, stat],
        [pl.BlockSpec((HP, rows), lambda p, i: (p, 0)),
         pl.BlockSpec((bt, HP), lambda p, i: (i, p)),
         pl.BlockSpec((bt, VP), lambda p, i: (i, p))],
        [jax.ShapeDtypeStruct((QW, rows), F32), jax.ShapeDtypeStruct((rows, QW), F32),
         jax.ShapeDtypeStruct((rows, VW), F32)],
        [pltpu.VMEM((bt, HP), F32), pltpu.VMEM((bt, VP), F32)],
        (q, k, kt, v, do_t, lse, delta), comm)


def _dot_cast_w(a, w_ref):
    return jnp.dot(a, w_ref[...].astype(BF16), preferred_element_type=F32)


def _dot_cast_wt(a, w_ref):
    return lax.dot_general(a, w_ref[...].astype(BF16), (NT, ((), ())), preferred_element_type=F32)


LAT_W = 768
KPE_BLOCK = MLA_Q_RANK // LANES
KV_BLOCK = (MLA_Q_RANK + LANES) // MLA_KV_RANK


def mla_layer_fwd(tag, h, ln_g, w, tabs, comm=None):
    lat, hn = norm_mm(f"mla_in_{tag}", h, ln_g, w["w_in"], tn=LAT_W)
    qraw, qn = norm_mm(f"mla_q_{tag}", lat, w["q_a"], w["w_q"], tn=512, k_cols=MLA_Q_RANK, col_block=0,
                       out_dtype=BF16)
    kvraw, kvn = norm_mm(f"mla_kv_{tag}", lat, w["kv_a"], w["w_kv"], tn=512, k_cols=MLA_KV_RANK, col_block=KV_BLOCK,
                         out_dtype=BF16)
    q, k, kt, v, vt = qk_prep(f"mla_prep_{tag}", qraw, kvraw, lat, KPE_BLOCK, w["q_norm"], w["k_norm"], tabs)
    (o_t, lse), cres = attn_fwd(f"mla_attn_{tag}", q, k, vt, comm)
    out = res_mm(f"mla_out_{tag}", [(o_t, (VW, None), lambda i, j, kk: (0, i))],
                 lambda r: r[0][...].T.astype(BF16), VW, w["w_out"], h, tn=512, tm=_big_tile(h.shape[0]))
    return out, (h, hn, lat, qn, kvn, qraw, kvraw, q, k, kt, v, o_t, lse), cres


def mla_dwout(tag, dh, saved):
    o_t = saved[11]
    rows, d = dh.shape
    tm = _row_tile(rows)

    def epi_set(acc, e_refs, o_refs, i, j):
        o_refs[0][...] = acc.astype(BF16)

    return fused_mm(
        f"mla_dwout_{tag}", rows=VW, k=rows, n=d, tm=512, tn=d, tk=tm,
        a_ops=[(o_t, (512, tm), lambda i, j, kk: (i, kk))], pro=lambda a, o_, i: a[0][...].astype(BF16),
        w=dh, w_block=(tm, d), w_imap=lambda i, j, kk: (kk, 0), dot=_dot_cast_w, epi=epi_set,
        outs=[((VW, d), BF16, (512, d), lambda i, j, kk: (i, 0))])[0]


def mla_layer_bwd(tag, dh, saved, ln_g, w, tabs, dw_out, comm=None):
    h, hn, lat, qn, kvn, qraw, kvraw, q, k, kt, v, o_t, lse = saved
    rows, d = h.shape
    tm = _row_tile(rows)

    def epi_do(acc, e_refs, o_refs, i, j):
        o_refs[0][...] = acc.astype(BF16)
        prod = acc * e_refs[0][...]
        o_refs[1][...] = jnp.zeros((N_PAIRS, SUBLANES, tm), F32)
        for hd in range(MLA_HEADS):
            o_refs[1][hd // 2, hd % 2:hd % 2 + 1, :] = jnp.sum(prod[hd * MLA_V:(hd + 1) * MLA_V, :], axis=0,
                                                               keepdims=True)

    wo, wo_block, wo_imap = _lw(w["w_out"], (VW, d), lambda i, j, kk: (0, 0))
    do_t, delta = fused_mm(
        f"mla_do_{tag}", rows=VW, k=d, n=rows, tm=VW, tn=tm,
        a_ops=[(wo, wo_block, wo_imap)], pro=lambda a, o_, i: a[0][...],
        w=dh, w_block=(tm, d), w_imap=lambda i, j, kk: (j, 0), dot=_dot_cast_wt,
        e_ops=[(o_t, (VW, tm), lambda i, j, kk: (0, j))], epi=epi_do,
        outs=[((VW, rows), BF16, (VW, tm), lambda i, j, kk: (0, j)),
              ((N_PAIRS, SUBLANES, rows), F32, (N_PAIRS, SUBLANES, tm), lambda i, j, kk: (0, 0, j))])
    (dq_t, dk, dv), cres = attn_bwd(f"mla_dattn_{tag}", q, k, kt, v, do_t, lse, delta, comm)
    dqraw, dkvraw, dpe, dqg, dkg = qk_prep_bwd(f"mla_dprep_{tag}", dq_t, dk, dv, qraw, kvraw, lat, KPE_BLOCK,
                                               w["q_norm"], w["k_norm"], tabs)
    dw_q = simple_wgrad(f"mla_dwq_{tag}", qn, dqraw, t1=MLA_Q_RANK, tn=512)
    dqlat, dqa = rms_bwd_mm(
        f"mla_dqlat_{tag}", [(dqraw, (None, QW), lambda i, j, kk: (i, 0))], lambda r: r[0][...], QW,
        w["w_q"], (MLA_Q_RANK, QW), lambda i, j, kk: (0, 0), _dot_wt, lat, w["q_a"], None,
        h_cols=MLA_Q_RANK, h_col_block=0, add_dh=False)
    dw_kv = simple_wgrad(f"mla_dwkv_{tag}", kvn, dkvraw, t1=MLA_KV_RANK, tn=512)
    dkvlat, dkva = rms_bwd_mm(
        f"mla_dkvlat_{tag}", [(dkvraw, (None, QW + VW), lambda i, j, kk: (i, 0))], lambda r: r[0][...], QW + VW,
        w["w_kv"], (MLA_KV_RANK, QW + VW), lambda i, j, kk: (0, 0), _dot_wt, lat, w["kv_a"], None,
        h_cols=MLA_KV_RANK, h_col_block=KV_BLOCK, add_dh=False)
    dlat = jnp.concatenate([dqlat.astype(BF16), dpe.astype(BF16), dkvlat.astype(BF16)], axis=1)
    dw_in = simple_wgrad(f"mla_dwin_{tag}", hn, dlat, t1=512, tn=LAT_W)
    dh_in, dln = rms_bwd_mm(
        f"mla_dh_{tag}", [(dlat, (None, LAT_W), lambda i, j, kk: (i, 0))], lambda r: r[0][...], LAT_W,
        w["w_in"], (d, LAT_W), lambda i, j, kk: (0, 0), _dot_wt, h, ln_g, dh)
    grads = dict(w_in=dw_in, w_q=dw_q, w_kv=dw_kv, w_out=dw_out, q_a=dqa, kv_a=dkva, q_norm=dqg, k_norm=dkg, ln=dln)
    return dh_in, grads, cres


def loss_head(h, target):
    rows, d = h.shape
    nb = rows // CHUNK

    def body(h_ref, t_ref, l_ref, dh_ref):
        i = pl.program_id(0)
        err = jnp.where(i > 0, h_ref[...] - t_ref[...], 0.0)
        dh_ref[...] = err * (1.0 / d)
        _acc_out(l_ref, _rows8(err * err) * (0.5 / d), i == 0)

    return pl.pallas_call(
        body, name="loss_head", grid=(nb,),
        in_specs=[pl.BlockSpec((CHUNK, d), lambda i: (i, 0)),
                  pl.BlockSpec((CHUNK, d), lambda i: (jnp.maximum(i - 1, 0), 0))],
        out_specs=[pl.BlockSpec((SUBLANES, d), lambda i: (0, 0)), pl.BlockSpec((CHUNK, d), lambda i: (i, 0))],
        out_shape=[jax.ShapeDtypeStruct((SUBLANES, d), F32), jax.ShapeDtypeStruct((rows, d), F32)],
        compiler_params=_params(1),
    )(h, target)


def _adamw(w, g, m, v):
    m = ADAM_B1 * m + (1.0 - ADAM_B1) * g
    v = ADAM_B2 * v + (1.0 - ADAM_B2) * jnp.square(g)
    m_hat = m / (1.0 - ADAM_B1 ** ADAM_STEP)
    v_hat = v / (1.0 - ADAM_B2 ** ADAM_STEP)
    delta = -ADAM_LR * (m_hat / (jnp.sqrt(v_hat) + ADAM_EPS) + ADAM_WD * w)
    return delta, m, v


def reduce_adamw(name, recvs, w, m, v):
    nl, r, c = w.shape
    tr = 128 if r % 128 == 0 else r
    nr = r // tr

    def body(*refs):
        r_refs = refs[:nl]
        w_ref, m_ref, v_ref, g_ref, d_ref, mo_ref, vo_ref = refs[nl:]
        layer = pl.program_id(0)
        for l in range(nl):
            @pl.when(layer == l)
            def _(l=l):
                g = r_refs[l][0].astype(F32)
                for s in range(1, N_DEV):
                    g = g + r_refs[l][s].astype(F32)
                g_ref[...] = g
                d_ref[...], mo_ref[...], vo_ref[...] = _adamw(w_ref[...], g, m_ref[...], v_ref[...])

    def recv_spec(l):
        return pl.BlockSpec((N_DEV, tr, c),
                            lambda y, i: (0, jnp.where(y == l, i, jnp.where(y < l, 0, nr - 1)), 0))

    blk = pl.BlockSpec((None, tr, c), lambda y, i: (y, i, 0))
    return pl.pallas_call(
        body, name=name, grid=(nl, nr),
        in_specs=[recv_spec(l) for l in range(nl)] + [blk, blk, blk],
        out_specs=[blk] * 4, out_shape=[jax.ShapeDtypeStruct((nl, r, c), F32)] * 4,
        compiler_params=_params(2),
    )(*recvs, w, m, v)


def small_reduce(recv):
    def body(r_ref, o_ref):
        g = r_ref[0]
        for s in range(1, N_DEV):
            g = g + r_ref[s]
        o_ref[...] = g

    return pl.pallas_call(body, name="small_reduce", out_shape=jax.ShapeDtypeStruct(recv.shape[1:], F32))(recv)


def small_adamw(w, g, m, v):
    def body(w_ref, g_ref, m_ref, v_ref, d_ref, mo_ref, vo_ref):
        d_ref[...], mo_ref[...], vo_ref[...] = _adamw(w_ref[...], g_ref[...], m_ref[...], v_ref[...])

    return pl.pallas_call(body, name="small_adamw", out_shape=[jax.ShapeDtypeStruct(w.shape, F32)] * 3)(w, g, m, v)


def _pack(parts):
    flat, meta, off = [], [], 0
    for p in parts:
        n = int(np.prod(p.shape))
        flat.append(p.reshape(-1).astype(F32))
        meta.append((off, p.shape))
        off += n
    total = -(-off // (SUBLANES * LANES)) * (SUBLANES * LANES)
    flat.append(jnp.zeros((total - off,), F32))
    return jnp.concatenate(flat).reshape(total // LANES, LANES), meta


def _unpack(packed, meta):
    flat = packed.reshape(-1)
    return [flat[off:off + int(np.prod(shape))].reshape(shape) for off, shape in meta]


MESH = pl.DeviceIdType.MESH
N_PEERS = N_DEV - 1


def _me():
    return lax.axis_index("x"), lax.axis_index("y"), lax.axis_index("c")


def _peer(k):
    x, y, c = _me()
    return (1 - x if k & 4 else x, 1 - y if k & 2 else y, 1 - c if k & 1 else c)


def _dev_index(pos):
    return 4 * pos[0] + 2 * pos[1] + pos[2]


RELAYED = (3, 5, 7)


def make_comm(items):
    n = len(items)

    def part(ref, a, idx):
        rows = items[a][1]
        if rows == "all":
            return ref
        return ref.at[idx] if rows is None else ref.at[pl.ds(idx * rows, rows)]

    def part_shape(a):
        arr, rows = items[a]
        if rows == "all":
            return arr.shape
        return arr.shape[1:] if rows is None else (rows,) + arr.shape[1:]

    def run(phase, ins, outs, send_sems, recv_sems, local_sems):
        me = _dev_index(_me())

        def copy(a, k, src, slot, to):
            return pltpu.make_async_remote_copy(
                src_ref=src, dst_ref=outs[a].at[slot], send_sem=send_sems.at[a, k - 1],
                recv_sem=recv_sems.at[a, k - 1], device_id=to, device_id_type=MESH)

        for a in range(n):
            gather = items[a][1] == "all"
            local = pltpu.make_async_copy(part(ins[a], a, me), outs[a].at[me], local_sems.at[a])
            if phase == "start":
                local.start()
            for k in range(1, N_DEV):
                if gather and k in RELAYED:
                    continue
                peer = _peer(k)
                if phase == "start":
                    copy(a, k, part(ins[a], a, _dev_index(peer)), me, peer).start()
                else:
                    cp = copy(a, k, part(ins[a], a, me), _dev_index(peer), peer)
                    cp.wait_recv()
                    cp.wait_send()
            if phase == "wait":
                local.wait()
                if gather:
                    sibling = _peer(1)
                    relays = []
                    for k in RELAYED:
                        origin = _dev_index(_peer(k - 1))
                        relays.append(copy(a, k, outs[a].at[origin], origin, sibling))
                        relays[-1].start()
                    for k in RELAYED:
                        copy(a, k, ins[a], _dev_index(_peer(k)), sibling).wait_recv()
                    for cp in relays:
                        cp.wait_send()

    return dict(
        ins=[it[0] for it in items],
        outs=[jax.ShapeDtypeStruct((N_DEV,) + part_shape(a), items[a][0].dtype) for a in range(n)],
        sems=[pltpu.SemaphoreType.DMA((n, N_PEERS)), pltpu.SemaphoreType.DMA((n, N_PEERS)),
              pltpu.SemaphoreType.DMA((n,))],
        run=run)


ANY_SPEC = pl.BlockSpec(memory_space=pl.ANY)


def comm_call(name, comm):
    n, no = len(comm["ins"]), len(comm["outs"])

    def body(*refs):
        comm["run"]("start", refs[:n], refs[n:n + no], *refs[n + no:])
        comm["run"]("wait", refs[:n], refs[n:n + no], *refs[n + no:])

    return pl.pallas_call(
        body, name=name, in_specs=[ANY_SPEC] * n, out_specs=[ANY_SPEC] * no, out_shape=comm["outs"],
        scratch_shapes=comm["sems"])(*comm["ins"])


def with_comm(comm, body, grid, in_specs, out_specs, out_shape, scratch_shapes):
    if comm is None:
        return body, in_specs, out_specs, out_shape, scratch_shapes, [], len(out_shape)
    n_in, n_out, n_scr = len(in_specs), len(out_shape), len(scratch_shapes)
    ci, co = len(comm["ins"]), len(comm["outs"])

    def wrapped(*refs):
        ins, cins = refs[:n_in], refs[n_in:n_in + ci]
        outs = refs[n_in + ci:n_in + ci + n_out]
        couts = refs[n_in + ci + n_out:n_in + ci + n_out + co]
        rest = refs[n_in + ci + n_out + co:]
        scr, sems = rest[:n_scr], rest[n_scr:]
        first = functools.reduce(jnp.logical_and, [pl.program_id(a) == 0 for a in range(len(grid))])
        last = functools.reduce(jnp.logical_and, [pl.program_id(a) == grid[a] - 1 for a in range(len(grid))])

        @pl.when(first)
        def _():
            comm["run"]("start", cins, couts, *sems)

        body(*ins, *outs, *scr)

        @pl.when(last)
        def _():
            comm["run"]("wait", cins, couts, *sems)

    return (wrapped, list(in_specs) + [ANY_SPEC] * ci, list(out_specs) + [ANY_SPEC] * co,
            list(out_shape) + list(comm["outs"]), list(scratch_shapes) + list(comm["sems"]), list(comm["ins"]), n_out)


WEIGHTS = ['meta_tokens', 'ln_mix', 'ln_mlp', 'ssd_w_in', 'ssd_conv_w', 'ssd_conv_b', 'ssd_dt_bias', 'ssd_a_log',
           'ssd_d', 'ssd_norm', 'ssd_w_out', 'mla_w_in', 'mla_q_a_norm', 'mla_w_q_b', 'mla_kv_a_norm', 'mla_w_kv_b',
           'mla_q_norm', 'mla_k_norm', 'mla_w_out', 'mlp_w_up', 'mlp_w_down']
BIG = ['ssd_w_in', 'ssd_w_out', 'mla_w_in', 'mla_w_q_b', 'mla_w_kv_b', 'mla_w_out', 'mlp_w_up', 'mlp_w_down']
SMALL_SHARDED = ['meta_tokens', 'ssd_conv_w', 'mla_q_a_norm', 'mla_kv_a_norm']
SMALL_REPL = ['ln_mix', 'ln_mlp', 'ssd_conv_b', 'ssd_dt_bias', 'ssd_a_log', 'ssd_d', 'ssd_norm', 'mla_q_norm',
              'mla_k_norm']
SMALL = SMALL_REPL + SMALL_SHARDED
SSD_IN_PAD = 6272
SSD_IN_TN = 896
MLA_IN = MLA_Q_RANK + MLA_KV_RANK + MLA_ROPE


def _pad_last(v, n):
    return jnp.pad(v, [(0, 0)] * (v.ndim - 1) + [(0, n - v.shape[-1])])


SSD_BIG = ['ssd_w_in', 'ssd_w_out']
MLA_BIG = ['mla_w_in', 'mla_w_q_b', 'mla_w_kv_b', 'mla_w_out']
MLP_BIG = ['mlp_w_up', 'mlp_w_down']


def _mix_big(i):
    return [(n, i // 2) for n in (SSD_BIG if i % 2 == 0 else MLA_BIG)]


def _mlp_big(i):
    return [(n, i) for n in MLP_BIG]


def _mix_weights(i, gw, W, full):
    j = i // 2
    d = W['ln_mix'].shape[-1]
    if i % 2 == 0:
        wi = gw[('ssd_w_in', j)].transpose(1, 0, 2).reshape(d, -1)
        par = jnp.concatenate([_pad_last(W[n][j][None], LANES) for n in ('ssd_dt_bias', 'ssd_a_log', 'ssd_d')]
                              + [jnp.zeros((SUBLANES - 3, LANES), F32)])
        return dict(w_in=_pad_last(wi, SSD_IN_PAD), conv_w=full['ssd_conv_w'][j], conv_b=W['ssd_conv_b'][j][None],
                    par=par, norm=W['ssd_norm'][j][None], w_out=gw[('ssd_w_out', j)].reshape(SSD_D_INNER, d))
    wi = gw[('mla_w_in', j)].reshape(d, MLA_IN)
    kpe = jnp.pad(wi[:, MLA_Q_RANK + MLA_KV_RANK:], ((0, 0), (MLA_NOPE, LANES - MLA_QK)))
    wq = gw[('mla_w_q_b', j)].transpose(1, 0, 2).reshape(MLA_Q_RANK, MLA_HEADS, MLA_QK)
    wkv = gw[('mla_w_kv_b', j)].transpose(1, 0, 2).reshape(MLA_KV_RANK, MLA_HEADS, MLA_NOPE + MLA_V)
    return dict(
        w_in=jnp.concatenate([wi[:, :MLA_Q_RANK], kpe, wi[:, MLA_Q_RANK:MLA_Q_RANK + MLA_KV_RANK]], axis=1),
        w_q=_pad_last(wq, LANES).reshape(MLA_Q_RANK, QW),
        w_kv=jnp.concatenate([_pad_last(wkv[..., :MLA_NOPE], LANES).reshape(MLA_KV_RANK, QW),
                              wkv[..., MLA_NOPE:].reshape(MLA_KV_RANK, VW)], axis=1),
        w_out=gw[('mla_w_out', j)].reshape(VW, d), q_a=full['mla_q_a_norm'][j][None],
        kv_a=full['mla_kv_a_norm'][j][None],
        q_norm=_pad_last(W['mla_q_norm'][j][None], LANES), k_norm=_pad_last(W['mla_k_norm'][j][None], LANES))


def _step(x, target, W, M, V):
    d = x.shape[-1]
    me = _dev_index(_me())
    depth = W['ln_mix'].shape[0]

    def gather_keys(i):
        return _mlp_big(i) + (_mix_big(i + 1) if i + 1 < depth else [])

    def gather_items(keys):
        return [(W[n][l].astype(BF16), "all") for n, l in keys]

    small_pack, small_meta = _pack([W[n] for n in SMALL_SHARDED])
    got = comm_call("gather_0", make_comm(gather_items(_mix_big(0)) + [(small_pack, "all")]))
    per_dev = [_unpack(got[-1][s], small_meta) for s in range(N_DEV)]
    full = {n: jnp.concatenate([per_dev[s][i] for s in range(N_DEV)], axis=-1) for i, n in enumerate(SMALL_SHARDED)}
    gw = dict(zip(_mix_big(0), got))

    h = jnp.concatenate([jnp.zeros((PAD, d), F32), full['meta_tokens'], x], axis=0)
    rows = h.shape[0]
    tabs = rope_tables(rows)
    saved, weights = [], []
    for i in range(depth):
        comm = make_comm(gather_items(gather_keys(i)))
        mix = _mix_weights(i, gw, W, full)
        if i % 2 == 0:
            h, s_mix, got = ssd_layer_fwd(f"{i}", h, W['ln_mix'][i][None], mix, comm)
        else:
            h, s_mix, got = mla_layer_fwd(f"{i}", h, W['ln_mix'][i][None], mix, tabs, comm)
        gw.update(zip(gather_keys(i), got))
        up, down = gw[('mlp_w_up', i)], gw[('mlp_w_down', i)].reshape(-1, d)
        h, s_mlp = mlp_fwd(f"{i}", h, W['ln_mlp'][i][None], up, down)
        saved.append((s_mix, s_mlp))
        weights.append((mix, up, down))
    loss_part, dh = loss_head(h, target)
    loss = lax.psum(jnp.sum(loss_part), ("x", "y", "c"))

    recv = {}
    pending = []
    small = {n: [None] * W[n].shape[0] for n in SMALL if n != 'meta_tokens'}
    for i in reversed(range(depth)):
        j = i // 2
        s_mix, s_mlp = saved[i]
        mix, up, down = weights[i]
        dh, dw_up, dw_down, dg = mlp_bwd(f"{i}", dh, s_mlp, W['ln_mlp'][i][None], up, down)
        small['ln_mlp'][i] = dg.sum(0)
        pending += list(zip(_mlp_big(i), [(dw_up, None), (dw_down, down.shape[0] // N_DEV)]))
        dw_out = (ssd_dwout if i % 2 == 0 else mla_dwout)(f"{i}", dh, s_mix)
        pending.append((_mix_big(i)[-1], (dw_out, dw_out.shape[0] // N_DEV)))
        comm = make_comm([it for _, it in pending])
        sends = []
        if i % 2 == 0:
            dh, g, got = ssd_layer_bwd(f"{i}", dh, s_mix, W['ln_mix'][i][None], mix, dw_out, comm)
            n_in = W['ssd_w_in'].shape[-1]
            sends.append((g['w_in'][:, :N_DEV * n_in].reshape(d, N_DEV, n_in).transpose(1, 0, 2), None))
            small['ssd_conv_w'][j] = g['conv_w'].reshape(SSD_CONV, SUBLANES, -1).sum(1)
            small['ssd_conv_b'][j] = g['conv_b'].sum(0)
            small['ssd_dt_bias'][j] = g['par'][0, :SSD_HEADS]
            small['ssd_a_log'][j] = g['par'][1, :SSD_HEADS]
            small['ssd_d'][j] = g['par'][2, :SSD_HEADS]
            small['ssd_norm'][j] = g['norm'].sum(0)
        else:
            dh, g, got = mla_layer_bwd(f"{i}", dh, s_mix, W['ln_mix'][i][None], mix, tabs, dw_out, comm)
            gi = g['w_in']
            gi = jnp.concatenate([gi[:, :MLA_Q_RANK], gi[:, MLA_Q_RANK + LANES:],
                                  gi[:, MLA_Q_RANK + MLA_NOPE:MLA_Q_RANK + MLA_QK]], axis=1)
            sends.append((gi, d // N_DEV))
            gq = g['w_q'].reshape(MLA_Q_RANK, MLA_HEADS, LANES)[..., :MLA_QK]
            sends.append((gq.reshape(MLA_Q_RANK, N_DEV, -1).transpose(1, 0, 2), None))
            gkv = jnp.concatenate([g['w_kv'][:, :QW].reshape(MLA_KV_RANK, MLA_HEADS, LANES)[..., :MLA_NOPE],
                                   g['w_kv'][:, QW:].reshape(MLA_KV_RANK, MLA_HEADS, MLA_V)], axis=-1)
            sends.append((gkv.reshape(MLA_KV_RANK, N_DEV, -1).transpose(1, 0, 2), None))
            small['mla_q_a_norm'][j] = g['q_a'].sum(0)
            small['mla_kv_a_norm'][j] = g['kv_a'].sum(0)
            small['mla_q_norm'][j] = g['q_norm'].sum(0)[:MLA_QK]
            small['mla_k_norm'][j] = g['k_norm'].sum(0)[:MLA_QK]
        small['ln_mix'][i] = g['ln'].sum(0)
        recv.update({key: a for (key, _), a in zip(pending, got)})
        pending = list(zip(_mix_big(i)[:-1], sends))
    grad_x = dh[CHUNK:]
    small_full = {n: jnp.stack(v) for n, v in small.items()}
    small_full['meta_tokens'] = dh[PAD:CHUNK]

    gpack, gmeta = _pack([small_full[n] for n in SMALL])
    got = comm_call("exchange_0", make_comm([it for _, it in pending] + [(gpack, "all")]))
    recv.update({key: a for (key, _), a in zip(pending, got)})
    res = {}
    for n in BIG:
        res[n] = reduce_adamw(f"adamw_{n}", [recv[(n, l)] for l in range(W[n].shape[0])], W[n], M[n], V[n])
    gsum = dict(zip(SMALL, _unpack(small_reduce(got[-1]), gmeta)))
    for n in SMALL_SHARDED:
        wl = W[n].shape[-1]
        gsum[n] = lax.dynamic_slice_in_dim(gsum[n], me * wl, wl, axis=gsum[n].ndim - 1)
    wp, wmeta = _pack([W[n] for n in SMALL])
    gp, _ = _pack([gsum[n] for n in SMALL])
    mp, _ = _pack([M[n] for n in SMALL])
    vp, _ = _pack([V[n] for n in SMALL])
    upd = [_unpack(o, wmeta) for o in small_adamw(wp, gp, mp, vp)]
    for a, n in enumerate(SMALL):
        res[n] = [gsum[n], upd[0][a], upd[1][a], upd[2][a]]
    return (loss, grad_x[None]) + tuple(res[n][q] for q in range(4) for n in WEIGHTS)


def kernel(x, meta_tokens, ln_mix, ln_mlp, ssd_w_in, ssd_conv_w, ssd_conv_b, ssd_dt_bias, ssd_a_log, ssd_d, ssd_norm, ssd_w_out, mla_w_in, mla_q_a_norm, mla_w_q_b, mla_kv_a_norm, mla_w_kv_b, mla_q_norm, mla_k_norm, mla_w_out, mlp_w_up, mlp_w_down, loss_target, m_meta_tokens, m_ln_mix, m_ln_mlp, m_ssd_w_in, m_ssd_conv_w, m_ssd_conv_b, m_ssd_dt_bias, m_ssd_a_log, m_ssd_d, m_ssd_norm, m_ssd_w_out, m_mla_w_in, m_mla_q_a_norm, m_mla_w_q_b, m_mla_kv_a_norm, m_mla_w_kv_b, m_mla_q_norm, m_mla_k_norm, m_mla_w_out, m_mlp_w_up, m_mlp_w_down, v_meta_tokens, v_ln_mix, v_ln_mlp, v_ssd_w_in, v_ssd_conv_w, v_ssd_conv_b, v_ssd_dt_bias, v_ssd_a_log, v_ssd_d, v_ssd_norm, v_ssd_w_out, v_mla_w_in, v_mla_q_a_norm, v_mla_w_q_b, v_mla_kv_a_norm, v_mla_w_kv_b, v_mla_q_norm, v_mla_k_norm, v_mla_w_out, v_mlp_w_up, v_mlp_w_down):
    given = dict(locals())
    W = {n: given[n] for n in WEIGHTS}
    M = {n: given["m_" + n] for n in WEIGHTS}
    V = {n: given["v_" + n] for n in WEIGHTS}
    return _step(x[0], loss_target[0], W, M, V)
```
